```python
import jax, jax.numpy as jnp
from jax import lax
import numpy as np

D_MODEL = 1024
BATCH = 16
SEQ = 256
DEPTH = 2
DEC_BATCH = 8
DEC_SEQ = 4096
PAST_LEN = 512

GRID_W = 64
EPS = 1e-6
H_MLA = 4
Q_LORA = 256
KV_LORA = 128
NOPE_DIM = 64
ROPE_DIM = 32
V_DIM = 128
QK_DIM = NOPE_DIM + ROPE_DIM
ROPE_THETA = 10000.0
Q_BLOCK = 128
H_RWKV = 4
N_RWKV = 64
D_RWKV = H_RWKV * N_RWKV
W_LORA = 64
A_LORA = 64
G_LORA = 128
GN_EPS = 64e-5
D_LRU = 256
H_LRU = 4
BS_LRU = D_LRU // H_LRU
CONV_W = 4
LRU_C = 8.0
D_MLA_OUT = H_MLA * V_DIM
D_MIX = D_MLA_OUT + D_RWKV + D_LRU
P_MLA = Q_LORA + KV_LORA + ROPE_DIM
P_RWKV = 3 * D_RWKV + 2 * W_LORA + 2 * A_LORA + G_LORA
P_LRU = 2 * D_LRU
P_IN = P_MLA + P_RWKV + P_LRU
N_EXPERTS = 64
TOP_K = 6
N_GROUPS = 8
TOPK_GROUPS = 4
D_EXPERT = 256
D_SHARED = 256
ROUTED_SCALE = 2.5
MOE_BLOCK = 512

kernel_name = 'hybrid_mla_rwkv7_rglru_moe_dit_step'


def rms_norm(x, g):
    xf = x.astype(jnp.float32)
    y = xf * lax.rsqrt(jnp.mean(xf * xf, axis=-1, keepdims=True) + EPS)
    return (y * g.astype(jnp.float32)).astype(x.dtype)


def modulation(cvec, w, b):
    m = jax.nn.silu(cvec) @ w + b
    return jnp.split(m[..., None, :], 6, axis=-1)


def swiglu(x, wg, wu, wd):
    return (jax.nn.silu(x @ wg) * (x @ wu)) @ wd


def centred_shift(p):
    zero = jnp.zeros_like(p[:, :1])
    prev = jnp.concatenate([zero, p[:, :-1]], axis=1)
    nxt = jnp.concatenate([p[:, 1:], zero], axis=1)
    return 0.5 * (prev + nxt)


def axial_rope_tables(n_tokens):
    rows = n_tokens // GRID_W
    row = jnp.repeat(jnp.arange(rows, dtype=jnp.float32), GRID_W)
    col = jnp.tile(jnp.arange(GRID_W, dtype=jnp.float32), rows)
    n_freq = ROPE_DIM // 4
    inv_freq = ROPE_THETA ** (-jnp.arange(n_freq, dtype=jnp.float32) / n_freq)
    ang_r = row[:, None] * inv_freq
    ang_c = col[:, None] * inv_freq
    ang = jnp.concatenate([ang_r, ang_r, ang_c, ang_c], axis=-1)
    return jnp.cos(ang), jnp.sin(ang)


def apply_axial_rope(x, cos, sin):
    xs = x.reshape(x.shape[:-1] + (2, 2, ROPE_DIM // 4))
    rot = jnp.stack([-xs[..., 1, :], xs[..., 0, :]], axis=-2).reshape(x.shape)
    return x * cos.astype(x.dtype) + rot * sin.astype(x.dtype)


def mla_expand_kv(lat, lp):
    B, L, _ = lat.shape
    kv = (lat[..., :KV_LORA] @ lp['w_kv_b']).reshape(B, L, H_MLA, NOPE_DIM + V_DIM)
    k_rope = jnp.broadcast_to(lat[:, :, None, KV_LORA:], (B, L, H_MLA, ROPE_DIM))
    k = jnp.concatenate([kv[..., :NOPE_DIM], k_rope], axis=-1)
    return k, kv[..., NOPE_DIM:]


def block_attention(q, k, v):
    B, T, H, _ = q.shape
    qb = jnp.moveaxis(q.reshape(B, T // Q_BLOCK, Q_BLOCK, H, QK_DIM), 1, 0)
    scale = QK_DIM ** -0.5

    def one_block(q_blk):
        s = jnp.einsum('bqhd,bkhd->bhqk', q_blk, k, preferred_element_type=jnp.float32) * scale
        prob = jax.nn.softmax(s, axis=-1).astype(v.dtype)
        return jnp.einsum('bhqk,bkhd->bqhd', prob, v)

    o = lax.map(one_block, qb)
    return jnp.moveaxis(o, 0, 1).reshape(B, T, H * V_DIM)


def mla_mixer(p, lp, ctx_cache):
    B, T, _ = p.shape
    c_q = rms_norm(p[..., :Q_LORA], lp['g_q_a'])
    c_kv = rms_norm(p[..., Q_LORA:Q_LORA + KV_LORA], lp['g_kv_a'])
    k_rope = p[..., Q_LORA + KV_LORA:]
    q = (c_q @ lp['w_q_b']).reshape(B, T, H_MLA, QK_DIM)
    if ctx_cache is None:
        new_cache = jnp.concatenate([c_kv, k_rope], axis=-1)
        k, v = mla_expand_kv(new_cache, lp)
    else:
        new_cache = None
        cos, sin = axial_rope_tables(T)
        q = jnp.concatenate([q[..., :NOPE_DIM], apply_axial_rope(q[..., NOPE_DIM:], cos[:, None], sin[:, None])], axis=-1)
        k_lat, v_lat = mla_expand_kv(jnp.concatenate([c_kv, apply_axial_rope(k_rope, cos, sin)], axis=-1), lp)
        k_ctx, v_ctx = mla_expand_kv(ctx_cache.astype(p.dtype), lp)
        k = jnp.concatenate([k_ctx, k_lat], axis=1)
        v = jnp.concatenate([v_ctx, v_lat], axis=1)
    o = block_attention(q, k, v)
    return rms_norm(o, lp['g_mla_out']), new_cache


def wkv7_scan(r, w, k, v, a, b, s0, reverse):
    def step(s, inp):
        r_t, w_t, k_t, v_t, a_t, b_t = inp
        sa = jnp.einsum('bhij,bhj->bhi', s, a_t)
        s = s * w_t[:, :, None, :] + sa[..., None] * b_t[:, :, None, :] + v_t[..., None] * k_t[:, :, None, :]
        return s, jnp.einsum('bhij,bhj->bhi', s, r_t)

    xs = tuple(jnp.moveaxis(z, 1, 0) for z in (r, w, k, v, a, b))
    s_fin, ys = lax.scan(step, s0, xs, reverse=reverse)
    return jnp.moveaxis(ys, 0, 1), s_fin


def rwkv7_mixer(p, lp, ctx_state):
    B, T, _ = p.shape
    f32 = jnp.float32
    p = p + (centred_shift(p) - p) * lp['mu_rwkv']
    r, k, v, wl, al, gl = jnp.split(p, np.cumsum([D_RWKV, D_RWKV, D_RWKV, 2 * W_LORA, 2 * A_LORA]).tolist(), axis=-1)
    heads = lambda z: z.astype(f32).reshape(B, T, H_RWKV, N_RWKV)
    wl = wl.reshape(B, T, 2, W_LORA)
    al = al.reshape(B, T, 2, A_LORA)
    kk = heads(k * lp['k_k'])
    kk = kk / jnp.maximum(jnp.sqrt(jnp.sum(kk * kk, axis=-1, keepdims=True)), 1e-12)
    rh, vh = heads(r), heads(v)
    ys, bonuses, finals = [], [], []
    for d, reverse in ((0, False), (1, True)):
        w = -jax.nn.softplus(-(lp['w0_rwkv'][d] + jnp.tanh(wl[:, :, d]) @ lp['w2_rwkv'][d])) - 0.5
        decay = jnp.exp(-jnp.exp(heads(w)))
        a = jax.nn.sigmoid(lp['a0_rwkv'][d] + al[:, :, d] @ lp['a2_rwkv'][d])
        kd = heads(k * (1 + (a - 1) * lp['k_a']))
        s0 = jnp.zeros((B, H_RWKV, N_RWKV, N_RWKV), f32) if ctx_state is None else ctx_state[:, d].astype(f32)
        y, s_fin = wkv7_scan(rh, decay, kd, vh, -kk, kk * heads(a), s0, reverse)
        ys.append(y)
        bonuses.append(jnp.sum(rh * kd * lp['r_k'].astype(f32), axis=-1, keepdims=True) * vh)
        finals.append(s_fin)
    y = ys[0] + ys[1]
    mu = jnp.mean(y, axis=-1, keepdims=True)
    var = jnp.mean(jnp.square(y - mu), axis=-1, keepdims=True)
    y = ((y - mu) * lax.rsqrt(var + GN_EPS)).reshape(B, T, D_RWKV) * lp['ln_x_w'].astype(f32) + lp['ln_x_b'].astype(f32)
    y = y + (bonuses[0] + bonuses[1]).reshape(B, T, D_RWKV)
    out = (y * (jax.nn.sigmoid(gl) @ lp['g2_rwkv']).astype(f32)).astype(p.dtype)
    return out, (jnp.stack(finals, axis=1) if ctx_state is None else None)


def depthwise_conv(x, w, b, pad):
    y = lax.conv_general_dilated(x, w[:, None, :].astype(x.dtype), window_strides=(1,), padding=[pad],
                                 dimension_numbers=('NWC', 'WIO', 'NWC'), feature_group_count=x.shape[-1])
    return y + b


def linear_scan(a, u, h0, reverse):
    def combine(left, right):
        return left[0] * right[0], right[0] * left[1] + right[1]

    a_cum, h = lax.associative_scan(combine, (a, u), axis=1, reverse=reverse)
    return a_cum * h0[:, None] + h


def rglru_mixer(p, lp, ctx_state):
    B, T, _ = p.shape
    f32 = jnp.float32
    xb, gb = p[..., :D_LRU], p[..., D_LRU:]
    hs, finals = [], []
    for d, reverse, pad in ((0, False, (CONV_W - 1, 0)), (1, True, (0, CONV_W - 1))):
        xc = depthwise_conv(xb, lp['conv_w'][d], lp['conv_b'][d], pad)
        xh = xc.reshape(B, T, H_LRU, BS_LRU)
        gate_r = jax.nn.sigmoid(jnp.einsum('bthi,hij->bthj', xh, lp['w_lru_a'][d]).reshape(B, T, D_LRU) + lp['b_lru_a'][d])
        gate_i = jax.nn.sigmoid(jnp.einsum('bthi,hij->bthj', xh, lp['w_lru_x'][d]).reshape(B, T, D_LRU) + lp['b_lru_x'][d])
        log_a = -LRU_C * gate_r.astype(f32) * jax.nn.softplus(-lp['lam'][d].astype(f32))
        u = jnp.sqrt(-jnp.expm1(2.0 * log_a)) * (gate_i * xc).astype(f32)
        h0 = jnp.zeros((B, D_LRU), f32) if ctx_state is None else ctx_state[:, d].astype(f32)
        h = linear_scan(jnp.exp(log_a), u, h0, reverse)
        hs.append(h)
        finals.append(h[:, 0] if reverse else h[:, -1])
    out = (hs[0] + hs[1]).astype(p.dtype) * jax.nn.gelu(gb)
    return rms_norm(out, lp['g_lru_out']), (jnp.stack(finals, axis=1) if ctx_state is None else None)


def token_mixing(h, lp, ctx):
    proj = h @ lp['w_in']
    p_mla = proj[..., :P_MLA]
    p_rwkv = proj[..., P_MLA:P_MLA + P_RWKV]
    p_lru = proj[..., P_MLA + P_RWKV:]
    ctx_mla, ctx_rwkv, ctx_lru = (None, None, None) if ctx is None else ctx
    o_mla, s_mla = mla_mixer(p_mla, lp, ctx_mla)
    o_rwkv, s_rwkv = rwkv7_mixer(p_rwkv, lp, ctx_rwkv)
    o_lru, s_lru = rglru_mixer(p_lru, lp, ctx_lru)
    out = jnp.concatenate([o_mla, o_rwkv, o_lru], axis=-1) @ lp['w_out']
    return out, (s_mla, s_rwkv, s_lru)


def routed_experts(xf, idx, wts, w_gate, w_up, w_down):
    n_assign = idx.size
    n_blocks = -(-n_assign // MOE_BLOCK) + N_EXPERTS
    e_flat = idx.reshape(-1)
    order = jnp.argsort(e_flat)
    e_sorted = e_flat[order]
    counts = jnp.bincount(e_flat, length=N_EXPERTS)
    blocks_per_e = (counts + MOE_BLOCK - 1) // MOE_BLOCK
    blk_end = jnp.cumsum(blocks_per_e)
    blk_start = blk_end - blocks_per_e
    seg_start = jnp.cumsum(counts) - counts
    dest = blk_start[e_sorted] * MOE_BLOCK + jnp.arange(n_assign) - seg_start[e_sorted]
    slot_tok = jnp.zeros((n_blocks * MOE_BLOCK,), jnp.int32).at[dest].set((order // TOP_K).astype(jnp.int32))
    slot_w = jnp.zeros((n_blocks * MOE_BLOCK,), xf.dtype).at[dest].set(wts.reshape(-1)[order])
    block_e = jnp.minimum(jnp.searchsorted(blk_end, jnp.arange(n_blocks), side='right'), N_EXPERTS - 1)

    def step(out, blk):
        tok, wt, e = blk
        y = swiglu(xf[tok], w_gate[e], w_up[e], w_down[e])
        return out.at[tok].add(y * wt[:, None]), None

    out, _ = lax.scan(step, jnp.zeros_like(xf),
                      (slot_tok.reshape(n_blocks, MOE_BLOCK), slot_w.reshape(n_blocks, MOE_BLOCK), block_e))
    return out


def moe_ffn(h, lp):
    B, T, D = h.shape
    xf = h.reshape(B * T, D)
    scores = jax.nn.sigmoid(jnp.einsum('td,de->te', xf, lp['w_router'], preferred_element_type=jnp.float32))
    sel = scores + lp['b_router'].astype(jnp.float32)
    grp_score = jnp.sum(lax.top_k(sel.reshape(B * T, N_GROUPS, N_EXPERTS // N_GROUPS), 2)[0], axis=-1)
    _, gidx = lax.top_k(grp_score, TOPK_GROUPS)
    gmask = jnp.any(gidx[..., None] == jnp.arange(N_GROUPS), axis=-2)
    sel = jnp.where(jnp.repeat(gmask, N_EXPERTS // N_GROUPS, axis=-1), sel, -jnp.inf)
    _, idx = lax.top_k(sel, TOP_K)
    wts = jnp.take_along_axis(scores, idx, axis=-1)
    wts = wts / jnp.sum(wts, axis=-1, keepdims=True) * ROUTED_SCALE
    routed = routed_experts(xf, idx, wts.astype(xf.dtype), lp['w_e_gate'], lp['w_e_up'], lp['w_e_down'])
    shared = swiglu(xf, lp['w_s_gate'], lp['w_s_up'], lp['w_s_down'])
    return (routed + shared).reshape(B, T, D)


def trunk_layer(x, cvec, lp, ctx):
    sh1, sc1, g1, sh2, sc2, g2 = modulation(cvec, lp['w_ada'], lp['b_ada'])
    h = rms_norm(x, lp['g_pre_mix']) * (1 + sc1) + sh1
    m, new_ctx = token_mixing(h, lp, ctx)
    x = x + g1 * rms_norm(m, lp['g_post_mix'])
    h = rms_norm(x, lp['g_pre_ffn']) * (1 + sc2) + sh2
    x = x + g2 * rms_norm(moe_ffn(h, lp), lp['g_post_ffn'])
    return x, new_ctx


def setup_inputs(seed: int = 0) -> dict:
    key = jax.random.key(seed)
    keys = iter(jax.random.split(key, 64))
    f32 = jnp.float32

    def nrm(shape, scale):
        return scale * jax.random.normal(next(keys), shape, f32)

    def gain(shape):
        return 1.0 + 0.05 * jax.random.normal(next(keys), shape, f32)

    def unif(shape, lo, hi):
        return jax.random.uniform(next(keys), shape, f32, lo, hi)

    L, D = DEPTH, D_MODEL
    lru_a = unif((L, 2, D_LRU), 0.9, 0.999)
    return {
        'x_prompt': nrm((BATCH, SEQ, D), 1.0),
        'x_sample': nrm((DEC_BATCH, DEC_SEQ, D), 1.0),
        'c': nrm((DEC_BATCH, D), 1.0),
        'cache_mla': nrm((DEC_BATCH, L, PAST_LEN, KV_LORA + ROPE_DIM), 1.0),
        'state_rwkv': nrm((DEC_BATCH, L, 2, H_RWKV, N_RWKV, N_RWKV), 0.5),
        'state_lru': nrm((DEC_BATCH, L, 2, D_LRU), 0.5),
        'c_ctx': nrm((D,), 1.0),
        'w_ada': nrm((L, D, 6 * D), 0.1 * D ** -0.5),
        'b_ada': nrm((L, 6 * D), 0.02),
        'g_pre_mix': gain((L, D)),
        'g_post_mix': gain((L, D)),
        'g_pre_ffn': gain((L, D)),
        'g_post_ffn': gain((L, D)),
        'w_in': nrm((L, D, P_IN), D ** -0.5),
        'g_q_a': gain((L, Q_LORA)),
        'w_q_b': nrm((L, Q_LORA, H_MLA * QK_DIM), Q_LORA ** -0.5),
        'g_kv_a': gain((L, KV_LORA)),
        'w_kv_b': nrm((L, KV_LORA, H_MLA * (NOPE_DIM + V_DIM)), KV_LORA ** -0.5),
        'g_mla_out': gain((L, D_MLA_OUT)),
        'mu_rwkv': unif((L, P_RWKV), 0.0, 1.0),
        'w0_rwkv': unif((L, 2, D_RWKV), -6.0, 1.0),
        'w2_rwkv': nrm((L, 2, W_LORA, D_RWKV), 0.5 * W_LORA ** -0.5),
        'a0_rwkv': nrm((L, 2, D_RWKV), 0.1),
        'a2_rwkv': nrm((L, 2, A_LORA, D_RWKV), A_LORA ** -0.5),
        'g2_rwkv': nrm((L, G_LORA, D_RWKV), G_LORA ** -0.5),
        'k_k': 0.85 + nrm((L, D_RWKV), 0.05),
        'k_a': 1.0 + nrm((L, D_RWKV), 0.05),
        'r_k': nrm((L, H_RWKV, N_RWKV), 0.1),
        'ln_x_w': gain((L, D_RWKV)),
        'ln_x_b': nrm((L, D_RWKV), 0.02),
        'conv_w': nrm((L, 2, CONV_W, D_LRU), CONV_W ** -0.5),
        'conv_b': nrm((L, 2, D_LRU), 0.02),
        'w_lru_a': nrm((L, 2, H_LRU, BS_LRU, BS_LRU), BS_LRU ** -0.5),
        'b_lru_a': nrm((L, 2, D_LRU), 0.02),
        'w_lru_x': nrm((L, 2, H_LRU, BS_LRU, BS_LRU), BS_LRU ** -0.5),
        'b_lru_x': nrm((L, 2, D_LRU), 0.02),
        'lam': jnp.log(lru_a) - jnp.log1p(-lru_a),
        'g_lru_out': gain((L, D_LRU)),
        'w_out': nrm((L, D_MIX, D), D_MIX ** -0.5),
        'w_router': nrm((L, D, N_EXPERTS), D ** -0.5),
        'b_router': nrm((L, N_EXPERTS), 0.01),
        'w_e_gate': nrm((L, N_EXPERTS, D, D_EXPERT), D ** -0.5),
        'w_e_up': nrm((L, N_EXPERTS, D, D_EXPERT), D ** -0.5),
        'w_e_down': nrm((L, N_EXPERTS, D_EXPERT, D), D_EXPERT ** -0.5),
        'w_s_gate': nrm((L, D, D_SHARED), D ** -0.5),
        'w_s_up': nrm((L, D, D_SHARED), D ** -0.5),
        'w_s_down': nrm((L, D_SHARED, D), D_SHARED ** -0.5),
    }


def reference(x_prompt, x_sample, c, cache_mla, state_rwkv, state_lru, c_ctx, w_ada, b_ada, g_pre_mix, g_post_mix,
              g_pre_ffn, g_post_ffn, w_in, g_q_a, w_q_b, g_kv_a, w_kv_b, g_mla_out, mu_rwkv, w0_rwkv, w2_rwkv,
              a0_rwkv, a2_rwkv, g2_rwkv, k_k, k_a, r_k, ln_x_w, ln_x_b, conv_w, conv_b, w_lru_a, b_lru_a,
              w_lru_x, b_lru_x, lam, g_lru_out, w_out, w_router, b_router, w_e_gate, w_e_up, w_e_down,
              w_s_gate, w_s_up, w_s_down):
    layer_params = (('w_ada', w_ada), ('b_ada', b_ada), ('g_pre_mix', g_pre_mix), ('g_post_mix', g_post_mix),
                    ('g_pre_ffn', g_pre_ffn), ('g_post_ffn', g_post_ffn), ('w_in', w_in), ('g_q_a', g_q_a),
                    ('w_q_b', w_q_b), ('g_kv_a', g_kv_a), ('w_kv_b', w_kv_b), ('g_mla_out', g_mla_out),
                    ('mu_rwkv', mu_rwkv), ('w0_rwkv', w0_rwkv), ('w2_rwkv', w2_rwkv), ('a0_rwkv', a0_rwkv),
                    ('a2_rwkv', a2_rwkv), ('g2_rwkv', g2_rwkv), ('k_k', k_k), ('k_a', k_a), ('r_k', r_k),
                    ('ln_x_w', ln_x_w), ('ln_x_b', ln_x_b), ('conv_w', conv_w), ('conv_b', conv_b),
                    ('w_lru_a', w_lru_a), ('b_lru_a', b_lru_a), ('w_lru_x', w_lru_x), ('b_lru_x', b_lru_x),
                    ('lam', lam), ('g_lru_out', g_lru_out), ('w_out', w_out), ('w_router', w_router),
                    ('b_router', b_router), ('w_e_gate', w_e_gate), ('w_e_up', w_e_up), ('w_e_down', w_e_down),
                    ('w_s_gate', w_s_gate), ('w_s_up', w_s_up), ('w_s_down', w_s_down))
    y_prompt, y_sample = x_prompt, x_sample
    mla_states, rwkv_states, lru_states = [], [], []
    for l in range(DEPTH):
        lp = {name: arr[l] for name, arr in layer_params}
        y_prompt, (s_mla, s_rwkv, s_lru) = trunk_layer(y_prompt, c_ctx, lp, None)
        mla_states.append(s_mla)
        rwkv_states.append(s_rwkv)
        lru_states.append(s_lru)
        y_sample, _ = trunk_layer(y_sample, c, lp, (cache_mla[:, l], state_rwkv[:, l], state_lru[:, l]))
    new_cache_mla = jnp.stack(mla_states, axis=1)
    new_state_rwkv = jnp.stack(rwkv_states, axis=1)
    new_state_lru = jnp.stack(lru_states, axis=1)
    return (y_prompt, y_sample, new_cache_mla, new_state_rwkv, new_state_lru)
```

```python
import functools

import numpy as np
import jax
import jax.numpy as jnp
from jax import lax
from jax.experimental import pallas as pl
from jax.experimental.pallas import tpu as pltpu

F32 = jnp.float32
BF16 = jnp.bfloat16
I32 = jnp.int32

D_MODEL = 1024
GRID_W = 64
EPS = 1e-6
H_MLA = 4
Q_LORA = 256
KV_LORA = 128
NOPE_DIM = 64
ROPE_DIM = 32
V_DIM = 128
QK_DIM = NOPE_DIM + ROPE_DIM
ROPE_THETA = 10000.0
HEAD_PAD = 128
H_RWKV = 4
N_RWKV = 64
D_RWKV = H_RWKV * N_RWKV
W_LORA = 64
A_LORA = 64
G_LORA = 128
GN_EPS = 64e-5
D_LRU = 256
H_LRU = 4
BS_LRU = D_LRU // H_LRU
CONV_W = 4
LRU_C = 8.0
D_MLA_OUT = H_MLA * V_DIM
P_MLA_PAD = Q_LORA + KV_LORA + HEAD_PAD
P_RWKV = 3 * D_RWKV + 2 * W_LORA + 2 * A_LORA + G_LORA
P_LRU = 2 * D_LRU
N_EXPERTS = 64
TOP_K = 6
N_GROUPS = 8
TOPK_GROUPS = 4
GROUP_SIZE = N_EXPERTS // N_GROUPS
D_EXPERT = 256
ROUTED_SCALE = 2.5

SUBLANES = 8
RWKV_CHUNK = 64
MOE_BLOCK = 256
ROUTER_TILE = 512
TOKEN_TILE = 256
SLOT_ROWS = 8
VMEM_LIMIT = 56 * 1024 * 1024


def _cparams(*sem):
    return pltpu.CompilerParams(dimension_semantics=sem, vmem_limit_bytes=VMEM_LIMIT)


def _sigmoid(x):
    return 1.0 / (1.0 + jnp.exp(-x))


def _softplus(x):
    return jnp.maximum(x, 0.0) + jnp.log1p(jnp.exp(-jnp.abs(x)))


def _expm1(z):
    e = jnp.exp(z)
    direct = (e == 1.0) | (z < -1.0)
    corrected = (e - 1.0) * z / jnp.log(jnp.where(direct, 2.0, e))
    return jnp.where(e == 1.0, z, jnp.where(z < -1.0, e - 1.0, corrected))


def _rms(x, g):
    return x * lax.rsqrt(jnp.mean(x * x, axis=-1, keepdims=True) + EPS) * g


def _split2(x):
    hi = x.astype(BF16)
    lo = (x - hi.astype(F32)).astype(BF16)
    return hi, lo


def _split3(x):
    hi = x.astype(BF16)
    r1 = x - hi.astype(F32)
    mid = r1.astype(BF16)
    lo = (r1 - mid.astype(F32)).astype(BF16)
    return hi, mid, lo


_NN = (((1,), (0,)), ((), ()))
_NT = (((1,), (1,)), ((), ()))


def _dg(a, b, dims):
    return lax.dot_general(a, b, dims, preferred_element_type=F32)


def _dot1(a, b, dims=_NN):
    return _dg(a.astype(BF16), b.astype(BF16), dims)


def _dot3(a, b, dims=_NN):
    ah, al = _split2(a)
    bh, bl = _split2(b)
    return _dg(ah, bh, dims) + (_dg(ah, bl, dims) + _dg(al, bh, dims))


def _dot_x(a, b_exact, dims=_NN):
    h, m, l = _split3(a)
    bb = b_exact.astype(BF16)
    return _dg(h, bb, dims) + (_dg(m, bb, dims) + _dg(l, bb, dims))


def _x_dot(a_exact, b, dims=_NN):
    h, m, l = _split3(b)
    aa = a_exact.astype(BF16)
    return _dg(aa, h, dims) + (_dg(aa, m, dims) + _dg(aa, l, dims))


def _head_ones(n, seg):
    r = lax.broadcasted_iota(I32, (n, n), 0) // seg
    c = lax.broadcasted_iota(I32, (n, n), 1) // seg
    return (r == c).astype(F32)


def _ada_body(c_ref, w_ref, b_ref, o_ref):
    cv = c_ref[...]
    s = cv * _sigmoid(cv)
    o_ref[0] = _dot3(s, w_ref[0]) + b_ref[0]


def _ada(cvecs, w_ada, b_ada):
    n_layers, d, d6 = w_ada.shape
    rows = cvecs.shape[0]
    nt = d6 // d
    return pl.pallas_call(
        _ada_body,
        grid=(n_layers, nt),
        in_specs=[
            pl.BlockSpec((rows, d), lambda l, j: (0, 0)),
            pl.BlockSpec((1, d, d), lambda l, j: (l, 0, j)),
            pl.BlockSpec((1, 1, d), lambda l, j: (l, 0, j)),
        ],
        out_specs=pl.BlockSpec((1, rows, d), lambda l, j: (l, 0, j)),
        out_shape=jax.ShapeDtypeStruct((n_layers, rows, d6), F32),
        compiler_params=_cparams("parallel", "parallel"),
        name="ada",
    )(cvecs, w_ada, b_ada.reshape(n_layers, 1, d6))


def _mod_spec(per_batch):
    if per_batch:
        return pl.BlockSpec((1, 1, D_MODEL), lambda b, i: (b, 0, 0))
    return pl.BlockSpec((1, 1, D_MODEL), lambda b, i: (0, 0, 0))


def _full(shape):
    nd = len(shape)
    return pl.BlockSpec(shape, lambda *_: (0,) * nd)


def _in_proj_body(x_ref, g_ref, sc_ref, sh_ref, w1_ref, w2_ref, w3_ref, o1_ref, o2_ref, o3_ref):
    h = _rms(x_ref[0], g_ref[...]) * (1.0 + sc_ref[0]) + sh_ref[0]
    hb = h.astype(BF16)
    o1_ref[0] = _dg(hb, w1_ref[...], _NN)
    o2_ref[0] = _dg(hb, w2_ref[...], _NN)
    o3_ref[0] = _dg(hb, w3_ref[...], _NN)


def _in_proj(x, g, sc, sh, w_mla, w_rwkv, w_lru, tt):
    nb, t, d = x.shape
    per_batch = sc.shape[0] > 1
    outs = [w_mla.shape[1], w_rwkv.shape[1], w_lru.shape[1]]
    return pl.pallas_call(
        _in_proj_body,
        grid=(nb, t // tt),
        in_specs=[
            pl.BlockSpec((1, tt, d), lambda b, i: (b, i, 0)),
            _full((1, d)),
            _mod_spec(per_batch),
            _mod_spec(per_batch),
            _full(w_mla.shape),
            _full(w_rwkv.shape),
            _full(w_lru.shape),
        ],
        out_specs=[pl.BlockSpec((1, tt, n), lambda b, i: (b, i, 0)) for n in outs],
        out_shape=[jax.ShapeDtypeStruct((nb, t, n), F32) for n in outs],
        compiler_params=_cparams("parallel", "parallel"),
        name="in_proj",
    )(x, g, sc, sh, w_mla, w_rwkv, w_lru)


def _rope_lanes():
    lane = lax.broadcasted_iota(I32, (1, HEAD_PAD), 1)
    return ((lane >= NOPE_DIM) & (lane < QK_DIM)).astype(F32)


def _mla_prep_body(rope, *refs):
    if rope:
        (p_ref, gq_ref, gkv_ref, wq_ref, wqr_ref, wk_ref, wv_ref, cq_ref, sq_ref, cr_ref, sr_ref,
         q_ref, k_ref, v_ref) = refs
    else:
        p_ref, gq_ref, gkv_ref, wq_ref, wk_ref, wv_ref, q_ref, k_ref, v_ref, c_ref = refs
    p = p_ref[0]
    cq = _rms(p[:, :Q_LORA], gq_ref[...])
    ckv = _rms(p[:, Q_LORA:Q_LORA + KV_LORA], gkv_ref[...])
    kr = p[:, Q_LORA + KV_LORA:]
    cqb = cq.astype(BF16)
    ckvb = ckv.astype(BF16)
    scale = QK_DIM ** -0.5
    qa = _dg(cqb, wq_ref[...], _NN) * scale
    ka = _dg(ckvb, wk_ref[...], _NN)
    va = _dg(ckvb, wv_ref[...], _NN)
    if rope:
        qr = _dg(cqb, wqr_ref[...], _NN) * scale
        krp = kr * cr_ref[...] + pltpu.roll(kr, ROPE_DIM, 1) * sr_ref[...]
    else:
        krp = kr * _rope_lanes()
    for h in range(H_MLA):
        sl = slice(h * HEAD_PAD, (h + 1) * HEAD_PAD)
        qh = qa[:, sl]
        if rope:
            qh = qh * cq_ref[...] + qr[:, sl] * sq_ref[...]
        q_ref[0, h] = qh.astype(BF16)
        k_ref[0, h] = (ka[:, sl] + krp).astype(BF16)
        v_ref[0, h] = va[:, sl].astype(BF16)
    if not rope:
        c_ref[0, :, :KV_LORA] = ckv
        c_ref[0, :, KV_LORA:] = pltpu.roll(kr, HEAD_PAD - NOPE_DIM, 1)[:, :ROPE_DIM]


def _mla_prep(p_mla, gq, gkv, wq, wqr, wk, wv, tables, tt):
    nb, t, pw = p_mla.shape
    rope = tables is not None
    hw = H_MLA * HEAD_PAD
    head_spec = pl.BlockSpec((1, H_MLA, tt, HEAD_PAD), lambda b, i: (b, 0, i, 0))
    head_shape = jax.ShapeDtypeStruct((nb, H_MLA, t, HEAD_PAD), BF16)
    in_specs = [pl.BlockSpec((1, tt, pw), lambda b, i: (b, i, 0)), _full(gq.shape), _full(gkv.shape), _full(wq.shape)]
    args = [p_mla, gq, gkv, wq]
    if rope:
        in_specs.append(_full(wqr.shape))
        args.append(wqr)
    in_specs += [_full(wk.shape), _full(wv.shape)]
    args += [wk, wv]
    out_specs = [head_spec, head_spec, head_spec]
    out_shape = [head_shape, head_shape, head_shape]
    if rope:
        in_specs += [pl.BlockSpec((tt, HEAD_PAD), lambda b, i: (i, 0))] * 4
        args += list(tables)
    else:
        out_specs.append(pl.BlockSpec((1, tt, KV_LORA + ROPE_DIM), lambda b, i: (b, i, 0)))
        out_shape.append(jax.ShapeDtypeStruct((nb, t, KV_LORA + ROPE_DIM), F32))
    del hw
    return pl.pallas_call(
        functools.partial(_mla_prep_body, rope),
        grid=(nb, t // tt),
        in_specs=in_specs,
        out_specs=out_specs,
        out_shape=out_shape,
        compiler_params=_cparams("parallel", "parallel"),
        name="mla_prep_rope" if rope else "mla_prep",
    )(*args)


def _ctx_kv_body(lat_ref, kr_ref, wk_ref, wv_ref, k_ref, v_ref):
    latb = lat_ref[0].astype(BF16)
    ka = _dg(latb, wk_ref[...], _NN)
    va = _dg(latb, wv_ref[...], _NN)
    kr = kr_ref[0]
    for h in range(H_MLA):
        sl = slice(h * HEAD_PAD, (h + 1) * HEAD_PAD)
        k_ref[0, h] = (ka[:, sl] + kr).astype(BF16)
        v_ref[0, h] = va[:, sl].astype(BF16)


def _ctx_kv(lat, kr_placed, wk, wv):
    nb, s, _ = lat.shape
    head_spec = pl.BlockSpec((1, H_MLA, s, HEAD_PAD), lambda b: (b, 0, 0, 0))
    head_shape = jax.ShapeDtypeStruct((nb, H_MLA, s, HEAD_PAD), BF16)
    return pl.pallas_call(
        _ctx_kv_body,
        grid=(nb,),
        in_specs=[
            pl.BlockSpec((1, s, KV_LORA), lambda b: (b, 0, 0)),
            pl.BlockSpec((1, s, HEAD_PAD), lambda b: (b, 0, 0)),
            _full(wk.shape),
            _full(wv.shape),
        ],
        out_specs=[head_spec, head_spec],
        out_shape=[head_shape, head_shape],
        compiler_params=_cparams("parallel"),
        name="ctx_kv",
    )(lat, kr_placed, wk, wv)


def _attn_body(has_ctx, *refs):
    if has_ctx:
        q_ref, k_ref, v_ref, kc_ref, vc_ref, o_ref = refs
    else:
        q_ref, k_ref, v_ref, o_ref = refs
    q = q_ref[0, 0]
    s = _dg(q, k_ref[0, 0], _NT)
    m = jnp.max(s, axis=-1, keepdims=True)
    if has_ctx:
        sc = _dg(q, kc_ref[0, 0], _NT)
        m = jnp.maximum(m, jnp.max(sc, axis=-1, keepdims=True))
    p = jnp.exp(s - m)
    l = jnp.sum(p, axis=-1, keepdims=True)
    o = _dg(p.astype(BF16), v_ref[0, 0], _NN)
    if has_ctx:
        pc = jnp.exp(sc - m)
        l = l + jnp.sum(pc, axis=-1, keepdims=True)
        o = o + _dg(pc.astype(BF16), vc_ref[0, 0], _NN)
    o_ref[0] = o / l


def _attention(q, k, v, kc, vc, tq):
    nb, nh, t, hd = q.shape
    s = k.shape[2]
    has_ctx = kc is not None
    in_specs = [
        pl.BlockSpec((1, 1, tq, hd), lambda b, h, i: (b, h, i, 0)),
        pl.BlockSpec((1, 1, s, hd), lambda b, h, i: (b, h, 0, 0)),
        pl.BlockSpec((1, 1, s, V_DIM), lambda b, h, i: (b, h, 0, 0)),
    ]
    args = [q, k, v]
    if has_ctx:
        sc = kc.shape[2]
        in_specs += [
            pl.BlockSpec((1, 1, sc, hd), lambda b, h, i: (b, h, 0, 0)),
            pl.BlockSpec((1, 1, sc, V_DIM), lambda b, h, i: (b, h, 0, 0)),
        ]
        args += [kc, vc]
    return pl.pallas_call(
        functools.partial(_attn_body, has_ctx),
        grid=(nb, nh, t // tq),
        in_specs=in_specs,
        out_specs=pl.BlockSpec((1, tq, V_DIM), lambda b, h, i: (b, i, h)),
        out_shape=jax.ShapeDtypeStruct((nb, t, nh * V_DIM), F32),
        compiler_params=_cparams("parallel", "parallel", "parallel"),
        name="attention",
    )(*args)


def _rwkv_prep_body(p_ref, hp_ref, hn_ref, mu_ref, kk_ref, ka_ref, rk_ref, w0_ref, w2_ref, a0_ref, a2_ref, g2_ref,
                    r_out, v_out, kk_out, lw_out, a_out, kd_out, g_out, bonus_out):
    p = p_ref[0]
    tt = p.shape[0]
    row = lax.broadcasted_iota(I32, (tt, 1), 0)
    prev = jnp.where(row == 0, hp_ref[0, 0], pltpu.roll(p, 1, 0))
    nxt = jnp.where(row == tt - 1, hn_ref[0, 0], pltpu.roll(p, tt - 1, 0))
    ps = p + (0.5 * (prev + nxt) - p) * mu_ref[...]
    r = ps[:, :D_RWKV]
    k = ps[:, D_RWKV:2 * D_RWKV]
    v = ps[:, 2 * D_RWKV:3 * D_RWKV]
    wl = ps[:, 3 * D_RWKV:3 * D_RWKV + 2 * W_LORA]
    al = ps[:, 3 * D_RWKV + 2 * W_LORA:3 * D_RWKV + 2 * W_LORA + 2 * A_LORA]
    gl = ps[:, 3 * D_RWKV + 2 * W_LORA + 2 * A_LORA:]
    ones = _head_ones(D_RWKV, N_RWKV)
    kk = k * kk_ref[...]
    kk = kk / jnp.maximum(jnp.sqrt(_dot_x(kk * kk, ones)), 1e-12)
    wlin = w0_ref[...] + _dot3(jnp.tanh(wl), w2_ref[...])
    lw = -jnp.exp(-_softplus(-wlin) - 0.5)
    ag = _sigmoid(a0_ref[...] + _dot3(al, a2_ref[...]))
    bonus = jnp.zeros_like(v)
    for d in range(2):
        sl = slice(d * D_RWKV, (d + 1) * D_RWKV)
        kd = k * (1.0 + (ag[:, sl] - 1.0) * ka_ref[...])
        kd_out[0, :, sl] = kd
        bonus = bonus + _dot_x(r * kd * rk_ref[...], ones) * v
    r_out[0] = r
    v_out[0] = v
    kk_out[0] = kk
    lw_out[0] = lw
    a_out[0] = ag
    g_out[0] = _dot1(_sigmoid(gl), g2_ref[...])
    bonus_out[0] = bonus


def _halo_rows(x, tt, width):
    nb, t, c = x.shape
    nt = t // tt
    xt = x.reshape(nb, nt, tt, c)
    zero = jnp.zeros((nb, 1, width, c), x.dtype)
    prev = jnp.concatenate([zero, xt[:, :-1, tt - width:]], axis=1)
    nxt = jnp.concatenate([xt[:, 1:, :width], zero], axis=1)
    return prev, nxt


def _rwkv_prep(p_rwkv, lp, tt):
    nb, t, pw = p_rwkv.shape
    hp, hn = _halo_rows(p_rwkv, tt, 1)
    tok = lambda n: pl.BlockSpec((1, tt, n), lambda b, i: (b, i, 0))
    halo = pl.BlockSpec((1, 1, 1, pw), lambda b, i: (b, i, 0, 0))
    small = [lp['mu_rwkv'], lp['k_k'], lp['k_a'], lp['r_k'], lp['w0_cat'], lp['w2_bd'], lp['a0_cat'], lp['a2_bd'],
             lp['g2_rwkv']]
    widths = [D_RWKV, D_RWKV, D_RWKV, 2 * D_RWKV, 2 * D_RWKV, 2 * D_RWKV, D_RWKV, D_RWKV]
    return pl.pallas_call(
        _rwkv_prep_body,
        grid=(nb, t // tt),
        in_specs=[tok(pw), halo, halo] + [_full(a.shape) for a in small],
        out_specs=[tok(n) for n in widths],
        out_shape=[jax.ShapeDtypeStruct((nb, t, n), F32) for n in widths],
        compiler_params=_cparams("parallel", "parallel"),
        name="rwkv_prep",
    )(p_rwkv, hp, hn, *small)


def _rwkv_scan_body(r_ref, v_ref, kk_ref, lw_ref, a_ref, kd_ref, s0_ref, y_ref, sf_ref, s_scr):
    d = pl.program_id(1)
    c = pl.program_id(2)
    nc = pl.num_programs(2)
    cl = r_ref.shape[1]
    hw = D_RWKV

    @pl.when(c == 0)
    def _():
        s_scr[...] = s0_ref[0, 0]

    r = r_ref[0]
    v = v_ref[0]
    kk = kk_ref[0]
    lw = lw_ref[0]
    ag = a_ref[0]
    kd = kd_ref[0]
    sgn = jnp.where(d == 0, 1, -1)

    ti = lax.broadcasted_iota(I32, (cl, cl), 0)
    si = lax.broadcasted_iota(I32, (cl, cl), 1)
    tri = ((ti - si) * sgn >= 0).astype(F32)
    cum = _x_dot(tri, lw)
    tot = jnp.sum(lw, axis=0, keepdims=True)
    e_neg = jnp.exp(-cum)
    e_rem = jnp.exp(tot - cum)
    at = -kk * jnp.exp(cum - lw)
    rt = r * jnp.exp(cum)
    b0 = kk * ag
    bt = b0 * e_neg
    kt = kd * e_neg
    bh = b0 * e_rem
    kh = kd * e_rem
    w_c = jnp.exp(tot)

    lane_head = lax.broadcasted_iota(I32, (1, hw), 1) // N_RWKV
    hms = [(lane_head == h).astype(F32) for h in range(H_RWKV)]
    stack = lambda z: jnp.concatenate([z * hm for hm in hms], axis=0)
    bks = jnp.concatenate([stack(bt), stack(kt)], axis=0)
    vs = stack(v)
    ar = jnp.concatenate([at, rt], axis=0)

    nh = H_RWKV * cl
    ll = _dot3(ar, bks, _NT)
    t2 = lax.broadcasted_iota(I32, (cl, nh), 0)
    s2 = lax.broadcasted_iota(I32, (cl, nh), 1) % cl
    diff = (t2 - s2) * sgn
    strict = diff > 0
    incl = diff >= 0
    l_ab = jnp.where(strict, ll[:cl, :nh], 0.0)
    l_ak = jnp.where(strict, ll[:cl, nh:], 0.0)
    l_rb = jnp.where(incl, ll[cl:, :nh], 0.0)
    l_rk = jnp.where(incl, ll[cl:, nh:], 0.0)

    bd_c = _head_ones(nh, cl)
    x = jnp.concatenate([l_ab] * H_RWKV, axis=0) * bd_c
    eye = (lax.broadcasted_iota(I32, (nh, nh), 0) == lax.broadcasted_iota(I32, (nh, nh), 1)).astype(F32)
    inv = eye + x
    span = 2
    while span < cl:
        x = _dot3(x, x)
        inv = inv + _dot3(inv, x)
        span *= 2

    s_old = s_scr[...]
    g2 = _dot3(ar, s_old, _NT)
    rhs = g2[:cl] + _dot3(l_ak, vs)
    us = _dot3(inv, stack(rhs))
    um = us[:cl]
    for h in range(1, H_RWKV):
        um = um + us[h * cl:(h + 1) * cl]
    y = g2[cl:] + _dot3(jnp.concatenate([l_rb, l_rk], axis=1), jnp.concatenate([us, vs], axis=0))
    y_ref[0, 0] = y

    uv_t = jnp.concatenate([um, v], axis=0).T
    s_new = s_old * w_c + _dot3(uv_t, jnp.concatenate([bh, kh], axis=0)) * _head_ones(hw, N_RWKV)
    s_scr[...] = s_new

    @pl.when(c == nc - 1)
    def _():
        sf_ref[0, 0] = s_new


def _rwkv_scan(r, v, kk, lw, ag, kd, s0):
    nb, t, hw = r.shape
    cl = RWKV_CHUNK
    nc = t // cl

    def cidx(d, c):
        return c + d * (nc - 1 - 2 * c)

    shared = pl.BlockSpec((1, cl, hw), lambda b, d, c: (b, cidx(d, c), 0))
    perdir = pl.BlockSpec((1, cl, hw), lambda b, d, c: (b, cidx(d, c), d))
    state = pl.BlockSpec((1, 1, hw, hw), lambda b, d, c: (b, d, 0, 0))
    return pl.pallas_call(
        _rwkv_scan_body,
        grid=(nb, 2, nc),
        in_specs=[shared, shared, shared, perdir, perdir, perdir, state],
        out_specs=[pl.BlockSpec((1, 1, cl, hw), lambda b, d, c: (d, b, cidx(d, c), 0)), state],
        out_shape=[jax.ShapeDtypeStruct((2, nb, t, hw), F32), jax.ShapeDtypeStruct((nb, 2, hw, hw), F32)],
        scratch_shapes=[pltpu.VMEM((hw, hw), F32)],
        compiler_params=_cparams("parallel", "parallel", "arbitrary"),
        name="rwkv_scan",
    )(r, v, kk, lw, ag, kd, s0)


def _lru_body(reverse, p_ref, halo_ref, cw_ref, cb_ref, wa_ref, ba_ref, wx_ref, bx_ref, lam_ref, h0_ref,
              h_out, fin_out, a_scr, u_scr, carry_scr):
    i = pl.program_id(1)
    nt = pl.num_programs(1)
    tt = p_ref.shape[1]
    n_ext = tt + SUBLANES

    @pl.when(i == 0)
    def _():
        carry_scr[...] = jnp.broadcast_to(h0_ref[0], (SUBLANES, D_LRU))

    xb = p_ref[0, :, :D_LRU]
    halo = halo_ref[0, 0]
    xc = cb_ref[...] + jnp.zeros_like(xb)
    if reverse:
        xe = jnp.concatenate([xb, halo], axis=0)
        for k in range(CONV_W):
            sh = xb if k == 0 else pltpu.roll(xe, n_ext - k, 0)[:tt]
            xc = xc + cw_ref[k:k + 1, :] * sh
    else:
        xe = jnp.concatenate([halo, xb], axis=0)
        for k in range(CONV_W):
            s = CONV_W - 1 - k
            sh = xb if s == 0 else pltpu.roll(xe, s, 0)[SUBLANES:]
            xc = xc + cw_ref[k:k + 1, :] * sh
    gate_r = _sigmoid(_dot3(xc, wa_ref[...]) + ba_ref[...])
    gate_i = _sigmoid(_dot3(xc, wx_ref[...]) + bx_ref[...])
    log_a = -LRU_C * gate_r * _softplus(-lam_ref[...])
    a = jnp.exp(log_a)
    u = jnp.sqrt(-_expm1(2.0 * log_a)) * (gate_i * xc)

    rowm = lax.broadcasted_iota(I32, (tt, 1), 0) % SUBLANES
    for s in (1, 2, 4):
        if reverse:
            a_s = pltpu.roll(a, tt - s, 0)
            u_s = pltpu.roll(u, tt - s, 0)
            m = rowm < SUBLANES - s
        else:
            a_s = pltpu.roll(a, s, 0)
            u_s = pltpu.roll(u, s, 0)
            m = rowm >= s
        u = jnp.where(m, a * u_s + u, u)
        a = jnp.where(m, a * a_s, a)
    a_scr[...] = a
    u_scr[...] = u
    ng = tt // SUBLANES

    def group(g, carry):
        gi = (ng - 1 - g) if reverse else g
        r0 = pl.multiple_of(gi * SUBLANES, SUBLANES)
        hg = a_scr[pl.ds(r0, SUBLANES), :] * carry + u_scr[pl.ds(r0, SUBLANES), :]
        h_out[0, pl.ds(r0, SUBLANES), :] = hg
        edge = hg[0:1, :] if reverse else hg[SUBLANES - 1:SUBLANES, :]
        return jnp.broadcast_to(edge, (SUBLANES, D_LRU))

    carry = lax.fori_loop(0, ng, group, carry_scr[...])
    carry_scr[...] = carry

    @pl.when(i == nt - 1)
    def _():
        fin_out[0] = carry[0:1, :]


def _lru_dir(p_lru, halo, lp, d, h0, tt):
    nb, t, pw = p_lru.shape
    nt = t // tt
    reverse = d == 1
    tidx = (lambda i: nt - 1 - i) if reverse else (lambda i: i)
    small = [lp['conv_w'][d], lp['conv_b'][d:d + 1], lp['wa_bd'][d], lp['b_lru_a'][d:d + 1], lp['wx_bd'][d],
             lp['b_lru_x'][d:d + 1], lp['lam'][d:d + 1]]
    return pl.pallas_call(
        functools.partial(_lru_body, reverse),
        grid=(nb, nt),
        in_specs=[
            pl.BlockSpec((1, tt, pw), lambda b, i: (b, tidx(i), 0)),
            pl.BlockSpec((1, 1, SUBLANES, D_LRU), lambda b, i: (b, tidx(i), 0, 0)),
        ] + [_full(a.shape) for a in small] + [pl.BlockSpec((1, 1, D_LRU), lambda b, i: (b, 0, 0))],
        out_specs=[
            pl.BlockSpec((1, tt, D_LRU), lambda b, i: (b, tidx(i), 0)),
            pl.BlockSpec((1, 1, D_LRU), lambda b, i: (b, 0, 0)),
        ],
        out_shape=[jax.ShapeDtypeStruct((nb, t, D_LRU), F32), jax.ShapeDtypeStruct((nb, 1, D_LRU), F32)],
        scratch_shapes=[pltpu.VMEM((tt, D_LRU), F32), pltpu.VMEM((tt, D_LRU), F32), pltpu.VMEM((SUBLANES, D_LRU), F32)],
        compiler_params=_cparams("parallel", "arbitrary"),
        name="lru_bwd" if reverse else "lru_fwd",
    )(p_lru, halo, *small, h0)


def _gelu_tanh(x):
    return 0.5 * x * (1.0 + jnp.tanh(np.sqrt(2.0 / np.pi).astype(np.float32) * (x + 0.044715 * (x * x * x))))


def _out_proj_body(x_ref, o_ref, y0_ref, y1_ref, bonus_ref, g_ref, h0_ref, h1_ref, p_ref,
                   gmla_ref, lnw_ref, lnb_ref, glru_ref, wo_ref, gpost_ref, gate_ref, gpre_ref, sc_ref, sh_ref,
                   x1_out, h2_out):
    o_mla = _rms(o_ref[0], gmla_ref[...])
    ones = _head_ones(D_RWKV, N_RWKV)
    y = y0_ref[0, 0] + y1_ref[0, 0]
    mu = _dot_x(y, ones) * (1.0 / N_RWKV)
    yc = y - mu
    var = _dot_x(yc * yc, ones) * (1.0 / N_RWKV)
    y = yc * lax.rsqrt(var + GN_EPS) * lnw_ref[...] + lnb_ref[...]
    o_rwkv = (y + bonus_ref[0]) * g_ref[0]
    o_lru = _rms((h0_ref[0] + h1_ref[0]) * _gelu_tanh(p_ref[0, :, D_LRU:]), glru_ref[...])
    m = (_dg(o_mla.astype(BF16), wo_ref[:D_MLA_OUT, :], _NN)
         + _dg(o_rwkv.astype(BF16), wo_ref[D_MLA_OUT:D_MLA_OUT + D_RWKV, :], _NN)
         + _dg(o_lru.astype(BF16), wo_ref[D_MLA_OUT + D_RWKV:, :], _NN))
    x1 = x_ref[0] + gate_ref[0] * _rms(m, gpost_ref[...])
    x1_out[0] = x1
    h2_out[0] = _rms(x1, gpre_ref[...]) * (1.0 + sc_ref[0]) + sh_ref[0]


def _out_proj(x, o, y, bonus, g, h0, h1, p_lru, lp, mod, tt):
    nb, t, d = x.shape
    per_batch = mod['g1'].shape[0] > 1
    tok = lambda n: pl.BlockSpec((1, tt, n), lambda b, i: (b, i, 0))
    ydir = lambda dd: pl.BlockSpec((1, 1, tt, D_RWKV), lambda b, i: (dd, b, i, 0))
    small = [lp['g_mla_out'], lp['ln_x_w'], lp['ln_x_b'], lp['g_lru_out'], lp['w_out'], lp['g_post_mix']]
    return pl.pallas_call(
        _out_proj_body,
        grid=(nb, t // tt),
        in_specs=[tok(d), tok(D_MLA_OUT), ydir(0), ydir(1), tok(D_RWKV), tok(D_RWKV), tok(D_LRU), tok(D_LRU),
                  tok(P_LRU)] + [_full(a.shape) for a in small]
                 + [_mod_spec(per_batch), _full(lp['g_pre_ffn'].shape), _mod_spec(per_batch), _mod_spec(per_batch)],
        out_specs=[tok(d), tok(d)],
        out_shape=[jax.ShapeDtypeStruct((nb, t, d), F32)] * 2,
        compiler_params=_cparams("parallel", "parallel"),
        name="out_proj",
    )(x, o, y, y, bonus, g, h0, h1, p_lru, *small, mod['g1'], lp['g_pre_ffn'], mod['sc2'], mod['sh2'])


def _first_index(mask, idx, sentinel):
    return jnp.min(jnp.where(mask, idx, sentinel), axis=0, keepdims=True)


def _router_body(h_ref, wr_ref, br_ref, e_out, pos_out, w_out, cnt_out, carry):
    i = pl.program_id(0)
    tn = h_ref.shape[0]

    @pl.when(i == 0)
    def _():
        carry[...] = jnp.zeros_like(carry)

    logits = lax.dot_general(wr_ref[...], h_ref[...], _NT, precision=lax.Precision.HIGHEST,
                             preferred_element_type=F32)
    scores = _sigmoid(logits)
    sel = scores + br_ref[...]

    i8 = lax.broadcasted_iota(I32, (GROUP_SIZE, tn), 0)
    blocks, gscore = [], []
    for g in range(N_GROUPS):
        blk = sel[g * GROUP_SIZE:(g + 1) * GROUP_SIZE, :]
        m1 = jnp.max(blk, axis=0, keepdims=True)
        f1 = _first_index(blk == m1, i8, GROUP_SIZE)
        m2 = jnp.max(jnp.where(i8 == f1, -jnp.inf, blk), axis=0, keepdims=True)
        blocks.append(blk)
        gscore.append(m1 + m2)
    masked = []
    for g in range(N_GROUPS):
        beaten = jnp.zeros((1, tn), I32)
        for o in range(N_GROUPS):
            if o == g:
                continue
            wins = (gscore[o] >= gscore[g]) if o < g else (gscore[o] > gscore[g])
            beaten = beaten + wins.astype(I32)
        masked.append(jnp.where(beaten < TOPK_GROUPS, blocks[g], -jnp.inf))
    cur = jnp.concatenate(masked, axis=0)

    ie = lax.broadcasted_iota(I32, (N_EXPERTS, tn), 0)
    firsts, raw = [], []
    chosen = jnp.zeros((N_EXPERTS, tn), F32)
    for _ in range(TOP_K):
        mx = jnp.max(cur, axis=0, keepdims=True)
        first = _first_index(cur == mx, ie, N_EXPERTS)
        hit = ie == first
        firsts.append(first)
        raw.append(jnp.sum(jnp.where(hit, scores, 0.0), axis=0, keepdims=True))
        cur = jnp.where(hit, -jnp.inf, cur)
        chosen = chosen + hit.astype(F32)
    total = raw[0]
    for k in range(1, TOP_K):
        total = total + raw[k]

    before = (lax.broadcasted_iota(I32, (tn, tn), 0) < lax.broadcasted_iota(I32, (tn, tn), 1)).astype(BF16)
    rank = _dg(chosen.astype(BF16), before, _NN) + carry[:, 0:1]
    e_out[...] = jnp.zeros_like(e_out)
    pos_out[...] = jnp.zeros_like(pos_out)
    w_out[...] = jnp.zeros_like(w_out)
    for k in range(TOP_K):
        e_out[k:k + 1, :] = firsts[k]
        pos_out[k:k + 1, :] = jnp.sum(jnp.where(ie == firsts[k], rank, 0.0), axis=0, keepdims=True).astype(I32)
        w_out[k:k + 1, :] = raw[k] / total * ROUTED_SCALE
    new_carry = carry[...] + jnp.sum(chosen, axis=1, keepdims=True)
    carry[...] = new_carry
    cnt_out[...] = new_carry


def _router(h2, wr_t, b_col):
    n, d = h2.shape
    tn = ROUTER_TILE
    row_spec = pl.BlockSpec((SLOT_ROWS, tn), lambda i: (0, i))
    cnt_spec = pl.BlockSpec((N_EXPERTS, 128), lambda i: (0, 0))
    return pl.pallas_call(
        _router_body,
        grid=(n // tn,),
        in_specs=[pl.BlockSpec((tn, d), lambda i: (i, 0)), _full(wr_t.shape), _full(b_col.shape)],
        out_specs=[row_spec, row_spec, row_spec, cnt_spec],
        out_shape=[jax.ShapeDtypeStruct((SLOT_ROWS, n), I32), jax.ShapeDtypeStruct((SLOT_ROWS, n), I32),
                   jax.ShapeDtypeStruct((SLOT_ROWS, n), F32), jax.ShapeDtypeStruct((N_EXPERTS, 128), F32)],
        scratch_shapes=[pltpu.VMEM((N_EXPERTS, 128), F32)],
        compiler_params=_cparams("arbitrary"),
        name="router",
    )(h2, wr_t, b_col)


def _row_copy(src, src_row, dst, dst_row, sem):
    return pltpu.make_async_copy(src.at[pl.ds(src_row, 1), :], dst.at[pl.ds(dst_row, 1), :], sem)


def _load_slots(slot_ref, slot_smem, sem):
    cp = pltpu.make_async_copy(slot_ref.at[0], slot_smem, sem)
    cp.start()
    cp.wait()


def _dispatch_body(slot_ref, h_hbm, xs_in, xs_out, slot_smem, sem_s, sem):
    del xs_in
    tn = slot_smem.shape[1]
    base = pl.program_id(0) * tn
    _load_slots(slot_ref, slot_smem, sem_s)

    def issue(j, carry):
        for k in range(TOP_K):
            _row_copy(h_hbm, base + j, xs_out, slot_smem[k, j], sem).start()
        return carry

    def drain(j, carry):
        for k in range(TOP_K):
            _row_copy(h_hbm, 0, xs_out, 0, sem).wait()
        return carry

    lax.fori_loop(0, tn, issue, 0)
    lax.fori_loop(0, tn, drain, 0)


def _dispatch(slots3, h2, xs_zero):
    ntiles, _, tn = slots3.shape
    return pl.pallas_call(
        _dispatch_body,
        grid=(ntiles,),
        in_specs=[pl.BlockSpec((1, SLOT_ROWS, tn), lambda i: (i, 0, 0)),
                  pl.BlockSpec(memory_space=pl.ANY), pl.BlockSpec(memory_space=pl.ANY)],
        out_specs=pl.BlockSpec(memory_space=pl.ANY),
        out_shape=jax.ShapeDtypeStruct(xs_zero.shape, xs_zero.dtype),
        scratch_shapes=[pltpu.SMEM((SLOT_ROWS, tn), I32), pltpu.SemaphoreType.DMA, pltpu.SemaphoreType.DMA],
        input_output_aliases={2: 0},
        compiler_params=_cparams("arbitrary"),
        name="dispatch",
    )(slots3, h2, xs_zero)


def _swiglu(x, wg, wu, wd):
    g = _dg(x, wg, _NN)
    u = _dg(x, wu, _NN)
    return _dg((g * _sigmoid(g) * u).astype(BF16), wd, _NN)


def _expert_body(be_ref, nu_ref, xs_ref, wg_ref, wu_ref, wd_ref, ys_ref):
    del be_ref
    b = pl.program_id(0)

    @pl.when(b < nu_ref[0])
    def _():
        ys_ref[...] = _swiglu(xs_ref[...].astype(BF16), wg_ref[0], wu_ref[0], wd_ref[0])

    @pl.when(b >= nu_ref[0])
    def _():
        ys_ref[...] = jnp.zeros_like(ys_ref)


def _experts(block_e, n_used, xs, wg, wu, wd):
    n_slots, d = xs.shape
    bm = MOE_BLOCK
    de = wg.shape[2]
    grid_spec = pltpu.PrefetchScalarGridSpec(
        num_scalar_prefetch=2,
        grid=(n_slots // bm,),
        in_specs=[
            pl.BlockSpec((bm, d), lambda b, be, nu: (b, 0)),
            pl.BlockSpec((1, d, de), lambda b, be, nu: (be[b], 0, 0)),
            pl.BlockSpec((1, d, de), lambda b, be, nu: (be[b], 0, 0)),
            pl.BlockSpec((1, de, d), lambda b, be, nu: (be[b], 0, 0)),
        ],
        out_specs=pl.BlockSpec((bm, d), lambda b, be, nu: (b, 0)),
    )
    return pl.pallas_call(
        _expert_body,
        grid_spec=grid_spec,
        out_shape=jax.ShapeDtypeStruct((n_slots, d), F32),
        compiler_params=_cparams("arbitrary"),
        name="experts",
    )(block_e, n_used, xs, wg, wu, wd)


def _combine_body(slot_ref, w_ref, x1_ref, h2_ref, gate_ref, ys_hbm, wsg_ref, wsu_ref, wsd_ref, gpost_ref,
                  out_ref, slot_smem, buf, sem_s, sem):
    tn = slot_smem.shape[1]
    _load_slots(slot_ref, slot_smem, sem_s)

    def issue(j, carry):
        for k in range(TOP_K):
            _row_copy(ys_hbm, slot_smem[k, j], buf.at[k], j, sem).start()
        return carry

    def drain(j, carry):
        for k in range(TOP_K):
            _row_copy(ys_hbm, 0, buf.at[k], 0, sem).wait()
        return carry

    lax.fori_loop(0, tn, issue, 0)
    acc = _swiglu(h2_ref[...].astype(BF16), wsg_ref[...], wsu_ref[...], wsd_ref[...])
    lax.fori_loop(0, tn, drain, 0)
    for k in range(TOP_K):
        acc = acc + buf[k] * w_ref[:, k:k + 1]
    out_ref[...] = x1_ref[...] + gate_ref[0] * _rms(acc, gpost_ref[...])


def _combine(slots3, w_tok, x1, h2, gate_rows, ys, wsg, wsu, wsd, gpost):
    ntiles, _, tn = slots3.shape
    n, d = x1.shape
    tok = pl.BlockSpec((tn, d), lambda i: (i, 0))
    return pl.pallas_call(
        _combine_body,
        grid=(ntiles,),
        in_specs=[pl.BlockSpec((1, SLOT_ROWS, tn), lambda i: (i, 0, 0)),
                  pl.BlockSpec((tn, SLOT_ROWS), lambda i: (i, 0)), tok, tok,
                  pl.BlockSpec((1, 1, d), lambda i: (i, 0, 0)),
                  pl.BlockSpec(memory_space=pl.ANY),
                  _full(wsg.shape), _full(wsu.shape), _full(wsd.shape), _full(gpost.shape)],
        out_specs=tok,
        out_shape=jax.ShapeDtypeStruct((n, d), F32),
        scratch_shapes=[pltpu.SMEM((SLOT_ROWS, tn), I32), pltpu.VMEM((TOP_K, tn, d), F32),
                        pltpu.SemaphoreType.DMA, pltpu.SemaphoreType.DMA],
        compiler_params=_cparams("arbitrary"),
        name="combine",
    )(slots3, w_tok, x1, h2, gate_rows, ys, wsg, wsu, wsd, gpost)


def _moe(x1, h2, gate_rows, lp):
    n, d = h2.shape
    bm = MOE_BLOCK
    tn = TOKEN_TILE
    e_idx, pos, w_rows, cnt = _router(h2, lp['wr_t'], lp['b_router_col'])
    counts = cnt[:, 0].astype(I32)
    blocks_per_e = (counts + bm - 1) // bm
    blk_end = jnp.cumsum(blocks_per_e)
    blk_start = blk_end - blocks_per_e
    n_blocks = -(-(n * TOP_K) // bm) + N_EXPERTS
    slots = blk_start[e_idx] * bm + pos
    block_e = jnp.minimum(jnp.searchsorted(blk_end, jnp.arange(n_blocks, dtype=I32), side='right'),
                          N_EXPERTS - 1).astype(I32)
    n_used = blk_end[-1:].astype(I32)
    slots3 = slots.reshape(SLOT_ROWS, n // tn, tn).transpose(1, 0, 2)
    xs = _dispatch(slots3, h2, jnp.zeros((n_blocks * bm, d), F32))
    ys = _experts(block_e, n_used, xs, lp['w_e_gate'], lp['w_e_up'], lp['w_e_down'])
    return _combine(slots3, w_rows.T, x1, h2, gate_rows, ys, lp['w_s_gate'], lp['w_s_up'], lp['w_s_down'],
                    lp['g_post_ffn'])


def _rope_tables(n_tokens):
    rows = n_tokens // GRID_W
    row = jnp.repeat(jnp.arange(rows, dtype=F32), GRID_W)
    col = jnp.tile(jnp.arange(GRID_W, dtype=F32), rows)
    n_freq = ROPE_DIM // 4
    inv_freq = ROPE_THETA ** (-jnp.arange(n_freq, dtype=F32) / n_freq)
    ang_r = row[:, None] * inv_freq
    ang_c = col[:, None] * inv_freq
    ang = jnp.concatenate([ang_r, ang_r, ang_c, ang_c], axis=-1)
    cos, sin = jnp.cos(ang), jnp.sin(ang)
    pad = lambda z, fill: jnp.concatenate(
        [jnp.full((n_tokens, NOPE_DIM), fill, F32), z, jnp.zeros((n_tokens, HEAD_PAD - QK_DIM), F32)], axis=1)
    return pad(cos, 1.0), pad(sin, 0.0), pad(cos, 0.0), pad(sin, 0.0)


def _rot_cols(w):
    q = ROPE_DIM // 4
    perm = np.concatenate([np.arange(q, 2 * q), np.arange(0, q), np.arange(3 * q, 4 * q), np.arange(2 * q, 3 * q)])
    sign = np.concatenate([-np.ones(q), np.ones(q), -np.ones(q), np.ones(q)]).astype(np.float32)
    return w[..., perm] * sign


def _block_diag(blocks):
    h, m, n = blocks.shape[-3:]
    eye = jnp.eye(h, dtype=blocks.dtype)
    out = blocks[..., :, :, None, :] * eye[:, None, :, None]
    return out.reshape(blocks.shape[:-3] + (h * m, h * n))


def _diag_blocks(mat, h):
    m, n = mat.shape[-2] // h, mat.shape[-1] // h
    z = mat.reshape(mat.shape[:-2] + (h, m, h, n))
    return jnp.stack([z[..., i, :, i, :] for i in range(h)], axis=-3)


def _layer_operands(l, a):
    w_in = a['w_in'][l]
    d = w_in.shape[0]
    p_mla = Q_LORA + KV_LORA + ROPE_DIM
    w_kr = w_in[:, Q_LORA + KV_LORA:p_mla]
    z32 = jnp.zeros((d, ROPE_DIM), F32)
    w_mla = jnp.concatenate([w_in[:, :Q_LORA + KV_LORA], z32, _rot_cols(w_kr), w_kr, z32], axis=1)
    wq3 = a['w_q_b'][l].reshape(Q_LORA, H_MLA, QK_DIM)
    zq = lambda n: jnp.zeros((Q_LORA, H_MLA, n), F32)
    wq = jnp.concatenate([wq3, zq(HEAD_PAD - QK_DIM)], axis=2).reshape(Q_LORA, H_MLA * HEAD_PAD)
    wqr = jnp.concatenate([zq(NOPE_DIM), _rot_cols(wq3[:, :, NOPE_DIM:]), zq(HEAD_PAD - QK_DIM)],
                          axis=2).reshape(Q_LORA, H_MLA * HEAD_PAD)
    wkv3 = a['w_kv_b'][l].reshape(KV_LORA, H_MLA, NOPE_DIM + V_DIM)
    wk = jnp.concatenate([wkv3[:, :, :NOPE_DIM], jnp.zeros((KV_LORA, H_MLA, HEAD_PAD - NOPE_DIM), F32)],
                         axis=2).reshape(KV_LORA, H_MLA * HEAD_PAD)
    wv = wkv3[:, :, NOPE_DIM:].reshape(KV_LORA, H_MLA * V_DIM)
    row = lambda z: z.reshape(1, -1)
    two = lambda z: jnp.concatenate([z[0], z[1]], axis=-1)
    bd2 = lambda z: jnp.concatenate([jnp.concatenate([z[0], jnp.zeros_like(z[0])], axis=1),
                                     jnp.concatenate([jnp.zeros_like(z[1]), z[1]], axis=1)], axis=0)
    return {
        'g_pre_mix': row(a['g_pre_mix'][l]), 'g_post_mix': row(a['g_post_mix'][l]),
        'g_pre_ffn': row(a['g_pre_ffn'][l]), 'g_post_ffn': row(a['g_post_ffn'][l]),
        'w_mla': w_mla.astype(BF16),
        'w_rwkv': w_in[:, p_mla:p_mla + P_RWKV].astype(BF16),
        'w_lru': w_in[:, p_mla + P_RWKV:].astype(BF16),
        'g_q_a': row(a['g_q_a'][l]), 'g_kv_a': row(a['g_kv_a'][l]),
        'wq': wq.astype(BF16), 'wqr': wqr.astype(BF16), 'wk': wk.astype(BF16), 'wv': wv.astype(BF16),
        'g_mla_out': row(a['g_mla_out'][l]),
        'mu_rwkv': row(a['mu_rwkv'][l]), 'k_k': row(a['k_k'][l]), 'k_a': row(a['k_a'][l]), 'r_k': row(a['r_k'][l]),
        'w0_cat': row(two(a['w0_rwkv'][l])), 'w2_bd': bd2(a['w2_rwkv'][l]),
        'a0_cat': row(two(a['a0_rwkv'][l])), 'a2_bd': bd2(a['a2_rwkv'][l]),
        'g2_rwkv': a['g2_rwkv'][l], 'ln_x_w': row(a['ln_x_w'][l]), 'ln_x_b': row(a['ln_x_b'][l]),
        'conv_w': a['conv_w'][l], 'conv_b': a['conv_b'][l],
        'wa_bd': _block_diag(a['w_lru_a'][l]), 'b_lru_a': a['b_lru_a'][l],
        'wx_bd': _block_diag(a['w_lru_x'][l]), 'b_lru_x': a['b_lru_x'][l],
        'lam': a['lam'][l], 'g_lru_out': row(a['g_lru_out'][l]),
        'w_out': a['w_out'][l].astype(BF16),
        'wr_t': a['w_router'][l].T, 'b_router_col': a['b_router'][l].reshape(N_EXPERTS, 1),
        'w_e_gate': a['w_e_gate'][l].astype(BF16), 'w_e_up': a['w_e_up'][l].astype(BF16),
        'w_e_down': a['w_e_down'][l].astype(BF16),
        'w_s_gate': a['w_s_gate'][l].astype(BF16), 'w_s_up': a['w_s_up'][l].astype(BF16),
        'w_s_down': a['w_s_down'][l].astype(BF16),
    }


def _mixer(x, mod, lp, ctx, tables):
    nb, t, _ = x.shape
    tt = min(t, 512)
    p_mla, p_rwkv, p_lru = _in_proj(x, lp['g_pre_mix'], mod['sc1'], mod['sh1'], lp['w_mla'], lp['w_rwkv'],
                                    lp['w_lru'], tt)
    if ctx is None:
        q, k, v, new_cache = _mla_prep(p_mla, lp['g_q_a'], lp['g_kv_a'], lp['wq'], None, lp['wk'], lp['wv'], None, tt)
        kc = vc = None
        s0 = jnp.zeros((nb, 2, D_RWKV, D_RWKV), F32)
        h0 = jnp.zeros((nb, 2, 1, D_LRU), F32)
    else:
        cache, s_rwkv, s_lru = ctx
        q, k, v = _mla_prep(p_mla, lp['g_q_a'], lp['g_kv_a'], lp['wq'], lp['wqr'], lp['wk'], lp['wv'], tables, tt)
        new_cache = None
        kr_placed = jnp.pad(cache[..., KV_LORA:], ((0, 0), (0, 0), (NOPE_DIM, HEAD_PAD - QK_DIM)))
        kc, vc = _ctx_kv(cache[..., :KV_LORA], kr_placed, lp['wk'], lp['wv'])
        s0 = _block_diag(s_rwkv)
        h0 = s_lru[:, :, None, :]
    o = _attention(q, k, v, kc, vc, min(t, 256))

    r, vv, kk, lw, ag, kd, g, bonus = _rwkv_prep(p_rwkv, lp, tt)
    y, s_fin = _rwkv_scan(r, vv, kk, lw, ag, kd, s0)

    hp, hn = _halo_rows(p_lru[..., :D_LRU], tt, SUBLANES)
    h_f, fin_f = _lru_dir(p_lru, hp, lp, 0, h0[:, 0], tt)
    h_b, fin_b = _lru_dir(p_lru, hn, lp, 1, h0[:, 1], tt)

    x1, h2 = _out_proj(x, o, y, bonus, g, h_f, h_b, p_lru, lp, mod, tt)
    states = None
    if ctx is None:
        states = (new_cache, _diag_blocks(s_fin, H_RWKV), jnp.concatenate([fin_f, fin_b], axis=1))
    return x1, h2, states


def kernel(x_prompt, x_sample, c, cache_mla, state_rwkv, state_lru, c_ctx, w_ada, b_ada, g_pre_mix, g_post_mix, g_pre_ffn, g_post_ffn, w_in, g_q_a, w_q_b, g_kv_a, w_kv_b, g_mla_out, mu_rwkv, w0_rwkv, w2_rwkv, a0_rwkv, a2_rwkv, g2_rwkv, k_k, k_a, r_k, ln_x_w, ln_x_b, conv_w, conv_b, w_lru_a, b_lru_a, w_lru_x, b_lru_x, lam, g_lru_out, w_out, w_router, b_router, w_e_gate, w_e_up, w_e_down, w_s_gate, w_s_up, w_s_down):
    a = dict(w_in=w_in, g_pre_mix=g_pre_mix, g_post_mix=g_post_mix, g_pre_ffn=g_pre_ffn, g_post_ffn=g_post_ffn,
             g_q_a=g_q_a, w_q_b=w_q_b, g_kv_a=g_kv_a, w_kv_b=w_kv_b, g_mla_out=g_mla_out, mu_rwkv=mu_rwkv,
             w0_rwkv=w0_rwkv, w2_rwkv=w2_rwkv, a0_rwkv=a0_rwkv, a2_rwkv=a2_rwkv, g2_rwkv=g2_rwkv, k_k=k_k, k_a=k_a,
             r_k=r_k.reshape(r_k.shape[0], -1), ln_x_w=ln_x_w, ln_x_b=ln_x_b, conv_w=conv_w, conv_b=conv_b,
             w_lru_a=w_lru_a, b_lru_a=b_lru_a, w_lru_x=w_lru_x, b_lru_x=b_lru_x, lam=lam, g_lru_out=g_lru_out,
             w_out=w_out, w_router=w_router, b_router=b_router, w_e_gate=w_e_gate, w_e_up=w_e_up,
             w_e_down=w_e_down, w_s_gate=w_s_gate, w_s_up=w_s_up, w_s_down=w_s_down)
    n_layers = w_in.shape[0]
    nbp, tp, d = x_prompt.shape
    nbs, ts, _ = x_sample.shape
    rows = -(-(1 + nbs) // SUBLANES) * SUBLANES
    cvecs = jnp.concatenate([c_ctx[None, :], c, jnp.zeros((rows - 1 - nbs, d), F32)], axis=0)
    mods = _ada(cvecs, w_ada, b_ada)
    names = ('sh1', 'sc1', 'g1', 'sh2', 'sc2', 'g2')
    tables = _rope_tables(ts)
    tn = TOKEN_TILE

    yp, ys = x_prompt, x_sample
    caches, rwkv_states, lru_states = [], [], []
    for l in range(n_layers):
        lp = _layer_operands(l, a)
        mod_p = {nm: mods[l, 0:1, i * d:(i + 1) * d][:, None, :] for i, nm in enumerate(names)}
        mod_s = {nm: mods[l, 1:1 + nbs, i * d:(i + 1) * d][:, None, :] for i, nm in enumerate(names)}
        x1p, h2p, states = _mixer(yp, mod_p, lp, None, None)
        x1s, h2s, _ = _mixer(ys, mod_s, lp, (cache_mla[:, l], state_rwkv[:, l], state_lru[:, l]), tables)
        caches.append(states[0])
        rwkv_states.append(states[1])
        lru_states.append(states[2])
        x1 = jnp.concatenate([x1p.reshape(-1, d), x1s.reshape(-1, d)], axis=0)
        h2 = jnp.concatenate([h2p.reshape(-1, d), h2s.reshape(-1, d)], axis=0)
        gate_rows = jnp.concatenate([jnp.repeat(mod_p['g2'], nbp * tp // tn, axis=0),
                                     jnp.repeat(mod_s['g2'], ts // tn, axis=0)], axis=0)
        x2 = _moe(x1, h2, gate_rows, lp)
        yp = x2[:nbp * tp].reshape(nbp, tp, d)
        ys = x2[nbp * tp:].reshape(nbs, ts, d)
    return (yp, ys, jnp.stack(caches, axis=1), jnp.stack(rwkv_states, axis=1), jnp.stack(lru_states, axis=1))
```

```python
import functools

import numpy as np
import jax
import jax.numpy as jnp
from jax import lax
from jax.experimental import pallas as pl
from jax.experimental.pallas import tpu as pltpu

F32 = jnp.float32
BF16 = jnp.bfloat16
I32 = jnp.int32

D_MODEL = 1024
GRID_W = 64
EPS = 1e-6
H_MLA = 4
Q_LORA = 256
KV_LORA = 128
NOPE_DIM = 64
ROPE_DIM = 32
V_DIM = 128
QK_DIM = NOPE_DIM + ROPE_DIM
ROPE_THETA = 10000.0
HEAD_PAD = 128
H_RWKV = 4
N_RWKV = 64
D_RWKV = H_RWKV * N_RWKV
W_LORA = 64
A_LORA = 64
G_LORA = 128
GN_EPS = 64e-5
D_LRU = 256
H_LRU = 4
BS_LRU = D_LRU // H_LRU
CONV_W = 4
LRU_C = 8.0
D_MLA_OUT = H_MLA * V_DIM
P_MLA_PAD = Q_LORA + KV_LORA + HEAD_PAD
P_RWKV = 3 * D_RWKV + 2 * W_LORA + 2 * A_LORA + G_LORA
P_LRU = 2 * D_LRU
N_EXPERTS = 64
TOP_K = 6
N_GROUPS = 8
TOPK_GROUPS = 4
GROUP_SIZE = N_EXPERTS // N_GROUPS
D_EXPERT = 256
ROUTED_SCALE = 2.5

SUBLANES = 8
RWKV_CHUNK = 64
MOE_BLOCK = 256
ROUTER_TILE = 512
TOKEN_TILE = 256
SLOT_ROWS = 8
VMEM_LIMIT = 56 * 1024 * 1024


def _cparams(*sem):
    return pltpu.CompilerParams(dimension_semantics=sem, vmem_limit_bytes=VMEM_LIMIT)


def _sigmoid(x):
    return 1.0 / (1.0 + jnp.exp(-x))


def _softplus(x):
    return jnp.maximum(x, 0.0) + jnp.log1p(jnp.exp(-jnp.abs(x)))


def _expm1(z):
    e = jnp.exp(z)
    direct = (e == 1.0) | (z < -1.0)
    corrected = (e - 1.0) * z / jnp.log(jnp.where(direct, 2.0, e))
    return jnp.where(e == 1.0, z, jnp.where(z < -1.0, e - 1.0, corrected))


def _rms(x, g):
    return x * lax.rsqrt(jnp.mean(x * x, axis=-1, keepdims=True) + EPS) * g


def _split2(x):
    hi = x.astype(BF16)
    lo = (x - hi.astype(F32)).astype(BF16)
    return hi, lo


def _split3(x):
    hi = x.astype(BF16)
    r1 = x - hi.astype(F32)
    mid = r1.astype(BF16)
    lo = (r1 - mid.astype(F32)).astype(BF16)
    return hi, mid, lo


_NN = (((1,), (0,)), ((), ()))
_NT = (((1,), (1,)), ((), ()))


def _dg(a, b, dims):
    return lax.dot_general(a, b, dims, preferred_element_type=F32)


def _dot1(a, b, dims=_NN):
    return _dg(a.astype(BF16), b.astype(BF16), dims)


def _dot3(a, b, dims=_NN):
    ah, al = _split2(a)
    bh, bl = _split2(b)
    return _dg(ah, bh, dims) + (_dg(ah, bl, dims) + _dg(al, bh, dims))


def _dot_x(a, b_exact, dims=_NN):
    h, m, l = _split3(a)
    bb = b_exact.astype(BF16)
    return _dg(h, bb, dims) + (_dg(m, bb, dims) + _dg(l, bb, dims))


def _x_dot(a_exact, b, dims=_NN):
    h, m, l = _split3(b)
    aa = a_exact.astype(BF16)
    return _dg(aa, h, dims) + (_dg(aa, m, dims) + _dg(aa, l, dims))


def _head_ones(n, seg):
    r = lax.broadcasted_iota(I32, (n, n), 0) // seg
    c = lax.broadcasted_iota(I32, (n, n), 1) // seg
    return (r == c).astype(F32)


def _ada_body(c_ref, w_ref, b_ref, o_ref):
    cv = c_ref[...]
    s = cv * _sigmoid(cv)
    o_ref[0] = _dot3(s, w_ref[0]) + b_ref[0]


def _ada(cvecs, w_ada, b_ada):
    n_layers, d, d6 = w_ada.shape
    rows = cvecs.shape[0]
    nt = d6 // d
    return pl.pallas_call(
        _ada_body,
        grid=(n_layers, nt),
        in_specs=[
            pl.BlockSpec((rows, d), lambda l, j: (0, 0)),
            pl.BlockSpec((1, d, d), lambda l, j: (l, 0, j)),
            pl.BlockSpec((1, 1, d), lambda l, j: (l, 0, j)),
        ],
        out_specs=pl.BlockSpec((1, rows, d), lambda l, j: (l, 0, j)),
        out_shape=jax.ShapeDtypeStruct((n_layers, rows, d6), F32),
        compiler_params=_cparams("parallel", "parallel"),
        name="ada",
    )(cvecs, w_ada, b_ada.reshape(n_layers, 1, d6))


def _mod_spec(per_batch):
    if per_batch:
        return pl.BlockSpec((1, 1, D_MODEL), lambda b, i: (b, 0, 0))
    return pl.BlockSpec((1, 1, D_MODEL), lambda b, i: (0, 0, 0))


def _full(shape):
    nd = len(shape)
    return pl.BlockSpec(shape, lambda *_: (0,) * nd)


def _in_proj_body(x_ref, g_ref, sc_ref, sh_ref, w1_ref, w2_ref, w3_ref, o1_ref, o2_ref, o3_ref):
    h = _rms(x_ref[0], g_ref[...]) * (1.0 + sc_ref[0]) + sh_ref[0]
    hb = h.astype(BF16)
    o1_ref[0] = _dg(hb, w1_ref[...], _NN)
    o2_ref[0] = _dg(hb, w2_ref[...], _NN)
    o3_ref[0] = _dg(hb, w3_ref[...], _NN)


def _in_proj(x, g, sc, sh, w_mla, w_rwkv, w_lru, tt):
    nb, t, d = x.shape
    per_batch = sc.shape[0] > 1
    outs = [w_mla.shape[1], w_rwkv.shape[1], w_lru.shape[1]]
    return pl.pallas_call(
        _in_proj_body,
        grid=(nb, t // tt),
        in_specs=[
            pl.BlockSpec((1, tt, d), lambda b, i: (b, i, 0)),
            _full((1, d)),
            _mod_spec(per_batch),
            _mod_spec(per_batch),
            _full(w_mla.shape),
            _full(w_rwkv.shape),
            _full(w_lru.shape),
        ],
        out_specs=[pl.BlockSpec((1, tt, n), lambda b, i: (b, i, 0)) for n in outs],
        out_shape=[jax.ShapeDtypeStruct((nb, t, n), F32) for n in outs],
        compiler_params=_cparams("parallel", "parallel"),
        name="in_proj",
    )(x, g, sc, sh, w_mla, w_rwkv, w_lru)


def _rope_lanes():
    lane = lax.broadcasted_iota(I32, (1, HEAD_PAD), 1)
    return ((lane >= NOPE_DIM) & (lane < QK_DIM)).astype(F32)


def _mla_prep_body(rope, *refs):
    if rope:
        (p_ref, gq_ref, gkv_ref, wq_ref, wqr_ref, wk_ref, wv_ref, cq_ref, sq_ref, cr_ref, sr_ref,
         q_ref, k_ref, v_ref) = refs
    else:
        p_ref, gq_ref, gkv_ref, wq_ref, wk_ref, wv_ref, q_ref, k_ref, v_ref, c_ref = refs
    p = p_ref[0]
    cq = _rms(p[:, :Q_LORA], gq_ref[...])
    ckv = _rms(p[:, Q_LORA:Q_LORA + KV_LORA], gkv_ref[...])
    kr = p[:, Q_LORA + KV_LORA:]
    cqb = cq.astype(BF16)
    ckvb = ckv.astype(BF16)
    scale = QK_DIM ** -0.5
    qa = _dg(cqb, wq_ref[...], _NN) * scale
    ka = _dg(ckvb, wk_ref[...], _NN)
    va = _dg(ckvb, wv_ref[...], _NN)
    if rope:
        qr = _dg(cqb, wqr_ref[...], _NN) * scale
        krp = kr * cr_ref[...] + pltpu.roll(kr, ROPE_DIM, 1) * sr_ref[...]
    else:
        krp = kr * _rope_lanes()
    for h in range(H_MLA):
        sl = slice(h * HEAD_PAD, (h + 1) * HEAD_PAD)
        qh = qa[:, sl]
        if rope:
            qh = qh * cq_ref[...] + qr[:, sl] * sq_ref[...]
        q_ref[0, h] = qh.astype(BF16)
        k_ref[0, h] = (ka[:, sl] + krp).astype(BF16)
        v_ref[0, h] = va[:, sl].astype(BF16)
    if not rope:
        c_ref[0, :, :KV_LORA] = ckv
        c_ref[0, :, KV_LORA:] = pltpu.roll(kr, HEAD_PAD - NOPE_DIM, 1)[:, :ROPE_DIM]


def _mla_prep(p_mla, gq, gkv, wq, wqr, wk, wv, tables, tt):
    nb, t, pw = p_mla.shape
    rope = tables is not None
    hw = H_MLA * HEAD_PAD
    head_spec = pl.BlockSpec((1, H_MLA, tt, HEAD_PAD), lambda b, i: (b, 0, i, 0))
    head_shape = jax.ShapeDtypeStruct((nb, H_MLA, t, HEAD_PAD), BF16)
    in_specs = [pl.BlockSpec((1, tt, pw), lambda b, i: (b, i, 0)), _full(gq.shape), _full(gkv.shape), _full(wq.shape)]
    args = [p_mla, gq, gkv, wq]
    if rope:
        in_specs.append(_full(wqr.shape))
        args.append(wqr)
    in_specs += [_full(wk.shape), _full(wv.shape)]
    args += [wk, wv]
    out_specs = [head_spec, head_spec, head_spec]
    out_shape = [head_shape, head_shape, head_shape]
    if rope:
        in_specs += [pl.BlockSpec((tt, HEAD_PAD), lambda b, i: (i, 0))] * 4
        args += list(tables)
    else:
        out_specs.append(pl.BlockSpec((1, tt, KV_LORA + ROPE_DIM), lambda b, i: (b, i, 0)))
        out_shape.append(jax.ShapeDtypeStruct((nb, t, KV_LORA + ROPE_DIM), F32))
    del hw
    return pl.pallas_call(
        functools.partial(_mla_prep_body, rope),
        grid=(nb, t // tt),
        in_specs=in_specs,
        out_specs=out_specs,
        out_shape=out_shape,
        compiler_params=_cparams("parallel", "parallel"),
        name="mla_prep_rope" if rope else "mla_prep",
    )(*args)


def _ctx_kv_body(lat_ref, kr_ref, wk_ref, wv_ref, k_ref, v_ref):
    latb = lat_ref[0].astype(BF16)
    ka = _dg(latb, wk_ref[...], _NN)
    va = _dg(latb, wv_ref[...], _NN)
    kr = kr_ref[0]
    for h in range(H_MLA):
        sl = slice(h * HEAD_PAD, (h + 1) * HEAD_PAD)
        k_ref[0, h] = (ka[:, sl] + kr).astype(BF16)
        v_ref[0, h] = va[:, sl].astype(BF16)


def _ctx_kv(lat, kr_placed, wk, wv):
    nb, s, _ = lat.shape
    head_spec = pl.BlockSpec((1, H_MLA, s, HEAD_PAD), lambda b: (b, 0, 0, 0))
    head_shape = jax.ShapeDtypeStruct((nb, H_MLA, s, HEAD_PAD), BF16)
    return pl.pallas_call(
        _ctx_kv_body,
        grid=(nb,),
        in_specs=[
            pl.BlockSpec((1, s, KV_LORA), lambda b: (b, 0, 0)),
            pl.BlockSpec((1, s, HEAD_PAD), lambda b: (b, 0, 0)),
            _full(wk.shape),
            _full(wv.shape),
        ],
        out_specs=[head_spec, head_spec],
        out_shape=[head_shape, head_shape],
        compiler_params=_cparams("parallel"),
        name="ctx_kv",
    )(lat, kr_placed, wk, wv)


def _attn_body(has_ctx, *refs):
    if has_ctx:
        q_ref, k_ref, v_ref, kc_ref, vc_ref, o_ref = refs
    else:
        q_ref, k_ref, v_ref, o_ref = refs
    q = q_ref[0, 0]
    s = _dg(q, k_ref[0, 0], _NT)
    m = jnp.max(s, axis=-1, keepdims=True)
    if has_ctx:
        sc = _dg(q, kc_ref[0, 0], _NT)
        m = jnp.maximum(m, jnp.max(sc, axis=-1, keepdims=True))
    p = jnp.exp(s - m)
    l = jnp.sum(p, axis=-1, keepdims=True)
    o = _dg(p.astype(BF16), v_ref[0, 0], _NN)
    if has_ctx:
        pc = jnp.exp(sc - m)
        l = l + jnp.sum(pc, axis=-1, keepdims=True)
        o = o + _dg(pc.astype(BF16), vc_ref[0, 0], _NN)
    o_ref[0] = o / l


def _attention(q, k, v, kc, vc, tq):
    nb, nh, t, hd = q.shape
    s = k.shape[2]
    has_ctx = kc is not None
    in_specs = [
        pl.BlockSpec((1, 1, tq, hd), lambda b, h, i: (b, h, i, 0)),
        pl.BlockSpec((1, 1, s, hd), lambda b, h, i: (b, h, 0, 0)),
        pl.BlockSpec((1, 1, s, V_DIM), lambda b, h, i: (b, h, 0, 0)),
    ]
    args = [q, k, v]
    if has_ctx:
        sc = kc.shape[2]
        in_specs += [
            pl.BlockSpec((1, 1, sc, hd), lambda b, h, i: (b, h, 0, 0)),
            pl.BlockSpec((1, 1, sc, V_DIM), lambda b, h, i: (b, h, 0, 0)),
        ]
        args += [kc, vc]
    return pl.pallas_call(
        functools.partial(_attn_body, has_ctx),
        grid=(nb, nh, t // tq),
        in_specs=in_specs,
        out_specs=pl.BlockSpec((1, tq, V_DIM), lambda b, h, i: (b, i, h)),
        out_shape=jax.ShapeDtypeStruct((nb, t, nh * V_DIM), F32),
        compiler_params=_cparams("parallel", "parallel", "parallel"),
        name="attention",
    )(*args)


def _rwkv_prep_body(p_ref, hp_ref, hn_ref, mu_ref, kk_ref, ka_ref, rk_ref, w0_ref, w2_ref, a0_ref, a2_ref, g2_ref,
                    r_out, v_out, kk_out, lw_out, a_out, kd_out, g_out, bonus_out):
    p = p_ref[0]
    tt = p.shape[0]
    row = lax.broadcasted_iota(I32, (tt, 1), 0)
    prev = jnp.where(row == 0, hp_ref[0, 0], pltpu.roll(p, 1, 0))
    nxt = jnp.where(row == tt - 1, hn_ref[0, 0], pltpu.roll(p, tt - 1, 0))
    ps = p + (0.5 * (prev + nxt) - p) * mu_ref[...]
    r = ps[:, :D_RWKV]
    k = ps[:, D_RWKV:2 * D_RWKV]
    v = ps[:, 2 * D_RWKV:3 * D_RWKV]
    wl = ps[:, 3 * D_RWKV:3 * D_RWKV + 2 * W_LORA]
    al = ps[:, 3 * D_RWKV + 2 * W_LORA:3 * D_RWKV + 2 * W_LORA + 2 * A_LORA]
    gl = ps[:, 3 * D_RWKV + 2 * W_LORA + 2 * A_LORA:]
    ones = _head_ones(D_RWKV, N_RWKV)
    kk = k * kk_ref[...]
    kk = kk / jnp.maximum(jnp.sqrt(_dot_x(kk * kk, ones)), 1e-12)
    wlin = w0_ref[...] + _dot3(jnp.tanh(wl), w2_ref[...])
    lw = -jnp.exp(-_softplus(-wlin) - 0.5)
    ag = _sigmoid(a0_ref[...] + _dot3(al, a2_ref[...]))
    bonus = jnp.zeros_like(v)
    for d in range(2):
        sl = slice(d * D_RWKV, (d + 1) * D_RWKV)
        kd = k * (1.0 + (ag[:, sl] - 1.0) * ka_ref[...])
        kd_out[0, :, sl] = kd
        bonus = bonus + _dot_x(r * kd * rk_ref[...], ones) * v
    r_out[0] = r
    v_out[0] = v
    kk_out[0] = kk
    lw_out[0] = lw
    a_out[0] = ag
    g_out[0] = _dot1(_sigmoid(gl), g2_ref[...])
    bonus_out[0] = bonus


def _halo_rows(x, tt, width):
    nb, t, c = x.shape
    nt = t // tt
    xt = x.reshape(nb, nt, tt, c)
    zero = jnp.zeros((nb, 1, width, c), x.dtype)
    prev = jnp.concatenate([zero, xt[:, :-1, tt - width:]], axis=1)
    nxt = jnp.concatenate([xt[:, 1:, :width], zero], axis=1)
    return prev, nxt


def _rwkv_prep(p_rwkv, lp, tt):
    nb, t, pw = p_rwkv.shape
    hp, hn = _halo_rows(p_rwkv, tt, 1)
    tok = lambda n: pl.BlockSpec((1, tt, n), lambda b, i: (b, i, 0))
    halo = pl.BlockSpec((1, 1, 1, pw), lambda b, i: (b, i, 0, 0))
    small = [lp['mu_rwkv'], lp['k_k'], lp['k_a'], lp['r_k'], lp['w0_cat'], lp['w2_bd'], lp['a0_cat'], lp['a2_bd'],
             lp['g2_rwkv']]
    widths = [D_RWKV, D_RWKV, D_RWKV, 2 * D_RWKV, 2 * D_RWKV, 2 * D_RWKV, D_RWKV, D_RWKV]
    return pl.pallas_call(
        _rwkv_prep_body,
        grid=(nb, t // tt),
        in_specs=[tok(pw), halo, halo] + [_full(a.shape) for a in small],
        out_specs=[tok(n) for n in widths],
        out_shape=[jax.ShapeDtypeStruct((nb, t, n), F32) for n in widths],
        compiler_params=_cparams("parallel", "parallel"),
        name="rwkv_prep",
    )(p_rwkv, hp, hn, *small)


def _rwkv_scan_body(r_ref, v_ref, kk_ref, lw_ref, a_ref, kd_ref, s0_ref, y_ref, sf_ref, s_scr):
    d = pl.program_id(1)
    c = pl.program_id(2)
    nc = pl.num_programs(2)
    cl = r_ref.shape[1]
    hw = D_RWKV

    @pl.when(c == 0)
    def _():
        s_scr[...] = s0_ref[0, 0]

    r = r_ref[0]
    v = v_ref[0]
    kk = kk_ref[0]
    lw = lw_ref[0]
    ag = a_ref[0]
    kd = kd_ref[0]
    sgn = jnp.where(d == 0, 1, -1)

    ti = lax.broadcasted_iota(I32, (cl, cl), 0)
    si = lax.broadcasted_iota(I32, (cl, cl), 1)
    tri = ((ti - si) * sgn >= 0).astype(F32)
    cum = _x_dot(tri, lw)
    tot = jnp.sum(lw, axis=0, keepdims=True)
    e_neg = jnp.exp(-cum)
    e_rem = jnp.exp(tot - cum)
    at = -kk * jnp.exp(cum - lw)
    rt = r * jnp.exp(cum)
    b0 = kk * ag
    bt = b0 * e_neg
    kt = kd * e_neg
    bh = b0 * e_rem
    kh = kd * e_rem
    w_c = jnp.exp(tot)

    lane_head = lax.broadcasted_iota(I32, (1, hw), 1) // N_RWKV
    hms = [(lane_head == h).astype(F32) for h in range(H_RWKV)]
    stack = lambda z: jnp.concatenate([z * hm for hm in hms], axis=0)
    bks = jnp.concatenate([stack(bt), stack(kt)], axis=0)
    vs = stack(v)
    ar = jnp.concatenate([at, rt], axis=0)

    nh = H_RWKV * cl
    ll = _dot3(ar, bks, _NT)
    t2 = lax.broadcasted_iota(I32, (cl, nh), 0)
    s2 = lax.broadcasted_iota(I32, (cl, nh), 1) % cl
    diff = (t2 - s2) * sgn
    strict = diff > 0
    incl = diff >= 0
    l_ab = jnp.where(strict, ll[:cl, :nh], 0.0)
    l_ak = jnp.where(strict, ll[:cl, nh:], 0.0)
    l_rb = jnp.where(incl, ll[cl:, :nh], 0.0)
    l_rk = jnp.where(incl, ll[cl:, nh:], 0.0)

    bd_c = _head_ones(nh, cl)
    x = jnp.concatenate([l_ab] * H_RWKV, axis=0) * bd_c
    eye = (lax.broadcasted_iota(I32, (nh, nh), 0) == lax.broadcasted_iota(I32, (nh, nh), 1)).astype(F32)
    inv = eye + x
    span = 2
    while span < cl:
        x = _dot3(x, x)
        inv = inv + _dot3(inv, x)
        span *= 2

    s_old = s_scr[...]
    g2 = _dot3(ar, s_old, _NT)
    rhs = g2[:cl] + _dot3(l_ak, vs)
    us = _dot3(inv, stack(rhs))
    um = us[:cl]
    for h in range(1, H_RWKV):
        um = um + us[h * cl:(h + 1) * cl]
    y = g2[cl:] + _dot3(jnp.concatenate([l_rb, l_rk], axis=1), jnp.concatenate([us, vs], axis=0))
    y_ref[0, 0] = y

    uv_t = jnp.concatenate([um, v], axis=0).T
    s_new = s_old * w_c + _dot3(uv_t, jnp.concatenate([bh, kh], axis=0)) * _head_ones(hw, N_RWKV)
    s_scr[...] = s_new

    @pl.when(c == nc - 1)
    def _():
        sf_ref[0, 0] = s_new


def _rwkv_scan(r, v, kk, lw, ag, kd, s0):
    nb, t, hw = r.shape
    cl = RWKV_CHUNK
    nc = t // cl

    def cidx(d, c):
        return c + d * (nc - 1 - 2 * c)

    shared = pl.BlockSpec((1, cl, hw), lambda b, d, c: (b, cidx(d, c), 0))
    perdir = pl.BlockSpec((1, cl, hw), lambda b, d, c: (b, cidx(d, c), d))
    state = pl.BlockSpec((1, 1, hw, hw), lambda b, d, c: (b, d, 0, 0))
    return pl.pallas_call(
        _rwkv_scan_body,
        grid=(nb, 2, nc),
        in_specs=[shared, shared, shared, perdir, perdir, perdir, state],
        out_specs=[pl.BlockSpec((1, 1, cl, hw), lambda b, d, c: (d, b, cidx(d, c), 0)), state],
        out_shape=[jax.ShapeDtypeStruct((2, nb, t, hw), F32), jax.ShapeDtypeStruct((nb, 2, hw, hw), F32)],
        scratch_shapes=[pltpu.VMEM((hw, hw), F32)],
        compiler_params=_cparams("parallel", "parallel", "arbitrary"),
        name="rwkv_scan",
    )(r, v, kk, lw, ag, kd, s0)


def _lru_body(reverse, p_ref, halo_ref, cw_ref, cb_ref, wa_ref, ba_ref, wx_ref, bx_ref, lam_ref, h0_ref,
              h_out, fin_out, a_scr, u_scr, carry_scr):
    i = pl.program_id(1)
    nt = pl.num_programs(1)
    tt = p_ref.shape[1]
    n_ext = tt + SUBLANES

    @pl.when(i == 0)
    def _():
        carry_scr[...] = jnp.broadcast_to(h0_ref[0], (SUBLANES, D_LRU))

    xb = p_ref[0, :, :D_LRU]
    halo = halo_ref[0, 0]
    xc = cb_ref[...] + jnp.zeros_like(xb)
    if reverse:
        xe = jnp.concatenate([xb, halo], axis=0)
        for k in range(CONV_W):
            sh = xb if k == 0 else pltpu.roll(xe, n_ext - k, 0)[:tt]
            xc = xc + cw_ref[k:k + 1, :] * sh
    else:
        xe = jnp.concatenate([halo, xb], axis=0)
        for k in range(CONV_W):
            s = CONV_W - 1 - k
            sh = xb if s == 0 else pltpu.roll(xe, s, 0)[SUBLANES:]
            xc = xc + cw_ref[k:k + 1, :] * sh
    gate_r = _sigmoid(_dot3(xc, wa_ref[...]) + ba_ref[...])
    gate_i = _sigmoid(_dot3(xc, wx_ref[...]) + bx_ref[...])
    log_a = -LRU_C * gate_r * _softplus(-lam_ref[...])
    a = jnp.exp(log_a)
    u = jnp.sqrt(-_expm1(2.0 * log_a)) * (gate_i * xc)

    rowm = lax.broadcasted_iota(I32, (tt, 1), 0) % SUBLANES
    for s in (1, 2, 4):
        if reverse:
            a_s = pltpu.roll(a, tt - s, 0)
            u_s = pltpu.roll(u, tt - s, 0)
            m = rowm < SUBLANES - s
        else:
            a_s = pltpu.roll(a, s, 0)
            u_s = pltpu.roll(u, s, 0)
            m = rowm >= s
        u = jnp.where(m, a * u_s + u, u)
        a = jnp.where(m, a * a_s, a)
    a_scr[...] = a
    u_scr[...] = u
    ng = tt // SUBLANES

    def group(g, carry):
        gi = (ng - 1 - g) if reverse else g
        r0 = pl.multiple_of(gi * SUBLANES, SUBLANES)
        hg = a_scr[pl.ds(r0, SUBLANES), :] * carry + u_scr[pl.ds(r0, SUBLANES), :]
        h_out[0, pl.ds(r0, SUBLANES), :] = hg
        edge = hg[0:1, :] if reverse else hg[SUBLANES - 1:SUBLANES, :]
        return jnp.broadcast_to(edge, (SUBLANES, D_LRU))

    carry = lax.fori_loop(0, ng, group, carry_scr[...])
    carry_scr[...] = carry

    @pl.when(i == nt - 1)
    def _():
        fin_out[0] = carry[0:1, :]


def _lru_dir(p_lru, halo, lp, d, h0, tt):
    nb, t, pw = p_lru.shape
    nt = t // tt
    reverse = d == 1
    tidx = (lambda i: nt - 1 - i) if reverse else (lambda i: i)
    small = [lp['conv_w'][d], lp['conv_b'][d:d + 1], lp['wa_bd'][d], lp['b_lru_a'][d:d + 1], lp['wx_bd'][d],
             lp['b_lru_x'][d:d + 1], lp['lam'][d:d + 1]]
    return pl.pallas_call(
        functools.partial(_lru_body, reverse),
        grid=(nb, nt),
        in_specs=[
            pl.BlockSpec((1, tt, pw), lambda b, i: (b, tidx(i), 0)),
            pl.BlockSpec((1, 1, SUBLANES, D_LRU), lambda b, i: (b, tidx(i), 0, 0)),
        ] + [_full(a.shape) for a in small] + [pl.BlockSpec((1, 1, D_LRU), lambda b, i: (b, 0, 0))],
        out_specs=[
            pl.BlockSpec((1, tt, D_LRU), lambda b, i: (b, tidx(i), 0)),
            pl.BlockSpec((1, 1, D_LRU), lambda b, i: (b, 0, 0)),
        ],
        out_shape=[jax.ShapeDtypeStruct((nb, t, D_LRU), F32), jax.ShapeDtypeStruct((nb, 1, D_LRU), F32)],
        scratch_shapes=[pltpu.VMEM((tt, D_LRU), F32), pltpu.VMEM((tt, D_LRU), F32), pltpu.VMEM((SUBLANES, D_LRU), F32)],
        compiler_params=_cparams("parallel", "arbitrary"),
        name="lru_bwd" if reverse else "lru_fwd",
    )(p_lru, halo, *small, h0)


def _gelu_tanh(x):
    return 0.5 * x * (1.0 + jnp.tanh(np.sqrt(2.0 / np.pi).astype(np.float32) * (x + 0.044715 * (x * x * x))))


def _out_proj_body(x_ref, o_ref, y0_ref, y1_ref, bonus_ref, g_ref, h0_ref, h1_ref, p_ref,
                   gmla_ref, lnw_ref, lnb_ref, glru_ref, wo_ref, gpost_ref, gate_ref, gpre_ref, sc_ref, sh_ref,
                   x1_out, h2_out):
    o_mla = _rms(o_ref[0], gmla_ref[...])
    ones = _head_ones(D_RWKV, N_RWKV)
    y = y0_ref[0, 0] + y1_ref[0, 0]
    mu = _dot_x(y, ones) * (1.0 / N_RWKV)
    yc = y - mu
    var = _dot_x(yc * yc, ones) * (1.0 / N_RWKV)
    y = yc * lax.rsqrt(var + GN_EPS) * lnw_ref[...] + lnb_ref[...]
    o_rwkv = (y + bonus_ref[0]) * g_ref[0]
    o_lru = _rms((h0_ref[0] + h1_ref[0]) * _gelu_tanh(p_ref[0, :, D_LRU:]), glru_ref[...])
    m = (_dg(o_mla.astype(BF16), wo_ref[:D_MLA_OUT, :], _NN)
         + _dg(o_rwkv.astype(BF16), wo_ref[D_MLA_OUT:D_MLA_OUT + D_RWKV, :], _NN)
         + _dg(o_lru.astype(BF16), wo_ref[D_MLA_OUT + D_RWKV:, :], _NN))
    x1 = x_ref[0] + gate_ref[0] * _rms(m, gpost_ref[...])
    x1_out[0] = x1
    h2_out[0] = _rms(x1, gpre_ref[...]) * (1.0 + sc_ref[0]) + sh_ref[0]


def _out_proj(x, o, y, bonus, g, h0, h1, p_lru, lp, mod, tt):
    nb, t, d = x.shape
    per_batch = mod['g1'].shape[0] > 1
    tok = lambda n: pl.BlockSpec((1, tt, n), lambda b, i: (b, i, 0))
    ydir = lambda dd: pl.BlockSpec((1, 1, tt, D_RWKV), lambda b, i: (dd, b, i, 0))
    small = [lp['g_mla_out'], lp['ln_x_w'], lp['ln_x_b'], lp['g_lru_out'], lp['w_out'], lp['g_post_mix']]
    return pl.pallas_call(
        _out_proj_body,
        grid=(nb, t // tt),
        in_specs=[tok(d), tok(D_MLA_OUT), ydir(0), ydir(1), tok(D_RWKV), tok(D_RWKV), tok(D_LRU), tok(D_LRU),
                  tok(P_LRU)] + [_full(a.shape) for a in small]
                 + [_mod_spec(per_batch), _full(lp['g_pre_ffn'].shape), _mod_spec(per_batch), _mod_spec(per_batch)],
        out_specs=[tok(d), tok(d)],
        out_shape=[jax.ShapeDtypeStruct((nb, t, d), F32)] * 2,
        compiler_params=_cparams("parallel", "parallel"),
        name="out_proj",
    )(x, o, y, y, bonus, g, h0, h1, p_lru, *small, mod['g1'], lp['g_pre_ffn'], mod['sc2'], mod['sh2'])


def _first_index(mask, idx, sentinel):
    return jnp.min(jnp.where(mask, idx, sentinel), axis=0, keepdims=True)


def _router_body(h_ref, wr_ref, br_ref, e_out, pos_out, w_out, cnt_out, carry):
    i = pl.program_id(0)
    tn = h_ref.shape[0]

    @pl.when(i == 0)
    def _():
        carry[...] = jnp.zeros_like(carry)

    logits = lax.dot_general(wr_ref[...], h_ref[...], _NT, precision=lax.Precision.HIGHEST,
                             preferred_element_type=F32)
    scores = _sigmoid(logits)
    sel = scores + br_ref[...]

    i8 = lax.broadcasted_iota(I32, (GROUP_SIZE, tn), 0)
    blocks, gscore = [], []
    for g in range(N_GROUPS):
        blk = sel[g * GROUP_SIZE:(g + 1) * GROUP_SIZE, :]
        m1 = jnp.max(blk, axis=0, keepdims=True)
        f1 = _first_index(blk == m1, i8, GROUP_SIZE)
        m2 = jnp.max(jnp.where(i8 == f1, -jnp.inf, blk), axis=0, keepdims=True)
        blocks.append(blk)
        gscore.append(m1 + m2)
    masked = []
    for g in range(N_GROUPS):
        beaten = jnp.zeros((1, tn), I32)
        for o in range(N_GROUPS):
            if o == g:
                continue
            wins = (gscore[o] >= gscore[g]) if o < g else (gscore[o] > gscore[g])
            beaten = beaten + wins.astype(I32)
        masked.append(jnp.where(beaten < TOPK_GROUPS, blocks[g], -jnp.inf))
    cur = jnp.concatenate(masked, axis=0)

    ie = lax.broadcasted_iota(I32, (N_EXPERTS, tn), 0)
    firsts, raw = [], []
    chosen = jnp.zeros((N_EXPERTS, tn), F32)
    for _ in range(TOP_K):
        mx = jnp.max(cur, axis=0, keepdims=True)
        first = _first_index(cur == mx, ie, N_EXPERTS)
        hit = ie == first
        firsts.append(first)
        raw.append(jnp.sum(jnp.where(hit, scores, 0.0), axis=0, keepdims=True))
        cur = jnp.where(hit, -jnp.inf, cur)
        chosen = chosen + hit.astype(F32)
    total = raw[0]
    for k in range(1, TOP_K):
        total = total + raw[k]

    before = (lax.broadcasted_iota(I32, (tn, tn), 0) < lax.broadcasted_iota(I32, (tn, tn), 1)).astype(BF16)
    rank = _dg(chosen.astype(BF16), before, _NN) + carry[:, 0:1]
    e_out[...] = jnp.zeros_like(e_out)
    pos_out[...] = jnp.zeros_like(pos_out)
    w_out[...] = jnp.zeros_like(w_out)
    for k in range(TOP_K):
        e_out[0, k:k + 1, :] = firsts[k]
        pos_out[0, k:k + 1, :] = jnp.sum(jnp.where(ie == firsts[k], rank, 0.0), axis=0, keepdims=True).astype(I32)
        w_out[k:k + 1, :] = raw[k] / total * ROUTED_SCALE
    new_carry = carry[...] + jnp.sum(chosen, axis=1, keepdims=True)
    carry[...] = new_carry
    cnt_out[...] = new_carry


def _router(h2, wr_t, b_col):
    n, d = h2.shape
    tn = ROUTER_TILE
    row_spec = pl.BlockSpec((SLOT_ROWS, tn), lambda i: (0, i))
    tile_spec = pl.BlockSpec((1, SLOT_ROWS, tn), lambda i: (i, 0, 0))
    cnt_spec = pl.BlockSpec((N_EXPERTS, 128), lambda i: (0, 0))
    tile_shape = jax.ShapeDtypeStruct((n // tn, SLOT_ROWS, tn), I32)
    return pl.pallas_call(
        _router_body,
        grid=(n // tn,),
        in_specs=[pl.BlockSpec((tn, d), lambda i: (i, 0)), _full(wr_t.shape), _full(b_col.shape)],
        out_specs=[tile_spec, tile_spec, row_spec, cnt_spec],
        out_shape=[tile_shape, tile_shape,
                   jax.ShapeDtypeStruct((SLOT_ROWS, n), F32), jax.ShapeDtypeStruct((N_EXPERTS, 128), F32)],
        scratch_shapes=[pltpu.VMEM((N_EXPERTS, 128), F32)],
        compiler_params=_cparams("arbitrary"),
        name="router",
    )(h2, wr_t, b_col)


def _row_copy(src, src_row, dst, dst_row, sem):
    return pltpu.make_async_copy(src.at[pl.ds(src_row, 1), :], dst.at[pl.ds(dst_row, 1), :], sem)


def _load_slots(slot_ref, slot_smem, sem):
    cp = pltpu.make_async_copy(slot_ref.at[0], slot_smem, sem)
    cp.start()
    cp.wait()


def _dispatch_body(start_ref, e_ref, pos_ref, h_ref, xs_in, slots_out, xs_out, slot_smem, sem_s, sem):
    del xs_in
    tn = slot_smem.shape[1]
    e = e_ref[0]
    slot = pos_ref[0]
    for x in range(N_EXPERTS):
        slot = slot + jnp.where(e == x, start_ref[x], 0)
    slots_out[0] = slot
    _load_slots(slots_out, slot_smem, sem_s)

    def issue(j, carry):
        for k in range(TOP_K):
            _row_copy(h_ref, j, xs_out, slot_smem[k, j], sem).start()
        return carry

    def drain(j, carry):
        for k in range(TOP_K):
            _row_copy(h_ref, 0, xs_out, 0, sem).wait()
        return carry

    lax.fori_loop(0, tn, issue, 0)
    lax.fori_loop(0, tn, drain, 0)


def _dispatch(row_start, e3, pos3, h2, xs_zero, tn):
    n, d = h2.shape
    ratio = e3.shape[2] // tn
    tile = lambda: pl.BlockSpec((1, SLOT_ROWS, tn), lambda i, st: (i // ratio, 0, i % ratio))
    grid_spec = pltpu.PrefetchScalarGridSpec(
        num_scalar_prefetch=1,
        grid=(n // tn,),
        in_specs=[tile(), tile(), pl.BlockSpec((tn, d), lambda i, st: (i, 0)), pl.BlockSpec(memory_space=pl.ANY)],
        out_specs=[pl.BlockSpec((1, SLOT_ROWS, tn), lambda i, st: (i, 0, 0)), pl.BlockSpec(memory_space=pl.ANY)],
        scratch_shapes=[pltpu.SMEM((SLOT_ROWS, tn), I32), pltpu.SemaphoreType.DMA, pltpu.SemaphoreType.DMA],
    )
    return pl.pallas_call(
        _dispatch_body,
        grid_spec=grid_spec,
        out_shape=[jax.ShapeDtypeStruct((n // tn, SLOT_ROWS, tn), I32),
                   jax.ShapeDtypeStruct(xs_zero.shape, xs_zero.dtype)],
        input_output_aliases={4: 1},
        compiler_params=_cparams("arbitrary"),
        name="dispatch",
    )(row_start, e3, pos3, h2, xs_zero)


def _swiglu(x, wg, wu, wd):
    g = _dg(x, wg, _NN)
    u = _dg(x, wu, _NN)
    return _dg((g * _sigmoid(g) * u).astype(BF16), wd, _NN)


def _expert_body(be_ref, nu_ref, xs_ref, wg_ref, wu_ref, wd_ref, ys_ref):
    del be_ref
    b = pl.program_id(0)

    @pl.when(b < nu_ref[0])
    def _():
        ys_ref[...] = _swiglu(xs_ref[...].astype(BF16), wg_ref[0], wu_ref[0], wd_ref[0])

    @pl.when(b >= nu_ref[0])
    def _():
        ys_ref[...] = jnp.zeros_like(ys_ref)


def _experts(block_e, n_used, xs, wg, wu, wd):
    n_slots, d = xs.shape
    bm = MOE_BLOCK
    de = wg.shape[2]
    grid_spec = pltpu.PrefetchScalarGridSpec(
        num_scalar_prefetch=2,
        grid=(n_slots // bm,),
        in_specs=[
            pl.BlockSpec((bm, d), lambda b, be, nu: (b, 0)),
            pl.BlockSpec((1, d, de), lambda b, be, nu: (be[b], 0, 0)),
            pl.BlockSpec((1, d, de), lambda b, be, nu: (be[b], 0, 0)),
            pl.BlockSpec((1, de, d), lambda b, be, nu: (be[b], 0, 0)),
        ],
        out_specs=pl.BlockSpec((bm, d), lambda b, be, nu: (b, 0)),
    )
    return pl.pallas_call(
        _expert_body,
        grid_spec=grid_spec,
        out_shape=jax.ShapeDtypeStruct((n_slots, d), F32),
        compiler_params=_cparams("arbitrary"),
        name="experts",
    )(block_e, n_used, xs, wg, wu, wd)


def _combine_body(slot_ref, w_ref, x1_ref, h2_ref, gate_ref, ys_hbm, wsg_ref, wsu_ref, wsd_ref, gpost_ref,
                  out_ref, slot_smem, buf, sem_s, sem):
    tn = slot_smem.shape[1]
    _load_slots(slot_ref, slot_smem, sem_s)

    def issue(j, carry):
        for k in range(TOP_K):
            _row_copy(ys_hbm, slot_smem[k, j], buf.at[k], j, sem).start()
        return carry

    def drain(j, carry):
        for k in range(TOP_K):
            _row_copy(ys_hbm, 0, buf.at[k], 0, sem).wait()
        return carry

    lax.fori_loop(0, tn, issue, 0)
    acc = _swiglu(h2_ref[...].astype(BF16), wsg_ref[...], wsu_ref[...], wsd_ref[...])
    lax.fori_loop(0, tn, drain, 0)
    for k in range(TOP_K):
        acc = acc + buf[k] * w_ref[:, k:k + 1]
    out_ref[...] = x1_ref[...] + gate_ref[0] * _rms(acc, gpost_ref[...])


def _combine(slots3, w_tok, x1, h2, gate_rows, ys, wsg, wsu, wsd, gpost):
    ntiles, _, tn = slots3.shape
    n, d = x1.shape
    tok = pl.BlockSpec((tn, d), lambda i: (i, 0))
    return pl.pallas_call(
        _combine_body,
        grid=(ntiles,),
        in_specs=[pl.BlockSpec((1, SLOT_ROWS, tn), lambda i: (i, 0, 0)),
                  pl.BlockSpec((tn, SLOT_ROWS), lambda i: (i, 0)), tok, tok,
                  pl.BlockSpec((1, 1, d), lambda i: (i, 0, 0)),
                  pl.BlockSpec(memory_space=pl.ANY),
                  _full(wsg.shape), _full(wsu.shape), _full(wsd.shape), _full(gpost.shape)],
        out_specs=tok,
        out_shape=jax.ShapeDtypeStruct((n, d), F32),
        scratch_shapes=[pltpu.SMEM((SLOT_ROWS, tn), I32), pltpu.VMEM((TOP_K, tn, d), F32),
                        pltpu.SemaphoreType.DMA, pltpu.SemaphoreType.DMA],
        compiler_params=_cparams("arbitrary"),
        name="combine",
    )(slots3, w_tok, x1, h2, gate_rows, ys, wsg, wsu, wsd, gpost)


def _moe(x1, h2, gate_rows, lp):
    n, d = h2.shape
    bm = MOE_BLOCK
    tn = TOKEN_TILE
    e3, pos3, w_rows, cnt = _router(h2, lp['wr_t'], lp['b_router_col'])
    counts = cnt[:, 0].astype(I32)
    blocks_per_e = (counts + bm - 1) // bm
    blk_end = jnp.cumsum(blocks_per_e)
    blk_start = blk_end - blocks_per_e
    n_blocks = -(-(n * TOP_K) // bm) + N_EXPERTS
    block_e = jnp.minimum(jnp.sum(blk_end[None, :] <= jnp.arange(n_blocks, dtype=I32)[:, None], axis=1),
                          N_EXPERTS - 1).astype(I32)
    n_used = blk_end[-1:].astype(I32)
    slots3, xs = _dispatch((blk_start * bm).astype(I32), e3, pos3, h2, jnp.zeros((n_blocks * bm, d), F32), tn)
    ys = _experts(block_e, n_used, xs, lp['w_e_gate'], lp['w_e_up'], lp['w_e_down'])
    return _combine(slots3, w_rows.T, x1, h2, gate_rows, ys, lp['w_s_gate'], lp['w_s_up'], lp['w_s_down'],
                    lp['g_post_ffn'])


def _rope_tables(n_tokens):
    rows = n_tokens // GRID_W
    row = jnp.repeat(jnp.arange(rows, dtype=F32), GRID_W)
    col = jnp.tile(jnp.arange(GRID_W, dtype=F32), rows)
    n_freq = ROPE_DIM // 4
    inv_freq = ROPE_THETA ** (-jnp.arange(n_freq, dtype=F32) / n_freq)
    ang_r = row[:, None] * inv_freq
    ang_c = col[:, None] * inv_freq
    ang = jnp.concatenate([ang_r, ang_r, ang_c, ang_c], axis=-1)
    cos, sin = jnp.cos(ang), jnp.sin(ang)
    pad = lambda z, fill: jnp.concatenate(
        [jnp.full((n_tokens, NOPE_DIM), fill, F32), z, jnp.zeros((n_tokens, HEAD_PAD - QK_DIM), F32)], axis=1)
    return pad(cos, 1.0), pad(sin, 0.0), pad(cos, 0.0), pad(sin, 0.0)


def _rot_cols(w):
    q = ROPE_DIM // 4
    perm = np.concatenate([np.arange(q, 2 * q), np.arange(0, q), np.arange(3 * q, 4 * q), np.arange(2 * q, 3 * q)])
    sign = np.concatenate([-np.ones(q), np.ones(q), -np.ones(q), np.ones(q)]).astype(np.float32)
    return w[..., perm] * sign


def _block_diag(blocks):
    h, m, n = blocks.shape[-3:]
    eye = jnp.eye(h, dtype=blocks.dtype)
    out = blocks[..., :, :, None, :] * eye[:, None, :, None]
    return out.reshape(blocks.shape[:-3] + (h * m, h * n))


def _diag_blocks(mat, h):
    m, n = mat.shape[-2] // h, mat.shape[-1] // h
    z = mat.reshape(mat.shape[:-2] + (h, m, h, n))
    return jnp.stack([z[..., i, :, i, :] for i in range(h)], axis=-3)


def _layer_operands(l, a):
    w_in = a['w_in'][l]
    d = w_in.shape[0]
    p_mla = Q_LORA + KV_LORA + ROPE_DIM
    w_kr = w_in[:, Q_LORA + KV_LORA:p_mla]
    z32 = jnp.zeros((d, ROPE_DIM), F32)
    w_mla = jnp.concatenate([w_in[:, :Q_LORA + KV_LORA], z32, _rot_cols(w_kr), w_kr, z32], axis=1)
    wq3 = a['w_q_b'][l].reshape(Q_LORA, H_MLA, QK_DIM)
    zq = lambda n: jnp.zeros((Q_LORA, H_MLA, n), F32)
    wq = jnp.concatenate([wq3, zq(HEAD_PAD - QK_DIM)], axis=2).reshape(Q_LORA, H_MLA * HEAD_PAD)
    wqr = jnp.concatenate([zq(NOPE_DIM), _rot_cols(wq3[:, :, NOPE_DIM:]), zq(HEAD_PAD - QK_DIM)],
                          axis=2).reshape(Q_LORA, H_MLA * HEAD_PAD)
    wkv3 = a['w_kv_b'][l].reshape(KV_LORA, H_MLA, NOPE_DIM + V_DIM)
    wk = jnp.concatenate([wkv3[:, :, :NOPE_DIM], jnp.zeros((KV_LORA, H_MLA, HEAD_PAD - NOPE_DIM), F32)],
                         axis=2).reshape(KV_LORA, H_MLA * HEAD_PAD)
    wv = wkv3[:, :, NOPE_DIM:].reshape(KV_LORA, H_MLA * V_DIM)
    row = lambda z: z.reshape(1, -1)
    two = lambda z: jnp.concatenate([z[0], z[1]], axis=-1)
    bd2 = lambda z: jnp.concatenate([jnp.concatenate([z[0], jnp.zeros_like(z[0])], axis=1),
                                     jnp.concatenate([jnp.zeros_like(z[1]), z[1]], axis=1)], axis=0)
    return {
        'g_pre_mix': row(a['g_pre_mix'][l]), 'g_post_mix': row(a['g_post_mix'][l]),
        'g_pre_ffn': row(a['g_pre_ffn'][l]), 'g_post_ffn': row(a['g_post_ffn'][l]),
        'w_mla': w_mla.astype(BF16),
        'w_rwkv': w_in[:, p_mla:p_mla + P_RWKV].astype(BF16),
        'w_lru': w_in[:, p_mla + P_RWKV:].astype(BF16),
        'g_q_a': row(a['g_q_a'][l]), 'g_kv_a': row(a['g_kv_a'][l]),
        'wq': wq.astype(BF16), 'wqr': wqr.astype(BF16), 'wk': wk.astype(BF16), 'wv': wv.astype(BF16),
        'g_mla_out': row(a['g_mla_out'][l]),
        'mu_rwkv': row(a['mu_rwkv'][l]), 'k_k': row(a['k_k'][l]), 'k_a': row(a['k_a'][l]), 'r_k': row(a['r_k'][l]),
        'w0_cat': row(two(a['w0_rwkv'][l])), 'w2_bd': bd2(a['w2_rwkv'][l]),
        'a0_cat': row(two(a['a0_rwkv'][l])), 'a2_bd': bd2(a['a2_rwkv'][l]),
        'g2_rwkv': a['g2_rwkv'][l], 'ln_x_w': row(a['ln_x_w'][l]), 'ln_x_b': row(a['ln_x_b'][l]),
        'conv_w': a['conv_w'][l], 'conv_b': a['conv_b'][l],
        'wa_bd': _block_diag(a['w_lru_a'][l]), 'b_lru_a': a['b_lru_a'][l],
        'wx_bd': _block_diag(a['w_lru_x'][l]), 'b_lru_x': a['b_lru_x'][l],
        'lam': a['lam'][l], 'g_lru_out': row(a['g_lru_out'][l]),
        'w_out': a['w_out'][l].astype(BF16),
        'wr_t': a['w_router'][l].T, 'b_router_col': a['b_router'][l].reshape(N_EXPERTS, 1),
        'w_e_gate': a['w_e_gate'][l].astype(BF16), 'w_e_up': a['w_e_up'][l].astype(BF16),
        'w_e_down': a['w_e_down'][l].astype(BF16),
        'w_s_gate': a['w_s_gate'][l].astype(BF16), 'w_s_up': a['w_s_up'][l].astype(BF16),
        'w_s_down': a['w_s_down'][l].astype(BF16),
    }


def _mixer(x, mod, lp, ctx, tables):
    nb, t, _ = x.shape
    tt = min(t, 512)
    p_mla, p_rwkv, p_lru = _in_proj(x, lp['g_pre_mix'], mod['sc1'], mod['sh1'], lp['w_mla'], lp['w_rwkv'],
                                    lp['w_lru'], tt)
    if ctx is None:
        q, k, v, new_cache = _mla_prep(p_mla, lp['g_q_a'], lp['g_kv_a'], lp['wq'], None, lp['wk'], lp['wv'], None, tt)
        kc = vc = None
        s0 = jnp.zeros((nb, 2, D_RWKV, D_RWKV), F32)
        h0 = jnp.zeros((nb, 2, 1, D_LRU), F32)
    else:
        cache, s_rwkv, s_lru = ctx
        q, k, v = _mla_prep(p_mla, lp['g_q_a'], lp['g_kv_a'], lp['wq'], lp['wqr'], lp['wk'], lp['wv'], tables, tt)
        new_cache = None
        kr_placed = jnp.pad(cache[..., KV_LORA:], ((0, 0), (0, 0), (NOPE_DIM, HEAD_PAD - QK_DIM)))
        kc, vc = _ctx_kv(cache[..., :KV_LORA], kr_placed, lp['wk'], lp['wv'])
        s0 = _block_diag(s_rwkv)
        h0 = s_lru[:, :, None, :]
    o = _attention(q, k, v, kc, vc, min(t, 256))

    r, vv, kk, lw, ag, kd, g, bonus = _rwkv_prep(p_rwkv, lp, tt)
    y, s_fin = _rwkv_scan(r, vv, kk, lw, ag, kd, s0)

    hp, hn = _halo_rows(p_lru[..., :D_LRU], tt, SUBLANES)
    h_f, fin_f = _lru_dir(p_lru, hp, lp, 0, h0[:, 0], tt)
    h_b, fin_b = _lru_dir(p_lru, hn, lp, 1, h0[:, 1], tt)

    x1, h2 = _out_proj(x, o, y, bonus, g, h_f, h_b, p_lru, lp, mod, tt)
    states = None
    if ctx is None:
        states = (new_cache, _diag_blocks(s_fin, H_RWKV), jnp.concatenate([fin_f, fin_b], axis=1))
    return x1, h2, states


def kernel(x_prompt, x_sample, c, cache_mla, state_rwkv, state_lru, c_ctx, w_ada, b_ada, g_pre_mix, g_post_mix, g_pre_ffn, g_post_ffn, w_in, g_q_a, w_q_b, g_kv_a, w_kv_b, g_mla_out, mu_rwkv, w0_rwkv, w2_rwkv, a0_rwkv, a2_rwkv, g2_rwkv, k_k, k_a, r_k, ln_x_w, ln_x_b, conv_w, conv_b, w_lru_a, b_lru_a, w_lru_x, b_lru_x, lam, g_lru_out, w_out, w_router, b_router, w_e_gate, w_e_up, w_e_down, w_s_gate, w_s_up, w_s_down):
    a = dict(w_in=w_in, g_pre_mix=g_pre_mix, g_post_mix=g_post_mix, g_pre_ffn=g_pre_ffn, g_post_ffn=g_post_ffn,
             g_q_a=g_q_a, w_q_b=w_q_b, g_kv_a=g_kv_a, w_kv_b=w_kv_b, g_mla_out=g_mla_out, mu_rwkv=mu_rwkv,
             w0_rwkv=w0_rwkv, w2_rwkv=w2_rwkv, a0_rwkv=a0_rwkv, a2_rwkv=a2_rwkv, g2_rwkv=g2_rwkv, k_k=k_k, k_a=k_a,
             r_k=r_k.reshape(r_k.shape[0], -1), ln_x_w=ln_x_w, ln_x_b=ln_x_b, conv_w=conv_w, conv_b=conv_b,
             w_lru_a=w_lru_a, b_lru_a=b_lru_a, w_lru_x=w_lru_x, b_lru_x=b_lru_x, lam=lam, g_lru_out=g_lru_out,
             w_out=w_out, w_router=w_router, b_router=b_router, w_e_gate=w_e_gate, w_e_up=w_e_up,
             w_e_down=w_e_down, w_s_gate=w_s_gate, w_s_up=w_s_up, w_s_down=w_s_down)
    n_layers = w_in.shape[0]
    nbp, tp, d = x_prompt.shape
    nbs, ts, _ = x_sample.shape
    rows = -(-(1 + nbs) // SUBLANES) * SUBLANES
    cvecs = jnp.concatenate([c_ctx[None, :], c, jnp.zeros((rows - 1 - nbs, d), F32)], axis=0)
    mods = _ada(cvecs, w_ada, b_ada)
    names = ('sh1', 'sc1', 'g1', 'sh2', 'sc2', 'g2')
    tables = _rope_tables(ts)
    tn = TOKEN_TILE

    yp, ys = x_prompt, x_sample
    caches, rwkv_states, lru_states = [], [], []
    for l in range(n_layers):
        lp = _layer_operands(l, a)
        mod_p = {nm: mods[l, 0:1, i * d:(i + 1) * d][:, None, :] for i, nm in enumerate(names)}
        mod_s = {nm: mods[l, 1:1 + nbs, i * d:(i + 1) * d][:, None, :] for i, nm in enumerate(names)}
        x1p, h2p, states = _mixer(yp, mod_p, lp, None, None)
        x1s, h2s, _ = _mixer(ys, mod_s, lp, (cache_mla[:, l], state_rwkv[:, l], state_lru[:, l]), tables)
        caches.append(states[0])
        rwkv_states.append(states[1])
        lru_states.append(states[2])
        x1 = jnp.concatenate([x1p.reshape(-1, d), x1s.reshape(-1, d)], axis=0)
        h2 = jnp.concatenate([h2p.reshape(-1, d), h2s.reshape(-1, d)], axis=0)
        gate_rows = jnp.concatenate([jnp.repeat(mod_p['g2'], nbp * tp // tn, axis=0),
                                     jnp.repeat(mod_s['g2'], ts // tn, axis=0)], axis=0)
        x2 = _moe(x1, h2, gate_rows, lp)
        yp = x2[:nbp * tp].reshape(nbp, tp, d)
        ys = x2[nbp * tp:].reshape(nbs, ts, d)
    return (yp, ys, jnp.stack(caches, axis=1), jnp.stack(rwkv_states, axis=1), jnp.stack(lru_states, axis=1))
```

```python
import functools

import numpy as np
import jax
import jax.numpy as jnp
from jax import lax
from jax.experimental import pallas as pl
from jax.experimental.pallas import tpu as pltpu

F32 = jnp.float32
BF16 = jnp.bfloat16
I32 = jnp.int32

D_MODEL = 1024
GRID_W = 64
EPS = 1e-6
H_MLA = 4
Q_LORA = 256
KV_LORA = 128
NOPE_DIM = 64
ROPE_DIM = 32
V_DIM = 128
QK_DIM = NOPE_DIM + ROPE_DIM
ROPE_THETA = 10000.0
HEAD_PAD = 128
H_RWKV = 4
N_RWKV = 64
D_RWKV = H_RWKV * N_RWKV
W_LORA = 64
A_LORA = 64
G_LORA = 128
GN_EPS = 64e-5
D_LRU = 256
H_LRU = 4
BS_LRU = D_LRU // H_LRU
CONV_W = 4
LRU_C = 8.0
D_MLA_OUT = H_MLA * V_DIM
P_MLA_PAD = Q_LORA + KV_LORA + HEAD_PAD
P_RWKV = 3 * D_RWKV + 2 * W_LORA + 2 * A_LORA + G_LORA
P_LRU = 2 * D_LRU
N_EXPERTS = 64
TOP_K = 6
N_GROUPS = 8
TOPK_GROUPS = 4
GROUP_SIZE = N_EXPERTS // N_GROUPS
D_EXPERT = 256
ROUTED_SCALE = 2.5

SUBLANES = 8
RWKV_CHUNK = 64
RWKV_ROWS = 2
MOE_BLOCK = 256
ROUTER_TILE = 512
TOKEN_TILE = 256
SLOT_ROWS = 8
VMEM_LIMIT = 56 * 1024 * 1024


def _cparams(*sem):
    return pltpu.CompilerParams(dimension_semantics=sem, vmem_limit_bytes=VMEM_LIMIT)


def _sigmoid(x):
    return 1.0 / (1.0 + jnp.exp(-x))


def _softplus(x):
    return jnp.maximum(x, 0.0) + jnp.log1p(jnp.exp(-jnp.abs(x)))


def _expm1(z):
    e = jnp.exp(z)
    direct = (e == 1.0) | (z < -1.0)
    corrected = (e - 1.0) * z / jnp.log(jnp.where(direct, 2.0, e))
    return jnp.where(e == 1.0, z, jnp.where(z < -1.0, e - 1.0, corrected))


def _rms(x, g):
    return x * lax.rsqrt(jnp.mean(x * x, axis=-1, keepdims=True) + EPS) * g


def _split2(x):
    hi = x.astype(BF16)
    lo = (x - hi.astype(F32)).astype(BF16)
    return hi, lo


def _split3(x):
    hi = x.astype(BF16)
    r1 = x - hi.astype(F32)
    mid = r1.astype(BF16)
    lo = (r1 - mid.astype(F32)).astype(BF16)
    return hi, mid, lo


_NN = (((1,), (0,)), ((), ()))
_NT = (((1,), (1,)), ((), ()))


def _dg(a, b, dims):
    return lax.dot_general(a, b, dims, preferred_element_type=F32)


def _dot1(a, b, dims=_NN):
    return _dg(a.astype(BF16), b.astype(BF16), dims)


def _dot3(a, b, dims=_NN):
    ah, al = _split2(a)
    bh, bl = _split2(b)
    return _dg(ah, bh, dims) + (_dg(ah, bl, dims) + _dg(al, bh, dims))


def _dot_x(a, b_exact, dims=_NN):
    h, m, l = _split3(a)
    bb = b_exact.astype(BF16)
    return _dg(h, bb, dims) + (_dg(m, bb, dims) + _dg(l, bb, dims))


def _x_dot(a_exact, b, dims=_NN):
    h, m, l = _split3(b)
    aa = a_exact.astype(BF16)
    return _dg(aa, h, dims) + (_dg(aa, m, dims) + _dg(aa, l, dims))


def _head_ones(n, seg):
    r = lax.broadcasted_iota(I32, (n, n), 0) // seg
    c = lax.broadcasted_iota(I32, (n, n), 1) // seg
    return (r == c).astype(F32)


def _ada_body(c_ref, w_ref, b_ref, o_ref):
    cv = c_ref[...]
    s = cv * _sigmoid(cv)
    o_ref[0] = _dot3(s, w_ref[0]) + b_ref[0]


def _ada(cvecs, w_ada, b_ada):
    n_layers, d, d6 = w_ada.shape
    rows = cvecs.shape[0]
    nt = d6 // d
    return pl.pallas_call(
        _ada_body,
        grid=(n_layers, nt),
        in_specs=[
            pl.BlockSpec((rows, d), lambda l, j: (0, 0)),
            pl.BlockSpec((1, d, d), lambda l, j: (l, 0, j)),
            pl.BlockSpec((1, 1, d), lambda l, j: (l, 0, j)),
        ],
        out_specs=pl.BlockSpec((1, rows, d), lambda l, j: (l, 0, j)),
        out_shape=jax.ShapeDtypeStruct((n_layers, rows, d6), F32),
        compiler_params=_cparams("parallel", "parallel"),
        name="ada",
    )(cvecs, w_ada, b_ada.reshape(n_layers, 1, d6))


def _mod_spec(per_batch):
    if per_batch:
        return pl.BlockSpec((1, 1, D_MODEL), lambda b, i: (b, 0, 0))
    return pl.BlockSpec((1, 1, D_MODEL), lambda b, i: (0, 0, 0))


def _full(shape):
    nd = len(shape)
    return pl.BlockSpec(shape, lambda *_: (0,) * nd)


def _in_proj_body(x_ref, g_ref, sc_ref, sh_ref, w1_ref, w2_ref, w3_ref, o1_ref, o2_ref, o3_ref):
    h = _rms(x_ref[0], g_ref[...]) * (1.0 + sc_ref[0]) + sh_ref[0]
    hb = h.astype(BF16)
    o1_ref[0] = _dg(hb, w1_ref[...], _NN)
    o2_ref[0] = _dg(hb, w2_ref[...], _NN)
    o3_ref[0] = _dg(hb, w3_ref[...], _NN)


def _in_proj(x, g, sc, sh, w_mla, w_rwkv, w_lru, tt):
    nb, t, d = x.shape
    per_batch = sc.shape[0] > 1
    outs = [w_mla.shape[1], w_rwkv.shape[1], w_lru.shape[1]]
    return pl.pallas_call(
        _in_proj_body,
        grid=(nb, t // tt),
        in_specs=[
            pl.BlockSpec((1, tt, d), lambda b, i: (b, i, 0)),
            _full((1, d)),
            _mod_spec(per_batch),
            _mod_spec(per_batch),
            _full(w_mla.shape),
            _full(w_rwkv.shape),
            _full(w_lru.shape),
        ],
        out_specs=[pl.BlockSpec((1, tt, n), lambda b, i: (b, i, 0)) for n in outs],
        out_shape=[jax.ShapeDtypeStruct((nb, t, n), F32) for n in outs],
        compiler_params=_cparams("parallel", "parallel"),
        name="in_proj",
    )(x, g, sc, sh, w_mla, w_rwkv, w_lru)


def _rope_lanes():
    lane = lax.broadcasted_iota(I32, (1, HEAD_PAD), 1)
    return ((lane >= NOPE_DIM) & (lane < QK_DIM)).astype(F32)


def _mla_prep_body(rope, *refs):
    if rope:
        (p_ref, gq_ref, gkv_ref, wq_ref, wqr_ref, wk_ref, wv_ref, cq_ref, sq_ref, cr_ref, sr_ref,
         q_ref, k_ref, v_ref) = refs
    else:
        p_ref, gq_ref, gkv_ref, wq_ref, wk_ref, wv_ref, q_ref, k_ref, v_ref, c_ref = refs
    p = p_ref[0]
    cq = _rms(p[:, :Q_LORA], gq_ref[...])
    ckv = _rms(p[:, Q_LORA:Q_LORA + KV_LORA], gkv_ref[...])
    kr = p[:, Q_LORA + KV_LORA:]
    cqb = cq.astype(BF16)
    ckvb = ckv.astype(BF16)
    scale = QK_DIM ** -0.5
    qa = _dg(cqb, wq_ref[...], _NN) * scale
    ka = _dg(ckvb, wk_ref[...], _NN)
    va = _dg(ckvb, wv_ref[...], _NN)
    if rope:
        qr = _dg(cqb, wqr_ref[...], _NN) * scale
        krp = kr * cr_ref[...] + pltpu.roll(kr, ROPE_DIM, 1) * sr_ref[...]
    else:
        krp = kr * _rope_lanes()
    for h in range(H_MLA):
        sl = slice(h * HEAD_PAD, (h + 1) * HEAD_PAD)
        qh = qa[:, sl]
        if rope:
            qh = qh * cq_ref[...] + qr[:, sl] * sq_ref[...]
        q_ref[0, h] = qh.astype(BF16)
        k_ref[0, h] = (ka[:, sl] + krp).astype(BF16)
        v_ref[0, h] = va[:, sl].astype(BF16)
    if not rope:
        c_ref[0, :, :KV_LORA] = ckv
        c_ref[0, :, KV_LORA:] = pltpu.roll(kr, HEAD_PAD - NOPE_DIM, 1)[:, :ROPE_DIM]


def _mla_prep(p_mla, gq, gkv, wq, wqr, wk, wv, tables, tt):
    nb, t, pw = p_mla.shape
    rope = tables is not None
    hw = H_MLA * HEAD_PAD
    head_spec = pl.BlockSpec((1, H_MLA, tt, HEAD_PAD), lambda b, i: (b, 0, i, 0))
    head_shape = jax.ShapeDtypeStruct((nb, H_MLA, t, HEAD_PAD), BF16)
    in_specs = [pl.BlockSpec((1, tt, pw), lambda b, i: (b, i, 0)), _full(gq.shape), _full(gkv.shape), _full(wq.shape)]
    args = [p_mla, gq, gkv, wq]
    if rope:
        in_specs.append(_full(wqr.shape))
        args.append(wqr)
    in_specs += [_full(wk.shape), _full(wv.shape)]
    args += [wk, wv]
    out_specs = [head_spec, head_spec, head_spec]
    out_shape = [head_shape, head_shape, head_shape]
    if rope:
        in_specs += [pl.BlockSpec((tt, HEAD_PAD), lambda b, i: (i, 0))] * 4
        args += list(tables)
    else:
        out_specs.append(pl.BlockSpec((1, tt, KV_LORA + ROPE_DIM), lambda b, i: (b, i, 0)))
        out_shape.append(jax.ShapeDtypeStruct((nb, t, KV_LORA + ROPE_DIM), F32))
    del hw
    return pl.pallas_call(
        functools.partial(_mla_prep_body, rope),
        grid=(nb, t // tt),
        in_specs=in_specs,
        out_specs=out_specs,
        out_shape=out_shape,
        compiler_params=_cparams("parallel", "parallel"),
        name="mla_prep_rope" if rope else "mla_prep",
    )(*args)


def _ctx_kv_body(lat_ref, kr_ref, wk_ref, wv_ref, k_ref, v_ref):
    latb = lat_ref[0].astype(BF16)
    ka = _dg(latb, wk_ref[...], _NN)
    va = _dg(latb, wv_ref[...], _NN)
    kr = kr_ref[0]
    for h in range(H_MLA):
        sl = slice(h * HEAD_PAD, (h + 1) * HEAD_PAD)
        k_ref[0, h] = (ka[:, sl] + kr).astype(BF16)
        v_ref[0, h] = va[:, sl].astype(BF16)


def _ctx_kv(lat, kr_placed, wk, wv):
    nb, s, _ = lat.shape
    head_spec = pl.BlockSpec((1, H_MLA, s, HEAD_PAD), lambda b: (b, 0, 0, 0))
    head_shape = jax.ShapeDtypeStruct((nb, H_MLA, s, HEAD_PAD), BF16)
    return pl.pallas_call(
        _ctx_kv_body,
        grid=(nb,),
        in_specs=[
            pl.BlockSpec((1, s, KV_LORA), lambda b: (b, 0, 0)),
            pl.BlockSpec((1, s, HEAD_PAD), lambda b: (b, 0, 0)),
            _full(wk.shape),
            _full(wv.shape),
        ],
        out_specs=[head_spec, head_spec],
        out_shape=[head_shape, head_shape],
        compiler_params=_cparams("parallel"),
        name="ctx_kv",
    )(lat, kr_placed, wk, wv)


def _attn_body(has_ctx, *refs):
    if has_ctx:
        q_ref, k_ref, v_ref, kc_ref, vc_ref, o_ref = refs
    else:
        q_ref, k_ref, v_ref, o_ref = refs
    q = q_ref[0, 0]
    s = _dg(q, k_ref[0, 0], _NT)
    m = jnp.max(s, axis=-1, keepdims=True)
    if has_ctx:
        sc = _dg(q, kc_ref[0, 0], _NT)
        m = jnp.maximum(m, jnp.max(sc, axis=-1, keepdims=True))
    p = jnp.exp(s - m)
    l = jnp.sum(p, axis=-1, keepdims=True)
    o = _dg(p.astype(BF16), v_ref[0, 0], _NN)
    if has_ctx:
        pc = jnp.exp(sc - m)
        l = l + jnp.sum(pc, axis=-1, keepdims=True)
        o = o + _dg(pc.astype(BF16), vc_ref[0, 0], _NN)
    o_ref[0] = o / l


def _attention(q, k, v, kc, vc, tq):
    nb, nh, t, hd = q.shape
    s = k.shape[2]
    has_ctx = kc is not None
    in_specs = [
        pl.BlockSpec((1, 1, tq, hd), lambda b, h, i: (b, h, i, 0)),
        pl.BlockSpec((1, 1, s, hd), lambda b, h, i: (b, h, 0, 0)),
        pl.BlockSpec((1, 1, s, V_DIM), lambda b, h, i: (b, h, 0, 0)),
    ]
    args = [q, k, v]
    if has_ctx:
        sc = kc.shape[2]
        in_specs += [
            pl.BlockSpec((1, 1, sc, hd), lambda b, h, i: (b, h, 0, 0)),
            pl.BlockSpec((1, 1, sc, V_DIM), lambda b, h, i: (b, h, 0, 0)),
        ]
        args += [kc, vc]
    return pl.pallas_call(
        functools.partial(_attn_body, has_ctx),
        grid=(nb, nh, t // tq),
        in_specs=in_specs,
        out_specs=pl.BlockSpec((1, tq, V_DIM), lambda b, h, i: (b, i, h)),
        out_shape=jax.ShapeDtypeStruct((nb, t, nh * V_DIM), F32),
        compiler_params=_cparams("parallel", "parallel", "parallel"),
        name="attention",
    )(*args)


def _rwkv_prep_body(p_ref, hp_ref, hn_ref, mu_ref, kk_ref, ka_ref, rk_ref, w0_ref, w2_ref, a0_ref, a2_ref, g2_ref,
                    r_out, v_out, kk_out, lw_out, a_out, kd_out, g_out, bonus_out):
    p = p_ref[0]
    tt = p.shape[0]
    row = lax.broadcasted_iota(I32, (tt, 1), 0)
    prev = jnp.where(row == 0, hp_ref[0, 0], pltpu.roll(p, 1, 0))
    nxt = jnp.where(row == tt - 1, hn_ref[0, 0], pltpu.roll(p, tt - 1, 0))
    ps = p + (0.5 * (prev + nxt) - p) * mu_ref[...]
    r = ps[:, :D_RWKV]
    k = ps[:, D_RWKV:2 * D_RWKV]
    v = ps[:, 2 * D_RWKV:3 * D_RWKV]
    wl = ps[:, 3 * D_RWKV:3 * D_RWKV + 2 * W_LORA]
    al = ps[:, 3 * D_RWKV + 2 * W_LORA:3 * D_RWKV + 2 * W_LORA + 2 * A_LORA]
    gl = ps[:, 3 * D_RWKV + 2 * W_LORA + 2 * A_LORA:]
    ones = _head_ones(D_RWKV, N_RWKV)
    kk = k * kk_ref[...]
    kk = kk / jnp.maximum(jnp.sqrt(_dot_x(kk * kk, ones)), 1e-12)
    wlin = w0_ref[...] + _dot3(jnp.tanh(wl), w2_ref[...])
    lw = -jnp.exp(-_softplus(-wlin) - 0.5)
    ag = _sigmoid(a0_ref[...] + _dot3(al, a2_ref[...]))
    bonus = jnp.zeros_like(v)
    for d in range(2):
        sl = slice(d * D_RWKV, (d + 1) * D_RWKV)
        kd = k * (1.0 + (ag[:, sl] - 1.0) * ka_ref[...])
        kd_out[0, :, sl] = kd
        bonus = bonus + _dot_x(r * kd * rk_ref[...], ones) * v
    r_out[0] = r
    v_out[0] = v
    kk_out[0] = kk
    lw_out[0] = lw
    a_out[0] = ag
    g_out[0] = _dot1(_sigmoid(gl), g2_ref[...])
    bonus_out[0] = bonus


def _halo_rows(x, tt, width):
    nb, t, c = x.shape
    nt = t // tt
    xt = x.reshape(nb, nt, tt, c)
    zero = jnp.zeros((nb, 1, width, c), x.dtype)
    prev = jnp.concatenate([zero, xt[:, :-1, tt - width:]], axis=1)
    nxt = jnp.concatenate([xt[:, 1:, :width], zero], axis=1)
    return prev, nxt


def _rwkv_prep(p_rwkv, lp, tt):
    nb, t, pw = p_rwkv.shape
    hp, hn = _halo_rows(p_rwkv, tt, 1)
    tok = lambda n: pl.BlockSpec((1, tt, n), lambda b, i: (b, i, 0))
    halo = pl.BlockSpec((1, 1, 1, pw), lambda b, i: (b, i, 0, 0))
    small = [lp['mu_rwkv'], lp['k_k'], lp['k_a'], lp['r_k'], lp['w0_cat'], lp['w2_bd'], lp['a0_cat'], lp['a2_bd'],
             lp['g2_rwkv']]
    widths = [D_RWKV, D_RWKV, D_RWKV, 2 * D_RWKV, 2 * D_RWKV, 2 * D_RWKV, D_RWKV, D_RWKV]
    return pl.pallas_call(
        _rwkv_prep_body,
        grid=(nb, t // tt),
        in_specs=[tok(pw), halo, halo] + [_full(a.shape) for a in small],
        out_specs=[tok(n) for n in widths],
        out_shape=[jax.ShapeDtypeStruct((nb, t, n), F32) for n in widths],
        compiler_params=_cparams("parallel", "parallel"),
        name="rwkv_prep",
    )(p_rwkv, hp, hn, *small)


def _rwkv_chunk(reverse, r, v, kk, lw, ag, kd, s_old):
    cl = r.shape[0]
    hw = D_RWKV
    nh = H_RWKV * cl

    def later(rows, cols, strict):
        t = lax.broadcasted_iota(I32, (rows, cols), 0)
        s = lax.broadcasted_iota(I32, (rows, cols), 1) % cl
        dlt = (s - t) if reverse else (t - s)
        return (dlt > 0) if strict else (dlt >= 0)

    cum = _x_dot(later(cl, cl, False).astype(F32), lw)
    yield
    tot = jnp.sum(lw, axis=0, keepdims=True)
    e_neg = jnp.exp(-cum)
    e_rem = jnp.exp(tot - cum)
    at = -kk * jnp.exp(cum - lw)
    rt = r * jnp.exp(cum)
    b0 = kk * ag
    bt = b0 * e_neg
    kt = kd * e_neg
    bh = b0 * e_rem
    kh = kd * e_rem
    w_c = jnp.exp(tot)

    lane_head = lax.broadcasted_iota(I32, (1, hw), 1) // N_RWKV
    hms = [(lane_head == h).astype(F32) for h in range(H_RWKV)]
    stack = lambda z: jnp.concatenate([z * hm for hm in hms], axis=0)
    bks = jnp.concatenate([stack(bt), stack(kt)], axis=0)
    vs = stack(v)

    ls = _dot3(at, bks, _NT)
    ly = _dot1(rt, bks, _NT)
    yield
    strict = later(cl, nh, True)
    incl = later(cl, nh, False)
    l_ab = jnp.where(strict, ls[:, :nh], 0.0)
    l_ak = jnp.where(strict, ls[:, nh:], 0.0)
    l_rb = jnp.where(incl, ly[:, :nh], 0.0)
    l_rk = jnp.where(incl, ly[:, nh:], 0.0)

    x = jnp.concatenate([l_ab] * H_RWKV, axis=0) * _head_ones(nh, cl)
    eye = (lax.broadcasted_iota(I32, (nh, nh), 0) == lax.broadcasted_iota(I32, (nh, nh), 1)).astype(F32)
    inv = eye + x
    span = 2
    while span < cl:
        x = _dot3(x, x)
        yield
        inv = inv + _dot3(inv, x)
        yield
        span *= 2

    rhs = _dot3(at, s_old, _NT) + _dot3(l_ak, vs)
    yield
    us = _dot3(inv, stack(rhs))
    yield
    um = us[:cl]
    for h in range(1, H_RWKV):
        um = um + us[h * cl:(h + 1) * cl]
    y = _dot1(rt, s_old, _NT) + _dot1(jnp.concatenate([l_rb, l_rk], axis=1), jnp.concatenate([us, vs], axis=0))
    yield
    uv_t = jnp.concatenate([um, v], axis=0).T
    s_new = s_old * w_c + _dot3(uv_t, jnp.concatenate([bh, kh], axis=0)) * _head_ones(hw, N_RWKV)
    yield y, s_new


def _lockstep(gens):
    last = [None] * len(gens)
    live = list(range(len(gens)))
    while live:
        for i in list(live):
            try:
                last[i] = next(gens[i])
            except StopIteration:
                live.remove(i)
    return last


def _rwkv_scan_body(rf_ref, vf_ref, kkf_ref, lwf_ref, af_ref, kdf_ref, rb_ref, vb_ref, kkb_ref, lwb_ref, ab_ref,
                    kdb_ref, s0_ref, yf_ref, yb_ref, sf_ref, s_scr):
    c = pl.program_id(1)
    nc = pl.num_programs(1)

    @pl.when(c == 0)
    def _():
        s_scr[...] = s0_ref[...]

    nrow = rf_ref.shape[0]
    chains = []
    for i in range(nrow):
        chains.append(_rwkv_chunk(False, rf_ref[i], vf_ref[i], kkf_ref[i], lwf_ref[i], af_ref[i], kdf_ref[i],
                                  s_scr[i, 0]))
        chains.append(_rwkv_chunk(True, rb_ref[i], vb_ref[i], kkb_ref[i], lwb_ref[i], ab_ref[i], kdb_ref[i],
                                  s_scr[i, 1]))
    done = _lockstep(chains)
    for i in range(nrow):
        (y_f, s_f), (y_b, s_b) = done[2 * i], done[2 * i + 1]
        yf_ref[i] = y_f
        yb_ref[i] = y_b
        s_scr[i, 0] = s_f
        s_scr[i, 1] = s_b

    @pl.when(c == nc - 1)
    def _():
        sf_ref[...] = s_scr[...]


def _rwkv_scan(r, v, kk, lw, ag, kd, s0):
    nb, t, hw = r.shape
    cl = RWKV_CHUNK
    nr = RWKV_ROWS
    nc = t // cl
    fwd = lambda lane: pl.BlockSpec((nr, cl, hw), lambda b, c: (b, c, lane))
    bwd = lambda lane: pl.BlockSpec((nr, cl, hw), lambda b, c: (b, nc - 1 - c, lane))
    state = pl.BlockSpec((nr, 2, hw, hw), lambda b, c: (b, 0, 0, 0))
    y_shape = jax.ShapeDtypeStruct((nb, t, hw), F32)
    return pl.pallas_call(
        _rwkv_scan_body,
        grid=(nb // nr, nc),
        in_specs=[fwd(0)] * 6 + [bwd(0)] * 3 + [bwd(1)] * 3 + [state],
        out_specs=[fwd(0), bwd(0), state],
        out_shape=[y_shape, y_shape, jax.ShapeDtypeStruct((nb, 2, hw, hw), F32)],
        scratch_shapes=[pltpu.VMEM((nr, 2, hw, hw), F32)],
        compiler_params=_cparams("parallel", "arbitrary"),
        name="rwkv_scan",
    )(r, v, kk, lw, ag, kd, r, v, kk, lw, ag, kd, s0)


def _lru_body(reverse, p_ref, halo_ref, cw_ref, cb_ref, wa_ref, ba_ref, wx_ref, bx_ref, lam_ref, h0_ref,
              h_out, fin_out, a_scr, u_scr, carry_scr):
    i = pl.program_id(1)
    nt = pl.num_programs(1)
    tt = p_ref.shape[1]
    n_ext = tt + SUBLANES

    @pl.when(i == 0)
    def _():
        carry_scr[...] = jnp.broadcast_to(h0_ref[0], (SUBLANES, D_LRU))

    xb = p_ref[0, :, :D_LRU]
    halo = halo_ref[0, 0]
    xc = cb_ref[...] + jnp.zeros_like(xb)
    if reverse:
        xe = jnp.concatenate([xb, halo], axis=0)
        for k in range(CONV_W):
            sh = xb if k == 0 else pltpu.roll(xe, n_ext - k, 0)[:tt]
            xc = xc + cw_ref[k:k + 1, :] * sh
    else:
        xe = jnp.concatenate([halo, xb], axis=0)
        for k in range(CONV_W):
            s = CONV_W - 1 - k
            sh = xb if s == 0 else pltpu.roll(xe, s, 0)[SUBLANES:]
            xc = xc + cw_ref[k:k + 1, :] * sh
    gate_r = _sigmoid(_dot3(xc, wa_ref[...]) + ba_ref[...])
    gate_i = _sigmoid(_dot3(xc, wx_ref[...]) + bx_ref[...])
    log_a = -LRU_C * gate_r * _softplus(-lam_ref[...])
    a = jnp.exp(log_a)
    u = jnp.sqrt(-_expm1(2.0 * log_a)) * (gate_i * xc)

    rowm = lax.broadcasted_iota(I32, (tt, 1), 0) % SUBLANES
    for s in (1, 2, 4):
        if reverse:
            a_s = pltpu.roll(a, tt - s, 0)
            u_s = pltpu.roll(u, tt - s, 0)
            m = rowm < SUBLANES - s
        else:
            a_s = pltpu.roll(a, s, 0)
            u_s = pltpu.roll(u, s, 0)
            m = rowm >= s
        u = jnp.where(m, a * u_s + u, u)
        a = jnp.where(m, a * a_s, a)
    a_scr[...] = a
    u_scr[...] = u
    ng = tt // SUBLANES

    def group(g, carry):
        gi = (ng - 1 - g) if reverse else g
        r0 = pl.multiple_of(gi * SUBLANES, SUBLANES)
        hg = a_scr[pl.ds(r0, SUBLANES), :] * carry + u_scr[pl.ds(r0, SUBLANES), :]
        h_out[0, pl.ds(r0, SUBLANES), :] = hg
        edge = hg[0:1, :] if reverse else hg[SUBLANES - 1:SUBLANES, :]
        return jnp.broadcast_to(edge, (SUBLANES, D_LRU))

    carry = lax.fori_loop(0, ng, group, carry_scr[...])
    carry_scr[...] = carry

    @pl.when(i == nt - 1)
    def _():
        fin_out[0] = carry[0:1, :]


def _lru_dir(p_lru, halo, lp, d, h0, tt):
    nb, t, pw = p_lru.shape
    nt = t // tt
    reverse = d == 1
    tidx = (lambda i: nt - 1 - i) if reverse else (lambda i: i)
    small = [lp['conv_w'][d], lp['conv_b'][d:d + 1], lp['wa_bd'][d], lp['b_lru_a'][d:d + 1], lp['wx_bd'][d],
             lp['b_lru_x'][d:d + 1], lp['lam'][d:d + 1]]
    return pl.pallas_call(
        functools.partial(_lru_body, reverse),
        grid=(nb, nt),
        in_specs=[
            pl.BlockSpec((1, tt, pw), lambda b, i: (b, tidx(i), 0)),
            pl.BlockSpec((1, 1, SUBLANES, D_LRU), lambda b, i: (b, tidx(i), 0, 0)),
        ] + [_full(a.shape) for a in small] + [pl.BlockSpec((1, 1, D_LRU), lambda b, i: (b, 0, 0))],
        out_specs=[
            pl.BlockSpec((1, tt, D_LRU), lambda b, i: (b, tidx(i), 0)),
            pl.BlockSpec((1, 1, D_LRU), lambda b, i: (b, 0, 0)),
        ],
        out_shape=[jax.ShapeDtypeStruct((nb, t, D_LRU), F32), jax.ShapeDtypeStruct((nb, 1, D_LRU), F32)],
        scratch_shapes=[pltpu.VMEM((tt, D_LRU), F32), pltpu.VMEM((tt, D_LRU), F32), pltpu.VMEM((SUBLANES, D_LRU), F32)],
        compiler_params=_cparams("parallel", "arbitrary"),
        name="lru_bwd" if reverse else "lru_fwd",
    )(p_lru, halo, *small, h0)


def _gelu_tanh(x):
    return 0.5 * x * (1.0 + jnp.tanh(np.sqrt(2.0 / np.pi).astype(np.float32) * (x + 0.044715 * (x * x * x))))


def _out_proj_body(x_ref, o_ref, y0_ref, y1_ref, bonus_ref, g_ref, h0_ref, h1_ref, p_ref,
                   gmla_ref, lnw_ref, lnb_ref, glru_ref, wo_ref, gpost_ref, gate_ref, gpre_ref, sc_ref, sh_ref,
                   x1_out, h2_out):
    o_mla = _rms(o_ref[0], gmla_ref[...])
    ones = _head_ones(D_RWKV, N_RWKV)
    y = y0_ref[0] + y1_ref[0]
    mu = _dot_x(y, ones) * (1.0 / N_RWKV)
    yc = y - mu
    var = _dot_x(yc * yc, ones) * (1.0 / N_RWKV)
    y = yc * lax.rsqrt(var + GN_EPS) * lnw_ref[...] + lnb_ref[...]
    o_rwkv = (y + bonus_ref[0]) * g_ref[0]
    o_lru = _rms((h0_ref[0] + h1_ref[0]) * _gelu_tanh(p_ref[0, :, D_LRU:]), glru_ref[...])
    m = (_dg(o_mla.astype(BF16), wo_ref[:D_MLA_OUT, :], _NN)
         + _dg(o_rwkv.astype(BF16), wo_ref[D_MLA_OUT:D_MLA_OUT + D_RWKV, :], _NN)
         + _dg(o_lru.astype(BF16), wo_ref[D_MLA_OUT + D_RWKV:, :], _NN))
    x1 = x_ref[0] + gate_ref[0] * _rms(m, gpost_ref[...])
    x1_out[0] = x1
    h2_out[0] = _rms(x1, gpre_ref[...]) * (1.0 + sc_ref[0]) + sh_ref[0]


def _out_proj(x, o, y_f, y_b, bonus, g, h0, h1, p_lru, lp, mod, tt):
    nb, t, d = x.shape
    per_batch = mod['g1'].shape[0] > 1
    tok = lambda n: pl.BlockSpec((1, tt, n), lambda b, i: (b, i, 0))
    small = [lp['g_mla_out'], lp['ln_x_w'], lp['ln_x_b'], lp['g_lru_out'], lp['w_out'], lp['g_post_mix']]
    return pl.pallas_call(
        _out_proj_body,
        grid=(nb, t // tt),
        in_specs=[tok(d), tok(D_MLA_OUT), tok(D_RWKV), tok(D_RWKV), tok(D_RWKV), tok(D_RWKV), tok(D_LRU), tok(D_LRU),
                  tok(P_LRU)] + [_full(a.shape) for a in small]
                 + [_mod_spec(per_batch), _full(lp['g_pre_ffn'].shape), _mod_spec(per_batch), _mod_spec(per_batch)],
        out_specs=[tok(d), tok(d)],
        out_shape=[jax.ShapeDtypeStruct((nb, t, d), F32)] * 2,
        compiler_params=_cparams("parallel", "parallel"),
        name="out_proj",
    )(x, o, y_f, y_b, bonus, g, h0, h1, p_lru, *small, mod['g1'], lp['g_pre_ffn'], mod['sc2'], mod['sh2'])


def _first_index(mask, idx, sentinel):
    return jnp.min(jnp.where(mask, idx, sentinel), axis=0, keepdims=True)


def _router_body(h_ref, wr_ref, br_ref, e_out, pos_out, w_out, cnt_out, carry):
    i = pl.program_id(0)
    tn = h_ref.shape[0]

    @pl.when(i == 0)
    def _():
        carry[...] = jnp.zeros_like(carry)

    logits = lax.dot_general(wr_ref[...], h_ref[...], _NT, precision=lax.Precision.HIGHEST,
                             preferred_element_type=F32)
    scores = _sigmoid(logits)
    sel = scores + br_ref[...]

    i8 = lax.broadcasted_iota(I32, (GROUP_SIZE, tn), 0)
    blocks, gscore = [], []
    for g in range(N_GROUPS):
        blk = sel[g * GROUP_SIZE:(g + 1) * GROUP_SIZE, :]
        m1 = jnp.max(blk, axis=0, keepdims=True)
        f1 = _first_index(blk == m1, i8, GROUP_SIZE)
        m2 = jnp.max(jnp.where(i8 == f1, -jnp.inf, blk), axis=0, keepdims=True)
        blocks.append(blk)
        gscore.append(m1 + m2)
    masked = []
    for g in range(N_GROUPS):
        beaten = jnp.zeros((1, tn), I32)
        for o in range(N_GROUPS):
            if o == g:
                continue
            wins = (gscore[o] >= gscore[g]) if o < g else (gscore[o] > gscore[g])
            beaten = beaten + wins.astype(I32)
        masked.append(jnp.where(beaten < TOPK_GROUPS, blocks[g], -jnp.inf))
    cur = jnp.concatenate(masked, axis=0)

    ie = lax.broadcasted_iota(I32, (N_EXPERTS, tn), 0)
    firsts, raw = [], []
    chosen = jnp.zeros((N_EXPERTS, tn), F32)
    for _ in range(TOP_K):
        mx = jnp.max(cur, axis=0, keepdims=True)
        first = _first_index(cur == mx, ie, N_EXPERTS)
        hit = ie == first
        firsts.append(first)
        raw.append(jnp.sum(jnp.where(hit, scores, 0.0), axis=0, keepdims=True))
        cur = jnp.where(hit, -jnp.inf, cur)
        chosen = chosen + hit.astype(F32)
    total = raw[0]
    for k in range(1, TOP_K):
        total = total + raw[k]

    before = (lax.broadcasted_iota(I32, (tn, tn), 0) < lax.broadcasted_iota(I32, (tn, tn), 1)).astype(BF16)
    rank = _dg(chosen.astype(BF16), before, _NN) + carry[:, 0:1]
    e_out[...] = jnp.zeros_like(e_out)
    pos_out[...] = jnp.zeros_like(pos_out)
    w_out[...] = jnp.zeros_like(w_out)
    for k in range(TOP_K):
        e_out[0, k:k + 1, :] = firsts[k]
        pos_out[0, k:k + 1, :] = jnp.sum(jnp.where(ie == firsts[k], rank, 0.0), axis=0, keepdims=True).astype(I32)
        w_out[k:k + 1, :] = raw[k] / total * ROUTED_SCALE
    new_carry = carry[...] + jnp.sum(chosen, axis=1, keepdims=True)
    carry[...] = new_carry
    cnt_out[...] = new_carry


def _router(h2, wr_t, b_col):
    n, d = h2.shape
    tn = ROUTER_TILE
    row_spec = pl.BlockSpec((SLOT_ROWS, tn), lambda i: (0, i))
    tile_spec = pl.BlockSpec((1, SLOT_ROWS, tn), lambda i: (i, 0, 0))
    cnt_spec = pl.BlockSpec((N_EXPERTS, 128), lambda i: (0, 0))
    tile_shape = jax.ShapeDtypeStruct((n // tn, SLOT_ROWS, tn), I32)
    return pl.pallas_call(
        _router_body,
        grid=(n // tn,),
        in_specs=[pl.BlockSpec((tn, d), lambda i: (i, 0)), _full(wr_t.shape), _full(b_col.shape)],
        out_specs=[tile_spec, tile_spec, row_spec, cnt_spec],
        out_shape=[tile_shape, tile_shape,
                   jax.ShapeDtypeStruct((SLOT_ROWS, n), F32), jax.ShapeDtypeStruct((N_EXPERTS, 128), F32)],
        scratch_shapes=[pltpu.VMEM((N_EXPERTS, 128), F32)],
        compiler_params=_cparams("arbitrary"),
        name="router",
    )(h2, wr_t, b_col)


def _row_copy(src, src_row, dst, dst_row, sem):
    return pltpu.make_async_copy(src.at[pl.ds(src_row, 1), :], dst.at[pl.ds(dst_row, 1), :], sem)


def _load_slots(slot_ref, slot_smem, sem):
    cp = pltpu.make_async_copy(slot_ref.at[0], slot_smem, sem)
    cp.start()
    cp.wait()


def _dispatch_body(start_ref, e_ref, pos_ref, h_ref, xs_in, slots_out, xs_out, slot_smem, sem_s, sem):
    del xs_in
    tn = slot_smem.shape[1]
    e = e_ref[0]
    slot = pos_ref[0]
    for x in range(N_EXPERTS):
        slot = slot + jnp.where(e == x, start_ref[x], 0)
    slots_out[0] = slot
    _load_slots(slots_out, slot_smem, sem_s)

    def issue(j, carry):
        for k in range(TOP_K):
            _row_copy(h_ref, j, xs_out, slot_smem[k, j], sem).start()
        return carry

    def drain(j, carry):
        for k in range(TOP_K):
            _row_copy(h_ref, 0, xs_out, 0, sem).wait()
        return carry

    lax.fori_loop(0, tn, issue, 0)
    lax.fori_loop(0, tn, drain, 0)


def _dispatch(row_start, e3, pos3, h2, xs_zero, tn):
    n, d = h2.shape
    ratio = e3.shape[2] // tn
    tile = lambda: pl.BlockSpec((1, SLOT_ROWS, tn), lambda i, st: (i // ratio, 0, i % ratio))
    grid_spec = pltpu.PrefetchScalarGridSpec(
        num_scalar_prefetch=1,
        grid=(n // tn,),
        in_specs=[tile(), tile(), pl.BlockSpec((tn, d), lambda i, st: (i, 0)), pl.BlockSpec(memory_space=pl.ANY)],
        out_specs=[pl.BlockSpec((1, SLOT_ROWS, tn), lambda i, st: (i, 0, 0)), pl.BlockSpec(memory_space=pl.ANY)],
        scratch_shapes=[pltpu.SMEM((SLOT_ROWS, tn), I32), pltpu.SemaphoreType.DMA, pltpu.SemaphoreType.DMA],
    )
    return pl.pallas_call(
        _dispatch_body,
        grid_spec=grid_spec,
        out_shape=[jax.ShapeDtypeStruct((n // tn, SLOT_ROWS, tn), I32),
                   jax.ShapeDtypeStruct(xs_zero.shape, xs_zero.dtype)],
        input_output_aliases={4: 1},
        compiler_params=_cparams("arbitrary"),
        name="dispatch",
    )(row_start, e3, pos3, h2, xs_zero)


def _swiglu(x, wg, wu, wd):
    g = _dg(x, wg, _NN)
    u = _dg(x, wu, _NN)
    return _dg((g * _sigmoid(g) * u).astype(BF16), wd, _NN)


def _expert_body(be_ref, nu_ref, xs_ref, wg_ref, wu_ref, wd_ref, ys_ref):
    del be_ref
    b = pl.program_id(0)

    @pl.when(b < nu_ref[0])
    def _():
        ys_ref[...] = _swiglu(xs_ref[...].astype(BF16), wg_ref[0], wu_ref[0], wd_ref[0])

    @pl.when(b >= nu_ref[0])
    def _():
        ys_ref[...] = jnp.zeros_like(ys_ref)


def _experts(block_e, n_used, xs, wg, wu, wd):
    n_slots, d = xs.shape
    bm = MOE_BLOCK
    de = wg.shape[2]
    grid_spec = pltpu.PrefetchScalarGridSpec(
        num_scalar_prefetch=2,
        grid=(n_slots // bm,),
        in_specs=[
            pl.BlockSpec((bm, d), lambda b, be, nu: (b, 0)),
            pl.BlockSpec((1, d, de), lambda b, be, nu: (be[b], 0, 0)),
            pl.BlockSpec((1, d, de), lambda b, be, nu: (be[b], 0, 0)),
            pl.BlockSpec((1, de, d), lambda b, be, nu: (be[b], 0, 0)),
        ],
        out_specs=pl.BlockSpec((bm, d), lambda b, be, nu: (b, 0)),
    )
    return pl.pallas_call(
        _expert_body,
        grid_spec=grid_spec,
        out_shape=jax.ShapeDtypeStruct((n_slots, d), F32),
        compiler_params=_cparams("arbitrary"),
        name="experts",
    )(block_e, n_used, xs, wg, wu, wd)


def _combine_body(slot_ref, w_ref, x1_ref, h2_ref, gate_ref, ys_hbm, wsg_ref, wsu_ref, wsd_ref, gpost_ref,
                  out_ref, slot_smem, buf, sem_s, sem):
    tn = slot_smem.shape[1]
    _load_slots(slot_ref, slot_smem, sem_s)

    def issue(j, carry):
        for k in range(TOP_K):
            _row_copy(ys_hbm, slot_smem[k, j], buf.at[k], j, sem).start()
        return carry

    def drain(j, carry):
        for k in range(TOP_K):
            _row_copy(ys_hbm, 0, buf.at[k], 0, sem).wait()
        return carry

    lax.fori_loop(0, tn, issue, 0)
    acc = _swiglu(h2_ref[...].astype(BF16), wsg_ref[...], wsu_ref[...], wsd_ref[...])
    lax.fori_loop(0, tn, drain, 0)
    for k in range(TOP_K):
        acc = acc + buf[k] * w_ref[:, k:k + 1]
    out_ref[...] = x1_ref[...] + gate_ref[0] * _rms(acc, gpost_ref[...])


def _combine(slots3, w_tok, x1, h2, gate_rows, ys, wsg, wsu, wsd, gpost):
    ntiles, _, tn = slots3.shape
    n, d = x1.shape
    tok = pl.BlockSpec((tn, d), lambda i: (i, 0))
    return pl.pallas_call(
        _combine_body,
        grid=(ntiles,),
        in_specs=[pl.BlockSpec((1, SLOT_ROWS, tn), lambda i: (i, 0, 0)),
                  pl.BlockSpec((tn, SLOT_ROWS), lambda i: (i, 0)), tok, tok,
                  pl.BlockSpec((1, 1, d), lambda i: (i, 0, 0)),
                  pl.BlockSpec(memory_space=pl.ANY),
                  _full(wsg.shape), _full(wsu.shape), _full(wsd.shape), _full(gpost.shape)],
        out_specs=tok,
        out_shape=jax.ShapeDtypeStruct((n, d), F32),
        scratch_shapes=[pltpu.SMEM((SLOT_ROWS, tn), I32), pltpu.VMEM((TOP_K, tn, d), F32),
                        pltpu.SemaphoreType.DMA, pltpu.SemaphoreType.DMA],
        compiler_params=_cparams("arbitrary"),
        name="combine",
    )(slots3, w_tok, x1, h2, gate_rows, ys, wsg, wsu, wsd, gpost)


def _moe(x1, h2, gate_rows, lp):
    n, d = h2.shape
    bm = MOE_BLOCK
    tn = TOKEN_TILE
    e3, pos3, w_rows, cnt = _router(h2, lp['wr_t'], lp['b_router_col'])
    counts = cnt[:, 0].astype(I32)
    blocks_per_e = (counts + bm - 1) // bm
    blk_end = jnp.cumsum(blocks_per_e)
    blk_start = blk_end - blocks_per_e
    n_blocks = -(-(n * TOP_K) // bm) + N_EXPERTS
    block_e = jnp.minimum(jnp.sum(blk_end[None, :] <= jnp.arange(n_blocks, dtype=I32)[:, None], axis=1),
                          N_EXPERTS - 1).astype(I32)
    n_used = blk_end[-1:].astype(I32)
    slots3, xs = _dispatch((blk_start * bm).astype(I32), e3, pos3, h2, jnp.zeros((n_blocks * bm, d), F32), tn)
    ys = _experts(block_e, n_used, xs, lp['w_e_gate'], lp['w_e_up'], lp['w_e_down'])
    return _combine(slots3, w_rows.T, x1, h2, gate_rows, ys, lp['w_s_gate'], lp['w_s_up'], lp['w_s_down'],
                    lp['g_post_ffn'])


def _rope_tables(n_tokens):
    rows = n_tokens // GRID_W
    row = jnp.repeat(jnp.arange(rows, dtype=F32), GRID_W)
    col = jnp.tile(jnp.arange(GRID_W, dtype=F32), rows)
    n_freq = ROPE_DIM // 4
    inv_freq = ROPE_THETA ** (-jnp.arange(n_freq, dtype=F32) / n_freq)
    ang_r = row[:, None] * inv_freq
    ang_c = col[:, None] * inv_freq
    ang = jnp.concatenate([ang_r, ang_r, ang_c, ang_c], axis=-1)
    cos, sin = jnp.cos(ang), jnp.sin(ang)
    pad = lambda z, fill: jnp.concatenate(
        [jnp.full((n_tokens, NOPE_DIM), fill, F32), z, jnp.zeros((n_tokens, HEAD_PAD - QK_DIM), F32)], axis=1)
    return pad(cos, 1.0), pad(sin, 0.0), pad(cos, 0.0), pad(sin, 0.0)


def _rot_cols(w):
    q = ROPE_DIM // 4
    perm = np.concatenate([np.arange(q, 2 * q), np.arange(0, q), np.arange(3 * q, 4 * q), np.arange(2 * q, 3 * q)])
    sign = np.concatenate([-np.ones(q), np.ones(q), -np.ones(q), np.ones(q)]).astype(np.float32)
    return w[..., perm] * sign


def _block_diag(blocks):
    h, m, n = blocks.shape[-3:]
    eye = jnp.eye(h, dtype=blocks.dtype)
    out = blocks[..., :, :, None, :] * eye[:, None, :, None]
    return out.reshape(blocks.shape[:-3] + (h * m, h * n))


def _diag_blocks(mat, h):
    m, n = mat.shape[-2] // h, mat.shape[-1] // h
    z = mat.reshape(mat.shape[:-2] + (h, m, h, n))
    return jnp.stack([z[..., i, :, i, :] for i in range(h)], axis=-3)


def _layer_operands(l, a):
    w_in = a['w_in'][l]
    d = w_in.shape[0]
    p_mla = Q_LORA + KV_LORA + ROPE_DIM
    w_kr = w_in[:, Q_LORA + KV_LORA:p_mla]
    z32 = jnp.zeros((d, ROPE_DIM), F32)
    w_mla = jnp.concatenate([w_in[:, :Q_LORA + KV_LORA], z32, _rot_cols(w_kr), w_kr, z32], axis=1)
    wq3 = a['w_q_b'][l].reshape(Q_LORA, H_MLA, QK_DIM)
    zq = lambda n: jnp.zeros((Q_LORA, H_MLA, n), F32)
    wq = jnp.concatenate([wq3, zq(HEAD_PAD - QK_DIM)], axis=2).reshape(Q_LORA, H_MLA * HEAD_PAD)
    wqr = jnp.concatenate([zq(NOPE_DIM), _rot_cols(wq3[:, :, NOPE_DIM:]), zq(HEAD_PAD - QK_DIM)],
                          axis=2).reshape(Q_LORA, H_MLA * HEAD_PAD)
    wkv3 = a['w_kv_b'][l].reshape(KV_LORA, H_MLA, NOPE_DIM + V_DIM)
    wk = jnp.concatenate([wkv3[:, :, :NOPE_DIM], jnp.zeros((KV_LORA, H_MLA, HEAD_PAD - NOPE_DIM), F32)],
                         axis=2).reshape(KV_LORA, H_MLA * HEAD_PAD)
    wv = wkv3[:, :, NOPE_DIM:].reshape(KV_LORA, H_MLA * V_DIM)
    row = lambda z: z.reshape(1, -1)
    two = lambda z: jnp.concatenate([z[0], z[1]], axis=-1)
    bd2 = lambda z: jnp.concatenate([jnp.concatenate([z[0], jnp.zeros_like(z[0])], axis=1),
                                     jnp.concatenate([jnp.zeros_like(z[1]), z[1]], axis=1)], axis=0)
    return {
        'g_pre_mix': row(a['g_pre_mix'][l]), 'g_post_mix': row(a['g_post_mix'][l]),
        'g_pre_ffn': row(a['g_pre_ffn'][l]), 'g_post_ffn': row(a['g_post_ffn'][l]),
        'w_mla': w_mla.astype(BF16),
        'w_rwkv': w_in[:, p_mla:p_mla + P_RWKV].astype(BF16),
        'w_lru': w_in[:, p_mla + P_RWKV:].astype(BF16),
        'g_q_a': row(a['g_q_a'][l]), 'g_kv_a': row(a['g_kv_a'][l]),
        'wq': wq.astype(BF16), 'wqr': wqr.astype(BF16), 'wk': wk.astype(BF16), 'wv': wv.astype(BF16),
        'g_mla_out': row(a['g_mla_out'][l]),
        'mu_rwkv': row(a['mu_rwkv'][l]), 'k_k': row(a['k_k'][l]), 'k_a': row(a['k_a'][l]), 'r_k': row(a['r_k'][l]),
        'w0_cat': row(two(a['w0_rwkv'][l])), 'w2_bd': bd2(a['w2_rwkv'][l]),
        'a0_cat': row(two(a['a0_rwkv'][l])), 'a2_bd': bd2(a['a2_rwkv'][l]),
        'g2_rwkv': a['g2_rwkv'][l], 'ln_x_w': row(a['ln_x_w'][l]), 'ln_x_b': row(a['ln_x_b'][l]),
        'conv_w': a['conv_w'][l], 'conv_b': a['conv_b'][l],
        'wa_bd': _block_diag(a['w_lru_a'][l]), 'b_lru_a': a['b_lru_a'][l],
        'wx_bd': _block_diag(a['w_lru_x'][l]), 'b_lru_x': a['b_lru_x'][l],
        'lam': a['lam'][l], 'g_lru_out': row(a['g_lru_out'][l]),
        'w_out': a['w_out'][l].astype(BF16),
        'wr_t': a['w_router'][l].T, 'b_router_col': a['b_router'][l].reshape(N_EXPERTS, 1),
        'w_e_gate': a['w_e_gate'][l].astype(BF16), 'w_e_up': a['w_e_up'][l].astype(BF16),
        'w_e_down': a['w_e_down'][l].astype(BF16),
        'w_s_gate': a['w_s_gate'][l].astype(BF16), 'w_s_up': a['w_s_up'][l].astype(BF16),
        'w_s_down': a['w_s_down'][l].astype(BF16),
    }


def _mixer(x, mod, lp, ctx, tables):
    nb, t, _ = x.shape
    tt = min(t, 512)
    p_mla, p_rwkv, p_lru = _in_proj(x, lp['g_pre_mix'], mod['sc1'], mod['sh1'], lp['w_mla'], lp['w_rwkv'],
                                    lp['w_lru'], tt)
    if ctx is None:
        q, k, v, new_cache = _mla_prep(p_mla, lp['g_q_a'], lp['g_kv_a'], lp['wq'], None, lp['wk'], lp['wv'], None, tt)
        kc = vc = None
        s0 = jnp.zeros((nb, 2, D_RWKV, D_RWKV), F32)
        h0 = jnp.zeros((nb, 2, 1, D_LRU), F32)
    else:
        cache, s_rwkv, s_lru = ctx
        q, k, v = _mla_prep(p_mla, lp['g_q_a'], lp['g_kv_a'], lp['wq'], lp['wqr'], lp['wk'], lp['wv'], tables, tt)
        new_cache = None
        kr_placed = jnp.pad(cache[..., KV_LORA:], ((0, 0), (0, 0), (NOPE_DIM, HEAD_PAD - QK_DIM)))
        kc, vc = _ctx_kv(cache[..., :KV_LORA], kr_placed, lp['wk'], lp['wv'])
        s0 = _block_diag(s_rwkv)
        h0 = s_lru[:, :, None, :]
    o = _attention(q, k, v, kc, vc, min(t, 256))

    r, vv, kk, lw, ag, kd, g, bonus = _rwkv_prep(p_rwkv, lp, tt)
    y_f, y_b, s_fin = _rwkv_scan(r, vv, kk, lw, ag, kd, s0)

    hp, hn = _halo_rows(p_lru[..., :D_LRU], tt, SUBLANES)
    h_f, fin_f = _lru_dir(p_lru, hp, lp, 0, h0[:, 0], tt)
    h_b, fin_b = _lru_dir(p_lru, hn, lp, 1, h0[:, 1], tt)

    x1, h2 = _out_proj(x, o, y_f, y_b, bonus, g, h_f, h_b, p_lru, lp, mod, tt)
    states = None
    if ctx is None:
        states = (new_cache, _diag_blocks(s_fin, H_RWKV), jnp.concatenate([fin_f, fin_b], axis=1))
    return x1, h2, states


def kernel(x_prompt, x_sample, c, cache_mla, state_rwkv, state_lru, c_ctx, w_ada, b_ada, g_pre_mix, g_post_mix, g_pre_ffn, g_post_ffn, w_in, g_q_a, w_q_b, g_kv_a, w_kv_b, g_mla_out, mu_rwkv, w0_rwkv, w2_rwkv, a0_rwkv, a2_rwkv, g2_rwkv, k_k, k_a, r_k, ln_x_w, ln_x_b, conv_w, conv_b, w_lru_a, b_lru_a, w_lru_x, b_lru_x, lam, g_lru_out, w_out, w_router, b_router, w_e_gate, w_e_up, w_e_down, w_s_gate, w_s_up, w_s_down):
    a = dict(w_in=w_in, g_pre_mix=g_pre_mix, g_post_mix=g_post_mix, g_pre_ffn=g_pre_ffn, g_post_ffn=g_post_ffn,
             g_q_a=g_q_a, w_q_b=w_q_b, g_kv_a=g_kv_a, w_kv_b=w_kv_b, g_mla_out=g_mla_out, mu_rwkv=mu_rwkv,
             w0_rwkv=w0_rwkv, w2_rwkv=w2_rwkv, a0_rwkv=a0_rwkv, a2_rwkv=a2_rwkv, g2_rwkv=g2_rwkv, k_k=k_k, k_a=k_a,
             r_k=r_k.reshape(r_k.shape[0], -1), ln_x_w=ln_x_w, ln_x_b=ln_x_b, conv_w=conv_w, conv_b=conv_b,
             w_lru_a=w_lru_a, b_lru_a=b_lru_a, w_lru_x=w_lru_x, b_lru_x=b_lru_x, lam=lam, g_lru_out=g_lru_out,
             w_out=w_out, w_router=w_router, b_router=b_router, w_e_gate=w_e_gate, w_e_up=w_e_up,
             w_e_down=w_e_down, w_s_gate=w_s_gate, w_s_up=w_s_up, w_s_down=w_s_down)
    n_layers = w_in.shape[0]
    nbp, tp, d = x_prompt.shape
    nbs, ts, _ = x_sample.shape
    rows = -(-(1 + nbs) // SUBLANES) * SUBLANES
    cvecs = jnp.concatenate([c_ctx[None, :], c, jnp.zeros((rows - 1 - nbs, d), F32)], axis=0)
    mods = _ada(cvecs, w_ada, b_ada)
    names = ('sh1', 'sc1', 'g1', 'sh2', 'sc2', 'g2')
    tables = _rope_tables(ts)
    tn = TOKEN_TILE

    yp, ys = x_prompt, x_sample
    caches, rwkv_states, lru_states = [], [], []
    for l in range(n_layers):
        lp = _layer_operands(l, a)
        mod_p = {nm: mods[l, 0:1, i * d:(i + 1) * d][:, None, :] for i, nm in enumerate(names)}
        mod_s = {nm: mods[l, 1:1 + nbs, i * d:(i + 1) * d][:, None, :] for i, nm in enumerate(names)}
        x1p, h2p, states = _mixer(yp, mod_p, lp, None, None)
        x1s, h2s, _ = _mixer(ys, mod_s, lp, (cache_mla[:, l], state_rwkv[:, l], state_lru[:, l]), tables)
        caches.append(states[0])
        rwkv_states.append(states[1])
        lru_states.append(states[2])
        x1 = jnp.concatenate([x1p.reshape(-1, d), x1s.reshape(-1, d)], axis=0)
        h2 = jnp.concatenate([h2p.reshape(-1, d), h2s.reshape(-1, d)], axis=0)
        gate_rows = jnp.concatenate([jnp.repeat(mod_p['g2'], nbp * tp // tn, axis=0),
                                     jnp.repeat(mod_s['g2'], ts // tn, axis=0)], axis=0)
        x2 = _moe(x1, h2, gate_rows, lp)
        yp = x2[:nbp * tp].reshape(nbp, tp, d)
        ys = x2[nbp * tp:].reshape(nbs, ts, d)
    return (yp, ys, jnp.stack(caches, axis=1), jnp.stack(rwkv_states, axis=1), jnp.stack(lru_states, axis=1))
```

```python
import functools

import numpy as np
import jax
import jax.numpy as jnp
from jax import lax
from jax.experimental import pallas as pl
from jax.experimental.pallas import tpu as pltpu

F32 = jnp.float32
BF16 = jnp.bfloat16
I32 = jnp.int32

D_MODEL = 1024
GRID_W = 64
EPS = 1e-6
H_MLA = 4
Q_LORA = 256
KV_LORA = 128
NOPE_DIM = 64
ROPE_DIM = 32
V_DIM = 128
QK_DIM = NOPE_DIM + ROPE_DIM
ROPE_THETA = 10000.0
HEAD_PAD = 128
H_RWKV = 4
N_RWKV = 64
D_RWKV = H_RWKV * N_RWKV
W_LORA = 64
A_LORA = 64
G_LORA = 128
GN_EPS = 64e-5
D_LRU = 256
H_LRU = 4
BS_LRU = D_LRU // H_LRU
CONV_W = 4
LRU_C = 8.0
D_MLA_OUT = H_MLA * V_DIM
P_MLA_PAD = Q_LORA + KV_LORA + HEAD_PAD
P_RWKV = 3 * D_RWKV + 2 * W_LORA + 2 * A_LORA + G_LORA
P_LRU = 2 * D_LRU
N_EXPERTS = 64
TOP_K = 6
N_GROUPS = 8
TOPK_GROUPS = 4
GROUP_SIZE = N_EXPERTS // N_GROUPS
D_EXPERT = 256
ROUTED_SCALE = 2.5

SUBLANES = 8
RWKV_CHUNK = 64
RWKV_ROWS = 2
MOE_BLOCK = 256
ROUTER_TILE = 512
TOKEN_TILE = 256
SLOT_ROWS = 8
VMEM_LIMIT = 56 * 1024 * 1024


def _cparams(*sem):
    return pltpu.CompilerParams(dimension_semantics=sem, vmem_limit_bytes=VMEM_LIMIT)


def _sigmoid(x):
    return 1.0 / (1.0 + jnp.exp(-x))


def _softplus(x):
    return jnp.maximum(x, 0.0) + jnp.log1p(jnp.exp(-jnp.abs(x)))


def _expm1(z):
    e = jnp.exp(z)
    direct = (e == 1.0) | (z < -1.0)
    corrected = (e - 1.0) * z / jnp.log(jnp.where(direct, 2.0, e))
    return jnp.where(e == 1.0, z, jnp.where(z < -1.0, e - 1.0, corrected))


def _rms(x, g):
    return x * lax.rsqrt(jnp.mean(x * x, axis=-1, keepdims=True) + EPS) * g


def _split2(x):
    hi = x.astype(BF16)
    lo = (x - hi.astype(F32)).astype(BF16)
    return hi, lo


def _split3(x):
    hi = x.astype(BF16)
    r1 = x - hi.astype(F32)
    mid = r1.astype(BF16)
    lo = (r1 - mid.astype(F32)).astype(BF16)
    return hi, mid, lo


_NN = (((1,), (0,)), ((), ()))
_NT = (((1,), (1,)), ((), ()))


def _dg(a, b, dims):
    return lax.dot_general(a, b, dims, preferred_element_type=F32)


def _dot1(a, b, dims=_NN):
    return _dg(a.astype(BF16), b.astype(BF16), dims)


def _dot3(a, b, dims=_NN):
    ah, al = _split2(a)
    bh, bl = _split2(b)
    return _dg(ah, bh, dims) + (_dg(ah, bl, dims) + _dg(al, bh, dims))


def _dot_x(a, b_exact, dims=_NN):
    h, m, l = _split3(a)
    bb = b_exact.astype(BF16)
    return _dg(h, bb, dims) + (_dg(m, bb, dims) + _dg(l, bb, dims))


def _x_dot(a_exact, b, dims=_NN):
    h, m, l = _split3(b)
    aa = a_exact.astype(BF16)
    return _dg(aa, h, dims) + (_dg(aa, m, dims) + _dg(aa, l, dims))


def _head_ones(n, seg):
    r = lax.broadcasted_iota(I32, (n, n), 0) // seg
    c = lax.broadcasted_iota(I32, (n, n), 1) // seg
    return (r == c).astype(F32)


def _ada_body(c_ref, w_ref, b_ref, o_ref):
    cv = c_ref[...]
    s = cv * _sigmoid(cv)
    o_ref[0] = _dot3(s, w_ref[0]) + b_ref[0]


def _ada(cvecs, w_ada, b_ada):
    n_layers, d, d6 = w_ada.shape
    rows = cvecs.shape[0]
    nt = d6 // d
    return pl.pallas_call(
        _ada_body,
        grid=(n_layers, nt),
        in_specs=[
            pl.BlockSpec((rows, d), lambda l, j: (0, 0)),
            pl.BlockSpec((1, d, d), lambda l, j: (l, 0, j)),
            pl.BlockSpec((1, 1, d), lambda l, j: (l, 0, j)),
        ],
        out_specs=pl.BlockSpec((1, rows, d), lambda l, j: (l, 0, j)),
        out_shape=jax.ShapeDtypeStruct((n_layers, rows, d6), F32),
        compiler_params=_cparams("parallel", "parallel"),
        name="ada",
    )(cvecs, w_ada, b_ada.reshape(n_layers, 1, d6))


def _mod_spec(per_batch):
    if per_batch:
        return pl.BlockSpec((1, 1, D_MODEL), lambda b, i: (b, 0, 0))
    return pl.BlockSpec((1, 1, D_MODEL), lambda b, i: (0, 0, 0))


def _full(shape):
    nd = len(shape)
    return pl.BlockSpec(shape, lambda *_: (0,) * nd)


def _in_proj_body(x_ref, g_ref, sc_ref, sh_ref, w1_ref, w2_ref, w3_ref, o1_ref, o2_ref, o3_ref):
    h = _rms(x_ref[0], g_ref[...]) * (1.0 + sc_ref[0]) + sh_ref[0]
    hb = h.astype(BF16)
    o1_ref[0] = _dg(hb, w1_ref[...], _NN)
    o2_ref[0] = _dg(hb, w2_ref[...], _NN)
    o3_ref[0] = _dg(hb, w3_ref[...], _NN)


def _in_proj(x, g, sc, sh, w_mla, w_rwkv, w_lru, tt):
    nb, t, d = x.shape
    per_batch = sc.shape[0] > 1
    outs = [w_mla.shape[1], w_rwkv.shape[1], w_lru.shape[1]]
    return pl.pallas_call(
        _in_proj_body,
        grid=(nb, t // tt),
        in_specs=[
            pl.BlockSpec((1, tt, d), lambda b, i: (b, i, 0)),
            _full((1, d)),
            _mod_spec(per_batch),
            _mod_spec(per_batch),
            _full(w_mla.shape),
            _full(w_rwkv.shape),
            _full(w_lru.shape),
        ],
        out_specs=[pl.BlockSpec((1, tt, n), lambda b, i: (b, i, 0)) for n in outs],
        out_shape=[jax.ShapeDtypeStruct((nb, t, n), F32) for n in outs],
        compiler_params=_cparams("parallel", "parallel"),
        name="in_proj",
    )(x, g, sc, sh, w_mla, w_rwkv, w_lru)


def _rope_lanes():
    lane = lax.broadcasted_iota(I32, (1, HEAD_PAD), 1)
    return ((lane >= NOPE_DIM) & (lane < QK_DIM)).astype(F32)


def _mla_prep_body(rope, *refs):
    if rope:
        (p_ref, gq_ref, gkv_ref, wq_ref, wqr_ref, wk_ref, wv_ref, cq_ref, sq_ref, cr_ref, sr_ref,
         q_ref, k_ref, v_ref) = refs
    else:
        p_ref, gq_ref, gkv_ref, wq_ref, wk_ref, wv_ref, q_ref, k_ref, v_ref, c_ref = refs
    p = p_ref[0]
    cq = _rms(p[:, :Q_LORA], gq_ref[...])
    ckv = _rms(p[:, Q_LORA:Q_LORA + KV_LORA], gkv_ref[...])
    kr = p[:, Q_LORA + KV_LORA:]
    cqb = cq.astype(BF16)
    ckvb = ckv.astype(BF16)
    scale = QK_DIM ** -0.5
    qa = _dg(cqb, wq_ref[...], _NN) * scale
    ka = _dg(ckvb, wk_ref[...], _NN)
    va = _dg(ckvb, wv_ref[...], _NN)
    if rope:
        qr = _dg(cqb, wqr_ref[...], _NN) * scale
        krp = kr * cr_ref[...] + pltpu.roll(kr, ROPE_DIM, 1) * sr_ref[...]
    else:
        krp = kr * _rope_lanes()
    for h in range(H_MLA):
        sl = slice(h * HEAD_PAD, (h + 1) * HEAD_PAD)
        qh = qa[:, sl]
        if rope:
            qh = qh * cq_ref[...] + qr[:, sl] * sq_ref[...]
        q_ref[0, h] = qh.astype(BF16)
        k_ref[0, h] = (ka[:, sl] + krp).astype(BF16)
        v_ref[0, h] = va[:, sl].astype(BF16)
    if not rope:
        c_ref[0, :, :KV_LORA] = ckv
        c_ref[0, :, KV_LORA:] = pltpu.roll(kr, HEAD_PAD - NOPE_DIM, 1)[:, :ROPE_DIM]


def _mla_prep(p_mla, gq, gkv, wq, wqr, wk, wv, tables, tt):
    nb, t, pw = p_mla.shape
    rope = tables is not None
    hw = H_MLA * HEAD_PAD
    head_spec = pl.BlockSpec((1, H_MLA, tt, HEAD_PAD), lambda b, i: (b, 0, i, 0))
    head_shape = jax.ShapeDtypeStruct((nb, H_MLA, t, HEAD_PAD), BF16)
    in_specs = [pl.BlockSpec((1, tt, pw), lambda b, i: (b, i, 0)), _full(gq.shape), _full(gkv.shape), _full(wq.shape)]
    args = [p_mla, gq, gkv, wq]
    if rope:
        in_specs.append(_full(wqr.shape))
        args.append(wqr)
    in_specs += [_full(wk.shape), _full(wv.shape)]
    args += [wk, wv]
    out_specs = [head_spec, head_spec, head_spec]
    out_shape = [head_shape, head_shape, head_shape]
    if rope:
        in_specs += [pl.BlockSpec((tt, HEAD_PAD), lambda b, i: (i, 0))] * 4
        args += list(tables)
    else:
        out_specs.append(pl.BlockSpec((1, tt, KV_LORA + ROPE_DIM), lambda b, i: (b, i, 0)))
        out_shape.append(jax.ShapeDtypeStruct((nb, t, KV_LORA + ROPE_DIM), F32))
    del hw
    return pl.pallas_call(
        functools.partial(_mla_prep_body, rope),
        grid=(nb, t // tt),
        in_specs=in_specs,
        out_specs=out_specs,
        out_shape=out_shape,
        compiler_params=_cparams("parallel", "parallel"),
        name="mla_prep_rope" if rope else "mla_prep",
    )(*args)


def _ctx_kv_body(lat_ref, kr_ref, wk_ref, wv_ref, k_ref, v_ref):
    latb = lat_ref[0].astype(BF16)
    ka = _dg(latb, wk_ref[...], _NN)
    va = _dg(latb, wv_ref[...], _NN)
    kr = kr_ref[0]
    for h in range(H_MLA):
        sl = slice(h * HEAD_PAD, (h + 1) * HEAD_PAD)
        k_ref[0, h] = (ka[:, sl] + kr).astype(BF16)
        v_ref[0, h] = va[:, sl].astype(BF16)


def _ctx_kv(lat, kr_placed, wk, wv):
    nb, s, _ = lat.shape
    head_spec = pl.BlockSpec((1, H_MLA, s, HEAD_PAD), lambda b: (b, 0, 0, 0))
    head_shape = jax.ShapeDtypeStruct((nb, H_MLA, s, HEAD_PAD), BF16)
    return pl.pallas_call(
        _ctx_kv_body,
        grid=(nb,),
        in_specs=[
            pl.BlockSpec((1, s, KV_LORA), lambda b: (b, 0, 0)),
            pl.BlockSpec((1, s, HEAD_PAD), lambda b: (b, 0, 0)),
            _full(wk.shape),
            _full(wv.shape),
        ],
        out_specs=[head_spec, head_spec],
        out_shape=[head_shape, head_shape],
        compiler_params=_cparams("parallel"),
        name="ctx_kv",
    )(lat, kr_placed, wk, wv)


def _attn_body(has_ctx, *refs):
    if has_ctx:
        q_ref, k_ref, v_ref, kc_ref, vc_ref, o_ref = refs
    else:
        q_ref, k_ref, v_ref, o_ref = refs
    q = q_ref[0, 0]
    s = _dg(q, k_ref[0, 0], _NT)
    m = jnp.max(s, axis=-1, keepdims=True)
    if has_ctx:
        sc = _dg(q, kc_ref[0, 0], _NT)
        m = jnp.maximum(m, jnp.max(sc, axis=-1, keepdims=True))
    p = jnp.exp(s - m)
    l = jnp.sum(p, axis=-1, keepdims=True)
    o = _dg(p.astype(BF16), v_ref[0, 0], _NN)
    if has_ctx:
        pc = jnp.exp(sc - m)
        l = l + jnp.sum(pc, axis=-1, keepdims=True)
        o = o + _dg(pc.astype(BF16), vc_ref[0, 0], _NN)
    o_ref[0] = o / l


def _attention(q, k, v, kc, vc, tq):
    nb, nh, t, hd = q.shape
    s = k.shape[2]
    has_ctx = kc is not None
    in_specs = [
        pl.BlockSpec((1, 1, tq, hd), lambda b, h, i: (b, h, i, 0)),
        pl.BlockSpec((1, 1, s, hd), lambda b, h, i: (b, h, 0, 0)),
        pl.BlockSpec((1, 1, s, V_DIM), lambda b, h, i: (b, h, 0, 0)),
    ]
    args = [q, k, v]
    if has_ctx:
        sc = kc.shape[2]
        in_specs += [
            pl.BlockSpec((1, 1, sc, hd), lambda b, h, i: (b, h, 0, 0)),
            pl.BlockSpec((1, 1, sc, V_DIM), lambda b, h, i: (b, h, 0, 0)),
        ]
        args += [kc, vc]
    return pl.pallas_call(
        functools.partial(_attn_body, has_ctx),
        grid=(nb, nh, t // tq),
        in_specs=in_specs,
        out_specs=pl.BlockSpec((1, tq, V_DIM), lambda b, h, i: (b, i, h)),
        out_shape=jax.ShapeDtypeStruct((nb, t, nh * V_DIM), F32),
        compiler_params=_cparams("parallel", "parallel", "parallel"),
        name="attention",
    )(*args)


def _rwkv_prep_body(p_ref, hp_ref, hn_ref, mu_ref, kk_ref, ka_ref, rk_ref, w0_ref, w2_ref, a0_ref, a2_ref, g2_ref,
                    r_out, v_out, kk_out, lw_out, a_out, kd_out, g_out, bonus_out):
    p = p_ref[0]
    tt = p.shape[0]
    row = lax.broadcasted_iota(I32, (tt, 1), 0)
    prev = jnp.where(row == 0, hp_ref[0, 0], pltpu.roll(p, 1, 0))
    nxt = jnp.where(row == tt - 1, hn_ref[0, 0], pltpu.roll(p, tt - 1, 0))
    ps = p + (0.5 * (prev + nxt) - p) * mu_ref[...]
    r = ps[:, :D_RWKV]
    k = ps[:, D_RWKV:2 * D_RWKV]
    v = ps[:, 2 * D_RWKV:3 * D_RWKV]
    wl = ps[:, 3 * D_RWKV:3 * D_RWKV + 2 * W_LORA]
    al = ps[:, 3 * D_RWKV + 2 * W_LORA:3 * D_RWKV + 2 * W_LORA + 2 * A_LORA]
    gl = ps[:, 3 * D_RWKV + 2 * W_LORA + 2 * A_LORA:]
    ones = _head_ones(D_RWKV, N_RWKV)
    kk = k * kk_ref[...]
    kk = kk / jnp.maximum(jnp.sqrt(_dot_x(kk * kk, ones)), 1e-12)
    wlin = w0_ref[...] + _dot3(jnp.tanh(wl), w2_ref[...])
    lw = -jnp.exp(-_softplus(-wlin) - 0.5)
    ag = _sigmoid(a0_ref[...] + _dot3(al, a2_ref[...]))
    bonus = jnp.zeros_like(v)
    for d in range(2):
        sl = slice(d * D_RWKV, (d + 1) * D_RWKV)
        kd = k * (1.0 + (ag[:, sl] - 1.0) * ka_ref[...])
        kd_out[0, :, sl] = kd
        bonus = bonus + _dot_x(r * kd * rk_ref[...], ones) * v
    r_out[0] = r
    v_out[0] = v
    kk_out[0] = kk
    lw_out[0] = lw
    a_out[0] = ag
    g_out[0] = _dot1(_sigmoid(gl), g2_ref[...])
    bonus_out[0] = bonus


def _halo_rows(x, tt, width):
    nb, t, c = x.shape
    nt = t // tt
    xt = x.reshape(nb, nt, tt, c)
    zero = jnp.zeros((nb, 1, width, c), x.dtype)
    prev = jnp.concatenate([zero, xt[:, :-1, tt - width:]], axis=1)
    nxt = jnp.concatenate([xt[:, 1:, :width], zero], axis=1)
    return prev, nxt


def _rwkv_prep(p_rwkv, lp, tt):
    nb, t, pw = p_rwkv.shape
    hp, hn = _halo_rows(p_rwkv, tt, 1)
    tok = lambda n: pl.BlockSpec((1, tt, n), lambda b, i: (b, i, 0))
    halo = pl.BlockSpec((1, 1, 1, pw), lambda b, i: (b, i, 0, 0))
    small = [lp['mu_rwkv'], lp['k_k'], lp['k_a'], lp['r_k'], lp['w0_cat'], lp['w2_bd'], lp['a0_cat'], lp['a2_bd'],
             lp['g2_rwkv']]
    widths = [D_RWKV, D_RWKV, D_RWKV, 2 * D_RWKV, 2 * D_RWKV, 2 * D_RWKV, D_RWKV, D_RWKV]
    return pl.pallas_call(
        _rwkv_prep_body,
        grid=(nb, t // tt),
        in_specs=[tok(pw), halo, halo] + [_full(a.shape) for a in small],
        out_specs=[tok(n) for n in widths],
        out_shape=[jax.ShapeDtypeStruct((nb, t, n), F32) for n in widths],
        compiler_params=_cparams("parallel", "parallel"),
        name="rwkv_prep",
    )(p_rwkv, hp, hn, *small)


def _rwkv_chunk(reverse, r, v, kk, lw, ag, kd, s_old):
    cl = r.shape[0]
    hw = D_RWKV
    nh = H_RWKV * cl

    def later(rows, cols, strict):
        t = lax.broadcasted_iota(I32, (rows, cols), 0)
        s = lax.broadcasted_iota(I32, (rows, cols), 1) % cl
        dlt = (s - t) if reverse else (t - s)
        return (dlt > 0) if strict else (dlt >= 0)

    cum = _x_dot(later(cl, cl, False).astype(F32), lw)
    yield
    tot = jnp.sum(lw, axis=0, keepdims=True)
    e_neg = jnp.exp(-cum)
    e_rem = jnp.exp(tot - cum)
    at = -kk * jnp.exp(cum - lw)
    rt = r * jnp.exp(cum)
    b0 = kk * ag
    bt = b0 * e_neg
    kt = kd * e_neg
    bh = b0 * e_rem
    kh = kd * e_rem
    w_c = jnp.exp(tot)

    lane_head = lax.broadcasted_iota(I32, (1, hw), 1) // N_RWKV
    hms = [(lane_head == h).astype(F32) for h in range(H_RWKV)]
    stack = lambda z: jnp.concatenate([z * hm for hm in hms], axis=0)
    bks = jnp.concatenate([stack(bt), stack(kt)], axis=0)
    vs = stack(v)

    ls = _dot3(at, bks, _NT)
    ly = _dot1(rt, bks, _NT)
    yield
    strict = later(cl, nh, True)
    incl = later(cl, nh, False)
    l_ab = jnp.where(strict, ls[:, :nh], 0.0)
    l_ak = jnp.where(strict, ls[:, nh:], 0.0)
    l_rb = jnp.where(incl, ly[:, :nh], 0.0)
    l_rk = jnp.where(incl, ly[:, nh:], 0.0)

    x = jnp.concatenate([l_ab] * H_RWKV, axis=0) * _head_ones(nh, cl)
    eye = (lax.broadcasted_iota(I32, (nh, nh), 0) == lax.broadcasted_iota(I32, (nh, nh), 1)).astype(F32)
    inv = eye + x
    span = 2
    while span < cl:
        x = _dot3(x, x)
        yield
        inv = inv + _dot3(inv, x)
        yield
        span *= 2

    rhs = _dot3(at, s_old, _NT) + _dot3(l_ak, vs)
    yield
    us = _dot3(inv, stack(rhs))
    yield
    um = us[:cl]
    for h in range(1, H_RWKV):
        um = um + us[h * cl:(h + 1) * cl]
    y = _dot1(rt, s_old, _NT) + _dot1(jnp.concatenate([l_rb, l_rk], axis=1), jnp.concatenate([us, vs], axis=0))
    yield
    uv_t = jnp.concatenate([um, v], axis=0).T
    s_new = s_old * w_c + _dot3(uv_t, jnp.concatenate([bh, kh], axis=0)) * _head_ones(hw, N_RWKV)
    yield y, s_new


def _lockstep(gens):
    last = [None] * len(gens)
    live = list(range(len(gens)))
    while live:
        for i in list(live):
            try:
                last[i] = next(gens[i])
            except StopIteration:
                live.remove(i)
    return last


def _rwkv_scan_body(rf_ref, vf_ref, kkf_ref, lwf_ref, af_ref, kdf_ref, rb_ref, vb_ref, kkb_ref, lwb_ref, ab_ref,
                    kdb_ref, s0_ref, yf_ref, yb_ref, sf_ref, s_scr):
    c = pl.program_id(1)
    nc = pl.num_programs(1)

    @pl.when(c == 0)
    def _():
        s_scr[...] = s0_ref[...]

    nrow = rf_ref.shape[0]
    chains = []
    for i in range(nrow):
        chains.append(_rwkv_chunk(False, rf_ref[i], vf_ref[i], kkf_ref[i], lwf_ref[i], af_ref[i], kdf_ref[i],
                                  s_scr[i, 0]))
        chains.append(_rwkv_chunk(True, rb_ref[i], vb_ref[i], kkb_ref[i], lwb_ref[i], ab_ref[i], kdb_ref[i],
                                  s_scr[i, 1]))
    done = _lockstep(chains)
    for i in range(nrow):
        (y_f, s_f), (y_b, s_b) = done[2 * i], done[2 * i + 1]
        yf_ref[i] = y_f
        yb_ref[i] = y_b
        s_scr[i, 0] = s_f
        s_scr[i, 1] = s_b

    @pl.when(c == nc - 1)
    def _():
        sf_ref[...] = s_scr[...]


def _rwkv_scan(r, v, kk, lw, ag, kd, s0):
    nb, t, hw = r.shape
    cl = RWKV_CHUNK
    nr = RWKV_ROWS
    nc = t // cl
    fwd = lambda lane: pl.BlockSpec((nr, cl, hw), lambda b, c: (b, c, lane))
    bwd = lambda lane: pl.BlockSpec((nr, cl, hw), lambda b, c: (b, nc - 1 - c, lane))
    state = pl.BlockSpec((nr, 2, hw, hw), lambda b, c: (b, 0, 0, 0))
    y_shape = jax.ShapeDtypeStruct((nb, t, hw), F32)
    return pl.pallas_call(
        _rwkv_scan_body,
        grid=(nb // nr, nc),
        in_specs=[fwd(0)] * 6 + [bwd(0)] * 3 + [bwd(1)] * 3 + [state],
        out_specs=[fwd(0), bwd(0), state],
        out_shape=[y_shape, y_shape, jax.ShapeDtypeStruct((nb, 2, hw, hw), F32)],
        scratch_shapes=[pltpu.VMEM((nr, 2, hw, hw), F32)],
        compiler_params=_cparams("parallel", "arbitrary"),
        name="rwkv_scan",
    )(r, v, kk, lw, ag, kd, r, v, kk, lw, ag, kd, s0)


def _lru_body(reverse, p_ref, halo_ref, cw_ref, cb_ref, wa_ref, ba_ref, wx_ref, bx_ref, lam_ref, h0_ref,
              h_out, fin_out, a_scr, u_scr, carry_scr):
    i = pl.program_id(1)
    nt = pl.num_programs(1)
    tt = p_ref.shape[1]
    n_ext = tt + SUBLANES

    @pl.when(i == 0)
    def _():
        carry_scr[...] = jnp.broadcast_to(h0_ref[0], (SUBLANES, D_LRU))

    xb = p_ref[0, :, :D_LRU]
    halo = halo_ref[0, 0]
    xc = cb_ref[...] + jnp.zeros_like(xb)
    if reverse:
        xe = jnp.concatenate([xb, halo], axis=0)
        for k in range(CONV_W):
            sh = xb if k == 0 else pltpu.roll(xe, n_ext - k, 0)[:tt]
            xc = xc + cw_ref[k:k + 1, :] * sh
    else:
        xe = jnp.concatenate([halo, xb], axis=0)
        for k in range(CONV_W):
            s = CONV_W - 1 - k
            sh = xb if s == 0 else pltpu.roll(xe, s, 0)[SUBLANES:]
            xc = xc + cw_ref[k:k + 1, :] * sh
    gate_r = _sigmoid(_dot3(xc, wa_ref[...]) + ba_ref[...])
    gate_i = _sigmoid(_dot3(xc, wx_ref[...]) + bx_ref[...])
    log_a = -LRU_C * gate_r * _softplus(-lam_ref[...])
    a = jnp.exp(log_a)
    u = jnp.sqrt(-_expm1(2.0 * log_a)) * (gate_i * xc)

    rowm = lax.broadcasted_iota(I32, (tt, 1), 0) % SUBLANES
    for s in (1, 2, 4):
        if reverse:
            a_s = pltpu.roll(a, tt - s, 0)
            u_s = pltpu.roll(u, tt - s, 0)
            m = rowm < SUBLANES - s
        else:
            a_s = pltpu.roll(a, s, 0)
            u_s = pltpu.roll(u, s, 0)
            m = rowm >= s
        u = jnp.where(m, a * u_s + u, u)
        a = jnp.where(m, a * a_s, a)
    a_scr[...] = a
    u_scr[...] = u
    ng = tt // SUBLANES

    def group(g, carry):
        gi = (ng - 1 - g) if reverse else g
        r0 = pl.multiple_of(gi * SUBLANES, SUBLANES)
        hg = a_scr[pl.ds(r0, SUBLANES), :] * carry + u_scr[pl.ds(r0, SUBLANES), :]
        h_out[0, pl.ds(r0, SUBLANES), :] = hg
        edge = hg[0:1, :] if reverse else hg[SUBLANES - 1:SUBLANES, :]
        return jnp.broadcast_to(edge, (SUBLANES, D_LRU))

    carry = lax.fori_loop(0, ng, group, carry_scr[...])
    carry_scr[...] = carry

    @pl.when(i == nt - 1)
    def _():
        fin_out[0] = carry[0:1, :]


def _lru_dir(p_lru, halo, lp, d, h0, tt):
    nb, t, pw = p_lru.shape
    nt = t // tt
    reverse = d == 1
    tidx = (lambda i: nt - 1 - i) if reverse else (lambda i: i)
    small = [lp['conv_w'][d], lp['conv_b'][d:d + 1], lp['wa_bd'][d], lp['b_lru_a'][d:d + 1], lp['wx_bd'][d],
             lp['b_lru_x'][d:d + 1], lp['lam'][d:d + 1]]
    return pl.pallas_call(
        functools.partial(_lru_body, reverse),
        grid=(nb, nt),
        in_specs=[
            pl.BlockSpec((1, tt, pw), lambda b, i: (b, tidx(i), 0)),
            pl.BlockSpec((1, 1, SUBLANES, D_LRU), lambda b, i: (b, tidx(i), 0, 0)),
        ] + [_full(a.shape) for a in small] + [pl.BlockSpec((1, 1, D_LRU), lambda b, i: (b, 0, 0))],
        out_specs=[
            pl.BlockSpec((1, tt, D_LRU), lambda b, i: (b, tidx(i), 0)),
            pl.BlockSpec((1, 1, D_LRU), lambda b, i: (b, 0, 0)),
        ],
        out_shape=[jax.ShapeDtypeStruct((nb, t, D_LRU), F32), jax.ShapeDtypeStruct((nb, 1, D_LRU), F32)],
        scratch_shapes=[pltpu.VMEM((tt, D_LRU), F32), pltpu.VMEM((tt, D_LRU), F32), pltpu.VMEM((SUBLANES, D_LRU), F32)],
        compiler_params=_cparams("parallel", "arbitrary"),
        name="lru_bwd" if reverse else "lru_fwd",
    )(p_lru, halo, *small, h0)


def _gelu_tanh(x):
    return 0.5 * x * (1.0 + jnp.tanh(np.sqrt(2.0 / np.pi).astype(np.float32) * (x + 0.044715 * (x * x * x))))


def _out_proj_body(x_ref, o_ref, y0_ref, y1_ref, bonus_ref, g_ref, h0_ref, h1_ref, p_ref,
                   gmla_ref, lnw_ref, lnb_ref, glru_ref, wo_ref, gpost_ref, gate_ref, gpre_ref, sc_ref, sh_ref,
                   x1_out, h2_out):
    o_mla = _rms(o_ref[0], gmla_ref[...])
    ones = _head_ones(D_RWKV, N_RWKV)
    y = y0_ref[0] + y1_ref[0]
    mu = _dot_x(y, ones) * (1.0 / N_RWKV)
    yc = y - mu
    var = _dot_x(yc * yc, ones) * (1.0 / N_RWKV)
    y = yc * lax.rsqrt(var + GN_EPS) * lnw_ref[...] + lnb_ref[...]
    o_rwkv = (y + bonus_ref[0]) * g_ref[0]
    o_lru = _rms((h0_ref[0] + h1_ref[0]) * _gelu_tanh(p_ref[0, :, D_LRU:]), glru_ref[...])
    m = (_dg(o_mla.astype(BF16), wo_ref[:D_MLA_OUT, :], _NN)
         + _dg(o_rwkv.astype(BF16), wo_ref[D_MLA_OUT:D_MLA_OUT + D_RWKV, :], _NN)
         + _dg(o_lru.astype(BF16), wo_ref[D_MLA_OUT + D_RWKV:, :], _NN))
    x1 = x_ref[0] + gate_ref[0] * _rms(m, gpost_ref[...])
    x1_out[0] = x1
    h2_out[0] = _rms(x1, gpre_ref[...]) * (1.0 + sc_ref[0]) + sh_ref[0]


def _out_proj(x, o, y_f, y_b, bonus, g, h0, h1, p_lru, lp, mod, tt):
    nb, t, d = x.shape
    per_batch = mod['g1'].shape[0] > 1
    tok = lambda n: pl.BlockSpec((1, tt, n), lambda b, i: (b, i, 0))
    small = [lp['g_mla_out'], lp['ln_x_w'], lp['ln_x_b'], lp['g_lru_out'], lp['w_out'], lp['g_post_mix']]
    return pl.pallas_call(
        _out_proj_body,
        grid=(nb, t // tt),
        in_specs=[tok(d), tok(D_MLA_OUT), tok(D_RWKV), tok(D_RWKV), tok(D_RWKV), tok(D_RWKV), tok(D_LRU), tok(D_LRU),
                  tok(P_LRU)] + [_full(a.shape) for a in small]
                 + [_mod_spec(per_batch), _full(lp['g_pre_ffn'].shape), _mod_spec(per_batch), _mod_spec(per_batch)],
        out_specs=[tok(d), tok(d)],
        out_shape=[jax.ShapeDtypeStruct((nb, t, d), F32)] * 2,
        compiler_params=_cparams("parallel", "parallel"),
        name="out_proj",
    )(x, o, y_f, y_b, bonus, g, h0, h1, p_lru, *small, mod['g1'], lp['g_pre_ffn'], mod['sc2'], mod['sh2'])


def _first_index(mask, idx, sentinel):
    return jnp.min(jnp.where(mask, idx, sentinel), axis=0, keepdims=True)


def _router_body(h_ref, wr_ref, br_ref, e_out, pos_out, w_out, cnt_out, carry):
    i = pl.program_id(0)
    tn = h_ref.shape[0]

    @pl.when(i == 0)
    def _():
        carry[...] = jnp.zeros_like(carry)

    logits = lax.dot_general(wr_ref[...], h_ref[...], _NT, precision=lax.Precision.HIGHEST,
                             preferred_element_type=F32)
    scores = _sigmoid(logits)
    sel = scores + br_ref[...]

    i8 = lax.broadcasted_iota(I32, (GROUP_SIZE, tn), 0)
    blocks, gscore = [], []
    for g in range(N_GROUPS):
        blk = sel[g * GROUP_SIZE:(g + 1) * GROUP_SIZE, :]
        m1 = jnp.max(blk, axis=0, keepdims=True)
        f1 = _first_index(blk == m1, i8, GROUP_SIZE)
        m2 = jnp.max(jnp.where(i8 == f1, -jnp.inf, blk), axis=0, keepdims=True)
        blocks.append(blk)
        gscore.append(m1 + m2)
    masked = []
    for g in range(N_GROUPS):
        beaten = jnp.zeros((1, tn), I32)
        for o in range(N_GROUPS):
            if o == g:
                continue
            wins = (gscore[o] >= gscore[g]) if o < g else (gscore[o] > gscore[g])
            beaten = beaten + wins.astype(I32)
        masked.append(jnp.where(beaten < TOPK_GROUPS, blocks[g], -jnp.inf))
    cur = jnp.concatenate(masked, axis=0)

    ie = lax.broadcasted_iota(I32, (N_EXPERTS, tn), 0)
    firsts, raw = [], []
    chosen = jnp.zeros((N_EXPERTS, tn), F32)
    for _ in range(TOP_K):
        mx = jnp.max(cur, axis=0, keepdims=True)
        first = _first_index(cur == mx, ie, N_EXPERTS)
        hit = ie == first
        firsts.append(first)
        raw.append(jnp.sum(jnp.where(hit, scores, 0.0), axis=0, keepdims=True))
        cur = jnp.where(hit, -jnp.inf, cur)
        chosen = chosen + hit.astype(F32)
    total = raw[0]
    for k in range(1, TOP_K):
        total = total + raw[k]

    before = (lax.broadcasted_iota(I32, (tn, tn), 0) < lax.broadcasted_iota(I32, (tn, tn), 1)).astype(BF16)
    rank = _dg(chosen.astype(BF16), before, _NN) + carry[:, 0:1]
    e_out[...] = jnp.zeros_like(e_out)
    pos_out[...] = jnp.zeros_like(pos_out)
    w_out[...] = jnp.zeros_like(w_out)
    for k in range(TOP_K):
        e_out[0, k:k + 1, :] = firsts[k]
        pos_out[0, k:k + 1, :] = jnp.sum(jnp.where(ie == firsts[k], rank, 0.0), axis=0, keepdims=True).astype(I32)
        w_out[k:k + 1, :] = raw[k] / total * ROUTED_SCALE
    new_carry = carry[...] + jnp.sum(chosen, axis=1, keepdims=True)
    carry[...] = new_carry
    cnt_out[...] = new_carry


def _router(h2, wr_t, b_col):
    n, d = h2.shape
    tn = ROUTER_TILE
    row_spec = pl.BlockSpec((SLOT_ROWS, tn), lambda i: (0, i))
    tile_spec = pl.BlockSpec((1, SLOT_ROWS, tn), lambda i: (i, 0, 0))
    cnt_spec = pl.BlockSpec((N_EXPERTS, 128), lambda i: (0, 0))
    tile_shape = jax.ShapeDtypeStruct((n // tn, SLOT_ROWS, tn), I32)
    return pl.pallas_call(
        _router_body,
        grid=(n // tn,),
        in_specs=[pl.BlockSpec((tn, d), lambda i: (i, 0)), _full(wr_t.shape), _full(b_col.shape)],
        out_specs=[tile_spec, tile_spec, row_spec, cnt_spec],
        out_shape=[tile_shape, tile_shape,
                   jax.ShapeDtypeStruct((SLOT_ROWS, n), F32), jax.ShapeDtypeStruct((N_EXPERTS, 128), F32)],
        scratch_shapes=[pltpu.VMEM((N_EXPERTS, 128), F32)],
        compiler_params=_cparams("arbitrary"),
        name="router",
    )(h2, wr_t, b_col)


ROW_LANES = 128


def _to_row_tiles(ref, x):
    rows = x.shape[0]
    for s in range(SUBLANES):
        ref[pl.ds(s, rows, stride=SUBLANES), :] = x[:, s * ROW_LANES:(s + 1) * ROW_LANES]


def _from_row_tiles(ref, rows):
    return jnp.concatenate([ref[pl.ds(s, rows, stride=SUBLANES), :] for s in range(SUBLANES)], axis=1)


def _row_copy(src, src_row8, dst, dst_row8, sem):
    return pltpu.make_async_copy(src.at[pl.ds(pl.multiple_of(src_row8, SUBLANES), SUBLANES), :],
                                 dst.at[pl.ds(pl.multiple_of(dst_row8, SUBLANES), SUBLANES), :], sem)


def _load_slots(slot_ref, slot_smem, sem):
    cp = pltpu.make_async_copy(slot_ref.at[0], slot_smem, sem)
    cp.start()
    cp.wait()


def _dispatch_body(start_ref, e_ref, pos_ref, h_ref, xs_in, slots_out, xs_out, slot_smem, h_tiles, sem_s, sem):
    del xs_in
    tn = slot_smem.shape[1]
    e = e_ref[0]
    slot = pos_ref[0]
    for x in range(N_EXPERTS):
        slot = slot + jnp.where(e == x, start_ref[x], 0)
    slots_out[0] = slot * SUBLANES
    _load_slots(slots_out, slot_smem, sem_s)
    _to_row_tiles(h_tiles, h_ref[...])

    def issue(j, carry):
        for k in range(TOP_K):
            _row_copy(h_tiles, j * SUBLANES, xs_out, slot_smem[k, j], sem).start(priority=k % 2)
        return carry

    def drain(j, carry):
        for k in range(TOP_K):
            _row_copy(h_tiles, 0, xs_out, 0, sem).wait()
        return carry

    lax.fori_loop(0, tn, issue, 0)
    lax.fori_loop(0, tn, drain, 0)


def _dispatch(row_start, e3, pos3, h2, xs_zero, tn):
    n, d = h2.shape
    ratio = e3.shape[2] // tn
    tile = lambda: pl.BlockSpec((1, SLOT_ROWS, tn), lambda i, st: (i // ratio, 0, i % ratio))
    grid_spec = pltpu.PrefetchScalarGridSpec(
        num_scalar_prefetch=1,
        grid=(n // tn,),
        in_specs=[tile(), tile(), pl.BlockSpec((tn, d), lambda i, st: (i, 0)), pl.BlockSpec(memory_space=pl.ANY)],
        out_specs=[pl.BlockSpec((1, SLOT_ROWS, tn), lambda i, st: (i, 0, 0)), pl.BlockSpec(memory_space=pl.ANY)],
        scratch_shapes=[pltpu.SMEM((SLOT_ROWS, tn), I32), pltpu.VMEM((tn * SUBLANES, ROW_LANES), F32),
                        pltpu.SemaphoreType.DMA, pltpu.SemaphoreType.DMA],
    )
    return pl.pallas_call(
        _dispatch_body,
        grid_spec=grid_spec,
        out_shape=[jax.ShapeDtypeStruct((n // tn, SLOT_ROWS, tn), I32),
                   jax.ShapeDtypeStruct(xs_zero.shape, xs_zero.dtype)],
        input_output_aliases={4: 1},
        compiler_params=_cparams("arbitrary"),
        name="dispatch",
    )(row_start, e3, pos3, h2, xs_zero)


def _swiglu(x, wg, wu, wd):
    g = _dg(x, wg, _NN)
    u = _dg(x, wu, _NN)
    return _dg((g * _sigmoid(g) * u).astype(BF16), wd, _NN)


def _expert_body(be_ref, nu_ref, xs_ref, wg_ref, wu_ref, wd_ref, ys_ref):
    del be_ref
    b = pl.program_id(0)

    @pl.when(b < nu_ref[0])
    def _():
        x = _from_row_tiles(xs_ref, MOE_BLOCK).astype(BF16)
        _to_row_tiles(ys_ref, _swiglu(x, wg_ref[0], wu_ref[0], wd_ref[0]))

    @pl.when(b >= nu_ref[0])
    def _():
        ys_ref[...] = jnp.zeros_like(ys_ref)


def _experts(block_e, n_used, xs, wg, wu, wd):
    bm = MOE_BLOCK
    _, d, de = wg.shape
    rows = bm * SUBLANES
    grid_spec = pltpu.PrefetchScalarGridSpec(
        num_scalar_prefetch=2,
        grid=(xs.shape[0] // rows,),
        in_specs=[
            pl.BlockSpec((rows, ROW_LANES), lambda b, be, nu: (b, 0)),
            pl.BlockSpec((1, d, de), lambda b, be, nu: (be[b], 0, 0)),
            pl.BlockSpec((1, d, de), lambda b, be, nu: (be[b], 0, 0)),
            pl.BlockSpec((1, de, d), lambda b, be, nu: (be[b], 0, 0)),
        ],
        out_specs=pl.BlockSpec((rows, ROW_LANES), lambda b, be, nu: (b, 0)),
    )
    return pl.pallas_call(
        _expert_body,
        grid_spec=grid_spec,
        out_shape=jax.ShapeDtypeStruct(xs.shape, F32),
        compiler_params=_cparams("arbitrary"),
        name="experts",
    )(block_e, n_used, xs, wg, wu, wd)


def _combine_body(slot_ref, w_ref, x1_ref, h2_ref, gate_ref, ys_hbm, wsg_ref, wsu_ref, wsd_ref, gpost_ref,
                  out_ref, slot_smem, buf, sem_s, sem):
    tn = slot_smem.shape[1]
    _load_slots(slot_ref, slot_smem, sem_s)

    def issue(j, carry):
        for k in range(TOP_K):
            _row_copy(ys_hbm, slot_smem[k, j], buf.at[k], j * SUBLANES, sem).start(priority=k % 2)
        return carry

    def drain(j, carry):
        for k in range(TOP_K):
            _row_copy(ys_hbm, 0, buf.at[k], 0, sem).wait()
        return carry

    lax.fori_loop(0, tn, issue, 0)
    acc = _swiglu(h2_ref[...].astype(BF16), wsg_ref[...], wsu_ref[...], wsd_ref[...])
    lax.fori_loop(0, tn, drain, 0)
    for k in range(TOP_K):
        acc = acc + _from_row_tiles(buf.at[k], tn) * w_ref[:, k:k + 1]
    out_ref[...] = x1_ref[...] + gate_ref[0] * _rms(acc, gpost_ref[...])


def _combine(slots3, w_tok, x1, h2, gate_rows, ys, wsg, wsu, wsd, gpost):
    ntiles, _, tn = slots3.shape
    n, d = x1.shape
    tok = pl.BlockSpec((tn, d), lambda i: (i, 0))
    return pl.pallas_call(
        _combine_body,
        grid=(ntiles,),
        in_specs=[pl.BlockSpec((1, SLOT_ROWS, tn), lambda i: (i, 0, 0)),
                  pl.BlockSpec((tn, SLOT_ROWS), lambda i: (i, 0)), tok, tok,
                  pl.BlockSpec((1, 1, d), lambda i: (i, 0, 0)),
                  pl.BlockSpec(memory_space=pl.ANY),
                  _full(wsg.shape), _full(wsu.shape), _full(wsd.shape), _full(gpost.shape)],
        out_specs=tok,
        out_shape=jax.ShapeDtypeStruct((n, d), F32),
        scratch_shapes=[pltpu.SMEM((SLOT_ROWS, tn), I32), pltpu.VMEM((TOP_K, tn * SUBLANES, ROW_LANES), F32),
                        pltpu.SemaphoreType.DMA, pltpu.SemaphoreType.DMA],
        compiler_params=_cparams("arbitrary"),
        name="combine",
    )(slots3, w_tok, x1, h2, gate_rows, ys, wsg, wsu, wsd, gpost)


def _moe(x1, h2, gate_rows, lp):
    n, d = h2.shape
    bm = MOE_BLOCK
    tn = TOKEN_TILE
    e3, pos3, w_rows, cnt = _router(h2, lp['wr_t'], lp['b_router_col'])
    counts = cnt[:, 0].astype(I32)
    blocks_per_e = (counts + bm - 1) // bm
    blk_end = jnp.cumsum(blocks_per_e)
    blk_start = blk_end - blocks_per_e
    n_blocks = -(-(n * TOP_K) // bm) + N_EXPERTS
    block_e = jnp.minimum(jnp.sum(blk_end[None, :] <= jnp.arange(n_blocks, dtype=I32)[:, None], axis=1),
                          N_EXPERTS - 1).astype(I32)
    n_used = blk_end[-1:].astype(I32)
    xs_zero = jnp.zeros((n_blocks * bm * SUBLANES, d // SUBLANES), F32)
    slots3, xs = _dispatch((blk_start * bm).astype(I32), e3, pos3, h2, xs_zero, tn)
    ys = _experts(block_e, n_used, xs, lp['w_e_gate'], lp['w_e_up'], lp['w_e_down'])
    return _combine(slots3, w_rows.T, x1, h2, gate_rows, ys, lp['w_s_gate'], lp['w_s_up'], lp['w_s_down'],
                    lp['g_post_ffn'])


def _rope_tables(n_tokens):
    rows = n_tokens // GRID_W
    row = jnp.repeat(jnp.arange(rows, dtype=F32), GRID_W)
    col = jnp.tile(jnp.arange(GRID_W, dtype=F32), rows)
    n_freq = ROPE_DIM // 4
    inv_freq = ROPE_THETA ** (-jnp.arange(n_freq, dtype=F32) / n_freq)
    ang_r = row[:, None] * inv_freq
    ang_c = col[:, None] * inv_freq
    ang = jnp.concatenate([ang_r, ang_r, ang_c, ang_c], axis=-1)
    cos, sin = jnp.cos(ang), jnp.sin(ang)
    pad = lambda z, fill: jnp.concatenate(
        [jnp.full((n_tokens, NOPE_DIM), fill, F32), z, jnp.zeros((n_tokens, HEAD_PAD - QK_DIM), F32)], axis=1)
    return pad(cos, 1.0), pad(sin, 0.0), pad(cos, 0.0), pad(sin, 0.0)


def _rot_cols(w):
    q = ROPE_DIM // 4
    perm = np.concatenate([np.arange(q, 2 * q), np.arange(0, q), np.arange(3 * q, 4 * q), np.arange(2 * q, 3 * q)])
    sign = np.concatenate([-np.ones(q), np.ones(q), -np.ones(q), np.ones(q)]).astype(np.float32)
    return w[..., perm] * sign


def _block_diag(blocks):
    h, m, n = blocks.shape[-3:]
    eye = jnp.eye(h, dtype=blocks.dtype)
    out = blocks[..., :, :, None, :] * eye[:, None, :, None]
    return out.reshape(blocks.shape[:-3] + (h * m, h * n))


def _diag_blocks(mat, h):
    m, n = mat.shape[-2] // h, mat.shape[-1] // h
    z = mat.reshape(mat.shape[:-2] + (h, m, h, n))
    return jnp.stack([z[..., i, :, i, :] for i in range(h)], axis=-3)


def _layer_operands(l, a):
    w_in = a['w_in'][l]
    d = w_in.shape[0]
    p_mla = Q_LORA + KV_LORA + ROPE_DIM
    w_kr = w_in[:, Q_LORA + KV_LORA:p_mla]
    z32 = jnp.zeros((d, ROPE_DIM), F32)
    w_mla = jnp.concatenate([w_in[:, :Q_LORA + KV_LORA], z32, _rot_cols(w_kr), w_kr, z32], axis=1)
    wq3 = a['w_q_b'][l].reshape(Q_LORA, H_MLA, QK_DIM)
    zq = lambda n: jnp.zeros((Q_LORA, H_MLA, n), F32)
    wq = jnp.concatenate([wq3, zq(HEAD_PAD - QK_DIM)], axis=2).reshape(Q_LORA, H_MLA * HEAD_PAD)
    wqr = jnp.concatenate([zq(NOPE_DIM), _rot_cols(wq3[:, :, NOPE_DIM:]), zq(HEAD_PAD - QK_DIM)],
                          axis=2).reshape(Q_LORA, H_MLA * HEAD_PAD)
    wkv3 = a['w_kv_b'][l].reshape(KV_LORA, H_MLA, NOPE_DIM + V_DIM)
    wk = jnp.concatenate([wkv3[:, :, :NOPE_DIM], jnp.zeros((KV_LORA, H_MLA, HEAD_PAD - NOPE_DIM), F32)],
                         axis=2).reshape(KV_LORA, H_MLA * HEAD_PAD)
    wv = wkv3[:, :, NOPE_DIM:].reshape(KV_LORA, H_MLA * V_DIM)
    row = lambda z: z.reshape(1, -1)
    two = lambda z: jnp.concatenate([z[0], z[1]], axis=-1)
    bd2 = lambda z: jnp.concatenate([jnp.concatenate([z[0], jnp.zeros_like(z[0])], axis=1),
                                     jnp.concatenate([jnp.zeros_like(z[1]), z[1]], axis=1)], axis=0)
    return {
        'g_pre_mix': row(a['g_pre_mix'][l]), 'g_post_mix': row(a['g_post_mix'][l]),
        'g_pre_ffn': row(a['g_pre_ffn'][l]), 'g_post_ffn': row(a['g_post_ffn'][l]),
        'w_mla': w_mla.astype(BF16),
        'w_rwkv': w_in[:, p_mla:p_mla + P_RWKV].astype(BF16),
        'w_lru': w_in[:, p_mla + P_RWKV:].astype(BF16),
        'g_q_a': row(a['g_q_a'][l]), 'g_kv_a': row(a['g_kv_a'][l]),
        'wq': wq.astype(BF16), 'wqr': wqr.astype(BF16), 'wk': wk.astype(BF16), 'wv': wv.astype(BF16),
        'g_mla_out': row(a['g_mla_out'][l]),
        'mu_rwkv': row(a['mu_rwkv'][l]), 'k_k': row(a['k_k'][l]), 'k_a': row(a['k_a'][l]), 'r_k': row(a['r_k'][l]),
        'w0_cat': row(two(a['w0_rwkv'][l])), 'w2_bd': bd2(a['w2_rwkv'][l]),
        'a0_cat': row(two(a['a0_rwkv'][l])), 'a2_bd': bd2(a['a2_rwkv'][l]),
        'g2_rwkv': a['g2_rwkv'][l], 'ln_x_w': row(a['ln_x_w'][l]), 'ln_x_b': row(a['ln_x_b'][l]),
        'conv_w': a['conv_w'][l], 'conv_b': a['conv_b'][l],
        'wa_bd': _block_diag(a['w_lru_a'][l]), 'b_lru_a': a['b_lru_a'][l],
        'wx_bd': _block_diag(a['w_lru_x'][l]), 'b_lru_x': a['b_lru_x'][l],
        'lam': a['lam'][l], 'g_lru_out': row(a['g_lru_out'][l]),
        'w_out': a['w_out'][l].astype(BF16),
        'wr_t': a['w_router'][l].T, 'b_router_col': a['b_router'][l].reshape(N_EXPERTS, 1),
        'w_e_gate': a['w_e_gate'][l].astype(BF16), 'w_e_up': a['w_e_up'][l].astype(BF16),
        'w_e_down': a['w_e_down'][l].astype(BF16),
        'w_s_gate': a['w_s_gate'][l].astype(BF16), 'w_s_up': a['w_s_up'][l].astype(BF16),
        'w_s_down': a['w_s_down'][l].astype(BF16),
    }


def _mixer(x, mod, lp, ctx, tables):
    nb, t, _ = x.shape
    tt = min(t, 512)
    p_mla, p_rwkv, p_lru = _in_proj(x, lp['g_pre_mix'], mod['sc1'], mod['sh1'], lp['w_mla'], lp['w_rwkv'],
                                    lp['w_lru'], tt)
    if ctx is None:
        q, k, v, new_cache = _mla_prep(p_mla, lp['g_q_a'], lp['g_kv_a'], lp['wq'], None, lp['wk'], lp['wv'], None, tt)
        kc = vc = None
        s0 = jnp.zeros((nb, 2, D_RWKV, D_RWKV), F32)
        h0 = jnp.zeros((nb, 2, 1, D_LRU), F32)
    else:
        cache, s_rwkv, s_lru = ctx
        q, k, v = _mla_prep(p_mla, lp['g_q_a'], lp['g_kv_a'], lp['wq'], lp['wqr'], lp['wk'], lp['wv'], tables, tt)
        new_cache = None
        kr_placed = jnp.pad(cache[..., KV_LORA:], ((0, 0), (0, 0), (NOPE_DIM, HEAD_PAD - QK_DIM)))
        kc, vc = _ctx_kv(cache[..., :KV_LORA], kr_placed, lp['wk'], lp['wv'])
        s0 = _block_diag(s_rwkv)
        h0 = s_lru[:, :, None, :]
    o = _attention(q, k, v, kc, vc, min(t, 256))

    r, vv, kk, lw, ag, kd, g, bonus = _rwkv_prep(p_rwkv, lp, tt)
    y_f, y_b, s_fin = _rwkv_scan(r, vv, kk, lw, ag, kd, s0)

    hp, hn = _halo_rows(p_lru[..., :D_LRU], tt, SUBLANES)
    h_f, fin_f = _lru_dir(p_lru, hp, lp, 0, h0[:, 0], tt)
    h_b, fin_b = _lru_dir(p_lru, hn, lp, 1, h0[:, 1], tt)

    x1, h2 = _out_proj(x, o, y_f, y_b, bonus, g, h_f, h_b, p_lru, lp, mod, tt)
    states = None
    if ctx is None:
        states = (new_cache, _diag_blocks(s_fin, H_RWKV), jnp.concatenate([fin_f, fin_b], axis=1))
    return x1, h2, states


def kernel(x_prompt, x_sample, c, cache_mla, state_rwkv, state_lru, c_ctx, w_ada, b_ada, g_pre_mix, g_post_mix, g_pre_ffn, g_post_ffn, w_in, g_q_a, w_q_b, g_kv_a, w_kv_b, g_mla_out, mu_rwkv, w0_rwkv, w2_rwkv, a0_rwkv, a2_rwkv, g2_rwkv, k_k, k_a, r_k, ln_x_w, ln_x_b, conv_w, conv_b, w_lru_a, b_lru_a, w_lru_x, b_lru_x, lam, g_lru_out, w_out, w_router, b_router, w_e_gate, w_e_up, w_e_down, w_s_gate, w_s_up, w_s_down):
    a = dict(w_in=w_in, g_pre_mix=g_pre_mix, g_post_mix=g_post_mix, g_pre_ffn=g_pre_ffn, g_post_ffn=g_post_ffn,
             g_q_a=g_q_a, w_q_b=w_q_b, g_kv_a=g_kv_a, w_kv_b=w_kv_b, g_mla_out=g_mla_out, mu_rwkv=mu_rwkv,
             w0_rwkv=w0_rwkv, w2_rwkv=w2_rwkv, a0_rwkv=a0_rwkv, a2_rwkv=a2_rwkv, g2_rwkv=g2_rwkv, k_k=k_k, k_a=k_a,
             r_k=r_k.reshape(r_k.shape[0], -1), ln_x_w=ln_x_w, ln_x_b=ln_x_b, conv_w=conv_w, conv_b=conv_b,
             w_lru_a=w_lru_a, b_lru_a=b_lru_a, w_lru_x=w_lru_x, b_lru_x=b_lru_x, lam=lam, g_lru_out=g_lru_out,
             w_out=w_out, w_router=w_router, b_router=b_router, w_e_gate=w_e_gate, w_e_up=w_e_up,
             w_e_down=w_e_down, w_s_gate=w_s_gate, w_s_up=w_s_up, w_s_down=w_s_down)
    n_layers = w_in.shape[0]
    nbp, tp, d = x_prompt.shape
    nbs, ts, _ = x_sample.shape
    rows = -(-(1 + nbs) // SUBLANES) * SUBLANES
    cvecs = jnp.concatenate([c_ctx[None, :], c, jnp.zeros((rows - 1 - nbs, d), F32)], axis=0)
    mods = _ada(cvecs, w_ada, b_ada)
    names = ('sh1', 'sc1', 'g1', 'sh2', 'sc2', 'g2')
    tables = _rope_tables(ts)
    tn = TOKEN_TILE

    yp, ys = x_prompt, x_sample
    caches, rwkv_states, lru_states = [], [], []
    for l in range(n_layers):
        lp = _layer_operands(l, a)
        mod_p = {nm: mods[l, 0:1, i * d:(i + 1) * d][:, None, :] for i, nm in enumerate(names)}
        mod_s = {nm: mods[l, 1:1 + nbs, i * d:(i + 1) * d][:, None, :] for i, nm in enumerate(names)}
        x1p, h2p, states = _mixer(yp, mod_p, lp, None, None)
        x1s, h2s, _ = _mixer(ys, mod_s, lp, (cache_mla[:, l], state_rwkv[:, l], state_lru[:, l]), tables)
        caches.append(states[0])
        rwkv_states.append(states[1])
        lru_states.append(states[2])
        x1 = jnp.concatenate([x1p.reshape(-1, d), x1s.reshape(-1, d)], axis=0)
        h2 = jnp.concatenate([h2p.reshape(-1, d), h2s.reshape(-1, d)], axis=0)
        gate_rows = jnp.concatenate([jnp.repeat(mod_p['g2'], nbp * tp // tn, axis=0),
                                     jnp.repeat(mod_s['g2'], ts // tn, axis=0)], axis=0)
        x2 = _moe(x1, h2, gate_rows, lp)
        yp = x2[:nbp * tp].reshape(nbp, tp, d)
        ys = x2[nbp * tp:].reshape(nbs, ts, d)
    return (yp, ys, jnp.stack(caches, axis=1), jnp.stack(rwkv_states, axis=1), jnp.stack(lru_states, axis=1))
```

```python
import functools

import numpy as np
import jax
import jax.numpy as jnp
from jax import lax
from jax.experimental import pallas as pl
from jax.experimental.pallas import tpu as pltpu

F32 = jnp.float32
BF16 = jnp.bfloat16
I32 = jnp.int32

D_MODEL = 1024
GRID_W = 64
EPS = 1e-6
H_MLA = 4
Q_LORA = 256
KV_LORA = 128
NOPE_DIM = 64
ROPE_DIM = 32
V_DIM = 128
QK_DIM = NOPE_DIM + ROPE_DIM
ROPE_THETA = 10000.0
HEAD_PAD = 128
H_RWKV = 4
N_RWKV = 64
D_RWKV = H_RWKV * N_RWKV
W_LORA = 64
A_LORA = 64
G_LORA = 128
GN_EPS = 64e-5
D_LRU = 256
H_LRU = 4
BS_LRU = D_LRU // H_LRU
CONV_W = 4
LRU_C = 8.0
D_MLA_OUT = H_MLA * V_DIM
P_MLA_PAD = Q_LORA + KV_LORA + HEAD_PAD
P_RWKV = 3 * D_RWKV + 2 * W_LORA + 2 * A_LORA + G_LORA
P_LRU = 2 * D_LRU
N_EXPERTS = 64
TOP_K = 6
N_GROUPS = 8
TOPK_GROUPS = 4
GROUP_SIZE = N_EXPERTS // N_GROUPS
D_EXPERT = 256
ROUTED_SCALE = 2.5

SUBLANES = 8
RWKV_CHUNK = 64
RWKV_ROWS = 2
MOE_BLOCK = 512
ROUTER_TILE = 512
TOKEN_TILE = 256
SLOT_ROWS = 8
VMEM_LIMIT = 56 * 1024 * 1024


def _cparams(*sem):
    return pltpu.CompilerParams(dimension_semantics=sem, vmem_limit_bytes=VMEM_LIMIT)


def _sigmoid(x):
    return 1.0 / (1.0 + jnp.exp(-x))


def _softplus(x):
    return jnp.maximum(x, 0.0) + jnp.log1p(jnp.exp(-jnp.abs(x)))


def _expm1(z):
    e = jnp.exp(z)
    direct = (e == 1.0) | (z < -1.0)
    corrected = (e - 1.0) * z / jnp.log(jnp.where(direct, 2.0, e))
    return jnp.where(e == 1.0, z, jnp.where(z < -1.0, e - 1.0, corrected))


def _rms(x, g):
    return x * lax.rsqrt(jnp.mean(x * x, axis=-1, keepdims=True) + EPS) * g


def _split2(x):
    hi = x.astype(BF16)
    lo = (x - hi.astype(F32)).astype(BF16)
    return hi, lo


def _split3(x):
    hi = x.astype(BF16)
    r1 = x - hi.astype(F32)
    mid = r1.astype(BF16)
    lo = (r1 - mid.astype(F32)).astype(BF16)
    return hi, mid, lo


_NN = (((1,), (0,)), ((), ()))
_NT = (((1,), (1,)), ((), ()))


def _dg(a, b, dims):
    return lax.dot_general(a, b, dims, preferred_element_type=F32)


def _dot1(a, b, dims=_NN):
    return _dg(a.astype(BF16), b.astype(BF16), dims)


def _dot3(a, b, dims=_NN):
    ah, al = _split2(a)
    bh, bl = _split2(b)
    return _dg(ah, bh, dims) + (_dg(ah, bl, dims) + _dg(al, bh, dims))


def _dot_x(a, b_exact, dims=_NN):
    h, m, l = _split3(a)
    bb = b_exact.astype(BF16)
    return _dg(h, bb, dims) + (_dg(m, bb, dims) + _dg(l, bb, dims))


def _x_dot(a_exact, b, dims=_NN):
    h, m, l = _split3(b)
    aa = a_exact.astype(BF16)
    return _dg(aa, h, dims) + (_dg(aa, m, dims) + _dg(aa, l, dims))


def _head_ones(n, seg):
    r = lax.broadcasted_iota(I32, (n, n), 0) // seg
    c = lax.broadcasted_iota(I32, (n, n), 1) // seg
    return (r == c).astype(F32)


def _ada_body(c_ref, w_ref, b_ref, o_ref):
    cv = c_ref[...]
    s = cv * _sigmoid(cv)
    o_ref[0] = _dot3(s, w_ref[0]) + b_ref[0]


def _ada(cvecs, w_ada, b_ada):
    n_layers, d, d6 = w_ada.shape
    rows = cvecs.shape[0]
    nt = d6 // d
    return pl.pallas_call(
        _ada_body,
        grid=(n_layers, nt),
        in_specs=[
            pl.BlockSpec((rows, d), lambda l, j: (0, 0)),
            pl.BlockSpec((1, d, d), lambda l, j: (l, 0, j)),
            pl.BlockSpec((1, 1, d), lambda l, j: (l, 0, j)),
        ],
        out_specs=pl.BlockSpec((1, rows, d), lambda l, j: (l, 0, j)),
        out_shape=jax.ShapeDtypeStruct((n_layers, rows, d6), F32),
        compiler_params=_cparams("parallel", "parallel"),
        name="ada",
    )(cvecs, w_ada, b_ada.reshape(n_layers, 1, d6))


def _mod_spec(per_batch):
    if per_batch:
        return pl.BlockSpec((1, 1, D_MODEL), lambda b, i: (b, 0, 0))
    return pl.BlockSpec((1, 1, D_MODEL), lambda b, i: (0, 0, 0))


def _full(shape):
    nd = len(shape)
    return pl.BlockSpec(shape, lambda *_: (0,) * nd)


def _in_proj_body(x_ref, g_ref, sc_ref, sh_ref, w1_ref, w2_ref, w3_ref, o1_ref, o2_ref, o3_ref):
    h = _rms(x_ref[...], g_ref[...]) * (1.0 + sc_ref[0]) + sh_ref[0]
    hb = h.astype(BF16)
    o1_ref[0] = _dg(hb, w1_ref[...], _NN)
    o2_ref[0] = _dg(hb, w2_ref[...], _NN)
    o3_ref[0] = _dg(hb, w3_ref[...], _NN)


def _group_rows(grp, tt):
    row_off, nb, t = grp
    base, per = row_off // tt, t // tt
    return pl.BlockSpec((tt, D_MODEL), lambda b, i: (base + b * per + i, 0))


def _in_proj(x, grp, g, sc, sh, w_mla, w_rwkv, w_lru, tt):
    _, nb, t = grp
    d = x.shape[1]
    per_batch = sc.shape[0] > 1
    outs = [w_mla.shape[1], w_rwkv.shape[1], w_lru.shape[1]]
    return pl.pallas_call(
        _in_proj_body,
        grid=(nb, t // tt),
        in_specs=[
            _group_rows(grp, tt),
            _full((1, d)),
            _mod_spec(per_batch),
            _mod_spec(per_batch),
            _full(w_mla.shape),
            _full(w_rwkv.shape),
            _full(w_lru.shape),
        ],
        out_specs=[pl.BlockSpec((1, tt, n), lambda b, i: (b, i, 0)) for n in outs],
        out_shape=[jax.ShapeDtypeStruct((nb, t, n), F32) for n in outs],
        compiler_params=_cparams("parallel", "parallel"),
        name="in_proj",
    )(x, g, sc, sh, w_mla, w_rwkv, w_lru)


def _rope_lanes():
    lane = lax.broadcasted_iota(I32, (1, HEAD_PAD), 1)
    return ((lane >= NOPE_DIM) & (lane < QK_DIM)).astype(F32)


def _mla_prep_body(rope, *refs):
    if rope:
        (p_ref, gq_ref, gkv_ref, wq_ref, wqr_ref, wk_ref, wv_ref, cq_ref, sq_ref, cr_ref, sr_ref,
         q_ref, k_ref, v_ref) = refs
    else:
        p_ref, gq_ref, gkv_ref, wq_ref, wk_ref, wv_ref, q_ref, k_ref, v_ref, c_ref = refs
    p = p_ref[0]
    cq = _rms(p[:, :Q_LORA], gq_ref[...])
    ckv = _rms(p[:, Q_LORA:Q_LORA + KV_LORA], gkv_ref[...])
    kr = p[:, Q_LORA + KV_LORA:]
    cqb = cq.astype(BF16)
    ckvb = ckv.astype(BF16)
    scale = QK_DIM ** -0.5 * float(np.log2(np.e))
    qa = _dg(cqb, wq_ref[...], _NN) * scale
    ka = _dg(ckvb, wk_ref[...], _NN)
    va = _dg(ckvb, wv_ref[...], _NN)
    if rope:
        qr = _dg(cqb, wqr_ref[...], _NN) * scale
        krp = kr * cr_ref[...] + pltpu.roll(kr, ROPE_DIM, 1) * sr_ref[...]
    else:
        krp = kr * _rope_lanes()
    for h in range(H_MLA):
        sl = slice(h * HEAD_PAD, (h + 1) * HEAD_PAD)
        qh = qa[:, sl]
        if rope:
            qh = qh * cq_ref[...] + qr[:, sl] * sq_ref[...]
        q_ref[0, h] = qh.astype(BF16)
        k_ref[0, h] = (ka[:, sl] + krp).astype(BF16)
        v_ref[0, h] = va[:, sl].astype(BF16)
    if not rope:
        c_ref[0, :, :KV_LORA] = ckv
        c_ref[0, :, KV_LORA:] = pltpu.roll(kr, HEAD_PAD - NOPE_DIM, 1)[:, :ROPE_DIM]


def _mla_prep(p_mla, gq, gkv, wq, wqr, wk, wv, tables, tt):
    nb, t, pw = p_mla.shape
    rope = tables is not None
    hw = H_MLA * HEAD_PAD
    head_spec = pl.BlockSpec((1, H_MLA, tt, HEAD_PAD), lambda b, i: (b, 0, i, 0))
    head_shape = jax.ShapeDtypeStruct((nb, H_MLA, t, HEAD_PAD), BF16)
    in_specs = [pl.BlockSpec((1, tt, pw), lambda b, i: (b, i, 0)), _full(gq.shape), _full(gkv.shape), _full(wq.shape)]
    args = [p_mla, gq, gkv, wq]
    if rope:
        in_specs.append(_full(wqr.shape))
        args.append(wqr)
    in_specs += [_full(wk.shape), _full(wv.shape)]
    args += [wk, wv]
    out_specs = [head_spec, head_spec, head_spec]
    out_shape = [head_shape, head_shape, head_shape]
    if rope:
        in_specs += [pl.BlockSpec((tt, HEAD_PAD), lambda b, i: (i, 0))] * 4
        args += list(tables)
    else:
        out_specs.append(pl.BlockSpec((1, tt, KV_LORA + ROPE_DIM), lambda b, i: (b, i, 0)))
        out_shape.append(jax.ShapeDtypeStruct((nb, t, KV_LORA + ROPE_DIM), F32))
    del hw
    return pl.pallas_call(
        functools.partial(_mla_prep_body, rope),
        grid=(nb, t // tt),
        in_specs=in_specs,
        out_specs=out_specs,
        out_shape=out_shape,
        compiler_params=_cparams("parallel", "parallel"),
        name="mla_prep_rope" if rope else "mla_prep",
    )(*args)


def _ctx_kv_body(lat_ref, kr_ref, wk_ref, wv_ref, k_ref, v_ref):
    latb = lat_ref[0].astype(BF16)
    ka = _dg(latb, wk_ref[...], _NN)
    va = _dg(latb, wv_ref[...], _NN)
    kr = kr_ref[0]
    for h in range(H_MLA):
        sl = slice(h * HEAD_PAD, (h + 1) * HEAD_PAD)
        k_ref[0, h] = (ka[:, sl] + kr).astype(BF16)
        v_ref[0, h] = va[:, sl].astype(BF16)


def _ctx_kv(lat, kr_placed, wk, wv):
    nb, s, _ = lat.shape
    head_spec = pl.BlockSpec((1, H_MLA, s, HEAD_PAD), lambda b: (b, 0, 0, 0))
    head_shape = jax.ShapeDtypeStruct((nb, H_MLA, s, HEAD_PAD), BF16)
    return pl.pallas_call(
        _ctx_kv_body,
        grid=(nb,),
        in_specs=[
            pl.BlockSpec((1, s, KV_LORA), lambda b: (b, 0, 0)),
            pl.BlockSpec((1, s, HEAD_PAD), lambda b: (b, 0, 0)),
            _full(wk.shape),
            _full(wv.shape),
        ],
        out_specs=[head_spec, head_spec],
        out_shape=[head_shape, head_shape],
        compiler_params=_cparams("parallel"),
        name="ctx_kv",
    )(lat, kr_placed, wk, wv)


def _attn_body(has_ctx, *refs):
    if has_ctx:
        q_ref, k_ref, v_ref, kc_ref, vc_ref, o_ref = refs
    else:
        q_ref, k_ref, v_ref, o_ref = refs
    q = q_ref[0, 0]
    s = _dg(q, k_ref[0, 0], _NT)
    m = jnp.max(s, axis=-1, keepdims=True)
    if has_ctx:
        sc = _dg(q, kc_ref[0, 0], _NT)
        m = jnp.maximum(m, jnp.max(sc, axis=-1, keepdims=True))
    p = jnp.exp2(s - m)
    l = jnp.sum(p, axis=-1, keepdims=True)
    o = _dg(p.astype(BF16), v_ref[0, 0], _NN)
    if has_ctx:
        pc = jnp.exp2(sc - m)
        l = l + jnp.sum(pc, axis=-1, keepdims=True)
        o = o + _dg(pc.astype(BF16), vc_ref[0, 0], _NN)
    o_ref[0] = o / l


def _attention(q, k, v, kc, vc, tq):
    nb, nh, t, hd = q.shape
    s = k.shape[2]
    has_ctx = kc is not None
    in_specs = [
        pl.BlockSpec((1, 1, tq, hd), lambda b, h, i: (b, h, i, 0)),
        pl.BlockSpec((1, 1, s, hd), lambda b, h, i: (b, h, 0, 0)),
        pl.BlockSpec((1, 1, s, V_DIM), lambda b, h, i: (b, h, 0, 0)),
    ]
    args = [q, k, v]
    if has_ctx:
        sc = kc.shape[2]
        in_specs += [
            pl.BlockSpec((1, 1, sc, hd), lambda b, h, i: (b, h, 0, 0)),
            pl.BlockSpec((1, 1, sc, V_DIM), lambda b, h, i: (b, h, 0, 0)),
        ]
        args += [kc, vc]
    return pl.pallas_call(
        functools.partial(_attn_body, has_ctx),
        grid=(nb, nh, t // tq),
        in_specs=in_specs,
        out_specs=pl.BlockSpec((1, tq, V_DIM), lambda b, h, i: (b, i, h)),
        out_shape=jax.ShapeDtypeStruct((nb, t, nh * V_DIM), F32),
        compiler_params=_cparams("parallel", "parallel", "parallel"),
        name="attention",
    )(*args)


def _rwkv_prep_body(p_ref, hp_ref, hn_ref, mu_ref, kk_ref, ka_ref, rk_ref, w0_ref, w2_ref, a0_ref, a2_ref, g2_ref,
                    r_out, v_out, kk_out, lw_out, a_out, kd_out, g_out, bonus_out):
    p = p_ref[0]
    tt = p.shape[0]
    row = lax.broadcasted_iota(I32, (tt, 1), 0)
    prev = jnp.where(row == 0, hp_ref[0, 0], pltpu.roll(p, 1, 0))
    nxt = jnp.where(row == tt - 1, hn_ref[0, 0], pltpu.roll(p, tt - 1, 0))
    ps = p + (0.5 * (prev + nxt) - p) * mu_ref[...]
    r = ps[:, :D_RWKV]
    k = ps[:, D_RWKV:2 * D_RWKV]
    v = ps[:, 2 * D_RWKV:3 * D_RWKV]
    wl = ps[:, 3 * D_RWKV:3 * D_RWKV + 2 * W_LORA]
    al = ps[:, 3 * D_RWKV + 2 * W_LORA:3 * D_RWKV + 2 * W_LORA + 2 * A_LORA]
    gl = ps[:, 3 * D_RWKV + 2 * W_LORA + 2 * A_LORA:]
    ones = _head_ones(D_RWKV, N_RWKV)
    kk = k * kk_ref[...]
    kk = kk / jnp.maximum(jnp.sqrt(_dot_x(kk * kk, ones)), 1e-12)
    wlin = w0_ref[...] + _dot3(jnp.tanh(wl), w2_ref[...])
    lw = -jnp.exp(-_softplus(-wlin) - 0.5)
    ag = _sigmoid(a0_ref[...] + _dot3(al, a2_ref[...]))
    bonus = jnp.zeros_like(v)
    for d in range(2):
        sl = slice(d * D_RWKV, (d + 1) * D_RWKV)
        kd = k * (1.0 + (ag[:, sl] - 1.0) * ka_ref[...])
        kd_out[0, :, sl] = kd
        bonus = bonus + _dot_x(r * kd * rk_ref[...], ones) * v
    r_out[0] = r
    v_out[0] = v
    kk_out[0] = kk
    lw_out[0] = lw
    a_out[0] = ag
    g_out[0] = _dot1(_sigmoid(gl), g2_ref[...])
    bonus_out[0] = bonus


def _halo_rows(x, tt, width):
    nb, t, c = x.shape
    nt = t // tt
    xt = x.reshape(nb, nt, tt, c)
    zero = jnp.zeros((nb, 1, width, c), x.dtype)
    prev = jnp.concatenate([zero, xt[:, :-1, tt - width:]], axis=1)
    nxt = jnp.concatenate([xt[:, 1:, :width], zero], axis=1)
    return prev, nxt


def _rwkv_prep(p_rwkv, lp, tt):
    nb, t, pw = p_rwkv.shape
    hp, hn = _halo_rows(p_rwkv, tt, 1)
    tok = lambda n: pl.BlockSpec((1, tt, n), lambda b, i: (b, i, 0))
    halo = pl.BlockSpec((1, 1, 1, pw), lambda b, i: (b, i, 0, 0))
    small = [lp['mu_rwkv'], lp['k_k'], lp['k_a'], lp['r_k'], lp['w0_cat'], lp['w2_bd'], lp['a0_cat'], lp['a2_bd'],
             lp['g2_rwkv']]
    widths = [D_RWKV, D_RWKV, D_RWKV, 2 * D_RWKV, 2 * D_RWKV, 2 * D_RWKV, D_RWKV, D_RWKV]
    return pl.pallas_call(
        _rwkv_prep_body,
        grid=(nb, t // tt),
        in_specs=[tok(pw), halo, halo] + [_full(a.shape) for a in small],
        out_specs=[tok(n) for n in widths],
        out_shape=[jax.ShapeDtypeStruct((nb, t, n), F32) for n in widths],
        compiler_params=_cparams("parallel", "parallel"),
        name="rwkv_prep",
    )(p_rwkv, hp, hn, *small)


def _rwkv_chunk(reverse, r, v, kk, lw, ag, kd, s_old):
    cl = r.shape[0]
    hw = D_RWKV
    nh = H_RWKV * cl

    def later(rows, cols, strict):
        t = lax.broadcasted_iota(I32, (rows, cols), 0)
        s = lax.broadcasted_iota(I32, (rows, cols), 1) % cl
        dlt = (s - t) if reverse else (t - s)
        return (dlt > 0) if strict else (dlt >= 0)

    cum = _x_dot(later(cl, cl, False).astype(F32), lw)
    yield
    tot = jnp.sum(lw, axis=0, keepdims=True)
    e_neg = jnp.exp(-cum)
    e_rem = jnp.exp(tot - cum)
    at = -kk * jnp.exp(cum - lw)
    rt = r * jnp.exp(cum)
    b0 = kk * ag
    bt = b0 * e_neg
    kt = kd * e_neg
    bh = b0 * e_rem
    kh = kd * e_rem
    w_c = jnp.exp(tot)

    lane_head = lax.broadcasted_iota(I32, (1, hw), 1) // N_RWKV
    hms = [(lane_head == h).astype(F32) for h in range(H_RWKV)]
    stack = lambda z: jnp.concatenate([z * hm for hm in hms], axis=0)
    bks = jnp.concatenate([stack(bt), stack(kt)], axis=0)
    vs = stack(v)

    ls = _dot3(at, bks, _NT)
    ly = _dot1(rt, bks, _NT)
    yield
    strict = later(cl, nh, True)
    incl = later(cl, nh, False)
    l_ab = jnp.where(strict, ls[:, :nh], 0.0)
    l_ak = jnp.where(strict, ls[:, nh:], 0.0)
    l_rb = jnp.where(incl, ly[:, :nh], 0.0)
    l_rk = jnp.where(incl, ly[:, nh:], 0.0)

    x = jnp.concatenate([l_ab] * H_RWKV, axis=0) * _head_ones(nh, cl)
    eye = (lax.broadcasted_iota(I32, (nh, nh), 0) == lax.broadcasted_iota(I32, (nh, nh), 1)).astype(F32)
    inv = eye + x
    span = 2
    while span < cl:
        x = _dot3(x, x)
        yield
        inv = inv + _dot3(inv, x)
        yield
        span *= 2

    rhs = _dot3(at, s_old, _NT) + _dot3(l_ak, vs)
    yield
    us = _dot3(inv, stack(rhs))
    yield
    um = us[:cl]
    for h in range(1, H_RWKV):
        um = um + us[h * cl:(h + 1) * cl]
    y = _dot1(rt, s_old, _NT) + _dot1(jnp.concatenate([l_rb, l_rk], axis=1), jnp.concatenate([us, vs], axis=0))
    yield
    uv_t = jnp.concatenate([um, v], axis=0).T
    s_new = s_old * w_c + _dot3(uv_t, jnp.concatenate([bh, kh], axis=0)) * _head_ones(hw, N_RWKV)
    yield y, s_new


def _lockstep(gens):
    last = [None] * len(gens)
    live = list(range(len(gens)))
    while live:
        for i in list(live):
            try:
                last[i] = next(gens[i])
            except StopIteration:
                live.remove(i)
    return last


def _rwkv_scan_body(rf_ref, vf_ref, kkf_ref, lwf_ref, af_ref, kdf_ref, rb_ref, vb_ref, kkb_ref, lwb_ref, ab_ref,
                    kdb_ref, s0_ref, yf_ref, yb_ref, sf_ref, s_scr):
    c = pl.program_id(1)
    nc = pl.num_programs(1)

    @pl.when(c == 0)
    def _():
        s_scr[...] = s0_ref[...]

    nrow = rf_ref.shape[0]
    chains = []
    for i in range(nrow):
        chains.append(_rwkv_chunk(False, rf_ref[i], vf_ref[i], kkf_ref[i], lwf_ref[i], af_ref[i], kdf_ref[i],
                                  s_scr[i, 0]))
        chains.append(_rwkv_chunk(True, rb_ref[i], vb_ref[i], kkb_ref[i], lwb_ref[i], ab_ref[i], kdb_ref[i],
                                  s_scr[i, 1]))
    done = _lockstep(chains)
    for i in range(nrow):
        (y_f, s_f), (y_b, s_b) = done[2 * i], done[2 * i + 1]
        yf_ref[i] = y_f
        yb_ref[i] = y_b
        s_scr[i, 0] = s_f
        s_scr[i, 1] = s_b

    @pl.when(c == nc - 1)
    def _():
        sf_ref[...] = s_scr[...]


def _rwkv_scan(r, v, kk, lw, ag, kd, s0):
    nb, t, hw = r.shape
    cl = RWKV_CHUNK
    nr = RWKV_ROWS
    nc = t // cl
    fwd = lambda lane: pl.BlockSpec((nr, cl, hw), lambda b, c: (b, c, lane))
    bwd = lambda lane: pl.BlockSpec((nr, cl, hw), lambda b, c: (b, nc - 1 - c, lane))
    state = pl.BlockSpec((nr, 2, hw, hw), lambda b, c: (b, 0, 0, 0))
    y_shape = jax.ShapeDtypeStruct((nb, t, hw), F32)
    return pl.pallas_call(
        _rwkv_scan_body,
        grid=(nb // nr, nc),
        in_specs=[fwd(0)] * 6 + [bwd(0)] * 3 + [bwd(1)] * 3 + [state],
        out_specs=[fwd(0), bwd(0), state],
        out_shape=[y_shape, y_shape, jax.ShapeDtypeStruct((nb, 2, hw, hw), F32)],
        scratch_shapes=[pltpu.VMEM((nr, 2, hw, hw), F32)],
        compiler_params=_cparams("parallel", "arbitrary"),
        name="rwkv_scan",
    )(r, v, kk, lw, ag, kd, r, v, kk, lw, ag, kd, s0)


def _lru_body(reverse, p_ref, halo_ref, cw_ref, cb_ref, wa_ref, ba_ref, wx_ref, bx_ref, lam_ref, h0_ref,
              h_out, fin_out, a_scr, u_scr, carry_scr):
    i = pl.program_id(1)
    nt = pl.num_programs(1)
    tt = p_ref.shape[1]
    n_ext = tt + SUBLANES

    @pl.when(i == 0)
    def _():
        carry_scr[...] = jnp.broadcast_to(h0_ref[0], (SUBLANES, D_LRU))

    xb = p_ref[0, :, :D_LRU]
    halo = halo_ref[0, 0]
    xc = cb_ref[...] + jnp.zeros_like(xb)
    if reverse:
        xe = jnp.concatenate([xb, halo], axis=0)
        for k in range(CONV_W):
            sh = xb if k == 0 else pltpu.roll(xe, n_ext - k, 0)[:tt]
            xc = xc + cw_ref[k:k + 1, :] * sh
    else:
        xe = jnp.concatenate([halo, xb], axis=0)
        for k in range(CONV_W):
            s = CONV_W - 1 - k
            sh = xb if s == 0 else pltpu.roll(xe, s, 0)[SUBLANES:]
            xc = xc + cw_ref[k:k + 1, :] * sh
    gate_r = _sigmoid(_dot3(xc, wa_ref[...]) + ba_ref[...])
    gate_i = _sigmoid(_dot3(xc, wx_ref[...]) + bx_ref[...])
    log_a = -LRU_C * gate_r * _softplus(-lam_ref[...])
    a = jnp.exp(log_a)
    u = jnp.sqrt(-_expm1(2.0 * log_a)) * (gate_i * xc)

    rowm = lax.broadcasted_iota(I32, (tt, 1), 0) % SUBLANES
    for s in (1, 2, 4):
        if reverse:
            a_s = pltpu.roll(a, tt - s, 0)
            u_s = pltpu.roll(u, tt - s, 0)
            m = rowm < SUBLANES - s
        else:
            a_s = pltpu.roll(a, s, 0)
            u_s = pltpu.roll(u, s, 0)
            m = rowm >= s
        u = jnp.where(m, a * u_s + u, u)
        a = jnp.where(m, a * a_s, a)
    a_scr[...] = a
    u_scr[...] = u
    ng = tt // SUBLANES

    def group(g, carry):
        gi = (ng - 1 - g) if reverse else g
        r0 = pl.multiple_of(gi * SUBLANES, SUBLANES)
        hg = a_scr[pl.ds(r0, SUBLANES), :] * carry + u_scr[pl.ds(r0, SUBLANES), :]
        h_out[0, pl.ds(r0, SUBLANES), :] = hg
        edge = hg[0:1, :] if reverse else hg[SUBLANES - 1:SUBLANES, :]
        return jnp.broadcast_to(edge, (SUBLANES, D_LRU))

    carry = lax.fori_loop(0, ng, group, carry_scr[...])
    carry_scr[...] = carry

    @pl.when(i == nt - 1)
    def _():
        fin_out[0] = carry[0:1, :]


def _lru_dir(p_lru, halo, lp, d, h0, tt):
    nb, t, pw = p_lru.shape
    nt = t // tt
    reverse = d == 1
    tidx = (lambda i: nt - 1 - i) if reverse else (lambda i: i)
    small = [lp['conv_w'][d], lp['conv_b'][d:d + 1], lp['wa_bd'][d], lp['b_lru_a'][d:d + 1], lp['wx_bd'][d],
             lp['b_lru_x'][d:d + 1], lp['lam'][d:d + 1]]
    return pl.pallas_call(
        functools.partial(_lru_body, reverse),
        grid=(nb, nt),
        in_specs=[
            pl.BlockSpec((1, tt, pw), lambda b, i: (b, tidx(i), 0)),
            pl.BlockSpec((1, 1, SUBLANES, D_LRU), lambda b, i: (b, tidx(i), 0, 0)),
        ] + [_full(a.shape) for a in small] + [pl.BlockSpec((1, 1, D_LRU), lambda b, i: (b, 0, 0))],
        out_specs=[
            pl.BlockSpec((1, tt, D_LRU), lambda b, i: (b, tidx(i), 0)),
            pl.BlockSpec((1, 1, D_LRU), lambda b, i: (b, 0, 0)),
        ],
        out_shape=[jax.ShapeDtypeStruct((nb, t, D_LRU), F32), jax.ShapeDtypeStruct((nb, 1, D_LRU), F32)],
        scratch_shapes=[pltpu.VMEM((tt, D_LRU), F32), pltpu.VMEM((tt, D_LRU), F32), pltpu.VMEM((SUBLANES, D_LRU), F32)],
        compiler_params=_cparams("parallel", "arbitrary"),
        name="lru_bwd" if reverse else "lru_fwd",
    )(p_lru, halo, *small, h0)


def _gelu_tanh(x):
    return 0.5 * x * (1.0 + jnp.tanh(np.sqrt(2.0 / np.pi).astype(np.float32) * (x + 0.044715 * (x * x * x))))


def _out_proj_body(x_ref, o_ref, y0_ref, y1_ref, bonus_ref, g_ref, h0_ref, h1_ref, p_ref,
                   gmla_ref, lnw_ref, lnb_ref, glru_ref, wo_ref, gpost_ref, gate_ref, gpre_ref, sc_ref, sh_ref,
                   *rest):
    x1_out, h2_out = rest[-2:]
    o_mla = _rms(o_ref[0], gmla_ref[...])
    ones = _head_ones(D_RWKV, N_RWKV)
    y = y0_ref[0] + y1_ref[0]
    mu = _dot_x(y, ones) * (1.0 / N_RWKV)
    yc = y - mu
    var = _dot_x(yc * yc, ones) * (1.0 / N_RWKV)
    y = yc * lax.rsqrt(var + GN_EPS) * lnw_ref[...] + lnb_ref[...]
    o_rwkv = (y + bonus_ref[0]) * g_ref[0]
    o_lru = _rms((h0_ref[0] + h1_ref[0]) * _gelu_tanh(p_ref[0, :, D_LRU:]), glru_ref[...])
    m = (_dg(o_mla.astype(BF16), wo_ref[:D_MLA_OUT, :], _NN)
         + _dg(o_rwkv.astype(BF16), wo_ref[D_MLA_OUT:D_MLA_OUT + D_RWKV, :], _NN)
         + _dg(o_lru.astype(BF16), wo_ref[D_MLA_OUT + D_RWKV:, :], _NN))
    x1 = x_ref[...] + gate_ref[0] * _rms(m, gpost_ref[...])
    x1_out[...] = x1
    h2_out[...] = _rms(x1, gpre_ref[...]) * (1.0 + sc_ref[0]) + sh_ref[0]


def _out_proj(x, grp, o, y_f, y_b, bonus, g, h0, h1, p_lru, lp, mod, tt, prev):
    _, nb, t = grp
    per_batch = mod['g1'].shape[0] > 1
    tok = lambda n: pl.BlockSpec((1, tt, n), lambda b, i: (b, i, 0))
    rows = lambda: _group_rows(grp, tt)
    small = [lp['g_mla_out'], lp['ln_x_w'], lp['ln_x_b'], lp['g_lru_out'], lp['w_out'], lp['g_post_mix']]
    args = [x, o, y_f, y_b, bonus, g, h0, h1, p_lru, *small, mod['g1'], lp['g_pre_ffn'], mod['sc2'], mod['sh2']]
    in_specs = ([rows(), tok(D_MLA_OUT), tok(D_RWKV), tok(D_RWKV), tok(D_RWKV), tok(D_RWKV), tok(D_LRU), tok(D_LRU),
                 tok(P_LRU)] + [_full(a.shape) for a in small]
                + [_mod_spec(per_batch), _full(lp['g_pre_ffn'].shape), _mod_spec(per_batch), _mod_spec(per_batch)])
    aliases = {}
    if prev is not None:
        aliases = {len(args): 0, len(args) + 1: 1}
        args += list(prev)
        in_specs += [pl.BlockSpec(memory_space=pl.ANY)] * 2
    return pl.pallas_call(
        _out_proj_body,
        grid=(nb, t // tt),
        in_specs=in_specs,
        out_specs=[rows(), rows()],
        out_shape=[jax.ShapeDtypeStruct(x.shape, F32)] * 2,
        input_output_aliases=aliases,
        compiler_params=_cparams("parallel", "parallel"),
        name="out_proj",
    )(*args)


def _first_index(mask, idx, sentinel):
    return jnp.min(jnp.where(mask, idx, sentinel), axis=0, keepdims=True)


def _router_body(h_ref, wr_ref, br_ref, e_out, pos_out, w_out, cnt_out, carry):
    i = pl.program_id(0)
    tn = h_ref.shape[0]

    @pl.when(i == 0)
    def _():
        carry[...] = jnp.zeros_like(carry)

    logits = lax.dot_general(wr_ref[...], h_ref[...], _NT, precision=lax.Precision.HIGHEST,
                             preferred_element_type=F32)
    scores = _sigmoid(logits)
    sel = scores + br_ref[...]

    i8 = lax.broadcasted_iota(I32, (GROUP_SIZE, tn), 0)
    blocks, gscore = [], []
    for g in range(N_GROUPS):
        blk = sel[g * GROUP_SIZE:(g + 1) * GROUP_SIZE, :]
        m1 = jnp.max(blk, axis=0, keepdims=True)
        f1 = _first_index(blk == m1, i8, GROUP_SIZE)
        m2 = jnp.max(jnp.where(i8 == f1, -jnp.inf, blk), axis=0, keepdims=True)
        blocks.append(blk)
        gscore.append(m1 + m2)
    masked = []
    for g in range(N_GROUPS):
        beaten = jnp.zeros((1, tn), I32)
        for o in range(N_GROUPS):
            if o == g:
                continue
            wins = (gscore[o] >= gscore[g]) if o < g else (gscore[o] > gscore[g])
            beaten = beaten + wins.astype(I32)
        masked.append(jnp.where(beaten < TOPK_GROUPS, blocks[g], -jnp.inf))
    cur = jnp.concatenate(masked, axis=0)

    ie = lax.broadcasted_iota(I32, (N_EXPERTS, tn), 0)
    firsts, raw = [], []
    chosen = jnp.zeros((N_EXPERTS, tn), F32)
    for _ in range(TOP_K):
        mx = jnp.max(cur, axis=0, keepdims=True)
        first = _first_index(cur == mx, ie, N_EXPERTS)
        hit = ie == first
        firsts.append(first)
        raw.append(jnp.sum(jnp.where(hit, scores, 0.0), axis=0, keepdims=True))
        cur = jnp.where(hit, -jnp.inf, cur)
        chosen = chosen + hit.astype(F32)
    total = raw[0]
    for k in range(1, TOP_K):
        total = total + raw[k]

    before = (lax.broadcasted_iota(I32, (tn, tn), 0) < lax.broadcasted_iota(I32, (tn, tn), 1)).astype(BF16)
    rank = _dg(chosen.astype(BF16), before, _NN) + carry[:, 0:1]
    e_out[...] = jnp.zeros_like(e_out)
    pos_out[...] = jnp.zeros_like(pos_out)
    w_out[...] = jnp.zeros_like(w_out)
    for k in range(TOP_K):
        e_out[0, k:k + 1, :] = firsts[k]
        pos_out[0, k:k + 1, :] = jnp.sum(jnp.where(ie == firsts[k], rank, 0.0), axis=0, keepdims=True).astype(I32)
        w_out[k:k + 1, :] = raw[k] / total * ROUTED_SCALE
    new_carry = carry[...] + jnp.sum(chosen, axis=1, keepdims=True)
    carry[...] = new_carry
    cnt_out[...] = new_carry


def _router(h2, wr_t, b_col):
    n, d = h2.shape
    tn = ROUTER_TILE
    row_spec = pl.BlockSpec((SLOT_ROWS, tn), lambda i: (0, i))
    tile_spec = pl.BlockSpec((1, SLOT_ROWS, tn), lambda i: (i, 0, 0))
    cnt_spec = pl.BlockSpec((N_EXPERTS, 128), lambda i: (0, 0))
    tile_shape = jax.ShapeDtypeStruct((n // tn, SLOT_ROWS, tn), I32)
    return pl.pallas_call(
        _router_body,
        grid=(n // tn,),
        in_specs=[pl.BlockSpec((tn, d), lambda i: (i, 0)), _full(wr_t.shape), _full(b_col.shape)],
        out_specs=[tile_spec, tile_spec, row_spec, cnt_spec],
        out_shape=[tile_shape, tile_shape,
                   jax.ShapeDtypeStruct((SLOT_ROWS, n), F32), jax.ShapeDtypeStruct((N_EXPERTS, 128), F32)],
        scratch_shapes=[pltpu.VMEM((N_EXPERTS, 128), F32)],
        compiler_params=_cparams("arbitrary"),
        name="router",
    )(h2, wr_t, b_col)


ROW_LANES = 128


def _to_row_tiles(ref, x):
    rows = x.shape[0]
    for s in range(SUBLANES):
        ref[pl.ds(s, rows, stride=SUBLANES), :] = x[:, s * ROW_LANES:(s + 1) * ROW_LANES]


def _from_row_tiles(ref, rows):
    return jnp.concatenate([ref[pl.ds(s, rows, stride=SUBLANES), :] for s in range(SUBLANES)], axis=1)


def _row_copy(src, src_row8, dst, dst_row8, sem):
    return pltpu.make_async_copy(src.at[pl.ds(pl.multiple_of(src_row8, SUBLANES), SUBLANES), :],
                                 dst.at[pl.ds(pl.multiple_of(dst_row8, SUBLANES), SUBLANES), :], sem)


def _load_slots(slot_ref, slot_smem, sem):
    cp = pltpu.make_async_copy(slot_ref.at[0], slot_smem, sem)
    cp.start()
    cp.wait()


def _dispatch_body(start_ref, e_ref, pos_ref, h_ref, xs_in, slots_out, xs_out, slot_smem, h_tiles, sem_s, sem):
    del xs_in
    tn = slot_smem.shape[1]
    e = e_ref[0]
    slot = pos_ref[0]
    for x in range(N_EXPERTS):
        slot = slot + jnp.where(e == x, start_ref[x], 0)
    slots_out[0] = slot * SUBLANES
    _load_slots(slots_out, slot_smem, sem_s)
    _to_row_tiles(h_tiles, h_ref[...])

    def issue(j, carry):
        for k in range(TOP_K):
            _row_copy(h_tiles, j * SUBLANES, xs_out, slot_smem[k, j], sem).start(priority=k % 2)
        return carry

    def drain(j, carry):
        for k in range(TOP_K):
            _row_copy(h_tiles, 0, xs_out, 0, sem).wait()
        return carry

    lax.fori_loop(0, tn, issue, 0)
    lax.fori_loop(0, tn, drain, 0)


def _zero_tail_body(lb_ref, o_ref):
    del lb_ref
    o_ref[...] = jnp.zeros_like(o_ref)


def _zero_tails(last_block, n_blocks):
    rows = MOE_BLOCK * SUBLANES
    grid_spec = pltpu.PrefetchScalarGridSpec(
        num_scalar_prefetch=1,
        grid=(last_block.shape[0],),
        in_specs=[],
        out_specs=pl.BlockSpec((rows, ROW_LANES), lambda e, lb: (lb[e], 0)),
    )
    return pl.pallas_call(
        _zero_tail_body,
        grid_spec=grid_spec,
        out_shape=jax.ShapeDtypeStruct((n_blocks * rows, ROW_LANES), F32),
        compiler_params=_cparams("arbitrary"),
        name="zero_tails",
    )(last_block)


def _dispatch(row_start, e3, pos3, h2, xs_zero, tn):
    n, d = h2.shape
    ratio = e3.shape[2] // tn
    tile = lambda: pl.BlockSpec((1, SLOT_ROWS, tn), lambda i, st: (i // ratio, 0, i % ratio))
    grid_spec = pltpu.PrefetchScalarGridSpec(
        num_scalar_prefetch=1,
        grid=(n // tn,),
        in_specs=[tile(), tile(), pl.BlockSpec((tn, d), lambda i, st: (i, 0)), pl.BlockSpec(memory_space=pl.ANY)],
        out_specs=[pl.BlockSpec((1, SLOT_ROWS, tn), lambda i, st: (i, 0, 0)), pl.BlockSpec(memory_space=pl.ANY)],
        scratch_shapes=[pltpu.SMEM((SLOT_ROWS, tn), I32), pltpu.VMEM((tn * SUBLANES, ROW_LANES), F32),
                        pltpu.SemaphoreType.DMA, pltpu.SemaphoreType.DMA],
    )
    return pl.pallas_call(
        _dispatch_body,
        grid_spec=grid_spec,
        out_shape=[jax.ShapeDtypeStruct((n // tn, SLOT_ROWS, tn), I32),
                   jax.ShapeDtypeStruct(xs_zero.shape, xs_zero.dtype)],
        input_output_aliases={4: 1},
        compiler_params=_cparams("arbitrary"),
        name="dispatch",
    )(row_start, e3, pos3, h2, xs_zero)


def _swiglu(x, wg, wu, wd):
    g = _dg(x, wg, _NN)
    u = _dg(x, wu, _NN)
    return _dg((g * _sigmoid(g) * u).astype(BF16), wd, _NN)


def _expert_body(be_ref, nu_ref, xs_ref, wg_ref, wu_ref, wd_ref, ys_ref):
    del be_ref
    b = pl.program_id(0)

    @pl.when(b < nu_ref[0])
    def _():
        x = _from_row_tiles(xs_ref, MOE_BLOCK).astype(BF16)
        _to_row_tiles(ys_ref, _swiglu(x, wg_ref[0], wu_ref[0], wd_ref[0]))

    @pl.when(b >= nu_ref[0])
    def _():
        ys_ref[...] = jnp.zeros_like(ys_ref)


def _experts(block_e, n_used, xs, wg, wu, wd):
    bm = MOE_BLOCK
    _, d, de = wg.shape
    rows = bm * SUBLANES
    grid_spec = pltpu.PrefetchScalarGridSpec(
        num_scalar_prefetch=2,
        grid=(xs.shape[0] // rows,),
        in_specs=[
            pl.BlockSpec((rows, ROW_LANES), lambda b, be, nu: (jnp.minimum(b, nu[0] - 1), 0)),
            pl.BlockSpec((1, d, de), lambda b, be, nu: (be[b], 0, 0)),
            pl.BlockSpec((1, d, de), lambda b, be, nu: (be[b], 0, 0)),
            pl.BlockSpec((1, de, d), lambda b, be, nu: (be[b], 0, 0)),
        ],
        out_specs=pl.BlockSpec((rows, ROW_LANES), lambda b, be, nu: (b, 0)),
    )
    return pl.pallas_call(
        _expert_body,
        grid_spec=grid_spec,
        out_shape=jax.ShapeDtypeStruct(xs.shape, F32),
        compiler_params=_cparams("arbitrary"),
        name="experts",
    )(block_e, n_used, xs, wg, wu, wd)


def _combine_body(slot_ref, w_ref, x1_ref, h2_ref, gate_ref, ys_hbm, wsg_ref, wsu_ref, wsd_ref, gpost_ref,
                  out_ref, slot_smem, buf, sem_s, sem):
    tn = slot_smem.shape[1]
    _load_slots(slot_ref, slot_smem, sem_s)

    def issue(j, carry):
        for k in range(TOP_K):
            _row_copy(ys_hbm, slot_smem[k, j], buf.at[k], j * SUBLANES, sem).start(priority=k % 2)
        return carry

    def drain(j, carry):
        for k in range(TOP_K):
            _row_copy(ys_hbm, 0, buf.at[k], 0, sem).wait()
        return carry

    lax.fori_loop(0, tn, issue, 0)
    acc = _swiglu(h2_ref[...].astype(BF16), wsg_ref[...], wsu_ref[...], wsd_ref[...])
    lax.fori_loop(0, tn, drain, 0)
    for k in range(TOP_K):
        acc = acc + _from_row_tiles(buf.at[k], tn) * w_ref[:, k:k + 1]
    out_ref[...] = x1_ref[...] + gate_ref[0] * _rms(acc, gpost_ref[...])


def _combine(slots3, w_tok, x1, h2, gate_rows, ys, wsg, wsu, wsd, gpost):
    ntiles, _, tn = slots3.shape
    n, d = x1.shape
    tok = pl.BlockSpec((tn, d), lambda i: (i, 0))
    return pl.pallas_call(
        _combine_body,
        grid=(ntiles,),
        in_specs=[pl.BlockSpec((1, SLOT_ROWS, tn), lambda i: (i, 0, 0)),
                  pl.BlockSpec((tn, SLOT_ROWS), lambda i: (i, 0)), tok, tok,
                  pl.BlockSpec((1, 1, d), lambda i: (i, 0, 0)),
                  pl.BlockSpec(memory_space=pl.ANY),
                  _full(wsg.shape), _full(wsu.shape), _full(wsd.shape), _full(gpost.shape)],
        out_specs=tok,
        out_shape=jax.ShapeDtypeStruct((n, d), F32),
        scratch_shapes=[pltpu.SMEM((SLOT_ROWS, tn), I32), pltpu.VMEM((TOP_K, tn * SUBLANES, ROW_LANES), F32),
                        pltpu.SemaphoreType.DMA, pltpu.SemaphoreType.DMA],
        compiler_params=_cparams("arbitrary"),
        name="combine",
    )(slots3, w_tok, x1, h2, gate_rows, ys, wsg, wsu, wsd, gpost)


def _moe(x1, h2, gate_rows, lp):
    n, d = h2.shape
    bm = MOE_BLOCK
    tn = TOKEN_TILE
    e3, pos3, w_rows, cnt = _router(h2, lp['wr_t'], lp['b_router_col'])
    counts = cnt[:, 0].astype(I32)
    blocks_per_e = (counts + bm - 1) // bm
    blk_end = jnp.cumsum(blocks_per_e)
    blk_start = blk_end - blocks_per_e
    n_blocks = -(-(n * TOP_K) // bm) + N_EXPERTS
    block_e = jnp.minimum(jnp.sum(blk_end[None, :] <= jnp.arange(n_blocks, dtype=I32)[:, None], axis=1),
                          N_EXPERTS - 1).astype(I32)
    n_used = blk_end[-1:].astype(I32)
    xs_tails = _zero_tails(jnp.clip(blk_end - 1, 0, n_blocks - 1).astype(I32), n_blocks)
    slots3, xs = _dispatch((blk_start * bm).astype(I32), e3, pos3, h2, xs_tails, tn)
    ys = _experts(block_e, n_used, xs, lp['w_e_gate'], lp['w_e_up'], lp['w_e_down'])
    return _combine(slots3, w_rows.T, x1, h2, gate_rows, ys, lp['w_s_gate'], lp['w_s_up'], lp['w_s_down'],
                    lp['g_post_ffn'])


def _rope_tables(n_tokens):
    rows = n_tokens // GRID_W
    row = jnp.repeat(jnp.arange(rows, dtype=F32), GRID_W)
    col = jnp.tile(jnp.arange(GRID_W, dtype=F32), rows)
    n_freq = ROPE_DIM // 4
    inv_freq = ROPE_THETA ** (-jnp.arange(n_freq, dtype=F32) / n_freq)
    ang_r = row[:, None] * inv_freq
    ang_c = col[:, None] * inv_freq
    ang = jnp.concatenate([ang_r, ang_r, ang_c, ang_c], axis=-1)
    cos, sin = jnp.cos(ang), jnp.sin(ang)
    pad = lambda z, fill: jnp.concatenate(
        [jnp.full((n_tokens, NOPE_DIM), fill, F32), z, jnp.zeros((n_tokens, HEAD_PAD - QK_DIM), F32)], axis=1)
    return pad(cos, 1.0), pad(sin, 0.0), pad(cos, 0.0), pad(sin, 0.0)


def _rot_cols(w):
    q = ROPE_DIM // 4
    perm = np.concatenate([np.arange(q, 2 * q), np.arange(0, q), np.arange(3 * q, 4 * q), np.arange(2 * q, 3 * q)])
    sign = np.concatenate([-np.ones(q), np.ones(q), -np.ones(q), np.ones(q)]).astype(np.float32)
    return w[..., perm] * sign


def _block_diag(blocks):
    h, m, n = blocks.shape[-3:]
    eye = jnp.eye(h, dtype=blocks.dtype)
    out = blocks[..., :, :, None, :] * eye[:, None, :, None]
    return out.reshape(blocks.shape[:-3] + (h * m, h * n))


def _diag_blocks(mat, h):
    m, n = mat.shape[-2] // h, mat.shape[-1] // h
    z = mat.reshape(mat.shape[:-2] + (h, m, h, n))
    return jnp.stack([z[..., i, :, i, :] for i in range(h)], axis=-3)


def _layer_operands(l, a):
    w_in = a['w_in'][l]
    d = w_in.shape[0]
    p_mla = Q_LORA + KV_LORA + ROPE_DIM
    w_kr = w_in[:, Q_LORA + KV_LORA:p_mla]
    z32 = jnp.zeros((d, ROPE_DIM), F32)
    w_mla = jnp.concatenate([w_in[:, :Q_LORA + KV_LORA], z32, _rot_cols(w_kr), w_kr, z32], axis=1)
    wq3 = a['w_q_b'][l].reshape(Q_LORA, H_MLA, QK_DIM)
    zq = lambda n: jnp.zeros((Q_LORA, H_MLA, n), F32)
    wq = jnp.concatenate([wq3, zq(HEAD_PAD - QK_DIM)], axis=2).reshape(Q_LORA, H_MLA * HEAD_PAD)
    wqr = jnp.concatenate([zq(NOPE_DIM), _rot_cols(wq3[:, :, NOPE_DIM:]), zq(HEAD_PAD - QK_DIM)],
                          axis=2).reshape(Q_LORA, H_MLA * HEAD_PAD)
    wkv3 = a['w_kv_b'][l].reshape(KV_LORA, H_MLA, NOPE_DIM + V_DIM)
    wk = jnp.concatenate([wkv3[:, :, :NOPE_DIM], jnp.zeros((KV_LORA, H_MLA, HEAD_PAD - NOPE_DIM), F32)],
                         axis=2).reshape(KV_LORA, H_MLA * HEAD_PAD)
    wv = wkv3[:, :, NOPE_DIM:].reshape(KV_LORA, H_MLA * V_DIM)
    row = lambda z: z.reshape(1, -1)
    two = lambda z: jnp.concatenate([z[0], z[1]], axis=-1)
    bd2 = lambda z: jnp.concatenate([jnp.concatenate([z[0], jnp.zeros_like(z[0])], axis=1),
                                     jnp.concatenate([jnp.zeros_like(z[1]), z[1]], axis=1)], axis=0)
    return {
        'g_pre_mix': row(a['g_pre_mix'][l]), 'g_post_mix': row(a['g_post_mix'][l]),
        'g_pre_ffn': row(a['g_pre_ffn'][l]), 'g_post_ffn': row(a['g_post_ffn'][l]),
        'w_mla': w_mla.astype(BF16),
        'w_rwkv': w_in[:, p_mla:p_mla + P_RWKV].astype(BF16),
        'w_lru': w_in[:, p_mla + P_RWKV:].astype(BF16),
        'g_q_a': row(a['g_q_a'][l]), 'g_kv_a': row(a['g_kv_a'][l]),
        'wq': wq.astype(BF16), 'wqr': wqr.astype(BF16), 'wk': wk.astype(BF16), 'wv': wv.astype(BF16),
        'g_mla_out': row(a['g_mla_out'][l]),
        'mu_rwkv': row(a['mu_rwkv'][l]), 'k_k': row(a['k_k'][l]), 'k_a': row(a['k_a'][l]), 'r_k': row(a['r_k'][l]),
        'w0_cat': row(two(a['w0_rwkv'][l])), 'w2_bd': bd2(a['w2_rwkv'][l]),
        'a0_cat': row(two(a['a0_rwkv'][l])), 'a2_bd': bd2(a['a2_rwkv'][l]),
        'g2_rwkv': a['g2_rwkv'][l], 'ln_x_w': row(a['ln_x_w'][l]), 'ln_x_b': row(a['ln_x_b'][l]),
        'conv_w': a['conv_w'][l], 'conv_b': a['conv_b'][l],
        'wa_bd': _block_diag(a['w_lru_a'][l]), 'b_lru_a': a['b_lru_a'][l],
        'wx_bd': _block_diag(a['w_lru_x'][l]), 'b_lru_x': a['b_lru_x'][l],
        'lam': a['lam'][l], 'g_lru_out': row(a['g_lru_out'][l]),
        'w_out': a['w_out'][l].astype(BF16),
        'wr_t': a['w_router'][l].T, 'b_router_col': a['b_router'][l].reshape(N_EXPERTS, 1),
        'w_e_gate': a['w_e_gate'][l].astype(BF16), 'w_e_up': a['w_e_up'][l].astype(BF16),
        'w_e_down': a['w_e_down'][l].astype(BF16),
        'w_s_gate': a['w_s_gate'][l].astype(BF16), 'w_s_up': a['w_s_up'][l].astype(BF16),
        'w_s_down': a['w_s_down'][l].astype(BF16),
    }


def _mixer(x, grp, mod, lp, ctx, tables, prev):
    _, nb, t = grp
    tt = min(t, 512)
    p_mla, p_rwkv, p_lru = _in_proj(x, grp, lp['g_pre_mix'], mod['sc1'], mod['sh1'], lp['w_mla'], lp['w_rwkv'],
                                    lp['w_lru'], tt)
    if ctx is None:
        q, k, v, new_cache = _mla_prep(p_mla, lp['g_q_a'], lp['g_kv_a'], lp['wq'], None, lp['wk'], lp['wv'], None, tt)
        kc = vc = None
        s0 = jnp.zeros((nb, 2, D_RWKV, D_RWKV), F32)
        h0 = jnp.zeros((nb, 2, 1, D_LRU), F32)
    else:
        cache, s_rwkv, s_lru = ctx
        q, k, v = _mla_prep(p_mla, lp['g_q_a'], lp['g_kv_a'], lp['wq'], lp['wqr'], lp['wk'], lp['wv'], tables, tt)
        new_cache = None
        kr_placed = jnp.pad(cache[..., KV_LORA:], ((0, 0), (0, 0), (NOPE_DIM, HEAD_PAD - QK_DIM)))
        kc, vc = _ctx_kv(cache[..., :KV_LORA], kr_placed, lp['wk'], lp['wv'])
        s0 = _block_diag(s_rwkv)
        h0 = s_lru[:, :, None, :]
    o = _attention(q, k, v, kc, vc, min(t, 256))

    r, vv, kk, lw, ag, kd, g, bonus = _rwkv_prep(p_rwkv, lp, tt)
    y_f, y_b, s_fin = _rwkv_scan(r, vv, kk, lw, ag, kd, s0)

    hp, hn = _halo_rows(p_lru[..., :D_LRU], tt, SUBLANES)
    h_f, fin_f = _lru_dir(p_lru, hp, lp, 0, h0[:, 0], tt)
    h_b, fin_b = _lru_dir(p_lru, hn, lp, 1, h0[:, 1], tt)

    x1, h2 = _out_proj(x, grp, o, y_f, y_b, bonus, g, h_f, h_b, p_lru, lp, mod, tt, prev)
    states = None
    if ctx is None:
        states = (new_cache, _diag_blocks(s_fin, H_RWKV), jnp.concatenate([fin_f, fin_b], axis=1))
    return x1, h2, states


def kernel(x_prompt, x_sample, c, cache_mla, state_rwkv, state_lru, c_ctx, w_ada, b_ada, g_pre_mix, g_post_mix, g_pre_ffn, g_post_ffn, w_in, g_q_a, w_q_b, g_kv_a, w_kv_b, g_mla_out, mu_rwkv, w0_rwkv, w2_rwkv, a0_rwkv, a2_rwkv, g2_rwkv, k_k, k_a, r_k, ln_x_w, ln_x_b, conv_w, conv_b, w_lru_a, b_lru_a, w_lru_x, b_lru_x, lam, g_lru_out, w_out, w_router, b_router, w_e_gate, w_e_up, w_e_down, w_s_gate, w_s_up, w_s_down):
    a = dict(w_in=w_in, g_pre_mix=g_pre_mix, g_post_mix=g_post_mix, g_pre_ffn=g_pre_ffn, g_post_ffn=g_post_ffn,
             g_q_a=g_q_a, w_q_b=w_q_b, g_kv_a=g_kv_a, w_kv_b=w_kv_b, g_mla_out=g_mla_out, mu_rwkv=mu_rwkv,
             w0_rwkv=w0_rwkv, w2_rwkv=w2_rwkv, a0_rwkv=a0_rwkv, a2_rwkv=a2_rwkv, g2_rwkv=g2_rwkv, k_k=k_k, k_a=k_a,
             r_k=r_k.reshape(r_k.shape[0], -1), ln_x_w=ln_x_w, ln_x_b=ln_x_b, conv_w=conv_w, conv_b=conv_b,
             w_lru_a=w_lru_a, b_lru_a=b_lru_a, w_lru_x=w_lru_x, b_lru_x=b_lru_x, lam=lam, g_lru_out=g_lru_out,
             w_out=w_out, w_router=w_router, b_router=b_router, w_e_gate=w_e_gate, w_e_up=w_e_up,
             w_e_down=w_e_down, w_s_gate=w_s_gate, w_s_up=w_s_up, w_s_down=w_s_down)
    n_layers = w_in.shape[0]
    nbp, tp, d = x_prompt.shape
    nbs, ts, _ = x_sample.shape
    rows = -(-(1 + nbs) // SUBLANES) * SUBLANES
    cvecs = jnp.concatenate([c_ctx[None, :], c, jnp.zeros((rows - 1 - nbs, d), F32)], axis=0)
    mods = _ada(cvecs, w_ada, b_ada)
    names = ('sh1', 'sc1', 'g1', 'sh2', 'sc2', 'g2')
    tables = _rope_tables(ts)
    tn = TOKEN_TILE

    grp_p, grp_s = (0, nbp, tp), (nbp * tp, nbs, ts)
    x = jnp.concatenate([x_prompt.reshape(-1, d), x_sample.reshape(-1, d)], axis=0)
    caches, rwkv_states, lru_states = [], [], []
    for l in range(n_layers):
        lp = _layer_operands(l, a)
        mod_p = {nm: mods[l, 0:1, i * d:(i + 1) * d][:, None, :] for i, nm in enumerate(names)}
        mod_s = {nm: mods[l, 1:1 + nbs, i * d:(i + 1) * d][:, None, :] for i, nm in enumerate(names)}
        x1, h2, states = _mixer(x, grp_p, mod_p, lp, None, None, None)
        x1, h2, _ = _mixer(x, grp_s, mod_s, lp, (cache_mla[:, l], state_rwkv[:, l], state_lru[:, l]), tables,
                           (x1, h2))
        caches.append(states[0])
        rwkv_states.append(states[1])
        lru_states.append(states[2])
        gate_rows = jnp.concatenate([jnp.repeat(mod_p['g2'], nbp * tp // tn, axis=0),
                                     jnp.repeat(mod_s['g2'], ts // tn, axis=0)], axis=0)
        x = _moe(x1, h2, gate_rows, lp)
    yp = x[:nbp * tp].reshape(nbp, tp, d)
    ys = x[nbp * tp:].reshape(nbs, ts, d)
    return (yp, ys, jnp.stack(caches, axis=1), jnp.stack(rwkv_states, axis=1), jnp.stack(lru_states, axis=1))
```

```python
import functools

import numpy as np
import jax
import jax.numpy as jnp
from jax import lax
from jax.experimental import pallas as pl
from jax.experimental.pallas import tpu as pltpu

F32 = jnp.float32
BF16 = jnp.bfloat16
I32 = jnp.int32

D_MODEL = 1024
GRID_W = 64
EPS = 1e-6
H_MLA = 4
Q_LORA = 256
KV_LORA = 128
NOPE_DIM = 64
ROPE_DIM = 32
V_DIM = 128
QK_DIM = NOPE_DIM + ROPE_DIM
ROPE_THETA = 10000.0
HEAD_PAD = 128
H_RWKV = 4
N_RWKV = 64
D_RWKV = H_RWKV * N_RWKV
W_LORA = 64
A_LORA = 64
G_LORA = 128
GN_EPS = 64e-5
D_LRU = 256
H_LRU = 4
BS_LRU = D_LRU // H_LRU
CONV_W = 4
LRU_C = 8.0
D_MLA_OUT = H_MLA * V_DIM
P_MLA_PAD = Q_LORA + KV_LORA + HEAD_PAD
P_RWKV = 3 * D_RWKV + 2 * W_LORA + 2 * A_LORA + G_LORA
P_LRU = 2 * D_LRU
N_EXPERTS = 64
TOP_K = 6
N_GROUPS = 8
TOPK_GROUPS = 4
GROUP_SIZE = N_EXPERTS // N_GROUPS
D_EXPERT = 256
ROUTED_SCALE = 2.5

SUBLANES = 8
RWKV_CHUNK = 64
RWKV_ROWS = 2
MOE_BLOCK = 512
ROUTER_TILE = 512
TOKEN_TILE = 256
SLOT_ROWS = 8
VMEM_LIMIT = 56 * 1024 * 1024


def _cparams(*sem):
    return pltpu.CompilerParams(dimension_semantics=sem, vmem_limit_bytes=VMEM_LIMIT)


def _sigmoid(x):
    return 1.0 / (1.0 + jnp.exp(-x))


def _softplus(x):
    return jnp.maximum(x, 0.0) + jnp.log1p(jnp.exp(-jnp.abs(x)))


def _expm1(z):
    e = jnp.exp(z)
    direct = (e == 1.0) | (z < -1.0)
    corrected = (e - 1.0) * z / jnp.log(jnp.where(direct, 2.0, e))
    return jnp.where(e == 1.0, z, jnp.where(z < -1.0, e - 1.0, corrected))


def _rms(x, g):
    return x * lax.rsqrt(jnp.mean(x * x, axis=-1, keepdims=True) + EPS) * g


def _split2(x):
    hi = x.astype(BF16)
    lo = (x - hi.astype(F32)).astype(BF16)
    return hi, lo


def _split3(x):
    hi = x.astype(BF16)
    r1 = x - hi.astype(F32)
    mid = r1.astype(BF16)
    lo = (r1 - mid.astype(F32)).astype(BF16)
    return hi, mid, lo


_NN = (((1,), (0,)), ((), ()))
_NT = (((1,), (1,)), ((), ()))


def _dg(a, b, dims):
    return lax.dot_general(a, b, dims, preferred_element_type=F32)


def _dot1(a, b, dims=_NN):
    return _dg(a.astype(BF16), b.astype(BF16), dims)


def _dot3(a, b, dims=_NN):
    ah, al = _split2(a)
    bh, bl = _split2(b)
    return _dg(ah, bh, dims) + (_dg(ah, bl, dims) + _dg(al, bh, dims))


def _dot_x(a, b_exact, dims=_NN):
    h, m, l = _split3(a)
    bb = b_exact.astype(BF16)
    return _dg(h, bb, dims) + (_dg(m, bb, dims) + _dg(l, bb, dims))


def _x_dot(a_exact, b, dims=_NN):
    h, m, l = _split3(b)
    aa = a_exact.astype(BF16)
    return _dg(aa, h, dims) + (_dg(aa, m, dims) + _dg(aa, l, dims))


def _head_ones(n, seg):
    r = lax.broadcasted_iota(I32, (n, n), 0) // seg
    c = lax.broadcasted_iota(I32, (n, n), 1) // seg
    return (r == c).astype(F32)


def _ada_body(c_ref, w_ref, b_ref, o_ref):
    cv = c_ref[...]
    s = cv * _sigmoid(cv)
    o_ref[0] = _dot3(s, w_ref[0]) + b_ref[0]


def _ada(cvecs, w_ada, b_ada):
    n_layers, d, d6 = w_ada.shape
    rows = cvecs.shape[0]
    nt = d6 // d
    return pl.pallas_call(
        _ada_body,
        grid=(n_layers, nt),
        in_specs=[
            pl.BlockSpec((rows, d), lambda l, j: (0, 0)),
            pl.BlockSpec((1, d, d), lambda l, j: (l, 0, j)),
            pl.BlockSpec((1, 1, d), lambda l, j: (l, 0, j)),
        ],
        out_specs=pl.BlockSpec((1, rows, d), lambda l, j: (l, 0, j)),
        out_shape=jax.ShapeDtypeStruct((n_layers, rows, d6), F32),
        compiler_params=_cparams("parallel", "parallel"),
        name="ada",
    )(cvecs, w_ada, b_ada.reshape(n_layers, 1, d6))


def _mod_spec(per_batch):
    if per_batch:
        return pl.BlockSpec((1, 1, D_MODEL), lambda b, i: (b, 0, 0))
    return pl.BlockSpec((1, 1, D_MODEL), lambda b, i: (0, 0, 0))


def _full(shape):
    nd = len(shape)
    return pl.BlockSpec(shape, lambda *_: (0,) * nd)


def _in_proj_body(x_ref, g_ref, sc_ref, sh_ref, w1_ref, w2_ref, w3_ref, o1_ref, o2_ref, o3_ref):
    h = _rms(x_ref[...], g_ref[...]) * (1.0 + sc_ref[0]) + sh_ref[0]
    hb = h.astype(BF16)
    o1_ref[0] = _dg(hb, w1_ref[...], _NN)
    o2_ref[0] = _dg(hb, w2_ref[...], _NN)
    o3_ref[0] = _dg(hb, w3_ref[...], _NN)


def _group_rows(grp, tt):
    row_off, nb, t = grp
    base, per = row_off // tt, t // tt
    return pl.BlockSpec((tt, D_MODEL), lambda b, i: (base + b * per + i, 0))


def _in_proj(x, grp, g, sc, sh, w_mla, w_rwkv, w_lru, tt):
    _, nb, t = grp
    d = x.shape[1]
    per_batch = sc.shape[0] > 1
    outs = [w_mla.shape[1], w_rwkv.shape[1], w_lru.shape[1]]
    return pl.pallas_call(
        _in_proj_body,
        grid=(nb, t // tt),
        in_specs=[
            _group_rows(grp, tt),
            _full((1, d)),
            _mod_spec(per_batch),
            _mod_spec(per_batch),
            _full(w_mla.shape),
            _full(w_rwkv.shape),
            _full(w_lru.shape),
        ],
        out_specs=[pl.BlockSpec((1, tt, n), lambda b, i: (b, i, 0)) for n in outs],
        out_shape=[jax.ShapeDtypeStruct((nb, t, n), F32) for n in outs],
        compiler_params=_cparams("parallel", "parallel"),
        name="in_proj",
    )(x, g, sc, sh, w_mla, w_rwkv, w_lru)


def _rope_lanes():
    lane = lax.broadcasted_iota(I32, (1, HEAD_PAD), 1)
    return ((lane >= NOPE_DIM) & (lane < QK_DIM)).astype(F32)


def _mla_prep_body(rope, *refs):
    if rope:
        (p_ref, gq_ref, gkv_ref, wq_ref, wqr_ref, wk_ref, wv_ref, cq_ref, sq_ref, cr_ref, sr_ref,
         q_ref, k_ref, v_ref) = refs
    else:
        p_ref, gq_ref, gkv_ref, wq_ref, wk_ref, wv_ref, q_ref, k_ref, v_ref, c_ref = refs
    p = p_ref[0]
    cq = _rms(p[:, :Q_LORA], gq_ref[...])
    ckv = _rms(p[:, Q_LORA:Q_LORA + KV_LORA], gkv_ref[...])
    kr = p[:, Q_LORA + KV_LORA:]
    cqb = cq.astype(BF16)
    ckvb = ckv.astype(BF16)
    scale = QK_DIM ** -0.5 * float(np.log2(np.e))
    qa = _dg(cqb, wq_ref[...], _NN) * scale
    ka = _dg(ckvb, wk_ref[...], _NN)
    va = _dg(ckvb, wv_ref[...], _NN)
    if rope:
        qr = _dg(cqb, wqr_ref[...], _NN) * scale
        krp = kr * cr_ref[...] + pltpu.roll(kr, ROPE_DIM, 1) * sr_ref[...]
    else:
        krp = kr * _rope_lanes()
    for h in range(H_MLA):
        sl = slice(h * HEAD_PAD, (h + 1) * HEAD_PAD)
        qh = qa[:, sl]
        if rope:
            qh = qh * cq_ref[...] + qr[:, sl] * sq_ref[...]
        q_ref[0, h] = qh.astype(BF16)
        k_ref[0, h] = (ka[:, sl] + krp).astype(BF16)
        v_ref[0, h] = va[:, sl].astype(BF16)
    if not rope:
        c_ref[0, :, :KV_LORA] = ckv
        c_ref[0, :, KV_LORA:] = pltpu.roll(kr, HEAD_PAD - NOPE_DIM, 1)[:, :ROPE_DIM]


def _mla_prep(p_mla, gq, gkv, wq, wqr, wk, wv, tables, tt):
    nb, t, pw = p_mla.shape
    rope = tables is not None
    hw = H_MLA * HEAD_PAD
    head_spec = pl.BlockSpec((1, H_MLA, tt, HEAD_PAD), lambda b, i: (b, 0, i, 0))
    head_shape = jax.ShapeDtypeStruct((nb, H_MLA, t, HEAD_PAD), BF16)
    in_specs = [pl.BlockSpec((1, tt, pw), lambda b, i: (b, i, 0)), _full(gq.shape), _full(gkv.shape), _full(wq.shape)]
    args = [p_mla, gq, gkv, wq]
    if rope:
        in_specs.append(_full(wqr.shape))
        args.append(wqr)
    in_specs += [_full(wk.shape), _full(wv.shape)]
    args += [wk, wv]
    out_specs = [head_spec, head_spec, head_spec]
    out_shape = [head_shape, head_shape, head_shape]
    if rope:
        in_specs += [pl.BlockSpec((tt, HEAD_PAD), lambda b, i: (i, 0))] * 4
        args += list(tables)
    else:
        out_specs.append(pl.BlockSpec((1, tt, KV_LORA + ROPE_DIM), lambda b, i: (b, i, 0)))
        out_shape.append(jax.ShapeDtypeStruct((nb, t, KV_LORA + ROPE_DIM), F32))
    del hw
    return pl.pallas_call(
        functools.partial(_mla_prep_body, rope),
        grid=(nb, t // tt),
        in_specs=in_specs,
        out_specs=out_specs,
        out_shape=out_shape,
        compiler_params=_cparams("parallel", "parallel"),
        name="mla_prep_rope" if rope else "mla_prep",
    )(*args)


def _ctx_kv_body(lat_ref, kr_ref, wk_ref, wv_ref, k_ref, v_ref):
    latb = lat_ref[0].astype(BF16)
    ka = _dg(latb, wk_ref[...], _NN)
    va = _dg(latb, wv_ref[...], _NN)
    kr = kr_ref[0]
    for h in range(H_MLA):
        sl = slice(h * HEAD_PAD, (h + 1) * HEAD_PAD)
        k_ref[0, h] = (ka[:, sl] + kr).astype(BF16)
        v_ref[0, h] = va[:, sl].astype(BF16)


def _ctx_kv(lat, kr_placed, wk, wv):
    nb, s, _ = lat.shape
    head_spec = pl.BlockSpec((1, H_MLA, s, HEAD_PAD), lambda b: (b, 0, 0, 0))
    head_shape = jax.ShapeDtypeStruct((nb, H_MLA, s, HEAD_PAD), BF16)
    return pl.pallas_call(
        _ctx_kv_body,
        grid=(nb,),
        in_specs=[
            pl.BlockSpec((1, s, KV_LORA), lambda b: (b, 0, 0)),
            pl.BlockSpec((1, s, HEAD_PAD), lambda b: (b, 0, 0)),
            _full(wk.shape),
            _full(wv.shape),
        ],
        out_specs=[head_spec, head_spec],
        out_shape=[head_shape, head_shape],
        compiler_params=_cparams("parallel"),
        name="ctx_kv",
    )(lat, kr_placed, wk, wv)


def _attn_body(has_ctx, *refs):
    if has_ctx:
        q_ref, k_ref, v_ref, kc_ref, vc_ref, o_ref = refs
    else:
        q_ref, k_ref, v_ref, o_ref = refs
    q = q_ref[0, 0]
    s = _dg(q, k_ref[0, 0], _NT)
    m = jnp.max(s, axis=-1, keepdims=True)
    if has_ctx:
        sc = _dg(q, kc_ref[0, 0], _NT)
        m = jnp.maximum(m, jnp.max(sc, axis=-1, keepdims=True))
    p = jnp.exp2(s - m)
    l = jnp.sum(p, axis=-1, keepdims=True)
    o = _dg(p.astype(BF16), v_ref[0, 0], _NN)
    if has_ctx:
        pc = jnp.exp2(sc - m)
        l = l + jnp.sum(pc, axis=-1, keepdims=True)
        o = o + _dg(pc.astype(BF16), vc_ref[0, 0], _NN)
    o_ref[0] = o / l


def _attention(q, k, v, kc, vc, tq):
    nb, nh, t, hd = q.shape
    s = k.shape[2]
    has_ctx = kc is not None
    in_specs = [
        pl.BlockSpec((1, 1, tq, hd), lambda b, h, i: (b, h, i, 0)),
        pl.BlockSpec((1, 1, s, hd), lambda b, h, i: (b, h, 0, 0)),
        pl.BlockSpec((1, 1, s, V_DIM), lambda b, h, i: (b, h, 0, 0)),
    ]
    args = [q, k, v]
    if has_ctx:
        sc = kc.shape[2]
        in_specs += [
            pl.BlockSpec((1, 1, sc, hd), lambda b, h, i: (b, h, 0, 0)),
            pl.BlockSpec((1, 1, sc, V_DIM), lambda b, h, i: (b, h, 0, 0)),
        ]
        args += [kc, vc]
    return pl.pallas_call(
        functools.partial(_attn_body, has_ctx),
        grid=(nb, nh, t // tq),
        in_specs=in_specs,
        out_specs=pl.BlockSpec((1, tq, V_DIM), lambda b, h, i: (b, i, h)),
        out_shape=jax.ShapeDtypeStruct((nb, t, nh * V_DIM), F32),
        compiler_params=_cparams("parallel", "parallel", "parallel"),
        name="attention",
    )(*args)


def _rwkv_prep_body(p_ref, hp_ref, hn_ref, mu_ref, kk_ref, ka_ref, rk_ref, w0_ref, w2_ref, a0_ref, a2_ref, g2_ref,
                    r_out, v_out, kk_out, lw_out, a_out, kd_out, g_out, bonus_out):
    p = p_ref[0]
    tt = p.shape[0]
    row = lax.broadcasted_iota(I32, (tt, 1), 0)
    prev = jnp.where(row == 0, hp_ref[0, 0], pltpu.roll(p, 1, 0))
    nxt = jnp.where(row == tt - 1, hn_ref[0, 0], pltpu.roll(p, tt - 1, 0))
    ps = p + (0.5 * (prev + nxt) - p) * mu_ref[...]
    r = ps[:, :D_RWKV]
    k = ps[:, D_RWKV:2 * D_RWKV]
    v = ps[:, 2 * D_RWKV:3 * D_RWKV]
    wl = ps[:, 3 * D_RWKV:3 * D_RWKV + 2 * W_LORA]
    al = ps[:, 3 * D_RWKV + 2 * W_LORA:3 * D_RWKV + 2 * W_LORA + 2 * A_LORA]
    gl = ps[:, 3 * D_RWKV + 2 * W_LORA + 2 * A_LORA:]
    ones = _head_ones(D_RWKV, N_RWKV)
    kk = k * kk_ref[...]
    kk = kk / jnp.maximum(jnp.sqrt(_dot_x(kk * kk, ones)), 1e-12)
    wlin = w0_ref[...] + _dot3(jnp.tanh(wl), w2_ref[...])
    lw = -jnp.exp(-_softplus(-wlin) - 0.5)
    ag = _sigmoid(a0_ref[...] + _dot3(al, a2_ref[...]))
    bonus = jnp.zeros_like(v)
    for d in range(2):
        sl = slice(d * D_RWKV, (d + 1) * D_RWKV)
        kd = k * (1.0 + (ag[:, sl] - 1.0) * ka_ref[...])
        kd_out[0, :, sl] = kd
        bonus = bonus + _dot_x(r * kd * rk_ref[...], ones) * v
    r_out[0] = r
    v_out[0] = v
    kk_out[0] = kk
    lw_out[0] = lw
    a_out[0] = ag
    g_out[0] = _dot1(_sigmoid(gl), g2_ref[...])
    bonus_out[0] = bonus


def _halo_rows(x, tt, width):
    nb, t, c = x.shape
    nt = t // tt
    xt = x.reshape(nb, nt, tt, c)
    zero = jnp.zeros((nb, 1, width, c), x.dtype)
    prev = jnp.concatenate([zero, xt[:, :-1, tt - width:]], axis=1)
    nxt = jnp.concatenate([xt[:, 1:, :width], zero], axis=1)
    return prev, nxt


def _rwkv_prep(p_rwkv, lp, tt):
    nb, t, pw = p_rwkv.shape
    hp, hn = _halo_rows(p_rwkv, tt, 1)
    tok = lambda n: pl.BlockSpec((1, tt, n), lambda b, i: (b, i, 0))
    halo = pl.BlockSpec((1, 1, 1, pw), lambda b, i: (b, i, 0, 0))
    small = [lp['mu_rwkv'], lp['k_k'], lp['k_a'], lp['r_k'], lp['w0_cat'], lp['w2_bd'], lp['a0_cat'], lp['a2_bd'],
             lp['g2_rwkv']]
    widths = [D_RWKV, D_RWKV, D_RWKV, 2 * D_RWKV, 2 * D_RWKV, 2 * D_RWKV, D_RWKV, D_RWKV]
    return pl.pallas_call(
        _rwkv_prep_body,
        grid=(nb, t // tt),
        in_specs=[tok(pw), halo, halo] + [_full(a.shape) for a in small],
        out_specs=[tok(n) for n in widths],
        out_shape=[jax.ShapeDtypeStruct((nb, t, n), F32) for n in widths],
        compiler_params=_cparams("parallel", "parallel"),
        name="rwkv_prep",
    )(p_rwkv, hp, hn, *small)


def _rwkv_chunk(reverse, r, v, kk, lw, ag, kd, s_old):
    cl = r.shape[0]
    hw = D_RWKV

    def later(rows, cols, strict):
        t = lax.broadcasted_iota(I32, (rows, cols), 0)
        s = lax.broadcasted_iota(I32, (rows, cols), 1) % cl
        dlt = (s - t) if reverse else (t - s)
        return (dlt > 0) if strict else (dlt >= 0)

    cum = _x_dot(later(cl, cl, False).astype(F32), lw)
    yield
    tot = jnp.sum(lw, axis=0, keepdims=True)
    e_neg = jnp.exp(-cum)
    e_rem = jnp.exp(tot - cum)
    at = -kk * jnp.exp(cum - lw)
    rt = r * jnp.exp(cum)
    b0 = kk * ag
    bt = b0 * e_neg
    kt = kd * e_neg
    bh = b0 * e_rem
    kh = kd * e_rem
    w_c = jnp.exp(tot)

    lane_head = lax.broadcasted_iota(I32, (1, hw), 1) // N_RWKV
    hms = [(lane_head == h).astype(F32) for h in range(H_RWKV)]
    heads = range(H_RWKV)
    bk = [jnp.concatenate([bt * hm, kt * hm], axis=0) for hm in hms]
    strict = later(cl, 2 * cl, True)
    incl = later(cl, 2 * cl, False)
    ls = [jnp.where(strict, _dot3(at, bk[h], _NT), 0.0) for h in heads]
    ly = [jnp.where(incl, _dot1(rt, bk[h], _NT), 0.0) for h in heads]
    yield

    x = [l[:, :cl] for l in ls]
    eye = (lax.broadcasted_iota(I32, (cl, cl), 0) == lax.broadcasted_iota(I32, (cl, cl), 1)).astype(F32)
    inv = [eye + xh for xh in x]
    span = 2
    while span < cl:
        x = [_dot3(xh, xh) for xh in x]
        yield
        inv = [ih + _dot3(ih, xh) for ih, xh in zip(inv, x)]
        yield
        span *= 2

    zero = jnp.zeros((cl, hw), F32)
    v_rows = jnp.concatenate([piece for hm in hms for piece in (zero, v * hm)], axis=0)
    rhs = _dot3(at, s_old, _NT) + _dot3(jnp.concatenate(ls, axis=1), v_rows)
    yield
    us = [_dot3(inv[h], rhs * hms[h]) for h in heads]
    yield
    um = us[0]
    for h in range(1, H_RWKV):
        um = um + us[h]
    uv_rows = jnp.concatenate([piece for h in heads for piece in (us[h], v * hms[h])], axis=0)
    y = _dot1(rt, s_old, _NT) + _dot1(jnp.concatenate(ly, axis=1), uv_rows)
    yield
    uv_t = jnp.concatenate([um, v], axis=0).T
    s_new = s_old * w_c + _dot3(uv_t, jnp.concatenate([bh, kh], axis=0)) * _head_ones(hw, N_RWKV)
    yield y, s_new


def _lockstep(gens):
    last = [None] * len(gens)
    live = list(range(len(gens)))
    while live:
        for i in list(live):
            try:
                last[i] = next(gens[i])
            except StopIteration:
                live.remove(i)
    return last


def _rwkv_scan_body(rf_ref, vf_ref, kkf_ref, lwf_ref, af_ref, kdf_ref, rb_ref, vb_ref, kkb_ref, lwb_ref, ab_ref,
                    kdb_ref, s0_ref, yf_ref, yb_ref, sf_ref, s_scr):
    c = pl.program_id(1)
    nc = pl.num_programs(1)

    @pl.when(c == 0)
    def _():
        s_scr[...] = s0_ref[...]

    nrow = rf_ref.shape[0]
    chains = []
    for i in range(nrow):
        chains.append(_rwkv_chunk(False, rf_ref[i], vf_ref[i], kkf_ref[i], lwf_ref[i], af_ref[i], kdf_ref[i],
                                  s_scr[i, 0]))
        chains.append(_rwkv_chunk(True, rb_ref[i], vb_ref[i], kkb_ref[i], lwb_ref[i], ab_ref[i], kdb_ref[i],
                                  s_scr[i, 1]))
    done = _lockstep(chains)
    for i in range(nrow):
        (y_f, s_f), (y_b, s_b) = done[2 * i], done[2 * i + 1]
        yf_ref[i] = y_f
        yb_ref[i] = y_b
        s_scr[i, 0] = s_f
        s_scr[i, 1] = s_b

    @pl.when(c == nc - 1)
    def _():
        sf_ref[...] = s_scr[...]


def _rwkv_scan(r, v, kk, lw, ag, kd, s0):
    nb, t, hw = r.shape
    cl = RWKV_CHUNK
    nr = RWKV_ROWS
    nc = t // cl
    fwd = lambda lane: pl.BlockSpec((nr, cl, hw), lambda b, c: (b, c, lane))
    bwd = lambda lane: pl.BlockSpec((nr, cl, hw), lambda b, c: (b, nc - 1 - c, lane))
    state = pl.BlockSpec((nr, 2, hw, hw), lambda b, c: (b, 0, 0, 0))
    y_shape = jax.ShapeDtypeStruct((nb, t, hw), F32)
    return pl.pallas_call(
        _rwkv_scan_body,
        grid=(nb // nr, nc),
        in_specs=[fwd(0)] * 6 + [bwd(0)] * 3 + [bwd(1)] * 3 + [state],
        out_specs=[fwd(0), bwd(0), state],
        out_shape=[y_shape, y_shape, jax.ShapeDtypeStruct((nb, 2, hw, hw), F32)],
        scratch_shapes=[pltpu.VMEM((nr, 2, hw, hw), F32)],
        compiler_params=_cparams("parallel", "arbitrary"),
        name="rwkv_scan",
    )(r, v, kk, lw, ag, kd, r, v, kk, lw, ag, kd, s0)


def _lru_body(reverse, p_ref, halo_ref, cw_ref, cb_ref, wa_ref, ba_ref, wx_ref, bx_ref, lam_ref, h0_ref,
              h_out, fin_out, a_scr, u_scr, carry_scr):
    i = pl.program_id(1)
    nt = pl.num_programs(1)
    tt = p_ref.shape[1]
    n_ext = tt + SUBLANES

    @pl.when(i == 0)
    def _():
        carry_scr[...] = jnp.broadcast_to(h0_ref[0], (SUBLANES, D_LRU))

    xb = p_ref[0, :, :D_LRU]
    halo = halo_ref[0, 0]
    xc = cb_ref[...] + jnp.zeros_like(xb)
    if reverse:
        xe = jnp.concatenate([xb, halo], axis=0)
        for k in range(CONV_W):
            sh = xb if k == 0 else pltpu.roll(xe, n_ext - k, 0)[:tt]
            xc = xc + cw_ref[k:k + 1, :] * sh
    else:
        xe = jnp.concatenate([halo, xb], axis=0)
        for k in range(CONV_W):
            s = CONV_W - 1 - k
            sh = xb if s == 0 else pltpu.roll(xe, s, 0)[SUBLANES:]
            xc = xc + cw_ref[k:k + 1, :] * sh
    gate_r = _sigmoid(_dot3(xc, wa_ref[...]) + ba_ref[...])
    gate_i = _sigmoid(_dot3(xc, wx_ref[...]) + bx_ref[...])
    log_a = -LRU_C * gate_r * _softplus(-lam_ref[...])
    a = jnp.exp(log_a)
    u = jnp.sqrt(-_expm1(2.0 * log_a)) * (gate_i * xc)

    rowm = lax.broadcasted_iota(I32, (tt, 1), 0) % SUBLANES
    for s in (1, 2, 4):
        if reverse:
            a_s = pltpu.roll(a, tt - s, 0)
            u_s = pltpu.roll(u, tt - s, 0)
            m = rowm < SUBLANES - s
        else:
            a_s = pltpu.roll(a, s, 0)
            u_s = pltpu.roll(u, s, 0)
            m = rowm >= s
        u = jnp.where(m, a * u_s + u, u)
        a = jnp.where(m, a * a_s, a)
    a_scr[...] = a
    u_scr[...] = u
    ng = tt // SUBLANES

    def group(g, carry):
        gi = (ng - 1 - g) if reverse else g
        r0 = pl.multiple_of(gi * SUBLANES, SUBLANES)
        hg = a_scr[pl.ds(r0, SUBLANES), :] * carry + u_scr[pl.ds(r0, SUBLANES), :]
        h_out[0, pl.ds(r0, SUBLANES), :] = hg
        edge = hg[0:1, :] if reverse else hg[SUBLANES - 1:SUBLANES, :]
        return jnp.broadcast_to(edge, (SUBLANES, D_LRU))

    carry = lax.fori_loop(0, ng, group, carry_scr[...])
    carry_scr[...] = carry

    @pl.when(i == nt - 1)
    def _():
        fin_out[0] = carry[0:1, :]


def _lru_dir(p_lru, halo, lp, d, h0, tt):
    nb, t, pw = p_lru.shape
    nt = t // tt
    reverse = d == 1
    tidx = (lambda i: nt - 1 - i) if reverse else (lambda i: i)
    small = [lp['conv_w'][d], lp['conv_b'][d:d + 1], lp['wa_bd'][d], lp['b_lru_a'][d:d + 1], lp['wx_bd'][d],
             lp['b_lru_x'][d:d + 1], lp['lam'][d:d + 1]]
    return pl.pallas_call(
        functools.partial(_lru_body, reverse),
        grid=(nb, nt),
        in_specs=[
            pl.BlockSpec((1, tt, pw), lambda b, i: (b, tidx(i), 0)),
            pl.BlockSpec((1, 1, SUBLANES, D_LRU), lambda b, i: (b, tidx(i), 0, 0)),
        ] + [_full(a.shape) for a in small] + [pl.BlockSpec((1, 1, D_LRU), lambda b, i: (b, 0, 0))],
        out_specs=[
            pl.BlockSpec((1, tt, D_LRU), lambda b, i: (b, tidx(i), 0)),
            pl.BlockSpec((1, 1, D_LRU), lambda b, i: (b, 0, 0)),
        ],
        out_shape=[jax.ShapeDtypeStruct((nb, t, D_LRU), F32), jax.ShapeDtypeStruct((nb, 1, D_LRU), F32)],
        scratch_shapes=[pltpu.VMEM((tt, D_LRU), F32), pltpu.VMEM((tt, D_LRU), F32), pltpu.VMEM((SUBLANES, D_LRU), F32)],
        compiler_params=_cparams("parallel", "arbitrary"),
        name="lru_bwd" if reverse else "lru_fwd",
    )(p_lru, halo, *small, h0)


def _gelu_tanh(x):
    return 0.5 * x * (1.0 + jnp.tanh(np.sqrt(2.0 / np.pi).astype(np.float32) * (x + 0.044715 * (x * x * x))))


def _out_proj_body(x_ref, o_ref, y0_ref, y1_ref, bonus_ref, g_ref, h0_ref, h1_ref, p_ref,
                   gmla_ref, lnw_ref, lnb_ref, glru_ref, wo_ref, gpost_ref, gate_ref, gpre_ref, sc_ref, sh_ref,
                   *rest):
    x1_out, h2_out = rest[-2:]
    o_mla = _rms(o_ref[0], gmla_ref[...])
    ones = _head_ones(D_RWKV, N_RWKV)
    y = y0_ref[0] + y1_ref[0]
    mu = _dot_x(y, ones) * (1.0 / N_RWKV)
    yc = y - mu
    var = _dot_x(yc * yc, ones) * (1.0 / N_RWKV)
    y = yc * lax.rsqrt(var + GN_EPS) * lnw_ref[...] + lnb_ref[...]
    o_rwkv = (y + bonus_ref[0]) * g_ref[0]
    o_lru = _rms((h0_ref[0] + h1_ref[0]) * _gelu_tanh(p_ref[0, :, D_LRU:]), glru_ref[...])
    m = (_dg(o_mla.astype(BF16), wo_ref[:D_MLA_OUT, :], _NN)
         + _dg(o_rwkv.astype(BF16), wo_ref[D_MLA_OUT:D_MLA_OUT + D_RWKV, :], _NN)
         + _dg(o_lru.astype(BF16), wo_ref[D_MLA_OUT + D_RWKV:, :], _NN))
    x1 = x_ref[...] + gate_ref[0] * _rms(m, gpost_ref[...])
    x1_out[...] = x1
    h2_out[...] = _rms(x1, gpre_ref[...]) * (1.0 + sc_ref[0]) + sh_ref[0]


def _out_proj(x, grp, o, y_f, y_b, bonus, g, h0, h1, p_lru, lp, mod, tt, prev):
    _, nb, t = grp
    per_batch = mod['g1'].shape[0] > 1
    tok = lambda n: pl.BlockSpec((1, tt, n), lambda b, i: (b, i, 0))
    rows = lambda: _group_rows(grp, tt)
    small = [lp['g_mla_out'], lp['ln_x_w'], lp['ln_x_b'], lp['g_lru_out'], lp['w_out'], lp['g_post_mix']]
    args = [x, o, y_f, y_b, bonus, g, h0, h1, p_lru, *small, mod['g1'], lp['g_pre_ffn'], mod['sc2'], mod['sh2']]
    in_specs = ([rows(), tok(D_MLA_OUT), tok(D_RWKV), tok(D_RWKV), tok(D_RWKV), tok(D_RWKV), tok(D_LRU), tok(D_LRU),
                 tok(P_LRU)] + [_full(a.shape) for a in small]
                + [_mod_spec(per_batch), _full(lp['g_pre_ffn'].shape), _mod_spec(per_batch), _mod_spec(per_batch)])
    aliases = {}
    if prev is not None:
        aliases = {len(args): 0, len(args) + 1: 1}
        args += list(prev)
        in_specs += [pl.BlockSpec(memory_space=pl.ANY)] * 2
    return pl.pallas_call(
        _out_proj_body,
        grid=(nb, t // tt),
        in_specs=in_specs,
        out_specs=[rows(), rows()],
        out_shape=[jax.ShapeDtypeStruct(x.shape, F32)] * 2,
        input_output_aliases=aliases,
        compiler_params=_cparams("parallel", "parallel"),
        name="out_proj",
    )(*args)


def _first_index(mask, idx, sentinel):
    return jnp.min(jnp.where(mask, idx, sentinel), axis=0, keepdims=True)


def _router_body(h_ref, wr_ref, br_ref, e_out, pos_out, w_out, cnt_out, carry):
    i = pl.program_id(0)
    tn = h_ref.shape[0]

    @pl.when(i == 0)
    def _():
        carry[...] = jnp.zeros_like(carry)

    logits = lax.dot_general(wr_ref[...], h_ref[...], _NT, precision=lax.Precision.HIGHEST,
                             preferred_element_type=F32)
    scores = _sigmoid(logits)
    sel = scores + br_ref[...]

    i8 = lax.broadcasted_iota(I32, (GROUP_SIZE, tn), 0)
    blocks, gscore = [], []
    for g in range(N_GROUPS):
        blk = sel[g * GROUP_SIZE:(g + 1) * GROUP_SIZE, :]
        m1 = jnp.max(blk, axis=0, keepdims=True)
        f1 = _first_index(blk == m1, i8, GROUP_SIZE)
        m2 = jnp.max(jnp.where(i8 == f1, -jnp.inf, blk), axis=0, keepdims=True)
        blocks.append(blk)
        gscore.append(m1 + m2)
    masked = []
    for g in range(N_GROUPS):
        beaten = jnp.zeros((1, tn), I32)
        for o in range(N_GROUPS):
            if o == g:
                continue
            wins = (gscore[o] >= gscore[g]) if o < g else (gscore[o] > gscore[g])
            beaten = beaten + wins.astype(I32)
        masked.append(jnp.where(beaten < TOPK_GROUPS, blocks[g], -jnp.inf))
    cur = jnp.concatenate(masked, axis=0)

    ie = lax.broadcasted_iota(I32, (N_EXPERTS, tn), 0)
    firsts, raw = [], []
    chosen = jnp.zeros((N_EXPERTS, tn), F32)
    for _ in range(TOP_K):
        mx = jnp.max(cur, axis=0, keepdims=True)
        first = _first_index(cur == mx, ie, N_EXPERTS)
        hit = ie == first
        firsts.append(first)
        raw.append(jnp.sum(jnp.where(hit, scores, 0.0), axis=0, keepdims=True))
        cur = jnp.where(hit, -jnp.inf, cur)
        chosen = chosen + hit.astype(F32)
    total = raw[0]
    for k in range(1, TOP_K):
        total = total + raw[k]

    before = (lax.broadcasted_iota(I32, (tn, tn), 0) < lax.broadcasted_iota(I32, (tn, tn), 1)).astype(BF16)
    rank = _dg(chosen.astype(BF16), before, _NN) + carry[:, 0:1]
    e_out[...] = jnp.zeros_like(e_out)
    pos_out[...] = jnp.zeros_like(pos_out)
    w_out[...] = jnp.zeros_like(w_out)
    for k in range(TOP_K):
        e_out[0, k:k + 1, :] = firsts[k]
        pos_out[0, k:k + 1, :] = jnp.sum(jnp.where(ie == firsts[k], rank, 0.0), axis=0, keepdims=True).astype(I32)
        w_out[k:k + 1, :] = raw[k] / total * ROUTED_SCALE
    new_carry = carry[...] + jnp.sum(chosen, axis=1, keepdims=True)
    carry[...] = new_carry
    cnt_out[...] = new_carry


def _router(h2, wr_t, b_col):
    n, d = h2.shape
    tn = ROUTER_TILE
    row_spec = pl.BlockSpec((SLOT_ROWS, tn), lambda i: (0, i))
    tile_spec = pl.BlockSpec((1, SLOT_ROWS, tn), lambda i: (i, 0, 0))
    cnt_spec = pl.BlockSpec((N_EXPERTS, 128), lambda i: (0, 0))
    tile_shape = jax.ShapeDtypeStruct((n // tn, SLOT_ROWS, tn), I32)
    return pl.pallas_call(
        _router_body,
        grid=(n // tn,),
        in_specs=[pl.BlockSpec((tn, d), lambda i: (i, 0)), _full(wr_t.shape), _full(b_col.shape)],
        out_specs=[tile_spec, tile_spec, row_spec, cnt_spec],
        out_shape=[tile_shape, tile_shape,
                   jax.ShapeDtypeStruct((SLOT_ROWS, n), F32), jax.ShapeDtypeStruct((N_EXPERTS, 128), F32)],
        scratch_shapes=[pltpu.VMEM((N_EXPERTS, 128), F32)],
        compiler_params=_cparams("arbitrary"),
        name="router",
    )(h2, wr_t, b_col)


ROW_LANES = 128


def _to_row_tiles(ref, x):
    rows = x.shape[0]
    for s in range(SUBLANES):
        ref[pl.ds(s, rows, stride=SUBLANES), :] = x[:, s * ROW_LANES:(s + 1) * ROW_LANES]


def _from_row_tiles(ref, rows):
    return jnp.concatenate([ref[pl.ds(s, rows, stride=SUBLANES), :] for s in range(SUBLANES)], axis=1)


def _row_copy(src, src_row8, dst, dst_row8, sem):
    return pltpu.make_async_copy(src.at[pl.ds(pl.multiple_of(src_row8, SUBLANES), SUBLANES), :],
                                 dst.at[pl.ds(pl.multiple_of(dst_row8, SUBLANES), SUBLANES), :], sem)


def _load_slots(slot_ref, slot_smem, sem):
    cp = pltpu.make_async_copy(slot_ref.at[0], slot_smem, sem)
    cp.start()
    cp.wait()


def _dispatch_body(start_ref, e_ref, pos_ref, h_ref, xs_in, slots_out, xs_out, slot_smem, h_tiles, sem_s, sem):
    del xs_in
    tn = slot_smem.shape[1]
    e = e_ref[0]
    slot = pos_ref[0]
    for x in range(N_EXPERTS):
        slot = slot + jnp.where(e == x, start_ref[x], 0)
    slots_out[0] = slot * SUBLANES
    _load_slots(slots_out, slot_smem, sem_s)
    _to_row_tiles(h_tiles, h_ref[...])

    def issue(j, carry):
        for k in range(TOP_K):
            _row_copy(h_tiles, j * SUBLANES, xs_out, slot_smem[k, j], sem).start(priority=k % 2)
        return carry

    def drain(j, carry):
        for k in range(TOP_K):
            _row_copy(h_tiles, 0, xs_out, 0, sem).wait()
        return carry

    lax.fori_loop(0, tn, issue, 0)
    lax.fori_loop(0, tn, drain, 0)


def _zero_tail_body(lb_ref, o_ref):
    del lb_ref
    o_ref[...] = jnp.zeros_like(o_ref)


def _zero_tails(last_block, n_blocks):
    rows = MOE_BLOCK * SUBLANES
    grid_spec = pltpu.PrefetchScalarGridSpec(
        num_scalar_prefetch=1,
        grid=(last_block.shape[0],),
        in_specs=[],
        out_specs=pl.BlockSpec((rows, ROW_LANES), lambda e, lb: (lb[e], 0)),
    )
    return pl.pallas_call(
        _zero_tail_body,
        grid_spec=grid_spec,
        out_shape=jax.ShapeDtypeStruct((n_blocks * rows, ROW_LANES), F32),
        compiler_params=_cparams("arbitrary"),
        name="zero_tails",
    )(last_block)


def _dispatch(row_start, e3, pos3, h2, xs_zero, tn):
    n, d = h2.shape
    ratio = e3.shape[2] // tn
    tile = lambda: pl.BlockSpec((1, SLOT_ROWS, tn), lambda i, st: (i // ratio, 0, i % ratio))
    grid_spec = pltpu.PrefetchScalarGridSpec(
        num_scalar_prefetch=1,
        grid=(n // tn,),
        in_specs=[tile(), tile(), pl.BlockSpec((tn, d), lambda i, st: (i, 0)), pl.BlockSpec(memory_space=pl.ANY)],
        out_specs=[pl.BlockSpec((1, SLOT_ROWS, tn), lambda i, st: (i, 0, 0)), pl.BlockSpec(memory_space=pl.ANY)],
        scratch_shapes=[pltpu.SMEM((SLOT_ROWS, tn), I32), pltpu.VMEM((tn * SUBLANES, ROW_LANES), F32),
                        pltpu.SemaphoreType.DMA, pltpu.SemaphoreType.DMA],
    )
    return pl.pallas_call(
        _dispatch_body,
        grid_spec=grid_spec,
        out_shape=[jax.ShapeDtypeStruct((n // tn, SLOT_ROWS, tn), I32),
                   jax.ShapeDtypeStruct(xs_zero.shape, xs_zero.dtype)],
        input_output_aliases={4: 1},
        compiler_params=_cparams("arbitrary"),
        name="dispatch",
    )(row_start, e3, pos3, h2, xs_zero)


def _swiglu(x, wg, wu, wd):
    g = _dg(x, wg, _NN)
    u = _dg(x, wu, _NN)
    return _dg((g * _sigmoid(g) * u).astype(BF16), wd, _NN)


def _expert_body(be_ref, nu_ref, xs_ref, wg_ref, wu_ref, wd_ref, ys_ref):
    del be_ref
    b = pl.program_id(0)

    @pl.when(b < nu_ref[0])
    def _():
        x = _from_row_tiles(xs_ref, MOE_BLOCK).astype(BF16)
        _to_row_tiles(ys_ref, _swiglu(x, wg_ref[0], wu_ref[0], wd_ref[0]))

    @pl.when(b >= nu_ref[0])
    def _():
        ys_ref[...] = jnp.zeros_like(ys_ref)


def _experts(block_e, n_used, xs, wg, wu, wd):
    bm = MOE_BLOCK
    _, d, de = wg.shape
    rows = bm * SUBLANES
    grid_spec = pltpu.PrefetchScalarGridSpec(
        num_scalar_prefetch=2,
        grid=(xs.shape[0] // rows,),
        in_specs=[
            pl.BlockSpec((rows, ROW_LANES), lambda b, be, nu: (jnp.minimum(b, nu[0] - 1), 0)),
            pl.BlockSpec((1, d, de), lambda b, be, nu: (be[b], 0, 0)),
            pl.BlockSpec((1, d, de), lambda b, be, nu: (be[b], 0, 0)),
            pl.BlockSpec((1, de, d), lambda b, be, nu: (be[b], 0, 0)),
        ],
        out_specs=pl.BlockSpec((rows, ROW_LANES), lambda b, be, nu: (b, 0)),
    )
    return pl.pallas_call(
        _expert_body,
        grid_spec=grid_spec,
        out_shape=jax.ShapeDtypeStruct(xs.shape, F32),
        compiler_params=_cparams("arbitrary"),
        name="experts",
    )(block_e, n_used, xs, wg, wu, wd)


def _combine_body(slot_ref, w_ref, x1_ref, h2_ref, gate_ref, ys_hbm, wsg_ref, wsu_ref, wsd_ref, gpost_ref,
                  out_ref, slot_smem, buf, sem_s, sem):
    tn = slot_smem.shape[1]
    _load_slots(slot_ref, slot_smem, sem_s)

    def issue(j, carry):
        for k in range(TOP_K):
            _row_copy(ys_hbm, slot_smem[k, j], buf.at[k], j * SUBLANES, sem).start(priority=k % 2)
        return carry

    def drain(j, carry):
        for k in range(TOP_K):
            _row_copy(ys_hbm, 0, buf.at[k], 0, sem).wait()
        return carry

    lax.fori_loop(0, tn, issue, 0)
    acc = _swiglu(h2_ref[...].astype(BF16), wsg_ref[...], wsu_ref[...], wsd_ref[...])
    lax.fori_loop(0, tn, drain, 0)
    for k in range(TOP_K):
        acc = acc + _from_row_tiles(buf.at[k], tn) * w_ref[:, k:k + 1]
    out_ref[...] = x1_ref[...] + gate_ref[0] * _rms(acc, gpost_ref[...])


def _combine(slots3, w_tok, x1, h2, gate_rows, ys, wsg, wsu, wsd, gpost):
    ntiles, _, tn = slots3.shape
    n, d = x1.shape
    tok = pl.BlockSpec((tn, d), lambda i: (i, 0))
    return pl.pallas_call(
        _combine_body,
        grid=(ntiles,),
        in_specs=[pl.BlockSpec((1, SLOT_ROWS, tn), lambda i: (i, 0, 0)),
                  pl.BlockSpec((tn, SLOT_ROWS), lambda i: (i, 0)), tok, tok,
                  pl.BlockSpec((1, 1, d), lambda i: (i, 0, 0)),
                  pl.BlockSpec(memory_space=pl.ANY),
                  _full(wsg.shape), _full(wsu.shape), _full(wsd.shape), _full(gpost.shape)],
        out_specs=tok,
        out_shape=jax.ShapeDtypeStruct((n, d), F32),
        scratch_shapes=[pltpu.SMEM((SLOT_ROWS, tn), I32), pltpu.VMEM((TOP_K, tn * SUBLANES, ROW_LANES), F32),
                        pltpu.SemaphoreType.DMA, pltpu.SemaphoreType.DMA],
        compiler_params=_cparams("arbitrary"),
        name="combine",
    )(slots3, w_tok, x1, h2, gate_rows, ys, wsg, wsu, wsd, gpost)


def _moe(x1, h2, gate_rows, lp):
    n, d = h2.shape
    bm = MOE_BLOCK
    tn = TOKEN_TILE
    e3, pos3, w_rows, cnt = _router(h2, lp['wr_t'], lp['b_router_col'])
    counts = cnt[:, 0].astype(I32)
    blocks_per_e = (counts + bm - 1) // bm
    blk_end = jnp.cumsum(blocks_per_e)
    blk_start = blk_end - blocks_per_e
    n_blocks = -(-(n * TOP_K) // bm) + N_EXPERTS
    block_e = jnp.minimum(jnp.sum(blk_end[None, :] <= jnp.arange(n_blocks, dtype=I32)[:, None], axis=1),
                          N_EXPERTS - 1).astype(I32)
    n_used = blk_end[-1:].astype(I32)
    xs_tails = _zero_tails(jnp.clip(blk_end - 1, 0, n_blocks - 1).astype(I32), n_blocks)
    slots3, xs = _dispatch((blk_start * bm).astype(I32), e3, pos3, h2, xs_tails, tn)
    ys = _experts(block_e, n_used, xs, lp['w_e_gate'], lp['w_e_up'], lp['w_e_down'])
    return _combine(slots3, w_rows.T, x1, h2, gate_rows, ys, lp['w_s_gate'], lp['w_s_up'], lp['w_s_down'],
                    lp['g_post_ffn'])


def _rope_tables(n_tokens):
    rows = n_tokens // GRID_W
    row = jnp.repeat(jnp.arange(rows, dtype=F32), GRID_W)
    col = jnp.tile(jnp.arange(GRID_W, dtype=F32), rows)
    n_freq = ROPE_DIM // 4
    inv_freq = ROPE_THETA ** (-jnp.arange(n_freq, dtype=F32) / n_freq)
    ang_r = row[:, None] * inv_freq
    ang_c = col[:, None] * inv_freq
    ang = jnp.concatenate([ang_r, ang_r, ang_c, ang_c], axis=-1)
    cos, sin = jnp.cos(ang), jnp.sin(ang)
    pad = lambda z, fill: jnp.concatenate(
        [jnp.full((n_tokens, NOPE_DIM), fill, F32), z, jnp.zeros((n_tokens, HEAD_PAD - QK_DIM), F32)], axis=1)
    return pad(cos, 1.0), pad(sin, 0.0), pad(cos, 0.0), pad(sin, 0.0)


def _rot_cols(w):
    q = ROPE_DIM // 4
    perm = np.concatenate([np.arange(q, 2 * q), np.arange(0, q), np.arange(3 * q, 4 * q), np.arange(2 * q, 3 * q)])
    sign = np.concatenate([-np.ones(q), np.ones(q), -np.ones(q), np.ones(q)]).astype(np.float32)
    return w[..., perm] * sign


def _block_diag(blocks):
    h, m, n = blocks.shape[-3:]
    eye = jnp.eye(h, dtype=blocks.dtype)
    out = blocks[..., :, :, None, :] * eye[:, None, :, None]
    return out.reshape(blocks.shape[:-3] + (h * m, h * n))


def _diag_blocks(mat, h):
    m, n = mat.shape[-2] // h, mat.shape[-1] // h
    z = mat.reshape(mat.shape[:-2] + (h, m, h, n))
    return jnp.stack([z[..., i, :, i, :] for i in range(h)], axis=-3)


def _layer_operands(l, a):
    w_in = a['w_in'][l]
    d = w_in.shape[0]
    p_mla = Q_LORA + KV_LORA + ROPE_DIM
    w_kr = w_in[:, Q_LORA + KV_LORA:p_mla]
    z32 = jnp.zeros((d, ROPE_DIM), F32)
    w_mla = jnp.concatenate([w_in[:, :Q_LORA + KV_LORA], z32, _rot_cols(w_kr), w_kr, z32], axis=1)
    wq3 = a['w_q_b'][l].reshape(Q_LORA, H_MLA, QK_DIM)
    zq = lambda n: jnp.zeros((Q_LORA, H_MLA, n), F32)
    wq = jnp.concatenate([wq3, zq(HEAD_PAD - QK_DIM)], axis=2).reshape(Q_LORA, H_MLA * HEAD_PAD)
    wqr = jnp.concatenate([zq(NOPE_DIM), _rot_cols(wq3[:, :, NOPE_DIM:]), zq(HEAD_PAD - QK_DIM)],
                          axis=2).reshape(Q_LORA, H_MLA * HEAD_PAD)
    wkv3 = a['w_kv_b'][l].reshape(KV_LORA, H_MLA, NOPE_DIM + V_DIM)
    wk = jnp.concatenate([wkv3[:, :, :NOPE_DIM], jnp.zeros((KV_LORA, H_MLA, HEAD_PAD - NOPE_DIM), F32)],
                         axis=2).reshape(KV_LORA, H_MLA * HEAD_PAD)
    wv = wkv3[:, :, NOPE_DIM:].reshape(KV_LORA, H_MLA * V_DIM)
    row = lambda z: z.reshape(1, -1)
    two = lambda z: jnp.concatenate([z[0], z[1]], axis=-1)
    bd2 = lambda z: jnp.concatenate([jnp.concatenate([z[0], jnp.zeros_like(z[0])], axis=1),
                                     jnp.concatenate([jnp.zeros_like(z[1]), z[1]], axis=1)], axis=0)
    return {
        'g_pre_mix': row(a['g_pre_mix'][l]), 'g_post_mix': row(a['g_post_mix'][l]),
        'g_pre_ffn': row(a['g_pre_ffn'][l]), 'g_post_ffn': row(a['g_post_ffn'][l]),
        'w_mla': w_mla.astype(BF16),
        'w_rwkv': w_in[:, p_mla:p_mla + P_RWKV].astype(BF16),
        'w_lru': w_in[:, p_mla + P_RWKV:].astype(BF16),
        'g_q_a': row(a['g_q_a'][l]), 'g_kv_a': row(a['g_kv_a'][l]),
        'wq': wq.astype(BF16), 'wqr': wqr.astype(BF16), 'wk': wk.astype(BF16), 'wv': wv.astype(BF16),
        'g_mla_out': row(a['g_mla_out'][l]),
        'mu_rwkv': row(a['mu_rwkv'][l]), 'k_k': row(a['k_k'][l]), 'k_a': row(a['k_a'][l]), 'r_k': row(a['r_k'][l]),
        'w0_cat': row(two(a['w0_rwkv'][l])), 'w2_bd': bd2(a['w2_rwkv'][l]),
        'a0_cat': row(two(a['a0_rwkv'][l])), 'a2_bd': bd2(a['a2_rwkv'][l]),
        'g2_rwkv': a['g2_rwkv'][l], 'ln_x_w': row(a['ln_x_w'][l]), 'ln_x_b': row(a['ln_x_b'][l]),
        'conv_w': a['conv_w'][l], 'conv_b': a['conv_b'][l],
        'wa_bd': _block_diag(a['w_lru_a'][l]), 'b_lru_a': a['b_lru_a'][l],
        'wx_bd': _block_diag(a['w_lru_x'][l]), 'b_lru_x': a['b_lru_x'][l],
        'lam': a['lam'][l], 'g_lru_out': row(a['g_lru_out'][l]),
        'w_out': a['w_out'][l].astype(BF16),
        'wr_t': a['w_router'][l].T, 'b_router_col': a['b_router'][l].reshape(N_EXPERTS, 1),
        'w_e_gate': a['w_e_gate'][l].astype(BF16), 'w_e_up': a['w_e_up'][l].astype(BF16),
        'w_e_down': a['w_e_down'][l].astype(BF16),
        'w_s_gate': a['w_s_gate'][l].astype(BF16), 'w_s_up': a['w_s_up'][l].astype(BF16),
        'w_s_down': a['w_s_down'][l].astype(BF16),
    }


def _mixer(x, grp, mod, lp, ctx, tables, prev):
    _, nb, t = grp
    tt = min(t, 512)
    p_mla, p_rwkv, p_lru = _in_proj(x, grp, lp['g_pre_mix'], mod['sc1'], mod['sh1'], lp['w_mla'], lp['w_rwkv'],
                                    lp['w_lru'], tt)
    if ctx is None:
        q, k, v, new_cache = _mla_prep(p_mla, lp['g_q_a'], lp['g_kv_a'], lp['wq'], None, lp['wk'], lp['wv'], None, tt)
        kc = vc = None
        s0 = jnp.zeros((nb, 2, D_RWKV, D_RWKV), F32)
        h0 = jnp.zeros((nb, 2, 1, D_LRU), F32)
    else:
        cache, s_rwkv, s_lru = ctx
        q, k, v = _mla_prep(p_mla, lp['g_q_a'], lp['g_kv_a'], lp['wq'], lp['wqr'], lp['wk'], lp['wv'], tables, tt)
        new_cache = None
        kr_placed = jnp.pad(cache[..., KV_LORA:], ((0, 0), (0, 0), (NOPE_DIM, HEAD_PAD - QK_DIM)))
        kc, vc = _ctx_kv(cache[..., :KV_LORA], kr_placed, lp['wk'], lp['wv'])
        s0 = _block_diag(s_rwkv)
        h0 = s_lru[:, :, None, :]
    o = _attention(q, k, v, kc, vc, min(t, 256))

    r, vv, kk, lw, ag, kd, g, bonus = _rwkv_prep(p_rwkv, lp, tt)
    y_f, y_b, s_fin = _rwkv_scan(r, vv, kk, lw, ag, kd, s0)

    hp, hn = _halo_rows(p_lru[..., :D_LRU], tt, SUBLANES)
    h_f, fin_f = _lru_dir(p_lru, hp, lp, 0, h0[:, 0], tt)
    h_b, fin_b = _lru_dir(p_lru, hn, lp, 1, h0[:, 1], tt)

    x1, h2 = _out_proj(x, grp, o, y_f, y_b, bonus, g, h_f, h_b, p_lru, lp, mod, tt, prev)
    states = None
    if ctx is None:
        states = (new_cache, _diag_blocks(s_fin, H_RWKV), jnp.concatenate([fin_f, fin_b], axis=1))
    return x1, h2, states


def kernel(x_prompt, x_sample, c, cache_mla, state_rwkv, state_lru, c_ctx, w_ada, b_ada, g_pre_mix, g_post_mix, g_pre_ffn, g_post_ffn, w_in, g_q_a, w_q_b, g_kv_a, w_kv_b, g_mla_out, mu_rwkv, w0_rwkv, w2_rwkv, a0_rwkv, a2_rwkv, g2_rwkv, k_k, k_a, r_k, ln_x_w, ln_x_b, conv_w, conv_b, w_lru_a, b_lru_a, w_lru_x, b_lru_x, lam, g_lru_out, w_out, w_router, b_router, w_e_gate, w_e_up, w_e_down, w_s_gate, w_s_up, w_s_down):
    a = dict(w_in=w_in, g_pre_mix=g_pre_mix, g_post_mix=g_post_mix, g_pre_ffn=g_pre_ffn, g_post_ffn=g_post_ffn,
             g_q_a=g_q_a, w_q_b=w_q_b, g_kv_a=g_kv_a, w_kv_b=w_kv_b, g_mla_out=g_mla_out, mu_rwkv=mu_rwkv,
             w0_rwkv=w0_rwkv, w2_rwkv=w2_rwkv, a0_rwkv=a0_rwkv, a2_rwkv=a2_rwkv, g2_rwkv=g2_rwkv, k_k=k_k, k_a=k_a,
             r_k=r_k.reshape(r_k.shape[0], -1), ln_x_w=ln_x_w, ln_x_b=ln_x_b, conv_w=conv_w, conv_b=conv_b,
             w_lru_a=w_lru_a, b_lru_a=b_lru_a, w_lru_x=w_lru_x, b_lru_x=b_lru_x, lam=lam, g_lru_out=g_lru_out,
             w_out=w_out, w_router=w_router, b_router=b_router, w_e_gate=w_e_gate, w_e_up=w_e_up,
             w_e_down=w_e_down, w_s_gate=w_s_gate, w_s_up=w_s_up, w_s_down=w_s_down)
    n_layers = w_in.shape[0]
    nbp, tp, d = x_prompt.shape
    nbs, ts, _ = x_sample.shape
    rows = -(-(1 + nbs) // SUBLANES) * SUBLANES
    cvecs = jnp.concatenate([c_ctx[None, :], c, jnp.zeros((rows - 1 - nbs, d), F32)], axis=0)
    mods = _ada(cvecs, w_ada, b_ada)
    names = ('sh1', 'sc1', 'g1', 'sh2', 'sc2', 'g2')
    tables = _rope_tables(ts)
    tn = TOKEN_TILE

    grp_p, grp_s = (0, nbp, tp), (nbp * tp, nbs, ts)
    x = jnp.concatenate([x_prompt.reshape(-1, d), x_sample.reshape(-1, d)], axis=0)
    caches, rwkv_states, lru_states = [], [], []
    for l in range(n_layers):
        lp = _layer_operands(l, a)
        mod_p = {nm: mods[l, 0:1, i * d:(i + 1) * d][:, None, :] for i, nm in enumerate(names)}
        mod_s = {nm: mods[l, 1:1 + nbs, i * d:(i + 1) * d][:, None, :] for i, nm in enumerate(names)}
        x1, h2, states = _mixer(x, grp_p, mod_p, lp, None, None, None)
        x1, h2, _ = _mixer(x, grp_s, mod_s, lp, (cache_mla[:, l], state_rwkv[:, l], state_lru[:, l]), tables,
                           (x1, h2))
        caches.append(states[0])
        rwkv_states.append(states[1])
        lru_states.append(states[2])
        gate_rows = jnp.concatenate([jnp.repeat(mod_p['g2'], nbp * tp // tn, axis=0),
                                     jnp.repeat(mod_s['g2'], ts // tn, axis=0)], axis=0)
        x = _moe(x1, h2, gate_rows, lp)
    yp = x[:nbp * tp].reshape(nbp, tp, d)
    ys = x[nbp * tp:].reshape(nbs, ts, d)
    return (yp, ys, jnp.stack(caches, axis=1), jnp.stack(rwkv_states, axis=1), jnp.stack(lru_states, axis=1))
```

```python
import functools

import numpy as np
import jax
import jax.numpy as jnp
from jax import lax
from jax.experimental import pallas as pl
from jax.experimental.pallas import tpu as pltpu

F32 = jnp.float32
BF16 = jnp.bfloat16
I32 = jnp.int32

D_MODEL = 1024
GRID_W = 64
EPS = 1e-6
H_MLA = 4
Q_LORA = 256
KV_LORA = 128
NOPE_DIM = 64
ROPE_DIM = 32
V_DIM = 128
QK_DIM = NOPE_DIM + ROPE_DIM
ROPE_THETA = 10000.0
HEAD_PAD = 128
H_RWKV = 4
N_RWKV = 64
D_RWKV = H_RWKV * N_RWKV
W_LORA = 64
A_LORA = 64
G_LORA = 128
GN_EPS = 64e-5
D_LRU = 256
H_LRU = 4
BS_LRU = D_LRU // H_LRU
CONV_W = 4
LRU_C = 8.0
D_MLA_OUT = H_MLA * V_DIM
P_MLA_PAD = Q_LORA + KV_LORA + HEAD_PAD
P_RWKV = 3 * D_RWKV + 2 * W_LORA + 2 * A_LORA + G_LORA
P_LRU = 2 * D_LRU
N_EXPERTS = 64
TOP_K = 6
N_GROUPS = 8
TOPK_GROUPS = 4
GROUP_SIZE = N_EXPERTS // N_GROUPS
D_EXPERT = 256
ROUTED_SCALE = 2.5

SUBLANES = 8
RWKV_CHUNK = 64
RWKV_ROWS = 2
MOE_BLOCK = 512
ROUTER_TILE = 512
TOKEN_TILE = 256
SLOT_ROWS = 8
VMEM_LIMIT = 56 * 1024 * 1024


def _cparams(*sem):
    return pltpu.CompilerParams(dimension_semantics=sem, vmem_limit_bytes=VMEM_LIMIT)


def _sigmoid(x):
    return 1.0 / (1.0 + jnp.exp(-x))


def _softplus(x):
    return jnp.maximum(x, 0.0) + jnp.log1p(jnp.exp(-jnp.abs(x)))


def _expm1(z):
    e = jnp.exp(z)
    direct = (e == 1.0) | (z < -1.0)
    corrected = (e - 1.0) * z / jnp.log(jnp.where(direct, 2.0, e))
    return jnp.where(e == 1.0, z, jnp.where(z < -1.0, e - 1.0, corrected))


def _rms(x, g):
    return x * lax.rsqrt(jnp.mean(x * x, axis=-1, keepdims=True) + EPS) * g


def _split2(x):
    hi = x.astype(BF16)
    lo = (x - hi.astype(F32)).astype(BF16)
    return hi, lo


def _split3(x):
    hi = x.astype(BF16)
    r1 = x - hi.astype(F32)
    mid = r1.astype(BF16)
    lo = (r1 - mid.astype(F32)).astype(BF16)
    return hi, mid, lo


_NN = (((1,), (0,)), ((), ()))
_NT = (((1,), (1,)), ((), ()))


def _dg(a, b, dims):
    return lax.dot_general(a, b, dims, preferred_element_type=F32)


def _dot1(a, b, dims=_NN):
    return _dg(a.astype(BF16), b.astype(BF16), dims)


def _dot3(a, b, dims=_NN):
    ah, al = _split2(a)
    bh, bl = _split2(b)
    return _dg(ah, bh, dims) + (_dg(ah, bl, dims) + _dg(al, bh, dims))


def _dot_x(a, b_exact, dims=_NN):
    h, m, l = _split3(a)
    bb = b_exact.astype(BF16)
    return _dg(h, bb, dims) + (_dg(m, bb, dims) + _dg(l, bb, dims))


def _x_dot(a_exact, b, dims=_NN):
    h, m, l = _split3(b)
    aa = a_exact.astype(BF16)
    return _dg(aa, h, dims) + (_dg(aa, m, dims) + _dg(aa, l, dims))


def _head_ones(n, seg):
    r = lax.broadcasted_iota(I32, (n, n), 0) // seg
    c = lax.broadcasted_iota(I32, (n, n), 1) // seg
    return (r == c).astype(F32)


def _ada_body(c_ref, w_ref, b_ref, o_ref):
    cv = c_ref[...]
    s = cv * _sigmoid(cv)
    o_ref[0] = _dot3(s, w_ref[0]) + b_ref[0]


def _ada(cvecs, w_ada, b_ada):
    n_layers, d, d6 = w_ada.shape
    rows = cvecs.shape[0]
    nt = d6 // d
    return pl.pallas_call(
        _ada_body,
        grid=(n_layers, nt),
        in_specs=[
            pl.BlockSpec((rows, d), lambda l, j: (0, 0)),
            pl.BlockSpec((1, d, d), lambda l, j: (l, 0, j)),
            pl.BlockSpec((1, 1, d), lambda l, j: (l, 0, j)),
        ],
        out_specs=pl.BlockSpec((1, rows, d), lambda l, j: (l, 0, j)),
        out_shape=jax.ShapeDtypeStruct((n_layers, rows, d6), F32),
        compiler_params=_cparams("parallel", "parallel"),
        name="ada",
    )(cvecs, w_ada, b_ada.reshape(n_layers, 1, d6))


def _mod_spec(per_batch):
    if per_batch:
        return pl.BlockSpec((1, 1, D_MODEL), lambda b, i: (b, 0, 0))
    return pl.BlockSpec((1, 1, D_MODEL), lambda b, i: (0, 0, 0))


def _full(shape):
    nd = len(shape)
    return pl.BlockSpec(shape, lambda *_: (0,) * nd)


def _in_proj_body(x_ref, g_ref, sc_ref, sh_ref, w1_ref, w2_ref, w3_ref, o1_ref, o2_ref, o3_ref):
    h = _rms(x_ref[...], g_ref[...]) * (1.0 + sc_ref[0]) + sh_ref[0]
    hb = h.astype(BF16)
    o1_ref[0] = _dg(hb, w1_ref[...], _NN)
    o2_ref[0] = _dg(hb, w2_ref[...], _NN)
    o3_ref[0] = _dg(hb, w3_ref[...], _NN)


def _group_rows(grp, tt):
    row_off, nb, t = grp
    base, per = row_off // tt, t // tt
    return pl.BlockSpec((tt, D_MODEL), lambda b, i: (base + b * per + i, 0))


def _in_proj(x, grp, g, sc, sh, w_mla, w_rwkv, w_lru, tt):
    _, nb, t = grp
    d = x.shape[1]
    per_batch = sc.shape[0] > 1
    outs = [w_mla.shape[1], w_rwkv.shape[1], w_lru.shape[1]]
    return pl.pallas_call(
        _in_proj_body,
        grid=(nb, t // tt),
        in_specs=[
            _group_rows(grp, tt),
            _full((1, d)),
            _mod_spec(per_batch),
            _mod_spec(per_batch),
            _full(w_mla.shape),
            _full(w_rwkv.shape),
            _full(w_lru.shape),
        ],
        out_specs=[pl.BlockSpec((1, tt, n), lambda b, i: (b, i, 0)) for n in outs],
        out_shape=[jax.ShapeDtypeStruct((nb, t, n), F32) for n in outs],
        compiler_params=_cparams("parallel", "parallel"),
        name="in_proj",
    )(x, g, sc, sh, w_mla, w_rwkv, w_lru)


def _rope_lanes():
    lane = lax.broadcasted_iota(I32, (1, HEAD_PAD), 1)
    return ((lane >= NOPE_DIM) & (lane < QK_DIM)).astype(F32)


def _mla_prep_body(rope, *refs):
    if rope:
        (p_ref, gq_ref, gkv_ref, wq_ref, wqr_ref, wk_ref, wv_ref, cq_ref, sq_ref, cr_ref, sr_ref,
         q_ref, k_ref, v_ref) = refs
    else:
        p_ref, gq_ref, gkv_ref, wq_ref, wk_ref, wv_ref, q_ref, k_ref, v_ref, c_ref = refs
    p = p_ref[0]
    cq = _rms(p[:, :Q_LORA], gq_ref[...])
    ckv = _rms(p[:, Q_LORA:Q_LORA + KV_LORA], gkv_ref[...])
    kr = p[:, Q_LORA + KV_LORA:]
    cqb = cq.astype(BF16)
    ckvb = ckv.astype(BF16)
    scale = QK_DIM ** -0.5 * float(np.log2(np.e))
    qa = _dg(cqb, wq_ref[...], _NN) * scale
    ka = _dg(ckvb, wk_ref[...], _NN)
    va = _dg(ckvb, wv_ref[...], _NN)
    if rope:
        qr = _dg(cqb, wqr_ref[...], _NN) * scale
        krp = kr * cr_ref[...] + pltpu.roll(kr, ROPE_DIM, 1) * sr_ref[...]
    else:
        krp = kr * _rope_lanes()
    for h in range(H_MLA):
        sl = slice(h * HEAD_PAD, (h + 1) * HEAD_PAD)
        qh = qa[:, sl]
        if rope:
            qh = qh * cq_ref[...] + qr[:, sl] * sq_ref[...]
        q_ref[0, h] = qh.astype(BF16)
        k_ref[0, h] = (ka[:, sl] + krp).astype(BF16)
        v_ref[0, h] = va[:, sl].astype(BF16)
    if not rope:
        c_ref[0, :, :KV_LORA] = ckv
        c_ref[0, :, KV_LORA:] = pltpu.roll(kr, HEAD_PAD - NOPE_DIM, 1)[:, :ROPE_DIM]


def _mla_prep(p_mla, gq, gkv, wq, wqr, wk, wv, tables, tt):
    nb, t, pw = p_mla.shape
    rope = tables is not None
    hw = H_MLA * HEAD_PAD
    head_spec = pl.BlockSpec((1, H_MLA, tt, HEAD_PAD), lambda b, i: (b, 0, i, 0))
    head_shape = jax.ShapeDtypeStruct((nb, H_MLA, t, HEAD_PAD), BF16)
    in_specs = [pl.BlockSpec((1, tt, pw), lambda b, i: (b, i, 0)), _full(gq.shape), _full(gkv.shape), _full(wq.shape)]
    args = [p_mla, gq, gkv, wq]
    if rope:
        in_specs.append(_full(wqr.shape))
        args.append(wqr)
    in_specs += [_full(wk.shape), _full(wv.shape)]
    args += [wk, wv]
    out_specs = [head_spec, head_spec, head_spec]
    out_shape = [head_shape, head_shape, head_shape]
    if rope:
        in_specs += [pl.BlockSpec((tt, HEAD_PAD), lambda b, i: (i, 0))] * 4
        args += list(tables)
    else:
        out_specs.append(pl.BlockSpec((1, tt, KV_LORA + ROPE_DIM), lambda b, i: (b, i, 0)))
        out_shape.append(jax.ShapeDtypeStruct((nb, t, KV_LORA + ROPE_DIM), F32))
    del hw
    return pl.pallas_call(
        functools.partial(_mla_prep_body, rope),
        grid=(nb, t // tt),
        in_specs=in_specs,
        out_specs=out_specs,
        out_shape=out_shape,
        compiler_params=_cparams("parallel", "parallel"),
        name="mla_prep_rope" if rope else "mla_prep",
    )(*args)


def _ctx_kv_body(lat_ref, kr_ref, wk_ref, wv_ref, k_ref, v_ref):
    latb = lat_ref[0].astype(BF16)
    ka = _dg(latb, wk_ref[...], _NN)
    va = _dg(latb, wv_ref[...], _NN)
    kr = kr_ref[0]
    for h in range(H_MLA):
        sl = slice(h * HEAD_PAD, (h + 1) * HEAD_PAD)
        k_ref[0, h] = (ka[:, sl] + kr).astype(BF16)
        v_ref[0, h] = va[:, sl].astype(BF16)


def _ctx_kv(lat, kr_placed, wk, wv):
    nb, s, _ = lat.shape
    head_spec = pl.BlockSpec((1, H_MLA, s, HEAD_PAD), lambda b: (b, 0, 0, 0))
    head_shape = jax.ShapeDtypeStruct((nb, H_MLA, s, HEAD_PAD), BF16)
    return pl.pallas_call(
        _ctx_kv_body,
        grid=(nb,),
        in_specs=[
            pl.BlockSpec((1, s, KV_LORA), lambda b: (b, 0, 0)),
            pl.BlockSpec((1, s, HEAD_PAD), lambda b: (b, 0, 0)),
            _full(wk.shape),
            _full(wv.shape),
        ],
        out_specs=[head_spec, head_spec],
        out_shape=[head_shape, head_shape],
        compiler_params=_cparams("parallel"),
        name="ctx_kv",
    )(lat, kr_placed, wk, wv)


def _attn_body(has_ctx, *refs):
    if has_ctx:
        q_ref, k_ref, v_ref, kc_ref, vc_ref, o_ref = refs
    else:
        q_ref, k_ref, v_ref, o_ref = refs
    q = q_ref[0, 0]
    s = _dg(q, k_ref[0, 0], _NT)
    m = jnp.max(s, axis=-1, keepdims=True)
    if has_ctx:
        sc = _dg(q, kc_ref[0, 0], _NT)
        m = jnp.maximum(m, jnp.max(sc, axis=-1, keepdims=True))
    p = jnp.exp2(s - m)
    l = jnp.sum(p, axis=-1, keepdims=True)
    o = _dg(p.astype(BF16), v_ref[0, 0], _NN)
    if has_ctx:
        pc = jnp.exp2(sc - m)
        l = l + jnp.sum(pc, axis=-1, keepdims=True)
        o = o + _dg(pc.astype(BF16), vc_ref[0, 0], _NN)
    o_ref[0] = o / l


def _attention(q, k, v, kc, vc, tq):
    nb, nh, t, hd = q.shape
    s = k.shape[2]
    has_ctx = kc is not None
    in_specs = [
        pl.BlockSpec((1, 1, tq, hd), lambda b, h, i: (b, h, i, 0)),
        pl.BlockSpec((1, 1, s, hd), lambda b, h, i: (b, h, 0, 0)),
        pl.BlockSpec((1, 1, s, V_DIM), lambda b, h, i: (b, h, 0, 0)),
    ]
    args = [q, k, v]
    if has_ctx:
        sc = kc.shape[2]
        in_specs += [
            pl.BlockSpec((1, 1, sc, hd), lambda b, h, i: (b, h, 0, 0)),
            pl.BlockSpec((1, 1, sc, V_DIM), lambda b, h, i: (b, h, 0, 0)),
        ]
        args += [kc, vc]
    return pl.pallas_call(
        functools.partial(_attn_body, has_ctx),
        grid=(nb, nh, t // tq),
        in_specs=in_specs,
        out_specs=pl.BlockSpec((1, tq, V_DIM), lambda b, h, i: (b, i, h)),
        out_shape=jax.ShapeDtypeStruct((nb, t, nh * V_DIM), F32),
        compiler_params=_cparams("parallel", "parallel", "parallel"),
        name="attention",
    )(*args)


def _rwkv_prep_body(p_ref, hp_ref, hn_ref, mu_ref, kk_ref, ka_ref, rk_ref, w0_ref, w2_ref, a0_ref, a2_ref, g2_ref,
                    r_out, v_out, kk_out, lw_out, a_out, kd_out, g_out, bonus_out):
    p = p_ref[0]
    tt = p.shape[0]
    row = lax.broadcasted_iota(I32, (tt, 1), 0)
    prev = jnp.where(row == 0, hp_ref[0, 0], pltpu.roll(p, 1, 0))
    nxt = jnp.where(row == tt - 1, hn_ref[0, 0], pltpu.roll(p, tt - 1, 0))
    ps = p + (0.5 * (prev + nxt) - p) * mu_ref[...]
    r = ps[:, :D_RWKV]
    k = ps[:, D_RWKV:2 * D_RWKV]
    v = ps[:, 2 * D_RWKV:3 * D_RWKV]
    wl = ps[:, 3 * D_RWKV:3 * D_RWKV + 2 * W_LORA]
    al = ps[:, 3 * D_RWKV + 2 * W_LORA:3 * D_RWKV + 2 * W_LORA + 2 * A_LORA]
    gl = ps[:, 3 * D_RWKV + 2 * W_LORA + 2 * A_LORA:]
    ones = _head_ones(D_RWKV, N_RWKV)
    kk = k * kk_ref[...]
    kk = kk / jnp.maximum(jnp.sqrt(_dot_x(kk * kk, ones)), 1e-12)
    wlin = w0_ref[...] + _dot3(jnp.tanh(wl), w2_ref[...])
    lw = -jnp.exp(-_softplus(-wlin) - 0.5)
    ag = _sigmoid(a0_ref[...] + _dot3(al, a2_ref[...]))
    bonus = jnp.zeros_like(v)
    for d in range(2):
        sl = slice(d * D_RWKV, (d + 1) * D_RWKV)
        kd = k * (1.0 + (ag[:, sl] - 1.0) * ka_ref[...])
        kd_out[0, :, sl] = kd
        bonus = bonus + _dot_x(r * kd * rk_ref[...], ones) * v
    r_out[0] = r
    v_out[0] = v
    kk_out[0] = kk
    lw_out[0] = lw
    a_out[0] = ag
    g_out[0] = _dot1(_sigmoid(gl), g2_ref[...])
    bonus_out[0] = bonus


def _halo_rows(x, tt, width):
    nb, t, c = x.shape
    nt = t // tt
    xt = x.reshape(nb, nt, tt, c)
    zero = jnp.zeros((nb, 1, width, c), x.dtype)
    prev = jnp.concatenate([zero, xt[:, :-1, tt - width:]], axis=1)
    nxt = jnp.concatenate([xt[:, 1:, :width], zero], axis=1)
    return prev, nxt


def _rwkv_prep(p_rwkv, lp, tt):
    nb, t, pw = p_rwkv.shape
    hp, hn = _halo_rows(p_rwkv, tt, 1)
    tok = lambda n: pl.BlockSpec((1, tt, n), lambda b, i: (b, i, 0))
    halo = pl.BlockSpec((1, 1, 1, pw), lambda b, i: (b, i, 0, 0))
    small = [lp['mu_rwkv'], lp['k_k'], lp['k_a'], lp['r_k'], lp['w0_cat'], lp['w2_bd'], lp['a0_cat'], lp['a2_bd'],
             lp['g2_rwkv']]
    widths = [D_RWKV, D_RWKV, D_RWKV, 2 * D_RWKV, 2 * D_RWKV, 2 * D_RWKV, D_RWKV, D_RWKV]
    return pl.pallas_call(
        _rwkv_prep_body,
        grid=(nb, t // tt),
        in_specs=[tok(pw), halo, halo] + [_full(a.shape) for a in small],
        out_specs=[tok(n) for n in widths],
        out_shape=[jax.ShapeDtypeStruct((nb, t, n), F32) for n in widths],
        compiler_params=_cparams("parallel", "parallel"),
        name="rwkv_prep",
    )(p_rwkv, hp, hn, *small)


def _rwkv_chunk(reverse, r, v, kk, lw, ag, kd, s_old):
    cl = r.shape[0]
    hw = D_RWKV

    def later(rows, cols, strict):
        t = lax.broadcasted_iota(I32, (rows, cols), 0)
        s = lax.broadcasted_iota(I32, (rows, cols), 1) % cl
        dlt = (s - t) if reverse else (t - s)
        return (dlt > 0) if strict else (dlt >= 0)

    cum = _x_dot(later(cl, cl, False).astype(F32), lw)
    yield
    tot = jnp.sum(lw, axis=0, keepdims=True)
    e_neg = jnp.exp(-cum)
    e_rem = jnp.exp(tot - cum)
    at = -kk * jnp.exp(cum - lw)
    rt = r * jnp.exp(cum)
    b0 = kk * ag
    bt = b0 * e_neg
    kt = kd * e_neg
    bh = b0 * e_rem
    kh = kd * e_rem
    w_c = jnp.exp(tot)

    lane_head = lax.broadcasted_iota(I32, (1, hw), 1) // N_RWKV
    hms = [(lane_head == h).astype(F32) for h in range(H_RWKV)]
    heads = range(H_RWKV)
    bk = [jnp.concatenate([bt * hm, kt * hm], axis=0) for hm in hms]
    strict = later(cl, 2 * cl, True)
    incl = later(cl, 2 * cl, False)
    ls = [jnp.where(strict, _dot3(at, bk[h], _NT), 0.0) for h in heads]
    ly = [jnp.where(incl, _dot1(rt, bk[h], _NT), 0.0) for h in heads]
    yield

    x = [l[:, :cl] for l in ls]
    eye = (lax.broadcasted_iota(I32, (cl, cl), 0) == lax.broadcasted_iota(I32, (cl, cl), 1)).astype(F32)
    inv = [eye + xh for xh in x]
    span = 2
    while span < cl:
        x = [_dot3(xh, xh) for xh in x]
        yield
        inv = [ih + _dot3(ih, xh) for ih, xh in zip(inv, x)]
        yield
        span *= 2

    zero = jnp.zeros((cl, hw), F32)
    v_rows = jnp.concatenate([piece for hm in hms for piece in (zero, v * hm)], axis=0)
    rhs = _dot3(at, s_old, _NT) + _dot3(jnp.concatenate(ls, axis=1), v_rows)
    yield
    us = [_dot3(inv[h], rhs * hms[h]) for h in heads]
    yield
    um = us[0]
    for h in range(1, H_RWKV):
        um = um + us[h]
    uv_rows = jnp.concatenate([piece for h in heads for piece in (us[h], v * hms[h])], axis=0)
    y = _dot1(rt, s_old, _NT) + _dot1(jnp.concatenate(ly, axis=1), uv_rows)
    yield
    uv_t = jnp.concatenate([um, v], axis=0).T
    s_new = s_old * w_c + _dot3(uv_t, jnp.concatenate([bh, kh], axis=0)) * _head_ones(hw, N_RWKV)
    yield y, s_new


def _lockstep(gens):
    last = [None] * len(gens)
    live = list(range(len(gens)))
    while live:
        for i in list(live):
            try:
                last[i] = next(gens[i])
            except StopIteration:
                live.remove(i)
    return last


def _rwkv_scan_body(rf_ref, vf_ref, kkf_ref, lwf_ref, af_ref, kdf_ref, rb_ref, vb_ref, kkb_ref, lwb_ref, ab_ref,
                    kdb_ref, s0_ref, yf_ref, yb_ref, sf_ref, s_scr):
    c = pl.program_id(1)
    nc = pl.num_programs(1)

    @pl.when(c == 0)
    def _():
        s_scr[...] = s0_ref[...]

    nrow = rf_ref.shape[0]
    chains = []
    for i in range(nrow):
        chains.append(_rwkv_chunk(False, rf_ref[i], vf_ref[i], kkf_ref[i], lwf_ref[i], af_ref[i], kdf_ref[i],
                                  s_scr[i, 0]))
        chains.append(_rwkv_chunk(True, rb_ref[i], vb_ref[i], kkb_ref[i], lwb_ref[i], ab_ref[i], kdb_ref[i],
                                  s_scr[i, 1]))
    done = _lockstep(chains)
    for i in range(nrow):
        (y_f, s_f), (y_b, s_b) = done[2 * i], done[2 * i + 1]
        yf_ref[i] = y_f
        yb_ref[i] = y_b
        s_scr[i, 0] = s_f
        s_scr[i, 1] = s_b

    @pl.when(c == nc - 1)
    def _():
        sf_ref[...] = s_scr[...]


def _rwkv_scan(r, v, kk, lw, ag, kd, s0):
    nb, t, hw = r.shape
    cl = RWKV_CHUNK
    nr = RWKV_ROWS
    nc = t // cl
    fwd = lambda lane: pl.BlockSpec((nr, cl, hw), lambda b, c: (b, c, lane))
    bwd = lambda lane: pl.BlockSpec((nr, cl, hw), lambda b, c: (b, nc - 1 - c, lane))
    state = pl.BlockSpec((nr, 2, hw, hw), lambda b, c: (b, 0, 0, 0))
    y_shape = jax.ShapeDtypeStruct((nb, t, hw), F32)
    return pl.pallas_call(
        _rwkv_scan_body,
        grid=(nb // nr, nc),
        in_specs=[fwd(0)] * 6 + [bwd(0)] * 3 + [bwd(1)] * 3 + [state],
        out_specs=[fwd(0), bwd(0), state],
        out_shape=[y_shape, y_shape, jax.ShapeDtypeStruct((nb, 2, hw, hw), F32)],
        scratch_shapes=[pltpu.VMEM((nr, 2, hw, hw), F32)],
        compiler_params=_cparams("parallel", "arbitrary"),
        name="rwkv_scan",
    )(r, v, kk, lw, ag, kd, r, v, kk, lw, ag, kd, s0)


def _lru_body(reverse, p_ref, halo_ref, cw_ref, cb_ref, wa_ref, ba_ref, wx_ref, bx_ref, lam_ref, h0_ref,
              h_out, fin_out, a_scr, u_scr, carry_scr):
    i = pl.program_id(1)
    nt = pl.num_programs(1)
    tt = p_ref.shape[1]
    n_ext = tt + SUBLANES

    @pl.when(i == 0)
    def _():
        carry_scr[...] = jnp.broadcast_to(h0_ref[0], (SUBLANES, D_LRU))

    xb = p_ref[0, :, :D_LRU]
    halo = halo_ref[0, 0]
    xc = cb_ref[...] + jnp.zeros_like(xb)
    if reverse:
        xe = jnp.concatenate([xb, halo], axis=0)
        for k in range(CONV_W):
            sh = xb if k == 0 else pltpu.roll(xe, n_ext - k, 0)[:tt]
            xc = xc + cw_ref[k:k + 1, :] * sh
    else:
        xe = jnp.concatenate([halo, xb], axis=0)
        for k in range(CONV_W):
            s = CONV_W - 1 - k
            sh = xb if s == 0 else pltpu.roll(xe, s, 0)[SUBLANES:]
            xc = xc + cw_ref[k:k + 1, :] * sh
    gate_r = _sigmoid(_dot3(xc, wa_ref[...]) + ba_ref[...])
    gate_i = _sigmoid(_dot3(xc, wx_ref[...]) + bx_ref[...])
    log_a = -LRU_C * gate_r * _softplus(-lam_ref[...])
    a = jnp.exp(log_a)
    u = jnp.sqrt(-_expm1(2.0 * log_a)) * (gate_i * xc)

    rowm = lax.broadcasted_iota(I32, (tt, 1), 0) % SUBLANES
    for s in (1, 2, 4):
        if reverse:
            a_s = pltpu.roll(a, tt - s, 0)
            u_s = pltpu.roll(u, tt - s, 0)
            m = rowm < SUBLANES - s
        else:
            a_s = pltpu.roll(a, s, 0)
            u_s = pltpu.roll(u, s, 0)
            m = rowm >= s
        u = jnp.where(m, a * u_s + u, u)
        a = jnp.where(m, a * a_s, a)
    a_scr[...] = a
    u_scr[...] = u
    ng = tt // SUBLANES

    def group(g, carry):
        gi = (ng - 1 - g) if reverse else g
        r0 = pl.multiple_of(gi * SUBLANES, SUBLANES)
        hg = a_scr[pl.ds(r0, SUBLANES), :] * carry + u_scr[pl.ds(r0, SUBLANES), :]
        h_out[0, pl.ds(r0, SUBLANES), :] = hg
        edge = hg[0:1, :] if reverse else hg[SUBLANES - 1:SUBLANES, :]
        return jnp.broadcast_to(edge, (SUBLANES, D_LRU))

    carry = lax.fori_loop(0, ng, group, carry_scr[...])
    carry_scr[...] = carry

    @pl.when(i == nt - 1)
    def _():
        fin_out[0] = carry[0:1, :]


def _lru_dir(p_lru, halo, lp, d, h0, tt):
    nb, t, pw = p_lru.shape
    nt = t // tt
    reverse = d == 1
    tidx = (lambda i: nt - 1 - i) if reverse else (lambda i: i)
    small = [lp['conv_w'][d], lp['conv_b'][d:d + 1], lp['wa_bd'][d], lp['b_lru_a'][d:d + 1], lp['wx_bd'][d],
             lp['b_lru_x'][d:d + 1], lp['lam'][d:d + 1]]
    return pl.pallas_call(
        functools.partial(_lru_body, reverse),
        grid=(nb, nt),
        in_specs=[
            pl.BlockSpec((1, tt, pw), lambda b, i: (b, tidx(i), 0)),
            pl.BlockSpec((1, 1, SUBLANES, D_LRU), lambda b, i: (b, tidx(i), 0, 0)),
        ] + [_full(a.shape) for a in small] + [pl.BlockSpec((1, 1, D_LRU), lambda b, i: (b, 0, 0))],
        out_specs=[
            pl.BlockSpec((1, tt, D_LRU), lambda b, i: (b, tidx(i), 0)),
            pl.BlockSpec((1, 1, D_LRU), lambda b, i: (b, 0, 0)),
        ],
        out_shape=[jax.ShapeDtypeStruct((nb, t, D_LRU), F32), jax.ShapeDtypeStruct((nb, 1, D_LRU), F32)],
        scratch_shapes=[pltpu.VMEM((tt, D_LRU), F32), pltpu.VMEM((tt, D_LRU), F32), pltpu.VMEM((SUBLANES, D_LRU), F32)],
        compiler_params=_cparams("parallel", "arbitrary"),
        name="lru_bwd" if reverse else "lru_fwd",
    )(p_lru, halo, *small, h0)


def _gelu_tanh(x):
    return 0.5 * x * (1.0 + jnp.tanh(np.sqrt(2.0 / np.pi).astype(np.float32) * (x + 0.044715 * (x * x * x))))


def _out_proj_body(x_ref, o_ref, y0_ref, y1_ref, bonus_ref, g_ref, h0_ref, h1_ref, p_ref,
                   gmla_ref, lnw_ref, lnb_ref, glru_ref, wo_ref, gpost_ref, gate_ref, gpre_ref, sc_ref, sh_ref,
                   *rest):
    x1_out, h2_out = rest[-2:]
    o_mla = _rms(o_ref[0], gmla_ref[...])
    ones = _head_ones(D_RWKV, N_RWKV)
    y = y0_ref[0] + y1_ref[0]
    mu = _dot_x(y, ones) * (1.0 / N_RWKV)
    yc = y - mu
    var = _dot_x(yc * yc, ones) * (1.0 / N_RWKV)
    y = yc * lax.rsqrt(var + GN_EPS) * lnw_ref[...] + lnb_ref[...]
    o_rwkv = (y + bonus_ref[0]) * g_ref[0]
    o_lru = _rms((h0_ref[0] + h1_ref[0]) * _gelu_tanh(p_ref[0, :, D_LRU:]), glru_ref[...])
    m = (_dg(o_mla.astype(BF16), wo_ref[:D_MLA_OUT, :], _NN)
         + _dg(o_rwkv.astype(BF16), wo_ref[D_MLA_OUT:D_MLA_OUT + D_RWKV, :], _NN)
         + _dg(o_lru.astype(BF16), wo_ref[D_MLA_OUT + D_RWKV:, :], _NN))
    x1 = x_ref[...] + gate_ref[0] * _rms(m, gpost_ref[...])
    x1_out[...] = x1
    h2_out[...] = _rms(x1, gpre_ref[...]) * (1.0 + sc_ref[0]) + sh_ref[0]


def _out_proj(x, grp, o, y_f, y_b, bonus, g, h0, h1, p_lru, lp, mod, tt, prev):
    _, nb, t = grp
    per_batch = mod['g1'].shape[0] > 1
    tok = lambda n: pl.BlockSpec((1, tt, n), lambda b, i: (b, i, 0))
    rows = lambda: _group_rows(grp, tt)
    small = [lp['g_mla_out'], lp['ln_x_w'], lp['ln_x_b'], lp['g_lru_out'], lp['w_out'], lp['g_post_mix']]
    args = [x, o, y_f, y_b, bonus, g, h0, h1, p_lru, *small, mod['g1'], lp['g_pre_ffn'], mod['sc2'], mod['sh2']]
    in_specs = ([rows(), tok(D_MLA_OUT), tok(D_RWKV), tok(D_RWKV), tok(D_RWKV), tok(D_RWKV), tok(D_LRU), tok(D_LRU),
                 tok(P_LRU)] + [_full(a.shape) for a in small]
                + [_mod_spec(per_batch), _full(lp['g_pre_ffn'].shape), _mod_spec(per_batch), _mod_spec(per_batch)])
    aliases = {}
    if prev is not None:
        aliases = {len(args): 0, len(args) + 1: 1}
        args += list(prev)
        in_specs += [pl.BlockSpec(memory_space=pl.ANY)] * 2
    return pl.pallas_call(
        _out_proj_body,
        grid=(nb, t // tt),
        in_specs=in_specs,
        out_specs=[rows(), rows()],
        out_shape=[jax.ShapeDtypeStruct(x.shape, F32)] * 2,
        input_output_aliases=aliases,
        compiler_params=_cparams("parallel", "parallel"),
        name="out_proj",
    )(*args)


def _first_index(mask, idx, sentinel):
    return jnp.min(jnp.where(mask, idx, sentinel), axis=0, keepdims=True)


def _router_body(h_ref, wr_ref, br_ref, e_out, pos_out, w_out, cnt_out, carry):
    i = pl.program_id(0)
    tn = h_ref.shape[0]

    @pl.when(i == 0)
    def _():
        carry[...] = jnp.zeros_like(carry)

    logits = lax.dot_general(wr_ref[...], h_ref[...], _NT, precision=lax.Precision.HIGHEST,
                             preferred_element_type=F32)
    scores = _sigmoid(logits)
    sel = scores + br_ref[...]

    i8 = lax.broadcasted_iota(I32, (GROUP_SIZE, tn), 0)
    blocks, gscore = [], []
    for g in range(N_GROUPS):
        blk = sel[g * GROUP_SIZE:(g + 1) * GROUP_SIZE, :]
        m1 = jnp.max(blk, axis=0, keepdims=True)
        f1 = _first_index(blk == m1, i8, GROUP_SIZE)
        m2 = jnp.max(jnp.where(i8 == f1, -jnp.inf, blk), axis=0, keepdims=True)
        blocks.append(blk)
        gscore.append(m1 + m2)
    masked = []
    for g in range(N_GROUPS):
        beaten = jnp.zeros((1, tn), I32)
        for o in range(N_GROUPS):
            if o == g:
                continue
            wins = (gscore[o] >= gscore[g]) if o < g else (gscore[o] > gscore[g])
            beaten = beaten + wins.astype(I32)
        masked.append(jnp.where(beaten < TOPK_GROUPS, blocks[g], -jnp.inf))
    cur = jnp.concatenate(masked, axis=0)

    ie = lax.broadcasted_iota(I32, (N_EXPERTS, tn), 0)
    firsts, raw = [], []
    chosen = jnp.zeros((N_EXPERTS, tn), F32)
    for _ in range(TOP_K):
        mx = jnp.max(cur, axis=0, keepdims=True)
        first = _first_index(cur == mx, ie, N_EXPERTS)
        hit = ie == first
        firsts.append(first)
        raw.append(jnp.sum(jnp.where(hit, scores, 0.0), axis=0, keepdims=True))
        cur = jnp.where(hit, -jnp.inf, cur)
        chosen = chosen + hit.astype(F32)
    total = raw[0]
    for k in range(1, TOP_K):
        total = total + raw[k]

    before = (lax.broadcasted_iota(I32, (tn, tn), 0) < lax.broadcasted_iota(I32, (tn, tn), 1)).astype(BF16)
    rank = _dg(chosen.astype(BF16), before, _NN) + carry[:, 0:1]
    e_out[...] = jnp.zeros_like(e_out)
    pos_out[...] = jnp.zeros_like(pos_out)
    w_out[...] = jnp.zeros_like(w_out)
    for k in range(TOP_K):
        e_out[0, k:k + 1, :] = firsts[k]
        pos_out[0, k:k + 1, :] = jnp.sum(jnp.where(ie == firsts[k], rank, 0.0), axis=0, keepdims=True).astype(I32)
        w_out[k:k + 1, :] = raw[k] / total * ROUTED_SCALE
    new_carry = carry[...] + jnp.sum(chosen, axis=1, keepdims=True)
    carry[...] = new_carry
    cnt_out[...] = new_carry


def _router(h2, wr_t, b_col):
    n, d = h2.shape
    tn = ROUTER_TILE
    row_spec = pl.BlockSpec((SLOT_ROWS, tn), lambda i: (0, i))
    tile_spec = pl.BlockSpec((1, SLOT_ROWS, tn), lambda i: (i, 0, 0))
    cnt_spec = pl.BlockSpec((N_EXPERTS, 128), lambda i: (0, 0))
    tile_shape = jax.ShapeDtypeStruct((n // tn, SLOT_ROWS, tn), I32)
    return pl.pallas_call(
        _router_body,
        grid=(n // tn,),
        in_specs=[pl.BlockSpec((tn, d), lambda i: (i, 0)), _full(wr_t.shape), _full(b_col.shape)],
        out_specs=[tile_spec, tile_spec, row_spec, cnt_spec],
        out_shape=[tile_shape, tile_shape,
                   jax.ShapeDtypeStruct((SLOT_ROWS, n), F32), jax.ShapeDtypeStruct((N_EXPERTS, 128), F32)],
        scratch_shapes=[pltpu.VMEM((N_EXPERTS, 128), F32)],
        compiler_params=_cparams("arbitrary"),
        name="router",
    )(h2, wr_t, b_col)


ROW_LANES = 128
ROW_SUB = D_MODEL // 2 // ROW_LANES
U32 = jnp.uint32
HIGH_HALF = np.uint32(0xFFFF0000)


def _bf16_bits(x):
    return lax.bitcast_convert_type(x.astype(BF16).astype(F32), U32)


def _to_row_tiles(ref, x):
    rows, d = x.shape
    words = (_bf16_bits(x[:, :d // 2]) >> 16) | (_bf16_bits(x[:, d // 2:]) & HIGH_HALF)
    for s in range(ROW_SUB):
        ref[pl.ds(s, rows, stride=ROW_SUB), :] = words[:, s * ROW_LANES:(s + 1) * ROW_LANES]


def _from_row_tiles(ref, rows):
    words = jnp.concatenate([ref[pl.ds(s, rows, stride=ROW_SUB), :] for s in range(ROW_SUB)], axis=1)
    low = lax.bitcast_convert_type(words << 16, F32)
    high = lax.bitcast_convert_type(words & HIGH_HALF, F32)
    return jnp.concatenate([low, high], axis=1)


def _row_copy(src, src_off, dst, dst_off, sem):
    return pltpu.make_async_copy(src.at[pl.ds(pl.multiple_of(src_off, ROW_SUB), ROW_SUB), :],
                                 dst.at[pl.ds(pl.multiple_of(dst_off, ROW_SUB), ROW_SUB), :], sem)


def _load_slots(slot_ref, slot_smem, sem):
    cp = pltpu.make_async_copy(slot_ref.at[0], slot_smem, sem)
    cp.start()
    cp.wait()


def _dispatch_body(start_ref, e_ref, pos_ref, h_ref, xs_in, slots_out, xs_out, slot_smem, h_tiles, sem_s, sem):
    del xs_in
    tn = slot_smem.shape[1]
    e = e_ref[0]
    slot = pos_ref[0]
    for x in range(N_EXPERTS):
        slot = slot + jnp.where(e == x, start_ref[x], 0)
    slots_out[0] = slot * ROW_SUB
    _load_slots(slots_out, slot_smem, sem_s)
    _to_row_tiles(h_tiles, h_ref[...])

    def issue(j, carry):
        for k in range(TOP_K):
            _row_copy(h_tiles, j * ROW_SUB, xs_out, slot_smem[k, j], sem).start(priority=k % 2)
        return carry

    def drain(j, carry):
        for k in range(TOP_K):
            _row_copy(h_tiles, 0, xs_out, 0, sem).wait()
        return carry

    lax.fori_loop(0, tn, issue, 0)
    lax.fori_loop(0, tn, drain, 0)


def _zero_tail_body(lb_ref, o_ref):
    del lb_ref
    o_ref[...] = jnp.zeros_like(o_ref)


def _zero_tails(last_block, n_blocks):
    rows = MOE_BLOCK * ROW_SUB
    grid_spec = pltpu.PrefetchScalarGridSpec(
        num_scalar_prefetch=1,
        grid=(last_block.shape[0],),
        in_specs=[],
        out_specs=pl.BlockSpec((rows, ROW_LANES), lambda e, lb: (lb[e], 0)),
    )
    return pl.pallas_call(
        _zero_tail_body,
        grid_spec=grid_spec,
        out_shape=jax.ShapeDtypeStruct((n_blocks * rows, ROW_LANES), U32),
        compiler_params=_cparams("arbitrary"),
        name="zero_tails",
    )(last_block)


def _dispatch(row_start, e3, pos3, h2, xs_zero, tn):
    n, d = h2.shape
    ratio = e3.shape[2] // tn
    tile = lambda: pl.BlockSpec((1, SLOT_ROWS, tn), lambda i, st: (i // ratio, 0, i % ratio))
    grid_spec = pltpu.PrefetchScalarGridSpec(
        num_scalar_prefetch=1,
        grid=(n // tn,),
        in_specs=[tile(), tile(), pl.BlockSpec((tn, d), lambda i, st: (i, 0)), pl.BlockSpec(memory_space=pl.ANY)],
        out_specs=[pl.BlockSpec((1, SLOT_ROWS, tn), lambda i, st: (i, 0, 0)), pl.BlockSpec(memory_space=pl.ANY)],
        scratch_shapes=[pltpu.SMEM((SLOT_ROWS, tn), I32), pltpu.VMEM((tn * ROW_SUB, ROW_LANES), U32),
                        pltpu.SemaphoreType.DMA, pltpu.SemaphoreType.DMA],
    )
    return pl.pallas_call(
        _dispatch_body,
        grid_spec=grid_spec,
        out_shape=[jax.ShapeDtypeStruct((n // tn, SLOT_ROWS, tn), I32),
                   jax.ShapeDtypeStruct(xs_zero.shape, xs_zero.dtype)],
        input_output_aliases={4: 1},
        compiler_params=_cparams("arbitrary"),
        name="dispatch",
    )(row_start, e3, pos3, h2, xs_zero)


def _swiglu(x, wg, wu, wd):
    g = _dg(x, wg, _NN)
    u = _dg(x, wu, _NN)
    return _dg((g * _sigmoid(g) * u).astype(BF16), wd, _NN)


def _expert_body(be_ref, nu_ref, xs_ref, wg_ref, wu_ref, wd_ref, ys_ref):
    del be_ref
    b = pl.program_id(0)

    @pl.when(b < nu_ref[0])
    def _():
        x = _from_row_tiles(xs_ref, MOE_BLOCK).astype(BF16)
        _to_row_tiles(ys_ref, _swiglu(x, wg_ref[0], wu_ref[0], wd_ref[0]))

    @pl.when(b >= nu_ref[0])
    def _():
        ys_ref[...] = jnp.zeros_like(ys_ref)


def _experts(block_e, n_used, xs, wg, wu, wd):
    bm = MOE_BLOCK
    _, d, de = wg.shape
    rows = bm * ROW_SUB
    grid_spec = pltpu.PrefetchScalarGridSpec(
        num_scalar_prefetch=2,
        grid=(xs.shape[0] // rows,),
        in_specs=[
            pl.BlockSpec((rows, ROW_LANES), lambda b, be, nu: (jnp.minimum(b, nu[0] - 1), 0)),
            pl.BlockSpec((1, d, de), lambda b, be, nu: (be[b], 0, 0)),
            pl.BlockSpec((1, d, de), lambda b, be, nu: (be[b], 0, 0)),
            pl.BlockSpec((1, de, d), lambda b, be, nu: (be[b], 0, 0)),
        ],
        out_specs=pl.BlockSpec((rows, ROW_LANES), lambda b, be, nu: (b, 0)),
    )
    return pl.pallas_call(
        _expert_body,
        grid_spec=grid_spec,
        out_shape=jax.ShapeDtypeStruct(xs.shape, xs.dtype),
        compiler_params=_cparams("arbitrary"),
        name="experts",
    )(block_e, n_used, xs, wg, wu, wd)


def _combine_body(slot_ref, w_ref, x1_ref, h2_ref, gate_ref, ys_hbm, wsg_ref, wsu_ref, wsd_ref, gpost_ref,
                  out_ref, slot_smem, buf, sem_s, sem):
    tn = slot_smem.shape[1]
    _load_slots(slot_ref, slot_smem, sem_s)

    def issue(j, carry):
        for k in range(TOP_K):
            _row_copy(ys_hbm, slot_smem[k, j], buf.at[k], j * ROW_SUB, sem).start(priority=k % 2)
        return carry

    def drain(j, carry):
        for k in range(TOP_K):
            _row_copy(ys_hbm, 0, buf.at[k], 0, sem).wait()
        return carry

    lax.fori_loop(0, tn, issue, 0)
    acc = _swiglu(h2_ref[...].astype(BF16), wsg_ref[...], wsu_ref[...], wsd_ref[...])
    lax.fori_loop(0, tn, drain, 0)
    for k in range(TOP_K):
        acc = acc + _from_row_tiles(buf.at[k], tn) * w_ref[:, k:k + 1]
    out_ref[...] = x1_ref[...] + gate_ref[0] * _rms(acc, gpost_ref[...])


def _combine(slots3, w_tok, x1, h2, gate_rows, ys, wsg, wsu, wsd, gpost):
    ntiles, _, tn = slots3.shape
    n, d = x1.shape
    tok = pl.BlockSpec((tn, d), lambda i: (i, 0))
    return pl.pallas_call(
        _combine_body,
        grid=(ntiles,),
        in_specs=[pl.BlockSpec((1, SLOT_ROWS, tn), lambda i: (i, 0, 0)),
                  pl.BlockSpec((tn, SLOT_ROWS), lambda i: (i, 0)), tok, tok,
                  pl.BlockSpec((1, 1, d), lambda i: (i, 0, 0)),
                  pl.BlockSpec(memory_space=pl.ANY),
                  _full(wsg.shape), _full(wsu.shape), _full(wsd.shape), _full(gpost.shape)],
        out_specs=tok,
        out_shape=jax.ShapeDtypeStruct((n, d), F32),
        scratch_shapes=[pltpu.SMEM((SLOT_ROWS, tn), I32), pltpu.VMEM((TOP_K, tn * ROW_SUB, ROW_LANES), U32),
                        pltpu.SemaphoreType.DMA, pltpu.SemaphoreType.DMA],
        compiler_params=_cparams("arbitrary"),
        name="combine",
    )(slots3, w_tok, x1, h2, gate_rows, ys, wsg, wsu, wsd, gpost)


def _moe(x1, h2, gate_rows, lp):
    n, d = h2.shape
    bm = MOE_BLOCK
    tn = TOKEN_TILE
    e3, pos3, w_rows, cnt = _router(h2, lp['wr_t'], lp['b_router_col'])
    counts = cnt[:, 0].astype(I32)
    blocks_per_e = (counts + bm - 1) // bm
    blk_end = jnp.cumsum(blocks_per_e)
    blk_start = blk_end - blocks_per_e
    n_blocks = -(-(n * TOP_K) // bm) + N_EXPERTS
    block_e = jnp.minimum(jnp.sum(blk_end[None, :] <= jnp.arange(n_blocks, dtype=I32)[:, None], axis=1),
                          N_EXPERTS - 1).astype(I32)
    n_used = blk_end[-1:].astype(I32)
    xs_tails = _zero_tails(jnp.clip(blk_end - 1, 0, n_blocks - 1).astype(I32), n_blocks)
    slots3, xs = _dispatch((blk_start * bm).astype(I32), e3, pos3, h2, xs_tails, tn)
    ys = _experts(block_e, n_used, xs, lp['w_e_gate'], lp['w_e_up'], lp['w_e_down'])
    return _combine(slots3, w_rows.T, x1, h2, gate_rows, ys, lp['w_s_gate'], lp['w_s_up'], lp['w_s_down'],
                    lp['g_post_ffn'])


def _rope_tables(n_tokens):
    rows = n_tokens // GRID_W
    row = jnp.repeat(jnp.arange(rows, dtype=F32), GRID_W)
    col = jnp.tile(jnp.arange(GRID_W, dtype=F32), rows)
    n_freq = ROPE_DIM // 4
    inv_freq = ROPE_THETA ** (-jnp.arange(n_freq, dtype=F32) / n_freq)
    ang_r = row[:, None] * inv_freq
    ang_c = col[:, None] * inv_freq
    ang = jnp.concatenate([ang_r, ang_r, ang_c, ang_c], axis=-1)
    cos, sin = jnp.cos(ang), jnp.sin(ang)
    pad = lambda z, fill: jnp.concatenate(
        [jnp.full((n_tokens, NOPE_DIM), fill, F32), z, jnp.zeros((n_tokens, HEAD_PAD - QK_DIM), F32)], axis=1)
    return pad(cos, 1.0), pad(sin, 0.0), pad(cos, 0.0), pad(sin, 0.0)


def _rot_cols(w):
    q = ROPE_DIM // 4
    perm = np.concatenate([np.arange(q, 2 * q), np.arange(0, q), np.arange(3 * q, 4 * q), np.arange(2 * q, 3 * q)])
    sign = np.concatenate([-np.ones(q), np.ones(q), -np.ones(q), np.ones(q)]).astype(np.float32)
    return w[..., perm] * sign


def _block_diag(blocks):
    h, m, n = blocks.shape[-3:]
    eye = jnp.eye(h, dtype=blocks.dtype)
    out = blocks[..., :, :, None, :] * eye[:, None, :, None]
    return out.reshape(blocks.shape[:-3] + (h * m, h * n))


def _diag_blocks(mat, h):
    m, n = mat.shape[-2] // h, mat.shape[-1] // h
    z = mat.reshape(mat.shape[:-2] + (h, m, h, n))
    return jnp.stack([z[..., i, :, i, :] for i in range(h)], axis=-3)


def _layer_operands(l, a):
    w_in = a['w_in'][l]
    d = w_in.shape[0]
    p_mla = Q_LORA + KV_LORA + ROPE_DIM
    w_kr = w_in[:, Q_LORA + KV_LORA:p_mla]
    z32 = jnp.zeros((d, ROPE_DIM), F32)
    w_mla = jnp.concatenate([w_in[:, :Q_LORA + KV_LORA], z32, _rot_cols(w_kr), w_kr, z32], axis=1)
    wq3 = a['w_q_b'][l].reshape(Q_LORA, H_MLA, QK_DIM)
    zq = lambda n: jnp.zeros((Q_LORA, H_MLA, n), F32)
    wq = jnp.concatenate([wq3, zq(HEAD_PAD - QK_DIM)], axis=2).reshape(Q_LORA, H_MLA * HEAD_PAD)
    wqr = jnp.concatenate([zq(NOPE_DIM), _rot_cols(wq3[:, :, NOPE_DIM:]), zq(HEAD_PAD - QK_DIM)],
                          axis=2).reshape(Q_LORA, H_MLA * HEAD_PAD)
    wkv3 = a['w_kv_b'][l].reshape(KV_LORA, H_MLA, NOPE_DIM + V_DIM)
    wk = jnp.concatenate([wkv3[:, :, :NOPE_DIM], jnp.zeros((KV_LORA, H_MLA, HEAD_PAD - NOPE_DIM), F32)],
                         axis=2).reshape(KV_LORA, H_MLA * HEAD_PAD)
    wv = wkv3[:, :, NOPE_DIM:].reshape(KV_LORA, H_MLA * V_DIM)
    row = lambda z: z.reshape(1, -1)
    two = lambda z: jnp.concatenate([z[0], z[1]], axis=-1)
    bd2 = lambda z: jnp.concatenate([jnp.concatenate([z[0], jnp.zeros_like(z[0])], axis=1),
                                     jnp.concatenate([jnp.zeros_like(z[1]), z[1]], axis=1)], axis=0)
    return {
        'g_pre_mix': row(a['g_pre_mix'][l]), 'g_post_mix': row(a['g_post_mix'][l]),
        'g_pre_ffn': row(a['g_pre_ffn'][l]), 'g_post_ffn': row(a['g_post_ffn'][l]),
        'w_mla': w_mla.astype(BF16),
        'w_rwkv': w_in[:, p_mla:p_mla + P_RWKV].astype(BF16),
        'w_lru': w_in[:, p_mla + P_RWKV:].astype(BF16),
        'g_q_a': row(a['g_q_a'][l]), 'g_kv_a': row(a['g_kv_a'][l]),
        'wq': wq.astype(BF16), 'wqr': wqr.astype(BF16), 'wk': wk.astype(BF16), 'wv': wv.astype(BF16),
        'g_mla_out': row(a['g_mla_out'][l]),
        'mu_rwkv': row(a['mu_rwkv'][l]), 'k_k': row(a['k_k'][l]), 'k_a': row(a['k_a'][l]), 'r_k': row(a['r_k'][l]),
        'w0_cat': row(two(a['w0_rwkv'][l])), 'w2_bd': bd2(a['w2_rwkv'][l]),
        'a0_cat': row(two(a['a0_rwkv'][l])), 'a2_bd': bd2(a['a2_rwkv'][l]),
        'g2_rwkv': a['g2_rwkv'][l], 'ln_x_w': row(a['ln_x_w'][l]), 'ln_x_b': row(a['ln_x_b'][l]),
        'conv_w': a['conv_w'][l], 'conv_b': a['conv_b'][l],
        'wa_bd': _block_diag(a['w_lru_a'][l]), 'b_lru_a': a['b_lru_a'][l],
        'wx_bd': _block_diag(a['w_lru_x'][l]), 'b_lru_x': a['b_lru_x'][l],
        'lam': a['lam'][l], 'g_lru_out': row(a['g_lru_out'][l]),
        'w_out': a['w_out'][l].astype(BF16),
        'wr_t': a['w_router'][l].T, 'b_router_col': a['b_router'][l].reshape(N_EXPERTS, 1),
        'w_e_gate': a['w_e_gate'][l].astype(BF16), 'w_e_up': a['w_e_up'][l].astype(BF16),
        'w_e_down': a['w_e_down'][l].astype(BF16),
        'w_s_gate': a['w_s_gate'][l].astype(BF16), 'w_s_up': a['w_s_up'][l].astype(BF16),
        'w_s_down': a['w_s_down'][l].astype(BF16),
    }


def _mixer(x, grp, mod, lp, ctx, tables, prev):
    _, nb, t = grp
    tt = min(t, 512)
    p_mla, p_rwkv, p_lru = _in_proj(x, grp, lp['g_pre_mix'], mod['sc1'], mod['sh1'], lp['w_mla'], lp['w_rwkv'],
                                    lp['w_lru'], tt)
    if ctx is None:
        q, k, v, new_cache = _mla_prep(p_mla, lp['g_q_a'], lp['g_kv_a'], lp['wq'], None, lp['wk'], lp['wv'], None, tt)
        kc = vc = None
        s0 = jnp.zeros((nb, 2, D_RWKV, D_RWKV), F32)
        h0 = jnp.zeros((nb, 2, 1, D_LRU), F32)
    else:
        cache, s_rwkv, s_lru = ctx
        q, k, v = _mla_prep(p_mla, lp['g_q_a'], lp['g_kv_a'], lp['wq'], lp['wqr'], lp['wk'], lp['wv'], tables, tt)
        new_cache = None
        kr_placed = jnp.pad(cache[..., KV_LORA:], ((0, 0), (0, 0), (NOPE_DIM, HEAD_PAD - QK_DIM)))
        kc, vc = _ctx_kv(cache[..., :KV_LORA], kr_placed, lp['wk'], lp['wv'])
        s0 = _block_diag(s_rwkv)
        h0 = s_lru[:, :, None, :]
    o = _attention(q, k, v, kc, vc, min(t, 256))

    r, vv, kk, lw, ag, kd, g, bonus = _rwkv_prep(p_rwkv, lp, tt)
    y_f, y_b, s_fin = _rwkv_scan(r, vv, kk, lw, ag, kd, s0)

    hp, hn = _halo_rows(p_lru[..., :D_LRU], tt, SUBLANES)
    h_f, fin_f = _lru_dir(p_lru, hp, lp, 0, h0[:, 0], tt)
    h_b, fin_b = _lru_dir(p_lru, hn, lp, 1, h0[:, 1], tt)

    x1, h2 = _out_proj(x, grp, o, y_f, y_b, bonus, g, h_f, h_b, p_lru, lp, mod, tt, prev)
    states = None
    if ctx is None:
        states = (new_cache, _diag_blocks(s_fin, H_RWKV), jnp.concatenate([fin_f, fin_b], axis=1))
    return x1, h2, states


def kernel(x_prompt, x_sample, c, cache_mla, state_rwkv, state_lru, c_ctx, w_ada, b_ada, g_pre_mix, g_post_mix, g_pre_ffn, g_post_ffn, w_in, g_q_a, w_q_b, g_kv_a, w_kv_b, g_mla_out, mu_rwkv, w0_rwkv, w2_rwkv, a0_rwkv, a2_rwkv, g2_rwkv, k_k, k_a, r_k, ln_x_w, ln_x_b, conv_w, conv_b, w_lru_a, b_lru_a, w_lru_x, b_lru_x, lam, g_lru_out, w_out, w_router, b_router, w_e_gate, w_e_up, w_e_down, w_s_gate, w_s_up, w_s_down):
    a = dict(w_in=w_in, g_pre_mix=g_pre_mix, g_post_mix=g_post_mix, g_pre_ffn=g_pre_ffn, g_post_ffn=g_post_ffn,
             g_q_a=g_q_a, w_q_b=w_q_b, g_kv_a=g_kv_a, w_kv_b=w_kv_b, g_mla_out=g_mla_out, mu_rwkv=mu_rwkv,
             w0_rwkv=w0_rwkv, w2_rwkv=w2_rwkv, a0_rwkv=a0_rwkv, a2_rwkv=a2_rwkv, g2_rwkv=g2_rwkv, k_k=k_k, k_a=k_a,
             r_k=r_k.reshape(r_k.shape[0], -1), ln_x_w=ln_x_w, ln_x_b=ln_x_b, conv_w=conv_w, conv_b=conv_b,
             w_lru_a=w_lru_a, b_lru_a=b_lru_a, w_lru_x=w_lru_x, b_lru_x=b_lru_x, lam=lam, g_lru_out=g_lru_out,
             w_out=w_out, w_router=w_router, b_router=b_router, w_e_gate=w_e_gate, w_e_up=w_e_up,
             w_e_down=w_e_down, w_s_gate=w_s_gate, w_s_up=w_s_up, w_s_down=w_s_down)
    n_layers = w_in.shape[0]
    nbp, tp, d = x_prompt.shape
    nbs, ts, _ = x_sample.shape
    rows = -(-(1 + nbs) // SUBLANES) * SUBLANES
    cvecs = jnp.concatenate([c_ctx[None, :], c, jnp.zeros((rows - 1 - nbs, d), F32)], axis=0)
    mods = _ada(cvecs, w_ada, b_ada)
    names = ('sh1', 'sc1', 'g1', 'sh2', 'sc2', 'g2')
    tables = _rope_tables(ts)
    tn = TOKEN_TILE

    grp_p, grp_s = (0, nbp, tp), (nbp * tp, nbs, ts)
    x = jnp.concatenate([x_prompt.reshape(-1, d), x_sample.reshape(-1, d)], axis=0)
    caches, rwkv_states, lru_states = [], [], []
    for l in range(n_layers):
        lp = _layer_operands(l, a)
        mod_p = {nm: mods[l, 0:1, i * d:(i + 1) * d][:, None, :] for i, nm in enumerate(names)}
        mod_s = {nm: mods[l, 1:1 + nbs, i * d:(i + 1) * d][:, None, :] for i, nm in enumerate(names)}
        x1, h2, states = _mixer(x, grp_p, mod_p, lp, None, None, None)
        x1, h2, _ = _mixer(x, grp_s, mod_s, lp, (cache_mla[:, l], state_rwkv[:, l], state_lru[:, l]), tables,
                           (x1, h2))
        caches.append(states[0])
        rwkv_states.append(states[1])
        lru_states.append(states[2])
        gate_rows = jnp.concatenate([jnp.repeat(mod_p['g2'], nbp * tp // tn, axis=0),
                                     jnp.repeat(mod_s['g2'], ts // tn, axis=0)], axis=0)
        x = _moe(x1, h2, gate_rows, lp)
    yp = x[:nbp * tp].reshape(nbp, tp, d)
    ys = x[nbp * tp:].reshape(nbs, ts, d)
    return (yp, ys, jnp.stack(caches, axis=1), jnp.stack(rwkv_states, axis=1), jnp.stack(lru_states, axis=1))
```

```python
import functools

import numpy as np
import jax
import jax.numpy as jnp
from jax import lax
from jax.experimental import pallas as pl
from jax.experimental.pallas import tpu as pltpu

F32 = jnp.float32
BF16 = jnp.bfloat16
I32 = jnp.int32

D_MODEL = 1024
GRID_W = 64
EPS = 1e-6
H_MLA = 4
Q_LORA = 256
KV_LORA = 128
NOPE_DIM = 64
ROPE_DIM = 32
V_DIM = 128
QK_DIM = NOPE_DIM + ROPE_DIM
ROPE_THETA = 10000.0
HEAD_PAD = 128
H_RWKV = 4
N_RWKV = 64
D_RWKV = H_RWKV * N_RWKV
W_LORA = 64
A_LORA = 64
G_LORA = 128
GN_EPS = 64e-5
D_LRU = 256
H_LRU = 4
BS_LRU = D_LRU // H_LRU
CONV_W = 4
LRU_C = 8.0
D_MLA_OUT = H_MLA * V_DIM
P_MLA_PAD = Q_LORA + KV_LORA + HEAD_PAD
P_RWKV = 3 * D_RWKV + 2 * W_LORA + 2 * A_LORA + G_LORA
P_LRU = 2 * D_LRU
N_EXPERTS = 64
TOP_K = 6
N_GROUPS = 8
TOPK_GROUPS = 4
GROUP_SIZE = N_EXPERTS // N_GROUPS
D_EXPERT = 256
ROUTED_SCALE = 2.5

SUBLANES = 8
RWKV_CHUNK = 64
RWKV_ROWS = 2
MOE_BLOCK = 512
ROUTER_TILE = 512
TOKEN_TILE = 256
SLOT_ROWS = 8
VMEM_LIMIT = 56 * 1024 * 1024


def _cparams(*sem):
    return pltpu.CompilerParams(dimension_semantics=sem, vmem_limit_bytes=VMEM_LIMIT)


def _sigmoid(x):
    return 1.0 / (1.0 + jnp.exp(-x))


def _softplus(x):
    return jnp.maximum(x, 0.0) + jnp.log1p(jnp.exp(-jnp.abs(x)))


def _expm1(z):
    e = jnp.exp(z)
    direct = (e == 1.0) | (z < -1.0)
    corrected = (e - 1.0) * z / jnp.log(jnp.where(direct, 2.0, e))
    return jnp.where(e == 1.0, z, jnp.where(z < -1.0, e - 1.0, corrected))


def _rms(x, g):
    return x * lax.rsqrt(jnp.mean(x * x, axis=-1, keepdims=True) + EPS) * g


def _split2(x):
    hi = x.astype(BF16)
    lo = (x - hi.astype(F32)).astype(BF16)
    return hi, lo


def _split3(x):
    hi = x.astype(BF16)
    r1 = x - hi.astype(F32)
    mid = r1.astype(BF16)
    lo = (r1 - mid.astype(F32)).astype(BF16)
    return hi, mid, lo


_NN = (((1,), (0,)), ((), ()))
_NT = (((1,), (1,)), ((), ()))


def _dg(a, b, dims):
    return lax.dot_general(a, b, dims, preferred_element_type=F32)


def _dot1(a, b, dims=_NN):
    return _dg(a.astype(BF16), b.astype(BF16), dims)


def _dot3(a, b, dims=_NN):
    ah, al = _split2(a)
    bh, bl = _split2(b)
    return _dg(ah, bh, dims) + (_dg(ah, bl, dims) + _dg(al, bh, dims))


def _dot_x(a, b_exact, dims=_NN):
    h, m, l = _split3(a)
    bb = b_exact.astype(BF16)
    return _dg(h, bb, dims) + (_dg(m, bb, dims) + _dg(l, bb, dims))


def _x_dot(a_exact, b, dims=_NN):
    h, m, l = _split3(b)
    aa = a_exact.astype(BF16)
    return _dg(aa, h, dims) + (_dg(aa, m, dims) + _dg(aa, l, dims))


def _head_ones(n, seg):
    r = lax.broadcasted_iota(I32, (n, n), 0) // seg
    c = lax.broadcasted_iota(I32, (n, n), 1) // seg
    return (r == c).astype(F32)


def _ada_body(c_ref, w_ref, b_ref, o_ref):
    cv = c_ref[...]
    s = cv * _sigmoid(cv)
    o_ref[0] = _dot3(s, w_ref[0]) + b_ref[0]


def _ada(cvecs, w_ada, b_ada):
    n_layers, d, d6 = w_ada.shape
    rows = cvecs.shape[0]
    nt = d6 // d
    return pl.pallas_call(
        _ada_body,
        grid=(n_layers, nt),
        in_specs=[
            pl.BlockSpec((rows, d), lambda l, j: (0, 0)),
            pl.BlockSpec((1, d, d), lambda l, j: (l, 0, j)),
            pl.BlockSpec((1, 1, d), lambda l, j: (l, 0, j)),
        ],
        out_specs=pl.BlockSpec((1, rows, d), lambda l, j: (l, 0, j)),
        out_shape=jax.ShapeDtypeStruct((n_layers, rows, d6), F32),
        compiler_params=_cparams("parallel", "parallel"),
        name="ada",
    )(cvecs, w_ada, b_ada.reshape(n_layers, 1, d6))


def _mod_spec(per_batch):
    if per_batch:
        return pl.BlockSpec((1, 1, D_MODEL), lambda b, i: (b, 0, 0))
    return pl.BlockSpec((1, 1, D_MODEL), lambda b, i: (0, 0, 0))


def _full(shape):
    nd = len(shape)
    return pl.BlockSpec(shape, lambda *_: (0,) * nd)


def _in_proj_body(x_ref, g_ref, sc_ref, sh_ref, w1_ref, w2_ref, w3_ref, o1_ref, o2_ref, o3_ref):
    h = _rms(x_ref[...], g_ref[...]) * (1.0 + sc_ref[0]) + sh_ref[0]
    hb = h.astype(BF16)
    o1_ref[0] = _dg(hb, w1_ref[...], _NN)
    o2_ref[0] = _dg(hb, w2_ref[...], _NN)
    o3_ref[0] = _dg(hb, w3_ref[...], _NN)


def _group_rows(grp, tt):
    row_off, nb, t = grp
    base, per = row_off // tt, t // tt
    return pl.BlockSpec((tt, D_MODEL), lambda b, i: (base + b * per + i, 0))


def _in_proj(x, grp, g, sc, sh, w_mla, w_rwkv, w_lru, tt):
    _, nb, t = grp
    d = x.shape[1]
    per_batch = sc.shape[0] > 1
    outs = [w_mla.shape[1], w_rwkv.shape[1], w_lru.shape[1]]
    return pl.pallas_call(
        _in_proj_body,
        grid=(nb, t // tt),
        in_specs=[
            _group_rows(grp, tt),
            _full((1, d)),
            _mod_spec(per_batch),
            _mod_spec(per_batch),
            _full(w_mla.shape),
            _full(w_rwkv.shape),
            _full(w_lru.shape),
        ],
        out_specs=[pl.BlockSpec((1, tt, n), lambda b, i: (b, i, 0)) for n in outs],
        out_shape=[jax.ShapeDtypeStruct((nb, t, n), F32) for n in outs],
        compiler_params=_cparams("parallel", "parallel"),
        name="in_proj",
    )(x, g, sc, sh, w_mla, w_rwkv, w_lru)


def _rope_lanes():
    lane = lax.broadcasted_iota(I32, (1, HEAD_PAD), 1)
    return ((lane >= NOPE_DIM) & (lane < QK_DIM)).astype(F32)


def _mla_prep_body(rope, *refs):
    if rope:
        (p_ref, gq_ref, gkv_ref, wq_ref, wqr_ref, wk_ref, wv_ref, cq_ref, sq_ref, cr_ref, sr_ref,
         q_ref, k_ref, v_ref) = refs
    else:
        p_ref, gq_ref, gkv_ref, wq_ref, wk_ref, wv_ref, q_ref, k_ref, v_ref, c_ref = refs
    p = p_ref[0]
    cq = _rms(p[:, :Q_LORA], gq_ref[...])
    ckv = _rms(p[:, Q_LORA:Q_LORA + KV_LORA], gkv_ref[...])
    kr = p[:, Q_LORA + KV_LORA:]
    cqb = cq.astype(BF16)
    ckvb = ckv.astype(BF16)
    scale = QK_DIM ** -0.5 * float(np.log2(np.e))
    qa = _dg(cqb, wq_ref[...], _NN) * scale
    ka = _dg(ckvb, wk_ref[...], _NN)
    va = _dg(ckvb, wv_ref[...], _NN)
    if rope:
        qr = _dg(cqb, wqr_ref[...], _NN) * scale
        krp = kr * cr_ref[...] + pltpu.roll(kr, ROPE_DIM, 1) * sr_ref[...]
    else:
        krp = kr * _rope_lanes()
    for h in range(H_MLA):
        sl = slice(h * HEAD_PAD, (h + 1) * HEAD_PAD)
        qh = qa[:, sl]
        if rope:
            qh = qh * cq_ref[...] + qr[:, sl] * sq_ref[...]
        q_ref[0, h] = qh.astype(BF16)
        k_ref[0, h] = (ka[:, sl] + krp).astype(BF16)
        v_ref[0, h] = va[:, sl].astype(BF16)
    if not rope:
        c_ref[0, :, :KV_LORA] = ckv
        c_ref[0, :, KV_LORA:] = pltpu.roll(kr, HEAD_PAD - NOPE_DIM, 1)[:, :ROPE_DIM]


def _mla_prep(p_mla, gq, gkv, wq, wqr, wk, wv, tables, tt):
    nb, t, pw = p_mla.shape
    rope = tables is not None
    hw = H_MLA * HEAD_PAD
    head_spec = pl.BlockSpec((1, H_MLA, tt, HEAD_PAD), lambda b, i: (b, 0, i, 0))
    head_shape = jax.ShapeDtypeStruct((nb, H_MLA, t, HEAD_PAD), BF16)
    in_specs = [pl.BlockSpec((1, tt, pw), lambda b, i: (b, i, 0)), _full(gq.shape), _full(gkv.shape), _full(wq.shape)]
    args = [p_mla, gq, gkv, wq]
    if rope:
        in_specs.append(_full(wqr.shape))
        args.append(wqr)
    in_specs += [_full(wk.shape), _full(wv.shape)]
    args += [wk, wv]
    out_specs = [head_spec, head_spec, head_spec]
    out_shape = [head_shape, head_shape, head_shape]
    if rope:
        in_specs += [pl.BlockSpec((tt, HEAD_PAD), lambda b, i: (i, 0))] * 4
        args += list(tables)
    else:
        out_specs.append(pl.BlockSpec((1, tt, KV_LORA + ROPE_DIM), lambda b, i: (b, i, 0)))
        out_shape.append(jax.ShapeDtypeStruct((nb, t, KV_LORA + ROPE_DIM), F32))
    del hw
    return pl.pallas_call(
        functools.partial(_mla_prep_body, rope),
        grid=(nb, t // tt),
        in_specs=in_specs,
        out_specs=out_specs,
        out_shape=out_shape,
        compiler_params=_cparams("parallel", "parallel"),
        name="mla_prep_rope" if rope else "mla_prep",
    )(*args)


def _ctx_kv_body(lat_ref, kr_ref, wk_ref, wv_ref, k_ref, v_ref):
    latb = lat_ref[0].astype(BF16)
    ka = _dg(latb, wk_ref[...], _NN)
    va = _dg(latb, wv_ref[...], _NN)
    kr = kr_ref[0]
    for h in range(H_MLA):
        sl = slice(h * HEAD_PAD, (h + 1) * HEAD_PAD)
        k_ref[0, h] = (ka[:, sl] + kr).astype(BF16)
        v_ref[0, h] = va[:, sl].astype(BF16)


def _ctx_kv(lat, kr_placed, wk, wv):
    nb, s, _ = lat.shape
    head_spec = pl.BlockSpec((1, H_MLA, s, HEAD_PAD), lambda b: (b, 0, 0, 0))
    head_shape = jax.ShapeDtypeStruct((nb, H_MLA, s, HEAD_PAD), BF16)
    return pl.pallas_call(
        _ctx_kv_body,
        grid=(nb,),
        in_specs=[
            pl.BlockSpec((1, s, KV_LORA), lambda b: (b, 0, 0)),
            pl.BlockSpec((1, s, HEAD_PAD), lambda b: (b, 0, 0)),
            _full(wk.shape),
            _full(wv.shape),
        ],
        out_specs=[head_spec, head_spec],
        out_shape=[head_shape, head_shape],
        compiler_params=_cparams("parallel"),
        name="ctx_kv",
    )(lat, kr_placed, wk, wv)


def _attn_body(has_ctx, *refs):
    if has_ctx:
        q_ref, k_ref, v_ref, kc_ref, vc_ref, o_ref = refs
    else:
        q_ref, k_ref, v_ref, o_ref = refs
    q = q_ref[0, 0]
    s = _dg(q, k_ref[0, 0], _NT)
    m = jnp.max(s, axis=-1, keepdims=True)
    if has_ctx:
        sc = _dg(q, kc_ref[0, 0], _NT)
        m = jnp.maximum(m, jnp.max(sc, axis=-1, keepdims=True))
    p = jnp.exp2(s - m)
    l = jnp.sum(p, axis=-1, keepdims=True)
    o = _dg(p.astype(BF16), v_ref[0, 0], _NN)
    if has_ctx:
        pc = jnp.exp2(sc - m)
        l = l + jnp.sum(pc, axis=-1, keepdims=True)
        o = o + _dg(pc.astype(BF16), vc_ref[0, 0], _NN)
    o_ref[0] = o / l


def _attention(q, k, v, kc, vc, tq):
    nb, nh, t, hd = q.shape
    s = k.shape[2]
    has_ctx = kc is not None
    in_specs = [
        pl.BlockSpec((1, 1, tq, hd), lambda b, h, i: (b, h, i, 0)),
        pl.BlockSpec((1, 1, s, hd), lambda b, h, i: (b, h, 0, 0)),
        pl.BlockSpec((1, 1, s, V_DIM), lambda b, h, i: (b, h, 0, 0)),
    ]
    args = [q, k, v]
    if has_ctx:
        sc = kc.shape[2]
        in_specs += [
            pl.BlockSpec((1, 1, sc, hd), lambda b, h, i: (b, h, 0, 0)),
            pl.BlockSpec((1, 1, sc, V_DIM), lambda b, h, i: (b, h, 0, 0)),
        ]
        args += [kc, vc]
    return pl.pallas_call(
        functools.partial(_attn_body, has_ctx),
        grid=(nb, nh, t // tq),
        in_specs=in_specs,
        out_specs=pl.BlockSpec((1, tq, V_DIM), lambda b, h, i: (b, i, h)),
        out_shape=jax.ShapeDtypeStruct((nb, t, nh * V_DIM), F32),
        compiler_params=_cparams("parallel", "parallel", "parallel"),
        name="attention",
    )(*args)


def _rwkv_prep_body(p_ref, hp_ref, hn_ref, mu_ref, kk_ref, ka_ref, rk_ref, w0_ref, w2_ref, a0_ref, a2_ref, g2_ref,
                    r_out, v_out, kk_out, lw_out, a_out, kd_out, g_out, bonus_out):
    p = p_ref[0]
    tt = p.shape[0]
    row = lax.broadcasted_iota(I32, (tt, 1), 0)
    prev = jnp.where(row == 0, hp_ref[0, 0], pltpu.roll(p, 1, 0))
    nxt = jnp.where(row == tt - 1, hn_ref[0, 0], pltpu.roll(p, tt - 1, 0))
    ps = p + (0.5 * (prev + nxt) - p) * mu_ref[...]
    r = ps[:, :D_RWKV]
    k = ps[:, D_RWKV:2 * D_RWKV]
    v = ps[:, 2 * D_RWKV:3 * D_RWKV]
    wl = ps[:, 3 * D_RWKV:3 * D_RWKV + 2 * W_LORA]
    al = ps[:, 3 * D_RWKV + 2 * W_LORA:3 * D_RWKV + 2 * W_LORA + 2 * A_LORA]
    gl = ps[:, 3 * D_RWKV + 2 * W_LORA + 2 * A_LORA:]
    ones = _head_ones(D_RWKV, N_RWKV)
    kk = k * kk_ref[...]
    kk = kk / jnp.maximum(jnp.sqrt(_dot_x(kk * kk, ones)), 1e-12)
    wlin = w0_ref[...] + _dot3(jnp.tanh(wl), w2_ref[...])
    lw = -jnp.exp(-_softplus(-wlin) - 0.5)
    ag = _sigmoid(a0_ref[...] + _dot3(al, a2_ref[...]))
    bonus = jnp.zeros_like(v)
    for d in range(2):
        sl = slice(d * D_RWKV, (d + 1) * D_RWKV)
        kd = k * (1.0 + (ag[:, sl] - 1.0) * ka_ref[...])
        kd_out[0, :, sl] = kd
        bonus = bonus + _dot_x(r * kd * rk_ref[...], ones) * v
    r_out[0] = r
    v_out[0] = v
    kk_out[0] = kk
    lw_out[0] = lw
    a_out[0] = ag
    g_out[0] = _dot1(_sigmoid(gl), g2_ref[...])
    bonus_out[0] = bonus


def _halo_rows(x, tt, width):
    nb, t, c = x.shape
    nt = t // tt
    xt = x.reshape(nb, nt, tt, c)
    zero = jnp.zeros((nb, 1, width, c), x.dtype)
    prev = jnp.concatenate([zero, xt[:, :-1, tt - width:]], axis=1)
    nxt = jnp.concatenate([xt[:, 1:, :width], zero], axis=1)
    return prev, nxt


def _rwkv_prep(p_rwkv, lp, tt):
    nb, t, pw = p_rwkv.shape
    hp, hn = _halo_rows(p_rwkv, tt, 1)
    tok = lambda n: pl.BlockSpec((1, tt, n), lambda b, i: (b, i, 0))
    halo = pl.BlockSpec((1, 1, 1, pw), lambda b, i: (b, i, 0, 0))
    small = [lp['mu_rwkv'], lp['k_k'], lp['k_a'], lp['r_k'], lp['w0_cat'], lp['w2_bd'], lp['a0_cat'], lp['a2_bd'],
             lp['g2_rwkv']]
    widths = [D_RWKV, D_RWKV, D_RWKV, 2 * D_RWKV, 2 * D_RWKV, 2 * D_RWKV, D_RWKV, D_RWKV]
    return pl.pallas_call(
        _rwkv_prep_body,
        grid=(nb, t // tt),
        in_specs=[tok(pw), halo, halo] + [_full(a.shape) for a in small],
        out_specs=[tok(n) for n in widths],
        out_shape=[jax.ShapeDtypeStruct((nb, t, n), F32) for n in widths],
        compiler_params=_cparams("parallel", "parallel"),
        name="rwkv_prep",
    )(p_rwkv, hp, hn, *small)


def _rwkv_chunk(reverse, r, v, kk, lw, ag, kd, s_old):
    cl = r.shape[0]
    hw = D_RWKV

    def later(rows, cols, strict):
        t = lax.broadcasted_iota(I32, (rows, cols), 0)
        s = lax.broadcasted_iota(I32, (rows, cols), 1) % cl
        dlt = (s - t) if reverse else (t - s)
        return (dlt > 0) if strict else (dlt >= 0)

    cum = _x_dot(later(cl, cl, False).astype(F32), lw)
    yield
    tot = jnp.sum(lw, axis=0, keepdims=True)
    e_neg = jnp.exp(-cum)
    e_rem = jnp.exp(tot - cum)
    at = -kk * jnp.exp(cum - lw)
    rt = r * jnp.exp(cum)
    b0 = kk * ag
    bt = b0 * e_neg
    kt = kd * e_neg
    bh = b0 * e_rem
    kh = kd * e_rem
    w_c = jnp.exp(tot)

    lane_head = lax.broadcasted_iota(I32, (1, hw), 1) // N_RWKV
    hms = [(lane_head == h).astype(F32) for h in range(H_RWKV)]
    heads = range(H_RWKV)
    bk = [jnp.concatenate([bt * hm, kt * hm], axis=0) for hm in hms]
    strict = later(cl, 2 * cl, True)
    incl = later(cl, 2 * cl, False)
    ls = [jnp.where(strict, _dot3(at, bk[h], _NT), 0.0) for h in heads]
    ly = [jnp.where(incl, _dot1(rt, bk[h], _NT), 0.0) for h in heads]
    yield

    x = [l[:, :cl] for l in ls]
    eye = (lax.broadcasted_iota(I32, (cl, cl), 0) == lax.broadcasted_iota(I32, (cl, cl), 1)).astype(F32)
    inv = [eye + xh for xh in x]
    span = 2
    while span < cl:
        x = [_dot3(xh, xh) for xh in x]
        yield
        inv = [ih + _dot3(ih, xh) for ih, xh in zip(inv, x)]
        yield
        span *= 2

    zero = jnp.zeros((cl, hw), F32)
    v_rows = jnp.concatenate([piece for hm in hms for piece in (zero, v * hm)], axis=0)
    rhs = _dot3(at, s_old, _NT) + _dot3(jnp.concatenate(ls, axis=1), v_rows)
    yield
    us = [_dot3(inv[h], rhs * hms[h]) for h in heads]
    yield
    um = us[0]
    for h in range(1, H_RWKV):
        um = um + us[h]
    uv_rows = jnp.concatenate([piece for h in heads for piece in (us[h], v * hms[h])], axis=0)
    y = _dot1(rt, s_old, _NT) + _dot1(jnp.concatenate(ly, axis=1), uv_rows)
    yield
    uv_t = jnp.concatenate([um, v], axis=0).T
    s_new = s_old * w_c + _dot3(uv_t, jnp.concatenate([bh, kh], axis=0)) * _head_ones(hw, N_RWKV)
    yield y, s_new


def _lockstep(gens):
    last = [None] * len(gens)
    live = list(range(len(gens)))
    while live:
        for i in list(live):
            try:
                last[i] = next(gens[i])
            except StopIteration:
                live.remove(i)
    return last


def _rwkv_scan_body(rf_ref, vf_ref, kkf_ref, lwf_ref, af_ref, kdf_ref, rb_ref, vb_ref, kkb_ref, lwb_ref, ab_ref,
                    kdb_ref, s0_ref, yf_ref, yb_ref, sf_ref, s_scr):
    c = pl.program_id(1)
    nc = pl.num_programs(1)

    @pl.when(c == 0)
    def _():
        s_scr[...] = s0_ref[...]

    nrow = rf_ref.shape[0]
    chains = []
    for i in range(nrow):
        chains.append(_rwkv_chunk(False, rf_ref[i], vf_ref[i], kkf_ref[i], lwf_ref[i], af_ref[i], kdf_ref[i],
                                  s_scr[i, 0]))
        chains.append(_rwkv_chunk(True, rb_ref[i], vb_ref[i], kkb_ref[i], lwb_ref[i], ab_ref[i], kdb_ref[i],
                                  s_scr[i, 1]))
    done = _lockstep(chains)
    for i in range(nrow):
        (y_f, s_f), (y_b, s_b) = done[2 * i], done[2 * i + 1]
        yf_ref[i] = y_f
        yb_ref[i] = y_b
        s_scr[i, 0] = s_f
        s_scr[i, 1] = s_b

    @pl.when(c == nc - 1)
    def _():
        sf_ref[...] = s_scr[...]


def _rwkv_scan(r, v, kk, lw, ag, kd, s0):
    nb, t, hw = r.shape
    cl = RWKV_CHUNK
    nr = RWKV_ROWS
    nc = t // cl
    fwd = lambda lane: pl.BlockSpec((nr, cl, hw), lambda b, c: (b, c, lane))
    bwd = lambda lane: pl.BlockSpec((nr, cl, hw), lambda b, c: (b, nc - 1 - c, lane))
    state = pl.BlockSpec((nr, 2, hw, hw), lambda b, c: (b, 0, 0, 0))
    y_shape = jax.ShapeDtypeStruct((nb, t, hw), F32)
    return pl.pallas_call(
        _rwkv_scan_body,
        grid=(nb // nr, nc),
        in_specs=[fwd(0)] * 6 + [bwd(0)] * 3 + [bwd(1)] * 3 + [state],
        out_specs=[fwd(0), bwd(0), state],
        out_shape=[y_shape, y_shape, jax.ShapeDtypeStruct((nb, 2, hw, hw), F32)],
        scratch_shapes=[pltpu.VMEM((nr, 2, hw, hw), F32)],
        compiler_params=_cparams("parallel", "arbitrary"),
        name="rwkv_scan",
    )(r, v, kk, lw, ag, kd, r, v, kk, lw, ag, kd, s0)


def _lru_body(reverse, p_ref, halo_ref, cw_ref, cb_ref, wa_ref, ba_ref, wx_ref, bx_ref, lam_ref, h0_ref,
              h_out, fin_out, a_scr, u_scr, carry_scr):
    i = pl.program_id(1)
    nt = pl.num_programs(1)
    tt = p_ref.shape[1]
    n_ext = tt + SUBLANES

    @pl.when(i == 0)
    def _():
        carry_scr[...] = jnp.broadcast_to(h0_ref[0], (SUBLANES, D_LRU))

    xb = p_ref[0, :, :D_LRU]
    halo = halo_ref[0, 0]
    xc = cb_ref[...] + jnp.zeros_like(xb)
    if reverse:
        xe = jnp.concatenate([xb, halo], axis=0)
        for k in range(CONV_W):
            sh = xb if k == 0 else pltpu.roll(xe, n_ext - k, 0)[:tt]
            xc = xc + cw_ref[k:k + 1, :] * sh
    else:
        xe = jnp.concatenate([halo, xb], axis=0)
        for k in range(CONV_W):
            s = CONV_W - 1 - k
            sh = xb if s == 0 else pltpu.roll(xe, s, 0)[SUBLANES:]
            xc = xc + cw_ref[k:k + 1, :] * sh
    gate_r = _sigmoid(_dot3(xc, wa_ref[...]) + ba_ref[...])
    gate_i = _sigmoid(_dot3(xc, wx_ref[...]) + bx_ref[...])
    log_a = -LRU_C * gate_r * _softplus(-lam_ref[...])
    a = jnp.exp(log_a)
    u = jnp.sqrt(-_expm1(2.0 * log_a)) * (gate_i * xc)

    rowm = lax.broadcasted_iota(I32, (tt, 1), 0) % SUBLANES
    for s in (1, 2, 4):
        if reverse:
            a_s = pltpu.roll(a, tt - s, 0)
            u_s = pltpu.roll(u, tt - s, 0)
            m = rowm < SUBLANES - s
        else:
            a_s = pltpu.roll(a, s, 0)
            u_s = pltpu.roll(u, s, 0)
            m = rowm >= s
        u = jnp.where(m, a * u_s + u, u)
        a = jnp.where(m, a * a_s, a)
    a_scr[...] = a
    u_scr[...] = u
    ng = tt // SUBLANES

    def group(g, carry):
        gi = (ng - 1 - g) if reverse else g
        r0 = pl.multiple_of(gi * SUBLANES, SUBLANES)
        hg = a_scr[pl.ds(r0, SUBLANES), :] * carry + u_scr[pl.ds(r0, SUBLANES), :]
        h_out[0, pl.ds(r0, SUBLANES), :] = hg
        edge = hg[0:1, :] if reverse else hg[SUBLANES - 1:SUBLANES, :]
        return jnp.broadcast_to(edge, (SUBLANES, D_LRU))

    carry = lax.fori_loop(0, ng, group, carry_scr[...])
    carry_scr[...] = carry

    @pl.when(i == nt - 1)
    def _():
        fin_out[0] = carry[0:1, :]


def _lru_dir(p_lru, halo, lp, d, h0, tt):
    nb, t, pw = p_lru.shape
    nt = t // tt
    reverse = d == 1
    tidx = (lambda i: nt - 1 - i) if reverse else (lambda i: i)
    small = [lp['conv_w'][d], lp['conv_b'][d:d + 1], lp['wa_bd'][d], lp['b_lru_a'][d:d + 1], lp['wx_bd'][d],
             lp['b_lru_x'][d:d + 1], lp['lam'][d:d + 1]]
    return pl.pallas_call(
        functools.partial(_lru_body, reverse),
        grid=(nb, nt),
        in_specs=[
            pl.BlockSpec((1, tt, pw), lambda b, i: (b, tidx(i), 0)),
            pl.BlockSpec((1, 1, SUBLANES, D_LRU), lambda b, i: (b, tidx(i), 0, 0)),
        ] + [_full(a.shape) for a in small] + [pl.BlockSpec((1, 1, D_LRU), lambda b, i: (b, 0, 0))],
        out_specs=[
            pl.BlockSpec((1, tt, D_LRU), lambda b, i: (b, tidx(i), 0)),
            pl.BlockSpec((1, 1, D_LRU), lambda b, i: (b, 0, 0)),
        ],
        out_shape=[jax.ShapeDtypeStruct((nb, t, D_LRU), F32), jax.ShapeDtypeStruct((nb, 1, D_LRU), F32)],
        scratch_shapes=[pltpu.VMEM((tt, D_LRU), F32), pltpu.VMEM((tt, D_LRU), F32), pltpu.VMEM((SUBLANES, D_LRU), F32)],
        compiler_params=_cparams("parallel", "arbitrary"),
        name="lru_bwd" if reverse else "lru_fwd",
    )(p_lru, halo, *small, h0)


def _gelu_tanh(x):
    return 0.5 * x * (1.0 + jnp.tanh(np.sqrt(2.0 / np.pi).astype(np.float32) * (x + 0.044715 * (x * x * x))))


def _out_proj_body(x_ref, o_ref, y0_ref, y1_ref, bonus_ref, g_ref, h0_ref, h1_ref, p_ref,
                   gmla_ref, lnw_ref, lnb_ref, glru_ref, wo_ref, gpost_ref, gate_ref, gpre_ref, sc_ref, sh_ref,
                   *rest):
    x1_out, h2_out = rest[-2:]
    o_mla = _rms(o_ref[0], gmla_ref[...])
    ones = _head_ones(D_RWKV, N_RWKV)
    y = y0_ref[0] + y1_ref[0]
    mu = _dot_x(y, ones) * (1.0 / N_RWKV)
    yc = y - mu
    var = _dot_x(yc * yc, ones) * (1.0 / N_RWKV)
    y = yc * lax.rsqrt(var + GN_EPS) * lnw_ref[...] + lnb_ref[...]
    o_rwkv = (y + bonus_ref[0]) * g_ref[0]
    o_lru = _rms((h0_ref[0] + h1_ref[0]) * _gelu_tanh(p_ref[0, :, D_LRU:]), glru_ref[...])
    m = (_dg(o_mla.astype(BF16), wo_ref[:D_MLA_OUT, :], _NN)
         + _dg(o_rwkv.astype(BF16), wo_ref[D_MLA_OUT:D_MLA_OUT + D_RWKV, :], _NN)
         + _dg(o_lru.astype(BF16), wo_ref[D_MLA_OUT + D_RWKV:, :], _NN))
    x1 = x_ref[...] + gate_ref[0] * _rms(m, gpost_ref[...])
    x1_out[...] = x1
    h2_out[...] = _rms(x1, gpre_ref[...]) * (1.0 + sc_ref[0]) + sh_ref[0]


def _out_proj(x, grp, o, y_f, y_b, bonus, g, h0, h1, p_lru, lp, mod, tt, prev):
    _, nb, t = grp
    per_batch = mod['g1'].shape[0] > 1
    tok = lambda n: pl.BlockSpec((1, tt, n), lambda b, i: (b, i, 0))
    rows = lambda: _group_rows(grp, tt)
    small = [lp['g_mla_out'], lp['ln_x_w'], lp['ln_x_b'], lp['g_lru_out'], lp['w_out'], lp['g_post_mix']]
    args = [x, o, y_f, y_b, bonus, g, h0, h1, p_lru, *small, mod['g1'], lp['g_pre_ffn'], mod['sc2'], mod['sh2']]
    in_specs = ([rows(), tok(D_MLA_OUT), tok(D_RWKV), tok(D_RWKV), tok(D_RWKV), tok(D_RWKV), tok(D_LRU), tok(D_LRU),
                 tok(P_LRU)] + [_full(a.shape) for a in small]
                + [_mod_spec(per_batch), _full(lp['g_pre_ffn'].shape), _mod_spec(per_batch), _mod_spec(per_batch)])
    aliases = {}
    if prev is not None:
        aliases = {len(args): 0, len(args) + 1: 1}
        args += list(prev)
        in_specs += [pl.BlockSpec(memory_space=pl.ANY)] * 2
    return pl.pallas_call(
        _out_proj_body,
        grid=(nb, t // tt),
        in_specs=in_specs,
        out_specs=[rows(), rows()],
        out_shape=[jax.ShapeDtypeStruct(x.shape, F32)] * 2,
        input_output_aliases=aliases,
        compiler_params=_cparams("parallel", "parallel"),
        name="out_proj",
    )(*args)


def _first_index(mask, idx, sentinel):
    return jnp.min(jnp.where(mask, idx, sentinel), axis=0, keepdims=True)


def _router_body(h_ref, wr_ref, br_ref, e_out, pos_out, w_out, cnt_out, carry):
    i = pl.program_id(0)
    tn = h_ref.shape[0]

    @pl.when(i == 0)
    def _():
        carry[...] = jnp.zeros_like(carry)

    logits = lax.dot_general(wr_ref[...], h_ref[...], _NT, precision=lax.Precision.HIGHEST,
                             preferred_element_type=F32)
    scores = _sigmoid(logits)
    sel = scores + br_ref[...]

    i8 = lax.broadcasted_iota(I32, (GROUP_SIZE, tn), 0)
    blocks, gscore = [], []
    for g in range(N_GROUPS):
        blk = sel[g * GROUP_SIZE:(g + 1) * GROUP_SIZE, :]
        m1 = jnp.max(blk, axis=0, keepdims=True)
        f1 = _first_index(blk == m1, i8, GROUP_SIZE)
        m2 = jnp.max(jnp.where(i8 == f1, -jnp.inf, blk), axis=0, keepdims=True)
        blocks.append(blk)
        gscore.append(m1 + m2)
    masked = []
    for g in range(N_GROUPS):
        beaten = jnp.zeros((1, tn), I32)
        for o in range(N_GROUPS):
            if o == g:
                continue
            wins = (gscore[o] >= gscore[g]) if o < g else (gscore[o] > gscore[g])
            beaten = beaten + wins.astype(I32)
        masked.append(jnp.where(beaten < TOPK_GROUPS, blocks[g], -jnp.inf))
    cur = jnp.concatenate(masked, axis=0)

    ie = lax.broadcasted_iota(I32, (N_EXPERTS, tn), 0)
    firsts, raw = [], []
    chosen = jnp.zeros((N_EXPERTS, tn), F32)
    for _ in range(TOP_K):
        mx = jnp.max(cur, axis=0, keepdims=True)
        first = _first_index(cur == mx, ie, N_EXPERTS)
        hit = ie == first
        firsts.append(first)
        raw.append(jnp.sum(jnp.where(hit, scores, 0.0), axis=0, keepdims=True))
        cur = jnp.where(hit, -jnp.inf, cur)
        chosen = chosen + hit.astype(F32)
    total = raw[0]
    for k in range(1, TOP_K):
        total = total + raw[k]

    before = (lax.broadcasted_iota(I32, (tn, tn), 0) < lax.broadcasted_iota(I32, (tn, tn), 1)).astype(BF16)
    rank = _dg(chosen.astype(BF16), before, _NN) + carry[:, 0:1]
    e_out[...] = jnp.zeros_like(e_out)
    pos_out[...] = jnp.zeros_like(pos_out)
    w_out[...] = jnp.zeros_like(w_out)
    for k in range(TOP_K):
        e_out[0, k:k + 1, :] = firsts[k]
        pos_out[0, k:k + 1, :] = jnp.sum(jnp.where(ie == firsts[k], rank, 0.0), axis=0, keepdims=True).astype(I32)
        w_out[k:k + 1, :] = raw[k] / total * ROUTED_SCALE
    new_carry = carry[...] + jnp.sum(chosen, axis=1, keepdims=True)
    carry[...] = new_carry
    cnt_out[...] = new_carry


def _router(h2, wr_t, b_col):
    n, d = h2.shape
    tn = ROUTER_TILE
    row_spec = pl.BlockSpec((SLOT_ROWS, tn), lambda i: (0, i))
    tile_spec = pl.BlockSpec((1, SLOT_ROWS, tn), lambda i: (i, 0, 0))
    cnt_spec = pl.BlockSpec((N_EXPERTS, 128), lambda i: (0, 0))
    tile_shape = jax.ShapeDtypeStruct((n // tn, SLOT_ROWS, tn), I32)
    return pl.pallas_call(
        _router_body,
        grid=(n // tn,),
        in_specs=[pl.BlockSpec((tn, d), lambda i: (i, 0)), _full(wr_t.shape), _full(b_col.shape)],
        out_specs=[tile_spec, tile_spec, row_spec, cnt_spec],
        out_shape=[tile_shape, tile_shape,
                   jax.ShapeDtypeStruct((SLOT_ROWS, n), F32), jax.ShapeDtypeStruct((N_EXPERTS, 128), F32)],
        scratch_shapes=[pltpu.VMEM((N_EXPERTS, 128), F32)],
        compiler_params=_cparams("arbitrary"),
        name="router",
    )(h2, wr_t, b_col)


ROW_LANES = 128
ROW_SUB = D_MODEL // 2 // ROW_LANES
U32 = jnp.uint32
HIGH_HALF = np.uint32(0xFFFF0000)


def _bf16_bits(x):
    return lax.bitcast_convert_type(x.astype(BF16).astype(F32), U32)


def _to_row_tiles(ref, x):
    rows, d = x.shape
    words = (_bf16_bits(x[:, :d // 2]) >> 16) | (_bf16_bits(x[:, d // 2:]) & HIGH_HALF)
    for s in range(ROW_SUB):
        ref[pl.ds(s, rows, stride=ROW_SUB), :] = words[:, s * ROW_LANES:(s + 1) * ROW_LANES]


def _from_row_tiles(ref, rows):
    words = jnp.concatenate([ref[pl.ds(s, rows, stride=ROW_SUB), :] for s in range(ROW_SUB)], axis=1)
    low = lax.bitcast_convert_type(words << 16, F32)
    high = lax.bitcast_convert_type(words & HIGH_HALF, F32)
    return jnp.concatenate([low, high], axis=1)


def _row_copy(src, src_off, dst, dst_off, sem):
    return pltpu.make_async_copy(src.at[pl.ds(pl.multiple_of(src_off, ROW_SUB), ROW_SUB), :],
                                 dst.at[pl.ds(pl.multiple_of(dst_off, ROW_SUB), ROW_SUB), :], sem)


def _load_slots(slot_ref, slot_smem, sem):
    cp = pltpu.make_async_copy(slot_ref.at[0], slot_smem, sem)
    cp.start()
    cp.wait()


def _dispatch_body(start_ref, e_ref, pos_ref, h_ref, xs_in, slots_out, xs_out, slot_smem, h_tiles, sem_s, sem):
    del xs_in
    i = pl.program_id(0)
    last = pl.num_programs(0) - 1
    tn = slot_smem.shape[2]
    cur = i % 2
    e = e_ref[0]
    slot = pos_ref[0]
    for x in range(N_EXPERTS):
        slot = slot + jnp.where(e == x, start_ref[x], 0)
    slots_out[0] = slot * ROW_SUB
    _load_slots(slots_out, slot_smem.at[cur], sem_s)
    _to_row_tiles(h_tiles.at[cur], h_ref[...])

    def issue(j, carry):
        for k in range(TOP_K):
            _row_copy(h_tiles.at[cur], j * ROW_SUB, xs_out, slot_smem[cur, k, j], sem.at[cur]).start(priority=k % 2)
        return carry

    def drain(which):
        def one(j, carry):
            for k in range(TOP_K):
                _row_copy(h_tiles.at[0], 0, xs_out, 0, sem.at[which]).wait()
            return carry
        lax.fori_loop(0, tn, one, 0)

    lax.fori_loop(0, tn, issue, 0)

    @pl.when(i > 0)
    def _():
        drain(1 - cur)

    @pl.when(i == last)
    def _():
        drain(cur)


def _zero_tail_body(lb_ref, o_ref):
    del lb_ref
    o_ref[...] = jnp.zeros_like(o_ref)


def _zero_tails(last_block, n_blocks):
    rows = MOE_BLOCK * ROW_SUB
    grid_spec = pltpu.PrefetchScalarGridSpec(
        num_scalar_prefetch=1,
        grid=(last_block.shape[0],),
        in_specs=[],
        out_specs=pl.BlockSpec((rows, ROW_LANES), lambda e, lb: (lb[e], 0)),
    )
    return pl.pallas_call(
        _zero_tail_body,
        grid_spec=grid_spec,
        out_shape=jax.ShapeDtypeStruct((n_blocks * rows, ROW_LANES), U32),
        compiler_params=_cparams("arbitrary"),
        name="zero_tails",
    )(last_block)


def _dispatch(row_start, e3, pos3, h2, xs_zero, tn):
    n, d = h2.shape
    ratio = e3.shape[2] // tn
    tile = lambda: pl.BlockSpec((1, SLOT_ROWS, tn), lambda i, st: (i // ratio, 0, i % ratio))
    grid_spec = pltpu.PrefetchScalarGridSpec(
        num_scalar_prefetch=1,
        grid=(n // tn,),
        in_specs=[tile(), tile(), pl.BlockSpec((tn, d), lambda i, st: (i, 0)), pl.BlockSpec(memory_space=pl.ANY)],
        out_specs=[pl.BlockSpec((1, SLOT_ROWS, tn), lambda i, st: (i, 0, 0)), pl.BlockSpec(memory_space=pl.ANY)],
        scratch_shapes=[pltpu.SMEM((2, SLOT_ROWS, tn), I32), pltpu.VMEM((2, tn * ROW_SUB, ROW_LANES), U32),
                        pltpu.SemaphoreType.DMA, pltpu.SemaphoreType.DMA((2,))],
    )
    return pl.pallas_call(
        _dispatch_body,
        grid_spec=grid_spec,
        out_shape=[jax.ShapeDtypeStruct((n // tn, SLOT_ROWS, tn), I32),
                   jax.ShapeDtypeStruct(xs_zero.shape, xs_zero.dtype)],
        input_output_aliases={4: 1},
        compiler_params=_cparams("arbitrary"),
        name="dispatch",
    )(row_start, e3, pos3, h2, xs_zero)


def _swiglu(x, wg, wu, wd):
    g = _dg(x, wg, _NN)
    u = _dg(x, wu, _NN)
    return _dg((g * _sigmoid(g) * u).astype(BF16), wd, _NN)


def _expert_body(be_ref, nu_ref, xs_ref, wg_ref, wu_ref, wd_ref, ys_ref):
    del be_ref
    b = pl.program_id(0)

    @pl.when(b < nu_ref[0])
    def _():
        x = _from_row_tiles(xs_ref, MOE_BLOCK).astype(BF16)
        _to_row_tiles(ys_ref, _swiglu(x, wg_ref[0], wu_ref[0], wd_ref[0]))

    @pl.when(b >= nu_ref[0])
    def _():
        ys_ref[...] = jnp.zeros_like(ys_ref)


def _experts(block_e, n_used, xs, wg, wu, wd):
    bm = MOE_BLOCK
    _, d, de = wg.shape
    rows = bm * ROW_SUB
    grid_spec = pltpu.PrefetchScalarGridSpec(
        num_scalar_prefetch=2,
        grid=(xs.shape[0] // rows,),
        in_specs=[
            pl.BlockSpec((rows, ROW_LANES), lambda b, be, nu: (jnp.minimum(b, nu[0] - 1), 0)),
            pl.BlockSpec((1, d, de), lambda b, be, nu: (be[b], 0, 0)),
            pl.BlockSpec((1, d, de), lambda b, be, nu: (be[b], 0, 0)),
            pl.BlockSpec((1, de, d), lambda b, be, nu: (be[b], 0, 0)),
        ],
        out_specs=pl.BlockSpec((rows, ROW_LANES), lambda b, be, nu: (b, 0)),
    )
    return pl.pallas_call(
        _expert_body,
        grid_spec=grid_spec,
        out_shape=jax.ShapeDtypeStruct(xs.shape, xs.dtype),
        compiler_params=_cparams("arbitrary"),
        name="experts",
    )(block_e, n_used, xs, wg, wu, wd)


def _combine_body(slot0_ref, slotn_ref, w_ref, x1_ref, h2_ref, gate_ref, ys_hbm, wsg_ref, wsu_ref, wsd_ref,
                  gpost_ref, out_ref, slot_smem, buf, sem_s, sem):
    i = pl.program_id(0)
    last = pl.num_programs(0) - 1
    tn = slot_smem.shape[2]
    cur = i % 2

    def fetch(slot_ref, which):
        _load_slots(slot_ref, slot_smem.at[which], sem_s)

        def issue(j, carry):
            for k in range(TOP_K):
                _row_copy(ys_hbm, slot_smem[which, k, j], buf.at[which, k], j * ROW_SUB,
                          sem.at[which]).start(priority=k % 2)
            return carry

        lax.fori_loop(0, tn, issue, 0)

    def drain(j, carry):
        for k in range(TOP_K):
            _row_copy(ys_hbm, 0, buf.at[0, k], 0, sem.at[cur]).wait()
        return carry

    @pl.when(i == 0)
    def _():
        fetch(slot0_ref, 0)

    @pl.when(i < last)
    def _():
        fetch(slotn_ref, 1 - cur)

    acc = _swiglu(h2_ref[...].astype(BF16), wsg_ref[...], wsu_ref[...], wsd_ref[...])
    lax.fori_loop(0, tn, drain, 0)
    for k in range(TOP_K):
        acc = acc + _from_row_tiles(buf.at[cur, k], tn) * w_ref[:, k:k + 1]
    out_ref[...] = x1_ref[...] + gate_ref[0] * _rms(acc, gpost_ref[...])


def _combine(slots3, w_tok, x1, h2, gate_rows, ys, wsg, wsu, wsd, gpost):
    ntiles, _, tn = slots3.shape
    n, d = x1.shape
    tok = pl.BlockSpec((tn, d), lambda i: (i, 0))
    return pl.pallas_call(
        _combine_body,
        grid=(ntiles,),
        in_specs=[pl.BlockSpec((1, SLOT_ROWS, tn), lambda i: (0, 0, 0)),
                  pl.BlockSpec((1, SLOT_ROWS, tn), lambda i: (jnp.minimum(i + 1, ntiles - 1), 0, 0)),
                  pl.BlockSpec((tn, SLOT_ROWS), lambda i: (i, 0)), tok, tok,
                  pl.BlockSpec((1, 1, d), lambda i: (i, 0, 0)),
                  pl.BlockSpec(memory_space=pl.ANY),
                  _full(wsg.shape), _full(wsu.shape), _full(wsd.shape), _full(gpost.shape)],
        out_specs=tok,
        out_shape=jax.ShapeDtypeStruct((n, d), F32),
        scratch_shapes=[pltpu.SMEM((2, SLOT_ROWS, tn), I32),
                        pltpu.VMEM((2, TOP_K, tn * ROW_SUB, ROW_LANES), U32),
                        pltpu.SemaphoreType.DMA, pltpu.SemaphoreType.DMA((2,))],
        compiler_params=_cparams("arbitrary"),
        name="combine",
    )(slots3, slots3, w_tok, x1, h2, gate_rows, ys, wsg, wsu, wsd, gpost)


def _moe(x1, h2, gate_rows, lp):
    n, d = h2.shape
    bm = MOE_BLOCK
    tn = TOKEN_TILE
    e3, pos3, w_rows, cnt = _router(h2, lp['wr_t'], lp['b_router_col'])
    counts = cnt[:, 0].astype(I32)
    blocks_per_e = (counts + bm - 1) // bm
    blk_end = jnp.cumsum(blocks_per_e)
    blk_start = blk_end - blocks_per_e
    n_blocks = -(-(n * TOP_K) // bm) + N_EXPERTS
    block_e = jnp.minimum(jnp.sum(blk_end[None, :] <= jnp.arange(n_blocks, dtype=I32)[:, None], axis=1),
                          N_EXPERTS - 1).astype(I32)
    n_used = blk_end[-1:].astype(I32)
    xs_tails = _zero_tails(jnp.clip(blk_end - 1, 0, n_blocks - 1).astype(I32), n_blocks)
    slots3, xs = _dispatch((blk_start * bm).astype(I32), e3, pos3, h2, xs_tails, tn)
    ys = _experts(block_e, n_used, xs, lp['w_e_gate'], lp['w_e_up'], lp['w_e_down'])
    return _combine(slots3, w_rows.T, x1, h2, gate_rows, ys, lp['w_s_gate'], lp['w_s_up'], lp['w_s_down'],
                    lp['g_post_ffn'])


def _rope_tables(n_tokens):
    rows = n_tokens // GRID_W
    row = jnp.repeat(jnp.arange(rows, dtype=F32), GRID_W)
    col = jnp.tile(jnp.arange(GRID_W, dtype=F32), rows)
    n_freq = ROPE_DIM // 4
    inv_freq = ROPE_THETA ** (-jnp.arange(n_freq, dtype=F32) / n_freq)
    ang_r = row[:, None] * inv_freq
    ang_c = col[:, None] * inv_freq
    ang = jnp.concatenate([ang_r, ang_r, ang_c, ang_c], axis=-1)
    cos, sin = jnp.cos(ang), jnp.sin(ang)
    pad = lambda z, fill: jnp.concatenate(
        [jnp.full((n_tokens, NOPE_DIM), fill, F32), z, jnp.zeros((n_tokens, HEAD_PAD - QK_DIM), F32)], axis=1)
    return pad(cos, 1.0), pad(sin, 0.0), pad(cos, 0.0), pad(sin, 0.0)


def _rot_cols(w):
    q = ROPE_DIM // 4
    perm = np.concatenate([np.arange(q, 2 * q), np.arange(0, q), np.arange(3 * q, 4 * q), np.arange(2 * q, 3 * q)])
    sign = np.concatenate([-np.ones(q), np.ones(q), -np.ones(q), np.ones(q)]).astype(np.float32)
    return w[..., perm] * sign


def _block_diag(blocks):
    h, m, n = blocks.shape[-3:]
    eye = jnp.eye(h, dtype=blocks.dtype)
    out = blocks[..., :, :, None, :] * eye[:, None, :, None]
    return out.reshape(blocks.shape[:-3] + (h * m, h * n))


def _diag_blocks(mat, h):
    m, n = mat.shape[-2] // h, mat.shape[-1] // h
    z = mat.reshape(mat.shape[:-2] + (h, m, h, n))
    return jnp.stack([z[..., i, :, i, :] for i in range(h)], axis=-3)


def _layer_operands(l, a):
    w_in = a['w_in'][l]
    d = w_in.shape[0]
    p_mla = Q_LORA + KV_LORA + ROPE_DIM
    w_kr = w_in[:, Q_LORA + KV_LORA:p_mla]
    z32 = jnp.zeros((d, ROPE_DIM), F32)
    w_mla = jnp.concatenate([w_in[:, :Q_LORA + KV_LORA], z32, _rot_cols(w_kr), w_kr, z32], axis=1)
    wq3 = a['w_q_b'][l].reshape(Q_LORA, H_MLA, QK_DIM)
    zq = lambda n: jnp.zeros((Q_LORA, H_MLA, n), F32)
    wq = jnp.concatenate([wq3, zq(HEAD_PAD - QK_DIM)], axis=2).reshape(Q_LORA, H_MLA * HEAD_PAD)
    wqr = jnp.concatenate([zq(NOPE_DIM), _rot_cols(wq3[:, :, NOPE_DIM:]), zq(HEAD_PAD - QK_DIM)],
                          axis=2).reshape(Q_LORA, H_MLA * HEAD_PAD)
    wkv3 = a['w_kv_b'][l].reshape(KV_LORA, H_MLA, NOPE_DIM + V_DIM)
    wk = jnp.concatenate([wkv3[:, :, :NOPE_DIM], jnp.zeros((KV_LORA, H_MLA, HEAD_PAD - NOPE_DIM), F32)],
                         axis=2).reshape(KV_LORA, H_MLA * HEAD_PAD)
    wv = wkv3[:, :, NOPE_DIM:].reshape(KV_LORA, H_MLA * V_DIM)
    row = lambda z: z.reshape(1, -1)
    two = lambda z: jnp.concatenate([z[0], z[1]], axis=-1)
    bd2 = lambda z: jnp.concatenate([jnp.concatenate([z[0], jnp.zeros_like(z[0])], axis=1),
                                     jnp.concatenate([jnp.zeros_like(z[1]), z[1]], axis=1)], axis=0)
    return {
        'g_pre_mix': row(a['g_pre_mix'][l]), 'g_post_mix': row(a['g_post_mix'][l]),
        'g_pre_ffn': row(a['g_pre_ffn'][l]), 'g_post_ffn': row(a['g_post_ffn'][l]),
        'w_mla': w_mla.astype(BF16),
        'w_rwkv': w_in[:, p_mla:p_mla + P_RWKV].astype(BF16),
        'w_lru': w_in[:, p_mla + P_RWKV:].astype(BF16),
        'g_q_a': row(a['g_q_a'][l]), 'g_kv_a': row(a['g_kv_a'][l]),
        'wq': wq.astype(BF16), 'wqr': wqr.astype(BF16), 'wk': wk.astype(BF16), 'wv': wv.astype(BF16),
        'g_mla_out': row(a['g_mla_out'][l]),
        'mu_rwkv': row(a['mu_rwkv'][l]), 'k_k': row(a['k_k'][l]), 'k_a': row(a['k_a'][l]), 'r_k': row(a['r_k'][l]),
        'w0_cat': row(two(a['w0_rwkv'][l])), 'w2_bd': bd2(a['w2_rwkv'][l]),
        'a0_cat': row(two(a['a0_rwkv'][l])), 'a2_bd': bd2(a['a2_rwkv'][l]),
        'g2_rwkv': a['g2_rwkv'][l], 'ln_x_w': row(a['ln_x_w'][l]), 'ln_x_b': row(a['ln_x_b'][l]),
        'conv_w': a['conv_w'][l], 'conv_b': a['conv_b'][l],
        'wa_bd': _block_diag(a['w_lru_a'][l]), 'b_lru_a': a['b_lru_a'][l],
        'wx_bd': _block_diag(a['w_lru_x'][l]), 'b_lru_x': a['b_lru_x'][l],
        'lam': a['lam'][l], 'g_lru_out': row(a['g_lru_out'][l]),
        'w_out': a['w_out'][l].astype(BF16),
        'wr_t': a['w_router'][l].T, 'b_router_col': a['b_router'][l].reshape(N_EXPERTS, 1),
        'w_e_gate': a['w_e_gate'][l].astype(BF16), 'w_e_up': a['w_e_up'][l].astype(BF16),
        'w_e_down': a['w_e_down'][l].astype(BF16),
        'w_s_gate': a['w_s_gate'][l].astype(BF16), 'w_s_up': a['w_s_up'][l].astype(BF16),
        'w_s_down': a['w_s_down'][l].astype(BF16),
    }


def _mixer(x, grp, mod, lp, ctx, tables, prev):
    _, nb, t = grp
    tt = min(t, 512)
    p_mla, p_rwkv, p_lru = _in_proj(x, grp, lp['g_pre_mix'], mod['sc1'], mod['sh1'], lp['w_mla'], lp['w_rwkv'],
                                    lp['w_lru'], tt)
    if ctx is None:
        q, k, v, new_cache = _mla_prep(p_mla, lp['g_q_a'], lp['g_kv_a'], lp['wq'], None, lp['wk'], lp['wv'], None, tt)
        kc = vc = None
        s0 = jnp.zeros((nb, 2, D_RWKV, D_RWKV), F32)
        h0 = jnp.zeros((nb, 2, 1, D_LRU), F32)
    else:
        cache, s_rwkv, s_lru = ctx
        q, k, v = _mla_prep(p_mla, lp['g_q_a'], lp['g_kv_a'], lp['wq'], lp['wqr'], lp['wk'], lp['wv'], tables, tt)
        new_cache = None
        kr_placed = jnp.pad(cache[..., KV_LORA:], ((0, 0), (0, 0), (NOPE_DIM, HEAD_PAD - QK_DIM)))
        kc, vc = _ctx_kv(cache[..., :KV_LORA], kr_placed, lp['wk'], lp['wv'])
        s0 = _block_diag(s_rwkv)
        h0 = s_lru[:, :, None, :]
    o = _attention(q, k, v, kc, vc, min(t, 256))

    r, vv, kk, lw, ag, kd, g, bonus = _rwkv_prep(p_rwkv, lp, tt)
    y_f, y_b, s_fin = _rwkv_scan(r, vv, kk, lw, ag, kd, s0)

    hp, hn = _halo_rows(p_lru[..., :D_LRU], tt, SUBLANES)
    h_f, fin_f = _lru_dir(p_lru, hp, lp, 0, h0[:, 0], tt)
    h_b, fin_b = _lru_dir(p_lru, hn, lp, 1, h0[:, 1], tt)

    x1, h2 = _out_proj(x, grp, o, y_f, y_b, bonus, g, h_f, h_b, p_lru, lp, mod, tt, prev)
    states = None
    if ctx is None:
        states = (new_cache, _diag_blocks(s_fin, H_RWKV), jnp.concatenate([fin_f, fin_b], axis=1))
    return x1, h2, states


def kernel(x_prompt, x_sample, c, cache_mla, state_rwkv, state_lru, c_ctx, w_ada, b_ada, g_pre_mix, g_post_mix, g_pre_ffn, g_post_ffn, w_in, g_q_a, w_q_b, g_kv_a, w_kv_b, g_mla_out, mu_rwkv, w0_rwkv, w2_rwkv, a0_rwkv, a2_rwkv, g2_rwkv, k_k, k_a, r_k, ln_x_w, ln_x_b, conv_w, conv_b, w_lru_a, b_lru_a, w_lru_x, b_lru_x, lam, g_lru_out, w_out, w_router, b_router, w_e_gate, w_e_up, w_e_down, w_s_gate, w_s_up, w_s_down):
    a = dict(w_in=w_in, g_pre_mix=g_pre_mix, g_post_mix=g_post_mix, g_pre_ffn=g_pre_ffn, g_post_ffn=g_post_ffn,
             g_q_a=g_q_a, w_q_b=w_q_b, g_kv_a=g_kv_a, w_kv_b=w_kv_b, g_mla_out=g_mla_out, mu_rwkv=mu_rwkv,
             w0_rwkv=w0_rwkv, w2_rwkv=w2_rwkv, a0_rwkv=a0_rwkv, a2_rwkv=a2_rwkv, g2_rwkv=g2_rwkv, k_k=k_k, k_a=k_a,
             r_k=r_k.reshape(r_k.shape[0], -1), ln_x_w=ln_x_w, ln_x_b=ln_x_b, conv_w=conv_w, conv_b=conv_b,
             w_lru_a=w_lru_a, b_lru_a=b_lru_a, w_lru_x=w_lru_x, b_lru_x=b_lru_x, lam=lam, g_lru_out=g_lru_out,
             w_out=w_out, w_router=w_router, b_router=b_router, w_e_gate=w_e_gate, w_e_up=w_e_up,
             w_e_down=w_e_down, w_s_gate=w_s_gate, w_s_up=w_s_up, w_s_down=w_s_down)
    n_layers = w_in.shape[0]
    nbp, tp, d = x_prompt.shape
    nbs, ts, _ = x_sample.shape
    rows = -(-(1 + nbs) // SUBLANES) * SUBLANES
    cvecs = jnp.concatenate([c_ctx[None, :], c, jnp.zeros((rows - 1 - nbs, d), F32)], axis=0)
    mods = _ada(cvecs, w_ada, b_ada)
    names = ('sh1', 'sc1', 'g1', 'sh2', 'sc2', 'g2')
    tables = _rope_tables(ts)
    tn = TOKEN_TILE

    grp_p, grp_s = (0, nbp, tp), (nbp * tp, nbs, ts)
    x = jnp.concatenate([x_prompt.reshape(-1, d), x_sample.reshape(-1, d)], axis=0)
    caches, rwkv_states, lru_states = [], [], []
    for l in range(n_layers):
        lp = _layer_operands(l, a)
        mod_p = {nm: mods[l, 0:1, i * d:(i + 1) * d][:, None, :] for i, nm in enumerate(names)}
        mod_s = {nm: mods[l, 1:1 + nbs, i * d:(i + 1) * d][:, None, :] for i, nm in enumerate(names)}
        x1, h2, states = _mixer(x, grp_p, mod_p, lp, None, None, None)
        x1, h2, _ = _mixer(x, grp_s, mod_s, lp, (cache_mla[:, l], state_rwkv[:, l], state_lru[:, l]), tables,
                           (x1, h2))
        caches.append(states[0])
        rwkv_states.append(states[1])
        lru_states.append(states[2])
        gate_rows = jnp.concatenate([jnp.repeat(mod_p['g2'], nbp * tp // tn, axis=0),
                                     jnp.repeat(mod_s['g2'], ts // tn, axis=0)], axis=0)
        x = _moe(x1, h2, gate_rows, lp)
    yp = x[:nbp * tp].reshape(nbp, tp, d)
    ys = x[nbp * tp:].reshape(nbs, ts, d)
    return (yp, ys, jnp.stack(caches, axis=1), jnp.stack(rwkv_states, axis=1), jnp.stack(lru_states, axis=1))
```

```python
import functools

import numpy as np
import jax
import jax.numpy as jnp
from jax import lax
from jax.experimental import pallas as pl
from jax.experimental.pallas import tpu as pltpu

F32 = jnp.float32
BF16 = jnp.bfloat16
I32 = jnp.int32

D_MODEL = 1024
GRID_W = 64
EPS = 1e-6
H_MLA = 4
Q_LORA = 256
KV_LORA = 128
NOPE_DIM = 64
ROPE_DIM = 32
V_DIM = 128
QK_DIM = NOPE_DIM + ROPE_DIM
ROPE_THETA = 10000.0
HEAD_PAD = 128
H_RWKV = 4
N_RWKV = 64
D_RWKV = H_RWKV * N_RWKV
W_LORA = 64
A_LORA = 64
G_LORA = 128
GN_EPS = 64e-5
D_LRU = 256
H_LRU = 4
BS_LRU = D_LRU // H_LRU
CONV_W = 4
LRU_C = 8.0
D_MLA_OUT = H_MLA * V_DIM
P_MLA_PAD = Q_LORA + KV_LORA + HEAD_PAD
P_RWKV = 3 * D_RWKV + 2 * W_LORA + 2 * A_LORA + G_LORA
P_LRU = 2 * D_LRU
N_EXPERTS = 64
TOP_K = 6
N_GROUPS = 8
TOPK_GROUPS = 4
GROUP_SIZE = N_EXPERTS // N_GROUPS
D_EXPERT = 256
ROUTED_SCALE = 2.5

SUBLANES = 8
RWKV_CHUNK = 64
RWKV_ROWS = 2
MOE_BLOCK = 512
ROUTER_TILE = 256
DISPATCH_SHIFT = 5
DISPATCH_CHUNK = 1 << DISPATCH_SHIFT
TOKEN_TILE = 256
SLOT_ROWS = 8
VMEM_LIMIT = 56 * 1024 * 1024


def _cparams(*sem):
    return pltpu.CompilerParams(dimension_semantics=sem, vmem_limit_bytes=VMEM_LIMIT)


def _sigmoid(x):
    return 1.0 / (1.0 + jnp.exp(-x))


def _softplus(x):
    return jnp.maximum(x, 0.0) + jnp.log1p(jnp.exp(-jnp.abs(x)))


def _expm1(z):
    e = jnp.exp(z)
    direct = (e == 1.0) | (z < -1.0)
    corrected = (e - 1.0) * z / jnp.log(jnp.where(direct, 2.0, e))
    return jnp.where(e == 1.0, z, jnp.where(z < -1.0, e - 1.0, corrected))


def _rms(x, g):
    return x * lax.rsqrt(jnp.mean(x * x, axis=-1, keepdims=True) + EPS) * g


def _split2(x):
    hi = x.astype(BF16)
    lo = (x - hi.astype(F32)).astype(BF16)
    return hi, lo


def _split3(x):
    hi = x.astype(BF16)
    r1 = x - hi.astype(F32)
    mid = r1.astype(BF16)
    lo = (r1 - mid.astype(F32)).astype(BF16)
    return hi, mid, lo


_NN = (((1,), (0,)), ((), ()))
_NT = (((1,), (1,)), ((), ()))


def _dg(a, b, dims):
    return lax.dot_general(a, b, dims, preferred_element_type=F32)


def _dot1(a, b, dims=_NN):
    return _dg(a.astype(BF16), b.astype(BF16), dims)


def _dot3(a, b, dims=_NN):
    ah, al = _split2(a)
    bh, bl = _split2(b)
    return _dg(ah, bh, dims) + (_dg(ah, bl, dims) + _dg(al, bh, dims))


def _dot_x(a, b_exact, dims=_NN):
    h, m, l = _split3(a)
    bb = b_exact.astype(BF16)
    return _dg(h, bb, dims) + (_dg(m, bb, dims) + _dg(l, bb, dims))


def _x_dot(a_exact, b, dims=_NN):
    h, m, l = _split3(b)
    aa = a_exact.astype(BF16)
    return _dg(aa, h, dims) + (_dg(aa, m, dims) + _dg(aa, l, dims))


def _head_ones(n, seg):
    r = lax.broadcasted_iota(I32, (n, n), 0) // seg
    c = lax.broadcasted_iota(I32, (n, n), 1) // seg
    return (r == c).astype(F32)


def _ada_body(c_ref, w_ref, b_ref, o_ref):
    cv = c_ref[...]
    s = cv * _sigmoid(cv)
    o_ref[0] = _dot3(s, w_ref[0]) + b_ref[0]


def _ada(cvecs, w_ada, b_ada):
    n_layers, d, d6 = w_ada.shape
    rows = cvecs.shape[0]
    nt = d6 // d
    return pl.pallas_call(
        _ada_body,
        grid=(n_layers, nt),
        in_specs=[
            pl.BlockSpec((rows, d), lambda l, j: (0, 0)),
            pl.BlockSpec((1, d, d), lambda l, j: (l, 0, j)),
            pl.BlockSpec((1, 1, d), lambda l, j: (l, 0, j)),
        ],
        out_specs=pl.BlockSpec((1, rows, d), lambda l, j: (l, 0, j)),
        out_shape=jax.ShapeDtypeStruct((n_layers, rows, d6), F32),
        compiler_params=_cparams("parallel", "parallel"),
        name="ada",
    )(cvecs, w_ada, b_ada.reshape(n_layers, 1, d6))


def _mod_spec(per_batch):
    if per_batch:
        return pl.BlockSpec((1, 1, D_MODEL), lambda b, i: (b, 0, 0))
    return pl.BlockSpec((1, 1, D_MODEL), lambda b, i: (0, 0, 0))


def _full(shape):
    nd = len(shape)
    return pl.BlockSpec(shape, lambda *_: (0,) * nd)


def _in_proj_body(x_ref, g_ref, sc_ref, sh_ref, w1_ref, w2_ref, w3_ref, o1_ref, o2_ref, o3_ref):
    h = _rms(x_ref[...], g_ref[...]) * (1.0 + sc_ref[0]) + sh_ref[0]
    hb = h.astype(BF16)
    o1_ref[0] = _dg(hb, w1_ref[...], _NN)
    o2_ref[0] = _dg(hb, w2_ref[...], _NN)
    o3_ref[0] = _dg(hb, w3_ref[...], _NN)


def _group_rows(grp, tt):
    row_off, nb, t = grp
    base, per = row_off // tt, t // tt
    return pl.BlockSpec((tt, D_MODEL), lambda b, i: (base + b * per + i, 0))


def _in_proj(x, grp, g, sc, sh, w_mla, w_rwkv, w_lru, tt):
    _, nb, t = grp
    d = x.shape[1]
    per_batch = sc.shape[0] > 1
    outs = [w_mla.shape[1], w_rwkv.shape[1], w_lru.shape[1]]
    return pl.pallas_call(
        _in_proj_body,
        grid=(nb, t // tt),
        in_specs=[
            _group_rows(grp, tt),
            _full((1, d)),
            _mod_spec(per_batch),
            _mod_spec(per_batch),
            _full(w_mla.shape),
            _full(w_rwkv.shape),
            _full(w_lru.shape),
        ],
        out_specs=[pl.BlockSpec((1, tt, n), lambda b, i: (b, i, 0)) for n in outs],
        out_shape=[jax.ShapeDtypeStruct((nb, t, n), F32) for n in outs],
        compiler_params=_cparams("parallel", "parallel"),
        name="in_proj",
    )(x, g, sc, sh, w_mla, w_rwkv, w_lru)


def _rope_lanes():
    lane = lax.broadcasted_iota(I32, (1, HEAD_PAD), 1)
    return ((lane >= NOPE_DIM) & (lane < QK_DIM)).astype(F32)


def _mla_prep_body(rope, *refs):
    if rope:
        (p_ref, gq_ref, gkv_ref, wq_ref, wqr_ref, wk_ref, wv_ref, cq_ref, sq_ref, cr_ref, sr_ref,
         q_ref, k_ref, v_ref) = refs
    else:
        p_ref, gq_ref, gkv_ref, wq_ref, wk_ref, wv_ref, q_ref, k_ref, v_ref, c_ref = refs
    p = p_ref[0]
    cq = _rms(p[:, :Q_LORA], gq_ref[...])
    ckv = _rms(p[:, Q_LORA:Q_LORA + KV_LORA], gkv_ref[...])
    kr = p[:, Q_LORA + KV_LORA:]
    cqb = cq.astype(BF16)
    ckvb = ckv.astype(BF16)
    scale = QK_DIM ** -0.5 * float(np.log2(np.e))
    qa = _dg(cqb, wq_ref[...], _NN) * scale
    ka = _dg(ckvb, wk_ref[...], _NN)
    va = _dg(ckvb, wv_ref[...], _NN)
    if rope:
        qr = _dg(cqb, wqr_ref[...], _NN) * scale
        krp = kr * cr_ref[...] + pltpu.roll(kr, ROPE_DIM, 1) * sr_ref[...]
    else:
        krp = kr * _rope_lanes()
    for h in range(H_MLA):
        sl = slice(h * HEAD_PAD, (h + 1) * HEAD_PAD)
        qh = qa[:, sl]
        if rope:
            qh = qh * cq_ref[...] + qr[:, sl] * sq_ref[...]
        q_ref[0, h] = qh.astype(BF16)
        k_ref[0, h] = (ka[:, sl] + krp).astype(BF16)
        v_ref[0, h] = va[:, sl].astype(BF16)
    if not rope:
        c_ref[0, :, :KV_LORA] = ckv
        c_ref[0, :, KV_LORA:] = pltpu.roll(kr, HEAD_PAD - NOPE_DIM, 1)[:, :ROPE_DIM]


def _mla_prep(p_mla, gq, gkv, wq, wqr, wk, wv, tables, tt):
    nb, t, pw = p_mla.shape
    rope = tables is not None
    hw = H_MLA * HEAD_PAD
    head_spec = pl.BlockSpec((1, H_MLA, tt, HEAD_PAD), lambda b, i: (b, 0, i, 0))
    head_shape = jax.ShapeDtypeStruct((nb, H_MLA, t, HEAD_PAD), BF16)
    in_specs = [pl.BlockSpec((1, tt, pw), lambda b, i: (b, i, 0)), _full(gq.shape), _full(gkv.shape), _full(wq.shape)]
    args = [p_mla, gq, gkv, wq]
    if rope:
        in_specs.append(_full(wqr.shape))
        args.append(wqr)
    in_specs += [_full(wk.shape), _full(wv.shape)]
    args += [wk, wv]
    out_specs = [head_spec, head_spec, head_spec]
    out_shape = [head_shape, head_shape, head_shape]
    if rope:
        in_specs += [pl.BlockSpec((tt, HEAD_PAD), lambda b, i: (i, 0))] * 4
        args += list(tables)
    else:
        out_specs.append(pl.BlockSpec((1, tt, KV_LORA + ROPE_DIM), lambda b, i: (b, i, 0)))
        out_shape.append(jax.ShapeDtypeStruct((nb, t, KV_LORA + ROPE_DIM), F32))
    del hw
    return pl.pallas_call(
        functools.partial(_mla_prep_body, rope),
        grid=(nb, t // tt),
        in_specs=in_specs,
        out_specs=out_specs,
        out_shape=out_shape,
        compiler_params=_cparams("parallel", "parallel"),
        name="mla_prep_rope" if rope else "mla_prep",
    )(*args)


def _ctx_kv_body(lat_ref, kr_ref, wk_ref, wv_ref, k_ref, v_ref):
    latb = lat_ref[0].astype(BF16)
    ka = _dg(latb, wk_ref[...], _NN)
    va = _dg(latb, wv_ref[...], _NN)
    kr = kr_ref[0]
    for h in range(H_MLA):
        sl = slice(h * HEAD_PAD, (h + 1) * HEAD_PAD)
        k_ref[0, h] = (ka[:, sl] + kr).astype(BF16)
        v_ref[0, h] = va[:, sl].astype(BF16)


def _ctx_kv(lat, kr_placed, wk, wv):
    nb, s, _ = lat.shape
    head_spec = pl.BlockSpec((1, H_MLA, s, HEAD_PAD), lambda b: (b, 0, 0, 0))
    head_shape = jax.ShapeDtypeStruct((nb, H_MLA, s, HEAD_PAD), BF16)
    return pl.pallas_call(
        _ctx_kv_body,
        grid=(nb,),
        in_specs=[
            pl.BlockSpec((1, s, KV_LORA), lambda b: (b, 0, 0)),
            pl.BlockSpec((1, s, HEAD_PAD), lambda b: (b, 0, 0)),
            _full(wk.shape),
            _full(wv.shape),
        ],
        out_specs=[head_spec, head_spec],
        out_shape=[head_shape, head_shape],
        compiler_params=_cparams("parallel"),
        name="ctx_kv",
    )(lat, kr_placed, wk, wv)


def _attn_body(has_ctx, *refs):
    if has_ctx:
        q_ref, k_ref, v_ref, kc_ref, vc_ref, o_ref = refs
    else:
        q_ref, k_ref, v_ref, o_ref = refs
    q = q_ref[0, 0]
    s = _dg(q, k_ref[0, 0], _NT)
    m = jnp.max(s, axis=-1, keepdims=True)
    if has_ctx:
        sc = _dg(q, kc_ref[0, 0], _NT)
        m = jnp.maximum(m, jnp.max(sc, axis=-1, keepdims=True))
    p = jnp.exp2(s - m)
    l = jnp.sum(p, axis=-1, keepdims=True)
    o = _dg(p.astype(BF16), v_ref[0, 0], _NN)
    if has_ctx:
        pc = jnp.exp2(sc - m)
        l = l + jnp.sum(pc, axis=-1, keepdims=True)
        o = o + _dg(pc.astype(BF16), vc_ref[0, 0], _NN)
    o_ref[0] = o / l


def _attention(q, k, v, kc, vc, tq):
    nb, nh, t, hd = q.shape
    s = k.shape[2]
    has_ctx = kc is not None
    in_specs = [
        pl.BlockSpec((1, 1, tq, hd), lambda b, h, i: (b, h, i, 0)),
        pl.BlockSpec((1, 1, s, hd), lambda b, h, i: (b, h, 0, 0)),
        pl.BlockSpec((1, 1, s, V_DIM), lambda b, h, i: (b, h, 0, 0)),
    ]
    args = [q, k, v]
    if has_ctx:
        sc = kc.shape[2]
        in_specs += [
            pl.BlockSpec((1, 1, sc, hd), lambda b, h, i: (b, h, 0, 0)),
            pl.BlockSpec((1, 1, sc, V_DIM), lambda b, h, i: (b, h, 0, 0)),
        ]
        args += [kc, vc]
    return pl.pallas_call(
        functools.partial(_attn_body, has_ctx),
        grid=(nb, nh, t // tq),
        in_specs=in_specs,
        out_specs=pl.BlockSpec((1, tq, V_DIM), lambda b, h, i: (b, i, h)),
        out_shape=jax.ShapeDtypeStruct((nb, t, nh * V_DIM), F32),
        compiler_params=_cparams("parallel", "parallel", "parallel"),
        name="attention",
    )(*args)


def _rwkv_prep_body(p_ref, hp_ref, hn_ref, mu_ref, kk_ref, ka_ref, rk_ref, w0_ref, w2_ref, a0_ref, a2_ref, g2_ref,
                    r_out, v_out, kk_out, lw_out, a_out, kd_out, g_out, bonus_out):
    p = p_ref[0]
    tt = p.shape[0]
    row = lax.broadcasted_iota(I32, (tt, 1), 0)
    prev = jnp.where(row == 0, hp_ref[0, 0], pltpu.roll(p, 1, 0))
    nxt = jnp.where(row == tt - 1, hn_ref[0, 0], pltpu.roll(p, tt - 1, 0))
    ps = p + (0.5 * (prev + nxt) - p) * mu_ref[...]
    r = ps[:, :D_RWKV]
    k = ps[:, D_RWKV:2 * D_RWKV]
    v = ps[:, 2 * D_RWKV:3 * D_RWKV]
    wl = ps[:, 3 * D_RWKV:3 * D_RWKV + 2 * W_LORA]
    al = ps[:, 3 * D_RWKV + 2 * W_LORA:3 * D_RWKV + 2 * W_LORA + 2 * A_LORA]
    gl = ps[:, 3 * D_RWKV + 2 * W_LORA + 2 * A_LORA:]
    ones = _head_ones(D_RWKV, N_RWKV)
    kk = k * kk_ref[...]
    kk = kk / jnp.maximum(jnp.sqrt(_dot_x(kk * kk, ones)), 1e-12)
    wlin = w0_ref[...] + _dot3(jnp.tanh(wl), w2_ref[...])
    lw = -jnp.exp(-_softplus(-wlin) - 0.5)
    ag = _sigmoid(a0_ref[...] + _dot3(al, a2_ref[...]))
    bonus = jnp.zeros_like(v)
    for d in range(2):
        sl = slice(d * D_RWKV, (d + 1) * D_RWKV)
        kd = k * (1.0 + (ag[:, sl] - 1.0) * ka_ref[...])
        kd_out[0, :, sl] = kd
        bonus = bonus + _dot_x(r * kd * rk_ref[...], ones) * v
    r_out[0] = r
    v_out[0] = v
    kk_out[0] = kk
    lw_out[0] = lw
    a_out[0] = ag
    g_out[0] = _dot1(_sigmoid(gl), g2_ref[...])
    bonus_out[0] = bonus


def _halo_rows(x, tt, width):
    nb, t, c = x.shape
    nt = t // tt
    xt = x.reshape(nb, nt, tt, c)
    zero = jnp.zeros((nb, 1, width, c), x.dtype)
    prev = jnp.concatenate([zero, xt[:, :-1, tt - width:]], axis=1)
    nxt = jnp.concatenate([xt[:, 1:, :width], zero], axis=1)
    return prev, nxt


def _rwkv_prep(p_rwkv, lp, tt):
    nb, t, pw = p_rwkv.shape
    hp, hn = _halo_rows(p_rwkv, tt, 1)
    tok = lambda n: pl.BlockSpec((1, tt, n), lambda b, i: (b, i, 0))
    halo = pl.BlockSpec((1, 1, 1, pw), lambda b, i: (b, i, 0, 0))
    small = [lp['mu_rwkv'], lp['k_k'], lp['k_a'], lp['r_k'], lp['w0_cat'], lp['w2_bd'], lp['a0_cat'], lp['a2_bd'],
             lp['g2_rwkv']]
    widths = [D_RWKV, D_RWKV, D_RWKV, 2 * D_RWKV, 2 * D_RWKV, 2 * D_RWKV, D_RWKV, D_RWKV]
    return pl.pallas_call(
        _rwkv_prep_body,
        grid=(nb, t // tt),
        in_specs=[tok(pw), halo, halo] + [_full(a.shape) for a in small],
        out_specs=[tok(n) for n in widths],
        out_shape=[jax.ShapeDtypeStruct((nb, t, n), F32) for n in widths],
        compiler_params=_cparams("parallel", "parallel"),
        name="rwkv_prep",
    )(p_rwkv, hp, hn, *small)


def _rwkv_chunk(reverse, r, v, kk, lw, ag, kd, s_old):
    cl = r.shape[0]
    hw = D_RWKV

    def later(rows, cols, strict):
        t = lax.broadcasted_iota(I32, (rows, cols), 0)
        s = lax.broadcasted_iota(I32, (rows, cols), 1) % cl
        dlt = (s - t) if reverse else (t - s)
        return (dlt > 0) if strict else (dlt >= 0)

    cum = _x_dot(later(cl, cl, False).astype(F32), lw)
    yield
    tot = jnp.sum(lw, axis=0, keepdims=True)
    e_neg = jnp.exp(-cum)
    e_rem = jnp.exp(tot - cum)
    at = -kk * jnp.exp(cum - lw)
    rt = r * jnp.exp(cum)
    b0 = kk * ag
    bt = b0 * e_neg
    kt = kd * e_neg
    bh = b0 * e_rem
    kh = kd * e_rem
    w_c = jnp.exp(tot)

    lane_head = lax.broadcasted_iota(I32, (1, hw), 1) // N_RWKV
    hms = [(lane_head == h).astype(F32) for h in range(H_RWKV)]
    heads = range(H_RWKV)
    bk = [jnp.concatenate([bt * hm, kt * hm], axis=0) for hm in hms]
    strict = later(cl, 2 * cl, True)
    incl = later(cl, 2 * cl, False)
    ls = [jnp.where(strict, _dot3(at, bk[h], _NT), 0.0) for h in heads]
    ly = [jnp.where(incl, _dot1(rt, bk[h], _NT), 0.0) for h in heads]
    yield

    x = [l[:, :cl] for l in ls]
    eye = (lax.broadcasted_iota(I32, (cl, cl), 0) == lax.broadcasted_iota(I32, (cl, cl), 1)).astype(F32)
    inv = [eye + xh for xh in x]
    span = 2
    while span < cl:
        x = [_dot3(xh, xh) for xh in x]
        yield
        inv = [ih + _dot3(ih, xh) for ih, xh in zip(inv, x)]
        yield
        span *= 2

    zero = jnp.zeros((cl, hw), F32)
    v_rows = jnp.concatenate([piece for hm in hms for piece in (zero, v * hm)], axis=0)
    rhs = _dot3(at, s_old, _NT) + _dot3(jnp.concatenate(ls, axis=1), v_rows)
    yield
    us = [_dot3(inv[h], rhs * hms[h]) for h in heads]
    yield
    um = us[0]
    for h in range(1, H_RWKV):
        um = um + us[h]
    uv_rows = jnp.concatenate([piece for h in heads for piece in (us[h], v * hms[h])], axis=0)
    y = _dot1(rt, s_old, _NT) + _dot1(jnp.concatenate(ly, axis=1), uv_rows)
    yield
    uv_t = jnp.concatenate([um, v], axis=0).T
    s_new = s_old * w_c + _dot3(uv_t, jnp.concatenate([bh, kh], axis=0)) * _head_ones(hw, N_RWKV)
    yield y, s_new


def _lockstep(gens):
    last = [None] * len(gens)
    live = list(range(len(gens)))
    while live:
        for i in list(live):
            try:
                last[i] = next(gens[i])
            except StopIteration:
                live.remove(i)
    return last


def _rwkv_scan_body(rf_ref, vf_ref, kkf_ref, lwf_ref, af_ref, kdf_ref, rb_ref, vb_ref, kkb_ref, lwb_ref, ab_ref,
                    kdb_ref, s0_ref, yf_ref, yb_ref, sf_ref, s_scr):
    c = pl.program_id(1)
    nc = pl.num_programs(1)

    @pl.when(c == 0)
    def _():
        s_scr[...] = s0_ref[...]

    nrow = rf_ref.shape[0]
    chains = []
    for i in range(nrow):
        chains.append(_rwkv_chunk(False, rf_ref[i], vf_ref[i], kkf_ref[i], lwf_ref[i], af_ref[i], kdf_ref[i],
                                  s_scr[i, 0]))
        chains.append(_rwkv_chunk(True, rb_ref[i], vb_ref[i], kkb_ref[i], lwb_ref[i], ab_ref[i], kdb_ref[i],
                                  s_scr[i, 1]))
    done = _lockstep(chains)
    for i in range(nrow):
        (y_f, s_f), (y_b, s_b) = done[2 * i], done[2 * i + 1]
        yf_ref[i] = y_f
        yb_ref[i] = y_b
        s_scr[i, 0] = s_f
        s_scr[i, 1] = s_b

    @pl.when(c == nc - 1)
    def _():
        sf_ref[...] = s_scr[...]


def _rwkv_scan(r, v, kk, lw, ag, kd, s0):
    nb, t, hw = r.shape
    cl = RWKV_CHUNK
    nr = RWKV_ROWS
    nc = t // cl
    fwd = lambda lane: pl.BlockSpec((nr, cl, hw), lambda b, c: (b, c, lane))
    bwd = lambda lane: pl.BlockSpec((nr, cl, hw), lambda b, c: (b, nc - 1 - c, lane))
    state = pl.BlockSpec((nr, 2, hw, hw), lambda b, c: (b, 0, 0, 0))
    y_shape = jax.ShapeDtypeStruct((nb, t, hw), F32)
    return pl.pallas_call(
        _rwkv_scan_body,
        grid=(nb // nr, nc),
        in_specs=[fwd(0)] * 6 + [bwd(0)] * 3 + [bwd(1)] * 3 + [state],
        out_specs=[fwd(0), bwd(0), state],
        out_shape=[y_shape, y_shape, jax.ShapeDtypeStruct((nb, 2, hw, hw), F32)],
        scratch_shapes=[pltpu.VMEM((nr, 2, hw, hw), F32)],
        compiler_params=_cparams("parallel", "arbitrary"),
        name="rwkv_scan",
    )(r, v, kk, lw, ag, kd, r, v, kk, lw, ag, kd, s0)


def _lru_body(reverse, p_ref, halo_ref, cw_ref, cb_ref, wa_ref, ba_ref, wx_ref, bx_ref, lam_ref, h0_ref,
              h_out, fin_out, a_scr, u_scr, carry_scr):
    i = pl.program_id(1)
    nt = pl.num_programs(1)
    tt = p_ref.shape[1]
    n_ext = tt + SUBLANES

    @pl.when(i == 0)
    def _():
        carry_scr[...] = jnp.broadcast_to(h0_ref[0], (SUBLANES, D_LRU))

    xb = p_ref[0, :, :D_LRU]
    halo = halo_ref[0, 0]
    xc = cb_ref[...] + jnp.zeros_like(xb)
    if reverse:
        xe = jnp.concatenate([xb, halo], axis=0)
        for k in range(CONV_W):
            sh = xb if k == 0 else pltpu.roll(xe, n_ext - k, 0)[:tt]
            xc = xc + cw_ref[k:k + 1, :] * sh
    else:
        xe = jnp.concatenate([halo, xb], axis=0)
        for k in range(CONV_W):
            s = CONV_W - 1 - k
            sh = xb if s == 0 else pltpu.roll(xe, s, 0)[SUBLANES:]
            xc = xc + cw_ref[k:k + 1, :] * sh
    gate_r = _sigmoid(_dot3(xc, wa_ref[...]) + ba_ref[...])
    gate_i = _sigmoid(_dot3(xc, wx_ref[...]) + bx_ref[...])
    log_a = -LRU_C * gate_r * _softplus(-lam_ref[...])
    a = jnp.exp(log_a)
    u = jnp.sqrt(-_expm1(2.0 * log_a)) * (gate_i * xc)

    rowm = lax.broadcasted_iota(I32, (tt, 1), 0) % SUBLANES
    for s in (1, 2, 4):
        if reverse:
            a_s = pltpu.roll(a, tt - s, 0)
            u_s = pltpu.roll(u, tt - s, 0)
            m = rowm < SUBLANES - s
        else:
            a_s = pltpu.roll(a, s, 0)
            u_s = pltpu.roll(u, s, 0)
            m = rowm >= s
        u = jnp.where(m, a * u_s + u, u)
        a = jnp.where(m, a * a_s, a)
    a_scr[...] = a
    u_scr[...] = u
    ng = tt // SUBLANES

    def group(g, carry):
        gi = (ng - 1 - g) if reverse else g
        r0 = pl.multiple_of(gi * SUBLANES, SUBLANES)
        hg = a_scr[pl.ds(r0, SUBLANES), :] * carry + u_scr[pl.ds(r0, SUBLANES), :]
        h_out[0, pl.ds(r0, SUBLANES), :] = hg
        edge = hg[0:1, :] if reverse else hg[SUBLANES - 1:SUBLANES, :]
        return jnp.broadcast_to(edge, (SUBLANES, D_LRU))

    carry = lax.fori_loop(0, ng, group, carry_scr[...])
    carry_scr[...] = carry

    @pl.when(i == nt - 1)
    def _():
        fin_out[0] = carry[0:1, :]


def _lru_dir(p_lru, halo, lp, d, h0, tt):
    nb, t, pw = p_lru.shape
    nt = t // tt
    reverse = d == 1
    tidx = (lambda i: nt - 1 - i) if reverse else (lambda i: i)
    small = [lp['conv_w'][d], lp['conv_b'][d:d + 1], lp['wa_bd'][d], lp['b_lru_a'][d:d + 1], lp['wx_bd'][d],
             lp['b_lru_x'][d:d + 1], lp['lam'][d:d + 1]]
    return pl.pallas_call(
        functools.partial(_lru_body, reverse),
        grid=(nb, nt),
        in_specs=[
            pl.BlockSpec((1, tt, pw), lambda b, i: (b, tidx(i), 0)),
            pl.BlockSpec((1, 1, SUBLANES, D_LRU), lambda b, i: (b, tidx(i), 0, 0)),
        ] + [_full(a.shape) for a in small] + [pl.BlockSpec((1, 1, D_LRU), lambda b, i: (b, 0, 0))],
        out_specs=[
            pl.BlockSpec((1, tt, D_LRU), lambda b, i: (b, tidx(i), 0)),
            pl.BlockSpec((1, 1, D_LRU), lambda b, i: (b, 0, 0)),
        ],
        out_shape=[jax.ShapeDtypeStruct((nb, t, D_LRU), F32), jax.ShapeDtypeStruct((nb, 1, D_LRU), F32)],
        scratch_shapes=[pltpu.VMEM((tt, D_LRU), F32), pltpu.VMEM((tt, D_LRU), F32), pltpu.VMEM((SUBLANES, D_LRU), F32)],
        compiler_params=_cparams("parallel", "arbitrary"),
        name="lru_bwd" if reverse else "lru_fwd",
    )(p_lru, halo, *small, h0)


def _gelu_tanh(x):
    return 0.5 * x * (1.0 + jnp.tanh(np.sqrt(2.0 / np.pi).astype(np.float32) * (x + 0.044715 * (x * x * x))))


def _out_proj_body(x_ref, o_ref, y0_ref, y1_ref, bonus_ref, g_ref, h0_ref, h1_ref, p_ref,
                   gmla_ref, lnw_ref, lnb_ref, glru_ref, wo_ref, gpost_ref, gate_ref, gpre_ref, sc_ref, sh_ref,
                   *rest):
    x1_out, h2_out = rest[-2:]
    o_mla = _rms(o_ref[0], gmla_ref[...])
    ones = _head_ones(D_RWKV, N_RWKV)
    y = y0_ref[0] + y1_ref[0]
    mu = _dot_x(y, ones) * (1.0 / N_RWKV)
    yc = y - mu
    var = _dot_x(yc * yc, ones) * (1.0 / N_RWKV)
    y = yc * lax.rsqrt(var + GN_EPS) * lnw_ref[...] + lnb_ref[...]
    o_rwkv = (y + bonus_ref[0]) * g_ref[0]
    o_lru = _rms((h0_ref[0] + h1_ref[0]) * _gelu_tanh(p_ref[0, :, D_LRU:]), glru_ref[...])
    m = (_dg(o_mla.astype(BF16), wo_ref[:D_MLA_OUT, :], _NN)
         + _dg(o_rwkv.astype(BF16), wo_ref[D_MLA_OUT:D_MLA_OUT + D_RWKV, :], _NN)
         + _dg(o_lru.astype(BF16), wo_ref[D_MLA_OUT + D_RWKV:, :], _NN))
    x1 = x_ref[...] + gate_ref[0] * _rms(m, gpost_ref[...])
    x1_out[...] = x1
    h2_out[...] = _rms(x1, gpre_ref[...]) * (1.0 + sc_ref[0]) + sh_ref[0]


def _out_proj(x, grp, o, y_f, y_b, bonus, g, h0, h1, p_lru, lp, mod, tt, prev):
    _, nb, t = grp
    per_batch = mod['g1'].shape[0] > 1
    tok = lambda n: pl.BlockSpec((1, tt, n), lambda b, i: (b, i, 0))
    rows = lambda: _group_rows(grp, tt)
    small = [lp['g_mla_out'], lp['ln_x_w'], lp['ln_x_b'], lp['g_lru_out'], lp['w_out'], lp['g_post_mix']]
    args = [x, o, y_f, y_b, bonus, g, h0, h1, p_lru, *small, mod['g1'], lp['g_pre_ffn'], mod['sc2'], mod['sh2']]
    in_specs = ([rows(), tok(D_MLA_OUT), tok(D_RWKV), tok(D_RWKV), tok(D_RWKV), tok(D_RWKV), tok(D_LRU), tok(D_LRU),
                 tok(P_LRU)] + [_full(a.shape) for a in small]
                + [_mod_spec(per_batch), _full(lp['g_pre_ffn'].shape), _mod_spec(per_batch), _mod_spec(per_batch)])
    aliases = {}
    if prev is not None:
        aliases = {len(args): 0, len(args) + 1: 1}
        args += list(prev)
        in_specs += [pl.BlockSpec(memory_space=pl.ANY)] * 2
    return pl.pallas_call(
        _out_proj_body,
        grid=(nb, t // tt),
        in_specs=in_specs,
        out_specs=[rows(), rows()],
        out_shape=[jax.ShapeDtypeStruct(x.shape, F32)] * 2,
        input_output_aliases=aliases,
        compiler_params=_cparams("parallel", "parallel"),
        name="out_proj",
    )(*args)


def _first_index(mask, idx, sentinel):
    return jnp.min(jnp.where(mask, idx, sentinel), axis=0, keepdims=True)


def _router_body(h_ref, wr_ref, br_ref, e_out, pos_out, loc_out, stat_out, w_out, cnt_out, carry):
    i = pl.program_id(0)
    tn = h_ref.shape[0]

    @pl.when(i == 0)
    def _():
        carry[...] = jnp.zeros_like(carry)

    logits = lax.dot_general(wr_ref[...], h_ref[...], _NT, precision=lax.Precision.HIGHEST,
                             preferred_element_type=F32)
    scores = _sigmoid(logits)
    sel = scores + br_ref[...]

    i8 = lax.broadcasted_iota(I32, (GROUP_SIZE, tn), 0)
    blocks, gscore = [], []
    for g in range(N_GROUPS):
        blk = sel[g * GROUP_SIZE:(g + 1) * GROUP_SIZE, :]
        m1 = jnp.max(blk, axis=0, keepdims=True)
        f1 = _first_index(blk == m1, i8, GROUP_SIZE)
        m2 = jnp.max(jnp.where(i8 == f1, -jnp.inf, blk), axis=0, keepdims=True)
        blocks.append(blk)
        gscore.append(m1 + m2)
    masked = []
    for g in range(N_GROUPS):
        beaten = jnp.zeros((1, tn), I32)
        for o in range(N_GROUPS):
            if o == g:
                continue
            wins = (gscore[o] >= gscore[g]) if o < g else (gscore[o] > gscore[g])
            beaten = beaten + wins.astype(I32)
        masked.append(jnp.where(beaten < TOPK_GROUPS, blocks[g], -jnp.inf))
    cur = jnp.concatenate(masked, axis=0)

    ie = lax.broadcasted_iota(I32, (N_EXPERTS, tn), 0)
    firsts, raw = [], []
    chosen = jnp.zeros((N_EXPERTS, tn), F32)
    for _ in range(TOP_K):
        mx = jnp.max(cur, axis=0, keepdims=True)
        first = _first_index(cur == mx, ie, N_EXPERTS)
        hit = ie == first
        firsts.append(first)
        raw.append(jnp.sum(jnp.where(hit, scores, 0.0), axis=0, keepdims=True))
        cur = jnp.where(hit, -jnp.inf, cur)
        chosen = chosen + hit.astype(F32)
    total = raw[0]
    for k in range(1, TOP_K):
        total = total + raw[k]

    before = (lax.broadcasted_iota(I32, (tn, tn), 0) < lax.broadcasted_iota(I32, (tn, tn), 1)).astype(BF16)
    rank_tile = _dg(chosen.astype(BF16), before, _NN)
    old_carry = carry[...]
    rank = rank_tile + old_carry[:, 0:1]
    tile_cnt = jnp.broadcast_to(jnp.sum(chosen, axis=1, keepdims=True), old_carry.shape)
    lower = (lax.broadcasted_iota(I32, (N_EXPERTS, N_EXPERTS), 1)
             < lax.broadcasted_iota(I32, (N_EXPERTS, N_EXPERTS), 0)).astype(F32)
    tile_off = _x_dot(lower, tile_cnt)
    local = rank_tile + tile_off[:, 0:1]
    e_out[...] = jnp.zeros_like(e_out)
    pos_out[...] = jnp.zeros_like(pos_out)
    loc_out[...] = jnp.zeros_like(loc_out)
    w_out[...] = jnp.zeros_like(w_out)
    for k in range(TOP_K):
        hit = ie == firsts[k]
        e_out[0, k:k + 1, :] = firsts[k]
        pos_out[0, k:k + 1, :] = jnp.sum(jnp.where(hit, rank, 0.0), axis=0, keepdims=True).astype(I32)
        loc_out[0, k:k + 1, :] = jnp.sum(jnp.where(hit, local, 0.0), axis=0, keepdims=True).astype(I32)
        w_out[k:k + 1, :] = raw[k] / total * ROUTED_SCALE
    lane = lax.broadcasted_iota(I32, old_carry.shape, 1)
    stat_out[0] = jnp.where(lane == 0, tile_cnt, jnp.where(lane == 1, tile_off, old_carry)).astype(I32)
    new_carry = old_carry + tile_cnt
    carry[...] = new_carry
    cnt_out[...] = new_carry


def _router(h2, wr_t, b_col):
    n, d = h2.shape
    tn = ROUTER_TILE
    row_spec = pl.BlockSpec((SLOT_ROWS, tn), lambda i: (0, i))
    tile_spec = pl.BlockSpec((1, SLOT_ROWS, tn), lambda i: (i, 0, 0))
    cnt_spec = pl.BlockSpec((N_EXPERTS, 128), lambda i: (0, 0))
    tile_shape = jax.ShapeDtypeStruct((n // tn, SLOT_ROWS, tn), I32)
    return pl.pallas_call(
        _router_body,
        grid=(n // tn,),
        in_specs=[pl.BlockSpec((tn, d), lambda i: (i, 0)), _full(wr_t.shape), _full(b_col.shape)],
        out_specs=[tile_spec, tile_spec, tile_spec, pl.BlockSpec((1, N_EXPERTS, 128), lambda i: (i, 0, 0)),
                   row_spec, cnt_spec],
        out_shape=[tile_shape, tile_shape, tile_shape, jax.ShapeDtypeStruct((n // tn, N_EXPERTS, 128), I32),
                   jax.ShapeDtypeStruct((SLOT_ROWS, n), F32), jax.ShapeDtypeStruct((N_EXPERTS, 128), F32)],
        scratch_shapes=[pltpu.VMEM((N_EXPERTS, 128), F32)],
        compiler_params=_cparams("arbitrary"),
        name="router",
    )(h2, wr_t, b_col)


ROW_LANES = 128
ROW_SUB = D_MODEL // 2 // ROW_LANES
U32 = jnp.uint32
HIGH_HALF = np.uint32(0xFFFF0000)


def _bf16_bits(x):
    return lax.bitcast_convert_type(x.astype(BF16).astype(F32), U32)


def _to_row_tiles(ref, x):
    rows, d = x.shape
    words = (_bf16_bits(x[:, :d // 2]) >> 16) | (_bf16_bits(x[:, d // 2:]) & HIGH_HALF)
    for s in range(ROW_SUB):
        ref[pl.ds(s, rows, stride=ROW_SUB), :] = words[:, s * ROW_LANES:(s + 1) * ROW_LANES]


def _from_row_tiles(ref, rows):
    words = jnp.concatenate([ref[pl.ds(s, rows, stride=ROW_SUB), :] for s in range(ROW_SUB)], axis=1)
    low = lax.bitcast_convert_type(words << 16, F32)
    high = lax.bitcast_convert_type(words & HIGH_HALF, F32)
    return jnp.concatenate([low, high], axis=1)


def _row_copy(src, src_off, dst, dst_off, sem):
    return pltpu.make_async_copy(src.at[pl.ds(pl.multiple_of(src_off, ROW_SUB), ROW_SUB), :],
                                 dst.at[pl.ds(pl.multiple_of(dst_off, ROW_SUB), ROW_SUB), :], sem)


def _load_slots(slot_ref, slot_smem, sem):
    cp = pltpu.make_async_copy(slot_ref.at[0], slot_smem, sem)
    cp.start()
    cp.wait()


def _chunk_copy(src, src_row, dst, dst_row, sem):
    n = DISPATCH_CHUNK * ROW_SUB
    return pltpu.make_async_copy(src.at[pl.ds(pl.multiple_of(src_row * ROW_SUB, ROW_SUB), n), :],
                                 dst.at[pl.ds(pl.multiple_of(dst_row * ROW_SUB, ROW_SUB), n), :], sem)


def _dispatch_body(start_ref, e_ref, pos_ref, loc_ref, stat_ref, h_ref, xs_in, slots_out, xs_out,
                   stat_smem, pending, staging, sem_s, sem):
    del xs_in
    i = pl.program_id(0)
    last = pl.num_programs(0) - 1
    tn = h_ref.shape[0]
    rows = TOP_K * tn
    cur = i % 2

    e = e_ref[0]
    slot = pos_ref[0]
    for x in range(N_EXPERTS):
        slot = slot + jnp.where(e == x, start_ref[x], 0)
    slots_out[0] = slot * ROW_SUB
    stats = pltpu.make_async_copy(stat_ref.at[0], stat_smem, sem_s)
    stats.start()

    @pl.when(i == 0)
    def _():
        pending[0] = 0
        pad = jnp.zeros((DISPATCH_CHUNK * ROW_SUB, ROW_LANES), U32)
        for b in range(2):
            staging[b, pl.ds(rows * ROW_SUB, DISPATCH_CHUNK * ROW_SUB), :] = pad

    loc = loc_ref[0]
    p_iota = lax.broadcasted_iota(I32, (rows, tn), 0)
    pick = jnp.zeros((rows, tn), F32)
    for k in range(TOP_K):
        pick = jnp.where(p_iota == loc[k:k + 1, :], 1.0, pick)
    _to_row_tiles(staging.at[cur], _dg(pick.astype(BF16), h_ref[...].astype(BF16), _NN))

    def drain(count):
        def one(c, carry):
            _chunk_copy(staging.at[0], 0, xs_out, 0, sem).wait()
            return carry
        lax.fori_loop(0, count, one, 0)

    drain(pending[0])
    stats.wait()

    def per_expert(x, total):
        count = stat_smem[x, 0]
        src = stat_smem[x, 1]
        dst = start_ref[x] + stat_smem[x, 2]
        n_chunks = lax.shift_right_logical(count + (DISPATCH_CHUNK - 1), DISPATCH_SHIFT)

        def one(c, carry):
            _chunk_copy(staging.at[cur], src + c * DISPATCH_CHUNK, xs_out, dst + c * DISPATCH_CHUNK, sem).start()
            return carry

        lax.fori_loop(0, n_chunks, one, 0)
        return total + n_chunks

    issued = lax.fori_loop(0, N_EXPERTS, per_expert, 0)
    pending[0] = issued

    @pl.when(i == last)
    def _():
        drain(issued)


def _zero_tail_body(lb_ref, o_ref):
    del lb_ref
    o_ref[...] = jnp.zeros_like(o_ref)


def _zero_tails(last_block, n_blocks):
    rows = MOE_BLOCK * ROW_SUB
    grid_spec = pltpu.PrefetchScalarGridSpec(
        num_scalar_prefetch=1,
        grid=(last_block.shape[0],),
        in_specs=[],
        out_specs=pl.BlockSpec((rows, ROW_LANES), lambda e, lb: (lb[e], 0)),
    )
    return pl.pallas_call(
        _zero_tail_body,
        grid_spec=grid_spec,
        out_shape=jax.ShapeDtypeStruct((n_blocks * rows, ROW_LANES), U32),
        compiler_params=_cparams("arbitrary"),
        name="zero_tails",
    )(last_block)


def _dispatch(row_start, e3, pos3, loc3, stats, h2, xs_zero):
    n, d = h2.shape
    tn = e3.shape[2]
    tile = lambda: pl.BlockSpec((1, SLOT_ROWS, tn), lambda i, st: (i, 0, 0))
    grid_spec = pltpu.PrefetchScalarGridSpec(
        num_scalar_prefetch=1,
        grid=(n // tn,),
        in_specs=[tile(), tile(), tile(), pl.BlockSpec((1, N_EXPERTS, 128), lambda i, st: (i, 0, 0)),
                  pl.BlockSpec((tn, d), lambda i, st: (i, 0)), pl.BlockSpec(memory_space=pl.ANY)],
        out_specs=[tile(), pl.BlockSpec(memory_space=pl.ANY)],
        scratch_shapes=[pltpu.SMEM((N_EXPERTS, 128), I32), pltpu.SMEM((1,), I32),
                        pltpu.VMEM((2, (TOP_K * tn + DISPATCH_CHUNK) * ROW_SUB, ROW_LANES), U32),
                        pltpu.SemaphoreType.DMA, pltpu.SemaphoreType.DMA],
    )
    return pl.pallas_call(
        _dispatch_body,
        grid_spec=grid_spec,
        out_shape=[jax.ShapeDtypeStruct((n // tn, SLOT_ROWS, tn), I32),
                   jax.ShapeDtypeStruct(xs_zero.shape, xs_zero.dtype)],
        input_output_aliases={6: 1},
        compiler_params=_cparams("arbitrary"),
        name="dispatch",
    )(row_start, e3, pos3, loc3, stats, h2, xs_zero)


def _swiglu(x, wg, wu, wd):
    g = _dg(x, wg, _NN)
    u = _dg(x, wu, _NN)
    return _dg((g * _sigmoid(g) * u).astype(BF16), wd, _NN)


def _expert_body(be_ref, nu_ref, xs_ref, wg_ref, wu_ref, wd_ref, ys_ref):
    del be_ref
    b = pl.program_id(0)

    @pl.when(b < nu_ref[0])
    def _():
        x = _from_row_tiles(xs_ref, MOE_BLOCK).astype(BF16)
        _to_row_tiles(ys_ref, _swiglu(x, wg_ref[0], wu_ref[0], wd_ref[0]))

    @pl.when(b >= nu_ref[0])
    def _():
        ys_ref[...] = jnp.zeros_like(ys_ref)


def _experts(block_e, n_used, xs, wg, wu, wd):
    bm = MOE_BLOCK
    _, d, de = wg.shape
    rows = bm * ROW_SUB
    grid_spec = pltpu.PrefetchScalarGridSpec(
        num_scalar_prefetch=2,
        grid=(xs.shape[0] // rows,),
        in_specs=[
            pl.BlockSpec((rows, ROW_LANES), lambda b, be, nu: (jnp.minimum(b, nu[0] - 1), 0)),
            pl.BlockSpec((1, d, de), lambda b, be, nu: (be[b], 0, 0)),
            pl.BlockSpec((1, d, de), lambda b, be, nu: (be[b], 0, 0)),
            pl.BlockSpec((1, de, d), lambda b, be, nu: (be[b], 0, 0)),
        ],
        out_specs=pl.BlockSpec((rows, ROW_LANES), lambda b, be, nu: (b, 0)),
    )
    return pl.pallas_call(
        _expert_body,
        grid_spec=grid_spec,
        out_shape=jax.ShapeDtypeStruct(xs.shape, xs.dtype),
        compiler_params=_cparams("arbitrary"),
        name="experts",
    )(block_e, n_used, xs, wg, wu, wd)


def _combine_body(slot_ref, w_ref, x1_ref, h2_ref, gate_ref, ys_hbm, wsg_ref, wsu_ref, wsd_ref, gpost_ref,
                  out_ref, slot_smem, buf, sem_s, sem):
    tn = slot_smem.shape[1]
    _load_slots(slot_ref, slot_smem, sem_s)

    def issue(j, carry):
        for k in range(TOP_K):
            _row_copy(ys_hbm, slot_smem[k, j], buf.at[k], j * ROW_SUB, sem).start(priority=k % 2)
        return carry

    def drain(j, carry):
        for k in range(TOP_K):
            _row_copy(ys_hbm, 0, buf.at[k], 0, sem).wait()
        return carry

    lax.fori_loop(0, tn, issue, 0)
    acc = _swiglu(h2_ref[...].astype(BF16), wsg_ref[...], wsu_ref[...], wsd_ref[...])
    lax.fori_loop(0, tn, drain, 0)
    for k in range(TOP_K):
        acc = acc + _from_row_tiles(buf.at[k], tn) * w_ref[:, k:k + 1]
    out_ref[...] = x1_ref[...] + gate_ref[0] * _rms(acc, gpost_ref[...])


def _combine(slots3, w_tok, x1, h2, gate_rows, ys, wsg, wsu, wsd, gpost):
    ntiles, _, tn = slots3.shape
    n, d = x1.shape
    tok = pl.BlockSpec((tn, d), lambda i: (i, 0))
    return pl.pallas_call(
        _combine_body,
        grid=(ntiles,),
        in_specs=[pl.BlockSpec((1, SLOT_ROWS, tn), lambda i: (i, 0, 0)),
                  pl.BlockSpec((tn, SLOT_ROWS), lambda i: (i, 0)), tok, tok,
                  pl.BlockSpec((1, 1, d), lambda i: (i, 0, 0)),
                  pl.BlockSpec(memory_space=pl.ANY),
                  _full(wsg.shape), _full(wsu.shape), _full(wsd.shape), _full(gpost.shape)],
        out_specs=tok,
        out_shape=jax.ShapeDtypeStruct((n, d), F32),
        scratch_shapes=[pltpu.SMEM((SLOT_ROWS, tn), I32), pltpu.VMEM((TOP_K, tn * ROW_SUB, ROW_LANES), U32),
                        pltpu.SemaphoreType.DMA, pltpu.SemaphoreType.DMA],
        compiler_params=_cparams("arbitrary"),
        name="combine",
    )(slots3, w_tok, x1, h2, gate_rows, ys, wsg, wsu, wsd, gpost)


def _moe(x1, h2, gate_rows, lp):
    n, d = h2.shape
    bm = MOE_BLOCK
    tn = TOKEN_TILE
    e3, pos3, loc3, stats, w_rows, cnt = _router(h2, lp['wr_t'], lp['b_router_col'])
    counts = cnt[:, 0].astype(I32)
    blocks_per_e = (counts + DISPATCH_CHUNK + bm - 1) // bm
    blk_end = jnp.cumsum(blocks_per_e)
    blk_start = blk_end - blocks_per_e
    n_blocks = -(-(n * TOP_K + N_EXPERTS * DISPATCH_CHUNK) // bm) + N_EXPERTS
    block_e = jnp.minimum(jnp.sum(blk_end[None, :] <= jnp.arange(n_blocks, dtype=I32)[:, None], axis=1),
                          N_EXPERTS - 1).astype(I32)
    n_used = blk_end[-1:].astype(I32)
    partial = jnp.stack([blk_start + counts // bm, blk_end - 1], axis=1).reshape(-1)
    xs_tails = _zero_tails(jnp.clip(partial, 0, n_blocks - 1).astype(I32), n_blocks)
    slots3, xs = _dispatch((blk_start * bm).astype(I32), e3, pos3, loc3, stats, h2, xs_tails)
    ys = _experts(block_e, n_used, xs, lp['w_e_gate'], lp['w_e_up'], lp['w_e_down'])
    return _combine(slots3, w_rows.T, x1, h2, gate_rows, ys, lp['w_s_gate'], lp['w_s_up'], lp['w_s_down'],
                    lp['g_post_ffn'])


def _rope_tables(n_tokens):
    rows = n_tokens // GRID_W
    row = jnp.repeat(jnp.arange(rows, dtype=F32), GRID_W)
    col = jnp.tile(jnp.arange(GRID_W, dtype=F32), rows)
    n_freq = ROPE_DIM // 4
    inv_freq = ROPE_THETA ** (-jnp.arange(n_freq, dtype=F32) / n_freq)
    ang_r = row[:, None] * inv_freq
    ang_c = col[:, None] * inv_freq
    ang = jnp.concatenate([ang_r, ang_r, ang_c, ang_c], axis=-1)
    cos, sin = jnp.cos(ang), jnp.sin(ang)
    pad = lambda z, fill: jnp.concatenate(
        [jnp.full((n_tokens, NOPE_DIM), fill, F32), z, jnp.zeros((n_tokens, HEAD_PAD - QK_DIM), F32)], axis=1)
    return pad(cos, 1.0), pad(sin, 0.0), pad(cos, 0.0), pad(sin, 0.0)


def _rot_cols(w):
    q = ROPE_DIM // 4
    perm = np.concatenate([np.arange(q, 2 * q), np.arange(0, q), np.arange(3 * q, 4 * q), np.arange(2 * q, 3 * q)])
    sign = np.concatenate([-np.ones(q), np.ones(q), -np.ones(q), np.ones(q)]).astype(np.float32)
    return w[..., perm] * sign


def _block_diag(blocks):
    h, m, n = blocks.shape[-3:]
    eye = jnp.eye(h, dtype=blocks.dtype)
    out = blocks[..., :, :, None, :] * eye[:, None, :, None]
    return out.reshape(blocks.shape[:-3] + (h * m, h * n))


def _diag_blocks(mat, h):
    m, n = mat.shape[-2] // h, mat.shape[-1] // h
    z = mat.reshape(mat.shape[:-2] + (h, m, h, n))
    return jnp.stack([z[..., i, :, i, :] for i in range(h)], axis=-3)


def _layer_operands(l, a):
    w_in = a['w_in'][l]
    d = w_in.shape[0]
    p_mla = Q_LORA + KV_LORA + ROPE_DIM
    w_kr = w_in[:, Q_LORA + KV_LORA:p_mla]
    z32 = jnp.zeros((d, ROPE_DIM), F32)
    w_mla = jnp.concatenate([w_in[:, :Q_LORA + KV_LORA], z32, _rot_cols(w_kr), w_kr, z32], axis=1)
    wq3 = a['w_q_b'][l].reshape(Q_LORA, H_MLA, QK_DIM)
    zq = lambda n: jnp.zeros((Q_LORA, H_MLA, n), F32)
    wq = jnp.concatenate([wq3, zq(HEAD_PAD - QK_DIM)], axis=2).reshape(Q_LORA, H_MLA * HEAD_PAD)
    wqr = jnp.concatenate([zq(NOPE_DIM), _rot_cols(wq3[:, :, NOPE_DIM:]), zq(HEAD_PAD - QK_DIM)],
                          axis=2).reshape(Q_LORA, H_MLA * HEAD_PAD)
    wkv3 = a['w_kv_b'][l].reshape(KV_LORA, H_MLA, NOPE_DIM + V_DIM)
    wk = jnp.concatenate([wkv3[:, :, :NOPE_DIM], jnp.zeros((KV_LORA, H_MLA, HEAD_PAD - NOPE_DIM), F32)],
                         axis=2).reshape(KV_LORA, H_MLA * HEAD_PAD)
    wv = wkv3[:, :, NOPE_DIM:].reshape(KV_LORA, H_MLA * V_DIM)
    row = lambda z: z.reshape(1, -1)
    two = lambda z: jnp.concatenate([z[0], z[1]], axis=-1)
    bd2 = lambda z: jnp.concatenate([jnp.concatenate([z[0], jnp.zeros_like(z[0])], axis=1),
                                     jnp.concatenate([jnp.zeros_like(z[1]), z[1]], axis=1)], axis=0)
    return {
        'g_pre_mix': row(a['g_pre_mix'][l]), 'g_post_mix': row(a['g_post_mix'][l]),
        'g_pre_ffn': row(a['g_pre_ffn'][l]), 'g_post_ffn': row(a['g_post_ffn'][l]),
        'w_mla': w_mla.astype(BF16),
        'w_rwkv': w_in[:, p_mla:p_mla + P_RWKV].astype(BF16),
        'w_lru': w_in[:, p_mla + P_RWKV:].astype(BF16),
        'g_q_a': row(a['g_q_a'][l]), 'g_kv_a': row(a['g_kv_a'][l]),
        'wq': wq.astype(BF16), 'wqr': wqr.astype(BF16), 'wk': wk.astype(BF16), 'wv': wv.astype(BF16),
        'g_mla_out': row(a['g_mla_out'][l]),
        'mu_rwkv': row(a['mu_rwkv'][l]), 'k_k': row(a['k_k'][l]), 'k_a': row(a['k_a'][l]), 'r_k': row(a['r_k'][l]),
        'w0_cat': row(two(a['w0_rwkv'][l])), 'w2_bd': bd2(a['w2_rwkv'][l]),
        'a0_cat': row(two(a['a0_rwkv'][l])), 'a2_bd': bd2(a['a2_rwkv'][l]),
        'g2_rwkv': a['g2_rwkv'][l], 'ln_x_w': row(a['ln_x_w'][l]), 'ln_x_b': row(a['ln_x_b'][l]),
        'conv_w': a['conv_w'][l], 'conv_b': a['conv_b'][l],
        'wa_bd': _block_diag(a['w_lru_a'][l]), 'b_lru_a': a['b_lru_a'][l],
        'wx_bd': _block_diag(a['w_lru_x'][l]), 'b_lru_x': a['b_lru_x'][l],
        'lam': a['lam'][l], 'g_lru_out': row(a['g_lru_out'][l]),
        'w_out': a['w_out'][l].astype(BF16),
        'wr_t': a['w_router'][l].T, 'b_router_col': a['b_router'][l].reshape(N_EXPERTS, 1),
        'w_e_gate': a['w_e_gate'][l].astype(BF16), 'w_e_up': a['w_e_up'][l].astype(BF16),
        'w_e_down': a['w_e_down'][l].astype(BF16),
        'w_s_gate': a['w_s_gate'][l].astype(BF16), 'w_s_up': a['w_s_up'][l].astype(BF16),
        'w_s_down': a['w_s_down'][l].astype(BF16),
    }


def _mixer(x, grp, mod, lp, ctx, tables, prev):
    _, nb, t = grp
    tt = min(t, 512)
    p_mla, p_rwkv, p_lru = _in_proj(x, grp, lp['g_pre_mix'], mod['sc1'], mod['sh1'], lp['w_mla'], lp['w_rwkv'],
                                    lp['w_lru'], tt)
    if ctx is None:
        q, k, v, new_cache = _mla_prep(p_mla, lp['g_q_a'], lp['g_kv_a'], lp['wq'], None, lp['wk'], lp['wv'], None, tt)
        kc = vc = None
        s0 = jnp.zeros((nb, 2, D_RWKV, D_RWKV), F32)
        h0 = jnp.zeros((nb, 2, 1, D_LRU), F32)
    else:
        cache, s_rwkv, s_lru = ctx
        q, k, v = _mla_prep(p_mla, lp['g_q_a'], lp['g_kv_a'], lp['wq'], lp['wqr'], lp['wk'], lp['wv'], tables, tt)
        new_cache = None
        kr_placed = jnp.pad(cache[..., KV_LORA:], ((0, 0), (0, 0), (NOPE_DIM, HEAD_PAD - QK_DIM)))
        kc, vc = _ctx_kv(cache[..., :KV_LORA], kr_placed, lp['wk'], lp['wv'])
        s0 = _block_diag(s_rwkv)
        h0 = s_lru[:, :, None, :]
    o = _attention(q, k, v, kc, vc, min(t, 256))

    r, vv, kk, lw, ag, kd, g, bonus = _rwkv_prep(p_rwkv, lp, tt)
    y_f, y_b, s_fin = _rwkv_scan(r, vv, kk, lw, ag, kd, s0)

    hp, hn = _halo_rows(p_lru[..., :D_LRU], tt, SUBLANES)
    h_f, fin_f = _lru_dir(p_lru, hp, lp, 0, h0[:, 0], tt)
    h_b, fin_b = _lru_dir(p_lru, hn, lp, 1, h0[:, 1], tt)

    x1, h2 = _out_proj(x, grp, o, y_f, y_b, bonus, g, h_f, h_b, p_lru, lp, mod, tt, prev)
    states = None
    if ctx is None:
        states = (new_cache, _diag_blocks(s_fin, H_RWKV), jnp.concatenate([fin_f, fin_b], axis=1))
    return x1, h2, states


def kernel(x_prompt, x_sample, c, cache_mla, state_rwkv, state_lru, c_ctx, w_ada, b_ada, g_pre_mix, g_post_mix, g_pre_ffn, g_post_ffn, w_in, g_q_a, w_q_b, g_kv_a, w_kv_b, g_mla_out, mu_rwkv, w0_rwkv, w2_rwkv, a0_rwkv, a2_rwkv, g2_rwkv, k_k, k_a, r_k, ln_x_w, ln_x_b, conv_w, conv_b, w_lru_a, b_lru_a, w_lru_x, b_lru_x, lam, g_lru_out, w_out, w_router, b_router, w_e_gate, w_e_up, w_e_down, w_s_gate, w_s_up, w_s_down):
    a = dict(w_in=w_in, g_pre_mix=g_pre_mix, g_post_mix=g_post_mix, g_pre_ffn=g_pre_ffn, g_post_ffn=g_post_ffn,
             g_q_a=g_q_a, w_q_b=w_q_b, g_kv_a=g_kv_a, w_kv_b=w_kv_b, g_mla_out=g_mla_out, mu_rwkv=mu_rwkv,
             w0_rwkv=w0_rwkv, w2_rwkv=w2_rwkv, a0_rwkv=a0_rwkv, a2_rwkv=a2_rwkv, g2_rwkv=g2_rwkv, k_k=k_k, k_a=k_a,
             r_k=r_k.reshape(r_k.shape[0], -1), ln_x_w=ln_x_w, ln_x_b=ln_x_b, conv_w=conv_w, conv_b=conv_b,
             w_lru_a=w_lru_a, b_lru_a=b_lru_a, w_lru_x=w_lru_x, b_lru_x=b_lru_x, lam=lam, g_lru_out=g_lru_out,
             w_out=w_out, w_router=w_router, b_router=b_router, w_e_gate=w_e_gate, w_e_up=w_e_up,
             w_e_down=w_e_down, w_s_gate=w_s_gate, w_s_up=w_s_up, w_s_down=w_s_down)
    n_layers = w_in.shape[0]
    nbp, tp, d = x_prompt.shape
    nbs, ts, _ = x_sample.shape
    rows = -(-(1 + nbs) // SUBLANES) * SUBLANES
    cvecs = jnp.concatenate([c_ctx[None, :], c, jnp.zeros((rows - 1 - nbs, d), F32)], axis=0)
    mods = _ada(cvecs, w_ada, b_ada)
    names = ('sh1', 'sc1', 'g1', 'sh2', 'sc2', 'g2')
    tables = _rope_tables(ts)
    tn = TOKEN_TILE

    grp_p, grp_s = (0, nbp, tp), (nbp * tp, nbs, ts)
    x = jnp.concatenate([x_prompt.reshape(-1, d), x_sample.reshape(-1, d)], axis=0)
    caches, rwkv_states, lru_states = [], [], []
    for l in range(n_layers):
        lp = _layer_operands(l, a)
        mod_p = {nm: mods[l, 0:1, i * d:(i + 1) * d][:, None, :] for i, nm in enumerate(names)}
        mod_s = {nm: mods[l, 1:1 + nbs, i * d:(i + 1) * d][:, None, :] for i, nm in enumerate(names)}
        x1, h2, states = _mixer(x, grp_p, mod_p, lp, None, None, None)
        x1, h2, _ = _mixer(x, grp_s, mod_s, lp, (cache_mla[:, l], state_rwkv[:, l], state_lru[:, l]), tables,
                           (x1, h2))
        caches.append(states[0])
        rwkv_states.append(states[1])
        lru_states.append(states[2])
        gate_rows = jnp.concatenate([jnp.repeat(mod_p['g2'], nbp * tp // tn, axis=0),
                                     jnp.repeat(mod_s['g2'], ts // tn, axis=0)], axis=0)
        x = _moe(x1, h2, gate_rows, lp)
    yp = x[:nbp * tp].reshape(nbp, tp, d)
    ys = x[nbp * tp:].reshape(nbs, ts, d)
    return (yp, ys, jnp.stack(caches, axis=1), jnp.stack(rwkv_states, axis=1), jnp.stack(lru_states, axis=1))
```

```python
import functools

import numpy as np
import jax
import jax.numpy as jnp
from jax import lax
from jax.experimental import pallas as pl
from jax.experimental.pallas import tpu as pltpu

F32 = jnp.float32
BF16 = jnp.bfloat16
I32 = jnp.int32

D_MODEL = 1024
GRID_W = 64
EPS = 1e-6
H_MLA = 4
Q_LORA = 256
KV_LORA = 128
NOPE_DIM = 64
ROPE_DIM = 32
V_DIM = 128
QK_DIM = NOPE_DIM + ROPE_DIM
ROPE_THETA = 10000.0
HEAD_PAD = 128
H_RWKV = 4
N_RWKV = 64
D_RWKV = H_RWKV * N_RWKV
W_LORA = 64
A_LORA = 64
G_LORA = 128
GN_EPS = 64e-5
D_LRU = 256
H_LRU = 4
BS_LRU = D_LRU // H_LRU
CONV_W = 4
LRU_C = 8.0
D_MLA_OUT = H_MLA * V_DIM
P_MLA_PAD = Q_LORA + KV_LORA + HEAD_PAD
P_RWKV = 3 * D_RWKV + 2 * W_LORA + 2 * A_LORA + G_LORA
P_LRU = 2 * D_LRU
N_EXPERTS = 64
TOP_K = 6
N_GROUPS = 8
TOPK_GROUPS = 4
GROUP_SIZE = N_EXPERTS // N_GROUPS
D_EXPERT = 256
ROUTED_SCALE = 2.5

SUBLANES = 8
RWKV_CHUNK = 64
RWKV_ROWS = 2
MOE_BLOCK = 512
ROUTER_TILE = 256
DISPATCH_SHIFT = 5
DISPATCH_CHUNK = 1 << DISPATCH_SHIFT
TOKEN_TILE = 256
COMBINE_ROWS = TOP_K * TOKEN_TILE + N_EXPERTS * DISPATCH_CHUNK
COMBINE_PIECES = 4
SLOT_ROWS = 8
VMEM_LIMIT = 56 * 1024 * 1024


def _cparams(*sem):
    return pltpu.CompilerParams(dimension_semantics=sem, vmem_limit_bytes=VMEM_LIMIT)


def _sigmoid(x):
    return 1.0 / (1.0 + jnp.exp(-x))


def _softplus(x):
    return jnp.maximum(x, 0.0) + jnp.log1p(jnp.exp(-jnp.abs(x)))


def _expm1(z):
    e = jnp.exp(z)
    direct = (e == 1.0) | (z < -1.0)
    corrected = (e - 1.0) * z / jnp.log(jnp.where(direct, 2.0, e))
    return jnp.where(e == 1.0, z, jnp.where(z < -1.0, e - 1.0, corrected))


def _rms(x, g):
    return x * lax.rsqrt(jnp.mean(x * x, axis=-1, keepdims=True) + EPS) * g


def _split2(x):
    hi = x.astype(BF16)
    lo = (x - hi.astype(F32)).astype(BF16)
    return hi, lo


def _split3(x):
    hi = x.astype(BF16)
    r1 = x - hi.astype(F32)
    mid = r1.astype(BF16)
    lo = (r1 - mid.astype(F32)).astype(BF16)
    return hi, mid, lo


_NN = (((1,), (0,)), ((), ()))
_NT = (((1,), (1,)), ((), ()))


def _dg(a, b, dims):
    return lax.dot_general(a, b, dims, preferred_element_type=F32)


def _dot1(a, b, dims=_NN):
    return _dg(a.astype(BF16), b.astype(BF16), dims)


def _dot3(a, b, dims=_NN):
    ah, al = _split2(a)
    bh, bl = _split2(b)
    return _dg(ah, bh, dims) + (_dg(ah, bl, dims) + _dg(al, bh, dims))


def _dot_x(a, b_exact, dims=_NN):
    h, m, l = _split3(a)
    bb = b_exact.astype(BF16)
    return _dg(h, bb, dims) + (_dg(m, bb, dims) + _dg(l, bb, dims))


def _x_dot(a_exact, b, dims=_NN):
    h, m, l = _split3(b)
    aa = a_exact.astype(BF16)
    return _dg(aa, h, dims) + (_dg(aa, m, dims) + _dg(aa, l, dims))


def _head_ones(n, seg):
    r = lax.broadcasted_iota(I32, (n, n), 0) // seg
    c = lax.broadcasted_iota(I32, (n, n), 1) // seg
    return (r == c).astype(F32)


def _ada_body(c_ref, w_ref, b_ref, o_ref):
    cv = c_ref[...]
    s = cv * _sigmoid(cv)
    o_ref[0] = _dot3(s, w_ref[0]) + b_ref[0]


def _ada(cvecs, w_ada, b_ada):
    n_layers, d, d6 = w_ada.shape
    rows = cvecs.shape[0]
    nt = d6 // d
    return pl.pallas_call(
        _ada_body,
        grid=(n_layers, nt),
        in_specs=[
            pl.BlockSpec((rows, d), lambda l, j: (0, 0)),
            pl.BlockSpec((1, d, d), lambda l, j: (l, 0, j)),
            pl.BlockSpec((1, 1, d), lambda l, j: (l, 0, j)),
        ],
        out_specs=pl.BlockSpec((1, rows, d), lambda l, j: (l, 0, j)),
        out_shape=jax.ShapeDtypeStruct((n_layers, rows, d6), F32),
        compiler_params=_cparams("parallel", "parallel"),
        name="ada",
    )(cvecs, w_ada, b_ada.reshape(n_layers, 1, d6))


def _mod_spec(per_batch):
    if per_batch:
        return pl.BlockSpec((1, 1, D_MODEL), lambda b, i: (b, 0, 0))
    return pl.BlockSpec((1, 1, D_MODEL), lambda b, i: (0, 0, 0))


def _full(shape):
    nd = len(shape)
    return pl.BlockSpec(shape, lambda *_: (0,) * nd)


def _in_proj_body(x_ref, g_ref, sc_ref, sh_ref, w1_ref, w2_ref, w3_ref, o1_ref, o2_ref, o3_ref):
    h = _rms(x_ref[...], g_ref[...]) * (1.0 + sc_ref[0]) + sh_ref[0]
    hb = h.astype(BF16)
    o1_ref[0] = _dg(hb, w1_ref[...], _NN)
    o2_ref[0] = _dg(hb, w2_ref[...], _NN)
    o3_ref[0] = _dg(hb, w3_ref[...], _NN)


def _group_rows(grp, tt):
    row_off, nb, t = grp
    base, per = row_off // tt, t // tt
    return pl.BlockSpec((tt, D_MODEL), lambda b, i: (base + b * per + i, 0))


def _in_proj(x, grp, g, sc, sh, w_mla, w_rwkv, w_lru, tt):
    _, nb, t = grp
    d = x.shape[1]
    per_batch = sc.shape[0] > 1
    outs = [w_mla.shape[1], w_rwkv.shape[1], w_lru.shape[1]]
    return pl.pallas_call(
        _in_proj_body,
        grid=(nb, t // tt),
        in_specs=[
            _group_rows(grp, tt),
            _full((1, d)),
            _mod_spec(per_batch),
            _mod_spec(per_batch),
            _full(w_mla.shape),
            _full(w_rwkv.shape),
            _full(w_lru.shape),
        ],
        out_specs=[pl.BlockSpec((1, tt, n), lambda b, i: (b, i, 0)) for n in outs],
        out_shape=[jax.ShapeDtypeStruct((nb, t, n), F32) for n in outs],
        compiler_params=_cparams("parallel", "parallel"),
        name="in_proj",
    )(x, g, sc, sh, w_mla, w_rwkv, w_lru)


def _rope_lanes():
    lane = lax.broadcasted_iota(I32, (1, HEAD_PAD), 1)
    return ((lane >= NOPE_DIM) & (lane < QK_DIM)).astype(F32)


def _mla_prep_body(rope, *refs):
    if rope:
        (p_ref, gq_ref, gkv_ref, wq_ref, wqr_ref, wk_ref, wv_ref, cq_ref, sq_ref, cr_ref, sr_ref,
         q_ref, k_ref, v_ref) = refs
    else:
        p_ref, gq_ref, gkv_ref, wq_ref, wk_ref, wv_ref, q_ref, k_ref, v_ref, c_ref = refs
    p = p_ref[0]
    cq = _rms(p[:, :Q_LORA], gq_ref[...])
    ckv = _rms(p[:, Q_LORA:Q_LORA + KV_LORA], gkv_ref[...])
    kr = p[:, Q_LORA + KV_LORA:]
    cqb = cq.astype(BF16)
    ckvb = ckv.astype(BF16)
    scale = QK_DIM ** -0.5 * float(np.log2(np.e))
    qa = _dg(cqb, wq_ref[...], _NN) * scale
    ka = _dg(ckvb, wk_ref[...], _NN)
    va = _dg(ckvb, wv_ref[...], _NN)
    if rope:
        qr = _dg(cqb, wqr_ref[...], _NN) * scale
        krp = kr * cr_ref[...] + pltpu.roll(kr, ROPE_DIM, 1) * sr_ref[...]
    else:
        krp = kr * _rope_lanes()
    for h in range(H_MLA):
        sl = slice(h * HEAD_PAD, (h + 1) * HEAD_PAD)
        qh = qa[:, sl]
        if rope:
            qh = qh * cq_ref[...] + qr[:, sl] * sq_ref[...]
        q_ref[0, h] = qh.astype(BF16)
        k_ref[0, h] = (ka[:, sl] + krp).astype(BF16)
        v_ref[0, h] = va[:, sl].astype(BF16)
    if not rope:
        c_ref[0, :, :KV_LORA] = ckv
        c_ref[0, :, KV_LORA:] = pltpu.roll(kr, HEAD_PAD - NOPE_DIM, 1)[:, :ROPE_DIM]


def _mla_prep(p_mla, gq, gkv, wq, wqr, wk, wv, tables, tt):
    nb, t, pw = p_mla.shape
    rope = tables is not None
    hw = H_MLA * HEAD_PAD
    head_spec = pl.BlockSpec((1, H_MLA, tt, HEAD_PAD), lambda b, i: (b, 0, i, 0))
    head_shape = jax.ShapeDtypeStruct((nb, H_MLA, t, HEAD_PAD), BF16)
    in_specs = [pl.BlockSpec((1, tt, pw), lambda b, i: (b, i, 0)), _full(gq.shape), _full(gkv.shape), _full(wq.shape)]
    args = [p_mla, gq, gkv, wq]
    if rope:
        in_specs.append(_full(wqr.shape))
        args.append(wqr)
    in_specs += [_full(wk.shape), _full(wv.shape)]
    args += [wk, wv]
    out_specs = [head_spec, head_spec, head_spec]
    out_shape = [head_shape, head_shape, head_shape]
    if rope:
        in_specs += [pl.BlockSpec((tt, HEAD_PAD), lambda b, i: (i, 0))] * 4
        args += list(tables)
    else:
        out_specs.append(pl.BlockSpec((1, tt, KV_LORA + ROPE_DIM), lambda b, i: (b, i, 0)))
        out_shape.append(jax.ShapeDtypeStruct((nb, t, KV_LORA + ROPE_DIM), F32))
    del hw
    return pl.pallas_call(
        functools.partial(_mla_prep_body, rope),
        grid=(nb, t // tt),
        in_specs=in_specs,
        out_specs=out_specs,
        out_shape=out_shape,
        compiler_params=_cparams("parallel", "parallel"),
        name="mla_prep_rope" if rope else "mla_prep",
    )(*args)


def _ctx_kv_body(lat_ref, kr_ref, wk_ref, wv_ref, k_ref, v_ref):
    latb = lat_ref[0].astype(BF16)
    ka = _dg(latb, wk_ref[...], _NN)
    va = _dg(latb, wv_ref[...], _NN)
    kr = kr_ref[0]
    for h in range(H_MLA):
        sl = slice(h * HEAD_PAD, (h + 1) * HEAD_PAD)
        k_ref[0, h] = (ka[:, sl] + kr).astype(BF16)
        v_ref[0, h] = va[:, sl].astype(BF16)


def _ctx_kv(lat, kr_placed, wk, wv):
    nb, s, _ = lat.shape
    head_spec = pl.BlockSpec((1, H_MLA, s, HEAD_PAD), lambda b: (b, 0, 0, 0))
    head_shape = jax.ShapeDtypeStruct((nb, H_MLA, s, HEAD_PAD), BF16)
    return pl.pallas_call(
        _ctx_kv_body,
        grid=(nb,),
        in_specs=[
            pl.BlockSpec((1, s, KV_LORA), lambda b: (b, 0, 0)),
            pl.BlockSpec((1, s, HEAD_PAD), lambda b: (b, 0, 0)),
            _full(wk.shape),
            _full(wv.shape),
        ],
        out_specs=[head_spec, head_spec],
        out_shape=[head_shape, head_shape],
        compiler_params=_cparams("parallel"),
        name="ctx_kv",
    )(lat, kr_placed, wk, wv)


def _attn_body(has_ctx, *refs):
    if has_ctx:
        q_ref, k_ref, v_ref, kc_ref, vc_ref, o_ref = refs
    else:
        q_ref, k_ref, v_ref, o_ref = refs
    q = q_ref[0, 0]
    s = _dg(q, k_ref[0, 0], _NT)
    m = jnp.max(s, axis=-1, keepdims=True)
    if has_ctx:
        sc = _dg(q, kc_ref[0, 0], _NT)
        m = jnp.maximum(m, jnp.max(sc, axis=-1, keepdims=True))
    p = jnp.exp2(s - m)
    l = jnp.sum(p, axis=-1, keepdims=True)
    o = _dg(p.astype(BF16), v_ref[0, 0], _NN)
    if has_ctx:
        pc = jnp.exp2(sc - m)
        l = l + jnp.sum(pc, axis=-1, keepdims=True)
        o = o + _dg(pc.astype(BF16), vc_ref[0, 0], _NN)
    o_ref[0] = o / l


def _attention(q, k, v, kc, vc, tq):
    nb, nh, t, hd = q.shape
    s = k.shape[2]
    has_ctx = kc is not None
    in_specs = [
        pl.BlockSpec((1, 1, tq, hd), lambda b, h, i: (b, h, i, 0)),
        pl.BlockSpec((1, 1, s, hd), lambda b, h, i: (b, h, 0, 0)),
        pl.BlockSpec((1, 1, s, V_DIM), lambda b, h, i: (b, h, 0, 0)),
    ]
    args = [q, k, v]
    if has_ctx:
        sc = kc.shape[2]
        in_specs += [
            pl.BlockSpec((1, 1, sc, hd), lambda b, h, i: (b, h, 0, 0)),
            pl.BlockSpec((1, 1, sc, V_DIM), lambda b, h, i: (b, h, 0, 0)),
        ]
        args += [kc, vc]
    return pl.pallas_call(
        functools.partial(_attn_body, has_ctx),
        grid=(nb, nh, t // tq),
        in_specs=in_specs,
        out_specs=pl.BlockSpec((1, tq, V_DIM), lambda b, h, i: (b, i, h)),
        out_shape=jax.ShapeDtypeStruct((nb, t, nh * V_DIM), F32),
        compiler_params=_cparams("parallel", "parallel", "parallel"),
        name="attention",
    )(*args)


def _rwkv_prep_body(p_ref, hp_ref, hn_ref, mu_ref, kk_ref, ka_ref, rk_ref, w0_ref, w2_ref, a0_ref, a2_ref, g2_ref,
                    r_out, v_out, kk_out, lw_out, a_out, kd_out, g_out, bonus_out):
    p = p_ref[0]
    tt = p.shape[0]
    row = lax.broadcasted_iota(I32, (tt, 1), 0)
    prev = jnp.where(row == 0, hp_ref[0, 0], pltpu.roll(p, 1, 0))
    nxt = jnp.where(row == tt - 1, hn_ref[0, 0], pltpu.roll(p, tt - 1, 0))
    ps = p + (0.5 * (prev + nxt) - p) * mu_ref[...]
    r = ps[:, :D_RWKV]
    k = ps[:, D_RWKV:2 * D_RWKV]
    v = ps[:, 2 * D_RWKV:3 * D_RWKV]
    wl = ps[:, 3 * D_RWKV:3 * D_RWKV + 2 * W_LORA]
    al = ps[:, 3 * D_RWKV + 2 * W_LORA:3 * D_RWKV + 2 * W_LORA + 2 * A_LORA]
    gl = ps[:, 3 * D_RWKV + 2 * W_LORA + 2 * A_LORA:]
    ones = _head_ones(D_RWKV, N_RWKV)
    kk = k * kk_ref[...]
    kk = kk / jnp.maximum(jnp.sqrt(_dot_x(kk * kk, ones)), 1e-12)
    wlin = w0_ref[...] + _dot3(jnp.tanh(wl), w2_ref[...])
    lw = -jnp.exp(-_softplus(-wlin) - 0.5)
    ag = _sigmoid(a0_ref[...] + _dot3(al, a2_ref[...]))
    bonus = jnp.zeros_like(v)
    for d in range(2):
        sl = slice(d * D_RWKV, (d + 1) * D_RWKV)
        kd = k * (1.0 + (ag[:, sl] - 1.0) * ka_ref[...])
        kd_out[0, :, sl] = kd
        bonus = bonus + _dot_x(r * kd * rk_ref[...], ones) * v
    r_out[0] = r
    v_out[0] = v
    kk_out[0] = kk
    lw_out[0] = lw
    a_out[0] = ag
    g_out[0] = _dot1(_sigmoid(gl), g2_ref[...])
    bonus_out[0] = bonus


def _halo_rows(x, tt, width):
    nb, t, c = x.shape
    nt = t // tt
    xt = x.reshape(nb, nt, tt, c)
    zero = jnp.zeros((nb, 1, width, c), x.dtype)
    prev = jnp.concatenate([zero, xt[:, :-1, tt - width:]], axis=1)
    nxt = jnp.concatenate([xt[:, 1:, :width], zero], axis=1)
    return prev, nxt


def _rwkv_prep(p_rwkv, lp, tt):
    nb, t, pw = p_rwkv.shape
    hp, hn = _halo_rows(p_rwkv, tt, 1)
    tok = lambda n: pl.BlockSpec((1, tt, n), lambda b, i: (b, i, 0))
    halo = pl.BlockSpec((1, 1, 1, pw), lambda b, i: (b, i, 0, 0))
    small = [lp['mu_rwkv'], lp['k_k'], lp['k_a'], lp['r_k'], lp['w0_cat'], lp['w2_bd'], lp['a0_cat'], lp['a2_bd'],
             lp['g2_rwkv']]
    widths = [D_RWKV, D_RWKV, D_RWKV, 2 * D_RWKV, 2 * D_RWKV, 2 * D_RWKV, D_RWKV, D_RWKV]
    return pl.pallas_call(
        _rwkv_prep_body,
        grid=(nb, t // tt),
        in_specs=[tok(pw), halo, halo] + [_full(a.shape) for a in small],
        out_specs=[tok(n) for n in widths],
        out_shape=[jax.ShapeDtypeStruct((nb, t, n), F32) for n in widths],
        compiler_params=_cparams("parallel", "parallel"),
        name="rwkv_prep",
    )(p_rwkv, hp, hn, *small)


def _rwkv_chunk(reverse, r, v, kk, lw, ag, kd, s_old):
    cl = r.shape[0]
    hw = D_RWKV

    def later(rows, cols, strict):
        t = lax.broadcasted_iota(I32, (rows, cols), 0)
        s = lax.broadcasted_iota(I32, (rows, cols), 1) % cl
        dlt = (s - t) if reverse else (t - s)
        return (dlt > 0) if strict else (dlt >= 0)

    cum = _x_dot(later(cl, cl, False).astype(F32), lw)
    yield
    tot = jnp.sum(lw, axis=0, keepdims=True)
    e_neg = jnp.exp(-cum)
    e_rem = jnp.exp(tot - cum)
    at = -kk * jnp.exp(cum - lw)
    rt = r * jnp.exp(cum)
    b0 = kk * ag
    bt = b0 * e_neg
    kt = kd * e_neg
    bh = b0 * e_rem
    kh = kd * e_rem
    w_c = jnp.exp(tot)

    lane_head = lax.broadcasted_iota(I32, (1, hw), 1) // N_RWKV
    hms = [(lane_head == h).astype(F32) for h in range(H_RWKV)]
    heads = range(H_RWKV)
    bk = [jnp.concatenate([bt * hm, kt * hm], axis=0) for hm in hms]
    strict = later(cl, 2 * cl, True)
    incl = later(cl, 2 * cl, False)
    ls = [jnp.where(strict, _dot3(at, bk[h], _NT), 0.0) for h in heads]
    ly = [jnp.where(incl, _dot1(rt, bk[h], _NT), 0.0) for h in heads]
    yield

    x = [l[:, :cl] for l in ls]
    eye = (lax.broadcasted_iota(I32, (cl, cl), 0) == lax.broadcasted_iota(I32, (cl, cl), 1)).astype(F32)
    inv = [eye + xh for xh in x]
    span = 2
    while span < cl:
        x = [_dot3(xh, xh) for xh in x]
        yield
        inv = [ih + _dot3(ih, xh) for ih, xh in zip(inv, x)]
        yield
        span *= 2

    zero = jnp.zeros((cl, hw), F32)
    v_rows = jnp.concatenate([piece for hm in hms for piece in (zero, v * hm)], axis=0)
    rhs = _dot3(at, s_old, _NT) + _dot3(jnp.concatenate(ls, axis=1), v_rows)
    yield
    us = [_dot3(inv[h], rhs * hms[h]) for h in heads]
    yield
    um = us[0]
    for h in range(1, H_RWKV):
        um = um + us[h]
    uv_rows = jnp.concatenate([piece for h in heads for piece in (us[h], v * hms[h])], axis=0)
    y = _dot1(rt, s_old, _NT) + _dot1(jnp.concatenate(ly, axis=1), uv_rows)
    yield
    uv_t = jnp.concatenate([um, v], axis=0).T
    s_new = s_old * w_c + _dot3(uv_t, jnp.concatenate([bh, kh], axis=0)) * _head_ones(hw, N_RWKV)
    yield y, s_new


def _lockstep(gens):
    last = [None] * len(gens)
    live = list(range(len(gens)))
    while live:
        for i in list(live):
            try:
                last[i] = next(gens[i])
            except StopIteration:
                live.remove(i)
    return last


def _rwkv_scan_body(rf_ref, vf_ref, kkf_ref, lwf_ref, af_ref, kdf_ref, rb_ref, vb_ref, kkb_ref, lwb_ref, ab_ref,
                    kdb_ref, s0_ref, yf_ref, yb_ref, sf_ref, s_scr):
    c = pl.program_id(1)
    nc = pl.num_programs(1)

    @pl.when(c == 0)
    def _():
        s_scr[...] = s0_ref[...]

    nrow = rf_ref.shape[0]
    chains = []
    for i in range(nrow):
        chains.append(_rwkv_chunk(False, rf_ref[i], vf_ref[i], kkf_ref[i], lwf_ref[i], af_ref[i], kdf_ref[i],
                                  s_scr[i, 0]))
        chains.append(_rwkv_chunk(True, rb_ref[i], vb_ref[i], kkb_ref[i], lwb_ref[i], ab_ref[i], kdb_ref[i],
                                  s_scr[i, 1]))
    done = _lockstep(chains)
    for i in range(nrow):
        (y_f, s_f), (y_b, s_b) = done[2 * i], done[2 * i + 1]
        yf_ref[i] = y_f
        yb_ref[i] = y_b
        s_scr[i, 0] = s_f
        s_scr[i, 1] = s_b

    @pl.when(c == nc - 1)
    def _():
        sf_ref[...] = s_scr[...]


def _rwkv_scan(r, v, kk, lw, ag, kd, s0):
    nb, t, hw = r.shape
    cl = RWKV_CHUNK
    nr = RWKV_ROWS
    nc = t // cl
    fwd = lambda lane: pl.BlockSpec((nr, cl, hw), lambda b, c: (b, c, lane))
    bwd = lambda lane: pl.BlockSpec((nr, cl, hw), lambda b, c: (b, nc - 1 - c, lane))
    state = pl.BlockSpec((nr, 2, hw, hw), lambda b, c: (b, 0, 0, 0))
    y_shape = jax.ShapeDtypeStruct((nb, t, hw), F32)
    return pl.pallas_call(
        _rwkv_scan_body,
        grid=(nb // nr, nc),
        in_specs=[fwd(0)] * 6 + [bwd(0)] * 3 + [bwd(1)] * 3 + [state],
        out_specs=[fwd(0), bwd(0), state],
        out_shape=[y_shape, y_shape, jax.ShapeDtypeStruct((nb, 2, hw, hw), F32)],
        scratch_shapes=[pltpu.VMEM((nr, 2, hw, hw), F32)],
        compiler_params=_cparams("parallel", "arbitrary"),
        name="rwkv_scan",
    )(r, v, kk, lw, ag, kd, r, v, kk, lw, ag, kd, s0)


def _lru_body(reverse, p_ref, halo_ref, cw_ref, cb_ref, wa_ref, ba_ref, wx_ref, bx_ref, lam_ref, h0_ref,
              h_out, fin_out, a_scr, u_scr, carry_scr):
    i = pl.program_id(1)
    nt = pl.num_programs(1)
    tt = p_ref.shape[1]
    n_ext = tt + SUBLANES

    @pl.when(i == 0)
    def _():
        carry_scr[...] = jnp.broadcast_to(h0_ref[0], (SUBLANES, D_LRU))

    xb = p_ref[0, :, :D_LRU]
    halo = halo_ref[0, 0]
    xc = cb_ref[...] + jnp.zeros_like(xb)
    if reverse:
        xe = jnp.concatenate([xb, halo], axis=0)
        for k in range(CONV_W):
            sh = xb if k == 0 else pltpu.roll(xe, n_ext - k, 0)[:tt]
            xc = xc + cw_ref[k:k + 1, :] * sh
    else:
        xe = jnp.concatenate([halo, xb], axis=0)
        for k in range(CONV_W):
            s = CONV_W - 1 - k
            sh = xb if s == 0 else pltpu.roll(xe, s, 0)[SUBLANES:]
            xc = xc + cw_ref[k:k + 1, :] * sh
    gate_r = _sigmoid(_dot3(xc, wa_ref[...]) + ba_ref[...])
    gate_i = _sigmoid(_dot3(xc, wx_ref[...]) + bx_ref[...])
    log_a = -LRU_C * gate_r * _softplus(-lam_ref[...])
    a = jnp.exp(log_a)
    u = jnp.sqrt(-_expm1(2.0 * log_a)) * (gate_i * xc)

    rowm = lax.broadcasted_iota(I32, (tt, 1), 0) % SUBLANES
    for s in (1, 2, 4):
        if reverse:
            a_s = pltpu.roll(a, tt - s, 0)
            u_s = pltpu.roll(u, tt - s, 0)
            m = rowm < SUBLANES - s
        else:
            a_s = pltpu.roll(a, s, 0)
            u_s = pltpu.roll(u, s, 0)
            m = rowm >= s
        u = jnp.where(m, a * u_s + u, u)
        a = jnp.where(m, a * a_s, a)
    a_scr[...] = a
    u_scr[...] = u
    ng = tt // SUBLANES

    def group(g, carry):
        gi = (ng - 1 - g) if reverse else g
        r0 = pl.multiple_of(gi * SUBLANES, SUBLANES)
        hg = a_scr[pl.ds(r0, SUBLANES), :] * carry + u_scr[pl.ds(r0, SUBLANES), :]
        h_out[0, pl.ds(r0, SUBLANES), :] = hg
        edge = hg[0:1, :] if reverse else hg[SUBLANES - 1:SUBLANES, :]
        return jnp.broadcast_to(edge, (SUBLANES, D_LRU))

    carry = lax.fori_loop(0, ng, group, carry_scr[...])
    carry_scr[...] = carry

    @pl.when(i == nt - 1)
    def _():
        fin_out[0] = carry[0:1, :]


def _lru_dir(p_lru, halo, lp, d, h0, tt):
    nb, t, pw = p_lru.shape
    nt = t // tt
    reverse = d == 1
    tidx = (lambda i: nt - 1 - i) if reverse else (lambda i: i)
    small = [lp['conv_w'][d], lp['conv_b'][d:d + 1], lp['wa_bd'][d], lp['b_lru_a'][d:d + 1], lp['wx_bd'][d],
             lp['b_lru_x'][d:d + 1], lp['lam'][d:d + 1]]
    return pl.pallas_call(
        functools.partial(_lru_body, reverse),
        grid=(nb, nt),
        in_specs=[
            pl.BlockSpec((1, tt, pw), lambda b, i: (b, tidx(i), 0)),
            pl.BlockSpec((1, 1, SUBLANES, D_LRU), lambda b, i: (b, tidx(i), 0, 0)),
        ] + [_full(a.shape) for a in small] + [pl.BlockSpec((1, 1, D_LRU), lambda b, i: (b, 0, 0))],
        out_specs=[
            pl.BlockSpec((1, tt, D_LRU), lambda b, i: (b, tidx(i), 0)),
            pl.BlockSpec((1, 1, D_LRU), lambda b, i: (b, 0, 0)),
        ],
        out_shape=[jax.ShapeDtypeStruct((nb, t, D_LRU), F32), jax.ShapeDtypeStruct((nb, 1, D_LRU), F32)],
        scratch_shapes=[pltpu.VMEM((tt, D_LRU), F32), pltpu.VMEM((tt, D_LRU), F32), pltpu.VMEM((SUBLANES, D_LRU), F32)],
        compiler_params=_cparams("parallel", "arbitrary"),
        name="lru_bwd" if reverse else "lru_fwd",
    )(p_lru, halo, *small, h0)


def _gelu_tanh(x):
    return 0.5 * x * (1.0 + jnp.tanh(np.sqrt(2.0 / np.pi).astype(np.float32) * (x + 0.044715 * (x * x * x))))


def _out_proj_body(x_ref, o_ref, y0_ref, y1_ref, bonus_ref, g_ref, h0_ref, h1_ref, p_ref,
                   gmla_ref, lnw_ref, lnb_ref, glru_ref, wo_ref, gpost_ref, gate_ref, gpre_ref, sc_ref, sh_ref,
                   *rest):
    x1_out, h2_out = rest[-2:]
    o_mla = _rms(o_ref[0], gmla_ref[...])
    ones = _head_ones(D_RWKV, N_RWKV)
    y = y0_ref[0] + y1_ref[0]
    mu = _dot_x(y, ones) * (1.0 / N_RWKV)
    yc = y - mu
    var = _dot_x(yc * yc, ones) * (1.0 / N_RWKV)
    y = yc * lax.rsqrt(var + GN_EPS) * lnw_ref[...] + lnb_ref[...]
    o_rwkv = (y + bonus_ref[0]) * g_ref[0]
    o_lru = _rms((h0_ref[0] + h1_ref[0]) * _gelu_tanh(p_ref[0, :, D_LRU:]), glru_ref[...])
    m = (_dg(o_mla.astype(BF16), wo_ref[:D_MLA_OUT, :], _NN)
         + _dg(o_rwkv.astype(BF16), wo_ref[D_MLA_OUT:D_MLA_OUT + D_RWKV, :], _NN)
         + _dg(o_lru.astype(BF16), wo_ref[D_MLA_OUT + D_RWKV:, :], _NN))
    x1 = x_ref[...] + gate_ref[0] * _rms(m, gpost_ref[...])
    x1_out[...] = x1
    h2_out[...] = _rms(x1, gpre_ref[...]) * (1.0 + sc_ref[0]) + sh_ref[0]


def _out_proj(x, grp, o, y_f, y_b, bonus, g, h0, h1, p_lru, lp, mod, tt, prev):
    _, nb, t = grp
    per_batch = mod['g1'].shape[0] > 1
    tok = lambda n: pl.BlockSpec((1, tt, n), lambda b, i: (b, i, 0))
    rows = lambda: _group_rows(grp, tt)
    small = [lp['g_mla_out'], lp['ln_x_w'], lp['ln_x_b'], lp['g_lru_out'], lp['w_out'], lp['g_post_mix']]
    args = [x, o, y_f, y_b, bonus, g, h0, h1, p_lru, *small, mod['g1'], lp['g_pre_ffn'], mod['sc2'], mod['sh2']]
    in_specs = ([rows(), tok(D_MLA_OUT), tok(D_RWKV), tok(D_RWKV), tok(D_RWKV), tok(D_RWKV), tok(D_LRU), tok(D_LRU),
                 tok(P_LRU)] + [_full(a.shape) for a in small]
                + [_mod_spec(per_batch), _full(lp['g_pre_ffn'].shape), _mod_spec(per_batch), _mod_spec(per_batch)])
    aliases = {}
    if prev is not None:
        aliases = {len(args): 0, len(args) + 1: 1}
        args += list(prev)
        in_specs += [pl.BlockSpec(memory_space=pl.ANY)] * 2
    return pl.pallas_call(
        _out_proj_body,
        grid=(nb, t // tt),
        in_specs=in_specs,
        out_specs=[rows(), rows()],
        out_shape=[jax.ShapeDtypeStruct(x.shape, F32)] * 2,
        input_output_aliases=aliases,
        compiler_params=_cparams("parallel", "parallel"),
        name="out_proj",
    )(*args)


def _first_index(mask, idx, sentinel):
    return jnp.min(jnp.where(mask, idx, sentinel), axis=0, keepdims=True)


def _router_body(h_ref, wr_ref, br_ref, e_out, pos_out, loc_out, pad_out, stat_out, w_out, cnt_out, carry):
    i = pl.program_id(0)
    tn = h_ref.shape[0]

    @pl.when(i == 0)
    def _():
        carry[...] = jnp.zeros_like(carry)

    logits = lax.dot_general(wr_ref[...], h_ref[...], _NT, precision=lax.Precision.HIGHEST,
                             preferred_element_type=F32)
    scores = _sigmoid(logits)
    sel = scores + br_ref[...]

    i8 = lax.broadcasted_iota(I32, (GROUP_SIZE, tn), 0)
    blocks, gscore = [], []
    for g in range(N_GROUPS):
        blk = sel[g * GROUP_SIZE:(g + 1) * GROUP_SIZE, :]
        m1 = jnp.max(blk, axis=0, keepdims=True)
        f1 = _first_index(blk == m1, i8, GROUP_SIZE)
        m2 = jnp.max(jnp.where(i8 == f1, -jnp.inf, blk), axis=0, keepdims=True)
        blocks.append(blk)
        gscore.append(m1 + m2)
    masked = []
    for g in range(N_GROUPS):
        beaten = jnp.zeros((1, tn), I32)
        for o in range(N_GROUPS):
            if o == g:
                continue
            wins = (gscore[o] >= gscore[g]) if o < g else (gscore[o] > gscore[g])
            beaten = beaten + wins.astype(I32)
        masked.append(jnp.where(beaten < TOPK_GROUPS, blocks[g], -jnp.inf))
    cur = jnp.concatenate(masked, axis=0)

    ie = lax.broadcasted_iota(I32, (N_EXPERTS, tn), 0)
    firsts, raw = [], []
    chosen = jnp.zeros((N_EXPERTS, tn), F32)
    for _ in range(TOP_K):
        mx = jnp.max(cur, axis=0, keepdims=True)
        first = _first_index(cur == mx, ie, N_EXPERTS)
        hit = ie == first
        firsts.append(first)
        raw.append(jnp.sum(jnp.where(hit, scores, 0.0), axis=0, keepdims=True))
        cur = jnp.where(hit, -jnp.inf, cur)
        chosen = chosen + hit.astype(F32)
    total = raw[0]
    for k in range(1, TOP_K):
        total = total + raw[k]

    before = (lax.broadcasted_iota(I32, (tn, tn), 0) < lax.broadcasted_iota(I32, (tn, tn), 1)).astype(BF16)
    rank_tile = _dg(chosen.astype(BF16), before, _NN)
    old_carry = carry[...]
    rank = rank_tile + old_carry[:, 0:1]
    tile_cnt = jnp.broadcast_to(jnp.sum(chosen, axis=1, keepdims=True), old_carry.shape)
    lower = (lax.broadcasted_iota(I32, (N_EXPERTS, N_EXPERTS), 1)
             < lax.broadcasted_iota(I32, (N_EXPERTS, N_EXPERTS), 0)).astype(F32)
    tile_off = _x_dot(lower, tile_cnt)
    local = rank_tile + tile_off[:, 0:1]
    tile_chunks = jnp.floor((tile_cnt + (DISPATCH_CHUNK - 1)) * (1.0 / DISPATCH_CHUNK))
    chunk_off = _x_dot(lower, tile_chunks)
    padded = rank_tile + chunk_off[:, 0:1] * DISPATCH_CHUNK
    e_out[...] = jnp.zeros_like(e_out)
    pos_out[...] = jnp.zeros_like(pos_out)
    loc_out[...] = jnp.zeros_like(loc_out)
    pad_out[...] = jnp.zeros_like(pad_out)
    w_out[...] = jnp.zeros_like(w_out)
    for k in range(TOP_K):
        hit = ie == firsts[k]
        e_out[0, k:k + 1, :] = firsts[k]
        pos_out[0, k:k + 1, :] = jnp.sum(jnp.where(hit, rank, 0.0), axis=0, keepdims=True).astype(I32)
        loc_out[0, k:k + 1, :] = jnp.sum(jnp.where(hit, local, 0.0), axis=0, keepdims=True).astype(I32)
        pad_out[0, k:k + 1, :] = jnp.sum(jnp.where(hit, padded, 0.0), axis=0, keepdims=True).astype(I32)
        w_out[k:k + 1, :] = raw[k] / total * ROUTED_SCALE
    lane = lax.broadcasted_iota(I32, old_carry.shape, 1)
    stat_out[0] = jnp.where(lane == 0, tile_cnt, jnp.where(lane == 1, tile_off, jnp.where(
        lane == 2, old_carry, chunk_off))).astype(I32)
    new_carry = old_carry + tile_cnt
    carry[...] = new_carry
    cnt_out[...] = new_carry


def _router(h2, wr_t, b_col):
    n, d = h2.shape
    tn = ROUTER_TILE
    row_spec = pl.BlockSpec((SLOT_ROWS, tn), lambda i: (0, i))
    tile_spec = pl.BlockSpec((1, SLOT_ROWS, tn), lambda i: (i, 0, 0))
    cnt_spec = pl.BlockSpec((N_EXPERTS, 128), lambda i: (0, 0))
    tile_shape = jax.ShapeDtypeStruct((n // tn, SLOT_ROWS, tn), I32)
    return pl.pallas_call(
        _router_body,
        grid=(n // tn,),
        in_specs=[pl.BlockSpec((tn, d), lambda i: (i, 0)), _full(wr_t.shape), _full(b_col.shape)],
        out_specs=[tile_spec, tile_spec, tile_spec, tile_spec,
                   pl.BlockSpec((1, N_EXPERTS, 128), lambda i: (i, 0, 0)), row_spec, cnt_spec],
        out_shape=[tile_shape, tile_shape, tile_shape, tile_shape,
                   jax.ShapeDtypeStruct((n // tn, N_EXPERTS, 128), I32),
                   jax.ShapeDtypeStruct((SLOT_ROWS, n), F32), jax.ShapeDtypeStruct((N_EXPERTS, 128), F32)],
        scratch_shapes=[pltpu.VMEM((N_EXPERTS, 128), F32)],
        compiler_params=_cparams("arbitrary"),
        name="router",
    )(h2, wr_t, b_col)


ROW_LANES = 128
ROW_SUB = D_MODEL // 2 // ROW_LANES
U32 = jnp.uint32
HIGH_HALF = np.uint32(0xFFFF0000)


def _bf16_bits(x):
    return lax.bitcast_convert_type(x.astype(BF16).astype(F32), U32)


def _to_row_tiles(ref, x):
    rows, d = x.shape
    words = (_bf16_bits(x[:, :d // 2]) >> 16) | (_bf16_bits(x[:, d // 2:]) & HIGH_HALF)
    for s in range(ROW_SUB):
        ref[pl.ds(s, rows, stride=ROW_SUB), :] = words[:, s * ROW_LANES:(s + 1) * ROW_LANES]


def _from_row_tiles(ref, rows):
    words = jnp.concatenate([ref[pl.ds(s, rows, stride=ROW_SUB), :] for s in range(ROW_SUB)], axis=1)
    low = lax.bitcast_convert_type(words << 16, F32)
    high = lax.bitcast_convert_type(words & HIGH_HALF, F32)
    return jnp.concatenate([low, high], axis=1)


def _row_copy(src, src_off, dst, dst_off, sem):
    return pltpu.make_async_copy(src.at[pl.ds(pl.multiple_of(src_off, ROW_SUB), ROW_SUB), :],
                                 dst.at[pl.ds(pl.multiple_of(dst_off, ROW_SUB), ROW_SUB), :], sem)


def _load_slots(slot_ref, slot_smem, sem):
    cp = pltpu.make_async_copy(slot_ref.at[0], slot_smem, sem)
    cp.start()
    cp.wait()


def _chunk_copy(src, src_row, dst, dst_row, sem):
    n = DISPATCH_CHUNK * ROW_SUB
    return pltpu.make_async_copy(src.at[pl.ds(pl.multiple_of(src_row * ROW_SUB, ROW_SUB), n), :],
                                 dst.at[pl.ds(pl.multiple_of(dst_row * ROW_SUB, ROW_SUB), n), :], sem)


def _dispatch_body(start_ref, e_ref, pos_ref, loc_ref, stat_ref, h_ref, xs_in, slots_out, xs_out,
                   stat_smem, pending, staging, sem_s, sem):
    del xs_in
    i = pl.program_id(0)
    last = pl.num_programs(0) - 1
    tn = h_ref.shape[0]
    rows = TOP_K * tn
    cur = i % 2

    e = e_ref[0]
    slot = pos_ref[0]
    for x in range(N_EXPERTS):
        slot = slot + jnp.where(e == x, start_ref[x], 0)
    slots_out[0] = slot * ROW_SUB
    stats = pltpu.make_async_copy(stat_ref.at[0], stat_smem, sem_s)
    stats.start()

    @pl.when(i == 0)
    def _():
        pending[0] = 0
        pad = jnp.zeros((DISPATCH_CHUNK * ROW_SUB, ROW_LANES), U32)
        for b in range(2):
            staging[b, pl.ds(rows * ROW_SUB, DISPATCH_CHUNK * ROW_SUB), :] = pad

    loc = loc_ref[0]
    p_iota = lax.broadcasted_iota(I32, (rows, tn), 0)
    pick = jnp.zeros((rows, tn), F32)
    for k in range(TOP_K):
        pick = jnp.where(p_iota == loc[k:k + 1, :], 1.0, pick)
    _to_row_tiles(staging.at[cur], _dg(pick.astype(BF16), h_ref[...].astype(BF16), _NN))

    def drain(count):
        def one(c, carry):
            _chunk_copy(staging.at[0], 0, xs_out, 0, sem).wait()
            return carry
        lax.fori_loop(0, count, one, 0)

    drain(pending[0])
    stats.wait()

    def per_expert(x, total):
        count = stat_smem[x, 0]
        src = stat_smem[x, 1]
        dst = start_ref[x] + stat_smem[x, 2]
        n_chunks = lax.shift_right_logical(count + (DISPATCH_CHUNK - 1), DISPATCH_SHIFT)

        def one(c, carry):
            _chunk_copy(staging.at[cur], src + c * DISPATCH_CHUNK, xs_out, dst + c * DISPATCH_CHUNK, sem).start()
            return carry

        lax.fori_loop(0, n_chunks, one, 0)
        return total + n_chunks

    issued = lax.fori_loop(0, N_EXPERTS, per_expert, 0)
    pending[0] = issued

    @pl.when(i == last)
    def _():
        drain(issued)


def _zero_tail_body(lb_ref, o_ref):
    del lb_ref
    o_ref[...] = jnp.zeros_like(o_ref)


def _zero_tails(last_block, n_blocks):
    rows = MOE_BLOCK * ROW_SUB
    grid_spec = pltpu.PrefetchScalarGridSpec(
        num_scalar_prefetch=1,
        grid=(last_block.shape[0],),
        in_specs=[],
        out_specs=pl.BlockSpec((rows, ROW_LANES), lambda e, lb: (lb[e], 0)),
    )
    return pl.pallas_call(
        _zero_tail_body,
        grid_spec=grid_spec,
        out_shape=jax.ShapeDtypeStruct((n_blocks * rows, ROW_LANES), U32),
        compiler_params=_cparams("arbitrary"),
        name="zero_tails",
    )(last_block)


def _dispatch(row_start, e3, pos3, loc3, stats, h2, xs_zero):
    n, d = h2.shape
    tn = e3.shape[2]
    tile = lambda: pl.BlockSpec((1, SLOT_ROWS, tn), lambda i, st: (i, 0, 0))
    grid_spec = pltpu.PrefetchScalarGridSpec(
        num_scalar_prefetch=1,
        grid=(n // tn,),
        in_specs=[tile(), tile(), tile(), pl.BlockSpec((1, N_EXPERTS, 128), lambda i, st: (i, 0, 0)),
                  pl.BlockSpec((tn, d), lambda i, st: (i, 0)), pl.BlockSpec(memory_space=pl.ANY)],
        out_specs=[tile(), pl.BlockSpec(memory_space=pl.ANY)],
        scratch_shapes=[pltpu.SMEM((N_EXPERTS, 128), I32), pltpu.SMEM((1,), I32),
                        pltpu.VMEM((2, (TOP_K * tn + DISPATCH_CHUNK) * ROW_SUB, ROW_LANES), U32),
                        pltpu.SemaphoreType.DMA, pltpu.SemaphoreType.DMA],
    )
    return pl.pallas_call(
        _dispatch_body,
        grid_spec=grid_spec,
        out_shape=[jax.ShapeDtypeStruct((n // tn, SLOT_ROWS, tn), I32),
                   jax.ShapeDtypeStruct(xs_zero.shape, xs_zero.dtype)],
        input_output_aliases={6: 1},
        compiler_params=_cparams("arbitrary"),
        name="dispatch",
    )(row_start, e3, pos3, loc3, stats, h2, xs_zero)


def _swiglu(x, wg, wu, wd):
    g = _dg(x, wg, _NN)
    u = _dg(x, wu, _NN)
    return _dg((g * _sigmoid(g) * u).astype(BF16), wd, _NN)


def _expert_body(be_ref, nu_ref, xs_ref, wg_ref, wu_ref, wd_ref, ys_ref):
    del be_ref
    b = pl.program_id(0)

    @pl.when(b < nu_ref[0])
    def _():
        x = _from_row_tiles(xs_ref, MOE_BLOCK).astype(BF16)
        _to_row_tiles(ys_ref, _swiglu(x, wg_ref[0], wu_ref[0], wd_ref[0]))

    @pl.when(b >= nu_ref[0])
    def _():
        ys_ref[...] = jnp.zeros_like(ys_ref)


def _experts(block_e, n_used, xs, wg, wu, wd):
    bm = MOE_BLOCK
    _, d, de = wg.shape
    rows = bm * ROW_SUB
    grid_spec = pltpu.PrefetchScalarGridSpec(
        num_scalar_prefetch=2,
        grid=(xs.shape[0] // rows,),
        in_specs=[
            pl.BlockSpec((rows, ROW_LANES), lambda b, be, nu: (jnp.minimum(b, nu[0] - 1), 0)),
            pl.BlockSpec((1, d, de), lambda b, be, nu: (be[b], 0, 0)),
            pl.BlockSpec((1, d, de), lambda b, be, nu: (be[b], 0, 0)),
            pl.BlockSpec((1, de, d), lambda b, be, nu: (be[b], 0, 0)),
        ],
        out_specs=pl.BlockSpec((rows, ROW_LANES), lambda b, be, nu: (b, 0)),
    )
    return pl.pallas_call(
        _expert_body,
        grid_spec=grid_spec,
        out_shape=jax.ShapeDtypeStruct(xs.shape, xs.dtype),
        compiler_params=_cparams("arbitrary"),
        name="experts",
    )(block_e, n_used, xs, wg, wu, wd)


def _combine_body(start_ref, pad_ref, stat_ref, w_ref, x1_ref, h2_ref, gate_ref, ys_hbm, wsg_ref, wsu_ref, wsd_ref,
                  gpost_ref, out_ref, stat_smem, staging, sem_s, sem):
    i = pl.program_id(0)
    tn = x1_ref.shape[0]
    stats = pltpu.make_async_copy(stat_ref.at[0], stat_smem, sem_s)
    stats.start()

    @pl.when(i == 0)
    def _():
        staging[...] = jnp.zeros_like(staging)

    stats.wait()

    def per_expert(x, total):
        count = stat_smem[x, 0]
        src = start_ref[x] + stat_smem[x, 2]
        dst = stat_smem[x, 3] * DISPATCH_CHUNK
        n_chunks = lax.shift_right_logical(count + (DISPATCH_CHUNK - 1), DISPATCH_SHIFT)

        def one(c, carry):
            _chunk_copy(ys_hbm, src + c * DISPATCH_CHUNK, staging, dst + c * DISPATCH_CHUNK, sem).start()
            return carry

        lax.fori_loop(0, n_chunks, one, 0)
        return total + n_chunks

    issued = lax.fori_loop(0, N_EXPERTS, per_expert, 0)
    acc = _swiglu(h2_ref[...].astype(BF16), wsg_ref[...], wsu_ref[...], wsd_ref[...])

    def drain(c, carry):
        _chunk_copy(ys_hbm, 0, staging, 0, sem).wait()
        return carry

    lax.fori_loop(0, issued, drain, 0)
    piece = COMBINE_ROWS // COMBINE_PIECES
    for q in range(COMBINE_PIECES):
        col = lax.broadcasted_iota(I32, (tn, piece), 1) + q * piece
        weight = jnp.zeros((tn, piece), F32)
        for k in range(TOP_K):
            weight = jnp.where(col == pad_ref[:, k:k + 1], w_ref[:, k:k + 1], weight)
        rows = _from_row_tiles(staging.at[pl.ds(q * piece * ROW_SUB, piece * ROW_SUB), :], piece)
        acc = acc + _dg(weight.astype(BF16), rows.astype(BF16), _NN)
    out_ref[...] = x1_ref[...] + gate_ref[0] * _rms(acc, gpost_ref[...])


def _combine(row_start, pad_tok, stats, w_tok, x1, h2, gate_rows, ys, wsg, wsu, wsd, gpost):
    n, d = x1.shape
    tn = TOKEN_TILE
    tok = pl.BlockSpec((tn, d), lambda i, st: (i, 0))
    small = pl.BlockSpec((tn, SLOT_ROWS), lambda i, st: (i, 0))
    grid_spec = pltpu.PrefetchScalarGridSpec(
        num_scalar_prefetch=1,
        grid=(n // tn,),
        in_specs=[small, pl.BlockSpec((1, N_EXPERTS, 128), lambda i, st: (i, 0, 0)), small, tok, tok,
                  pl.BlockSpec((1, 1, d), lambda i, st: (i, 0, 0)),
                  pl.BlockSpec(memory_space=pl.ANY),
                  pl.BlockSpec(wsg.shape, lambda i, st: (0, 0)), pl.BlockSpec(wsu.shape, lambda i, st: (0, 0)),
                  pl.BlockSpec(wsd.shape, lambda i, st: (0, 0)), pl.BlockSpec(gpost.shape, lambda i, st: (0, 0))],
        out_specs=tok,
        scratch_shapes=[pltpu.SMEM((N_EXPERTS, 128), I32), pltpu.VMEM((COMBINE_ROWS * ROW_SUB, ROW_LANES), U32),
                        pltpu.SemaphoreType.DMA, pltpu.SemaphoreType.DMA],
    )
    return pl.pallas_call(
        _combine_body,
        grid_spec=grid_spec,
        out_shape=jax.ShapeDtypeStruct((n, d), F32),
        compiler_params=_cparams("arbitrary"),
        name="combine",
    )(row_start, pad_tok, stats, w_tok, x1, h2, gate_rows, ys, wsg, wsu, wsd, gpost)


def _moe(x1, h2, gate_rows, lp):
    n, d = h2.shape
    bm = MOE_BLOCK
    tn = TOKEN_TILE
    e3, pos3, loc3, pad3, stats, w_rows, cnt = _router(h2, lp['wr_t'], lp['b_router_col'])
    counts = cnt[:, 0].astype(I32)
    blocks_per_e = (counts + DISPATCH_CHUNK + bm - 1) // bm
    blk_end = jnp.cumsum(blocks_per_e)
    blk_start = blk_end - blocks_per_e
    n_blocks = -(-(n * TOP_K + N_EXPERTS * DISPATCH_CHUNK) // bm) + N_EXPERTS
    block_e = jnp.minimum(jnp.sum(blk_end[None, :] <= jnp.arange(n_blocks, dtype=I32)[:, None], axis=1),
                          N_EXPERTS - 1).astype(I32)
    n_used = blk_end[-1:].astype(I32)
    partial = jnp.stack([blk_start + counts // bm, blk_end - 1], axis=1).reshape(-1)
    xs_tails = _zero_tails(jnp.clip(partial, 0, n_blocks - 1).astype(I32), n_blocks)
    row_start = (blk_start * bm).astype(I32)
    _, xs = _dispatch(row_start, e3, pos3, loc3, stats, h2, xs_tails)
    ys = _experts(block_e, n_used, xs, lp['w_e_gate'], lp['w_e_up'], lp['w_e_down'])
    pad_tok = pad3.transpose(0, 2, 1).reshape(n, SLOT_ROWS)
    return _combine(row_start, pad_tok, stats, w_rows.T, x1, h2, gate_rows, ys, lp['w_s_gate'], lp['w_s_up'],
                    lp['w_s_down'], lp['g_post_ffn'])


def _rope_tables(n_tokens):
    rows = n_tokens // GRID_W
    row = jnp.repeat(jnp.arange(rows, dtype=F32), GRID_W)
    col = jnp.tile(jnp.arange(GRID_W, dtype=F32), rows)
    n_freq = ROPE_DIM // 4
    inv_freq = ROPE_THETA ** (-jnp.arange(n_freq, dtype=F32) / n_freq)
    ang_r = row[:, None] * inv_freq
    ang_c = col[:, None] * inv_freq
    ang = jnp.concatenate([ang_r, ang_r, ang_c, ang_c], axis=-1)
    cos, sin = jnp.cos(ang), jnp.sin(ang)
    pad = lambda z, fill: jnp.concatenate(
        [jnp.full((n_tokens, NOPE_DIM), fill, F32), z, jnp.zeros((n_tokens, HEAD_PAD - QK_DIM), F32)], axis=1)
    return pad(cos, 1.0), pad(sin, 0.0), pad(cos, 0.0), pad(sin, 0.0)


def _rot_cols(w):
    q = ROPE_DIM // 4
    perm = np.concatenate([np.arange(q, 2 * q), np.arange(0, q), np.arange(3 * q, 4 * q), np.arange(2 * q, 3 * q)])
    sign = np.concatenate([-np.ones(q), np.ones(q), -np.ones(q), np.ones(q)]).astype(np.float32)
    return w[..., perm] * sign


def _block_diag(blocks):
    h, m, n = blocks.shape[-3:]
    eye = jnp.eye(h, dtype=blocks.dtype)
    out = blocks[..., :, :, None, :] * eye[:, None, :, None]
    return out.reshape(blocks.shape[:-3] + (h * m, h * n))


def _diag_blocks(mat, h):
    m, n = mat.shape[-2] // h, mat.shape[-1] // h
    z = mat.reshape(mat.shape[:-2] + (h, m, h, n))
    return jnp.stack([z[..., i, :, i, :] for i in range(h)], axis=-3)


def _layer_operands(l, a):
    w_in = a['w_in'][l]
    d = w_in.shape[0]
    p_mla = Q_LORA + KV_LORA + ROPE_DIM
    w_kr = w_in[:, Q_LORA + KV_LORA:p_mla]
    z32 = jnp.zeros((d, ROPE_DIM), F32)
    w_mla = jnp.concatenate([w_in[:, :Q_LORA + KV_LORA], z32, _rot_cols(w_kr), w_kr, z32], axis=1)
    wq3 = a['w_q_b'][l].reshape(Q_LORA, H_MLA, QK_DIM)
    zq = lambda n: jnp.zeros((Q_LORA, H_MLA, n), F32)
    wq = jnp.concatenate([wq3, zq(HEAD_PAD - QK_DIM)], axis=2).reshape(Q_LORA, H_MLA * HEAD_PAD)
    wqr = jnp.concatenate([zq(NOPE_DIM), _rot_cols(wq3[:, :, NOPE_DIM:]), zq(HEAD_PAD - QK_DIM)],
                          axis=2).reshape(Q_LORA, H_MLA * HEAD_PAD)
    wkv3 = a['w_kv_b'][l].reshape(KV_LORA, H_MLA, NOPE_DIM + V_DIM)
    wk = jnp.concatenate([wkv3[:, :, :NOPE_DIM], jnp.zeros((KV_LORA, H_MLA, HEAD_PAD - NOPE_DIM), F32)],
                         axis=2).reshape(KV_LORA, H_MLA * HEAD_PAD)
    wv = wkv3[:, :, NOPE_DIM:].reshape(KV_LORA, H_MLA * V_DIM)
    row = lambda z: z.reshape(1, -1)
    two = lambda z: jnp.concatenate([z[0], z[1]], axis=-1)
    bd2 = lambda z: jnp.concatenate([jnp.concatenate([z[0], jnp.zeros_like(z[0])], axis=1),
                                     jnp.concatenate([jnp.zeros_like(z[1]), z[1]], axis=1)], axis=0)
    return {
        'g_pre_mix': row(a['g_pre_mix'][l]), 'g_post_mix': row(a['g_post_mix'][l]),
        'g_pre_ffn': row(a['g_pre_ffn'][l]), 'g_post_ffn': row(a['g_post_ffn'][l]),
        'w_mla': w_mla.astype(BF16),
        'w_rwkv': w_in[:, p_mla:p_mla + P_RWKV].astype(BF16),
        'w_lru': w_in[:, p_mla + P_RWKV:].astype(BF16),
        'g_q_a': row(a['g_q_a'][l]), 'g_kv_a': row(a['g_kv_a'][l]),
        'wq': wq.astype(BF16), 'wqr': wqr.astype(BF16), 'wk': wk.astype(BF16), 'wv': wv.astype(BF16),
        'g_mla_out': row(a['g_mla_out'][l]),
        'mu_rwkv': row(a['mu_rwkv'][l]), 'k_k': row(a['k_k'][l]), 'k_a': row(a['k_a'][l]), 'r_k': row(a['r_k'][l]),
        'w0_cat': row(two(a['w0_rwkv'][l])), 'w2_bd': bd2(a['w2_rwkv'][l]),
        'a0_cat': row(two(a['a0_rwkv'][l])), 'a2_bd': bd2(a['a2_rwkv'][l]),
        'g2_rwkv': a['g2_rwkv'][l], 'ln_x_w': row(a['ln_x_w'][l]), 'ln_x_b': row(a['ln_x_b'][l]),
        'conv_w': a['conv_w'][l], 'conv_b': a['conv_b'][l],
        'wa_bd': _block_diag(a['w_lru_a'][l]), 'b_lru_a': a['b_lru_a'][l],
        'wx_bd': _block_diag(a['w_lru_x'][l]), 'b_lru_x': a['b_lru_x'][l],
        'lam': a['lam'][l], 'g_lru_out': row(a['g_lru_out'][l]),
        'w_out': a['w_out'][l].astype(BF16),
        'wr_t': a['w_router'][l].T, 'b_router_col': a['b_router'][l].reshape(N_EXPERTS, 1),
        'w_e_gate': a['w_e_gate'][l].astype(BF16), 'w_e_up': a['w_e_up'][l].astype(BF16),
        'w_e_down': a['w_e_down'][l].astype(BF16),
        'w_s_gate': a['w_s_gate'][l].astype(BF16), 'w_s_up': a['w_s_up'][l].astype(BF16),
        'w_s_down': a['w_s_down'][l].astype(BF16),
    }


def _mixer(x, grp, mod, lp, ctx, tables, prev):
    _, nb, t = grp
    tt = min(t, 512)
    p_mla, p_rwkv, p_lru = _in_proj(x, grp, lp['g_pre_mix'], mod['sc1'], mod['sh1'], lp['w_mla'], lp['w_rwkv'],
                                    lp['w_lru'], tt)
    if ctx is None:
        q, k, v, new_cache = _mla_prep(p_mla, lp['g_q_a'], lp['g_kv_a'], lp['wq'], None, lp['wk'], lp['wv'], None, tt)
        kc = vc = None
        s0 = jnp.zeros((nb, 2, D_RWKV, D_RWKV), F32)
        h0 = jnp.zeros((nb, 2, 1, D_LRU), F32)
    else:
        cache, s_rwkv, s_lru = ctx
        q, k, v = _mla_prep(p_mla, lp['g_q_a'], lp['g_kv_a'], lp['wq'], lp['wqr'], lp['wk'], lp['wv'], tables, tt)
        new_cache = None
        kr_placed = jnp.pad(cache[..., KV_LORA:], ((0, 0), (0, 0), (NOPE_DIM, HEAD_PAD - QK_DIM)))
        kc, vc = _ctx_kv(cache[..., :KV_LORA], kr_placed, lp['wk'], lp['wv'])
        s0 = _block_diag(s_rwkv)
        h0 = s_lru[:, :, None, :]
    o = _attention(q, k, v, kc, vc, min(t, 256))

    r, vv, kk, lw, ag, kd, g, bonus = _rwkv_prep(p_rwkv, lp, tt)
    y_f, y_b, s_fin = _rwkv_scan(r, vv, kk, lw, ag, kd, s0)

    hp, hn = _halo_rows(p_lru[..., :D_LRU], tt, SUBLANES)
    h_f, fin_f = _lru_dir(p_lru, hp, lp, 0, h0[:, 0], tt)
    h_b, fin_b = _lru_dir(p_lru, hn, lp, 1, h0[:, 1], tt)

    x1, h2 = _out_proj(x, grp, o, y_f, y_b, bonus, g, h_f, h_b, p_lru, lp, mod, tt, prev)
    states = None
    if ctx is None:
        states = (new_cache, _diag_blocks(s_fin, H_RWKV), jnp.concatenate([fin_f, fin_b], axis=1))
    return x1, h2, states


def kernel(x_prompt, x_sample, c, cache_mla, state_rwkv, state_lru, c_ctx, w_ada, b_ada, g_pre_mix, g_post_mix, g_pre_ffn, g_post_ffn, w_in, g_q_a, w_q_b, g_kv_a, w_kv_b, g_mla_out, mu_rwkv, w0_rwkv, w2_rwkv, a0_rwkv, a2_rwkv, g2_rwkv, k_k, k_a, r_k, ln_x_w, ln_x_b, conv_w, conv_b, w_lru_a, b_lru_a, w_lru_x, b_lru_x, lam, g_lru_out, w_out, w_router, b_router, w_e_gate, w_e_up, w_e_down, w_s_gate, w_s_up, w_s_down):
    a = dict(w_in=w_in, g_pre_mix=g_pre_mix, g_post_mix=g_post_mix, g_pre_ffn=g_pre_ffn, g_post_ffn=g_post_ffn,
             g_q_a=g_q_a, w_q_b=w_q_b, g_kv_a=g_kv_a, w_kv_b=w_kv_b, g_mla_out=g_mla_out, mu_rwkv=mu_rwkv,
             w0_rwkv=w0_rwkv, w2_rwkv=w2_rwkv, a0_rwkv=a0_rwkv, a2_rwkv=a2_rwkv, g2_rwkv=g2_rwkv, k_k=k_k, k_a=k_a,
             r_k=r_k.reshape(r_k.shape[0], -1), ln_x_w=ln_x_w, ln_x_b=ln_x_b, conv_w=conv_w, conv_b=conv_b,
             w_lru_a=w_lru_a, b_lru_a=b_lru_a, w_lru_x=w_lru_x, b_lru_x=b_lru_x, lam=lam, g_lru_out=g_lru_out,
             w_out=w_out, w_router=w_router, b_router=b_router, w_e_gate=w_e_gate, w_e_up=w_e_up,
             w_e_down=w_e_down, w_s_gate=w_s_gate, w_s_up=w_s_up, w_s_down=w_s_down)
    n_layers = w_in.shape[0]
    nbp, tp, d = x_prompt.shape
    nbs, ts, _ = x_sample.shape
    rows = -(-(1 + nbs) // SUBLANES) * SUBLANES
    cvecs = jnp.concatenate([c_ctx[None, :], c, jnp.zeros((rows - 1 - nbs, d), F32)], axis=0)
    mods = _ada(cvecs, w_ada, b_ada)
    names = ('sh1', 'sc1', 'g1', 'sh2', 'sc2', 'g2')
    tables = _rope_tables(ts)
    tn = TOKEN_TILE

    grp_p, grp_s = (0, nbp, tp), (nbp * tp, nbs, ts)
    x = jnp.concatenate([x_prompt.reshape(-1, d), x_sample.reshape(-1, d)], axis=0)
    caches, rwkv_states, lru_states = [], [], []
    for l in range(n_layers):
        lp = _layer_operands(l, a)
        mod_p = {nm: mods[l, 0:1, i * d:(i + 1) * d][:, None, :] for i, nm in enumerate(names)}
        mod_s = {nm: mods[l, 1:1 + nbs, i * d:(i + 1) * d][:, None, :] for i, nm in enumerate(names)}
        x1, h2, states = _mixer(x, grp_p, mod_p, lp, None, None, None)
        x1, h2, _ = _mixer(x, grp_s, mod_s, lp, (cache_mla[:, l], state_rwkv[:, l], state_lru[:, l]), tables,
                           (x1, h2))
        caches.append(states[0])
        rwkv_states.append(states[1])
        lru_states.append(states[2])
        gate_rows = jnp.concatenate([jnp.repeat(mod_p['g2'], nbp * tp // tn, axis=0),
                                     jnp.repeat(mod_s['g2'], ts // tn, axis=0)], axis=0)
        x = _moe(x1, h2, gate_rows, lp)
    yp = x[:nbp * tp].reshape(nbp, tp, d)
    ys = x[nbp * tp:].reshape(nbs, ts, d)
    return (yp, ys, jnp.stack(caches, axis=1), jnp.stack(rwkv_states, axis=1), jnp.stack(lru_states, axis=1))
```

```python
import functools

import numpy as np
import jax
import jax.numpy as jnp
from jax import lax
from jax.experimental import pallas as pl
from jax.experimental.pallas import tpu as pltpu

F32 = jnp.float32
BF16 = jnp.bfloat16
I32 = jnp.int32

D_MODEL = 1024
GRID_W = 64
EPS = 1e-6
H_MLA = 4
Q_LORA = 256
KV_LORA = 128
NOPE_DIM = 64
ROPE_DIM = 32
V_DIM = 128
QK_DIM = NOPE_DIM + ROPE_DIM
ROPE_THETA = 10000.0
HEAD_PAD = 128
H_RWKV = 4
N_RWKV = 64
D_RWKV = H_RWKV * N_RWKV
W_LORA = 64
A_LORA = 64
G_LORA = 128
GN_EPS = 64e-5
D_LRU = 256
H_LRU = 4
BS_LRU = D_LRU // H_LRU
CONV_W = 4
LRU_C = 8.0
D_MLA_OUT = H_MLA * V_DIM
P_MLA_PAD = Q_LORA + KV_LORA + HEAD_PAD
P_RWKV = 3 * D_RWKV + 2 * W_LORA + 2 * A_LORA + G_LORA
P_LRU = 2 * D_LRU
N_EXPERTS = 64
TOP_K = 6
N_GROUPS = 8
TOPK_GROUPS = 4
GROUP_SIZE = N_EXPERTS // N_GROUPS
D_EXPERT = 256
ROUTED_SCALE = 2.5

SUBLANES = 8
RWKV_CHUNK = 64
RWKV_ROWS = 2
MOE_BLOCK = 512
ROUTER_TILE = 256
DISPATCH_SHIFT = 5
DISPATCH_CHUNK = 1 << DISPATCH_SHIFT
TOKEN_TILE = 256
COMBINE_ROWS = TOP_K * TOKEN_TILE + N_EXPERTS * DISPATCH_CHUNK
COMBINE_PIECES = 4
SLOT_ROWS = 8
VMEM_LIMIT = 56 * 1024 * 1024


def _cparams(*sem):
    return pltpu.CompilerParams(dimension_semantics=sem, vmem_limit_bytes=VMEM_LIMIT)


def _sigmoid(x):
    return 1.0 / (1.0 + jnp.exp(-x))


def _softplus(x):
    return jnp.maximum(x, 0.0) + jnp.log1p(jnp.exp(-jnp.abs(x)))


def _expm1(z):
    e = jnp.exp(z)
    direct = (e == 1.0) | (z < -1.0)
    corrected = (e - 1.0) * z / jnp.log(jnp.where(direct, 2.0, e))
    return jnp.where(e == 1.0, z, jnp.where(z < -1.0, e - 1.0, corrected))


def _rms(x, g):
    return x * lax.rsqrt(jnp.mean(x * x, axis=-1, keepdims=True) + EPS) * g


def _split2(x):
    hi = x.astype(BF16)
    lo = (x - hi.astype(F32)).astype(BF16)
    return hi, lo


def _split3(x):
    hi = x.astype(BF16)
    r1 = x - hi.astype(F32)
    mid = r1.astype(BF16)
    lo = (r1 - mid.astype(F32)).astype(BF16)
    return hi, mid, lo


_NN = (((1,), (0,)), ((), ()))
_NT = (((1,), (1,)), ((), ()))


def _dg(a, b, dims):
    return lax.dot_general(a, b, dims, preferred_element_type=F32)


def _dot1(a, b, dims=_NN):
    return _dg(a.astype(BF16), b.astype(BF16), dims)


def _dot3(a, b, dims=_NN):
    ah, al = _split2(a)
    bh, bl = _split2(b)
    return _dg(ah, bh, dims) + (_dg(ah, bl, dims) + _dg(al, bh, dims))


def _dot_x(a, b_exact, dims=_NN):
    h, m, l = _split3(a)
    bb = b_exact.astype(BF16)
    return _dg(h, bb, dims) + (_dg(m, bb, dims) + _dg(l, bb, dims))


def _x_dot(a_exact, b, dims=_NN):
    h, m, l = _split3(b)
    aa = a_exact.astype(BF16)
    return _dg(aa, h, dims) + (_dg(aa, m, dims) + _dg(aa, l, dims))


def _head_ones(n, seg):
    r = lax.broadcasted_iota(I32, (n, n), 0) // seg
    c = lax.broadcasted_iota(I32, (n, n), 1) // seg
    return (r == c).astype(F32)


def _ada_body(c_ref, w_ref, b_ref, o_ref):
    cv = c_ref[...]
    s = cv * _sigmoid(cv)
    o_ref[0] = _dot3(s, w_ref[0]) + b_ref[0]


def _ada(cvecs, w_ada, b_ada):
    n_layers, d, d6 = w_ada.shape
    rows = cvecs.shape[0]
    nt = d6 // d
    return pl.pallas_call(
        _ada_body,
        grid=(n_layers, nt),
        in_specs=[
            pl.BlockSpec((rows, d), lambda l, j: (0, 0)),
            pl.BlockSpec((1, d, d), lambda l, j: (l, 0, j)),
            pl.BlockSpec((1, 1, d), lambda l, j: (l, 0, j)),
        ],
        out_specs=pl.BlockSpec((1, rows, d), lambda l, j: (l, 0, j)),
        out_shape=jax.ShapeDtypeStruct((n_layers, rows, d6), F32),
        compiler_params=_cparams("parallel", "parallel"),
        name="ada",
    )(cvecs, w_ada, b_ada.reshape(n_layers, 1, d6))


def _mod_spec(per_batch):
    if per_batch:
        return pl.BlockSpec((1, 1, D_MODEL), lambda b, i: (b, 0, 0))
    return pl.BlockSpec((1, 1, D_MODEL), lambda b, i: (0, 0, 0))


def _full(shape):
    nd = len(shape)
    return pl.BlockSpec(shape, lambda *_: (0,) * nd)


def _in_proj_body(x_ref, g_ref, sc_ref, sh_ref, w1_ref, w2_ref, w3_ref, o1_ref, o2_ref, o3_ref):
    h = _rms(x_ref[...], g_ref[...]) * (1.0 + sc_ref[0]) + sh_ref[0]
    hb = h.astype(BF16)
    o1_ref[0] = _dg(hb, w1_ref[...], _NN)
    o2_ref[0] = _dg(hb, w2_ref[...], _NN)
    o3_ref[0] = _dg(hb, w3_ref[...], _NN)


def _group_rows(grp, tt):
    row_off, nb, t = grp
    base, per = row_off // tt, t // tt
    return pl.BlockSpec((tt, D_MODEL), lambda b, i: (base + b * per + i, 0))


def _in_proj(x, grp, g, sc, sh, w_mla, w_rwkv, w_lru, tt):
    _, nb, t = grp
    d = x.shape[1]
    per_batch = sc.shape[0] > 1
    outs = [w_mla.shape[1], w_rwkv.shape[1], w_lru.shape[1]]
    return pl.pallas_call(
        _in_proj_body,
        grid=(nb, t // tt),
        in_specs=[
            _group_rows(grp, tt),
            _full((1, d)),
            _mod_spec(per_batch),
            _mod_spec(per_batch),
            _full(w_mla.shape),
            _full(w_rwkv.shape),
            _full(w_lru.shape),
        ],
        out_specs=[pl.BlockSpec((1, tt, n), lambda b, i: (b, i, 0)) for n in outs],
        out_shape=[jax.ShapeDtypeStruct((nb, t, n), F32) for n in outs],
        compiler_params=_cparams("parallel", "parallel"),
        name="in_proj",
    )(x, g, sc, sh, w_mla, w_rwkv, w_lru)


def _rope_lanes():
    lane = lax.broadcasted_iota(I32, (1, HEAD_PAD), 1)
    return ((lane >= NOPE_DIM) & (lane < QK_DIM)).astype(F32)


def _mla_prep_body(rope, *refs):
    if rope:
        (p_ref, gq_ref, gkv_ref, wq_ref, wqr_ref, wk_ref, wv_ref, cq_ref, sq_ref, cr_ref, sr_ref,
         q_ref, k_ref, v_ref) = refs
    else:
        p_ref, gq_ref, gkv_ref, wq_ref, wk_ref, wv_ref, q_ref, k_ref, v_ref, c_ref = refs
    p = p_ref[0]
    cq = _rms(p[:, :Q_LORA], gq_ref[...])
    ckv = _rms(p[:, Q_LORA:Q_LORA + KV_LORA], gkv_ref[...])
    kr = p[:, Q_LORA + KV_LORA:]
    cqb = cq.astype(BF16)
    ckvb = ckv.astype(BF16)
    scale = QK_DIM ** -0.5 * float(np.log2(np.e))
    qa = _dg(cqb, wq_ref[...], _NN) * scale
    ka = _dg(ckvb, wk_ref[...], _NN)
    va = _dg(ckvb, wv_ref[...], _NN)
    if rope:
        qr = _dg(cqb, wqr_ref[...], _NN) * scale
        krp = kr * cr_ref[...] + pltpu.roll(kr, ROPE_DIM, 1) * sr_ref[...]
    else:
        krp = kr * _rope_lanes()
    for h in range(H_MLA):
        sl = slice(h * HEAD_PAD, (h + 1) * HEAD_PAD)
        qh = qa[:, sl]
        if rope:
            qh = qh * cq_ref[...] + qr[:, sl] * sq_ref[...]
        q_ref[0, h] = qh.astype(BF16)
        k_ref[0, h] = (ka[:, sl] + krp).astype(BF16)
        v_ref[0, h] = va[:, sl].astype(BF16)
    if not rope:
        c_ref[0, :, :KV_LORA] = ckv
        c_ref[0, :, KV_LORA:] = pltpu.roll(kr, HEAD_PAD - NOPE_DIM, 1)[:, :ROPE_DIM]


def _mla_prep(p_mla, gq, gkv, wq, wqr, wk, wv, tables, tt):
    nb, t, pw = p_mla.shape
    rope = tables is not None
    hw = H_MLA * HEAD_PAD
    head_spec = pl.BlockSpec((1, H_MLA, tt, HEAD_PAD), lambda b, i: (b, 0, i, 0))
    head_shape = jax.ShapeDtypeStruct((nb, H_MLA, t, HEAD_PAD), BF16)
    in_specs = [pl.BlockSpec((1, tt, pw), lambda b, i: (b, i, 0)), _full(gq.shape), _full(gkv.shape), _full(wq.shape)]
    args = [p_mla, gq, gkv, wq]
    if rope:
        in_specs.append(_full(wqr.shape))
        args.append(wqr)
    in_specs += [_full(wk.shape), _full(wv.shape)]
    args += [wk, wv]
    out_specs = [head_spec, head_spec, head_spec]
    out_shape = [head_shape, head_shape, head_shape]
    if rope:
        in_specs += [pl.BlockSpec((tt, HEAD_PAD), lambda b, i: (i, 0))] * 4
        args += list(tables)
    else:
        out_specs.append(pl.BlockSpec((1, tt, KV_LORA + ROPE_DIM), lambda b, i: (b, i, 0)))
        out_shape.append(jax.ShapeDtypeStruct((nb, t, KV_LORA + ROPE_DIM), F32))
    del hw
    return pl.pallas_call(
        functools.partial(_mla_prep_body, rope),
        grid=(nb, t // tt),
        in_specs=in_specs,
        out_specs=out_specs,
        out_shape=out_shape,
        compiler_params=_cparams("parallel", "parallel"),
        name="mla_prep_rope" if rope else "mla_prep",
    )(*args)


def _ctx_kv_body(lat_ref, kr_ref, wk_ref, wv_ref, k_ref, v_ref):
    latb = lat_ref[0].astype(BF16)
    ka = _dg(latb, wk_ref[...], _NN)
    va = _dg(latb, wv_ref[...], _NN)
    kr = kr_ref[0]
    for h in range(H_MLA):
        sl = slice(h * HEAD_PAD, (h + 1) * HEAD_PAD)
        k_ref[0, h] = (ka[:, sl] + kr).astype(BF16)
        v_ref[0, h] = va[:, sl].astype(BF16)


def _ctx_kv(lat, kr_placed, wk, wv):
    nb, s, _ = lat.shape
    head_spec = pl.BlockSpec((1, H_MLA, s, HEAD_PAD), lambda b: (b, 0, 0, 0))
    head_shape = jax.ShapeDtypeStruct((nb, H_MLA, s, HEAD_PAD), BF16)
    return pl.pallas_call(
        _ctx_kv_body,
        grid=(nb,),
        in_specs=[
            pl.BlockSpec((1, s, KV_LORA), lambda b: (b, 0, 0)),
            pl.BlockSpec((1, s, HEAD_PAD), lambda b: (b, 0, 0)),
            _full(wk.shape),
            _full(wv.shape),
        ],
        out_specs=[head_spec, head_spec],
        out_shape=[head_shape, head_shape],
        compiler_params=_cparams("parallel"),
        name="ctx_kv",
    )(lat, kr_placed, wk, wv)


def _attn_body(has_ctx, *refs):
    if has_ctx:
        q_ref, k_ref, v_ref, kc_ref, vc_ref, o_ref = refs
    else:
        q_ref, k_ref, v_ref, o_ref = refs
    q = q_ref[0, 0]
    s = _dg(q, k_ref[0, 0], _NT)
    m = jnp.max(s, axis=-1, keepdims=True)
    if has_ctx:
        sc = _dg(q, kc_ref[0, 0], _NT)
        m = jnp.maximum(m, jnp.max(sc, axis=-1, keepdims=True))
    p = jnp.exp2(s - m)
    l = jnp.sum(p, axis=-1, keepdims=True)
    o = _dg(p.astype(BF16), v_ref[0, 0], _NN)
    if has_ctx:
        pc = jnp.exp2(sc - m)
        l = l + jnp.sum(pc, axis=-1, keepdims=True)
        o = o + _dg(pc.astype(BF16), vc_ref[0, 0], _NN)
    o_ref[0] = o / l


def _attention(q, k, v, kc, vc, tq):
    nb, nh, t, hd = q.shape
    s = k.shape[2]
    has_ctx = kc is not None
    in_specs = [
        pl.BlockSpec((1, 1, tq, hd), lambda b, h, i: (b, h, i, 0)),
        pl.BlockSpec((1, 1, s, hd), lambda b, h, i: (b, h, 0, 0)),
        pl.BlockSpec((1, 1, s, V_DIM), lambda b, h, i: (b, h, 0, 0)),
    ]
    args = [q, k, v]
    if has_ctx:
        sc = kc.shape[2]
        in_specs += [
            pl.BlockSpec((1, 1, sc, hd), lambda b, h, i: (b, h, 0, 0)),
            pl.BlockSpec((1, 1, sc, V_DIM), lambda b, h, i: (b, h, 0, 0)),
        ]
        args += [kc, vc]
    return pl.pallas_call(
        functools.partial(_attn_body, has_ctx),
        grid=(nb, nh, t // tq),
        in_specs=in_specs,
        out_specs=pl.BlockSpec((1, tq, V_DIM), lambda b, h, i: (b, i, h)),
        out_shape=jax.ShapeDtypeStruct((nb, t, nh * V_DIM), F32),
        compiler_params=_cparams("parallel", "parallel", "parallel"),
        name="attention",
    )(*args)


def _rwkv_prep_body(p_ref, hp_ref, hn_ref, mu_ref, kk_ref, ka_ref, rk_ref, w0_ref, w2_ref, a0_ref, a2_ref, g2_ref,
                    r_out, v_out, kk_out, lw_out, a_out, kd_out, g_out, bonus_out):
    p = p_ref[0]
    tt = p.shape[0]
    row = lax.broadcasted_iota(I32, (tt, 1), 0)
    prev = jnp.where(row == 0, hp_ref[0, 0], pltpu.roll(p, 1, 0))
    nxt = jnp.where(row == tt - 1, hn_ref[0, 0], pltpu.roll(p, tt - 1, 0))
    ps = p + (0.5 * (prev + nxt) - p) * mu_ref[...]
    r = ps[:, :D_RWKV]
    k = ps[:, D_RWKV:2 * D_RWKV]
    v = ps[:, 2 * D_RWKV:3 * D_RWKV]
    wl = ps[:, 3 * D_RWKV:3 * D_RWKV + 2 * W_LORA]
    al = ps[:, 3 * D_RWKV + 2 * W_LORA:3 * D_RWKV + 2 * W_LORA + 2 * A_LORA]
    gl = ps[:, 3 * D_RWKV + 2 * W_LORA + 2 * A_LORA:]
    ones = _head_ones(D_RWKV, N_RWKV)
    kk = k * kk_ref[...]
    kk = kk / jnp.maximum(jnp.sqrt(_dot_x(kk * kk, ones)), 1e-12)
    wlin = w0_ref[...] + _dot3(jnp.tanh(wl), w2_ref[...])
    lw = -jnp.exp(-_softplus(-wlin) - 0.5)
    ag = _sigmoid(a0_ref[...] + _dot3(al, a2_ref[...]))
    bonus = jnp.zeros_like(v)
    for d in range(2):
        sl = slice(d * D_RWKV, (d + 1) * D_RWKV)
        kd = k * (1.0 + (ag[:, sl] - 1.0) * ka_ref[...])
        kd_out[0, :, sl] = kd
        bonus = bonus + _dot_x(r * kd * rk_ref[...], ones) * v
    r_out[0] = r
    v_out[0] = v
    kk_out[0] = kk
    lw_out[0] = lw
    a_out[0] = ag
    g_out[0] = _dot1(_sigmoid(gl), g2_ref[...])
    bonus_out[0] = bonus


def _halo_rows(x, tt, width):
    nb, t, c = x.shape
    nt = t // tt
    xt = x.reshape(nb, nt, tt, c)
    zero = jnp.zeros((nb, 1, width, c), x.dtype)
    prev = jnp.concatenate([zero, xt[:, :-1, tt - width:]], axis=1)
    nxt = jnp.concatenate([xt[:, 1:, :width], zero], axis=1)
    return prev, nxt


def _rwkv_prep(p_rwkv, lp, tt):
    nb, t, pw = p_rwkv.shape
    hp, hn = _halo_rows(p_rwkv, tt, 1)
    tok = lambda n: pl.BlockSpec((1, tt, n), lambda b, i: (b, i, 0))
    halo = pl.BlockSpec((1, 1, 1, pw), lambda b, i: (b, i, 0, 0))
    small = [lp['mu_rwkv'], lp['k_k'], lp['k_a'], lp['r_k'], lp['w0_cat'], lp['w2_bd'], lp['a0_cat'], lp['a2_bd'],
             lp['g2_rwkv']]
    widths = [D_RWKV, D_RWKV, D_RWKV, 2 * D_RWKV, 2 * D_RWKV, 2 * D_RWKV, D_RWKV, D_RWKV]
    return pl.pallas_call(
        _rwkv_prep_body,
        grid=(nb, t // tt),
        in_specs=[tok(pw), halo, halo] + [_full(a.shape) for a in small],
        out_specs=[tok(n) for n in widths],
        out_shape=[jax.ShapeDtypeStruct((nb, t, n), F32) for n in widths],
        compiler_params=_cparams("parallel", "parallel"),
        name="rwkv_prep",
    )(p_rwkv, hp, hn, *small)


def _rwkv_chunk(reverse, r, v, kk, lw, ag, kd, s_old):
    cl = r.shape[0]
    hw = D_RWKV

    def later(rows, cols, strict):
        t = lax.broadcasted_iota(I32, (rows, cols), 0)
        s = lax.broadcasted_iota(I32, (rows, cols), 1) % cl
        dlt = (s - t) if reverse else (t - s)
        return (dlt > 0) if strict else (dlt >= 0)

    cum = _x_dot(later(cl, cl, False).astype(F32), lw)
    yield
    tot = jnp.sum(lw, axis=0, keepdims=True)
    e_neg = jnp.exp(-cum)
    e_rem = jnp.exp(tot - cum)
    at = -kk * jnp.exp(cum - lw)
    rt = r * jnp.exp(cum)
    b0 = kk * ag
    bt = b0 * e_neg
    kt = kd * e_neg
    bh = b0 * e_rem
    kh = kd * e_rem
    w_c = jnp.exp(tot)

    lane_head = lax.broadcasted_iota(I32, (1, hw), 1) // N_RWKV
    hms = [(lane_head == h).astype(F32) for h in range(H_RWKV)]
    heads = range(H_RWKV)
    bk = [jnp.concatenate([bt * hm, kt * hm], axis=0) for hm in hms]
    strict = later(cl, 2 * cl, True)
    incl = later(cl, 2 * cl, False)
    ls = [jnp.where(strict, _dot3(at, bk[h], _NT), 0.0) for h in heads]
    ly = [jnp.where(incl, _dot1(rt, bk[h], _NT), 0.0) for h in heads]
    yield

    x = [l[:, :cl] for l in ls]
    eye = (lax.broadcasted_iota(I32, (cl, cl), 0) == lax.broadcasted_iota(I32, (cl, cl), 1)).astype(F32)
    inv = [eye + xh for xh in x]
    span = 2
    while span < cl:
        x = [_dot3(xh, xh) for xh in x]
        yield
        inv = [ih + _dot3(ih, xh) for ih, xh in zip(inv, x)]
        yield
        span *= 2

    zero = jnp.zeros((cl, hw), F32)
    v_rows = jnp.concatenate([piece for hm in hms for piece in (zero, v * hm)], axis=0)
    rhs = _dot3(at, s_old, _NT) + _dot3(jnp.concatenate(ls, axis=1), v_rows)
    yield
    us = [_dot3(inv[h], rhs * hms[h]) for h in heads]
    yield
    um = us[0]
    for h in range(1, H_RWKV):
        um = um + us[h]
    uv_rows = jnp.concatenate([piece for h in heads for piece in (us[h], v * hms[h])], axis=0)
    y = _dot1(rt, s_old, _NT) + _dot1(jnp.concatenate(ly, axis=1), uv_rows)
    yield
    uv_t = jnp.concatenate([um, v], axis=0).T
    s_new = s_old * w_c + _dot3(uv_t, jnp.concatenate([bh, kh], axis=0)) * _head_ones(hw, N_RWKV)
    yield y, s_new


def _lockstep(gens):
    last = [None] * len(gens)
    live = list(range(len(gens)))
    while live:
        for i in list(live):
            try:
                last[i] = next(gens[i])
            except StopIteration:
                live.remove(i)
    return last


def _rwkv_scan_body(rf_ref, vf_ref, kkf_ref, lwf_ref, af_ref, kdf_ref, rb_ref, vb_ref, kkb_ref, lwb_ref, ab_ref,
                    kdb_ref, s0_ref, yf_ref, yb_ref, sf_ref, s_scr):
    c = pl.program_id(1)
    nc = pl.num_programs(1)

    @pl.when(c == 0)
    def _():
        s_scr[...] = s0_ref[...]

    nrow = rf_ref.shape[0]
    chains = []
    for i in range(nrow):
        chains.append(_rwkv_chunk(False, rf_ref[i], vf_ref[i], kkf_ref[i], lwf_ref[i], af_ref[i], kdf_ref[i],
                                  s_scr[i, 0]))
        chains.append(_rwkv_chunk(True, rb_ref[i], vb_ref[i], kkb_ref[i], lwb_ref[i], ab_ref[i], kdb_ref[i],
                                  s_scr[i, 1]))
    done = _lockstep(chains)
    for i in range(nrow):
        (y_f, s_f), (y_b, s_b) = done[2 * i], done[2 * i + 1]
        yf_ref[i] = y_f
        yb_ref[i] = y_b
        s_scr[i, 0] = s_f
        s_scr[i, 1] = s_b

    @pl.when(c == nc - 1)
    def _():
        sf_ref[...] = s_scr[...]


def _rwkv_scan(r, v, kk, lw, ag, kd, s0):
    nb, t, hw = r.shape
    cl = RWKV_CHUNK
    nr = RWKV_ROWS
    nc = t // cl
    fwd = lambda lane: pl.BlockSpec((nr, cl, hw), lambda b, c: (b, c, lane))
    bwd = lambda lane: pl.BlockSpec((nr, cl, hw), lambda b, c: (b, nc - 1 - c, lane))
    state = pl.BlockSpec((nr, 2, hw, hw), lambda b, c: (b, 0, 0, 0))
    y_shape = jax.ShapeDtypeStruct((nb, t, hw), F32)
    return pl.pallas_call(
        _rwkv_scan_body,
        grid=(nb // nr, nc),
        in_specs=[fwd(0)] * 6 + [bwd(0)] * 3 + [bwd(1)] * 3 + [state],
        out_specs=[fwd(0), bwd(0), state],
        out_shape=[y_shape, y_shape, jax.ShapeDtypeStruct((nb, 2, hw, hw), F32)],
        scratch_shapes=[pltpu.VMEM((nr, 2, hw, hw), F32)],
        compiler_params=_cparams("parallel", "arbitrary"),
        name="rwkv_scan",
    )(r, v, kk, lw, ag, kd, r, v, kk, lw, ag, kd, s0)


def _lru_body(reverse, p_ref, halo_ref, cw_ref, cb_ref, wa_ref, ba_ref, wx_ref, bx_ref, lam_ref, h0_ref,
              h_out, fin_out, a_scr, u_scr, carry_scr):
    i = pl.program_id(1)
    nt = pl.num_programs(1)
    tt = p_ref.shape[1]
    n_ext = tt + SUBLANES

    @pl.when(i == 0)
    def _():
        carry_scr[...] = jnp.broadcast_to(h0_ref[0], (SUBLANES, D_LRU))

    xb = p_ref[0, :, :D_LRU]
    halo = halo_ref[0, 0]
    xc = cb_ref[...] + jnp.zeros_like(xb)
    if reverse:
        xe = jnp.concatenate([xb, halo], axis=0)
        for k in range(CONV_W):
            sh = xb if k == 0 else pltpu.roll(xe, n_ext - k, 0)[:tt]
            xc = xc + cw_ref[k:k + 1, :] * sh
    else:
        xe = jnp.concatenate([halo, xb], axis=0)
        for k in range(CONV_W):
            s = CONV_W - 1 - k
            sh = xb if s == 0 else pltpu.roll(xe, s, 0)[SUBLANES:]
            xc = xc + cw_ref[k:k + 1, :] * sh
    gate_r = _sigmoid(_dot3(xc, wa_ref[...]) + ba_ref[...])
    gate_i = _sigmoid(_dot3(xc, wx_ref[...]) + bx_ref[...])
    log_a = -LRU_C * gate_r * _softplus(-lam_ref[...])
    a = jnp.exp(log_a)
    u = jnp.sqrt(-_expm1(2.0 * log_a)) * (gate_i * xc)

    rowm = lax.broadcasted_iota(I32, (tt, 1), 0) % SUBLANES
    for s in (1, 2, 4):
        if reverse:
            a_s = pltpu.roll(a, tt - s, 0)
            u_s = pltpu.roll(u, tt - s, 0)
            m = rowm < SUBLANES - s
        else:
            a_s = pltpu.roll(a, s, 0)
            u_s = pltpu.roll(u, s, 0)
            m = rowm >= s
        u = jnp.where(m, a * u_s + u, u)
        a = jnp.where(m, a * a_s, a)
    a_scr[...] = a
    u_scr[...] = u
    ng = tt // SUBLANES

    def group(g, carry):
        gi = (ng - 1 - g) if reverse else g
        r0 = pl.multiple_of(gi * SUBLANES, SUBLANES)
        hg = a_scr[pl.ds(r0, SUBLANES), :] * carry + u_scr[pl.ds(r0, SUBLANES), :]
        h_out[0, pl.ds(r0, SUBLANES), :] = hg
        edge = hg[0:1, :] if reverse else hg[SUBLANES - 1:SUBLANES, :]
        return jnp.broadcast_to(edge, (SUBLANES, D_LRU))

    carry = lax.fori_loop(0, ng, group, carry_scr[...])
    carry_scr[...] = carry

    @pl.when(i == nt - 1)
    def _():
        fin_out[0] = carry[0:1, :]


def _lru_dir(p_lru, halo, lp, d, h0, tt):
    nb, t, pw = p_lru.shape
    nt = t // tt
    reverse = d == 1
    tidx = (lambda i: nt - 1 - i) if reverse else (lambda i: i)
    small = [lp['conv_w'][d], lp['conv_b'][d:d + 1], lp['wa_bd'][d], lp['b_lru_a'][d:d + 1], lp['wx_bd'][d],
             lp['b_lru_x'][d:d + 1], lp['lam'][d:d + 1]]
    return pl.pallas_call(
        functools.partial(_lru_body, reverse),
        grid=(nb, nt),
        in_specs=[
            pl.BlockSpec((1, tt, pw), lambda b, i: (b, tidx(i), 0)),
            pl.BlockSpec((1, 1, SUBLANES, D_LRU), lambda b, i: (b, tidx(i), 0, 0)),
        ] + [_full(a.shape) for a in small] + [pl.BlockSpec((1, 1, D_LRU), lambda b, i: (b, 0, 0))],
        out_specs=[
            pl.BlockSpec((1, tt, D_LRU), lambda b, i: (b, tidx(i), 0)),
            pl.BlockSpec((1, 1, D_LRU), lambda b, i: (b, 0, 0)),
        ],
        out_shape=[jax.ShapeDtypeStruct((nb, t, D_LRU), F32), jax.ShapeDtypeStruct((nb, 1, D_LRU), F32)],
        scratch_shapes=[pltpu.VMEM((tt, D_LRU), F32), pltpu.VMEM((tt, D_LRU), F32), pltpu.VMEM((SUBLANES, D_LRU), F32)],
        compiler_params=_cparams("parallel", "arbitrary"),
        name="lru_bwd" if reverse else "lru_fwd",
    )(p_lru, halo, *small, h0)


def _gelu_tanh(x):
    return 0.5 * x * (1.0 + jnp.tanh(np.sqrt(2.0 / np.pi).astype(np.float32) * (x + 0.044715 * (x * x * x))))


def _out_proj_body(x_ref, o_ref, y0_ref, y1_ref, bonus_ref, g_ref, h0_ref, h1_ref, p_ref,
                   gmla_ref, lnw_ref, lnb_ref, glru_ref, wo_ref, gpost_ref, gate_ref, gpre_ref, sc_ref, sh_ref,
                   *rest):
    x1_out, h2_out = rest[-2:]
    o_mla = _rms(o_ref[0], gmla_ref[...])
    ones = _head_ones(D_RWKV, N_RWKV)
    y = y0_ref[0] + y1_ref[0]
    mu = _dot_x(y, ones) * (1.0 / N_RWKV)
    yc = y - mu
    var = _dot_x(yc * yc, ones) * (1.0 / N_RWKV)
    y = yc * lax.rsqrt(var + GN_EPS) * lnw_ref[...] + lnb_ref[...]
    o_rwkv = (y + bonus_ref[0]) * g_ref[0]
    o_lru = _rms((h0_ref[0] + h1_ref[0]) * _gelu_tanh(p_ref[0, :, D_LRU:]), glru_ref[...])
    m = (_dg(o_mla.astype(BF16), wo_ref[:D_MLA_OUT, :], _NN)
         + _dg(o_rwkv.astype(BF16), wo_ref[D_MLA_OUT:D_MLA_OUT + D_RWKV, :], _NN)
         + _dg(o_lru.astype(BF16), wo_ref[D_MLA_OUT + D_RWKV:, :], _NN))
    x1 = x_ref[...] + gate_ref[0] * _rms(m, gpost_ref[...])
    x1_out[...] = x1
    h2_out[...] = _rms(x1, gpre_ref[...]) * (1.0 + sc_ref[0]) + sh_ref[0]


def _out_proj(x, grp, o, y_f, y_b, bonus, g, h0, h1, p_lru, lp, mod, tt, prev):
    _, nb, t = grp
    per_batch = mod['g1'].shape[0] > 1
    tok = lambda n: pl.BlockSpec((1, tt, n), lambda b, i: (b, i, 0))
    rows = lambda: _group_rows(grp, tt)
    small = [lp['g_mla_out'], lp['ln_x_w'], lp['ln_x_b'], lp['g_lru_out'], lp['w_out'], lp['g_post_mix']]
    args = [x, o, y_f, y_b, bonus, g, h0, h1, p_lru, *small, mod['g1'], lp['g_pre_ffn'], mod['sc2'], mod['sh2']]
    in_specs = ([rows(), tok(D_MLA_OUT), tok(D_RWKV), tok(D_RWKV), tok(D_RWKV), tok(D_RWKV), tok(D_LRU), tok(D_LRU),
                 tok(P_LRU)] + [_full(a.shape) for a in small]
                + [_mod_spec(per_batch), _full(lp['g_pre_ffn'].shape), _mod_spec(per_batch), _mod_spec(per_batch)])
    aliases = {}
    if prev is not None:
        aliases = {len(args): 0, len(args) + 1: 1}
        args += list(prev)
        in_specs += [pl.BlockSpec(memory_space=pl.ANY)] * 2
    return pl.pallas_call(
        _out_proj_body,
        grid=(nb, t // tt),
        in_specs=in_specs,
        out_specs=[rows(), rows()],
        out_shape=[jax.ShapeDtypeStruct(x.shape, F32)] * 2,
        input_output_aliases=aliases,
        compiler_params=_cparams("parallel", "parallel"),
        name="out_proj",
    )(*args)


def _first_index(mask, idx, sentinel):
    return jnp.min(jnp.where(mask, idx, sentinel), axis=0, keepdims=True)


def _router_body(h_ref, wr_ref, br_ref, e_out, pos_out, loc_out, pad_out, stat_out, w_out, cnt_out, carry):
    i = pl.program_id(0)
    tn = h_ref.shape[0]

    @pl.when(i == 0)
    def _():
        carry[...] = jnp.zeros_like(carry)

    logits = lax.dot_general(wr_ref[...], h_ref[...], _NT, precision=lax.Precision.HIGHEST,
                             preferred_element_type=F32)
    scores = _sigmoid(logits)
    sel = scores + br_ref[...]

    i8 = lax.broadcasted_iota(I32, (GROUP_SIZE, tn), 0)
    blocks, gscore = [], []
    for g in range(N_GROUPS):
        blk = sel[g * GROUP_SIZE:(g + 1) * GROUP_SIZE, :]
        m1 = jnp.max(blk, axis=0, keepdims=True)
        f1 = _first_index(blk == m1, i8, GROUP_SIZE)
        m2 = jnp.max(jnp.where(i8 == f1, -jnp.inf, blk), axis=0, keepdims=True)
        blocks.append(blk)
        gscore.append(m1 + m2)
    masked = []
    for g in range(N_GROUPS):
        beaten = jnp.zeros((1, tn), I32)
        for o in range(N_GROUPS):
            if o == g:
                continue
            wins = (gscore[o] >= gscore[g]) if o < g else (gscore[o] > gscore[g])
            beaten = beaten + wins.astype(I32)
        masked.append(jnp.where(beaten < TOPK_GROUPS, blocks[g], -jnp.inf))
    cur = jnp.concatenate(masked, axis=0)

    ie = lax.broadcasted_iota(I32, (N_EXPERTS, tn), 0)
    firsts, raw = [], []
    chosen = jnp.zeros((N_EXPERTS, tn), F32)
    for _ in range(TOP_K):
        mx = jnp.max(cur, axis=0, keepdims=True)
        first = _first_index(cur == mx, ie, N_EXPERTS)
        hit = ie == first
        firsts.append(first)
        raw.append(jnp.sum(jnp.where(hit, scores, 0.0), axis=0, keepdims=True))
        cur = jnp.where(hit, -jnp.inf, cur)
        chosen = chosen + hit.astype(F32)
    total = raw[0]
    for k in range(1, TOP_K):
        total = total + raw[k]

    before = (lax.broadcasted_iota(I32, (tn, tn), 0) < lax.broadcasted_iota(I32, (tn, tn), 1)).astype(BF16)
    rank_tile = _dg(chosen.astype(BF16), before, _NN)
    old_carry = carry[...]
    rank = rank_tile + old_carry[:, 0:1]
    tile_cnt = jnp.broadcast_to(jnp.sum(chosen, axis=1, keepdims=True), old_carry.shape)
    lower = (lax.broadcasted_iota(I32, (N_EXPERTS, N_EXPERTS), 1)
             < lax.broadcasted_iota(I32, (N_EXPERTS, N_EXPERTS), 0)).astype(F32)
    tile_off = _x_dot(lower, tile_cnt)
    local = rank_tile + tile_off[:, 0:1]
    tile_chunks = jnp.floor((tile_cnt + (DISPATCH_CHUNK - 1)) * (1.0 / DISPATCH_CHUNK))
    chunk_off = _x_dot(lower, tile_chunks)
    padded = rank_tile + chunk_off[:, 0:1] * DISPATCH_CHUNK
    e_out[...] = jnp.zeros_like(e_out)
    pos_out[...] = jnp.zeros_like(pos_out)
    loc_out[...] = jnp.zeros_like(loc_out)
    pad_out[...] = jnp.zeros_like(pad_out)
    w_out[...] = jnp.zeros_like(w_out)
    for k in range(TOP_K):
        hit = ie == firsts[k]
        e_out[0, k:k + 1, :] = firsts[k]
        pos_out[0, k:k + 1, :] = jnp.sum(jnp.where(hit, rank, 0.0), axis=0, keepdims=True).astype(I32)
        loc_out[0, k:k + 1, :] = jnp.sum(jnp.where(hit, local, 0.0), axis=0, keepdims=True).astype(I32)
        pad_out[0, k:k + 1, :] = jnp.sum(jnp.where(hit, padded, 0.0), axis=0, keepdims=True).astype(I32)
        w_out[k:k + 1, :] = raw[k] / total * ROUTED_SCALE
    lane = lax.broadcasted_iota(I32, old_carry.shape, 1)
    stat_out[0] = jnp.where(lane == 0, tile_cnt, jnp.where(lane == 1, tile_off, jnp.where(
        lane == 2, old_carry, chunk_off))).astype(I32)
    new_carry = old_carry + tile_cnt
    carry[...] = new_carry
    cnt_out[...] = new_carry


def _router(h2, wr_t, b_col):
    n, d = h2.shape
    tn = ROUTER_TILE
    row_spec = pl.BlockSpec((SLOT_ROWS, tn), lambda i: (0, i))
    tile_spec = pl.BlockSpec((1, SLOT_ROWS, tn), lambda i: (i, 0, 0))
    cnt_spec = pl.BlockSpec((N_EXPERTS, 128), lambda i: (0, 0))
    tile_shape = jax.ShapeDtypeStruct((n // tn, SLOT_ROWS, tn), I32)
    return pl.pallas_call(
        _router_body,
        grid=(n // tn,),
        in_specs=[pl.BlockSpec((tn, d), lambda i: (i, 0)), _full(wr_t.shape), _full(b_col.shape)],
        out_specs=[tile_spec, tile_spec, tile_spec, tile_spec,
                   pl.BlockSpec((1, N_EXPERTS, 128), lambda i: (i, 0, 0)), row_spec, cnt_spec],
        out_shape=[tile_shape, tile_shape, tile_shape, tile_shape,
                   jax.ShapeDtypeStruct((n // tn, N_EXPERTS, 128), I32),
                   jax.ShapeDtypeStruct((SLOT_ROWS, n), F32), jax.ShapeDtypeStruct((N_EXPERTS, 128), F32)],
        scratch_shapes=[pltpu.VMEM((N_EXPERTS, 128), F32)],
        compiler_params=_cparams("arbitrary"),
        name="router",
    )(h2, wr_t, b_col)


ROW_LANES = 128
ROW_SUB = D_MODEL // 2 // ROW_LANES
U32 = jnp.uint32
HIGH_HALF = np.uint32(0xFFFF0000)


def _bf16_bits(x):
    return lax.bitcast_convert_type(x.astype(BF16).astype(F32), U32)


def _to_row_tiles(ref, x):
    rows, d = x.shape
    words = (_bf16_bits(x[:, :d // 2]) >> 16) | (_bf16_bits(x[:, d // 2:]) & HIGH_HALF)
    for s in range(ROW_SUB):
        ref[pl.ds(s, rows, stride=ROW_SUB), :] = words[:, s * ROW_LANES:(s + 1) * ROW_LANES]


def _from_row_tiles(ref, rows):
    words = jnp.concatenate([ref[pl.ds(s, rows, stride=ROW_SUB), :] for s in range(ROW_SUB)], axis=1)
    low = lax.bitcast_convert_type(words << 16, F32)
    high = lax.bitcast_convert_type(words & HIGH_HALF, F32)
    return jnp.concatenate([low, high], axis=1)


def _row_copy(src, src_off, dst, dst_off, sem):
    return pltpu.make_async_copy(src.at[pl.ds(pl.multiple_of(src_off, ROW_SUB), ROW_SUB), :],
                                 dst.at[pl.ds(pl.multiple_of(dst_off, ROW_SUB), ROW_SUB), :], sem)


def _load_slots(slot_ref, slot_smem, sem):
    cp = pltpu.make_async_copy(slot_ref.at[0], slot_smem, sem)
    cp.start()
    cp.wait()


def _chunk_copy(src, src_row, dst, dst_row, sem):
    n = DISPATCH_CHUNK * ROW_SUB
    return pltpu.make_async_copy(src.at[pl.ds(pl.multiple_of(src_row * ROW_SUB, ROW_SUB), n), :],
                                 dst.at[pl.ds(pl.multiple_of(dst_row * ROW_SUB, ROW_SUB), n), :], sem)


def _dispatch_body(start_ref, e_ref, pos_ref, loc_ref, stat_ref, h_ref, xs_in, slots_out, xs_out,
                   stat_smem, pending, staging, sem_s, sem):
    del xs_in
    i = pl.program_id(0)
    last = pl.num_programs(0) - 1
    tn = h_ref.shape[0]
    rows = TOP_K * tn
    cur = i % 2

    e = e_ref[0]
    slot = pos_ref[0]
    for x in range(N_EXPERTS):
        slot = slot + jnp.where(e == x, start_ref[x], 0)
    slots_out[0] = slot * ROW_SUB
    stats = pltpu.make_async_copy(stat_ref.at[0], stat_smem, sem_s)
    stats.start()

    @pl.when(i == 0)
    def _():
        pending[0] = 0
        pad = jnp.zeros((DISPATCH_CHUNK * ROW_SUB, ROW_LANES), U32)
        for b in range(2):
            staging[b, pl.ds(rows * ROW_SUB, DISPATCH_CHUNK * ROW_SUB), :] = pad

    loc = loc_ref[0]
    p_iota = lax.broadcasted_iota(I32, (rows, tn), 0)
    pick = jnp.zeros((rows, tn), F32)
    for k in range(TOP_K):
        pick = jnp.where(p_iota == loc[k:k + 1, :], 1.0, pick)
    _to_row_tiles(staging.at[cur], _dg(pick.astype(BF16), h_ref[...].astype(BF16), _NN))

    def drain(count):
        def one(c, carry):
            _chunk_copy(staging.at[0], 0, xs_out, 0, sem).wait()
            return carry
        lax.fori_loop(0, count, one, 0)

    drain(pending[0])
    stats.wait()

    def per_expert(x, total):
        count = stat_smem[x, 0]
        src = stat_smem[x, 1]
        dst = start_ref[x] + stat_smem[x, 2]
        n_chunks = lax.shift_right_logical(count + (DISPATCH_CHUNK - 1), DISPATCH_SHIFT)

        def one(c, carry):
            _chunk_copy(staging.at[cur], src + c * DISPATCH_CHUNK, xs_out, dst + c * DISPATCH_CHUNK, sem).start()
            return carry

        lax.fori_loop(0, n_chunks, one, 0)
        return total + n_chunks

    issued = lax.fori_loop(0, N_EXPERTS, per_expert, 0)
    pending[0] = issued

    @pl.when(i == last)
    def _():
        drain(issued)


def _zero_tail_body(lb_ref, o_ref):
    del lb_ref
    o_ref[...] = jnp.zeros_like(o_ref)


def _zero_tails(last_block, n_blocks):
    rows = MOE_BLOCK * ROW_SUB
    grid_spec = pltpu.PrefetchScalarGridSpec(
        num_scalar_prefetch=1,
        grid=(last_block.shape[0],),
        in_specs=[],
        out_specs=pl.BlockSpec((rows, ROW_LANES), lambda e, lb: (lb[e], 0)),
    )
    return pl.pallas_call(
        _zero_tail_body,
        grid_spec=grid_spec,
        out_shape=jax.ShapeDtypeStruct((n_blocks * rows, ROW_LANES), U32),
        compiler_params=_cparams("arbitrary"),
        name="zero_tails",
    )(last_block)


def _dispatch(row_start, e3, pos3, loc3, stats, h2, xs_zero):
    n, d = h2.shape
    tn = e3.shape[2]
    tile = lambda: pl.BlockSpec((1, SLOT_ROWS, tn), lambda i, st: (i, 0, 0))
    grid_spec = pltpu.PrefetchScalarGridSpec(
        num_scalar_prefetch=1,
        grid=(n // tn,),
        in_specs=[tile(), tile(), tile(), pl.BlockSpec((1, N_EXPERTS, 128), lambda i, st: (i, 0, 0)),
                  pl.BlockSpec((tn, d), lambda i, st: (i, 0)), pl.BlockSpec(memory_space=pl.ANY)],
        out_specs=[tile(), pl.BlockSpec(memory_space=pl.ANY)],
        scratch_shapes=[pltpu.SMEM((N_EXPERTS, 128), I32), pltpu.SMEM((1,), I32),
                        pltpu.VMEM((2, (TOP_K * tn + DISPATCH_CHUNK) * ROW_SUB, ROW_LANES), U32),
                        pltpu.SemaphoreType.DMA, pltpu.SemaphoreType.DMA],
    )
    return pl.pallas_call(
        _dispatch_body,
        grid_spec=grid_spec,
        out_shape=[jax.ShapeDtypeStruct((n // tn, SLOT_ROWS, tn), I32),
                   jax.ShapeDtypeStruct(xs_zero.shape, xs_zero.dtype)],
        input_output_aliases={6: 1},
        compiler_params=_cparams("arbitrary"),
        name="dispatch",
    )(row_start, e3, pos3, loc3, stats, h2, xs_zero)


def _swiglu(x, wg, wu, wd):
    g = _dg(x, wg, _NN)
    u = _dg(x, wu, _NN)
    return _dg((g * _sigmoid(g) * u).astype(BF16), wd, _NN)


def _expert_body(be_ref, nu_ref, xs_ref, wg_ref, wu_ref, wd_ref, ys_ref):
    del be_ref
    b = pl.program_id(0)

    @pl.when(b < nu_ref[0])
    def _():
        x = _from_row_tiles(xs_ref, MOE_BLOCK).astype(BF16)
        _to_row_tiles(ys_ref, _swiglu(x, wg_ref[0], wu_ref[0], wd_ref[0]))

    @pl.when(b >= nu_ref[0])
    def _():
        ys_ref[...] = jnp.zeros_like(ys_ref)


def _experts(block_e, n_used, xs, wg, wu, wd):
    bm = MOE_BLOCK
    _, d, de = wg.shape
    rows = bm * ROW_SUB
    grid_spec = pltpu.PrefetchScalarGridSpec(
        num_scalar_prefetch=2,
        grid=(xs.shape[0] // rows,),
        in_specs=[
            pl.BlockSpec((rows, ROW_LANES), lambda b, be, nu: (jnp.minimum(b, nu[0] - 1), 0)),
            pl.BlockSpec((1, d, de), lambda b, be, nu: (be[b], 0, 0)),
            pl.BlockSpec((1, d, de), lambda b, be, nu: (be[b], 0, 0)),
            pl.BlockSpec((1, de, d), lambda b, be, nu: (be[b], 0, 0)),
        ],
        out_specs=pl.BlockSpec((rows, ROW_LANES), lambda b, be, nu: (b, 0)),
    )
    return pl.pallas_call(
        _expert_body,
        grid_spec=grid_spec,
        out_shape=jax.ShapeDtypeStruct(xs.shape, xs.dtype),
        compiler_params=_cparams("arbitrary"),
        name="experts",
    )(block_e, n_used, xs, wg, wu, wd)


def _combine_body(start_ref, pad_ref, stat0_ref, statn_ref, w_ref, x1_ref, h2_ref, gate_ref, ys_hbm, wsg_ref,
                  wsu_ref, wsd_ref, gpost_ref, out_ref, stat_smem, pending, staging, sem_s, sem):
    i = pl.program_id(0)
    last = pl.num_programs(0) - 1
    tn = x1_ref.shape[0]
    cur = i % 2

    def fetch(stat_ref, which):
        stats = pltpu.make_async_copy(stat_ref.at[0], stat_smem, sem_s)
        stats.start()
        stats.wait()

        def per_expert(x, total):
            count = stat_smem[x, 0]
            src = start_ref[x] + stat_smem[x, 2]
            dst = stat_smem[x, 3] * DISPATCH_CHUNK
            n_chunks = lax.shift_right_logical(count + (DISPATCH_CHUNK - 1), DISPATCH_SHIFT)

            def one(c, carry):
                _chunk_copy(ys_hbm, src + c * DISPATCH_CHUNK, staging.at[which], dst + c * DISPATCH_CHUNK,
                            sem.at[which]).start()
                return carry

            lax.fori_loop(0, n_chunks, one, 0)
            return total + n_chunks

        pending[which] = lax.fori_loop(0, N_EXPERTS, per_expert, 0)

    @pl.when(i == 0)
    def _():
        staging[...] = jnp.zeros_like(staging)
        fetch(stat0_ref, 0)

    @pl.when(i < last)
    def _():
        fetch(statn_ref, 1 - cur)

    acc = _swiglu(h2_ref[...].astype(BF16), wsg_ref[...], wsu_ref[...], wsd_ref[...])

    def drain(c, carry):
        _chunk_copy(ys_hbm, 0, staging.at[0], 0, sem.at[cur]).wait()
        return carry

    lax.fori_loop(0, pending[cur], drain, 0)
    piece = COMBINE_ROWS // COMBINE_PIECES
    for q in range(COMBINE_PIECES):
        col = lax.broadcasted_iota(I32, (tn, piece), 1) + q * piece
        weight = jnp.zeros((tn, piece), F32)
        for k in range(TOP_K):
            weight = jnp.where(col == pad_ref[:, k:k + 1], w_ref[:, k:k + 1], weight)
        rows = _from_row_tiles(staging.at[cur, pl.ds(q * piece * ROW_SUB, piece * ROW_SUB), :], piece)
        acc = acc + _dg(weight.astype(BF16), rows.astype(BF16), _NN)
    out_ref[...] = x1_ref[...] + gate_ref[0] * _rms(acc, gpost_ref[...])


def _combine(row_start, pad_tok, stats, w_tok, x1, h2, gate_rows, ys, wsg, wsu, wsd, gpost):
    n, d = x1.shape
    tn = TOKEN_TILE
    tok = pl.BlockSpec((tn, d), lambda i, st: (i, 0))
    small = pl.BlockSpec((tn, SLOT_ROWS), lambda i, st: (i, 0))
    ntiles = n // tn
    grid_spec = pltpu.PrefetchScalarGridSpec(
        num_scalar_prefetch=1,
        grid=(ntiles,),
        in_specs=[small, pl.BlockSpec((1, N_EXPERTS, 128), lambda i, st: (0, 0, 0)),
                  pl.BlockSpec((1, N_EXPERTS, 128), lambda i, st: (jnp.minimum(i + 1, ntiles - 1), 0, 0)),
                  small, tok, tok,
                  pl.BlockSpec((1, 1, d), lambda i, st: (i, 0, 0)),
                  pl.BlockSpec(memory_space=pl.ANY),
                  pl.BlockSpec(wsg.shape, lambda i, st: (0, 0)), pl.BlockSpec(wsu.shape, lambda i, st: (0, 0)),
                  pl.BlockSpec(wsd.shape, lambda i, st: (0, 0)), pl.BlockSpec(gpost.shape, lambda i, st: (0, 0))],
        out_specs=tok,
        scratch_shapes=[pltpu.SMEM((N_EXPERTS, 128), I32), pltpu.SMEM((2,), I32),
                        pltpu.VMEM((2, COMBINE_ROWS * ROW_SUB, ROW_LANES), U32),
                        pltpu.SemaphoreType.DMA, pltpu.SemaphoreType.DMA((2,))],
    )
    return pl.pallas_call(
        _combine_body,
        grid_spec=grid_spec,
        out_shape=jax.ShapeDtypeStruct((n, d), F32),
        compiler_params=_cparams("arbitrary"),
        name="combine",
    )(row_start, pad_tok, stats, stats, w_tok, x1, h2, gate_rows, ys, wsg, wsu, wsd, gpost)


def _moe(x1, h2, gate_rows, lp):
    n, d = h2.shape
    bm = MOE_BLOCK
    tn = TOKEN_TILE
    e3, pos3, loc3, pad3, stats, w_rows, cnt = _router(h2, lp['wr_t'], lp['b_router_col'])
    counts = cnt[:, 0].astype(I32)
    blocks_per_e = (counts + DISPATCH_CHUNK + bm - 1) // bm
    blk_end = jnp.cumsum(blocks_per_e)
    blk_start = blk_end - blocks_per_e
    n_blocks = -(-(n * TOP_K + N_EXPERTS * DISPATCH_CHUNK) // bm) + N_EXPERTS
    block_e = jnp.minimum(jnp.sum(blk_end[None, :] <= jnp.arange(n_blocks, dtype=I32)[:, None], axis=1),
                          N_EXPERTS - 1).astype(I32)
    n_used = blk_end[-1:].astype(I32)
    partial = jnp.stack([blk_start + counts // bm, blk_end - 1], axis=1).reshape(-1)
    xs_tails = _zero_tails(jnp.clip(partial, 0, n_blocks - 1).astype(I32), n_blocks)
    row_start = (blk_start * bm).astype(I32)
    _, xs = _dispatch(row_start, e3, pos3, loc3, stats, h2, xs_tails)
    ys = _experts(block_e, n_used, xs, lp['w_e_gate'], lp['w_e_up'], lp['w_e_down'])
    pad_tok = pad3.transpose(0, 2, 1).reshape(n, SLOT_ROWS)
    return _combine(row_start, pad_tok, stats, w_rows.T, x1, h2, gate_rows, ys, lp['w_s_gate'], lp['w_s_up'],
                    lp['w_s_down'], lp['g_post_ffn'])


def _rope_tables(n_tokens):
    rows = n_tokens // GRID_W
    row = jnp.repeat(jnp.arange(rows, dtype=F32), GRID_W)
    col = jnp.tile(jnp.arange(GRID_W, dtype=F32), rows)
    n_freq = ROPE_DIM // 4
    inv_freq = ROPE_THETA ** (-jnp.arange(n_freq, dtype=F32) / n_freq)
    ang_r = row[:, None] * inv_freq
    ang_c = col[:, None] * inv_freq
    ang = jnp.concatenate([ang_r, ang_r, ang_c, ang_c], axis=-1)
    cos, sin = jnp.cos(ang), jnp.sin(ang)
    pad = lambda z, fill: jnp.concatenate(
        [jnp.full((n_tokens, NOPE_DIM), fill, F32), z, jnp.zeros((n_tokens, HEAD_PAD - QK_DIM), F32)], axis=1)
    return pad(cos, 1.0), pad(sin, 0.0), pad(cos, 0.0), pad(sin, 0.0)


def _rot_cols(w):
    q = ROPE_DIM // 4
    perm = np.concatenate([np.arange(q, 2 * q), np.arange(0, q), np.arange(3 * q, 4 * q), np.arange(2 * q, 3 * q)])
    sign = np.concatenate([-np.ones(q), np.ones(q), -np.ones(q), np.ones(q)]).astype(np.float32)
    return w[..., perm] * sign


def _block_diag(blocks):
    h, m, n = blocks.shape[-3:]
    eye = jnp.eye(h, dtype=blocks.dtype)
    out = blocks[..., :, :, None, :] * eye[:, None, :, None]
    return out.reshape(blocks.shape[:-3] + (h * m, h * n))


def _diag_blocks(mat, h):
    m, n = mat.shape[-2] // h, mat.shape[-1] // h
    z = mat.reshape(mat.shape[:-2] + (h, m, h, n))
    return jnp.stack([z[..., i, :, i, :] for i in range(h)], axis=-3)


def _layer_operands(l, a):
    w_in = a['w_in'][l]
    d = w_in.shape[0]
    p_mla = Q_LORA + KV_LORA + ROPE_DIM
    w_kr = w_in[:, Q_LORA + KV_LORA:p_mla]
    z32 = jnp.zeros((d, ROPE_DIM), F32)
    w_mla = jnp.concatenate([w_in[:, :Q_LORA + KV_LORA], z32, _rot_cols(w_kr), w_kr, z32], axis=1)
    wq3 = a['w_q_b'][l].reshape(Q_LORA, H_MLA, QK_DIM)
    zq = lambda n: jnp.zeros((Q_LORA, H_MLA, n), F32)
    wq = jnp.concatenate([wq3, zq(HEAD_PAD - QK_DIM)], axis=2).reshape(Q_LORA, H_MLA * HEAD_PAD)
    wqr = jnp.concatenate([zq(NOPE_DIM), _rot_cols(wq3[:, :, NOPE_DIM:]), zq(HEAD_PAD - QK_DIM)],
                          axis=2).reshape(Q_LORA, H_MLA * HEAD_PAD)
    wkv3 = a['w_kv_b'][l].reshape(KV_LORA, H_MLA, NOPE_DIM + V_DIM)
    wk = jnp.concatenate([wkv3[:, :, :NOPE_DIM], jnp.zeros((KV_LORA, H_MLA, HEAD_PAD - NOPE_DIM), F32)],
                         axis=2).reshape(KV_LORA, H_MLA * HEAD_PAD)
    wv = wkv3[:, :, NOPE_DIM:].reshape(KV_LORA, H_MLA * V_DIM)
    row = lambda z: z.reshape(1, -1)
    two = lambda z: jnp.concatenate([z[0], z[1]], axis=-1)
    bd2 = lambda z: jnp.concatenate([jnp.concatenate([z[0], jnp.zeros_like(z[0])], axis=1),
                                     jnp.concatenate([jnp.zeros_like(z[1]), z[1]], axis=1)], axis=0)
    return {
        'g_pre_mix': row(a['g_pre_mix'][l]), 'g_post_mix': row(a['g_post_mix'][l]),
        'g_pre_ffn': row(a['g_pre_ffn'][l]), 'g_post_ffn': row(a['g_post_ffn'][l]),
        'w_mla': w_mla.astype(BF16),
        'w_rwkv': w_in[:, p_mla:p_mla + P_RWKV].astype(BF16),
        'w_lru': w_in[:, p_mla + P_RWKV:].astype(BF16),
        'g_q_a': row(a['g_q_a'][l]), 'g_kv_a': row(a['g_kv_a'][l]),
        'wq': wq.astype(BF16), 'wqr': wqr.astype(BF16), 'wk': wk.astype(BF16), 'wv': wv.astype(BF16),
        'g_mla_out': row(a['g_mla_out'][l]),
        'mu_rwkv': row(a['mu_rwkv'][l]), 'k_k': row(a['k_k'][l]), 'k_a': row(a['k_a'][l]), 'r_k': row(a['r_k'][l]),
        'w0_cat': row(two(a['w0_rwkv'][l])), 'w2_bd': bd2(a['w2_rwkv'][l]),
        'a0_cat': row(two(a['a0_rwkv'][l])), 'a2_bd': bd2(a['a2_rwkv'][l]),
        'g2_rwkv': a['g2_rwkv'][l], 'ln_x_w': row(a['ln_x_w'][l]), 'ln_x_b': row(a['ln_x_b'][l]),
        'conv_w': a['conv_w'][l], 'conv_b': a['conv_b'][l],
        'wa_bd': _block_diag(a['w_lru_a'][l]), 'b_lru_a': a['b_lru_a'][l],
        'wx_bd': _block_diag(a['w_lru_x'][l]), 'b_lru_x': a['b_lru_x'][l],
        'lam': a['lam'][l], 'g_lru_out': row(a['g_lru_out'][l]),
        'w_out': a['w_out'][l].astype(BF16),
        'wr_t': a['w_router'][l].T, 'b_router_col': a['b_router'][l].reshape(N_EXPERTS, 1),
        'w_e_gate': a['w_e_gate'][l].astype(BF16), 'w_e_up': a['w_e_up'][l].astype(BF16),
        'w_e_down': a['w_e_down'][l].astype(BF16),
        'w_s_gate': a['w_s_gate'][l].astype(BF16), 'w_s_up': a['w_s_up'][l].astype(BF16),
        'w_s_down': a['w_s_down'][l].astype(BF16),
    }


def _mixer(x, grp, mod, lp, ctx, tables, prev):
    _, nb, t = grp
    tt = min(t, 512)
    p_mla, p_rwkv, p_lru = _in_proj(x, grp, lp['g_pre_mix'], mod['sc1'], mod['sh1'], lp['w_mla'], lp['w_rwkv'],
                                    lp['w_lru'], tt)
    if ctx is None:
        q, k, v, new_cache = _mla_prep(p_mla, lp['g_q_a'], lp['g_kv_a'], lp['wq'], None, lp['wk'], lp['wv'], None, tt)
        kc = vc = None
        s0 = jnp.zeros((nb, 2, D_RWKV, D_RWKV), F32)
        h0 = jnp.zeros((nb, 2, 1, D_LRU), F32)
    else:
        cache, s_rwkv, s_lru = ctx
        q, k, v = _mla_prep(p_mla, lp['g_q_a'], lp['g_kv_a'], lp['wq'], lp['wqr'], lp['wk'], lp['wv'], tables, tt)
        new_cache = None
        kr_placed = jnp.pad(cache[..., KV_LORA:], ((0, 0), (0, 0), (NOPE_DIM, HEAD_PAD - QK_DIM)))
        kc, vc = _ctx_kv(cache[..., :KV_LORA], kr_placed, lp['wk'], lp['wv'])
        s0 = _block_diag(s_rwkv)
        h0 = s_lru[:, :, None, :]
    o = _attention(q, k, v, kc, vc, min(t, 256))

    r, vv, kk, lw, ag, kd, g, bonus = _rwkv_prep(p_rwkv, lp, tt)
    y_f, y_b, s_fin = _rwkv_scan(r, vv, kk, lw, ag, kd, s0)

    hp, hn = _halo_rows(p_lru[..., :D_LRU], tt, SUBLANES)
    h_f, fin_f = _lru_dir(p_lru, hp, lp, 0, h0[:, 0], tt)
    h_b, fin_b = _lru_dir(p_lru, hn, lp, 1, h0[:, 1], tt)

    x1, h2 = _out_proj(x, grp, o, y_f, y_b, bonus, g, h_f, h_b, p_lru, lp, mod, tt, prev)
    states = None
    if ctx is None:
        states = (new_cache, _diag_blocks(s_fin, H_RWKV), jnp.concatenate([fin_f, fin_b], axis=1))
    return x1, h2, states


def kernel(x_prompt, x_sample, c, cache_mla, state_rwkv, state_lru, c_ctx, w_ada, b_ada, g_pre_mix, g_post_mix, g_pre_ffn, g_post_ffn, w_in, g_q_a, w_q_b, g_kv_a, w_kv_b, g_mla_out, mu_rwkv, w0_rwkv, w2_rwkv, a0_rwkv, a2_rwkv, g2_rwkv, k_k, k_a, r_k, ln_x_w, ln_x_b, conv_w, conv_b, w_lru_a, b_lru_a, w_lru_x, b_lru_x, lam, g_lru_out, w_out, w_router, b_router, w_e_gate, w_e_up, w_e_down, w_s_gate, w_s_up, w_s_down):
    a = dict(w_in=w_in, g_pre_mix=g_pre_mix, g_post_mix=g_post_mix, g_pre_ffn=g_pre_ffn, g_post_ffn=g_post_ffn,
             g_q_a=g_q_a, w_q_b=w_q_b, g_kv_a=g_kv_a, w_kv_b=w_kv_b, g_mla_out=g_mla_out, mu_rwkv=mu_rwkv,
             w0_rwkv=w0_rwkv, w2_rwkv=w2_rwkv, a0_rwkv=a0_rwkv, a2_rwkv=a2_rwkv, g2_rwkv=g2_rwkv, k_k=k_k, k_a=k_a,
             r_k=r_k.reshape(r_k.shape[0], -1), ln_x_w=ln_x_w, ln_x_b=ln_x_b, conv_w=conv_w, conv_b=conv_b,
             w_lru_a=w_lru_a, b_lru_a=b_lru_a, w_lru_x=w_lru_x, b_lru_x=b_lru_x, lam=lam, g_lru_out=g_lru_out,
             w_out=w_out, w_router=w_router, b_router=b_router, w_e_gate=w_e_gate, w_e_up=w_e_up,
             w_e_down=w_e_down, w_s_gate=w_s_gate, w_s_up=w_s_up, w_s_down=w_s_down)
    n_layers = w_in.shape[0]
    nbp, tp, d = x_prompt.shape
    nbs, ts, _ = x_sample.shape
    rows = -(-(1 + nbs) // SUBLANES) * SUBLANES
    cvecs = jnp.concatenate([c_ctx[None, :], c, jnp.zeros((rows - 1 - nbs, d), F32)], axis=0)
    mods = _ada(cvecs, w_ada, b_ada)
    names = ('sh1', 'sc1', 'g1', 'sh2', 'sc2', 'g2')
    tables = _rope_tables(ts)
    tn = TOKEN_TILE

    grp_p, grp_s = (0, nbp, tp), (nbp * tp, nbs, ts)
    x = jnp.concatenate([x_prompt.reshape(-1, d), x_sample.reshape(-1, d)], axis=0)
    caches, rwkv_states, lru_states = [], [], []
    for l in range(n_layers):
        lp = _layer_operands(l, a)
        mod_p = {nm: mods[l, 0:1, i * d:(i + 1) * d][:, None, :] for i, nm in enumerate(names)}
        mod_s = {nm: mods[l, 1:1 + nbs, i * d:(i + 1) * d][:, None, :] for i, nm in enumerate(names)}
        x1, h2, states = _mixer(x, grp_p, mod_p, lp, None, None, None)
        x1, h2, _ = _mixer(x, grp_s, mod_s, lp, (cache_mla[:, l], state_rwkv[:, l], state_lru[:, l]), tables,
                           (x1, h2))
        caches.append(states[0])
        rwkv_states.append(states[1])
        lru_states.append(states[2])
        gate_rows = jnp.concatenate([jnp.repeat(mod_p['g2'], nbp * tp // tn, axis=0),
                                     jnp.repeat(mod_s['g2'], ts // tn, axis=0)], axis=0)
        x = _moe(x1, h2, gate_rows, lp)
    yp = x[:nbp * tp].reshape(nbp, tp, d)
    ys = x[nbp * tp:].reshape(nbs, ts, d)
    return (yp, ys, jnp.stack(caches, axis=1), jnp.stack(rwkv_states, axis=1), jnp.stack(lru_states, axis=1))
```

```python
import functools

import numpy as np
import jax
import jax.numpy as jnp
from jax import lax
from jax.experimental import pallas as pl
from jax.experimental.pallas import tpu as pltpu

F32 = jnp.float32
BF16 = jnp.bfloat16
I32 = jnp.int32

D_MODEL = 1024
GRID_W = 64
EPS = 1e-6
H_MLA = 4
Q_LORA = 256
KV_LORA = 128
NOPE_DIM = 64
ROPE_DIM = 32
V_DIM = 128
QK_DIM = NOPE_DIM + ROPE_DIM
ROPE_THETA = 10000.0
HEAD_PAD = 128
H_RWKV = 4
N_RWKV = 64
D_RWKV = H_RWKV * N_RWKV
W_LORA = 64
A_LORA = 64
G_LORA = 128
GN_EPS = 64e-5
D_LRU = 256
H_LRU = 4
BS_LRU = D_LRU // H_LRU
CONV_W = 4
LRU_C = 8.0
D_MLA_OUT = H_MLA * V_DIM
P_MLA_PAD = Q_LORA + KV_LORA + HEAD_PAD
P_RWKV = 3 * D_RWKV + 2 * W_LORA + 2 * A_LORA + G_LORA
P_LRU = 2 * D_LRU
N_EXPERTS = 64
TOP_K = 6
N_GROUPS = 8
TOPK_GROUPS = 4
GROUP_SIZE = N_EXPERTS // N_GROUPS
D_EXPERT = 256
ROUTED_SCALE = 2.5

SUBLANES = 8
RWKV_CHUNK = 64
RWKV_ROWS = 2
MOE_BLOCK = 512
ROUTER_TILE = 256
DISPATCH_SHIFT = 4
DISPATCH_CHUNK = 1 << DISPATCH_SHIFT
TOKEN_TILE = 256
COMBINE_ROWS = TOP_K * TOKEN_TILE + N_EXPERTS * DISPATCH_CHUNK
COMBINE_PIECES = 4
SLOT_ROWS = 8
VMEM_LIMIT = 56 * 1024 * 1024


def _cparams(*sem):
    return pltpu.CompilerParams(dimension_semantics=sem, vmem_limit_bytes=VMEM_LIMIT)


def _sigmoid(x):
    return 1.0 / (1.0 + jnp.exp(-x))


def _softplus(x):
    return jnp.maximum(x, 0.0) + jnp.log1p(jnp.exp(-jnp.abs(x)))


def _expm1(z):
    e = jnp.exp(z)
    direct = (e == 1.0) | (z < -1.0)
    corrected = (e - 1.0) * z / jnp.log(jnp.where(direct, 2.0, e))
    return jnp.where(e == 1.0, z, jnp.where(z < -1.0, e - 1.0, corrected))


def _rms(x, g):
    return x * lax.rsqrt(jnp.mean(x * x, axis=-1, keepdims=True) + EPS) * g


def _split2(x):
    hi = x.astype(BF16)
    lo = (x - hi.astype(F32)).astype(BF16)
    return hi, lo


def _split3(x):
    hi = x.astype(BF16)
    r1 = x - hi.astype(F32)
    mid = r1.astype(BF16)
    lo = (r1 - mid.astype(F32)).astype(BF16)
    return hi, mid, lo


_NN = (((1,), (0,)), ((), ()))
_NT = (((1,), (1,)), ((), ()))


def _dg(a, b, dims):
    return lax.dot_general(a, b, dims, preferred_element_type=F32)


def _dot1(a, b, dims=_NN):
    return _dg(a.astype(BF16), b.astype(BF16), dims)


def _dot3(a, b, dims=_NN):
    ah, al = _split2(a)
    bh, bl = _split2(b)
    return _dg(ah, bh, dims) + (_dg(ah, bl, dims) + _dg(al, bh, dims))


def _dot_x(a, b_exact, dims=_NN):
    h, m, l = _split3(a)
    bb = b_exact.astype(BF16)
    return _dg(h, bb, dims) + (_dg(m, bb, dims) + _dg(l, bb, dims))


def _x_dot(a_exact, b, dims=_NN):
    h, m, l = _split3(b)
    aa = a_exact.astype(BF16)
    return _dg(aa, h, dims) + (_dg(aa, m, dims) + _dg(aa, l, dims))


def _head_ones(n, seg):
    r = lax.broadcasted_iota(I32, (n, n), 0) // seg
    c = lax.broadcasted_iota(I32, (n, n), 1) // seg
    return (r == c).astype(F32)


def _ada_body(c_ref, w_ref, b_ref, o_ref):
    cv = c_ref[...]
    s = cv * _sigmoid(cv)
    o_ref[0] = _dot3(s, w_ref[0]) + b_ref[0]


def _ada(cvecs, w_ada, b_ada):
    n_layers, d, d6 = w_ada.shape
    rows = cvecs.shape[0]
    nt = d6 // d
    return pl.pallas_call(
        _ada_body,
        grid=(n_layers, nt),
        in_specs=[
            pl.BlockSpec((rows, d), lambda l, j: (0, 0)),
            pl.BlockSpec((1, d, d), lambda l, j: (l, 0, j)),
            pl.BlockSpec((1, 1, d), lambda l, j: (l, 0, j)),
        ],
        out_specs=pl.BlockSpec((1, rows, d), lambda l, j: (l, 0, j)),
        out_shape=jax.ShapeDtypeStruct((n_layers, rows, d6), F32),
        compiler_params=_cparams("parallel", "parallel"),
        name="ada",
    )(cvecs, w_ada, b_ada.reshape(n_layers, 1, d6))


def _mod_spec(per_batch):
    if per_batch:
        return pl.BlockSpec((1, 1, D_MODEL), lambda b, i: (b, 0, 0))
    return pl.BlockSpec((1, 1, D_MODEL), lambda b, i: (0, 0, 0))


def _full(shape):
    nd = len(shape)
    return pl.BlockSpec(shape, lambda *_: (0,) * nd)


def _in_proj_body(x_ref, g_ref, sc_ref, sh_ref, w1_ref, w2_ref, w3_ref, o1_ref, o2_ref, o3_ref):
    h = _rms(x_ref[...], g_ref[...]) * (1.0 + sc_ref[0]) + sh_ref[0]
    hb = h.astype(BF16)
    o1_ref[0] = _dg(hb, w1_ref[...], _NN)
    o2_ref[0] = _dg(hb, w2_ref[...], _NN)
    o3_ref[0] = _dg(hb, w3_ref[...], _NN)


def _group_rows(grp, tt):
    row_off, nb, t = grp
    base, per = row_off // tt, t // tt
    return pl.BlockSpec((tt, D_MODEL), lambda b, i: (base + b * per + i, 0))


def _in_proj(x, grp, g, sc, sh, w_mla, w_rwkv, w_lru, tt):
    _, nb, t = grp
    d = x.shape[1]
    per_batch = sc.shape[0] > 1
    outs = [w_mla.shape[1], w_rwkv.shape[1], w_lru.shape[1]]
    return pl.pallas_call(
        _in_proj_body,
        grid=(nb, t // tt),
        in_specs=[
            _group_rows(grp, tt),
            _full((1, d)),
            _mod_spec(per_batch),
            _mod_spec(per_batch),
            _full(w_mla.shape),
            _full(w_rwkv.shape),
            _full(w_lru.shape),
        ],
        out_specs=[pl.BlockSpec((1, tt, n), lambda b, i: (b, i, 0)) for n in outs],
        out_shape=[jax.ShapeDtypeStruct((nb, t, n), F32) for n in outs],
        compiler_params=_cparams("parallel", "parallel"),
        name="in_proj",
    )(x, g, sc, sh, w_mla, w_rwkv, w_lru)


def _rope_lanes():
    lane = lax.broadcasted_iota(I32, (1, HEAD_PAD), 1)
    return ((lane >= NOPE_DIM) & (lane < QK_DIM)).astype(F32)


def _mla_prep_body(rope, *refs):
    if rope:
        (p_ref, gq_ref, gkv_ref, wq_ref, wqr_ref, wk_ref, wv_ref, cq_ref, sq_ref, cr_ref, sr_ref,
         q_ref, k_ref, v_ref) = refs
    else:
        p_ref, gq_ref, gkv_ref, wq_ref, wk_ref, wv_ref, q_ref, k_ref, v_ref, c_ref = refs
    p = p_ref[0]
    cq = _rms(p[:, :Q_LORA], gq_ref[...])
    ckv = _rms(p[:, Q_LORA:Q_LORA + KV_LORA], gkv_ref[...])
    kr = p[:, Q_LORA + KV_LORA:]
    cqb = cq.astype(BF16)
    ckvb = ckv.astype(BF16)
    scale = QK_DIM ** -0.5 * float(np.log2(np.e))
    qa = _dg(cqb, wq_ref[...], _NN) * scale
    ka = _dg(ckvb, wk_ref[...], _NN)
    va = _dg(ckvb, wv_ref[...], _NN)
    if rope:
        qr = _dg(cqb, wqr_ref[...], _NN) * scale
        krp = kr * cr_ref[...] + pltpu.roll(kr, ROPE_DIM, 1) * sr_ref[...]
    else:
        krp = kr * _rope_lanes()
    for h in range(H_MLA):
        sl = slice(h * HEAD_PAD, (h + 1) * HEAD_PAD)
        qh = qa[:, sl]
        if rope:
            qh = qh * cq_ref[...] + qr[:, sl] * sq_ref[...]
        q_ref[0, h] = qh.astype(BF16)
        k_ref[0, h] = (ka[:, sl] + krp).astype(BF16)
        v_ref[0, h] = va[:, sl].astype(BF16)
    if not rope:
        c_ref[0, :, :KV_LORA] = ckv
        c_ref[0, :, KV_LORA:] = pltpu.roll(kr, HEAD_PAD - NOPE_DIM, 1)[:, :ROPE_DIM]


def _mla_prep(p_mla, gq, gkv, wq, wqr, wk, wv, tables, tt):
    nb, t, pw = p_mla.shape
    rope = tables is not None
    hw = H_MLA * HEAD_PAD
    head_spec = pl.BlockSpec((1, H_MLA, tt, HEAD_PAD), lambda b, i: (b, 0, i, 0))
    head_shape = jax.ShapeDtypeStruct((nb, H_MLA, t, HEAD_PAD), BF16)
    in_specs = [pl.BlockSpec((1, tt, pw), lambda b, i: (b, i, 0)), _full(gq.shape), _full(gkv.shape), _full(wq.shape)]
    args = [p_mla, gq, gkv, wq]
    if rope:
        in_specs.append(_full(wqr.shape))
        args.append(wqr)
    in_specs += [_full(wk.shape), _full(wv.shape)]
    args += [wk, wv]
    out_specs = [head_spec, head_spec, head_spec]
    out_shape = [head_shape, head_shape, head_shape]
    if rope:
        in_specs += [pl.BlockSpec((tt, HEAD_PAD), lambda b, i: (i, 0))] * 4
        args += list(tables)
    else:
        out_specs.append(pl.BlockSpec((1, tt, KV_LORA + ROPE_DIM), lambda b, i: (b, i, 0)))
        out_shape.append(jax.ShapeDtypeStruct((nb, t, KV_LORA + ROPE_DIM), F32))
    del hw
    return pl.pallas_call(
        functools.partial(_mla_prep_body, rope),
        grid=(nb, t // tt),
        in_specs=in_specs,
        out_specs=out_specs,
        out_shape=out_shape,
        compiler_params=_cparams("parallel", "parallel"),
        name="mla_prep_rope" if rope else "mla_prep",
    )(*args)


def _ctx_kv_body(lat_ref, kr_ref, wk_ref, wv_ref, k_ref, v_ref):
    latb = lat_ref[0].astype(BF16)
    ka = _dg(latb, wk_ref[...], _NN)
    va = _dg(latb, wv_ref[...], _NN)
    kr = kr_ref[0]
    for h in range(H_MLA):
        sl = slice(h * HEAD_PAD, (h + 1) * HEAD_PAD)
        k_ref[0, h] = (ka[:, sl] + kr).astype(BF16)
        v_ref[0, h] = va[:, sl].astype(BF16)


def _ctx_kv(lat, kr_placed, wk, wv):
    nb, s, _ = lat.shape
    head_spec = pl.BlockSpec((1, H_MLA, s, HEAD_PAD), lambda b: (b, 0, 0, 0))
    head_shape = jax.ShapeDtypeStruct((nb, H_MLA, s, HEAD_PAD), BF16)
    return pl.pallas_call(
        _ctx_kv_body,
        grid=(nb,),
        in_specs=[
            pl.BlockSpec((1, s, KV_LORA), lambda b: (b, 0, 0)),
            pl.BlockSpec((1, s, HEAD_PAD), lambda b: (b, 0, 0)),
            _full(wk.shape),
            _full(wv.shape),
        ],
        out_specs=[head_spec, head_spec],
        out_shape=[head_shape, head_shape],
        compiler_params=_cparams("parallel"),
        name="ctx_kv",
    )(lat, kr_placed, wk, wv)


def _attn_body(has_ctx, *refs):
    if has_ctx:
        q_ref, k_ref, v_ref, kc_ref, vc_ref, o_ref = refs
    else:
        q_ref, k_ref, v_ref, o_ref = refs
    q = q_ref[0, 0]
    s = _dg(q, k_ref[0, 0], _NT)
    m = jnp.max(s, axis=-1, keepdims=True)
    if has_ctx:
        sc = _dg(q, kc_ref[0, 0], _NT)
        m = jnp.maximum(m, jnp.max(sc, axis=-1, keepdims=True))
    p = jnp.exp2(s - m)
    l = jnp.sum(p, axis=-1, keepdims=True)
    o = _dg(p.astype(BF16), v_ref[0, 0], _NN)
    if has_ctx:
        pc = jnp.exp2(sc - m)
        l = l + jnp.sum(pc, axis=-1, keepdims=True)
        o = o + _dg(pc.astype(BF16), vc_ref[0, 0], _NN)
    o_ref[0] = o / l


def _attention(q, k, v, kc, vc, tq):
    nb, nh, t, hd = q.shape
    s = k.shape[2]
    has_ctx = kc is not None
    in_specs = [
        pl.BlockSpec((1, 1, tq, hd), lambda b, h, i: (b, h, i, 0)),
        pl.BlockSpec((1, 1, s, hd), lambda b, h, i: (b, h, 0, 0)),
        pl.BlockSpec((1, 1, s, V_DIM), lambda b, h, i: (b, h, 0, 0)),
    ]
    args = [q, k, v]
    if has_ctx:
        sc = kc.shape[2]
        in_specs += [
            pl.BlockSpec((1, 1, sc, hd), lambda b, h, i: (b, h, 0, 0)),
            pl.BlockSpec((1, 1, sc, V_DIM), lambda b, h, i: (b, h, 0, 0)),
        ]
        args += [kc, vc]
    return pl.pallas_call(
        functools.partial(_attn_body, has_ctx),
        grid=(nb, nh, t // tq),
        in_specs=in_specs,
        out_specs=pl.BlockSpec((1, tq, V_DIM), lambda b, h, i: (b, i, h)),
        out_shape=jax.ShapeDtypeStruct((nb, t, nh * V_DIM), F32),
        compiler_params=_cparams("parallel", "parallel", "parallel"),
        name="attention",
    )(*args)


def _rwkv_prep_body(p_ref, hp_ref, hn_ref, mu_ref, kk_ref, ka_ref, rk_ref, w0_ref, w2_ref, a0_ref, a2_ref, g2_ref,
                    r_out, v_out, kk_out, lw_out, a_out, kd_out, g_out, bonus_out):
    p = p_ref[0]
    tt = p.shape[0]
    row = lax.broadcasted_iota(I32, (tt, 1), 0)
    prev = jnp.where(row == 0, hp_ref[0, 0], pltpu.roll(p, 1, 0))
    nxt = jnp.where(row == tt - 1, hn_ref[0, 0], pltpu.roll(p, tt - 1, 0))
    ps = p + (0.5 * (prev + nxt) - p) * mu_ref[...]
    r = ps[:, :D_RWKV]
    k = ps[:, D_RWKV:2 * D_RWKV]
    v = ps[:, 2 * D_RWKV:3 * D_RWKV]
    wl = ps[:, 3 * D_RWKV:3 * D_RWKV + 2 * W_LORA]
    al = ps[:, 3 * D_RWKV + 2 * W_LORA:3 * D_RWKV + 2 * W_LORA + 2 * A_LORA]
    gl = ps[:, 3 * D_RWKV + 2 * W_LORA + 2 * A_LORA:]
    ones = _head_ones(D_RWKV, N_RWKV)
    kk = k * kk_ref[...]
    kk = kk / jnp.maximum(jnp.sqrt(_dot_x(kk * kk, ones)), 1e-12)
    wlin = w0_ref[...] + _dot3(jnp.tanh(wl), w2_ref[...])
    lw = -jnp.exp(-_softplus(-wlin) - 0.5)
    ag = _sigmoid(a0_ref[...] + _dot3(al, a2_ref[...]))
    bonus = jnp.zeros_like(v)
    for d in range(2):
        sl = slice(d * D_RWKV, (d + 1) * D_RWKV)
        kd = k * (1.0 + (ag[:, sl] - 1.0) * ka_ref[...])
        kd_out[0, :, sl] = kd
        bonus = bonus + _dot_x(r * kd * rk_ref[...], ones) * v
    r_out[0] = r
    v_out[0] = v
    kk_out[0] = kk
    lw_out[0] = lw
    a_out[0] = ag
    g_out[0] = _dot1(_sigmoid(gl), g2_ref[...])
    bonus_out[0] = bonus


def _halo_rows(x, tt, width):
    nb, t, c = x.shape
    nt = t // tt
    xt = x.reshape(nb, nt, tt, c)
    zero = jnp.zeros((nb, 1, width, c), x.dtype)
    prev = jnp.concatenate([zero, xt[:, :-1, tt - width:]], axis=1)
    nxt = jnp.concatenate([xt[:, 1:, :width], zero], axis=1)
    return prev, nxt


def _rwkv_prep(p_rwkv, lp, tt):
    nb, t, pw = p_rwkv.shape
    hp, hn = _halo_rows(p_rwkv, tt, 1)
    tok = lambda n: pl.BlockSpec((1, tt, n), lambda b, i: (b, i, 0))
    halo = pl.BlockSpec((1, 1, 1, pw), lambda b, i: (b, i, 0, 0))
    small = [lp['mu_rwkv'], lp['k_k'], lp['k_a'], lp['r_k'], lp['w0_cat'], lp['w2_bd'], lp['a0_cat'], lp['a2_bd'],
             lp['g2_rwkv']]
    widths = [D_RWKV, D_RWKV, D_RWKV, 2 * D_RWKV, 2 * D_RWKV, 2 * D_RWKV, D_RWKV, D_RWKV]
    return pl.pallas_call(
        _rwkv_prep_body,
        grid=(nb, t // tt),
        in_specs=[tok(pw), halo, halo] + [_full(a.shape) for a in small],
        out_specs=[tok(n) for n in widths],
        out_shape=[jax.ShapeDtypeStruct((nb, t, n), F32) for n in widths],
        compiler_params=_cparams("parallel", "parallel"),
        name="rwkv_prep",
    )(p_rwkv, hp, hn, *small)


def _rwkv_chunk(reverse, r, v, kk, lw, ag, kd, s_old):
    cl = r.shape[0]
    hw = D_RWKV

    def later(rows, cols, strict):
        t = lax.broadcasted_iota(I32, (rows, cols), 0)
        s = lax.broadcasted_iota(I32, (rows, cols), 1) % cl
        dlt = (s - t) if reverse else (t - s)
        return (dlt > 0) if strict else (dlt >= 0)

    cum = _x_dot(later(cl, cl, False).astype(F32), lw)
    yield
    tot = jnp.sum(lw, axis=0, keepdims=True)
    e_neg = jnp.exp(-cum)
    e_rem = jnp.exp(tot - cum)
    at = -kk * jnp.exp(cum - lw)
    rt = r * jnp.exp(cum)
    b0 = kk * ag
    bt = b0 * e_neg
    kt = kd * e_neg
    bh = b0 * e_rem
    kh = kd * e_rem
    w_c = jnp.exp(tot)

    lane_head = lax.broadcasted_iota(I32, (1, hw), 1) // N_RWKV
    hms = [(lane_head == h).astype(F32) for h in range(H_RWKV)]
    heads = range(H_RWKV)
    bk = [jnp.concatenate([bt * hm, kt * hm], axis=0) for hm in hms]
    strict = later(cl, 2 * cl, True)
    incl = later(cl, 2 * cl, False)
    ls = [jnp.where(strict, _dot3(at, bk[h], _NT), 0.0) for h in heads]
    ly = [jnp.where(incl, _dot1(rt, bk[h], _NT), 0.0) for h in heads]
    yield

    x = [l[:, :cl] for l in ls]
    eye = (lax.broadcasted_iota(I32, (cl, cl), 0) == lax.broadcasted_iota(I32, (cl, cl), 1)).astype(F32)
    inv = [eye + xh for xh in x]
    span = 2
    while span < cl:
        x = [_dot3(xh, xh) for xh in x]
        yield
        inv = [ih + _dot3(ih, xh) for ih, xh in zip(inv, x)]
        yield
        span *= 2

    zero = jnp.zeros((cl, hw), F32)
    v_rows = jnp.concatenate([piece for hm in hms for piece in (zero, v * hm)], axis=0)
    rhs = _dot3(at, s_old, _NT) + _dot3(jnp.concatenate(ls, axis=1), v_rows)
    yield
    us = [_dot3(inv[h], rhs * hms[h]) for h in heads]
    yield
    um = us[0]
    for h in range(1, H_RWKV):
        um = um + us[h]
    uv_rows = jnp.concatenate([piece for h in heads for piece in (us[h], v * hms[h])], axis=0)
    y = _dot1(rt, s_old, _NT) + _dot1(jnp.concatenate(ly, axis=1), uv_rows)
    yield
    uv_t = jnp.concatenate([um, v], axis=0).T
    s_new = s_old * w_c + _dot3(uv_t, jnp.concatenate([bh, kh], axis=0)) * _head_ones(hw, N_RWKV)
    yield y, s_new


def _lockstep(gens):
    last = [None] * len(gens)
    live = list(range(len(gens)))
    while live:
        for i in list(live):
            try:
                last[i] = next(gens[i])
            except StopIteration:
                live.remove(i)
    return last


def _rwkv_scan_body(rf_ref, vf_ref, kkf_ref, lwf_ref, af_ref, kdf_ref, rb_ref, vb_ref, kkb_ref, lwb_ref, ab_ref,
                    kdb_ref, s0_ref, yf_ref, yb_ref, sf_ref, s_scr):
    c = pl.program_id(1)
    nc = pl.num_programs(1)

    @pl.when(c == 0)
    def _():
        s_scr[...] = s0_ref[...]

    nrow = rf_ref.shape[0]
    chains = []
    for i in range(nrow):
        chains.append(_rwkv_chunk(False, rf_ref[i], vf_ref[i], kkf_ref[i], lwf_ref[i], af_ref[i], kdf_ref[i],
                                  s_scr[i, 0]))
        chains.append(_rwkv_chunk(True, rb_ref[i], vb_ref[i], kkb_ref[i], lwb_ref[i], ab_ref[i], kdb_ref[i],
                                  s_scr[i, 1]))
    done = _lockstep(chains)
    for i in range(nrow):
        (y_f, s_f), (y_b, s_b) = done[2 * i], done[2 * i + 1]
        yf_ref[i] = y_f
        yb_ref[i] = y_b
        s_scr[i, 0] = s_f
        s_scr[i, 1] = s_b

    @pl.when(c == nc - 1)
    def _():
        sf_ref[...] = s_scr[...]


def _rwkv_scan(r, v, kk, lw, ag, kd, s0):
    nb, t, hw = r.shape
    cl = RWKV_CHUNK
    nr = RWKV_ROWS
    nc = t // cl
    fwd = lambda lane: pl.BlockSpec((nr, cl, hw), lambda b, c: (b, c, lane))
    bwd = lambda lane: pl.BlockSpec((nr, cl, hw), lambda b, c: (b, nc - 1 - c, lane))
    state = pl.BlockSpec((nr, 2, hw, hw), lambda b, c: (b, 0, 0, 0))
    y_shape = jax.ShapeDtypeStruct((nb, t, hw), F32)
    return pl.pallas_call(
        _rwkv_scan_body,
        grid=(nb // nr, nc),
        in_specs=[fwd(0)] * 6 + [bwd(0)] * 3 + [bwd(1)] * 3 + [state],
        out_specs=[fwd(0), bwd(0), state],
        out_shape=[y_shape, y_shape, jax.ShapeDtypeStruct((nb, 2, hw, hw), F32)],
        scratch_shapes=[pltpu.VMEM((nr, 2, hw, hw), F32)],
        compiler_params=_cparams("parallel", "arbitrary"),
        name="rwkv_scan",
    )(r, v, kk, lw, ag, kd, r, v, kk, lw, ag, kd, s0)


def _lru_body(reverse, p_ref, halo_ref, cw_ref, cb_ref, wa_ref, ba_ref, wx_ref, bx_ref, lam_ref, h0_ref,
              h_out, fin_out, a_scr, u_scr, carry_scr):
    i = pl.program_id(1)
    nt = pl.num_programs(1)
    tt = p_ref.shape[1]
    n_ext = tt + SUBLANES

    @pl.when(i == 0)
    def _():
        carry_scr[...] = jnp.broadcast_to(h0_ref[0], (SUBLANES, D_LRU))

    xb = p_ref[0, :, :D_LRU]
    halo = halo_ref[0, 0]
    xc = cb_ref[...] + jnp.zeros_like(xb)
    if reverse:
        xe = jnp.concatenate([xb, halo], axis=0)
        for k in range(CONV_W):
            sh = xb if k == 0 else pltpu.roll(xe, n_ext - k, 0)[:tt]
            xc = xc + cw_ref[k:k + 1, :] * sh
    else:
        xe = jnp.concatenate([halo, xb], axis=0)
        for k in range(CONV_W):
            s = CONV_W - 1 - k
            sh = xb if s == 0 else pltpu.roll(xe, s, 0)[SUBLANES:]
            xc = xc + cw_ref[k:k + 1, :] * sh
    gate_r = _sigmoid(_dot3(xc, wa_ref[...]) + ba_ref[...])
    gate_i = _sigmoid(_dot3(xc, wx_ref[...]) + bx_ref[...])
    log_a = -LRU_C * gate_r * _softplus(-lam_ref[...])
    a = jnp.exp(log_a)
    u = jnp.sqrt(-_expm1(2.0 * log_a)) * (gate_i * xc)

    rowm = lax.broadcasted_iota(I32, (tt, 1), 0) % SUBLANES
    for s in (1, 2, 4):
        if reverse:
            a_s = pltpu.roll(a, tt - s, 0)
            u_s = pltpu.roll(u, tt - s, 0)
            m = rowm < SUBLANES - s
        else:
            a_s = pltpu.roll(a, s, 0)
            u_s = pltpu.roll(u, s, 0)
            m = rowm >= s
        u = jnp.where(m, a * u_s + u, u)
        a = jnp.where(m, a * a_s, a)
    a_scr[...] = a
    u_scr[...] = u
    ng = tt // SUBLANES

    def group(g, carry):
        gi = (ng - 1 - g) if reverse else g
        r0 = pl.multiple_of(gi * SUBLANES, SUBLANES)
        hg = a_scr[pl.ds(r0, SUBLANES), :] * carry + u_scr[pl.ds(r0, SUBLANES), :]
        h_out[0, pl.ds(r0, SUBLANES), :] = hg
        edge = hg[0:1, :] if reverse else hg[SUBLANES - 1:SUBLANES, :]
        return jnp.broadcast_to(edge, (SUBLANES, D_LRU))

    carry = lax.fori_loop(0, ng, group, carry_scr[...])
    carry_scr[...] = carry

    @pl.when(i == nt - 1)
    def _():
        fin_out[0] = carry[0:1, :]


def _lru_dir(p_lru, halo, lp, d, h0, tt):
    nb, t, pw = p_lru.shape
    nt = t // tt
    reverse = d == 1
    tidx = (lambda i: nt - 1 - i) if reverse else (lambda i: i)
    small = [lp['conv_w'][d], lp['conv_b'][d:d + 1], lp['wa_bd'][d], lp['b_lru_a'][d:d + 1], lp['wx_bd'][d],
             lp['b_lru_x'][d:d + 1], lp['lam'][d:d + 1]]
    return pl.pallas_call(
        functools.partial(_lru_body, reverse),
        grid=(nb, nt),
        in_specs=[
            pl.BlockSpec((1, tt, pw), lambda b, i: (b, tidx(i), 0)),
            pl.BlockSpec((1, 1, SUBLANES, D_LRU), lambda b, i: (b, tidx(i), 0, 0)),
        ] + [_full(a.shape) for a in small] + [pl.BlockSpec((1, 1, D_LRU), lambda b, i: (b, 0, 0))],
        out_specs=[
            pl.BlockSpec((1, tt, D_LRU), lambda b, i: (b, tidx(i), 0)),
            pl.BlockSpec((1, 1, D_LRU), lambda b, i: (b, 0, 0)),
        ],
        out_shape=[jax.ShapeDtypeStruct((nb, t, D_LRU), F32), jax.ShapeDtypeStruct((nb, 1, D_LRU), F32)],
        scratch_shapes=[pltpu.VMEM((tt, D_LRU), F32), pltpu.VMEM((tt, D_LRU), F32), pltpu.VMEM((SUBLANES, D_LRU), F32)],
        compiler_params=_cparams("parallel", "arbitrary"),
        name="lru_bwd" if reverse else "lru_fwd",
    )(p_lru, halo, *small, h0)


def _gelu_tanh(x):
    return 0.5 * x * (1.0 + jnp.tanh(np.sqrt(2.0 / np.pi).astype(np.float32) * (x + 0.044715 * (x * x * x))))


def _out_proj_body(x_ref, o_ref, y0_ref, y1_ref, bonus_ref, g_ref, h0_ref, h1_ref, p_ref,
                   gmla_ref, lnw_ref, lnb_ref, glru_ref, wo_ref, gpost_ref, gate_ref, gpre_ref, sc_ref, sh_ref,
                   *rest):
    x1_out, h2_out = rest[-2:]
    o_mla = _rms(o_ref[0], gmla_ref[...])
    ones = _head_ones(D_RWKV, N_RWKV)
    y = y0_ref[0] + y1_ref[0]
    mu = _dot_x(y, ones) * (1.0 / N_RWKV)
    yc = y - mu
    var = _dot_x(yc * yc, ones) * (1.0 / N_RWKV)
    y = yc * lax.rsqrt(var + GN_EPS) * lnw_ref[...] + lnb_ref[...]
    o_rwkv = (y + bonus_ref[0]) * g_ref[0]
    o_lru = _rms((h0_ref[0] + h1_ref[0]) * _gelu_tanh(p_ref[0, :, D_LRU:]), glru_ref[...])
    m = (_dg(o_mla.astype(BF16), wo_ref[:D_MLA_OUT, :], _NN)
         + _dg(o_rwkv.astype(BF16), wo_ref[D_MLA_OUT:D_MLA_OUT + D_RWKV, :], _NN)
         + _dg(o_lru.astype(BF16), wo_ref[D_MLA_OUT + D_RWKV:, :], _NN))
    x1 = x_ref[...] + gate_ref[0] * _rms(m, gpost_ref[...])
    x1_out[...] = x1
    h2_out[...] = _rms(x1, gpre_ref[...]) * (1.0 + sc_ref[0]) + sh_ref[0]


def _out_proj(x, grp, o, y_f, y_b, bonus, g, h0, h1, p_lru, lp, mod, tt, prev):
    _, nb, t = grp
    per_batch = mod['g1'].shape[0] > 1
    tok = lambda n: pl.BlockSpec((1, tt, n), lambda b, i: (b, i, 0))
    rows = lambda: _group_rows(grp, tt)
    small = [lp['g_mla_out'], lp['ln_x_w'], lp['ln_x_b'], lp['g_lru_out'], lp['w_out'], lp['g_post_mix']]
    args = [x, o, y_f, y_b, bonus, g, h0, h1, p_lru, *small, mod['g1'], lp['g_pre_ffn'], mod['sc2'], mod['sh2']]
    in_specs = ([rows(), tok(D_MLA_OUT), tok(D_RWKV), tok(D_RWKV), tok(D_RWKV), tok(D_RWKV), tok(D_LRU), tok(D_LRU),
                 tok(P_LRU)] + [_full(a.shape) for a in small]
                + [_mod_spec(per_batch), _full(lp['g_pre_ffn'].shape), _mod_spec(per_batch), _mod_spec(per_batch)])
    aliases = {}
    if prev is not None:
        aliases = {len(args): 0, len(args) + 1: 1}
        args += list(prev)
        in_specs += [pl.BlockSpec(memory_space=pl.ANY)] * 2
    return pl.pallas_call(
        _out_proj_body,
        grid=(nb, t // tt),
        in_specs=in_specs,
        out_specs=[rows(), rows()],
        out_shape=[jax.ShapeDtypeStruct(x.shape, F32)] * 2,
        input_output_aliases=aliases,
        compiler_params=_cparams("parallel", "parallel"),
        name="out_proj",
    )(*args)


def _first_index(mask, idx, sentinel):
    return jnp.min(jnp.where(mask, idx, sentinel), axis=0, keepdims=True)


def _router_body(h_ref, wr_ref, br_ref, e_out, pos_out, loc_out, pad_out, stat_out, w_out, cnt_out, carry):
    i = pl.program_id(0)
    tn = h_ref.shape[0]

    @pl.when(i == 0)
    def _():
        carry[...] = jnp.zeros_like(carry)

    logits = lax.dot_general(wr_ref[...], h_ref[...], _NT, precision=lax.Precision.HIGHEST,
                             preferred_element_type=F32)
    scores = _sigmoid(logits)
    sel = scores + br_ref[...]

    i8 = lax.broadcasted_iota(I32, (GROUP_SIZE, tn), 0)
    blocks, gscore = [], []
    for g in range(N_GROUPS):
        blk = sel[g * GROUP_SIZE:(g + 1) * GROUP_SIZE, :]
        m1 = jnp.max(blk, axis=0, keepdims=True)
        f1 = _first_index(blk == m1, i8, GROUP_SIZE)
        m2 = jnp.max(jnp.where(i8 == f1, -jnp.inf, blk), axis=0, keepdims=True)
        blocks.append(blk)
        gscore.append(m1 + m2)
    masked = []
    for g in range(N_GROUPS):
        beaten = jnp.zeros((1, tn), I32)
        for o in range(N_GROUPS):
            if o == g:
                continue
            wins = (gscore[o] >= gscore[g]) if o < g else (gscore[o] > gscore[g])
            beaten = beaten + wins.astype(I32)
        masked.append(jnp.where(beaten < TOPK_GROUPS, blocks[g], -jnp.inf))
    cur = jnp.concatenate(masked, axis=0)

    ie = lax.broadcasted_iota(I32, (N_EXPERTS, tn), 0)
    firsts, raw = [], []
    chosen = jnp.zeros((N_EXPERTS, tn), F32)
    for _ in range(TOP_K):
        mx = jnp.max(cur, axis=0, keepdims=True)
        first = _first_index(cur == mx, ie, N_EXPERTS)
        hit = ie == first
        firsts.append(first)
        raw.append(jnp.sum(jnp.where(hit, scores, 0.0), axis=0, keepdims=True))
        cur = jnp.where(hit, -jnp.inf, cur)
        chosen = chosen + hit.astype(F32)
    total = raw[0]
    for k in range(1, TOP_K):
        total = total + raw[k]

    before = (lax.broadcasted_iota(I32, (tn, tn), 0) < lax.broadcasted_iota(I32, (tn, tn), 1)).astype(BF16)
    rank_tile = _dg(chosen.astype(BF16), before, _NN)
    old_carry = carry[...]
    rank = rank_tile + old_carry[:, 0:1]
    tile_cnt = jnp.broadcast_to(jnp.sum(chosen, axis=1, keepdims=True), old_carry.shape)
    lower = (lax.broadcasted_iota(I32, (N_EXPERTS, N_EXPERTS), 1)
             < lax.broadcasted_iota(I32, (N_EXPERTS, N_EXPERTS), 0)).astype(F32)
    tile_off = _x_dot(lower, tile_cnt)
    local = rank_tile + tile_off[:, 0:1]
    tile_chunks = jnp.floor((tile_cnt + (DISPATCH_CHUNK - 1)) * (1.0 / DISPATCH_CHUNK))
    chunk_off = _x_dot(lower, tile_chunks)
    padded = rank_tile + chunk_off[:, 0:1] * DISPATCH_CHUNK
    e_out[...] = jnp.zeros_like(e_out)
    pos_out[...] = jnp.zeros_like(pos_out)
    loc_out[...] = jnp.zeros_like(loc_out)
    pad_out[...] = jnp.zeros_like(pad_out)
    w_out[...] = jnp.zeros_like(w_out)
    for k in range(TOP_K):
        hit = ie == firsts[k]
        e_out[0, k:k + 1, :] = firsts[k]
        pos_out[0, k:k + 1, :] = jnp.sum(jnp.where(hit, rank, 0.0), axis=0, keepdims=True).astype(I32)
        loc_out[0, k:k + 1, :] = jnp.sum(jnp.where(hit, local, 0.0), axis=0, keepdims=True).astype(I32)
        pad_out[0, k:k + 1, :] = jnp.sum(jnp.where(hit, padded, 0.0), axis=0, keepdims=True).astype(I32)
        w_out[k:k + 1, :] = raw[k] / total * ROUTED_SCALE
    lane = lax.broadcasted_iota(I32, old_carry.shape, 1)
    stat_out[0] = jnp.where(lane == 0, tile_cnt, jnp.where(lane == 1, tile_off, jnp.where(
        lane == 2, old_carry, chunk_off))).astype(I32)
    new_carry = old_carry + tile_cnt
    carry[...] = new_carry
    cnt_out[...] = new_carry


def _router(h2, wr_t, b_col):
    n, d = h2.shape
    tn = ROUTER_TILE
    row_spec = pl.BlockSpec((SLOT_ROWS, tn), lambda i: (0, i))
    tile_spec = pl.BlockSpec((1, SLOT_ROWS, tn), lambda i: (i, 0, 0))
    cnt_spec = pl.BlockSpec((N_EXPERTS, 128), lambda i: (0, 0))
    tile_shape = jax.ShapeDtypeStruct((n // tn, SLOT_ROWS, tn), I32)
    return pl.pallas_call(
        _router_body,
        grid=(n // tn,),
        in_specs=[pl.BlockSpec((tn, d), lambda i: (i, 0)), _full(wr_t.shape), _full(b_col.shape)],
        out_specs=[tile_spec, tile_spec, tile_spec, tile_spec,
                   pl.BlockSpec((1, N_EXPERTS, 128), lambda i: (i, 0, 0)), row_spec, cnt_spec],
        out_shape=[tile_shape, tile_shape, tile_shape, tile_shape,
                   jax.ShapeDtypeStruct((n // tn, N_EXPERTS, 128), I32),
                   jax.ShapeDtypeStruct((SLOT_ROWS, n), F32), jax.ShapeDtypeStruct((N_EXPERTS, 128), F32)],
        scratch_shapes=[pltpu.VMEM((N_EXPERTS, 128), F32)],
        compiler_params=_cparams("arbitrary"),
        name="router",
    )(h2, wr_t, b_col)


ROW_LANES = 128
ROW_SUB = D_MODEL // 2 // ROW_LANES
U32 = jnp.uint32
HIGH_HALF = np.uint32(0xFFFF0000)


def _bf16_bits(x):
    return lax.bitcast_convert_type(x.astype(BF16).astype(F32), U32)


def _to_row_tiles(ref, x):
    rows, d = x.shape
    words = (_bf16_bits(x[:, :d // 2]) >> 16) | (_bf16_bits(x[:, d // 2:]) & HIGH_HALF)
    for s in range(ROW_SUB):
        ref[pl.ds(s, rows, stride=ROW_SUB), :] = words[:, s * ROW_LANES:(s + 1) * ROW_LANES]


def _from_row_tiles(ref, rows):
    words = jnp.concatenate([ref[pl.ds(s, rows, stride=ROW_SUB), :] for s in range(ROW_SUB)], axis=1)
    low = lax.bitcast_convert_type(words << 16, F32)
    high = lax.bitcast_convert_type(words & HIGH_HALF, F32)
    return jnp.concatenate([low, high], axis=1)


def _row_copy(src, src_off, dst, dst_off, sem):
    return pltpu.make_async_copy(src.at[pl.ds(pl.multiple_of(src_off, ROW_SUB), ROW_SUB), :],
                                 dst.at[pl.ds(pl.multiple_of(dst_off, ROW_SUB), ROW_SUB), :], sem)


def _load_slots(slot_ref, slot_smem, sem):
    cp = pltpu.make_async_copy(slot_ref.at[0], slot_smem, sem)
    cp.start()
    cp.wait()


def _chunk_copy(src, src_row, dst, dst_row, sem):
    n = DISPATCH_CHUNK * ROW_SUB
    return pltpu.make_async_copy(src.at[pl.ds(pl.multiple_of(src_row * ROW_SUB, ROW_SUB), n), :],
                                 dst.at[pl.ds(pl.multiple_of(dst_row * ROW_SUB, ROW_SUB), n), :], sem)


def _dispatch_body(start_ref, e_ref, pos_ref, loc_ref, stat_ref, h_ref, xs_in, slots_out, xs_out,
                   stat_smem, pending, staging, sem_s, sem):
    del xs_in
    i = pl.program_id(0)
    last = pl.num_programs(0) - 1
    tn = h_ref.shape[0]
    rows = TOP_K * tn
    cur = i % 2

    e = e_ref[0]
    slot = pos_ref[0]
    for x in range(N_EXPERTS):
        slot = slot + jnp.where(e == x, start_ref[x], 0)
    slots_out[0] = slot * ROW_SUB
    stats = pltpu.make_async_copy(stat_ref.at[0], stat_smem, sem_s)
    stats.start()

    @pl.when(i == 0)
    def _():
        pending[0] = 0
        pad = jnp.zeros((DISPATCH_CHUNK * ROW_SUB, ROW_LANES), U32)
        for b in range(2):
            staging[b, pl.ds(rows * ROW_SUB, DISPATCH_CHUNK * ROW_SUB), :] = pad

    loc = loc_ref[0]
    p_iota = lax.broadcasted_iota(I32, (rows, tn), 0)
    pick = jnp.zeros((rows, tn), F32)
    for k in range(TOP_K):
        pick = jnp.where(p_iota == loc[k:k + 1, :], 1.0, pick)
    _to_row_tiles(staging.at[cur], _dg(pick.astype(BF16), h_ref[...].astype(BF16), _NN))

    def drain(count):
        def one(c, carry):
            _chunk_copy(staging.at[0], 0, xs_out, 0, sem).wait()
            return carry
        lax.fori_loop(0, count, one, 0)

    drain(pending[0])
    stats.wait()

    def per_expert(x, total):
        count = stat_smem[x, 0]
        src = stat_smem[x, 1]
        dst = start_ref[x] + stat_smem[x, 2]
        n_chunks = lax.shift_right_logical(count + (DISPATCH_CHUNK - 1), DISPATCH_SHIFT)

        def one(c, carry):
            _chunk_copy(staging.at[cur], src + c * DISPATCH_CHUNK, xs_out, dst + c * DISPATCH_CHUNK, sem).start()
            return carry

        lax.fori_loop(0, n_chunks, one, 0)
        return total + n_chunks

    issued = lax.fori_loop(0, N_EXPERTS, per_expert, 0)
    pending[0] = issued

    @pl.when(i == last)
    def _():
        drain(issued)


def _zero_tail_body(lb_ref, o_ref):
    del lb_ref
    o_ref[...] = jnp.zeros_like(o_ref)


def _zero_tails(last_block, n_blocks):
    rows = MOE_BLOCK * ROW_SUB
    grid_spec = pltpu.PrefetchScalarGridSpec(
        num_scalar_prefetch=1,
        grid=(last_block.shape[0],),
        in_specs=[],
        out_specs=pl.BlockSpec((rows, ROW_LANES), lambda e, lb: (lb[e], 0)),
    )
    return pl.pallas_call(
        _zero_tail_body,
        grid_spec=grid_spec,
        out_shape=jax.ShapeDtypeStruct((n_blocks * rows, ROW_LANES), U32),
        compiler_params=_cparams("arbitrary"),
        name="zero_tails",
    )(last_block)


def _dispatch(row_start, e3, pos3, loc3, stats, h2, xs_zero):
    n, d = h2.shape
    tn = e3.shape[2]
    tile = lambda: pl.BlockSpec((1, SLOT_ROWS, tn), lambda i, st: (i, 0, 0))
    grid_spec = pltpu.PrefetchScalarGridSpec(
        num_scalar_prefetch=1,
        grid=(n // tn,),
        in_specs=[tile(), tile(), tile(), pl.BlockSpec((1, N_EXPERTS, 128), lambda i, st: (i, 0, 0)),
                  pl.BlockSpec((tn, d), lambda i, st: (i, 0)), pl.BlockSpec(memory_space=pl.ANY)],
        out_specs=[tile(), pl.BlockSpec(memory_space=pl.ANY)],
        scratch_shapes=[pltpu.SMEM((N_EXPERTS, 128), I32), pltpu.SMEM((1,), I32),
                        pltpu.VMEM((2, (TOP_K * tn + DISPATCH_CHUNK) * ROW_SUB, ROW_LANES), U32),
                        pltpu.SemaphoreType.DMA, pltpu.SemaphoreType.DMA],
    )
    return pl.pallas_call(
        _dispatch_body,
        grid_spec=grid_spec,
        out_shape=[jax.ShapeDtypeStruct((n // tn, SLOT_ROWS, tn), I32),
                   jax.ShapeDtypeStruct(xs_zero.shape, xs_zero.dtype)],
        input_output_aliases={6: 1},
        compiler_params=_cparams("arbitrary"),
        name="dispatch",
    )(row_start, e3, pos3, loc3, stats, h2, xs_zero)


def _swiglu(x, wg, wu, wd):
    g = _dg(x, wg, _NN)
    u = _dg(x, wu, _NN)
    return _dg((g * _sigmoid(g) * u).astype(BF16), wd, _NN)


def _expert_body(be_ref, nu_ref, xs_ref, wg_ref, wu_ref, wd_ref, ys_ref):
    del be_ref
    b = pl.program_id(0)

    @pl.when(b < nu_ref[0])
    def _():
        x = _from_row_tiles(xs_ref, MOE_BLOCK).astype(BF16)
        _to_row_tiles(ys_ref, _swiglu(x, wg_ref[0], wu_ref[0], wd_ref[0]))

    @pl.when(b >= nu_ref[0])
    def _():
        ys_ref[...] = jnp.zeros_like(ys_ref)


def _experts(block_e, n_used, xs, wg, wu, wd):
    bm = MOE_BLOCK
    _, d, de = wg.shape
    rows = bm * ROW_SUB
    grid_spec = pltpu.PrefetchScalarGridSpec(
        num_scalar_prefetch=2,
        grid=(xs.shape[0] // rows,),
        in_specs=[
            pl.BlockSpec((rows, ROW_LANES), lambda b, be, nu: (jnp.minimum(b, nu[0] - 1), 0)),
            pl.BlockSpec((1, d, de), lambda b, be, nu: (be[b], 0, 0)),
            pl.BlockSpec((1, d, de), lambda b, be, nu: (be[b], 0, 0)),
            pl.BlockSpec((1, de, d), lambda b, be, nu: (be[b], 0, 0)),
        ],
        out_specs=pl.BlockSpec((rows, ROW_LANES), lambda b, be, nu: (b, 0)),
    )
    return pl.pallas_call(
        _expert_body,
        grid_spec=grid_spec,
        out_shape=jax.ShapeDtypeStruct(xs.shape, xs.dtype),
        compiler_params=_cparams("arbitrary"),
        name="experts",
    )(block_e, n_used, xs, wg, wu, wd)


def _combine_body(start_ref, pad_ref, stat0_ref, statn_ref, w_ref, x1_ref, h2_ref, gate_ref, ys_hbm, wsg_ref,
                  wsu_ref, wsd_ref, gpost_ref, out_ref, stat_smem, pending, staging, sem_s, sem):
    i = pl.program_id(0)
    last = pl.num_programs(0) - 1
    tn = x1_ref.shape[0]
    cur = i % 2

    def fetch(stat_ref, which):
        stats = pltpu.make_async_copy(stat_ref.at[0], stat_smem, sem_s)
        stats.start()
        stats.wait()

        def per_expert(x, total):
            count = stat_smem[x, 0]
            src = start_ref[x] + stat_smem[x, 2]
            dst = stat_smem[x, 3] * DISPATCH_CHUNK
            n_chunks = lax.shift_right_logical(count + (DISPATCH_CHUNK - 1), DISPATCH_SHIFT)

            def one(c, carry):
                _chunk_copy(ys_hbm, src + c * DISPATCH_CHUNK, staging.at[which], dst + c * DISPATCH_CHUNK,
                            sem.at[which]).start()
                return carry

            lax.fori_loop(0, n_chunks, one, 0)
            return total + n_chunks

        pending[which] = lax.fori_loop(0, N_EXPERTS, per_expert, 0)

    @pl.when(i == 0)
    def _():
        staging[...] = jnp.zeros_like(staging)
        fetch(stat0_ref, 0)

    @pl.when(i < last)
    def _():
        fetch(statn_ref, 1 - cur)

    acc = _swiglu(h2_ref[...].astype(BF16), wsg_ref[...], wsu_ref[...], wsd_ref[...])

    def drain(c, carry):
        _chunk_copy(ys_hbm, 0, staging.at[0], 0, sem.at[cur]).wait()
        return carry

    lax.fori_loop(0, pending[cur], drain, 0)
    piece = COMBINE_ROWS // COMBINE_PIECES
    for q in range(COMBINE_PIECES):
        col = lax.broadcasted_iota(I32, (tn, piece), 1) + q * piece
        weight = jnp.zeros((tn, piece), F32)
        for k in range(TOP_K):
            weight = jnp.where(col == pad_ref[:, k:k + 1], w_ref[:, k:k + 1], weight)
        rows = _from_row_tiles(staging.at[cur, pl.ds(q * piece * ROW_SUB, piece * ROW_SUB), :], piece)
        acc = acc + _dg(weight.astype(BF16), rows.astype(BF16), _NN)
    out_ref[...] = x1_ref[...] + gate_ref[0] * _rms(acc, gpost_ref[...])


def _combine(row_start, pad_tok, stats, w_tok, x1, h2, gate_rows, ys, wsg, wsu, wsd, gpost):
    n, d = x1.shape
    tn = TOKEN_TILE
    tok = pl.BlockSpec((tn, d), lambda i, st: (i, 0))
    small = pl.BlockSpec((tn, SLOT_ROWS), lambda i, st: (i, 0))
    ntiles = n // tn
    grid_spec = pltpu.PrefetchScalarGridSpec(
        num_scalar_prefetch=1,
        grid=(ntiles,),
        in_specs=[small, pl.BlockSpec((1, N_EXPERTS, 128), lambda i, st: (0, 0, 0)),
                  pl.BlockSpec((1, N_EXPERTS, 128), lambda i, st: (jnp.minimum(i + 1, ntiles - 1), 0, 0)),
                  small, tok, tok,
                  pl.BlockSpec((1, 1, d), lambda i, st: (i, 0, 0)),
                  pl.BlockSpec(memory_space=pl.ANY),
                  pl.BlockSpec(wsg.shape, lambda i, st: (0, 0)), pl.BlockSpec(wsu.shape, lambda i, st: (0, 0)),
                  pl.BlockSpec(wsd.shape, lambda i, st: (0, 0)), pl.BlockSpec(gpost.shape, lambda i, st: (0, 0))],
        out_specs=tok,
        scratch_shapes=[pltpu.SMEM((N_EXPERTS, 128), I32), pltpu.SMEM((2,), I32),
                        pltpu.VMEM((2, COMBINE_ROWS * ROW_SUB, ROW_LANES), U32),
                        pltpu.SemaphoreType.DMA, pltpu.SemaphoreType.DMA((2,))],
    )
    return pl.pallas_call(
        _combine_body,
        grid_spec=grid_spec,
        out_shape=jax.ShapeDtypeStruct((n, d), F32),
        compiler_params=_cparams("arbitrary"),
        name="combine",
    )(row_start, pad_tok, stats, stats, w_tok, x1, h2, gate_rows, ys, wsg, wsu, wsd, gpost)


def _moe(x1, h2, gate_rows, lp):
    n, d = h2.shape
    bm = MOE_BLOCK
    tn = TOKEN_TILE
    e3, pos3, loc3, pad3, stats, w_rows, cnt = _router(h2, lp['wr_t'], lp['b_router_col'])
    counts = cnt[:, 0].astype(I32)
    blocks_per_e = (counts + DISPATCH_CHUNK + bm - 1) // bm
    blk_end = jnp.cumsum(blocks_per_e)
    blk_start = blk_end - blocks_per_e
    n_blocks = -(-(n * TOP_K + N_EXPERTS * DISPATCH_CHUNK) // bm) + N_EXPERTS
    block_e = jnp.minimum(jnp.sum(blk_end[None, :] <= jnp.arange(n_blocks, dtype=I32)[:, None], axis=1),
                          N_EXPERTS - 1).astype(I32)
    n_used = blk_end[-1:].astype(I32)
    partial = jnp.stack([blk_start + counts // bm, blk_end - 1], axis=1).reshape(-1)
    xs_tails = _zero_tails(jnp.clip(partial, 0, n_blocks - 1).astype(I32), n_blocks)
    row_start = (blk_start * bm).astype(I32)
    _, xs = _dispatch(row_start, e3, pos3, loc3, stats, h2, xs_tails)
    ys = _experts(block_e, n_used, xs, lp['w_e_gate'], lp['w_e_up'], lp['w_e_down'])
    pad_tok = pad3.transpose(0, 2, 1).reshape(n, SLOT_ROWS)
    return _combine(row_start, pad_tok, stats, w_rows.T, x1, h2, gate_rows, ys, lp['w_s_gate'], lp['w_s_up'],
                    lp['w_s_down'], lp['g_post_ffn'])


def _rope_tables(n_tokens):
    rows = n_tokens // GRID_W
    row = jnp.repeat(jnp.arange(rows, dtype=F32), GRID_W)
    col = jnp.tile(jnp.arange(GRID_W, dtype=F32), rows)
    n_freq = ROPE_DIM // 4
    inv_freq = ROPE_THETA ** (-jnp.arange(n_freq, dtype=F32) / n_freq)
    ang_r = row[:, None] * inv_freq
    ang_c = col[:, None] * inv_freq
    ang = jnp.concatenate([ang_r, ang_r, ang_c, ang_c], axis=-1)
    cos, sin = jnp.cos(ang), jnp.sin(ang)
    pad = lambda z, fill: jnp.concatenate(
        [jnp.full((n_tokens, NOPE_DIM), fill, F32), z, jnp.zeros((n_tokens, HEAD_PAD - QK_DIM), F32)], axis=1)
    return pad(cos, 1.0), pad(sin, 0.0), pad(cos, 0.0), pad(sin, 0.0)


def _rot_cols(w):
    q = ROPE_DIM // 4
    perm = np.concatenate([np.arange(q, 2 * q), np.arange(0, q), np.arange(3 * q, 4 * q), np.arange(2 * q, 3 * q)])
    sign = np.concatenate([-np.ones(q), np.ones(q), -np.ones(q), np.ones(q)]).astype(np.float32)
    return w[..., perm] * sign


def _block_diag(blocks):
    h, m, n = blocks.shape[-3:]
    eye = jnp.eye(h, dtype=blocks.dtype)
    out = blocks[..., :, :, None, :] * eye[:, None, :, None]
    return out.reshape(blocks.shape[:-3] + (h * m, h * n))


def _diag_blocks(mat, h):
    m, n = mat.shape[-2] // h, mat.shape[-1] // h
    z = mat.reshape(mat.shape[:-2] + (h, m, h, n))
    return jnp.stack([z[..., i, :, i, :] for i in range(h)], axis=-3)


def _layer_operands(l, a):
    w_in = a['w_in'][l]
    d = w_in.shape[0]
    p_mla = Q_LORA + KV_LORA + ROPE_DIM
    w_kr = w_in[:, Q_LORA + KV_LORA:p_mla]
    z32 = jnp.zeros((d, ROPE_DIM), F32)
    w_mla = jnp.concatenate([w_in[:, :Q_LORA + KV_LORA], z32, _rot_cols(w_kr), w_kr, z32], axis=1)
    wq3 = a['w_q_b'][l].reshape(Q_LORA, H_MLA, QK_DIM)
    zq = lambda n: jnp.zeros((Q_LORA, H_MLA, n), F32)
    wq = jnp.concatenate([wq3, zq(HEAD_PAD - QK_DIM)], axis=2).reshape(Q_LORA, H_MLA * HEAD_PAD)
    wqr = jnp.concatenate([zq(NOPE_DIM), _rot_cols(wq3[:, :, NOPE_DIM:]), zq(HEAD_PAD - QK_DIM)],
                          axis=2).reshape(Q_LORA, H_MLA * HEAD_PAD)
    wkv3 = a['w_kv_b'][l].reshape(KV_LORA, H_MLA, NOPE_DIM + V_DIM)
    wk = jnp.concatenate([wkv3[:, :, :NOPE_DIM], jnp.zeros((KV_LORA, H_MLA, HEAD_PAD - NOPE_DIM), F32)],
                         axis=2).reshape(KV_LORA, H_MLA * HEAD_PAD)
    wv = wkv3[:, :, NOPE_DIM:].reshape(KV_LORA, H_MLA * V_DIM)
    row = lambda z: z.reshape(1, -1)
    two = lambda z: jnp.concatenate([z[0], z[1]], axis=-1)
    bd2 = lambda z: jnp.concatenate([jnp.concatenate([z[0], jnp.zeros_like(z[0])], axis=1),
                                     jnp.concatenate([jnp.zeros_like(z[1]), z[1]], axis=1)], axis=0)
    return {
        'g_pre_mix': row(a['g_pre_mix'][l]), 'g_post_mix': row(a['g_post_mix'][l]),
        'g_pre_ffn': row(a['g_pre_ffn'][l]), 'g_post_ffn': row(a['g_post_ffn'][l]),
        'w_mla': w_mla.astype(BF16),
        'w_rwkv': w_in[:, p_mla:p_mla + P_RWKV].astype(BF16),
        'w_lru': w_in[:, p_mla + P_RWKV:].astype(BF16),
        'g_q_a': row(a['g_q_a'][l]), 'g_kv_a': row(a['g_kv_a'][l]),
        'wq': wq.astype(BF16), 'wqr': wqr.astype(BF16), 'wk': wk.astype(BF16), 'wv': wv.astype(BF16),
        'g_mla_out': row(a['g_mla_out'][l]),
        'mu_rwkv': row(a['mu_rwkv'][l]), 'k_k': row(a['k_k'][l]), 'k_a': row(a['k_a'][l]), 'r_k': row(a['r_k'][l]),
        'w0_cat': row(two(a['w0_rwkv'][l])), 'w2_bd': bd2(a['w2_rwkv'][l]),
        'a0_cat': row(two(a['a0_rwkv'][l])), 'a2_bd': bd2(a['a2_rwkv'][l]),
        'g2_rwkv': a['g2_rwkv'][l], 'ln_x_w': row(a['ln_x_w'][l]), 'ln_x_b': row(a['ln_x_b'][l]),
        'conv_w': a['conv_w'][l], 'conv_b': a['conv_b'][l],
        'wa_bd': _block_diag(a['w_lru_a'][l]), 'b_lru_a': a['b_lru_a'][l],
        'wx_bd': _block_diag(a['w_lru_x'][l]), 'b_lru_x': a['b_lru_x'][l],
        'lam': a['lam'][l], 'g_lru_out': row(a['g_lru_out'][l]),
        'w_out': a['w_out'][l].astype(BF16),
        'wr_t': a['w_router'][l].T, 'b_router_col': a['b_router'][l].reshape(N_EXPERTS, 1),
        'w_e_gate': a['w_e_gate'][l].astype(BF16), 'w_e_up': a['w_e_up'][l].astype(BF16),
        'w_e_down': a['w_e_down'][l].astype(BF16),
        'w_s_gate': a['w_s_gate'][l].astype(BF16), 'w_s_up': a['w_s_up'][l].astype(BF16),
        'w_s_down': a['w_s_down'][l].astype(BF16),
    }


def _mixer(x, grp, mod, lp, ctx, tables, prev):
    _, nb, t = grp
    tt = min(t, 512)
    p_mla, p_rwkv, p_lru = _in_proj(x, grp, lp['g_pre_mix'], mod['sc1'], mod['sh1'], lp['w_mla'], lp['w_rwkv'],
                                    lp['w_lru'], tt)
    if ctx is None:
        q, k, v, new_cache = _mla_prep(p_mla, lp['g_q_a'], lp['g_kv_a'], lp['wq'], None, lp['wk'], lp['wv'], None, tt)
        kc = vc = None
        s0 = jnp.zeros((nb, 2, D_RWKV, D_RWKV), F32)
        h0 = jnp.zeros((nb, 2, 1, D_LRU), F32)
    else:
        cache, s_rwkv, s_lru = ctx
        q, k, v = _mla_prep(p_mla, lp['g_q_a'], lp['g_kv_a'], lp['wq'], lp['wqr'], lp['wk'], lp['wv'], tables, tt)
        new_cache = None
        kr_placed = jnp.pad(cache[..., KV_LORA:], ((0, 0), (0, 0), (NOPE_DIM, HEAD_PAD - QK_DIM)))
        kc, vc = _ctx_kv(cache[..., :KV_LORA], kr_placed, lp['wk'], lp['wv'])
        s0 = _block_diag(s_rwkv)
        h0 = s_lru[:, :, None, :]
    o = _attention(q, k, v, kc, vc, min(t, 256))

    r, vv, kk, lw, ag, kd, g, bonus = _rwkv_prep(p_rwkv, lp, tt)
    y_f, y_b, s_fin = _rwkv_scan(r, vv, kk, lw, ag, kd, s0)

    hp, hn = _halo_rows(p_lru[..., :D_LRU], tt, SUBLANES)
    h_f, fin_f = _lru_dir(p_lru, hp, lp, 0, h0[:, 0], tt)
    h_b, fin_b = _lru_dir(p_lru, hn, lp, 1, h0[:, 1], tt)

    x1, h2 = _out_proj(x, grp, o, y_f, y_b, bonus, g, h_f, h_b, p_lru, lp, mod, tt, prev)
    states = None
    if ctx is None:
        states = (new_cache, _diag_blocks(s_fin, H_RWKV), jnp.concatenate([fin_f, fin_b], axis=1))
    return x1, h2, states


def kernel(x_prompt, x_sample, c, cache_mla, state_rwkv, state_lru, c_ctx, w_ada, b_ada, g_pre_mix, g_post_mix, g_pre_ffn, g_post_ffn, w_in, g_q_a, w_q_b, g_kv_a, w_kv_b, g_mla_out, mu_rwkv, w0_rwkv, w2_rwkv, a0_rwkv, a2_rwkv, g2_rwkv, k_k, k_a, r_k, ln_x_w, ln_x_b, conv_w, conv_b, w_lru_a, b_lru_a, w_lru_x, b_lru_x, lam, g_lru_out, w_out, w_router, b_router, w_e_gate, w_e_up, w_e_down, w_s_gate, w_s_up, w_s_down):
    a = dict(w_in=w_in, g_pre_mix=g_pre_mix, g_post_mix=g_post_mix, g_pre_ffn=g_pre_ffn, g_post_ffn=g_post_ffn,
             g_q_a=g_q_a, w_q_b=w_q_b, g_kv_a=g_kv_a, w_kv_b=w_kv_b, g_mla_out=g_mla_out, mu_rwkv=mu_rwkv,
             w0_rwkv=w0_rwkv, w2_rwkv=w2_rwkv, a0_rwkv=a0_rwkv, a2_rwkv=a2_rwkv, g2_rwkv=g2_rwkv, k_k=k_k, k_a=k_a,
             r_k=r_k.reshape(r_k.shape[0], -1), ln_x_w=ln_x_w, ln_x_b=ln_x_b, conv_w=conv_w, conv_b=conv_b,
             w_lru_a=w_lru_a, b_lru_a=b_lru_a, w_lru_x=w_lru_x, b_lru_x=b_lru_x, lam=lam, g_lru_out=g_lru_out,
             w_out=w_out, w_router=w_router, b_router=b_router, w_e_gate=w_e_gate, w_e_up=w_e_up,
             w_e_down=w_e_down, w_s_gate=w_s_gate, w_s_up=w_s_up, w_s_down=w_s_down)
    n_layers = w_in.shape[0]
    nbp, tp, d = x_prompt.shape
    nbs, ts, _ = x_sample.shape
    rows = -(-(1 + nbs) // SUBLANES) * SUBLANES
    cvecs = jnp.concatenate([c_ctx[None, :], c, jnp.zeros((rows - 1 - nbs, d), F32)], axis=0)
    mods = _ada(cvecs, w_ada, b_ada)
    names = ('sh1', 'sc1', 'g1', 'sh2', 'sc2', 'g2')
    tables = _rope_tables(ts)
    tn = TOKEN_TILE

    grp_p, grp_s = (0, nbp, tp), (nbp * tp, nbs, ts)
    x = jnp.concatenate([x_prompt.reshape(-1, d), x_sample.reshape(-1, d)], axis=0)
    caches, rwkv_states, lru_states = [], [], []
    for l in range(n_layers):
        lp = _layer_operands(l, a)
        mod_p = {nm: mods[l, 0:1, i * d:(i + 1) * d][:, None, :] for i, nm in enumerate(names)}
        mod_s = {nm: mods[l, 1:1 + nbs, i * d:(i + 1) * d][:, None, :] for i, nm in enumerate(names)}
        x1, h2, states = _mixer(x, grp_p, mod_p, lp, None, None, None)
        x1, h2, _ = _mixer(x, grp_s, mod_s, lp, (cache_mla[:, l], state_rwkv[:, l], state_lru[:, l]), tables,
                           (x1, h2))
        caches.append(states[0])
        rwkv_states.append(states[1])
        lru_states.append(states[2])
        gate_rows = jnp.concatenate([jnp.repeat(mod_p['g2'], nbp * tp // tn, axis=0),
                                     jnp.repeat(mod_s['g2'], ts // tn, axis=0)], axis=0)
        x = _moe(x1, h2, gate_rows, lp)
    yp = x[:nbp * tp].reshape(nbp, tp, d)
    ys = x[nbp * tp:].reshape(nbs, ts, d)
    return (yp, ys, jnp.stack(caches, axis=1), jnp.stack(rwkv_states, axis=1), jnp.stack(lru_states, axis=1))
```

```python
import functools

import numpy as np
import jax
import jax.numpy as jnp
from jax import lax
from jax.experimental import pallas as pl
from jax.experimental.pallas import tpu as pltpu

F32 = jnp.float32
BF16 = jnp.bfloat16
I32 = jnp.int32

D_MODEL = 1024
GRID_W = 64
EPS = 1e-6
H_MLA = 4
Q_LORA = 256
KV_LORA = 128
NOPE_DIM = 64
ROPE_DIM = 32
V_DIM = 128
QK_DIM = NOPE_DIM + ROPE_DIM
ROPE_THETA = 10000.0
HEAD_PAD = 128
H_RWKV = 4
N_RWKV = 64
D_RWKV = H_RWKV * N_RWKV
W_LORA = 64
A_LORA = 64
G_LORA = 128
GN_EPS = 64e-5
D_LRU = 256
H_LRU = 4
BS_LRU = D_LRU // H_LRU
CONV_W = 4
LRU_C = 8.0
D_MLA_OUT = H_MLA * V_DIM
P_MLA_PAD = Q_LORA + KV_LORA + HEAD_PAD
P_RWKV = 3 * D_RWKV + 2 * W_LORA + 2 * A_LORA + G_LORA
P_LRU = 2 * D_LRU
N_EXPERTS = 64
TOP_K = 6
N_GROUPS = 8
TOPK_GROUPS = 4
GROUP_SIZE = N_EXPERTS // N_GROUPS
D_EXPERT = 256
ROUTED_SCALE = 2.5

SUBLANES = 8
RWKV_CHUNK = 64
RWKV_ROWS = 2
MOE_BLOCK = 512
ROUTER_TILE = 256
DISPATCH_SHIFT = 5
DISPATCH_CHUNK = 1 << DISPATCH_SHIFT
TOKEN_TILE = 256
COMBINE_ROWS = TOP_K * TOKEN_TILE + N_EXPERTS * DISPATCH_CHUNK
COMBINE_PIECES = 4
SLOT_ROWS = 8
VMEM_LIMIT = 56 * 1024 * 1024


def _cparams(*sem):
    return pltpu.CompilerParams(dimension_semantics=sem, vmem_limit_bytes=VMEM_LIMIT)


def _sigmoid(x):
    return 1.0 / (1.0 + jnp.exp(-x))


def _softplus(x):
    return jnp.maximum(x, 0.0) + jnp.log1p(jnp.exp(-jnp.abs(x)))


def _expm1(z):
    e = jnp.exp(z)
    direct = (e == 1.0) | (z < -1.0)
    corrected = (e - 1.0) * z / jnp.log(jnp.where(direct, 2.0, e))
    return jnp.where(e == 1.0, z, jnp.where(z < -1.0, e - 1.0, corrected))


def _rms(x, g):
    return x * lax.rsqrt(jnp.mean(x * x, axis=-1, keepdims=True) + EPS) * g


def _split2(x):
    hi = x.astype(BF16)
    lo = (x - hi.astype(F32)).astype(BF16)
    return hi, lo


def _split3(x):
    hi = x.astype(BF16)
    r1 = x - hi.astype(F32)
    mid = r1.astype(BF16)
    lo = (r1 - mid.astype(F32)).astype(BF16)
    return hi, mid, lo


_NN = (((1,), (0,)), ((), ()))
_NT = (((1,), (1,)), ((), ()))


def _dg(a, b, dims):
    return lax.dot_general(a, b, dims, preferred_element_type=F32)


def _dot1(a, b, dims=_NN):
    return _dg(a.astype(BF16), b.astype(BF16), dims)


def _dot3(a, b, dims=_NN):
    ah, al = _split2(a)
    bh, bl = _split2(b)
    return _dg(ah, bh, dims) + (_dg(ah, bl, dims) + _dg(al, bh, dims))


def _dot_x(a, b_exact, dims=_NN):
    h, m, l = _split3(a)
    bb = b_exact.astype(BF16)
    return _dg(h, bb, dims) + (_dg(m, bb, dims) + _dg(l, bb, dims))


def _x_dot(a_exact, b, dims=_NN):
    h, m, l = _split3(b)
    aa = a_exact.astype(BF16)
    return _dg(aa, h, dims) + (_dg(aa, m, dims) + _dg(aa, l, dims))


def _head_ones(n, seg):
    r = lax.broadcasted_iota(I32, (n, n), 0) // seg
    c = lax.broadcasted_iota(I32, (n, n), 1) // seg
    return (r == c).astype(F32)


def _ada_body(c_ref, w_ref, b_ref, o_ref):
    cv = c_ref[...]
    s = cv * _sigmoid(cv)
    o_ref[0] = _dot3(s, w_ref[0]) + b_ref[0]


def _ada(cvecs, w_ada, b_ada):
    n_layers, d, d6 = w_ada.shape
    rows = cvecs.shape[0]
    nt = d6 // d
    return pl.pallas_call(
        _ada_body,
        grid=(n_layers, nt),
        in_specs=[
            pl.BlockSpec((rows, d), lambda l, j: (0, 0)),
            pl.BlockSpec((1, d, d), lambda l, j: (l, 0, j)),
            pl.BlockSpec((1, 1, d), lambda l, j: (l, 0, j)),
        ],
        out_specs=pl.BlockSpec((1, rows, d), lambda l, j: (l, 0, j)),
        out_shape=jax.ShapeDtypeStruct((n_layers, rows, d6), F32),
        compiler_params=_cparams("parallel", "parallel"),
        name="ada",
    )(cvecs, w_ada, b_ada.reshape(n_layers, 1, d6))


def _mod_spec(per_batch):
    if per_batch:
        return pl.BlockSpec((1, 1, D_MODEL), lambda b, i: (b, 0, 0))
    return pl.BlockSpec((1, 1, D_MODEL), lambda b, i: (0, 0, 0))


def _full(shape):
    nd = len(shape)
    return pl.BlockSpec(shape, lambda *_: (0,) * nd)


def _in_proj_body(x_ref, g_ref, sc_ref, sh_ref, w1_ref, w2_ref, w3_ref, o1_ref, o2_ref, o3_ref):
    h = _rms(x_ref[...], g_ref[...]) * (1.0 + sc_ref[0]) + sh_ref[0]
    hb = h.astype(BF16)
    o1_ref[0] = _dg(hb, w1_ref[...], _NN)
    o2_ref[0] = _dg(hb, w2_ref[...], _NN)
    o3_ref[0] = _dg(hb, w3_ref[...], _NN)


def _group_rows(grp, tt):
    row_off, nb, t = grp
    base, per = row_off // tt, t // tt
    return pl.BlockSpec((tt, D_MODEL), lambda b, i: (base + b * per + i, 0))


def _in_proj(x, grp, g, sc, sh, w_mla, w_rwkv, w_lru, tt):
    _, nb, t = grp
    d = x.shape[1]
    per_batch = sc.shape[0] > 1
    outs = [w_mla.shape[1], w_rwkv.shape[1], w_lru.shape[1]]
    return pl.pallas_call(
        _in_proj_body,
        grid=(nb, t // tt),
        in_specs=[
            _group_rows(grp, tt),
            _full((1, d)),
            _mod_spec(per_batch),
            _mod_spec(per_batch),
            _full(w_mla.shape),
            _full(w_rwkv.shape),
            _full(w_lru.shape),
        ],
        out_specs=[pl.BlockSpec((1, tt, n), lambda b, i: (b, i, 0)) for n in outs],
        out_shape=[jax.ShapeDtypeStruct((nb, t, n), F32) for n in outs],
        compiler_params=_cparams("parallel", "parallel"),
        name="in_proj",
    )(x, g, sc, sh, w_mla, w_rwkv, w_lru)


def _rope_lanes():
    lane = lax.broadcasted_iota(I32, (1, HEAD_PAD), 1)
    return ((lane >= NOPE_DIM) & (lane < QK_DIM)).astype(F32)


def _mla_prep_body(rope, *refs):
    if rope:
        (p_ref, gq_ref, gkv_ref, wq_ref, wqr_ref, wk_ref, wv_ref, cq_ref, sq_ref, cr_ref, sr_ref,
         q_ref, k_ref, v_ref) = refs
    else:
        p_ref, gq_ref, gkv_ref, wq_ref, wk_ref, wv_ref, q_ref, k_ref, v_ref, c_ref = refs
    p = p_ref[0]
    cq = _rms(p[:, :Q_LORA], gq_ref[...])
    ckv = _rms(p[:, Q_LORA:Q_LORA + KV_LORA], gkv_ref[...])
    kr = p[:, Q_LORA + KV_LORA:]
    cqb = cq.astype(BF16)
    ckvb = ckv.astype(BF16)
    scale = QK_DIM ** -0.5 * float(np.log2(np.e))
    qa = _dg(cqb, wq_ref[...], _NN) * scale
    ka = _dg(ckvb, wk_ref[...], _NN)
    va = _dg(ckvb, wv_ref[...], _NN)
    if rope:
        qr = _dg(cqb, wqr_ref[...], _NN) * scale
        krp = kr * cr_ref[...] + pltpu.roll(kr, ROPE_DIM, 1) * sr_ref[...]
    else:
        krp = kr * _rope_lanes()
    for h in range(H_MLA):
        sl = slice(h * HEAD_PAD, (h + 1) * HEAD_PAD)
        qh = qa[:, sl]
        if rope:
            qh = qh * cq_ref[...] + qr[:, sl] * sq_ref[...]
        q_ref[0, h] = qh.astype(BF16)
        k_ref[0, h] = (ka[:, sl] + krp).astype(BF16)
        v_ref[0, h] = va[:, sl].astype(BF16)
    if not rope:
        c_ref[0, :, :KV_LORA] = ckv
        c_ref[0, :, KV_LORA:] = pltpu.roll(kr, HEAD_PAD - NOPE_DIM, 1)[:, :ROPE_DIM]


def _mla_prep(p_mla, gq, gkv, wq, wqr, wk, wv, tables, tt):
    nb, t, pw = p_mla.shape
    rope = tables is not None
    hw = H_MLA * HEAD_PAD
    head_spec = pl.BlockSpec((1, H_MLA, tt, HEAD_PAD), lambda b, i: (b, 0, i, 0))
    head_shape = jax.ShapeDtypeStruct((nb, H_MLA, t, HEAD_PAD), BF16)
    in_specs = [pl.BlockSpec((1, tt, pw), lambda b, i: (b, i, 0)), _full(gq.shape), _full(gkv.shape), _full(wq.shape)]
    args = [p_mla, gq, gkv, wq]
    if rope:
        in_specs.append(_full(wqr.shape))
        args.append(wqr)
    in_specs += [_full(wk.shape), _full(wv.shape)]
    args += [wk, wv]
    out_specs = [head_spec, head_spec, head_spec]
    out_shape = [head_shape, head_shape, head_shape]
    if rope:
        in_specs += [pl.BlockSpec((tt, HEAD_PAD), lambda b, i: (i, 0))] * 4
        args += list(tables)
    else:
        out_specs.append(pl.BlockSpec((1, tt, KV_LORA + ROPE_DIM), lambda b, i: (b, i, 0)))
        out_shape.append(jax.ShapeDtypeStruct((nb, t, KV_LORA + ROPE_DIM), F32))
    del hw
    return pl.pallas_call(
        functools.partial(_mla_prep_body, rope),
        grid=(nb, t // tt),
        in_specs=in_specs,
        out_specs=out_specs,
        out_shape=out_shape,
        compiler_params=_cparams("parallel", "parallel"),
        name="mla_prep_rope" if rope else "mla_prep",
    )(*args)


def _ctx_kv_body(lat_ref, kr_ref, wk_ref, wv_ref, k_ref, v_ref):
    latb = lat_ref[0].astype(BF16)
    ka = _dg(latb, wk_ref[...], _NN)
    va = _dg(latb, wv_ref[...], _NN)
    kr = kr_ref[0]
    for h in range(H_MLA):
        sl = slice(h * HEAD_PAD, (h + 1) * HEAD_PAD)
        k_ref[0, h] = (ka[:, sl] + kr).astype(BF16)
        v_ref[0, h] = va[:, sl].astype(BF16)


def _ctx_kv(lat, kr_placed, wk, wv):
    nb, s, _ = lat.shape
    head_spec = pl.BlockSpec((1, H_MLA, s, HEAD_PAD), lambda b: (b, 0, 0, 0))
    head_shape = jax.ShapeDtypeStruct((nb, H_MLA, s, HEAD_PAD), BF16)
    return pl.pallas_call(
        _ctx_kv_body,
        grid=(nb,),
        in_specs=[
            pl.BlockSpec((1, s, KV_LORA), lambda b: (b, 0, 0)),
            pl.BlockSpec((1, s, HEAD_PAD), lambda b: (b, 0, 0)),
            _full(wk.shape),
            _full(wv.shape),
        ],
        out_specs=[head_spec, head_spec],
        out_shape=[head_shape, head_shape],
        compiler_params=_cparams("parallel"),
        name="ctx_kv",
    )(lat, kr_placed, wk, wv)


def _attn_body(has_ctx, *refs):
    if has_ctx:
        q_ref, k_ref, v_ref, kc_ref, vc_ref, o_ref = refs
    else:
        q_ref, k_ref, v_ref, o_ref = refs
    q = q_ref[0, 0]
    s = _dg(q, k_ref[0, 0], _NT)
    m = jnp.max(s, axis=-1, keepdims=True)
    if has_ctx:
        sc = _dg(q, kc_ref[0, 0], _NT)
        m = jnp.maximum(m, jnp.max(sc, axis=-1, keepdims=True))
    p = jnp.exp2(s - m)
    l = jnp.sum(p, axis=-1, keepdims=True)
    o = _dg(p.astype(BF16), v_ref[0, 0], _NN)
    if has_ctx:
        pc = jnp.exp2(sc - m)
        l = l + jnp.sum(pc, axis=-1, keepdims=True)
        o = o + _dg(pc.astype(BF16), vc_ref[0, 0], _NN)
    o_ref[0] = o / l


def _attention(q, k, v, kc, vc, tq):
    nb, nh, t, hd = q.shape
    s = k.shape[2]
    has_ctx = kc is not None
    in_specs = [
        pl.BlockSpec((1, 1, tq, hd), lambda b, h, i: (b, h, i, 0)),
        pl.BlockSpec((1, 1, s, hd), lambda b, h, i: (b, h, 0, 0)),
        pl.BlockSpec((1, 1, s, V_DIM), lambda b, h, i: (b, h, 0, 0)),
    ]
    args = [q, k, v]
    if has_ctx:
        sc = kc.shape[2]
        in_specs += [
            pl.BlockSpec((1, 1, sc, hd), lambda b, h, i: (b, h, 0, 0)),
            pl.BlockSpec((1, 1, sc, V_DIM), lambda b, h, i: (b, h, 0, 0)),
        ]
        args += [kc, vc]
    return pl.pallas_call(
        functools.partial(_attn_body, has_ctx),
        grid=(nb, nh, t // tq),
        in_specs=in_specs,
        out_specs=pl.BlockSpec((1, tq, V_DIM), lambda b, h, i: (b, i, h)),
        out_shape=jax.ShapeDtypeStruct((nb, t, nh * V_DIM), F32),
        compiler_params=_cparams("parallel", "parallel", "parallel"),
        name="attention",
    )(*args)


def _rwkv_prep_body(p_ref, hp_ref, hn_ref, mu_ref, kk_ref, ka_ref, rk_ref, w0_ref, w2_ref, a0_ref, a2_ref, g2_ref,
                    r_out, v_out, kk_out, lw_out, a_out, kd_out, g_out, bonus_out):
    p = p_ref[0]
    tt = p.shape[0]
    row = lax.broadcasted_iota(I32, (tt, 1), 0)
    prev = jnp.where(row == 0, hp_ref[0, 0], pltpu.roll(p, 1, 0))
    nxt = jnp.where(row == tt - 1, hn_ref[0, 0], pltpu.roll(p, tt - 1, 0))
    ps = p + (0.5 * (prev + nxt) - p) * mu_ref[...]
    r = ps[:, :D_RWKV]
    k = ps[:, D_RWKV:2 * D_RWKV]
    v = ps[:, 2 * D_RWKV:3 * D_RWKV]
    wl = ps[:, 3 * D_RWKV:3 * D_RWKV + 2 * W_LORA]
    al = ps[:, 3 * D_RWKV + 2 * W_LORA:3 * D_RWKV + 2 * W_LORA + 2 * A_LORA]
    gl = ps[:, 3 * D_RWKV + 2 * W_LORA + 2 * A_LORA:]
    ones = _head_ones(D_RWKV, N_RWKV)
    kk = k * kk_ref[...]
    kk = kk / jnp.maximum(jnp.sqrt(_dot_x(kk * kk, ones)), 1e-12)
    wlin = w0_ref[...] + _dot3(jnp.tanh(wl), w2_ref[...])
    lw = -jnp.exp(-_softplus(-wlin) - 0.5)
    ag = _sigmoid(a0_ref[...] + _dot3(al, a2_ref[...]))
    bonus = jnp.zeros_like(v)
    for d in range(2):
        sl = slice(d * D_RWKV, (d + 1) * D_RWKV)
        kd = k * (1.0 + (ag[:, sl] - 1.0) * ka_ref[...])
        kd_out[0, :, sl] = kd
        bonus = bonus + _dot_x(r * kd * rk_ref[...], ones) * v
    r_out[0] = r
    v_out[0] = v
    kk_out[0] = kk
    lw_out[0] = lw
    a_out[0] = ag
    g_out[0] = _dot1(_sigmoid(gl), g2_ref[...])
    bonus_out[0] = bonus


def _halo_rows(x, tt, width):
    nb, t, c = x.shape
    nt = t // tt
    xt = x.reshape(nb, nt, tt, c)
    zero = jnp.zeros((nb, 1, width, c), x.dtype)
    prev = jnp.concatenate([zero, xt[:, :-1, tt - width:]], axis=1)
    nxt = jnp.concatenate([xt[:, 1:, :width], zero], axis=1)
    return prev, nxt


def _rwkv_prep(p_rwkv, lp, tt):
    nb, t, pw = p_rwkv.shape
    hp, hn = _halo_rows(p_rwkv, tt, 1)
    tok = lambda n: pl.BlockSpec((1, tt, n), lambda b, i: (b, i, 0))
    halo = pl.BlockSpec((1, 1, 1, pw), lambda b, i: (b, i, 0, 0))
    small = [lp['mu_rwkv'], lp['k_k'], lp['k_a'], lp['r_k'], lp['w0_cat'], lp['w2_bd'], lp['a0_cat'], lp['a2_bd'],
             lp['g2_rwkv']]
    widths = [D_RWKV, D_RWKV, D_RWKV, 2 * D_RWKV, 2 * D_RWKV, 2 * D_RWKV, D_RWKV, D_RWKV]
    return pl.pallas_call(
        _rwkv_prep_body,
        grid=(nb, t // tt),
        in_specs=[tok(pw), halo, halo] + [_full(a.shape) for a in small],
        out_specs=[tok(n) for n in widths],
        out_shape=[jax.ShapeDtypeStruct((nb, t, n), F32) for n in widths],
        compiler_params=_cparams("parallel", "parallel"),
        name="rwkv_prep",
    )(p_rwkv, hp, hn, *small)


def _rwkv_chunk(reverse, r, v, kk, lw, ag, kd, s_old):
    cl = r.shape[0]
    hw = D_RWKV

    def later(rows, cols, strict):
        t = lax.broadcasted_iota(I32, (rows, cols), 0)
        s = lax.broadcasted_iota(I32, (rows, cols), 1) % cl
        dlt = (s - t) if reverse else (t - s)
        return (dlt > 0) if strict else (dlt >= 0)

    cum = _x_dot(later(cl, cl, False).astype(F32), lw)
    yield
    tot = jnp.sum(lw, axis=0, keepdims=True)
    e_neg = jnp.exp(-cum)
    e_rem = jnp.exp(tot - cum)
    at = -kk * jnp.exp(cum - lw)
    rt = r * jnp.exp(cum)
    b0 = kk * ag
    bt = b0 * e_neg
    kt = kd * e_neg
    bh = b0 * e_rem
    kh = kd * e_rem
    w_c = jnp.exp(tot)

    lane_head = lax.broadcasted_iota(I32, (1, hw), 1) // N_RWKV
    hms = [(lane_head == h).astype(F32) for h in range(H_RWKV)]
    heads = range(H_RWKV)
    bk = [jnp.concatenate([bt * hm, kt * hm], axis=0) for hm in hms]
    strict = later(cl, 2 * cl, True)
    incl = later(cl, 2 * cl, False)
    ls = [jnp.where(strict, _dot3(at, bk[h], _NT), 0.0) for h in heads]
    ly = [jnp.where(incl, _dot1(rt, bk[h], _NT), 0.0) for h in heads]
    yield

    x = [l[:, :cl] for l in ls]
    eye = (lax.broadcasted_iota(I32, (cl, cl), 0) == lax.broadcasted_iota(I32, (cl, cl), 1)).astype(F32)
    inv = [eye + xh for xh in x]
    span = 2
    while span < cl:
        x = [_dot3(xh, xh) for xh in x]
        yield
        inv = [ih + _dot3(ih, xh) for ih, xh in zip(inv, x)]
        yield
        span *= 2

    zero = jnp.zeros((cl, hw), F32)
    v_rows = jnp.concatenate([piece for hm in hms for piece in (zero, v * hm)], axis=0)
    rhs = _dot3(at, s_old, _NT) + _dot3(jnp.concatenate(ls, axis=1), v_rows)
    yield
    us = [_dot3(inv[h], rhs * hms[h]) for h in heads]
    yield
    um = us[0]
    for h in range(1, H_RWKV):
        um = um + us[h]
    uv_rows = jnp.concatenate([piece for h in heads for piece in (us[h], v * hms[h])], axis=0)
    y = _dot1(rt, s_old, _NT) + _dot1(jnp.concatenate(ly, axis=1), uv_rows)
    yield
    uv_t = jnp.concatenate([um, v], axis=0).T
    s_new = s_old * w_c + _dot3(uv_t, jnp.concatenate([bh, kh], axis=0)) * _head_ones(hw, N_RWKV)
    yield y, s_new


def _lockstep(gens):
    last = [None] * len(gens)
    live = list(range(len(gens)))
    while live:
        for i in list(live):
            try:
                last[i] = next(gens[i])
            except StopIteration:
                live.remove(i)
    return last


def _rwkv_scan_body(rf_ref, vf_ref, kkf_ref, lwf_ref, af_ref, kdf_ref, rb_ref, vb_ref, kkb_ref, lwb_ref, ab_ref,
                    kdb_ref, s0_ref, yf_ref, yb_ref, sf_ref, s_scr):
    c = pl.program_id(1)
    nc = pl.num_programs(1)

    @pl.when(c == 0)
    def _():
        s_scr[...] = s0_ref[...]

    nrow = rf_ref.shape[0]
    chains = []
    for i in range(nrow):
        chains.append(_rwkv_chunk(False, rf_ref[i], vf_ref[i], kkf_ref[i], lwf_ref[i], af_ref[i], kdf_ref[i],
                                  s_scr[i, 0]))
        chains.append(_rwkv_chunk(True, rb_ref[i], vb_ref[i], kkb_ref[i], lwb_ref[i], ab_ref[i], kdb_ref[i],
                                  s_scr[i, 1]))
    done = _lockstep(chains)
    for i in range(nrow):
        (y_f, s_f), (y_b, s_b) = done[2 * i], done[2 * i + 1]
        yf_ref[i] = y_f
        yb_ref[i] = y_b
        s_scr[i, 0] = s_f
        s_scr[i, 1] = s_b

    @pl.when(c == nc - 1)
    def _():
        sf_ref[...] = s_scr[...]


def _rwkv_scan(r, v, kk, lw, ag, kd, s0):
    nb, t, hw = r.shape
    cl = RWKV_CHUNK
    nr = RWKV_ROWS
    nc = t // cl
    fwd = lambda lane: pl.BlockSpec((nr, cl, hw), lambda b, c: (b, c, lane))
    bwd = lambda lane: pl.BlockSpec((nr, cl, hw), lambda b, c: (b, nc - 1 - c, lane))
    state = pl.BlockSpec((nr, 2, hw, hw), lambda b, c: (b, 0, 0, 0))
    y_shape = jax.ShapeDtypeStruct((nb, t, hw), F32)
    return pl.pallas_call(
        _rwkv_scan_body,
        grid=(nb // nr, nc),
        in_specs=[fwd(0)] * 6 + [bwd(0)] * 3 + [bwd(1)] * 3 + [state],
        out_specs=[fwd(0), bwd(0), state],
        out_shape=[y_shape, y_shape, jax.ShapeDtypeStruct((nb, 2, hw, hw), F32)],
        scratch_shapes=[pltpu.VMEM((nr, 2, hw, hw), F32)],
        compiler_params=_cparams("parallel", "arbitrary"),
        name="rwkv_scan",
    )(r, v, kk, lw, ag, kd, r, v, kk, lw, ag, kd, s0)


def _lru_body(reverse, p_ref, halo_ref, cw_ref, cb_ref, wa_ref, ba_ref, wx_ref, bx_ref, lam_ref, h0_ref,
              h_out, fin_out, a_scr, u_scr, carry_scr):
    i = pl.program_id(1)
    nt = pl.num_programs(1)
    tt = p_ref.shape[1]
    n_ext = tt + SUBLANES

    @pl.when(i == 0)
    def _():
        carry_scr[...] = jnp.broadcast_to(h0_ref[0], (SUBLANES, D_LRU))

    xb = p_ref[0, :, :D_LRU]
    halo = halo_ref[0, 0]
    xc = cb_ref[...] + jnp.zeros_like(xb)
    if reverse:
        xe = jnp.concatenate([xb, halo], axis=0)
        for k in range(CONV_W):
            sh = xb if k == 0 else pltpu.roll(xe, n_ext - k, 0)[:tt]
            xc = xc + cw_ref[k:k + 1, :] * sh
    else:
        xe = jnp.concatenate([halo, xb], axis=0)
        for k in range(CONV_W):
            s = CONV_W - 1 - k
            sh = xb if s == 0 else pltpu.roll(xe, s, 0)[SUBLANES:]
            xc = xc + cw_ref[k:k + 1, :] * sh
    gate_r = _sigmoid(_dot3(xc, wa_ref[...]) + ba_ref[...])
    gate_i = _sigmoid(_dot3(xc, wx_ref[...]) + bx_ref[...])
    log_a = -LRU_C * gate_r * _softplus(-lam_ref[...])
    a = jnp.exp(log_a)
    u = jnp.sqrt(-_expm1(2.0 * log_a)) * (gate_i * xc)

    rowm = lax.broadcasted_iota(I32, (tt, 1), 0) % SUBLANES
    for s in (1, 2, 4):
        if reverse:
            a_s = pltpu.roll(a, tt - s, 0)
            u_s = pltpu.roll(u, tt - s, 0)
            m = rowm < SUBLANES - s
        else:
            a_s = pltpu.roll(a, s, 0)
            u_s = pltpu.roll(u, s, 0)
            m = rowm >= s
        u = jnp.where(m, a * u_s + u, u)
        a = jnp.where(m, a * a_s, a)
    a_scr[...] = a
    u_scr[...] = u
    ng = tt // SUBLANES

    def group(g, carry):
        gi = (ng - 1 - g) if reverse else g
        r0 = pl.multiple_of(gi * SUBLANES, SUBLANES)
        hg = a_scr[pl.ds(r0, SUBLANES), :] * carry + u_scr[pl.ds(r0, SUBLANES), :]
        h_out[0, pl.ds(r0, SUBLANES), :] = hg
        edge = hg[0:1, :] if reverse else hg[SUBLANES - 1:SUBLANES, :]
        return jnp.broadcast_to(edge, (SUBLANES, D_LRU))

    carry = lax.fori_loop(0, ng, group, carry_scr[...])
    carry_scr[...] = carry

    @pl.when(i == nt - 1)
    def _():
        fin_out[0] = carry[0:1, :]


def _lru_dir(p_lru, halo, lp, d, h0, tt):
    nb, t, pw = p_lru.shape
    nt = t // tt
    reverse = d == 1
    tidx = (lambda i: nt - 1 - i) if reverse else (lambda i: i)
    small = [lp['conv_w'][d], lp['conv_b'][d:d + 1], lp['wa_bd'][d], lp['b_lru_a'][d:d + 1], lp['wx_bd'][d],
             lp['b_lru_x'][d:d + 1], lp['lam'][d:d + 1]]
    return pl.pallas_call(
        functools.partial(_lru_body, reverse),
        grid=(nb, nt),
        in_specs=[
            pl.BlockSpec((1, tt, pw), lambda b, i: (b, tidx(i), 0)),
            pl.BlockSpec((1, 1, SUBLANES, D_LRU), lambda b, i: (b, tidx(i), 0, 0)),
        ] + [_full(a.shape) for a in small] + [pl.BlockSpec((1, 1, D_LRU), lambda b, i: (b, 0, 0))],
        out_specs=[
            pl.BlockSpec((1, tt, D_LRU), lambda b, i: (b, tidx(i), 0)),
            pl.BlockSpec((1, 1, D_LRU), lambda b, i: (b, 0, 0)),
        ],
        out_shape=[jax.ShapeDtypeStruct((nb, t, D_LRU), F32), jax.ShapeDtypeStruct((nb, 1, D_LRU), F32)],
        scratch_shapes=[pltpu.VMEM((tt, D_LRU), F32), pltpu.VMEM((tt, D_LRU), F32), pltpu.VMEM((SUBLANES, D_LRU), F32)],
        compiler_params=_cparams("parallel", "arbitrary"),
        name="lru_bwd" if reverse else "lru_fwd",
    )(p_lru, halo, *small, h0)


def _gelu_tanh(x):
    return 0.5 * x * (1.0 + jnp.tanh(np.sqrt(2.0 / np.pi).astype(np.float32) * (x + 0.044715 * (x * x * x))))


def _out_proj_body(x_ref, o_ref, y0_ref, y1_ref, bonus_ref, g_ref, h0_ref, h1_ref, p_ref,
                   gmla_ref, lnw_ref, lnb_ref, glru_ref, wo_ref, gpost_ref, gate_ref, gpre_ref, sc_ref, sh_ref,
                   *rest):
    x1_out, h2_out = rest[-2:]
    o_mla = _rms(o_ref[0], gmla_ref[...])
    ones = _head_ones(D_RWKV, N_RWKV)
    y = y0_ref[0] + y1_ref[0]
    mu = _dot_x(y, ones) * (1.0 / N_RWKV)
    yc = y - mu
    var = _dot_x(yc * yc, ones) * (1.0 / N_RWKV)
    y = yc * lax.rsqrt(var + GN_EPS) * lnw_ref[...] + lnb_ref[...]
    o_rwkv = (y + bonus_ref[0]) * g_ref[0]
    o_lru = _rms((h0_ref[0] + h1_ref[0]) * _gelu_tanh(p_ref[0, :, D_LRU:]), glru_ref[...])
    m = (_dg(o_mla.astype(BF16), wo_ref[:D_MLA_OUT, :], _NN)
         + _dg(o_rwkv.astype(BF16), wo_ref[D_MLA_OUT:D_MLA_OUT + D_RWKV, :], _NN)
         + _dg(o_lru.astype(BF16), wo_ref[D_MLA_OUT + D_RWKV:, :], _NN))
    x1 = x_ref[...] + gate_ref[0] * _rms(m, gpost_ref[...])
    x1_out[...] = x1
    h2_out[...] = _rms(x1, gpre_ref[...]) * (1.0 + sc_ref[0]) + sh_ref[0]


def _out_proj(x, grp, o, y_f, y_b, bonus, g, h0, h1, p_lru, lp, mod, tt, prev):
    _, nb, t = grp
    per_batch = mod['g1'].shape[0] > 1
    tok = lambda n: pl.BlockSpec((1, tt, n), lambda b, i: (b, i, 0))
    rows = lambda: _group_rows(grp, tt)
    small = [lp['g_mla_out'], lp['ln_x_w'], lp['ln_x_b'], lp['g_lru_out'], lp['w_out'], lp['g_post_mix']]
    args = [x, o, y_f, y_b, bonus, g, h0, h1, p_lru, *small, mod['g1'], lp['g_pre_ffn'], mod['sc2'], mod['sh2']]
    in_specs = ([rows(), tok(D_MLA_OUT), tok(D_RWKV), tok(D_RWKV), tok(D_RWKV), tok(D_RWKV), tok(D_LRU), tok(D_LRU),
                 tok(P_LRU)] + [_full(a.shape) for a in small]
                + [_mod_spec(per_batch), _full(lp['g_pre_ffn'].shape), _mod_spec(per_batch), _mod_spec(per_batch)])
    aliases = {}
    if prev is not None:
        aliases = {len(args): 0, len(args) + 1: 1}
        args += list(prev)
        in_specs += [pl.BlockSpec(memory_space=pl.ANY)] * 2
    return pl.pallas_call(
        _out_proj_body,
        grid=(nb, t // tt),
        in_specs=in_specs,
        out_specs=[rows(), rows()],
        out_shape=[jax.ShapeDtypeStruct(x.shape, F32)] * 2,
        input_output_aliases=aliases,
        compiler_params=_cparams("parallel", "parallel"),
        name="out_proj",
    )(*args)


def _first_index(mask, idx, sentinel):
    return jnp.min(jnp.where(mask, idx, sentinel), axis=0, keepdims=True)


def _router_body(h_ref, wr_ref, br_ref, e_out, pos_out, loc_out, pad_out, stat_out, w_out, cnt_out, carry):
    i = pl.program_id(0)
    tn = h_ref.shape[0]

    @pl.when(i == 0)
    def _():
        carry[...] = jnp.zeros_like(carry)

    logits = lax.dot_general(wr_ref[...], h_ref[...], _NT, precision=lax.Precision.HIGHEST,
                             preferred_element_type=F32)
    scores = _sigmoid(logits)
    sel = scores + br_ref[...]

    i8 = lax.broadcasted_iota(I32, (GROUP_SIZE, tn), 0)
    blocks, gscore = [], []
    for g in range(N_GROUPS):
        blk = sel[g * GROUP_SIZE:(g + 1) * GROUP_SIZE, :]
        m1 = jnp.max(blk, axis=0, keepdims=True)
        f1 = _first_index(blk == m1, i8, GROUP_SIZE)
        m2 = jnp.max(jnp.where(i8 == f1, -jnp.inf, blk), axis=0, keepdims=True)
        blocks.append(blk)
        gscore.append(m1 + m2)
    masked = []
    for g in range(N_GROUPS):
        beaten = jnp.zeros((1, tn), I32)
        for o in range(N_GROUPS):
            if o == g:
                continue
            wins = (gscore[o] >= gscore[g]) if o < g else (gscore[o] > gscore[g])
            beaten = beaten + wins.astype(I32)
        masked.append(jnp.where(beaten < TOPK_GROUPS, blocks[g], -jnp.inf))
    cur = jnp.concatenate(masked, axis=0)

    ie = lax.broadcasted_iota(I32, (N_EXPERTS, tn), 0)
    firsts, raw = [], []
    chosen = jnp.zeros((N_EXPERTS, tn), F32)
    for _ in range(TOP_K):
        mx = jnp.max(cur, axis=0, keepdims=True)
        first = _first_index(cur == mx, ie, N_EXPERTS)
        hit = ie == first
        firsts.append(first)
        raw.append(jnp.sum(jnp.where(hit, scores, 0.0), axis=0, keepdims=True))
        cur = jnp.where(hit, -jnp.inf, cur)
        chosen = chosen + hit.astype(F32)
    total = raw[0]
    for k in range(1, TOP_K):
        total = total + raw[k]

    before = (lax.broadcasted_iota(I32, (tn, tn), 0) < lax.broadcasted_iota(I32, (tn, tn), 1)).astype(BF16)
    rank_tile = _dg(chosen.astype(BF16), before, _NN)
    old_carry = carry[...]
    rank = rank_tile + old_carry[:, 0:1]
    tile_cnt = jnp.broadcast_to(jnp.sum(chosen, axis=1, keepdims=True), old_carry.shape)
    lower = (lax.broadcasted_iota(I32, (N_EXPERTS, N_EXPERTS), 1)
             < lax.broadcasted_iota(I32, (N_EXPERTS, N_EXPERTS), 0)).astype(F32)
    tile_off = _x_dot(lower, tile_cnt)
    local = rank_tile + tile_off[:, 0:1]
    tile_chunks = jnp.floor((tile_cnt + (DISPATCH_CHUNK - 1)) * (1.0 / DISPATCH_CHUNK))
    chunk_off = _x_dot(lower, tile_chunks)
    padded = rank_tile + chunk_off[:, 0:1] * DISPATCH_CHUNK
    e_out[...] = jnp.zeros_like(e_out)
    pos_out[...] = jnp.zeros_like(pos_out)
    loc_out[...] = jnp.zeros_like(loc_out)
    pad_out[...] = jnp.zeros_like(pad_out)
    w_out[...] = jnp.zeros_like(w_out)
    for k in range(TOP_K):
        hit = ie == firsts[k]
        e_out[0, k:k + 1, :] = firsts[k]
        pos_out[0, k:k + 1, :] = jnp.sum(jnp.where(hit, rank, 0.0), axis=0, keepdims=True).astype(I32)
        loc_out[0, k:k + 1, :] = jnp.sum(jnp.where(hit, local, 0.0), axis=0, keepdims=True).astype(I32)
        pad_out[0, k:k + 1, :] = jnp.sum(jnp.where(hit, padded, 0.0), axis=0, keepdims=True).astype(I32)
        w_out[k:k + 1, :] = raw[k] / total * ROUTED_SCALE
    lane = lax.broadcasted_iota(I32, old_carry.shape, 1)
    stat_out[0] = jnp.where(lane == 0, tile_cnt, jnp.where(lane == 1, tile_off, jnp.where(
        lane == 2, old_carry, chunk_off))).astype(I32)
    new_carry = old_carry + tile_cnt
    carry[...] = new_carry
    cnt_out[...] = new_carry


def _router(h2, wr_t, b_col):
    n, d = h2.shape
    tn = ROUTER_TILE
    row_spec = pl.BlockSpec((SLOT_ROWS, tn), lambda i: (0, i))
    tile_spec = pl.BlockSpec((1, SLOT_ROWS, tn), lambda i: (i, 0, 0))
    cnt_spec = pl.BlockSpec((N_EXPERTS, 128), lambda i: (0, 0))
    tile_shape = jax.ShapeDtypeStruct((n // tn, SLOT_ROWS, tn), I32)
    return pl.pallas_call(
        _router_body,
        grid=(n // tn,),
        in_specs=[pl.BlockSpec((tn, d), lambda i: (i, 0)), _full(wr_t.shape), _full(b_col.shape)],
        out_specs=[tile_spec, tile_spec, tile_spec, tile_spec,
                   pl.BlockSpec((1, N_EXPERTS, 128), lambda i: (i, 0, 0)), row_spec, cnt_spec],
        out_shape=[tile_shape, tile_shape, tile_shape, tile_shape,
                   jax.ShapeDtypeStruct((n // tn, N_EXPERTS, 128), I32),
                   jax.ShapeDtypeStruct((SLOT_ROWS, n), F32), jax.ShapeDtypeStruct((N_EXPERTS, 128), F32)],
        scratch_shapes=[pltpu.VMEM((N_EXPERTS, 128), F32)],
        compiler_params=_cparams("arbitrary"),
        name="router",
    )(h2, wr_t, b_col)


ROW_LANES = 128
ROW_SUB = D_MODEL // 2 // ROW_LANES
U32 = jnp.uint32
HIGH_HALF = np.uint32(0xFFFF0000)


def _bf16_bits(x):
    return lax.bitcast_convert_type(x.astype(BF16).astype(F32), U32)


def _to_row_tiles(ref, x):
    rows, d = x.shape
    words = (_bf16_bits(x[:, :d // 2]) >> 16) | (_bf16_bits(x[:, d // 2:]) & HIGH_HALF)
    for s in range(ROW_SUB):
        ref[pl.ds(s, rows, stride=ROW_SUB), :] = words[:, s * ROW_LANES:(s + 1) * ROW_LANES]


def _from_row_tiles(ref, rows):
    words = jnp.concatenate([ref[pl.ds(s, rows, stride=ROW_SUB), :] for s in range(ROW_SUB)], axis=1)
    low = lax.bitcast_convert_type(words << 16, F32)
    high = lax.bitcast_convert_type(words & HIGH_HALF, F32)
    return jnp.concatenate([low, high], axis=1)


def _row_copy(src, src_off, dst, dst_off, sem):
    return pltpu.make_async_copy(src.at[pl.ds(pl.multiple_of(src_off, ROW_SUB), ROW_SUB), :],
                                 dst.at[pl.ds(pl.multiple_of(dst_off, ROW_SUB), ROW_SUB), :], sem)


def _load_slots(slot_ref, slot_smem, sem):
    cp = pltpu.make_async_copy(slot_ref.at[0], slot_smem, sem)
    cp.start()
    cp.wait()


def _chunk_copy(src, src_row, dst, dst_row, sem):
    n = DISPATCH_CHUNK * ROW_SUB
    return pltpu.make_async_copy(src.at[pl.ds(pl.multiple_of(src_row * ROW_SUB, ROW_SUB), n), :],
                                 dst.at[pl.ds(pl.multiple_of(dst_row * ROW_SUB, ROW_SUB), n), :], sem)


def _dispatch_body(start_ref, e_ref, pos_ref, loc_ref, stat_ref, h_ref, xs_in, slots_out, xs_out,
                   stat_smem, pending, staging, sem_s, sem):
    del xs_in
    i = pl.program_id(0)
    last = pl.num_programs(0) - 1
    tn = h_ref.shape[0]
    rows = TOP_K * tn
    cur = i % 2

    e = e_ref[0]
    slot = pos_ref[0]
    for x in range(N_EXPERTS):
        slot = slot + jnp.where(e == x, start_ref[x], 0)
    slots_out[0] = slot * ROW_SUB
    stats = pltpu.make_async_copy(stat_ref.at[0], stat_smem, sem_s)
    stats.start()

    @pl.when(i == 0)
    def _():
        pending[0] = 0
        pad = jnp.zeros((DISPATCH_CHUNK * ROW_SUB, ROW_LANES), U32)
        for b in range(2):
            staging[b, pl.ds(rows * ROW_SUB, DISPATCH_CHUNK * ROW_SUB), :] = pad

    loc = loc_ref[0]
    p_iota = lax.broadcasted_iota(I32, (rows, tn), 0)
    pick = jnp.zeros((rows, tn), F32)
    for k in range(TOP_K):
        pick = jnp.where(p_iota == loc[k:k + 1, :], 1.0, pick)
    _to_row_tiles(staging.at[cur], _dg(pick.astype(BF16), h_ref[...].astype(BF16), _NN))

    def drain(count):
        def one(c, carry):
            _chunk_copy(staging.at[0], 0, xs_out, 0, sem).wait()
            return carry
        lax.fori_loop(0, count, one, 0)

    drain(pending[0])
    stats.wait()

    def per_expert(x, total):
        count = stat_smem[x, 0]
        src = stat_smem[x, 1]
        dst = start_ref[x] + stat_smem[x, 2]
        n_chunks = lax.shift_right_logical(count + (DISPATCH_CHUNK - 1), DISPATCH_SHIFT)

        def one(c, carry):
            _chunk_copy(staging.at[cur], src + c * DISPATCH_CHUNK, xs_out, dst + c * DISPATCH_CHUNK, sem).start()
            return carry

        lax.fori_loop(0, n_chunks, one, 0)
        return total + n_chunks

    issued = lax.fori_loop(0, N_EXPERTS, per_expert, 0)
    pending[0] = issued

    @pl.when(i == last)
    def _():
        drain(issued)


def _zero_tail_body(lb_ref, o_ref):
    del lb_ref
    o_ref[...] = jnp.zeros_like(o_ref)


def _zero_tails(last_block, n_blocks):
    rows = MOE_BLOCK * ROW_SUB
    grid_spec = pltpu.PrefetchScalarGridSpec(
        num_scalar_prefetch=1,
        grid=(last_block.shape[0],),
        in_specs=[],
        out_specs=pl.BlockSpec((rows, ROW_LANES), lambda e, lb: (lb[e], 0)),
    )
    return pl.pallas_call(
        _zero_tail_body,
        grid_spec=grid_spec,
        out_shape=jax.ShapeDtypeStruct((n_blocks * rows, ROW_LANES), U32),
        compiler_params=_cparams("arbitrary"),
        name="zero_tails",
    )(last_block)


def _dispatch(row_start, e3, pos3, loc3, stats, h2, xs_zero):
    n, d = h2.shape
    tn = e3.shape[2]
    tile = lambda: pl.BlockSpec((1, SLOT_ROWS, tn), lambda i, st: (i, 0, 0))
    grid_spec = pltpu.PrefetchScalarGridSpec(
        num_scalar_prefetch=1,
        grid=(n // tn,),
        in_specs=[tile(), tile(), tile(), pl.BlockSpec((1, N_EXPERTS, 128), lambda i, st: (i, 0, 0)),
                  pl.BlockSpec((tn, d), lambda i, st: (i, 0)), pl.BlockSpec(memory_space=pl.ANY)],
        out_specs=[tile(), pl.BlockSpec(memory_space=pl.ANY)],
        scratch_shapes=[pltpu.SMEM((N_EXPERTS, 128), I32), pltpu.SMEM((1,), I32),
                        pltpu.VMEM((2, (TOP_K * tn + DISPATCH_CHUNK) * ROW_SUB, ROW_LANES), U32),
                        pltpu.SemaphoreType.DMA, pltpu.SemaphoreType.DMA],
    )
    return pl.pallas_call(
        _dispatch_body,
        grid_spec=grid_spec,
        out_shape=[jax.ShapeDtypeStruct((n // tn, SLOT_ROWS, tn), I32),
                   jax.ShapeDtypeStruct(xs_zero.shape, xs_zero.dtype)],
        input_output_aliases={6: 1},
        compiler_params=_cparams("arbitrary"),
        name="dispatch",
    )(row_start, e3, pos3, loc3, stats, h2, xs_zero)


def _swiglu(x, wg, wu, wd):
    g = _dg(x, wg, _NN)
    u = _dg(x, wu, _NN)
    return _dg((g * _sigmoid(g) * u).astype(BF16), wd, _NN)


EXPERT_RING = 3


def _expert_body(be_ref, nu_ref, xs_hbm, wg_ref, wu_ref, wd_ref, ys_ref, ring, sem):
    del be_ref
    b = pl.program_id(0)
    nb = pl.num_programs(0)
    rows = MOE_BLOCK * ROW_SUB

    def fetch(step):
        blk = jnp.minimum(step, nu_ref[0] - 1)
        slot = step % EXPERT_RING
        return pltpu.make_async_copy(xs_hbm.at[pl.ds(pl.multiple_of(blk * rows, rows), rows), :], ring.at[slot],
                                     sem.at[slot])

    @pl.when(b == 0)
    def _():
        for s in range(EXPERT_RING - 1):
            fetch(s).start()

    @pl.when(b + EXPERT_RING - 1 < nb)
    def _():
        fetch(b + EXPERT_RING - 1).start()

    fetch(b).wait()

    @pl.when(b < nu_ref[0])
    def _():
        x = _from_row_tiles(ring.at[b % EXPERT_RING], MOE_BLOCK).astype(BF16)
        _to_row_tiles(ys_ref, _swiglu(x, wg_ref[0], wu_ref[0], wd_ref[0]))

    @pl.when(b >= nu_ref[0])
    def _():
        ys_ref[...] = jnp.zeros_like(ys_ref)


def _experts(block_e, n_used, xs, wg, wu, wd):
    bm = MOE_BLOCK
    _, d, de = wg.shape
    rows = bm * ROW_SUB
    grid_spec = pltpu.PrefetchScalarGridSpec(
        num_scalar_prefetch=2,
        grid=(xs.shape[0] // rows,),
        in_specs=[
            pl.BlockSpec(memory_space=pl.ANY),
            pl.BlockSpec((1, d, de), lambda b, be, nu: (be[b], 0, 0)),
            pl.BlockSpec((1, d, de), lambda b, be, nu: (be[b], 0, 0)),
            pl.BlockSpec((1, de, d), lambda b, be, nu: (be[b], 0, 0)),
        ],
        out_specs=pl.BlockSpec((rows, ROW_LANES), lambda b, be, nu: (b, 0)),
        scratch_shapes=[pltpu.VMEM((EXPERT_RING, rows, ROW_LANES), U32), pltpu.SemaphoreType.DMA((EXPERT_RING,))],
    )
    return pl.pallas_call(
        _expert_body,
        grid_spec=grid_spec,
        out_shape=jax.ShapeDtypeStruct(xs.shape, xs.dtype),
        compiler_params=_cparams("arbitrary"),
        name="experts",
    )(block_e, n_used, xs, wg, wu, wd)


def _combine_body(start_ref, pad_ref, stat0_ref, statn_ref, w_ref, x1_ref, h2_ref, gate_ref, ys_hbm, wsg_ref,
                  wsu_ref, wsd_ref, gpost_ref, out_ref, stat_smem, pending, staging, sem_s, sem):
    i = pl.program_id(0)
    last = pl.num_programs(0) - 1
    tn = x1_ref.shape[0]
    cur = i % 2

    def fetch(stat_ref, which):
        stats = pltpu.make_async_copy(stat_ref.at[0], stat_smem, sem_s)
        stats.start()
        stats.wait()

        def per_expert(x, total):
            count = stat_smem[x, 0]
            src = start_ref[x] + stat_smem[x, 2]
            dst = stat_smem[x, 3] * DISPATCH_CHUNK
            n_chunks = lax.shift_right_logical(count + (DISPATCH_CHUNK - 1), DISPATCH_SHIFT)

            def one(c, carry):
                _chunk_copy(ys_hbm, src + c * DISPATCH_CHUNK, staging.at[which], dst + c * DISPATCH_CHUNK,
                            sem.at[which]).start()
                return carry

            lax.fori_loop(0, n_chunks, one, 0)
            return total + n_chunks

        pending[which] = lax.fori_loop(0, N_EXPERTS, per_expert, 0)

    @pl.when(i == 0)
    def _():
        staging[...] = jnp.zeros_like(staging)
        fetch(stat0_ref, 0)

    @pl.when(i < last)
    def _():
        fetch(statn_ref, 1 - cur)

    acc = _swiglu(h2_ref[...].astype(BF16), wsg_ref[...], wsu_ref[...], wsd_ref[...])

    def drain(c, carry):
        _chunk_copy(ys_hbm, 0, staging.at[0], 0, sem.at[cur]).wait()
        return carry

    lax.fori_loop(0, pending[cur], drain, 0)
    piece = COMBINE_ROWS // COMBINE_PIECES
    for q in range(COMBINE_PIECES):
        col = lax.broadcasted_iota(I32, (tn, piece), 1) + q * piece
        weight = jnp.zeros((tn, piece), F32)
        for k in range(TOP_K):
            weight = jnp.where(col == pad_ref[:, k:k + 1], w_ref[:, k:k + 1], weight)
        rows = _from_row_tiles(staging.at[cur, pl.ds(q * piece * ROW_SUB, piece * ROW_SUB), :], piece)
        acc = acc + _dg(weight.astype(BF16), rows.astype(BF16), _NN)
    out_ref[...] = x1_ref[...] + gate_ref[0] * _rms(acc, gpost_ref[...])


def _combine(row_start, pad_tok, stats, w_tok, x1, h2, gate_rows, ys, wsg, wsu, wsd, gpost):
    n, d = x1.shape
    tn = TOKEN_TILE
    tok = pl.BlockSpec((tn, d), lambda i, st: (i, 0))
    small = pl.BlockSpec((tn, SLOT_ROWS), lambda i, st: (i, 0))
    ntiles = n // tn
    grid_spec = pltpu.PrefetchScalarGridSpec(
        num_scalar_prefetch=1,
        grid=(ntiles,),
        in_specs=[small, pl.BlockSpec((1, N_EXPERTS, 128), lambda i, st: (0, 0, 0)),
                  pl.BlockSpec((1, N_EXPERTS, 128), lambda i, st: (jnp.minimum(i + 1, ntiles - 1), 0, 0)),
                  small, tok, tok,
                  pl.BlockSpec((1, 1, d), lambda i, st: (i, 0, 0)),
                  pl.BlockSpec(memory_space=pl.ANY),
                  pl.BlockSpec(wsg.shape, lambda i, st: (0, 0)), pl.BlockSpec(wsu.shape, lambda i, st: (0, 0)),
                  pl.BlockSpec(wsd.shape, lambda i, st: (0, 0)), pl.BlockSpec(gpost.shape, lambda i, st: (0, 0))],
        out_specs=tok,
        scratch_shapes=[pltpu.SMEM((N_EXPERTS, 128), I32), pltpu.SMEM((2,), I32),
                        pltpu.VMEM((2, COMBINE_ROWS * ROW_SUB, ROW_LANES), U32),
                        pltpu.SemaphoreType.DMA, pltpu.SemaphoreType.DMA((2,))],
    )
    return pl.pallas_call(
        _combine_body,
        grid_spec=grid_spec,
        out_shape=jax.ShapeDtypeStruct((n, d), F32),
        compiler_params=_cparams("arbitrary"),
        name="combine",
    )(row_start, pad_tok, stats, stats, w_tok, x1, h2, gate_rows, ys, wsg, wsu, wsd, gpost)


def _moe(x1, h2, gate_rows, lp):
    n, d = h2.shape
    bm = MOE_BLOCK
    tn = TOKEN_TILE
    e3, pos3, loc3, pad3, stats, w_rows, cnt = _router(h2, lp['wr_t'], lp['b_router_col'])
    counts = cnt[:, 0].astype(I32)
    blocks_per_e = (counts + DISPATCH_CHUNK + bm - 1) // bm
    blk_end = jnp.cumsum(blocks_per_e)
    blk_start = blk_end - blocks_per_e
    n_blocks = -(-(n * TOP_K + N_EXPERTS * DISPATCH_CHUNK) // bm) + N_EXPERTS
    block_e = jnp.minimum(jnp.sum(blk_end[None, :] <= jnp.arange(n_blocks, dtype=I32)[:, None], axis=1),
                          N_EXPERTS - 1).astype(I32)
    n_used = blk_end[-1:].astype(I32)
    partial = jnp.stack([blk_start + counts // bm, blk_end - 1], axis=1).reshape(-1)
    xs_tails = _zero_tails(jnp.clip(partial, 0, n_blocks - 1).astype(I32), n_blocks)
    row_start = (blk_start * bm).astype(I32)
    _, xs = _dispatch(row_start, e3, pos3, loc3, stats, h2, xs_tails)
    ys = _experts(block_e, n_used, xs, lp['w_e_gate'], lp['w_e_up'], lp['w_e_down'])
    pad_tok = pad3.transpose(0, 2, 1).reshape(n, SLOT_ROWS)
    return _combine(row_start, pad_tok, stats, w_rows.T, x1, h2, gate_rows, ys, lp['w_s_gate'], lp['w_s_up'],
                    lp['w_s_down'], lp['g_post_ffn'])


def _rope_tables(n_tokens):
    rows = n_tokens // GRID_W
    row = jnp.repeat(jnp.arange(rows, dtype=F32), GRID_W)
    col = jnp.tile(jnp.arange(GRID_W, dtype=F32), rows)
    n_freq = ROPE_DIM // 4
    inv_freq = ROPE_THETA ** (-jnp.arange(n_freq, dtype=F32) / n_freq)
    ang_r = row[:, None] * inv_freq
    ang_c = col[:, None] * inv_freq
    ang = jnp.concatenate([ang_r, ang_r, ang_c, ang_c], axis=-1)
    cos, sin = jnp.cos(ang), jnp.sin(ang)
    pad = lambda z, fill: jnp.concatenate(
        [jnp.full((n_tokens, NOPE_DIM), fill, F32), z, jnp.zeros((n_tokens, HEAD_PAD - QK_DIM), F32)], axis=1)
    return pad(cos, 1.0), pad(sin, 0.0), pad(cos, 0.0), pad(sin, 0.0)


def _rot_cols(w):
    q = ROPE_DIM // 4
    perm = np.concatenate([np.arange(q, 2 * q), np.arange(0, q), np.arange(3 * q, 4 * q), np.arange(2 * q, 3 * q)])
    sign = np.concatenate([-np.ones(q), np.ones(q), -np.ones(q), np.ones(q)]).astype(np.float32)
    return w[..., perm] * sign


def _block_diag(blocks):
    h, m, n = blocks.shape[-3:]
    eye = jnp.eye(h, dtype=blocks.dtype)
    out = blocks[..., :, :, None, :] * eye[:, None, :, None]
    return out.reshape(blocks.shape[:-3] + (h * m, h * n))


def _diag_blocks(mat, h):
    m, n = mat.shape[-2] // h, mat.shape[-1] // h
    z = mat.reshape(mat.shape[:-2] + (h, m, h, n))
    return jnp.stack([z[..., i, :, i, :] for i in range(h)], axis=-3)


def _layer_operands(l, a):
    w_in = a['w_in'][l]
    d = w_in.shape[0]
    p_mla = Q_LORA + KV_LORA + ROPE_DIM
    w_kr = w_in[:, Q_LORA + KV_LORA:p_mla]
    z32 = jnp.zeros((d, ROPE_DIM), F32)
    w_mla = jnp.concatenate([w_in[:, :Q_LORA + KV_LORA], z32, _rot_cols(w_kr), w_kr, z32], axis=1)
    wq3 = a['w_q_b'][l].reshape(Q_LORA, H_MLA, QK_DIM)
    zq = lambda n: jnp.zeros((Q_LORA, H_MLA, n), F32)
    wq = jnp.concatenate([wq3, zq(HEAD_PAD - QK_DIM)], axis=2).reshape(Q_LORA, H_MLA * HEAD_PAD)
    wqr = jnp.concatenate([zq(NOPE_DIM), _rot_cols(wq3[:, :, NOPE_DIM:]), zq(HEAD_PAD - QK_DIM)],
                          axis=2).reshape(Q_LORA, H_MLA * HEAD_PAD)
    wkv3 = a['w_kv_b'][l].reshape(KV_LORA, H_MLA, NOPE_DIM + V_DIM)
    wk = jnp.concatenate([wkv3[:, :, :NOPE_DIM], jnp.zeros((KV_LORA, H_MLA, HEAD_PAD - NOPE_DIM), F32)],
                         axis=2).reshape(KV_LORA, H_MLA * HEAD_PAD)
    wv = wkv3[:, :, NOPE_DIM:].reshape(KV_LORA, H_MLA * V_DIM)
    row = lambda z: z.reshape(1, -1)
    two = lambda z: jnp.concatenate([z[0], z[1]], axis=-1)
    bd2 = lambda z: jnp.concatenate([jnp.concatenate([z[0], jnp.zeros_like(z[0])], axis=1),
                                     jnp.concatenate([jnp.zeros_like(z[1]), z[1]], axis=1)], axis=0)
    return {
        'g_pre_mix': row(a['g_pre_mix'][l]), 'g_post_mix': row(a['g_post_mix'][l]),
        'g_pre_ffn': row(a['g_pre_ffn'][l]), 'g_post_ffn': row(a['g_post_ffn'][l]),
        'w_mla': w_mla.astype(BF16),
        'w_rwkv': w_in[:, p_mla:p_mla + P_RWKV].astype(BF16),
        'w_lru': w_in[:, p_mla + P_RWKV:].astype(BF16),
        'g_q_a': row(a['g_q_a'][l]), 'g_kv_a': row(a['g_kv_a'][l]),
        'wq': wq.astype(BF16), 'wqr': wqr.astype(BF16), 'wk': wk.astype(BF16), 'wv': wv.astype(BF16),
        'g_mla_out': row(a['g_mla_out'][l]),
        'mu_rwkv': row(a['mu_rwkv'][l]), 'k_k': row(a['k_k'][l]), 'k_a': row(a['k_a'][l]), 'r_k': row(a['r_k'][l]),
        'w0_cat': row(two(a['w0_rwkv'][l])), 'w2_bd': bd2(a['w2_rwkv'][l]),
        'a0_cat': row(two(a['a0_rwkv'][l])), 'a2_bd': bd2(a['a2_rwkv'][l]),
        'g2_rwkv': a['g2_rwkv'][l], 'ln_x_w': row(a['ln_x_w'][l]), 'ln_x_b': row(a['ln_x_b'][l]),
        'conv_w': a['conv_w'][l], 'conv_b': a['conv_b'][l],
        'wa_bd': _block_diag(a['w_lru_a'][l]), 'b_lru_a': a['b_lru_a'][l],
        'wx_bd': _block_diag(a['w_lru_x'][l]), 'b_lru_x': a['b_lru_x'][l],
        'lam': a['lam'][l], 'g_lru_out': row(a['g_lru_out'][l]),
        'w_out': a['w_out'][l].astype(BF16),
        'wr_t': a['w_router'][l].T, 'b_router_col': a['b_router'][l].reshape(N_EXPERTS, 1),
        'w_e_gate': a['w_e_gate'][l].astype(BF16), 'w_e_up': a['w_e_up'][l].astype(BF16),
        'w_e_down': a['w_e_down'][l].astype(BF16),
        'w_s_gate': a['w_s_gate'][l].astype(BF16), 'w_s_up': a['w_s_up'][l].astype(BF16),
        'w_s_down': a['w_s_down'][l].astype(BF16),
    }


def _mixer(x, grp, mod, lp, ctx, tables, prev):
    _, nb, t = grp
    tt = min(t, 512)
    p_mla, p_rwkv, p_lru = _in_proj(x, grp, lp['g_pre_mix'], mod['sc1'], mod['sh1'], lp['w_mla'], lp['w_rwkv'],
                                    lp['w_lru'], tt)
    if ctx is None:
        q, k, v, new_cache = _mla_prep(p_mla, lp['g_q_a'], lp['g_kv_a'], lp['wq'], None, lp['wk'], lp['wv'], None, tt)
        kc = vc = None
        s0 = jnp.zeros((nb, 2, D_RWKV, D_RWKV), F32)
        h0 = jnp.zeros((nb, 2, 1, D_LRU), F32)
    else:
        cache, s_rwkv, s_lru = ctx
        q, k, v = _mla_prep(p_mla, lp['g_q_a'], lp['g_kv_a'], lp['wq'], lp['wqr'], lp['wk'], lp['wv'], tables, tt)
        new_cache = None
        kr_placed = jnp.pad(cache[..., KV_LORA:], ((0, 0), (0, 0), (NOPE_DIM, HEAD_PAD - QK_DIM)))
        kc, vc = _ctx_kv(cache[..., :KV_LORA], kr_placed, lp['wk'], lp['wv'])
        s0 = _block_diag(s_rwkv)
        h0 = s_lru[:, :, None, :]
    o = _attention(q, k, v, kc, vc, min(t, 256))

    r, vv, kk, lw, ag, kd, g, bonus = _rwkv_prep(p_rwkv, lp, tt)
    y_f, y_b, s_fin = _rwkv_scan(r, vv, kk, lw, ag, kd, s0)

    hp, hn = _halo_rows(p_lru[..., :D_LRU], tt, SUBLANES)
    h_f, fin_f = _lru_dir(p_lru, hp, lp, 0, h0[:, 0], tt)
    h_b, fin_b = _lru_dir(p_lru, hn, lp, 1, h0[:, 1], tt)

    x1, h2 = _out_proj(x, grp, o, y_f, y_b, bonus, g, h_f, h_b, p_lru, lp, mod, tt, prev)
    states = None
    if ctx is None:
        states = (new_cache, _diag_blocks(s_fin, H_RWKV), jnp.concatenate([fin_f, fin_b], axis=1))
    return x1, h2, states


def kernel(x_prompt, x_sample, c, cache_mla, state_rwkv, state_lru, c_ctx, w_ada, b_ada, g_pre_mix, g_post_mix, g_pre_ffn, g_post_ffn, w_in, g_q_a, w_q_b, g_kv_a, w_kv_b, g_mla_out, mu_rwkv, w0_rwkv, w2_rwkv, a0_rwkv, a2_rwkv, g2_rwkv, k_k, k_a, r_k, ln_x_w, ln_x_b, conv_w, conv_b, w_lru_a, b_lru_a, w_lru_x, b_lru_x, lam, g_lru_out, w_out, w_router, b_router, w_e_gate, w_e_up, w_e_down, w_s_gate, w_s_up, w_s_down):
    a = dict(w_in=w_in, g_pre_mix=g_pre_mix, g_post_mix=g_post_mix, g_pre_ffn=g_pre_ffn, g_post_ffn=g_post_ffn,
             g_q_a=g_q_a, w_q_b=w_q_b, g_kv_a=g_kv_a, w_kv_b=w_kv_b, g_mla_out=g_mla_out, mu_rwkv=mu_rwkv,
             w0_rwkv=w0_rwkv, w2_rwkv=w2_rwkv, a0_rwkv=a0_rwkv, a2_rwkv=a2_rwkv, g2_rwkv=g2_rwkv, k_k=k_k, k_a=k_a,
             r_k=r_k.reshape(r_k.shape[0], -1), ln_x_w=ln_x_w, ln_x_b=ln_x_b, conv_w=conv_w, conv_b=conv_b,
             w_lru_a=w_lru_a, b_lru_a=b_lru_a, w_lru_x=w_lru_x, b_lru_x=b_lru_x, lam=lam, g_lru_out=g_lru_out,
             w_out=w_out, w_router=w_router, b_router=b_router, w_e_gate=w_e_gate, w_e_up=w_e_up,
             w_e_down=w_e_down, w_s_gate=w_s_gate, w_s_up=w_s_up, w_s_down=w_s_down)
    n_layers = w_in.shape[0]
    nbp, tp, d = x_prompt.shape
    nbs, ts, _ = x_sample.shape
    rows = -(-(1 + nbs) // SUBLANES) * SUBLANES
    cvecs = jnp.concatenate([c_ctx[None, :], c, jnp.zeros((rows - 1 - nbs, d), F32)], axis=0)
    mods = _ada(cvecs, w_ada, b_ada)
    names = ('sh1', 'sc1', 'g1', 'sh2', 'sc2', 'g2')
    tables = _rope_tables(ts)
    tn = TOKEN_TILE

    grp_p, grp_s = (0, nbp, tp), (nbp * tp, nbs, ts)
    x = jnp.concatenate([x_prompt.reshape(-1, d), x_sample.reshape(-1, d)], axis=0)
    caches, rwkv_states, lru_states = [], [], []
    for l in range(n_layers):
        lp = _layer_operands(l, a)
        mod_p = {nm: mods[l, 0:1, i * d:(i + 1) * d][:, None, :] for i, nm in enumerate(names)}
        mod_s = {nm: mods[l, 1:1 + nbs, i * d:(i + 1) * d][:, None, :] for i, nm in enumerate(names)}
        x1, h2, states = _mixer(x, grp_p, mod_p, lp, None, None, None)
        x1, h2, _ = _mixer(x, grp_s, mod_s, lp, (cache_mla[:, l], state_rwkv[:, l], state_lru[:, l]), tables,
                           (x1, h2))
        caches.append(states[0])
        rwkv_states.append(states[1])
        lru_states.append(states[2])
        gate_rows = jnp.concatenate([jnp.repeat(mod_p['g2'], nbp * tp // tn, axis=0),
                                     jnp.repeat(mod_s['g2'], ts // tn, axis=0)], axis=0)
        x = _moe(x1, h2, gate_rows, lp)
    yp = x[:nbp * tp].reshape(nbp, tp, d)
    ys = x[nbp * tp:].reshape(nbs, ts, d)
    return (yp, ys, jnp.stack(caches, axis=1), jnp.stack(rwkv_states, axis=1), jnp.stack(lru_states, axis=1))
```

```python
import functools

import numpy as np
import jax
import jax.numpy as jnp
from jax import lax
from jax.experimental import pallas as pl
from jax.experimental.pallas import tpu as pltpu

F32 = jnp.float32
BF16 = jnp.bfloat16
I32 = jnp.int32

D_MODEL = 1024
GRID_W = 64
EPS = 1e-6
H_MLA = 4
Q_LORA = 256
KV_LORA = 128
NOPE_DIM = 64
ROPE_DIM = 32
V_DIM = 128
QK_DIM = NOPE_DIM + ROPE_DIM
ROPE_THETA = 10000.0
HEAD_PAD = 128
H_RWKV = 4
N_RWKV = 64
D_RWKV = H_RWKV * N_RWKV
W_LORA = 64
A_LORA = 64
G_LORA = 128
GN_EPS = 64e-5
D_LRU = 256
H_LRU = 4
BS_LRU = D_LRU // H_LRU
CONV_W = 4
LRU_C = 8.0
D_MLA_OUT = H_MLA * V_DIM
P_MLA_PAD = Q_LORA + KV_LORA + HEAD_PAD
P_RWKV = 3 * D_RWKV + 2 * W_LORA + 2 * A_LORA + G_LORA
P_LRU = 2 * D_LRU
N_EXPERTS = 64
TOP_K = 6
N_GROUPS = 8
TOPK_GROUPS = 4
GROUP_SIZE = N_EXPERTS // N_GROUPS
D_EXPERT = 256
ROUTED_SCALE = 2.5

SUBLANES = 8
RWKV_CHUNK = 64
RWKV_ROWS = 2
MOE_BLOCK = 512
ROUTER_TILE = 256
DISPATCH_SHIFT = 5
DISPATCH_CHUNK = 1 << DISPATCH_SHIFT
TOKEN_TILE = 256
COMBINE_ROWS = TOP_K * TOKEN_TILE + N_EXPERTS * DISPATCH_CHUNK
COMBINE_PIECES = 7
SLOT_ROWS = 8
VMEM_LIMIT = 56 * 1024 * 1024


def _cparams(*sem):
    return pltpu.CompilerParams(dimension_semantics=sem, vmem_limit_bytes=VMEM_LIMIT)


def _sigmoid(x):
    return 1.0 / (1.0 + jnp.exp(-x))


def _softplus(x):
    return jnp.maximum(x, 0.0) + jnp.log1p(jnp.exp(-jnp.abs(x)))


def _expm1(z):
    e = jnp.exp(z)
    direct = (e == 1.0) | (z < -1.0)
    corrected = (e - 1.0) * z / jnp.log(jnp.where(direct, 2.0, e))
    return jnp.where(e == 1.0, z, jnp.where(z < -1.0, e - 1.0, corrected))


def _rms(x, g):
    return x * lax.rsqrt(jnp.mean(x * x, axis=-1, keepdims=True) + EPS) * g


def _split2(x):
    hi = x.astype(BF16)
    lo = (x - hi.astype(F32)).astype(BF16)
    return hi, lo


def _split3(x):
    hi = x.astype(BF16)
    r1 = x - hi.astype(F32)
    mid = r1.astype(BF16)
    lo = (r1 - mid.astype(F32)).astype(BF16)
    return hi, mid, lo


_NN = (((1,), (0,)), ((), ()))
_NT = (((1,), (1,)), ((), ()))


def _dg(a, b, dims):
    return lax.dot_general(a, b, dims, preferred_element_type=F32)


def _dot1(a, b, dims=_NN):
    return _dg(a.astype(BF16), b.astype(BF16), dims)


def _dot3(a, b, dims=_NN):
    ah, al = _split2(a)
    bh, bl = _split2(b)
    return _dg(ah, bh, dims) + (_dg(ah, bl, dims) + _dg(al, bh, dims))


def _dot_x(a, b_exact, dims=_NN):
    h, m, l = _split3(a)
    bb = b_exact.astype(BF16)
    return _dg(h, bb, dims) + (_dg(m, bb, dims) + _dg(l, bb, dims))


def _x_dot(a_exact, b, dims=_NN):
    h, m, l = _split3(b)
    aa = a_exact.astype(BF16)
    return _dg(aa, h, dims) + (_dg(aa, m, dims) + _dg(aa, l, dims))


def _head_ones(n, seg):
    r = lax.broadcasted_iota(I32, (n, n), 0) // seg
    c = lax.broadcasted_iota(I32, (n, n), 1) // seg
    return (r == c).astype(F32)


def _ada_body(c_ref, w_ref, b_ref, o_ref):
    cv = c_ref[...]
    s = cv * _sigmoid(cv)
    o_ref[0] = _dot3(s, w_ref[0]) + b_ref[0]


def _ada(cvecs, w_ada, b_ada):
    n_layers, d, d6 = w_ada.shape
    rows = cvecs.shape[0]
    nt = d6 // d
    return pl.pallas_call(
        _ada_body,
        grid=(n_layers, nt),
        in_specs=[
            pl.BlockSpec((rows, d), lambda l, j: (0, 0)),
            pl.BlockSpec((1, d, d), lambda l, j: (l, 0, j)),
            pl.BlockSpec((1, 1, d), lambda l, j: (l, 0, j)),
        ],
        out_specs=pl.BlockSpec((1, rows, d), lambda l, j: (l, 0, j)),
        out_shape=jax.ShapeDtypeStruct((n_layers, rows, d6), F32),
        compiler_params=_cparams("parallel", "parallel"),
        name="ada",
    )(cvecs, w_ada, b_ada.reshape(n_layers, 1, d6))


def _mod_spec(per_batch):
    if per_batch:
        return pl.BlockSpec((1, 1, D_MODEL), lambda b, i: (b, 0, 0))
    return pl.BlockSpec((1, 1, D_MODEL), lambda b, i: (0, 0, 0))


def _full(shape):
    nd = len(shape)
    return pl.BlockSpec(shape, lambda *_: (0,) * nd)


def _in_proj_body(x_ref, g_ref, sc_ref, sh_ref, w1_ref, w2_ref, w3_ref, o1_ref, o2_ref, o3_ref):
    h = _rms(x_ref[...], g_ref[...]) * (1.0 + sc_ref[0]) + sh_ref[0]
    hb = h.astype(BF16)
    o1_ref[0] = _dg(hb, w1_ref[...], _NN)
    o2_ref[0] = _dg(hb, w2_ref[...], _NN)
    o3_ref[0] = _dg(hb, w3_ref[...], _NN)


def _group_rows(grp, tt):
    row_off, nb, t = grp
    base, per = row_off // tt, t // tt
    return pl.BlockSpec((tt, D_MODEL), lambda b, i: (base + b * per + i, 0))


def _in_proj(x, grp, g, sc, sh, w_mla, w_rwkv, w_lru, tt):
    _, nb, t = grp
    d = x.shape[1]
    per_batch = sc.shape[0] > 1
    outs = [w_mla.shape[1], w_rwkv.shape[1], w_lru.shape[1]]
    return pl.pallas_call(
        _in_proj_body,
        grid=(nb, t // tt),
        in_specs=[
            _group_rows(grp, tt),
            _full((1, d)),
            _mod_spec(per_batch),
            _mod_spec(per_batch),
            _full(w_mla.shape),
            _full(w_rwkv.shape),
            _full(w_lru.shape),
        ],
        out_specs=[pl.BlockSpec((1, tt, n), lambda b, i: (b, i, 0)) for n in outs],
        out_shape=[jax.ShapeDtypeStruct((nb, t, n), F32) for n in outs],
        compiler_params=_cparams("parallel", "parallel"),
        name="in_proj",
    )(x, g, sc, sh, w_mla, w_rwkv, w_lru)


def _rope_lanes():
    lane = lax.broadcasted_iota(I32, (1, HEAD_PAD), 1)
    return ((lane >= NOPE_DIM) & (lane < QK_DIM)).astype(F32)


def _mla_prep_body(rope, *refs):
    if rope:
        (p_ref, gq_ref, gkv_ref, wq_ref, wqr_ref, wk_ref, wv_ref, cq_ref, sq_ref, cr_ref, sr_ref,
         q_ref, k_ref, v_ref) = refs
    else:
        p_ref, gq_ref, gkv_ref, wq_ref, wk_ref, wv_ref, q_ref, k_ref, v_ref, c_ref = refs
    p = p_ref[0]
    cq = _rms(p[:, :Q_LORA], gq_ref[...])
    ckv = _rms(p[:, Q_LORA:Q_LORA + KV_LORA], gkv_ref[...])
    kr = p[:, Q_LORA + KV_LORA:]
    cqb = cq.astype(BF16)
    ckvb = ckv.astype(BF16)
    scale = QK_DIM ** -0.5 * float(np.log2(np.e))
    qa = _dg(cqb, wq_ref[...], _NN) * scale
    ka = _dg(ckvb, wk_ref[...], _NN)
    va = _dg(ckvb, wv_ref[...], _NN)
    if rope:
        qr = _dg(cqb, wqr_ref[...], _NN) * scale
        krp = kr * cr_ref[...] + pltpu.roll(kr, ROPE_DIM, 1) * sr_ref[...]
    else:
        krp = kr * _rope_lanes()
    for h in range(H_MLA):
        sl = slice(h * HEAD_PAD, (h + 1) * HEAD_PAD)
        qh = qa[:, sl]
        if rope:
            qh = qh * cq_ref[...] + qr[:, sl] * sq_ref[...]
        q_ref[0, h] = qh.astype(BF16)
        k_ref[0, h] = (ka[:, sl] + krp).astype(BF16)
        v_ref[0, h] = va[:, sl].astype(BF16)
    if not rope:
        c_ref[0, :, :KV_LORA] = ckv
        c_ref[0, :, KV_LORA:] = pltpu.roll(kr, HEAD_PAD - NOPE_DIM, 1)[:, :ROPE_DIM]


def _mla_prep(p_mla, gq, gkv, wq, wqr, wk, wv, tables, tt):
    nb, t, pw = p_mla.shape
    rope = tables is not None
    hw = H_MLA * HEAD_PAD
    head_spec = pl.BlockSpec((1, H_MLA, tt, HEAD_PAD), lambda b, i: (b, 0, i, 0))
    head_shape = jax.ShapeDtypeStruct((nb, H_MLA, t, HEAD_PAD), BF16)
    in_specs = [pl.BlockSpec((1, tt, pw), lambda b, i: (b, i, 0)), _full(gq.shape), _full(gkv.shape), _full(wq.shape)]
    args = [p_mla, gq, gkv, wq]
    if rope:
        in_specs.append(_full(wqr.shape))
        args.append(wqr)
    in_specs += [_full(wk.shape), _full(wv.shape)]
    args += [wk, wv]
    out_specs = [head_spec, head_spec, head_spec]
    out_shape = [head_shape, head_shape, head_shape]
    if rope:
        in_specs += [pl.BlockSpec((tt, HEAD_PAD), lambda b, i: (i, 0))] * 4
        args += list(tables)
    else:
        out_specs.append(pl.BlockSpec((1, tt, KV_LORA + ROPE_DIM), lambda b, i: (b, i, 0)))
        out_shape.append(jax.ShapeDtypeStruct((nb, t, KV_LORA + ROPE_DIM), F32))
    del hw
    return pl.pallas_call(
        functools.partial(_mla_prep_body, rope),
        grid=(nb, t // tt),
        in_specs=in_specs,
        out_specs=out_specs,
        out_shape=out_shape,
        compiler_params=_cparams("parallel", "parallel"),
        name="mla_prep_rope" if rope else "mla_prep",
    )(*args)


def _ctx_kv_body(lat_ref, kr_ref, wk_ref, wv_ref, k_ref, v_ref):
    latb = lat_ref[0].astype(BF16)
    ka = _dg(latb, wk_ref[...], _NN)
    va = _dg(latb, wv_ref[...], _NN)
    kr = kr_ref[0]
    for h in range(H_MLA):
        sl = slice(h * HEAD_PAD, (h + 1) * HEAD_PAD)
        k_ref[0, h] = (ka[:, sl] + kr).astype(BF16)
        v_ref[0, h] = va[:, sl].astype(BF16)


def _ctx_kv(lat, kr_placed, wk, wv):
    nb, s, _ = lat.shape
    head_spec = pl.BlockSpec((1, H_MLA, s, HEAD_PAD), lambda b: (b, 0, 0, 0))
    head_shape = jax.ShapeDtypeStruct((nb, H_MLA, s, HEAD_PAD), BF16)
    return pl.pallas_call(
        _ctx_kv_body,
        grid=(nb,),
        in_specs=[
            pl.BlockSpec((1, s, KV_LORA), lambda b: (b, 0, 0)),
            pl.BlockSpec((1, s, HEAD_PAD), lambda b: (b, 0, 0)),
            _full(wk.shape),
            _full(wv.shape),
        ],
        out_specs=[head_spec, head_spec],
        out_shape=[head_shape, head_shape],
        compiler_params=_cparams("parallel"),
        name="ctx_kv",
    )(lat, kr_placed, wk, wv)


def _attn_body(has_ctx, *refs):
    if has_ctx:
        q_ref, k_ref, v_ref, kc_ref, vc_ref, o_ref = refs
    else:
        q_ref, k_ref, v_ref, o_ref = refs
    q = q_ref[0, 0]
    s = _dg(q, k_ref[0, 0], _NT)
    m = jnp.max(s, axis=-1, keepdims=True)
    if has_ctx:
        sc = _dg(q, kc_ref[0, 0], _NT)
        m = jnp.maximum(m, jnp.max(sc, axis=-1, keepdims=True))
    p = jnp.exp2(s - m)
    l = jnp.sum(p, axis=-1, keepdims=True)
    o = _dg(p.astype(BF16), v_ref[0, 0], _NN)
    if has_ctx:
        pc = jnp.exp2(sc - m)
        l = l + jnp.sum(pc, axis=-1, keepdims=True)
        o = o + _dg(pc.astype(BF16), vc_ref[0, 0], _NN)
    o_ref[0] = o / l


def _attention(q, k, v, kc, vc, tq):
    nb, nh, t, hd = q.shape
    s = k.shape[2]
    has_ctx = kc is not None
    in_specs = [
        pl.BlockSpec((1, 1, tq, hd), lambda b, h, i: (b, h, i, 0)),
        pl.BlockSpec((1, 1, s, hd), lambda b, h, i: (b, h, 0, 0)),
        pl.BlockSpec((1, 1, s, V_DIM), lambda b, h, i: (b, h, 0, 0)),
    ]
    args = [q, k, v]
    if has_ctx:
        sc = kc.shape[2]
        in_specs += [
            pl.BlockSpec((1, 1, sc, hd), lambda b, h, i: (b, h, 0, 0)),
            pl.BlockSpec((1, 1, sc, V_DIM), lambda b, h, i: (b, h, 0, 0)),
        ]
        args += [kc, vc]
    return pl.pallas_call(
        functools.partial(_attn_body, has_ctx),
        grid=(nb, nh, t // tq),
        in_specs=in_specs,
        out_specs=pl.BlockSpec((1, tq, V_DIM), lambda b, h, i: (b, i, h)),
        out_shape=jax.ShapeDtypeStruct((nb, t, nh * V_DIM), F32),
        compiler_params=_cparams("parallel", "parallel", "parallel"),
        name="attention",
    )(*args)


def _rwkv_prep_body(p_ref, hp_ref, hn_ref, mu_ref, kk_ref, ka_ref, rk_ref, w0_ref, w2_ref, a0_ref, a2_ref, g2_ref,
                    r_out, v_out, kk_out, lw_out, a_out, kd_out, g_out, bonus_out):
    p = p_ref[0]
    tt = p.shape[0]
    row = lax.broadcasted_iota(I32, (tt, 1), 0)
    prev = jnp.where(row == 0, hp_ref[0, 0], pltpu.roll(p, 1, 0))
    nxt = jnp.where(row == tt - 1, hn_ref[0, 0], pltpu.roll(p, tt - 1, 0))
    ps = p + (0.5 * (prev + nxt) - p) * mu_ref[...]
    r = ps[:, :D_RWKV]
    k = ps[:, D_RWKV:2 * D_RWKV]
    v = ps[:, 2 * D_RWKV:3 * D_RWKV]
    wl = ps[:, 3 * D_RWKV:3 * D_RWKV + 2 * W_LORA]
    al = ps[:, 3 * D_RWKV + 2 * W_LORA:3 * D_RWKV + 2 * W_LORA + 2 * A_LORA]
    gl = ps[:, 3 * D_RWKV + 2 * W_LORA + 2 * A_LORA:]
    ones = _head_ones(D_RWKV, N_RWKV)
    kk = k * kk_ref[...]
    kk = kk / jnp.maximum(jnp.sqrt(_dot_x(kk * kk, ones)), 1e-12)
    wlin = w0_ref[...] + _dot3(jnp.tanh(wl), w2_ref[...])
    lw = -jnp.exp(-_softplus(-wlin) - 0.5)
    ag = _sigmoid(a0_ref[...] + _dot3(al, a2_ref[...]))
    bonus = jnp.zeros_like(v)
    for d in range(2):
        sl = slice(d * D_RWKV, (d + 1) * D_RWKV)
        kd = k * (1.0 + (ag[:, sl] - 1.0) * ka_ref[...])
        kd_out[0, :, sl] = kd
        bonus = bonus + _dot_x(r * kd * rk_ref[...], ones) * v
    r_out[0] = r
    v_out[0] = v
    kk_out[0] = kk
    lw_out[0] = lw
    a_out[0] = ag
    g_out[0] = _dot1(_sigmoid(gl), g2_ref[...])
    bonus_out[0] = bonus


def _halo_rows(x, tt, width):
    nb, t, c = x.shape
    nt = t // tt
    xt = x.reshape(nb, nt, tt, c)
    zero = jnp.zeros((nb, 1, width, c), x.dtype)
    prev = jnp.concatenate([zero, xt[:, :-1, tt - width:]], axis=1)
    nxt = jnp.concatenate([xt[:, 1:, :width], zero], axis=1)
    return prev, nxt


def _rwkv_prep(p_rwkv, lp, tt):
    nb, t, pw = p_rwkv.shape
    hp, hn = _halo_rows(p_rwkv, tt, 1)
    tok = lambda n: pl.BlockSpec((1, tt, n), lambda b, i: (b, i, 0))
    halo = pl.BlockSpec((1, 1, 1, pw), lambda b, i: (b, i, 0, 0))
    small = [lp['mu_rwkv'], lp['k_k'], lp['k_a'], lp['r_k'], lp['w0_cat'], lp['w2_bd'], lp['a0_cat'], lp['a2_bd'],
             lp['g2_rwkv']]
    widths = [D_RWKV, D_RWKV, D_RWKV, 2 * D_RWKV, 2 * D_RWKV, 2 * D_RWKV, D_RWKV, D_RWKV]
    return pl.pallas_call(
        _rwkv_prep_body,
        grid=(nb, t // tt),
        in_specs=[tok(pw), halo, halo] + [_full(a.shape) for a in small],
        out_specs=[tok(n) for n in widths],
        out_shape=[jax.ShapeDtypeStruct((nb, t, n), F32) for n in widths],
        compiler_params=_cparams("parallel", "parallel"),
        name="rwkv_prep",
    )(p_rwkv, hp, hn, *small)


def _rwkv_chunk(reverse, r, v, kk, lw, ag, kd, s_old):
    cl = r.shape[0]
    hw = D_RWKV

    def later(rows, cols, strict):
        t = lax.broadcasted_iota(I32, (rows, cols), 0)
        s = lax.broadcasted_iota(I32, (rows, cols), 1) % cl
        dlt = (s - t) if reverse else (t - s)
        return (dlt > 0) if strict else (dlt >= 0)

    cum = _x_dot(later(cl, cl, False).astype(F32), lw)
    yield
    tot = jnp.sum(lw, axis=0, keepdims=True)
    e_neg = jnp.exp(-cum)
    e_rem = jnp.exp(tot - cum)
    at = -kk * jnp.exp(cum - lw)
    rt = r * jnp.exp(cum)
    b0 = kk * ag
    bt = b0 * e_neg
    kt = kd * e_neg
    bh = b0 * e_rem
    kh = kd * e_rem
    w_c = jnp.exp(tot)

    lane_head = lax.broadcasted_iota(I32, (1, hw), 1) // N_RWKV
    hms = [(lane_head == h).astype(F32) for h in range(H_RWKV)]
    heads = range(H_RWKV)
    bk = [jnp.concatenate([bt * hm, kt * hm], axis=0) for hm in hms]
    strict = later(cl, 2 * cl, True)
    incl = later(cl, 2 * cl, False)
    ls = [jnp.where(strict, _dot3(at, bk[h], _NT), 0.0) for h in heads]
    ly = [jnp.where(incl, _dot1(rt, bk[h], _NT), 0.0) for h in heads]
    yield

    x = [l[:, :cl] for l in ls]
    eye = (lax.broadcasted_iota(I32, (cl, cl), 0) == lax.broadcasted_iota(I32, (cl, cl), 1)).astype(F32)
    inv = [eye + xh for xh in x]
    span = 2
    while span < cl:
        x = [_dot3(xh, xh) for xh in x]
        yield
        inv = [ih + _dot3(ih, xh) for ih, xh in zip(inv, x)]
        yield
        span *= 2

    zero = jnp.zeros((cl, hw), F32)
    v_rows = jnp.concatenate([piece for hm in hms for piece in (zero, v * hm)], axis=0)
    rhs = _dot3(at, s_old, _NT) + _dot3(jnp.concatenate(ls, axis=1), v_rows)
    yield
    us = [_dot3(inv[h], rhs * hms[h]) for h in heads]
    yield
    um = us[0]
    for h in range(1, H_RWKV):
        um = um + us[h]
    uv_rows = jnp.concatenate([piece for h in heads for piece in (us[h], v * hms[h])], axis=0)
    y = _dot1(rt, s_old, _NT) + _dot1(jnp.concatenate(ly, axis=1), uv_rows)
    yield
    uv_t = jnp.concatenate([um, v], axis=0).T
    s_new = s_old * w_c + _dot3(uv_t, jnp.concatenate([bh, kh], axis=0)) * _head_ones(hw, N_RWKV)
    yield y, s_new


def _lockstep(gens):
    last = [None] * len(gens)
    live = list(range(len(gens)))
    while live:
        for i in list(live):
            try:
                last[i] = next(gens[i])
            except StopIteration:
                live.remove(i)
    return last


def _rwkv_scan_body(rf_ref, vf_ref, kkf_ref, lwf_ref, af_ref, kdf_ref, rb_ref, vb_ref, kkb_ref, lwb_ref, ab_ref,
                    kdb_ref, s0_ref, yf_ref, yb_ref, sf_ref, s_scr):
    c = pl.program_id(1)
    nc = pl.num_programs(1)

    @pl.when(c == 0)
    def _():
        s_scr[...] = s0_ref[...]

    nrow = rf_ref.shape[0]
    chains = []
    for i in range(nrow):
        chains.append(_rwkv_chunk(False, rf_ref[i], vf_ref[i], kkf_ref[i], lwf_ref[i], af_ref[i], kdf_ref[i],
                                  s_scr[i, 0]))
        chains.append(_rwkv_chunk(True, rb_ref[i], vb_ref[i], kkb_ref[i], lwb_ref[i], ab_ref[i], kdb_ref[i],
                                  s_scr[i, 1]))
    done = _lockstep(chains)
    for i in range(nrow):
        (y_f, s_f), (y_b, s_b) = done[2 * i], done[2 * i + 1]
        yf_ref[i] = y_f
        yb_ref[i] = y_b
        s_scr[i, 0] = s_f
        s_scr[i, 1] = s_b

    @pl.when(c == nc - 1)
    def _():
        sf_ref[...] = s_scr[...]


def _rwkv_scan(r, v, kk, lw, ag, kd, s0):
    nb, t, hw = r.shape
    cl = RWKV_CHUNK
    nr = RWKV_ROWS
    nc = t // cl
    fwd = lambda lane: pl.BlockSpec((nr, cl, hw), lambda b, c: (b, c, lane))
    bwd = lambda lane: pl.BlockSpec((nr, cl, hw), lambda b, c: (b, nc - 1 - c, lane))
    state = pl.BlockSpec((nr, 2, hw, hw), lambda b, c: (b, 0, 0, 0))
    y_shape = jax.ShapeDtypeStruct((nb, t, hw), F32)
    return pl.pallas_call(
        _rwkv_scan_body,
        grid=(nb // nr, nc),
        in_specs=[fwd(0)] * 6 + [bwd(0)] * 3 + [bwd(1)] * 3 + [state],
        out_specs=[fwd(0), bwd(0), state],
        out_shape=[y_shape, y_shape, jax.ShapeDtypeStruct((nb, 2, hw, hw), F32)],
        scratch_shapes=[pltpu.VMEM((nr, 2, hw, hw), F32)],
        compiler_params=_cparams("parallel", "arbitrary"),
        name="rwkv_scan",
    )(r, v, kk, lw, ag, kd, r, v, kk, lw, ag, kd, s0)


def _lru_body(reverse, p_ref, halo_ref, cw_ref, cb_ref, wa_ref, ba_ref, wx_ref, bx_ref, lam_ref, h0_ref,
              h_out, fin_out, a_scr, u_scr, carry_scr):
    i = pl.program_id(1)
    nt = pl.num_programs(1)
    tt = p_ref.shape[1]
    n_ext = tt + SUBLANES

    @pl.when(i == 0)
    def _():
        carry_scr[...] = jnp.broadcast_to(h0_ref[0], (SUBLANES, D_LRU))

    xb = p_ref[0, :, :D_LRU]
    halo = halo_ref[0, 0]
    xc = cb_ref[...] + jnp.zeros_like(xb)
    if reverse:
        xe = jnp.concatenate([xb, halo], axis=0)
        for k in range(CONV_W):
            sh = xb if k == 0 else pltpu.roll(xe, n_ext - k, 0)[:tt]
            xc = xc + cw_ref[k:k + 1, :] * sh
    else:
        xe = jnp.concatenate([halo, xb], axis=0)
        for k in range(CONV_W):
            s = CONV_W - 1 - k
            sh = xb if s == 0 else pltpu.roll(xe, s, 0)[SUBLANES:]
            xc = xc + cw_ref[k:k + 1, :] * sh
    gate_r = _sigmoid(_dot3(xc, wa_ref[...]) + ba_ref[...])
    gate_i = _sigmoid(_dot3(xc, wx_ref[...]) + bx_ref[...])
    log_a = -LRU_C * gate_r * _softplus(-lam_ref[...])
    a = jnp.exp(log_a)
    u = jnp.sqrt(-_expm1(2.0 * log_a)) * (gate_i * xc)

    rowm = lax.broadcasted_iota(I32, (tt, 1), 0) % SUBLANES
    for s in (1, 2, 4):
        if reverse:
            a_s = pltpu.roll(a, tt - s, 0)
            u_s = pltpu.roll(u, tt - s, 0)
            m = rowm < SUBLANES - s
        else:
            a_s = pltpu.roll(a, s, 0)
            u_s = pltpu.roll(u, s, 0)
            m = rowm >= s
        u = jnp.where(m, a * u_s + u, u)
        a = jnp.where(m, a * a_s, a)
    a_scr[...] = a
    u_scr[...] = u
    ng = tt // SUBLANES

    def group(g, carry):
        gi = (ng - 1 - g) if reverse else g
        r0 = pl.multiple_of(gi * SUBLANES, SUBLANES)
        hg = a_scr[pl.ds(r0, SUBLANES), :] * carry + u_scr[pl.ds(r0, SUBLANES), :]
        h_out[0, pl.ds(r0, SUBLANES), :] = hg
        edge = hg[0:1, :] if reverse else hg[SUBLANES - 1:SUBLANES, :]
        return jnp.broadcast_to(edge, (SUBLANES, D_LRU))

    carry = lax.fori_loop(0, ng, group, carry_scr[...])
    carry_scr[...] = carry

    @pl.when(i == nt - 1)
    def _():
        fin_out[0] = carry[0:1, :]


def _lru_dir(p_lru, halo, lp, d, h0, tt):
    nb, t, pw = p_lru.shape
    nt = t // tt
    reverse = d == 1
    tidx = (lambda i: nt - 1 - i) if reverse else (lambda i: i)
    small = [lp['conv_w'][d], lp['conv_b'][d:d + 1], lp['wa_bd'][d], lp['b_lru_a'][d:d + 1], lp['wx_bd'][d],
             lp['b_lru_x'][d:d + 1], lp['lam'][d:d + 1]]
    return pl.pallas_call(
        functools.partial(_lru_body, reverse),
        grid=(nb, nt),
        in_specs=[
            pl.BlockSpec((1, tt, pw), lambda b, i: (b, tidx(i), 0)),
            pl.BlockSpec((1, 1, SUBLANES, D_LRU), lambda b, i: (b, tidx(i), 0, 0)),
        ] + [_full(a.shape) for a in small] + [pl.BlockSpec((1, 1, D_LRU), lambda b, i: (b, 0, 0))],
        out_specs=[
            pl.BlockSpec((1, tt, D_LRU), lambda b, i: (b, tidx(i), 0)),
            pl.BlockSpec((1, 1, D_LRU), lambda b, i: (b, 0, 0)),
        ],
        out_shape=[jax.ShapeDtypeStruct((nb, t, D_LRU), F32), jax.ShapeDtypeStruct((nb, 1, D_LRU), F32)],
        scratch_shapes=[pltpu.VMEM((tt, D_LRU), F32), pltpu.VMEM((tt, D_LRU), F32), pltpu.VMEM((SUBLANES, D_LRU), F32)],
        compiler_params=_cparams("parallel", "arbitrary"),
        name="lru_bwd" if reverse else "lru_fwd",
    )(p_lru, halo, *small, h0)


def _gelu_tanh(x):
    return 0.5 * x * (1.0 + jnp.tanh(np.sqrt(2.0 / np.pi).astype(np.float32) * (x + 0.044715 * (x * x * x))))


def _out_proj_body(x_ref, o_ref, y0_ref, y1_ref, bonus_ref, g_ref, h0_ref, h1_ref, p_ref,
                   gmla_ref, lnw_ref, lnb_ref, glru_ref, wo_ref, gpost_ref, gate_ref, gpre_ref, sc_ref, sh_ref,
                   *rest):
    x1_out, h2_out = rest[-2:]
    o_mla = _rms(o_ref[0], gmla_ref[...])
    ones = _head_ones(D_RWKV, N_RWKV)
    y = y0_ref[0] + y1_ref[0]
    mu = _dot_x(y, ones) * (1.0 / N_RWKV)
    yc = y - mu
    var = _dot_x(yc * yc, ones) * (1.0 / N_RWKV)
    y = yc * lax.rsqrt(var + GN_EPS) * lnw_ref[...] + lnb_ref[...]
    o_rwkv = (y + bonus_ref[0]) * g_ref[0]
    o_lru = _rms((h0_ref[0] + h1_ref[0]) * _gelu_tanh(p_ref[0, :, D_LRU:]), glru_ref[...])
    m = (_dg(o_mla.astype(BF16), wo_ref[:D_MLA_OUT, :], _NN)
         + _dg(o_rwkv.astype(BF16), wo_ref[D_MLA_OUT:D_MLA_OUT + D_RWKV, :], _NN)
         + _dg(o_lru.astype(BF16), wo_ref[D_MLA_OUT + D_RWKV:, :], _NN))
    x1 = x_ref[...] + gate_ref[0] * _rms(m, gpost_ref[...])
    x1_out[...] = x1
    h2_out[...] = _rms(x1, gpre_ref[...]) * (1.0 + sc_ref[0]) + sh_ref[0]


def _out_proj(x, grp, o, y_f, y_b, bonus, g, h0, h1, p_lru, lp, mod, tt, prev):
    _, nb, t = grp
    per_batch = mod['g1'].shape[0] > 1
    tok = lambda n: pl.BlockSpec((1, tt, n), lambda b, i: (b, i, 0))
    rows = lambda: _group_rows(grp, tt)
    small = [lp['g_mla_out'], lp['ln_x_w'], lp['ln_x_b'], lp['g_lru_out'], lp['w_out'], lp['g_post_mix']]
    args = [x, o, y_f, y_b, bonus, g, h0, h1, p_lru, *small, mod['g1'], lp['g_pre_ffn'], mod['sc2'], mod['sh2']]
    in_specs = ([rows(), tok(D_MLA_OUT), tok(D_RWKV), tok(D_RWKV), tok(D_RWKV), tok(D_RWKV), tok(D_LRU), tok(D_LRU),
                 tok(P_LRU)] + [_full(a.shape) for a in small]
                + [_mod_spec(per_batch), _full(lp['g_pre_ffn'].shape), _mod_spec(per_batch), _mod_spec(per_batch)])
    aliases = {}
    if prev is not None:
        aliases = {len(args): 0, len(args) + 1: 1}
        args += list(prev)
        in_specs += [pl.BlockSpec(memory_space=pl.ANY)] * 2
    return pl.pallas_call(
        _out_proj_body,
        grid=(nb, t // tt),
        in_specs=in_specs,
        out_specs=[rows(), rows()],
        out_shape=[jax.ShapeDtypeStruct(x.shape, F32)] * 2,
        input_output_aliases=aliases,
        compiler_params=_cparams("parallel", "parallel"),
        name="out_proj",
    )(*args)


def _first_index(mask, idx, sentinel):
    return jnp.min(jnp.where(mask, idx, sentinel), axis=0, keepdims=True)


def _router_body(h_ref, wr_ref, br_ref, e_out, pos_out, loc_out, pad_out, stat_out, w_out, cnt_out, carry):
    i = pl.program_id(0)
    tn = h_ref.shape[0]

    @pl.when(i == 0)
    def _():
        carry[...] = jnp.zeros_like(carry)

    logits = lax.dot_general(wr_ref[...], h_ref[...], _NT, precision=lax.Precision.HIGHEST,
                             preferred_element_type=F32)
    scores = _sigmoid(logits)
    sel = scores + br_ref[...]

    i8 = lax.broadcasted_iota(I32, (GROUP_SIZE, tn), 0)
    blocks, gscore = [], []
    for g in range(N_GROUPS):
        blk = sel[g * GROUP_SIZE:(g + 1) * GROUP_SIZE, :]
        m1 = jnp.max(blk, axis=0, keepdims=True)
        f1 = _first_index(blk == m1, i8, GROUP_SIZE)
        m2 = jnp.max(jnp.where(i8 == f1, -jnp.inf, blk), axis=0, keepdims=True)
        blocks.append(blk)
        gscore.append(m1 + m2)
    masked = []
    for g in range(N_GROUPS):
        beaten = jnp.zeros((1, tn), I32)
        for o in range(N_GROUPS):
            if o == g:
                continue
            wins = (gscore[o] >= gscore[g]) if o < g else (gscore[o] > gscore[g])
            beaten = beaten + wins.astype(I32)
        masked.append(jnp.where(beaten < TOPK_GROUPS, blocks[g], -jnp.inf))
    cur = jnp.concatenate(masked, axis=0)

    ie = lax.broadcasted_iota(I32, (N_EXPERTS, tn), 0)
    firsts, raw = [], []
    chosen = jnp.zeros((N_EXPERTS, tn), F32)
    for _ in range(TOP_K):
        mx = jnp.max(cur, axis=0, keepdims=True)
        first = _first_index(cur == mx, ie, N_EXPERTS)
        hit = ie == first
        firsts.append(first)
        raw.append(jnp.sum(jnp.where(hit, scores, 0.0), axis=0, keepdims=True))
        cur = jnp.where(hit, -jnp.inf, cur)
        chosen = chosen + hit.astype(F32)
    total = raw[0]
    for k in range(1, TOP_K):
        total = total + raw[k]

    before = (lax.broadcasted_iota(I32, (tn, tn), 0) < lax.broadcasted_iota(I32, (tn, tn), 1)).astype(BF16)
    rank_tile = _dg(chosen.astype(BF16), before, _NN)
    old_carry = carry[...]
    rank = rank_tile + old_carry[:, 0:1]
    tile_cnt = jnp.broadcast_to(jnp.sum(chosen, axis=1, keepdims=True), old_carry.shape)
    lower = (lax.broadcasted_iota(I32, (N_EXPERTS, N_EXPERTS), 1)
             < lax.broadcasted_iota(I32, (N_EXPERTS, N_EXPERTS), 0)).astype(F32)
    tile_off = _x_dot(lower, tile_cnt)
    local = rank_tile + tile_off[:, 0:1]
    tile_chunks = jnp.floor((tile_cnt + (DISPATCH_CHUNK - 1)) * (1.0 / DISPATCH_CHUNK))
    chunk_off = _x_dot(lower, tile_chunks)
    padded = rank_tile + chunk_off[:, 0:1] * DISPATCH_CHUNK
    e_out[...] = jnp.zeros_like(e_out)
    pos_out[...] = jnp.zeros_like(pos_out)
    loc_out[...] = jnp.zeros_like(loc_out)
    pad_out[...] = jnp.zeros_like(pad_out)
    w_out[...] = jnp.zeros_like(w_out)
    for k in range(TOP_K):
        hit = ie == firsts[k]
        e_out[0, k:k + 1, :] = firsts[k]
        pos_out[0, k:k + 1, :] = jnp.sum(jnp.where(hit, rank, 0.0), axis=0, keepdims=True).astype(I32)
        loc_out[0, k:k + 1, :] = jnp.sum(jnp.where(hit, local, 0.0), axis=0, keepdims=True).astype(I32)
        pad_out[0, k:k + 1, :] = jnp.sum(jnp.where(hit, padded, 0.0), axis=0, keepdims=True).astype(I32)
        w_out[k:k + 1, :] = raw[k] / total * ROUTED_SCALE
    lane = lax.broadcasted_iota(I32, old_carry.shape, 1)
    stat_out[0] = jnp.where(lane == 0, tile_cnt, jnp.where(lane == 1, tile_off, jnp.where(
        lane == 2, old_carry, chunk_off))).astype(I32)
    new_carry = old_carry + tile_cnt
    carry[...] = new_carry
    cnt_out[...] = new_carry


def _router(h2, wr_t, b_col):
    n, d = h2.shape
    tn = ROUTER_TILE
    row_spec = pl.BlockSpec((SLOT_ROWS, tn), lambda i: (0, i))
    tile_spec = pl.BlockSpec((1, SLOT_ROWS, tn), lambda i: (i, 0, 0))
    cnt_spec = pl.BlockSpec((N_EXPERTS, 128), lambda i: (0, 0))
    tile_shape = jax.ShapeDtypeStruct((n // tn, SLOT_ROWS, tn), I32)
    return pl.pallas_call(
        _router_body,
        grid=(n // tn,),
        in_specs=[pl.BlockSpec((tn, d), lambda i: (i, 0)), _full(wr_t.shape), _full(b_col.shape)],
        out_specs=[tile_spec, tile_spec, tile_spec, tile_spec,
                   pl.BlockSpec((1, N_EXPERTS, 128), lambda i: (i, 0, 0)), row_spec, cnt_spec],
        out_shape=[tile_shape, tile_shape, tile_shape, tile_shape,
                   jax.ShapeDtypeStruct((n // tn, N_EXPERTS, 128), I32),
                   jax.ShapeDtypeStruct((SLOT_ROWS, n), F32), jax.ShapeDtypeStruct((N_EXPERTS, 128), F32)],
        scratch_shapes=[pltpu.VMEM((N_EXPERTS, 128), F32)],
        compiler_params=_cparams("arbitrary"),
        name="router",
    )(h2, wr_t, b_col)


ROW_LANES = 128
ROW_SUB = D_MODEL // 2 // ROW_LANES
U32 = jnp.uint32
HIGH_HALF = np.uint32(0xFFFF0000)


def _bf16_bits(x):
    return lax.bitcast_convert_type(x.astype(BF16).astype(F32), U32)


def _to_row_tiles(ref, x):
    rows, d = x.shape
    words = (_bf16_bits(x[:, :d // 2]) >> 16) | (_bf16_bits(x[:, d // 2:]) & HIGH_HALF)
    for s in range(ROW_SUB):
        ref[pl.ds(s, rows, stride=ROW_SUB), :] = words[:, s * ROW_LANES:(s + 1) * ROW_LANES]


def _from_row_tiles(ref, rows):
    words = jnp.concatenate([ref[pl.ds(s, rows, stride=ROW_SUB), :] for s in range(ROW_SUB)], axis=1)
    low = lax.bitcast_convert_type(words << 16, F32)
    high = lax.bitcast_convert_type(words & HIGH_HALF, F32)
    return jnp.concatenate([low, high], axis=1)


def _row_copy(src, src_off, dst, dst_off, sem):
    return pltpu.make_async_copy(src.at[pl.ds(pl.multiple_of(src_off, ROW_SUB), ROW_SUB), :],
                                 dst.at[pl.ds(pl.multiple_of(dst_off, ROW_SUB), ROW_SUB), :], sem)


def _load_slots(slot_ref, slot_smem, sem):
    cp = pltpu.make_async_copy(slot_ref.at[0], slot_smem, sem)
    cp.start()
    cp.wait()


def _chunk_copy(src, src_row, dst, dst_row, sem):
    n = DISPATCH_CHUNK * ROW_SUB
    return pltpu.make_async_copy(src.at[pl.ds(pl.multiple_of(src_row * ROW_SUB, ROW_SUB), n), :],
                                 dst.at[pl.ds(pl.multiple_of(dst_row * ROW_SUB, ROW_SUB), n), :], sem)


def _dispatch_body(start_ref, e_ref, pos_ref, loc_ref, stat_ref, h_ref, xs_in, slots_out, xs_out,
                   stat_smem, pending, staging, sem_s, sem):
    del xs_in
    i = pl.program_id(0)
    last = pl.num_programs(0) - 1
    tn = h_ref.shape[0]
    rows = TOP_K * tn
    cur = i % 2

    e = e_ref[0]
    slot = pos_ref[0]
    for x in range(N_EXPERTS):
        slot = slot + jnp.where(e == x, start_ref[x], 0)
    slots_out[0] = slot * ROW_SUB
    stats = pltpu.make_async_copy(stat_ref.at[0], stat_smem, sem_s)
    stats.start()

    @pl.when(i == 0)
    def _():
        pending[0] = 0
        pad = jnp.zeros((DISPATCH_CHUNK * ROW_SUB, ROW_LANES), U32)
        for b in range(2):
            staging[b, pl.ds(rows * ROW_SUB, DISPATCH_CHUNK * ROW_SUB), :] = pad

    loc = loc_ref[0]
    p_iota = lax.broadcasted_iota(I32, (rows, tn), 0)
    pick = jnp.zeros((rows, tn), F32)
    for k in range(TOP_K):
        pick = jnp.where(p_iota == loc[k:k + 1, :], 1.0, pick)
    _to_row_tiles(staging.at[cur], _dg(pick.astype(BF16), h_ref[...].astype(BF16), _NN))

    def drain(count):
        def one(c, carry):
            _chunk_copy(staging.at[0], 0, xs_out, 0, sem).wait()
            return carry
        lax.fori_loop(0, count, one, 0)

    drain(pending[0])
    stats.wait()

    def per_expert(x, total):
        count = stat_smem[x, 0]
        src = stat_smem[x, 1]
        dst = start_ref[x] + stat_smem[x, 2]
        n_chunks = lax.shift_right_logical(count + (DISPATCH_CHUNK - 1), DISPATCH_SHIFT)

        def one(c, carry):
            _chunk_copy(staging.at[cur], src + c * DISPATCH_CHUNK, xs_out, dst + c * DISPATCH_CHUNK, sem).start()
            return carry

        lax.fori_loop(0, n_chunks, one, 0)
        return total + n_chunks

    issued = lax.fori_loop(0, N_EXPERTS, per_expert, 0)
    pending[0] = issued

    @pl.when(i == last)
    def _():
        drain(issued)


def _zero_tail_body(lb_ref, o_ref):
    del lb_ref
    o_ref[...] = jnp.zeros_like(o_ref)


def _zero_tails(last_block, n_blocks):
    rows = MOE_BLOCK * ROW_SUB
    grid_spec = pltpu.PrefetchScalarGridSpec(
        num_scalar_prefetch=1,
        grid=(last_block.shape[0],),
        in_specs=[],
        out_specs=pl.BlockSpec((rows, ROW_LANES), lambda e, lb: (lb[e], 0)),
    )
    return pl.pallas_call(
        _zero_tail_body,
        grid_spec=grid_spec,
        out_shape=jax.ShapeDtypeStruct((n_blocks * rows, ROW_LANES), U32),
        compiler_params=_cparams("arbitrary"),
        name="zero_tails",
    )(last_block)


def _dispatch(row_start, e3, pos3, loc3, stats, h2, xs_zero):
    n, d = h2.shape
    tn = e3.shape[2]
    tile = lambda: pl.BlockSpec((1, SLOT_ROWS, tn), lambda i, st: (i, 0, 0))
    grid_spec = pltpu.PrefetchScalarGridSpec(
        num_scalar_prefetch=1,
        grid=(n // tn,),
        in_specs=[tile(), tile(), tile(), pl.BlockSpec((1, N_EXPERTS, 128), lambda i, st: (i, 0, 0)),
                  pl.BlockSpec((tn, d), lambda i, st: (i, 0)), pl.BlockSpec(memory_space=pl.ANY)],
        out_specs=[tile(), pl.BlockSpec(memory_space=pl.ANY)],
        scratch_shapes=[pltpu.SMEM((N_EXPERTS, 128), I32), pltpu.SMEM((1,), I32),
                        pltpu.VMEM((2, (TOP_K * tn + DISPATCH_CHUNK) * ROW_SUB, ROW_LANES), U32),
                        pltpu.SemaphoreType.DMA, pltpu.SemaphoreType.DMA],
    )
    return pl.pallas_call(
        _dispatch_body,
        grid_spec=grid_spec,
        out_shape=[jax.ShapeDtypeStruct((n // tn, SLOT_ROWS, tn), I32),
                   jax.ShapeDtypeStruct(xs_zero.shape, xs_zero.dtype)],
        input_output_aliases={6: 1},
        compiler_params=_cparams("arbitrary"),
        name="dispatch",
    )(row_start, e3, pos3, loc3, stats, h2, xs_zero)


def _swiglu(x, wg, wu, wd):
    g = _dg(x, wg, _NN)
    u = _dg(x, wu, _NN)
    return _dg((g * _sigmoid(g) * u).astype(BF16), wd, _NN)


EXPERT_RING = 3


def _expert_body(be_ref, nu_ref, xs_hbm, wg_ref, wu_ref, wd_ref, ys_ref, ring, sem):
    del be_ref
    b = pl.program_id(0)
    nb = pl.num_programs(0)
    rows = MOE_BLOCK * ROW_SUB

    def fetch(step):
        blk = jnp.minimum(step, nu_ref[0] - 1)
        slot = step % EXPERT_RING
        return pltpu.make_async_copy(xs_hbm.at[pl.ds(pl.multiple_of(blk * rows, rows), rows), :], ring.at[slot],
                                     sem.at[slot])

    @pl.when(b == 0)
    def _():
        for s in range(EXPERT_RING - 1):
            fetch(s).start()

    @pl.when(b + EXPERT_RING - 1 < nb)
    def _():
        fetch(b + EXPERT_RING - 1).start()

    fetch(b).wait()

    @pl.when(b < nu_ref[0])
    def _():
        x = _from_row_tiles(ring.at[b % EXPERT_RING], MOE_BLOCK).astype(BF16)
        _to_row_tiles(ys_ref, _swiglu(x, wg_ref[0], wu_ref[0], wd_ref[0]))

    @pl.when(b >= nu_ref[0])
    def _():
        ys_ref[...] = jnp.zeros_like(ys_ref)


def _experts(block_e, n_used, xs, wg, wu, wd):
    bm = MOE_BLOCK
    _, d, de = wg.shape
    rows = bm * ROW_SUB
    grid_spec = pltpu.PrefetchScalarGridSpec(
        num_scalar_prefetch=2,
        grid=(xs.shape[0] // rows,),
        in_specs=[
            pl.BlockSpec(memory_space=pl.ANY),
            pl.BlockSpec((1, d, de), lambda b, be, nu: (be[b], 0, 0)),
            pl.BlockSpec((1, d, de), lambda b, be, nu: (be[b], 0, 0)),
            pl.BlockSpec((1, de, d), lambda b, be, nu: (be[b], 0, 0)),
        ],
        out_specs=pl.BlockSpec((rows, ROW_LANES), lambda b, be, nu: (b, 0)),
        scratch_shapes=[pltpu.VMEM((EXPERT_RING, rows, ROW_LANES), U32), pltpu.SemaphoreType.DMA((EXPERT_RING,))],
    )
    return pl.pallas_call(
        _expert_body,
        grid_spec=grid_spec,
        out_shape=jax.ShapeDtypeStruct(xs.shape, xs.dtype),
        compiler_params=_cparams("arbitrary"),
        name="experts",
    )(block_e, n_used, xs, wg, wu, wd)


def _combine_body(start_ref, pad_ref, stat0_ref, statn_ref, w_ref, x1_ref, h2_ref, gate_ref, ys_hbm, wsg_ref,
                  wsu_ref, wsd_ref, gpost_ref, out_ref, stat_smem, pending, staging, sem_s, sem):
    i = pl.program_id(0)
    last = pl.num_programs(0) - 1
    tn = x1_ref.shape[0]
    cur = i % 2

    def fetch(stat_ref, which):
        stats = pltpu.make_async_copy(stat_ref.at[0], stat_smem, sem_s)
        stats.start()
        stats.wait()

        def per_expert(x, total):
            count = stat_smem[x, 0]
            src = start_ref[x] + stat_smem[x, 2]
            dst = stat_smem[x, 3] * DISPATCH_CHUNK
            n_chunks = lax.shift_right_logical(count + (DISPATCH_CHUNK - 1), DISPATCH_SHIFT)

            def one(c, carry):
                _chunk_copy(ys_hbm, src + c * DISPATCH_CHUNK, staging.at[which], dst + c * DISPATCH_CHUNK,
                            sem.at[which]).start()
                return carry

            lax.fori_loop(0, n_chunks, one, 0)
            return total + n_chunks

        pending[which] = lax.fori_loop(0, N_EXPERTS, per_expert, 0)

    @pl.when(i == 0)
    def _():
        staging[...] = jnp.zeros_like(staging)
        fetch(stat0_ref, 0)

    @pl.when(i < last)
    def _():
        fetch(statn_ref, 1 - cur)

    acc = _swiglu(h2_ref[...].astype(BF16), wsg_ref[...], wsu_ref[...], wsd_ref[...])

    def drain(c, carry):
        _chunk_copy(ys_hbm, 0, staging.at[0], 0, sem.at[cur]).wait()
        return carry

    lax.fori_loop(0, pending[cur], drain, 0)
    piece = COMBINE_ROWS // COMBINE_PIECES
    for q in range(COMBINE_PIECES):
        col = lax.broadcasted_iota(I32, (tn, piece), 1) + q * piece
        weight = jnp.zeros((tn, piece), F32)
        for k in range(TOP_K):
            weight = jnp.where(col == pad_ref[:, k:k + 1], w_ref[:, k:k + 1], weight)
        rows = _from_row_tiles(staging.at[cur, pl.ds(q * piece * ROW_SUB, piece * ROW_SUB), :], piece)
        acc = acc + _dg(weight.astype(BF16), rows.astype(BF16), _NN)
    out_ref[...] = x1_ref[...] + gate_ref[0] * _rms(acc, gpost_ref[...])


def _combine(row_start, pad_tok, stats, w_tok, x1, h2, gate_rows, ys, wsg, wsu, wsd, gpost):
    n, d = x1.shape
    tn = TOKEN_TILE
    tok = pl.BlockSpec((tn, d), lambda i, st: (i, 0))
    small = pl.BlockSpec((tn, SLOT_ROWS), lambda i, st: (i, 0))
    ntiles = n // tn
    grid_spec = pltpu.PrefetchScalarGridSpec(
        num_scalar_prefetch=1,
        grid=(ntiles,),
        in_specs=[small, pl.BlockSpec((1, N_EXPERTS, 128), lambda i, st: (0, 0, 0)),
                  pl.BlockSpec((1, N_EXPERTS, 128), lambda i, st: (jnp.minimum(i + 1, ntiles - 1), 0, 0)),
                  small, tok, tok,
                  pl.BlockSpec((1, 1, d), lambda i, st: (i, 0, 0)),
                  pl.BlockSpec(memory_space=pl.ANY),
                  pl.BlockSpec(wsg.shape, lambda i, st: (0, 0)), pl.BlockSpec(wsu.shape, lambda i, st: (0, 0)),
                  pl.BlockSpec(wsd.shape, lambda i, st: (0, 0)), pl.BlockSpec(gpost.shape, lambda i, st: (0, 0))],
        out_specs=tok,
        scratch_shapes=[pltpu.SMEM((N_EXPERTS, 128), I32), pltpu.SMEM((2,), I32),
                        pltpu.VMEM((2, COMBINE_ROWS * ROW_SUB, ROW_LANES), U32),
                        pltpu.SemaphoreType.DMA, pltpu.SemaphoreType.DMA((2,))],
    )
    return pl.pallas_call(
        _combine_body,
        grid_spec=grid_spec,
        out_shape=jax.ShapeDtypeStruct((n, d), F32),
        compiler_params=_cparams("arbitrary"),
        name="combine",
    )(row_start, pad_tok, stats, stats, w_tok, x1, h2, gate_rows, ys, wsg, wsu, wsd, gpost)


def _moe(x1, h2, gate_rows, lp):
    n, d = h2.shape
    bm = MOE_BLOCK
    tn = TOKEN_TILE
    e3, pos3, loc3, pad3, stats, w_rows, cnt = _router(h2, lp['wr_t'], lp['b_router_col'])
    counts = cnt[:, 0].astype(I32)
    blocks_per_e = (counts + DISPATCH_CHUNK + bm - 1) // bm
    blk_end = jnp.cumsum(blocks_per_e)
    blk_start = blk_end - blocks_per_e
    n_blocks = -(-(n * TOP_K + N_EXPERTS * DISPATCH_CHUNK) // bm) + N_EXPERTS
    block_e = jnp.minimum(jnp.sum(blk_end[None, :] <= jnp.arange(n_blocks, dtype=I32)[:, None], axis=1),
                          N_EXPERTS - 1).astype(I32)
    n_used = blk_end[-1:].astype(I32)
    partial = jnp.stack([blk_start + counts // bm, blk_end - 1], axis=1).reshape(-1)
    xs_tails = _zero_tails(jnp.clip(partial, 0, n_blocks - 1).astype(I32), n_blocks)
    row_start = (blk_start * bm).astype(I32)
    _, xs = _dispatch(row_start, e3, pos3, loc3, stats, h2, xs_tails)
    ys = _experts(block_e, n_used, xs, lp['w_e_gate'], lp['w_e_up'], lp['w_e_down'])
    pad_tok = pad3.transpose(0, 2, 1).reshape(n, SLOT_ROWS)
    return _combine(row_start, pad_tok, stats, w_rows.T, x1, h2, gate_rows, ys, lp['w_s_gate'], lp['w_s_up'],
                    lp['w_s_down'], lp['g_post_ffn'])


def _rope_tables(n_tokens):
    rows = n_tokens // GRID_W
    row = jnp.repeat(jnp.arange(rows, dtype=F32), GRID_W)
    col = jnp.tile(jnp.arange(GRID_W, dtype=F32), rows)
    n_freq = ROPE_DIM // 4
    inv_freq = ROPE_THETA ** (-jnp.arange(n_freq, dtype=F32) / n_freq)
    ang_r = row[:, None] * inv_freq
    ang_c = col[:, None] * inv_freq
    ang = jnp.concatenate([ang_r, ang_r, ang_c, ang_c], axis=-1)
    cos, sin = jnp.cos(ang), jnp.sin(ang)
    pad = lambda z, fill: jnp.concatenate(
        [jnp.full((n_tokens, NOPE_DIM), fill, F32), z, jnp.zeros((n_tokens, HEAD_PAD - QK_DIM), F32)], axis=1)
    return pad(cos, 1.0), pad(sin, 0.0), pad(cos, 0.0), pad(sin, 0.0)


def _rot_cols(w):
    q = ROPE_DIM // 4
    perm = np.concatenate([np.arange(q, 2 * q), np.arange(0, q), np.arange(3 * q, 4 * q), np.arange(2 * q, 3 * q)])
    sign = np.concatenate([-np.ones(q), np.ones(q), -np.ones(q), np.ones(q)]).astype(np.float32)
    return w[..., perm] * sign


def _block_diag(blocks):
    h, m, n = blocks.shape[-3:]
    eye = jnp.eye(h, dtype=blocks.dtype)
    out = blocks[..., :, :, None, :] * eye[:, None, :, None]
    return out.reshape(blocks.shape[:-3] + (h * m, h * n))


def _diag_blocks(mat, h):
    m, n = mat.shape[-2] // h, mat.shape[-1] // h
    z = mat.reshape(mat.shape[:-2] + (h, m, h, n))
    return jnp.stack([z[..., i, :, i, :] for i in range(h)], axis=-3)


def _layer_operands(l, a):
    w_in = a['w_in'][l]
    d = w_in.shape[0]
    p_mla = Q_LORA + KV_LORA + ROPE_DIM
    w_kr = w_in[:, Q_LORA + KV_LORA:p_mla]
    z32 = jnp.zeros((d, ROPE_DIM), F32)
    w_mla = jnp.concatenate([w_in[:, :Q_LORA + KV_LORA], z32, _rot_cols(w_kr), w_kr, z32], axis=1)
    wq3 = a['w_q_b'][l].reshape(Q_LORA, H_MLA, QK_DIM)
    zq = lambda n: jnp.zeros((Q_LORA, H_MLA, n), F32)
    wq = jnp.concatenate([wq3, zq(HEAD_PAD - QK_DIM)], axis=2).reshape(Q_LORA, H_MLA * HEAD_PAD)
    wqr = jnp.concatenate([zq(NOPE_DIM), _rot_cols(wq3[:, :, NOPE_DIM:]), zq(HEAD_PAD - QK_DIM)],
                          axis=2).reshape(Q_LORA, H_MLA * HEAD_PAD)
    wkv3 = a['w_kv_b'][l].reshape(KV_LORA, H_MLA, NOPE_DIM + V_DIM)
    wk = jnp.concatenate([wkv3[:, :, :NOPE_DIM], jnp.zeros((KV_LORA, H_MLA, HEAD_PAD - NOPE_DIM), F32)],
                         axis=2).reshape(KV_LORA, H_MLA * HEAD_PAD)
    wv = wkv3[:, :, NOPE_DIM:].reshape(KV_LORA, H_MLA * V_DIM)
    row = lambda z: z.reshape(1, -1)
    two = lambda z: jnp.concatenate([z[0], z[1]], axis=-1)
    bd2 = lambda z: jnp.concatenate([jnp.concatenate([z[0], jnp.zeros_like(z[0])], axis=1),
                                     jnp.concatenate([jnp.zeros_like(z[1]), z[1]], axis=1)], axis=0)
    return {
        'g_pre_mix': row(a['g_pre_mix'][l]), 'g_post_mix': row(a['g_post_mix'][l]),
        'g_pre_ffn': row(a['g_pre_ffn'][l]), 'g_post_ffn': row(a['g_post_ffn'][l]),
        'w_mla': w_mla.astype(BF16),
        'w_rwkv': w_in[:, p_mla:p_mla + P_RWKV].astype(BF16),
        'w_lru': w_in[:, p_mla + P_RWKV:].astype(BF16),
        'g_q_a': row(a['g_q_a'][l]), 'g_kv_a': row(a['g_kv_a'][l]),
        'wq': wq.astype(BF16), 'wqr': wqr.astype(BF16), 'wk': wk.astype(BF16), 'wv': wv.astype(BF16),
        'g_mla_out': row(a['g_mla_out'][l]),
        'mu_rwkv': row(a['mu_rwkv'][l]), 'k_k': row(a['k_k'][l]), 'k_a': row(a['k_a'][l]), 'r_k': row(a['r_k'][l]),
        'w0_cat': row(two(a['w0_rwkv'][l])), 'w2_bd': bd2(a['w2_rwkv'][l]),
        'a0_cat': row(two(a['a0_rwkv'][l])), 'a2_bd': bd2(a['a2_rwkv'][l]),
        'g2_rwkv': a['g2_rwkv'][l], 'ln_x_w': row(a['ln_x_w'][l]), 'ln_x_b': row(a['ln_x_b'][l]),
        'conv_w': a['conv_w'][l], 'conv_b': a['conv_b'][l],
        'wa_bd': _block_diag(a['w_lru_a'][l]), 'b_lru_a': a['b_lru_a'][l],
        'wx_bd': _block_diag(a['w_lru_x'][l]), 'b_lru_x': a['b_lru_x'][l],
        'lam': a['lam'][l], 'g_lru_out': row(a['g_lru_out'][l]),
        'w_out': a['w_out'][l].astype(BF16),
        'wr_t': a['w_router'][l].T, 'b_router_col': a['b_router'][l].reshape(N_EXPERTS, 1),
        'w_e_gate': a['w_e_gate'][l].astype(BF16), 'w_e_up': a['w_e_up'][l].astype(BF16),
        'w_e_down': a['w_e_down'][l].astype(BF16),
        'w_s_gate': a['w_s_gate'][l].astype(BF16), 'w_s_up': a['w_s_up'][l].astype(BF16),
        'w_s_down': a['w_s_down'][l].astype(BF16),
    }


def _mixer(x, grp, mod, lp, ctx, tables, prev):
    _, nb, t = grp
    tt = min(t, 512)
    p_mla, p_rwkv, p_lru = _in_proj(x, grp, lp['g_pre_mix'], mod['sc1'], mod['sh1'], lp['w_mla'], lp['w_rwkv'],
                                    lp['w_lru'], tt)
    if ctx is None:
        q, k, v, new_cache = _mla_prep(p_mla, lp['g_q_a'], lp['g_kv_a'], lp['wq'], None, lp['wk'], lp['wv'], None, tt)
        kc = vc = None
        s0 = jnp.zeros((nb, 2, D_RWKV, D_RWKV), F32)
        h0 = jnp.zeros((nb, 2, 1, D_LRU), F32)
    else:
        cache, s_rwkv, s_lru = ctx
        q, k, v = _mla_prep(p_mla, lp['g_q_a'], lp['g_kv_a'], lp['wq'], lp['wqr'], lp['wk'], lp['wv'], tables, tt)
        new_cache = None
        kr_placed = jnp.pad(cache[..., KV_LORA:], ((0, 0), (0, 0), (NOPE_DIM, HEAD_PAD - QK_DIM)))
        kc, vc = _ctx_kv(cache[..., :KV_LORA], kr_placed, lp['wk'], lp['wv'])
        s0 = _block_diag(s_rwkv)
        h0 = s_lru[:, :, None, :]
    o = _attention(q, k, v, kc, vc, min(t, 256))

    r, vv, kk, lw, ag, kd, g, bonus = _rwkv_prep(p_rwkv, lp, tt)
    y_f, y_b, s_fin = _rwkv_scan(r, vv, kk, lw, ag, kd, s0)

    hp, hn = _halo_rows(p_lru[..., :D_LRU], tt, SUBLANES)
    h_f, fin_f = _lru_dir(p_lru, hp, lp, 0, h0[:, 0], tt)
    h_b, fin_b = _lru_dir(p_lru, hn, lp, 1, h0[:, 1], tt)

    x1, h2 = _out_proj(x, grp, o, y_f, y_b, bonus, g, h_f, h_b, p_lru, lp, mod, tt, prev)
    states = None
    if ctx is None:
        states = (new_cache, _diag_blocks(s_fin, H_RWKV), jnp.concatenate([fin_f, fin_b], axis=1))
    return x1, h2, states


def kernel(x_prompt, x_sample, c, cache_mla, state_rwkv, state_lru, c_ctx, w_ada, b_ada, g_pre_mix, g_post_mix, g_pre_ffn, g_post_ffn, w_in, g_q_a, w_q_b, g_kv_a, w_kv_b, g_mla_out, mu_rwkv, w0_rwkv, w2_rwkv, a0_rwkv, a2_rwkv, g2_rwkv, k_k, k_a, r_k, ln_x_w, ln_x_b, conv_w, conv_b, w_lru_a, b_lru_a, w_lru_x, b_lru_x, lam, g_lru_out, w_out, w_router, b_router, w_e_gate, w_e_up, w_e_down, w_s_gate, w_s_up, w_s_down):
    a = dict(w_in=w_in, g_pre_mix=g_pre_mix, g_post_mix=g_post_mix, g_pre_ffn=g_pre_ffn, g_post_ffn=g_post_ffn,
             g_q_a=g_q_a, w_q_b=w_q_b, g_kv_a=g_kv_a, w_kv_b=w_kv_b, g_mla_out=g_mla_out, mu_rwkv=mu_rwkv,
             w0_rwkv=w0_rwkv, w2_rwkv=w2_rwkv, a0_rwkv=a0_rwkv, a2_rwkv=a2_rwkv, g2_rwkv=g2_rwkv, k_k=k_k, k_a=k_a,
             r_k=r_k.reshape(r_k.shape[0], -1), ln_x_w=ln_x_w, ln_x_b=ln_x_b, conv_w=conv_w, conv_b=conv_b,
             w_lru_a=w_lru_a, b_lru_a=b_lru_a, w_lru_x=w_lru_x, b_lru_x=b_lru_x, lam=lam, g_lru_out=g_lru_out,
             w_out=w_out, w_router=w_router, b_router=b_router, w_e_gate=w_e_gate, w_e_up=w_e_up,
             w_e_down=w_e_down, w_s_gate=w_s_gate, w_s_up=w_s_up, w_s_down=w_s_down)
    n_layers = w_in.shape[0]
    nbp, tp, d = x_prompt.shape
    nbs, ts, _ = x_sample.shape
    rows = -(-(1 + nbs) // SUBLANES) * SUBLANES
    cvecs = jnp.concatenate([c_ctx[None, :], c, jnp.zeros((rows - 1 - nbs, d), F32)], axis=0)
    mods = _ada(cvecs, w_ada, b_ada)
    names = ('sh1', 'sc1', 'g1', 'sh2', 'sc2', 'g2')
    tables = _rope_tables(ts)
    tn = TOKEN_TILE

    grp_p, grp_s = (0, nbp, tp), (nbp * tp, nbs, ts)
    x = jnp.concatenate([x_prompt.reshape(-1, d), x_sample.reshape(-1, d)], axis=0)
    caches, rwkv_states, lru_states = [], [], []
    for l in range(n_layers):
        lp = _layer_operands(l, a)
        mod_p = {nm: mods[l, 0:1, i * d:(i + 1) * d][:, None, :] for i, nm in enumerate(names)}
        mod_s = {nm: mods[l, 1:1 + nbs, i * d:(i + 1) * d][:, None, :] for i, nm in enumerate(names)}
        x1, h2, states = _mixer(x, grp_p, mod_p, lp, None, None, None)
        x1, h2, _ = _mixer(x, grp_s, mod_s, lp, (cache_mla[:, l], state_rwkv[:, l], state_lru[:, l]), tables,
                           (x1, h2))
        caches.append(states[0])
        rwkv_states.append(states[1])
        lru_states.append(states[2])
        gate_rows = jnp.concatenate([jnp.repeat(mod_p['g2'], nbp * tp // tn, axis=0),
                                     jnp.repeat(mod_s['g2'], ts // tn, axis=0)], axis=0)
        x = _moe(x1, h2, gate_rows, lp)
    yp = x[:nbp * tp].reshape(nbp, tp, d)
    ys = x[nbp * tp:].reshape(nbs, ts, d)
    return (yp, ys, jnp.stack(caches, axis=1), jnp.stack(rwkv_states, axis=1), jnp.stack(lru_states, axis=1))
```
